```python
import math
import jax, jax.numpy as jnp
from jax import lax
import numpy as np

D_MODEL = 1024
BATCH = 4
SEQ = 4096
DEPTH = 1

D_SSM = 512
SSM_CH = 16
SSM_GROUPS = D_SSM // SSM_CH
SSM_STATE = 64
D_ATT = 512
HEAD_DIM = 64
N_HEADS = D_ATT // HEAD_DIM
N_KV = 2
HPG = N_HEADS // N_KV
D_KV = N_KV * HEAD_DIM
N_BRANCH = 3
D_IN = D_SSM + D_ATT + 6 * D_KV + N_BRANCH * N_HEADS
D_MIX = D_SSM + D_ATT
CMP_STRIDE = 16
CMP_BLOCK = 2 * CMP_STRIDE
CMP_HIDDEN = 256
SEL_BLOCK = 64
N_SELECT = 16
WINDOW = 512
Q_BLOCK = 128
N_EXP_GROUPS = 4
EXPERTS_PER_GROUP = 8
N_EXPERTS = N_EXP_GROUPS * EXPERTS_PER_GROUP
TOP_K = 2
D_EXPERT = 256

EPS = 1e-6
NEG = -1e30
FORCE = 1e9

kernel_name = "hymba_s5_nsa_hier_moe"


def rms_norm(x, g):
    xf = x.astype(jnp.float32)
    y = xf * lax.rsqrt(jnp.mean(xf * xf, axis=-1, keepdims=True) + EPS)
    return (y * g.astype(jnp.float32)).astype(x.dtype)


def s5_mixer(u, lam_re, lam_im, log_step, b_re, b_im, c_re, c_im, d_skip, w_glu, b_glu):
    f32 = jnp.float32
    bsz, L, _ = u.shape
    ug = u.astype(f32).reshape(bsz, L, SSM_GROUPS, SSM_CH)
    lam = lax.complex(lam_re.astype(f32), lam_im.astype(f32))
    step = jnp.exp(log_step.astype(f32))[:, None]
    lam_bar = jnp.exp(lam * step)
    b = lax.complex(b_re.astype(f32), b_im.astype(f32))
    b_bar = ((lam_bar - 1.0) / lam)[..., None] * b
    bu = lax.complex(jnp.einsum('gph,blgh->blgp', jnp.real(b_bar), ug),
                     jnp.einsum('gph,blgh->blgp', jnp.imag(b_bar), ug))
    a = jnp.broadcast_to(lam_bar, bu.shape)

    def combine(left, right):
        a_l, b_l = left
        a_r, b_r = right
        return a_r * a_l, a_r * b_l + b_r

    _, states = lax.associative_scan(combine, (a, bu), axis=1)
    y = (jnp.einsum('ghp,blgp->blgh', c_re.astype(f32), jnp.real(states))
         - jnp.einsum('ghp,blgp->blgh', c_im.astype(f32), jnp.imag(states))
         + d_skip.astype(f32) * ug)
    y = jax.nn.gelu(y.reshape(bsz, L, D_SSM))
    y = y * jax.nn.sigmoid(y @ w_glu.astype(f32) + b_glu.astype(f32))
    return y.astype(u.dtype)


def compress_blocks(t, pos, w1, w2):
    bsz, L = t.shape[0], t.shape[1]
    chunks = t.reshape(bsz, L // CMP_STRIDE, CMP_STRIDE, N_KV, HEAD_DIM)
    blocks = jnp.concatenate([chunks[:, :-1], chunks[:, 1:]], axis=2)
    blocks = blocks + pos[None, None, :, None, :]
    nc = blocks.shape[1]
    flat = blocks.transpose(0, 1, 3, 2, 4).reshape(bsz, nc, N_KV, CMP_BLOCK * HEAD_DIM)
    return jax.nn.gelu(flat @ w1) @ w2


def nsa_mixer(att, g_q, g_kc, g_ks, g_kw, pos_k, pos_v, w_ck1, w_ck2, w_cv1, w_cv2):
    f32 = jnp.float32
    bsz, L, _ = att.shape
    splits = [D_ATT + i * D_KV for i in range(7)]
    q, kc, vc, ks, vs, kw, vw, gate = jnp.split(att, splits, axis=-1)
    kv_shape = (bsz, L, N_KV, HEAD_DIM)
    q = rms_norm(q.reshape(bsz, L, N_KV, HPG, HEAD_DIM), g_q) * (HEAD_DIM ** -0.5)
    ks = rms_norm(ks.reshape(kv_shape), g_ks)
    vs = vs.reshape(kv_shape)
    kw = rms_norm(kw.reshape(kv_shape), g_kw)
    vw = vw.reshape(kv_shape)
    qpos = jnp.arange(L)

    kcmp = rms_norm(compress_blocks(kc.reshape(kv_shape), pos_k, w_ck1, w_ck2), g_kc)
    vcmp = compress_blocks(vc.reshape(kv_shape), pos_v, w_cv1, w_cv2)
    nc = kcmp.shape[1]
    cstart = jnp.arange(nc) * CMP_STRIDE
    cmask = (cstart[None, :] + CMP_BLOCK - 1) <= qpos[:, None]
    s = jnp.einsum('blghd,bcgd->bghlc', q, kcmp).astype(f32)
    s = jnp.where(cmask, s, NEG)
    p_cmp = jax.nn.softmax(s, axis=-1) * jnp.any(cmask, axis=-1)[:, None].astype(f32)
    o_cmp = jnp.einsum('bghlc,bcgd->blghd', p_cmp.astype(vcmp.dtype), vcmp)

    ns = L // SEL_BLOCK
    sstart = jnp.arange(ns) * SEL_BLOCK
    overlap = ((cstart[:, None] < sstart[None, :] + SEL_BLOCK)
               & (cstart[:, None] + CMP_BLOCK > sstart[None, :])).astype(f32)
    imp = jnp.einsum('bghlc,cs->bgls', p_cmp, overlap)
    cur = qpos // SEL_BLOCK
    blk = jnp.arange(ns)
    valid = blk[None, :] <= cur[:, None]
    forced = (blk[None, :] == 0) | (blk[None, :] == cur[:, None]) | (blk[None, :] == cur[:, None] - 1)
    imp = jnp.where(forced, FORCE, jnp.where(valid, imp, NEG))
    n_sel = min(N_SELECT, ns)
    _, idx = lax.top_k(imp, n_sel)

    ks_blk = ks.reshape(bsz, ns, SEL_BLOCK, N_KV, HEAD_DIM).transpose(0, 3, 1, 2, 4)
    vs_blk = vs.reshape(bsz, ns, SEL_BLOCK, N_KV, HEAD_DIM).transpose(0, 3, 1, 2, 4)
    nq = L // Q_BLOCK
    q_chunks = q.reshape(bsz, nq, Q_BLOCK, N_KV, HPG, HEAD_DIM).transpose(1, 0, 2, 3, 4, 5)
    idx_chunks = idx.reshape(bsz, N_KV, nq, Q_BLOCK, n_sel).transpose(2, 0, 1, 3, 4)
    starts = jnp.arange(nq) * Q_BLOCK
    bi = jnp.arange(bsz)[:, None, None, None]
    gi = jnp.arange(N_KV)[None, :, None, None]

    def sel_block(args):
        qc, ic, st = args
        kb = ks_blk[bi, gi, ic]
        vb = vs_blk[bi, gi, ic]
        kpos = ic[..., None] * SEL_BLOCK + jnp.arange(SEL_BLOCK)
        tpos = st + jnp.arange(Q_BLOCK)
        m = kpos <= tpos[None, None, :, None, None]
        sc = jnp.einsum('bqghd,bgqnkd->bghqnk', qc, kb).astype(f32)
        sc = jnp.where(m[:, :, None], sc, NEG)
        sc = sc.reshape(bsz, N_KV, HPG, Q_BLOCK, n_sel * SEL_BLOCK)
        pr = jax.nn.softmax(sc, axis=-1).reshape(bsz, N_KV, HPG, Q_BLOCK, n_sel, SEL_BLOCK)
        return jnp.einsum('bghqnk,bgqnkd->bqghd', pr.astype(vb.dtype), vb)

    o_sel = lax.map(sel_block, (q_chunks, idx_chunks, starts))
    o_sel = o_sel.transpose(1, 0, 2, 3, 4, 5).reshape(bsz, L, N_KV, HPG, HEAD_DIM)

    pad = WINDOW // Q_BLOCK
    kw_p = jnp.pad(kw, ((0, 0), (WINDOW, 0), (0, 0), (0, 0))).reshape(bsz, nq + pad, Q_BLOCK, N_KV, HEAD_DIM)
    vw_p = jnp.pad(vw, ((0, 0), (WINDOW, 0), (0, 0), (0, 0))).reshape(bsz, nq + pad, Q_BLOCK, N_KV, HEAD_DIM)
    kw_band = jnp.concatenate([kw_p[:, i:i + nq] for i in range(pad + 1)], axis=2)
    vw_band = jnp.concatenate([vw_p[:, i:i + nq] for i in range(pad + 1)], axis=2)
    qb = q.reshape(bsz, nq, Q_BLOCK, N_KV, HPG, HEAD_DIM)
    sw = jnp.einsum('bnqghd,bnkgd->bnghqk', qb, kw_band).astype(f32)
    qp = jnp.arange(nq)[:, None] * Q_BLOCK + jnp.arange(Q_BLOCK)[None, :]
    kp = jnp.arange(nq)[:, None] * Q_BLOCK - WINDOW + jnp.arange((pad + 1) * Q_BLOCK)[None, :]
    wm = ((kp[:, None, :] <= qp[:, :, None]) & (kp[:, None, :] > qp[:, :, None] - WINDOW)
          & (kp[:, None, :] >= 0))
    sw = jnp.where(wm[None, :, None, None], sw, NEG)
    pw = jax.nn.softmax(sw, axis=-1)
    o_win = jnp.einsum('bnghqk,bnkgd->bnqghd', pw.astype(vw_band.dtype), vw_band)
    o_win = o_win.reshape(bsz, L, N_KV, HPG, HEAD_DIM)

    g = jax.nn.sigmoid(gate.astype(f32)).reshape(bsz, L, N_KV, HPG, N_BRANCH).astype(att.dtype)
    o = g[..., 0:1] * o_cmp + g[..., 1:2] * o_sel + g[..., 2:3] * o_win
    return o.reshape(bsz, L, D_ATT)


def hier_moe(h, w_grp, b_grp, w_exp, b_exp, w_gate, w_up, w_down):
    f32 = jnp.float32
    bsz, L, D = h.shape
    t = h.reshape(-1, D)
    n_tok = t.shape[0]
    glog = (t @ w_grp + b_grp).astype(f32)
    gprob = jax.nn.softmax(glog, axis=-1)
    gsel = jnp.argmax(glog, axis=-1)
    elog = (t @ w_exp + b_exp).astype(f32).reshape(n_tok, N_EXP_GROUPS, EXPERTS_PER_GROUP)
    elog_sel = jnp.take_along_axis(elog, gsel[:, None, None], axis=1)[:, 0]
    eprob = jax.nn.softmax(elog_sel, axis=-1)
    topv, topi = lax.top_k(eprob, TOP_K)
    topv = topv / jnp.sum(topv, axis=-1, keepdims=True)
    wsel = topv * jnp.take_along_axis(gprob, gsel[:, None], axis=1)
    eid = gsel[:, None] * EXPERTS_PER_GROUP + topi
    comb = jnp.sum(jax.nn.one_hot(eid, N_EXPERTS, dtype=f32) * wsel[..., None], axis=1)
    out = jnp.zeros((n_tok, D), f32)
    for e in range(N_EXPERTS):
        a = jax.nn.silu(t @ w_gate[e]) * (t @ w_up[e])
        out = out + comb[:, e:e + 1] * (a @ w_down[e]).astype(f32)
    return out.reshape(bsz, L, D).astype(h.dtype)


def setup_inputs(seed: int = 0) -> dict:
    key = jax.random.key(seed)
    ks = jax.random.split(key, 34)
    nrm = lambda k, shape, s: jax.random.normal(k, shape, jnp.float32) * s
    L = DEPTH
    G, P, H = SSM_GROUPS, SSM_STATE, SSM_CH
    lam_im0 = jnp.broadcast_to(jnp.pi * jnp.arange(P, dtype=jnp.float32), (L, G, P))
    return {
        "x": nrm(ks[0], (BATCH, SEQ, D_MODEL), 1.0),
        "norm1_g": 1.0 + nrm(ks[1], (L, D_MODEL), 0.02),
        "w_in": nrm(ks[2], (L, D_MODEL, D_IN), D_MODEL ** -0.5),
        "lam_re": -0.5 + nrm(ks[3], (L, G, P), 0.01),
        "lam_im": lam_im0 + nrm(ks[4], (L, G, P), 0.01),
        "log_step": jax.random.uniform(ks[5], (L, G), jnp.float32, math.log(1e-3), math.log(1e-1)),
        "b_re": nrm(ks[6], (L, G, P, H), (2 * H) ** -0.5),
        "b_im": nrm(ks[7], (L, G, P, H), (2 * H) ** -0.5),
        "c_re": nrm(ks[8], (L, G, H, P), (2 * P) ** -0.5),
        "c_im": nrm(ks[9], (L, G, H, P), (2 * P) ** -0.5),
        "d_skip": nrm(ks[10], (L, G, H), 1.0),
        "w_glu": nrm(ks[11], (L, D_SSM, D_SSM), D_SSM ** -0.5),
        "b_glu": nrm(ks[12], (L, D_SSM), 0.01),
        "g_q": 1.0 + nrm(ks[13], (L, HEAD_DIM), 0.02),
        "g_kc": 1.0 + nrm(ks[14], (L, HEAD_DIM), 0.02),
        "g_ks": 1.0 + nrm(ks[15], (L, HEAD_DIM), 0.02),
        "g_kw": 1.0 + nrm(ks[16], (L, HEAD_DIM), 0.02),
        "pos_k": nrm(ks[17], (L, CMP_BLOCK, HEAD_DIM), 0.02),
        "pos_v": nrm(ks[18], (L, CMP_BLOCK, HEAD_DIM), 0.02),
        "w_ck1": nrm(ks[19], (L, CMP_BLOCK * HEAD_DIM, CMP_HIDDEN), (CMP_BLOCK * HEAD_DIM) ** -0.5),
        "w_ck2": nrm(ks[20], (L, CMP_HIDDEN, HEAD_DIM), CMP_HIDDEN ** -0.5),
        "w_cv1": nrm(ks[21], (L, CMP_BLOCK * HEAD_DIM, CMP_HIDDEN), (CMP_BLOCK * HEAD_DIM) ** -0.5),
        "w_cv2": nrm(ks[22], (L, CMP_HIDDEN, HEAD_DIM), CMP_HIDDEN ** -0.5),
        "out_g_ssm": 1.0 + nrm(ks[23], (L, D_SSM), 0.02),
        "out_g_att": 1.0 + nrm(ks[24], (L, D_ATT), 0.02),
        "w_out": nrm(ks[25], (L, D_MIX, D_MODEL), D_MIX ** -0.5),
        "norm2_g": 1.0 + nrm(ks[26], (L, D_MODEL), 0.02),
        "w_grp": nrm(ks[27], (L, D_MODEL, N_EXP_GROUPS), D_MODEL ** -0.5),
        "b_grp": nrm(ks[28], (L, N_EXP_GROUPS), 0.01),
        "w_exp": nrm(ks[29], (L, D_MODEL, N_EXPERTS), D_MODEL ** -0.5),
        "b_exp": nrm(ks[30], (L, N_EXPERTS), 0.01),
        "w_gate": nrm(ks[31], (L, N_EXPERTS, D_MODEL, D_EXPERT), D_MODEL ** -0.5),
        "w_up": nrm(ks[32], (L, N_EXPERTS, D_MODEL, D_EXPERT), D_MODEL ** -0.5),
        "w_down": nrm(ks[33], (L, N_EXPERTS, D_EXPERT, D_MODEL), D_EXPERT ** -0.5),
    }


def reference(x, norm1_g, w_in, lam_re, lam_im, log_step, b_re, b_im, c_re, c_im, d_skip,
              w_glu, b_glu, g_q, g_kc, g_ks, g_kw, pos_k, pos_v, w_ck1, w_ck2, w_cv1, w_cv2,
              out_g_ssm, out_g_att, w_out, norm2_g, w_grp, b_grp, w_exp, b_exp,
              w_gate, w_up, w_down):
    for l in range(DEPTH):
        h = rms_norm(x, norm1_g[l])
        proj = h @ w_in[l]
        y_ssm = s5_mixer(proj[..., :D_SSM], lam_re[l], lam_im[l], log_step[l], b_re[l], b_im[l],
                         c_re[l], c_im[l], d_skip[l], w_glu[l], b_glu[l])
        y_att = nsa_mixer(proj[..., D_SSM:], g_q[l], g_kc[l], g_ks[l], g_kw[l], pos_k[l], pos_v[l],
                          w_ck1[l], w_ck2[l], w_cv1[l], w_cv2[l])
        mix = jnp.concatenate([rms_norm(y_ssm, out_g_ssm[l]), rms_norm(y_att, out_g_att[l])], axis=-1)
        x = x + mix @ w_out[l]
        x = x + hier_moe(rms_norm(x, norm2_g[l]), w_grp[l], b_grp[l], w_exp[l], b_exp[l],
                         w_gate[l], w_up[l], w_down[l])
    return x
```

```python
import functools
import math

import jax
import jax.numpy as jnp
from jax import lax
from jax.experimental import pallas as pl
from jax.experimental.pallas import tpu as pltpu

D_MODEL = 1024
D_SSM = 512
SSM_CH = 16
SSM_GROUPS = D_SSM // SSM_CH
SSM_STATE = 64
D_ATT = 512
HEAD_DIM = 64
N_HEADS = D_ATT // HEAD_DIM
N_KV = 2
HPG = N_HEADS // N_KV
D_KV = N_KV * HEAD_DIM
N_BRANCH = 3
CMP_STRIDE = 16
CMP_BLOCK = 2 * CMP_STRIDE
CMP_HIDDEN = 256
SEL_BLOCK = 64
N_SELECT = 16
WINDOW = 512
N_EXP_GROUPS = 4
EXPERTS_PER_GROUP = 8
N_EXPERTS = N_EXP_GROUPS * EXPERTS_PER_GROUP
D_EXPERT = 256
EPS = 1e-6
NEG = -1e30
FORCE = 1e9

LANES = 128
SSM_Q = 8
SSM_LT = D_SSM // LANES
ROUTER_OFF = N_EXP_GROUPS
VMEM_LIMIT = 56 * 1024 * 1024

F32 = jnp.float32
BF16 = jnp.bfloat16


def _dot(a, b):
    return jnp.dot(a, b, preferred_element_type=F32)


def _dot_nt(a, b):
    return lax.dot_general(a, b, (((1,), (1,)), ((), ())), preferred_element_type=F32)


def _split_dot(x, w):
    hi = x.astype(BF16)
    lo = (x - hi.astype(F32)).astype(BF16)
    return _dot(hi, w) + _dot(lo, w)


def _gelu(x):
    c = math.sqrt(2.0 / math.pi)
    return 0.5 * x * (1.0 + jnp.tanh(c * (x + 0.044715 * (x * x * x))))


def _sigmoid(x):
    return 1.0 / (1.0 + jnp.exp(-x))


def _params(sem):
    return pltpu.CompilerParams(dimension_semantics=sem, vmem_limit_bytes=VMEM_LIMIT)


def _in_proj_kernel(x_ref, g1_ref, wssm_ref, wq_ref, wkv_ref, wgt_ref, gq_ref, gks_ref, gkw_ref,
                    bd512_ref, bd128_ref,
                    u_ref, q_ref, kc_ref, vc_ref, ksa_ref, vs_ref, kw_ref, vw_ref, gate_ref, *, tm, nl):
    x = x_ref[...]
    ms = jnp.mean(x * x, axis=-1, keepdims=True)
    hn = (x * lax.rsqrt(ms + EPS) * g1_ref[...]).astype(BF16)

    u_ref[...] = _dot(hn, wssm_ref[...])

    q = _dot(hn, wq_ref[...])
    qss = _split_dot(q * q, bd512_ref[...])
    q_ref[...] = (q * lax.rsqrt(qss * (1.0 / HEAD_DIM) + EPS) * gq_ref[...]).astype(BF16)

    kv = _dot(hn, wkv_ref[...])
    kc, vc, ks, vs, kw, vw = [kv[:, i * D_KV:(i + 1) * D_KV] for i in range(6)]
    kss = _split_dot(ks * ks, bd128_ref[...])
    ksn = ks * lax.rsqrt(kss * (1.0 / HEAD_DIM) + EPS) * gks_ref[...]
    kws = _split_dot(kw * kw, bd128_ref[...])
    kwn = kw * lax.rsqrt(kws * (1.0 / HEAD_DIM) + EPS) * gkw_ref[...]

    t0 = (pl.program_id(0) % nl) * tm
    tpos = t0 + lax.broadcasted_iota(jnp.int32, (tm, HEAD_DIM), 0)
    blk = lax.broadcasted_iota(jnp.int32, (tm, HEAD_DIM), 1)
    onehot = jnp.where(tpos // SEL_BLOCK == blk, 1.0, 0.0).astype(BF16)
    for g in range(N_KV):
        sl = slice(g * HEAD_DIM, (g + 1) * HEAD_DIM)
        kc_ref[g] = kc[:, sl].astype(BF16)
        vc_ref[g] = vc[:, sl].astype(BF16)
        ksa_ref[g] = jnp.concatenate([ksn[:, sl].astype(BF16), onehot], axis=1)
        vs_ref[g] = vs[:, sl].astype(BF16)
        kw_ref[g] = kwn[:, sl].astype(BF16)
        vw_ref[g] = vw[:, sl].astype(BF16)

    gate_ref[...] = _sigmoid(_dot(hn, wgt_ref[...]))


def _in_proj(x2d, g1, wssm, wq, wkv, wgt, gq, gks, gkw, bd512, bd128, *, bsz, seq):
    tm = 512
    nl = seq // tm
    n_tok = bsz * seq
    kern = functools.partial(_in_proj_kernel, tm=tm, nl=nl)
    row = lambda i: (i, 0)
    const = lambda i: (0, 0)
    bgl = lambda i: (i // nl, 0, i % nl, 0)

    def kvspec(width):
        return pl.BlockSpec((None, N_KV, tm, width), bgl)

    def kvshape(width):
        return jax.ShapeDtypeStruct((bsz, N_KV, seq, width), BF16)

    return pl.pallas_call(
        kern,
        grid=(n_tok // tm,),
        in_specs=[
            pl.BlockSpec((tm, D_MODEL), row),
            pl.BlockSpec((1, D_MODEL), const),
            pl.BlockSpec((D_MODEL, D_SSM), const),
            pl.BlockSpec((D_MODEL, D_ATT), const),
            pl.BlockSpec((D_MODEL, 6 * D_KV), const),
            pl.BlockSpec((D_MODEL, N_KV * LANES), const),
            pl.BlockSpec((1, D_ATT), const),
            pl.BlockSpec((1, D_KV), const),
            pl.BlockSpec((1, D_KV), const),
            pl.BlockSpec((D_ATT, D_ATT), const),
            pl.BlockSpec((D_KV, D_KV), const),
        ],
        out_specs=[
            pl.BlockSpec((tm, D_SSM), row),
            pl.BlockSpec((tm, D_ATT), row),
            kvspec(HEAD_DIM), kvspec(HEAD_DIM), kvspec(2 * HEAD_DIM),
            kvspec(HEAD_DIM), kvspec(HEAD_DIM), kvspec(HEAD_DIM),
            pl.BlockSpec((tm, N_KV * LANES), row),
        ],
        out_shape=[
            jax.ShapeDtypeStruct((n_tok, D_SSM), F32),
            jax.ShapeDtypeStruct((n_tok, D_ATT), BF16),
            kvshape(HEAD_DIM), kvshape(HEAD_DIM), kvshape(2 * HEAD_DIM),
            kvshape(HEAD_DIM), kvshape(HEAD_DIM), kvshape(HEAD_DIM),
            jax.ShapeDtypeStruct((n_tok, N_KV * LANES), F32),
        ],
        compiler_params=_params(("parallel",)),
        name="in_proj",
    )(x2d, g1, wssm, wq, wkv, wgt, gq, gks, gkw, bd512, bd128)


def _s5_weights(lam_re, lam_im, log_step, b_re, b_im, c_re, c_im, d_skip, n_sub):
    q = SSM_Q
    lam = lax.complex(lam_re.astype(F32), lam_im.astype(F32))
    step = jnp.exp(log_step.astype(F32))[:, None]
    lam_bar = jnp.exp(lam * step)
    b_bar = ((lam_bar - 1.0) / lam)[..., None] * lax.complex(b_re.astype(F32), b_im.astype(F32))
    c = lax.complex(c_re.astype(F32), c_im.astype(F32))
    pows = [jnp.ones_like(lam_bar)]
    for _ in range(q):
        pows.append(pows[-1] * lam_bar)
    pw = jnp.stack(pows)
    eye = jnp.eye(LANES // SSM_CH, dtype=F32)
    lt, a8 = SSM_LT, LANES // SSM_CH

    kk = jnp.real(jnp.einsum('ghp,kgp,gpi->kghi', c, pw[:q], b_bar))
    lag = jnp.arange(q)[None, :] - jnp.arange(q)[:, None]
    kt = kk[jnp.clip(lag, 0, q - 1)] * (lag >= 0)[:, :, None, None, None].astype(F32)
    kt = kt.reshape(q, q, lt, a8, SSM_CH, SSM_CH)
    t_in = jnp.einsum('sjlahi,ab->lsaijbh', kt, eye).reshape(lt, q * LANES, q * LANES)

    wc = pw[q - 1 - jnp.arange(q)][..., None] * b_bar[None]
    wr = jnp.stack([jnp.real(wc), jnp.imag(wc)]).reshape(2, q, lt, a8, SSM_STATE, SSM_CH)
    w_loc = jnp.einsum('rslapi,ab->lsairbp', wr, eye).reshape(lt, q * LANES, 2 * a8 * SSM_STATE)

    cl = c[None] * pw[1:q + 1][:, :, None, :]
    mr = jnp.stack([jnp.real(cl), -jnp.imag(cl)]).reshape(2, q, lt, a8, SSM_CH, SSM_STATE)
    m_st = jnp.einsum('rjlahp,ab->lrapjbh', mr, eye).reshape(lt, 2 * a8 * SSM_STATE, q * LANES)

    n_lvl = max(1, (n_sub - 1).bit_length())
    lv = [pw[q]]
    for _ in range(n_lvl - 1):
        lv.append(lv[-1] * lv[-1])
    lvs = jnp.stack(lv).reshape(n_lvl, lt, 1, a8 * SSM_STATE)
    pw_re = jnp.real(lvs).transpose(1, 0, 2, 3)
    pw_im = jnp.imag(lvs).transpose(1, 0, 2, 3)
    dvec = jnp.tile(d_skip.astype(F32).reshape(lt, 1, LANES), (1, 1, q))
    return w_loc.astype(BF16), t_in.astype(BF16), m_st.astype(BF16), pw_re, pw_im, dvec


def _s5_kernel(u_ref, w_ref, t_ref, m_ref, pwr_ref, pwi_ref, d_ref, y_ref, *, n_sub, n_lvl):
    half = (LANES // SSM_CH) * SSM_STATE
    u = u_ref[...]
    ub = u.astype(BF16)
    s_loc = _dot(ub, w_ref[...])
    re = s_loc[:, :half]
    im = s_loc[:, half:]
    rowi = lax.broadcasted_iota(jnp.int32, (n_sub, half), 0)
    for k in range(n_lvl):
        d = 1 << k
        ar = pwr_ref[k]
        ai = pwi_ref[k]
        keep = rowi >= d
        sre = jnp.where(keep, pltpu.roll(re, d, axis=0), 0.0)
        sim = jnp.where(keep, pltpu.roll(im, d, axis=0), 0.0)
        re, im = re + (ar * sre - ai * sim), im + (ar * sim + ai * sre)
    keep = rowi >= 1
    xre = jnp.where(keep, pltpu.roll(re, 1, axis=0), 0.0)
    xim = jnp.where(keep, pltpu.roll(im, 1, axis=0), 0.0)
    xst = jnp.concatenate([xre, xim], axis=1).astype(BF16)
    y = _dot(ub, t_ref[...]) + _dot(xst, m_ref[...])
    y_ref[...] = y + d_ref[...] * u


def _s5(ucat, w_loc, t_in, m_st, pw_re, pw_im, dvec, *, bsz, n_sub):
    q = SSM_Q
    n_lvl = pw_re.shape[1]
    kern = functools.partial(_s5_kernel, n_sub=n_sub, n_lvl=n_lvl)
    wide = q * LANES
    return pl.pallas_call(
        kern,
        grid=(bsz, SSM_LT),
        in_specs=[
            pl.BlockSpec((None, n_sub, wide), lambda b, l: (b, 0, l)),
            pl.BlockSpec((None, wide, wide), lambda b, l: (l, 0, 0)),
            pl.BlockSpec((None, wide, wide), lambda b, l: (l, 0, 0)),
            pl.BlockSpec((None, wide, wide), lambda b, l: (l, 0, 0)),
            pl.BlockSpec((None, n_lvl, 1, wide // 2), lambda b, l: (l, 0, 0, 0)),
            pl.BlockSpec((None, n_lvl, 1, wide // 2), lambda b, l: (l, 0, 0, 0)),
            pl.BlockSpec((None, 1, wide), lambda b, l: (l, 0, 0)),
        ],
        out_specs=pl.BlockSpec((None, n_sub, wide), lambda b, l: (b, 0, l)),
        out_shape=jax.ShapeDtypeStruct((bsz, n_sub, SSM_LT * wide), F32),
        compiler_params=_params(("parallel", "parallel")),
        name="s5",
    )(ucat, w_loc, t_in, m_st, pw_re, pw_im, dvec)


def _compress_kernel(kc_ref, vc_ref, w1k_ref, w2k_ref, w1v_ref, w2v_ref, posk_ref, posv_ref, gkc_ref,
                     kcmp_ref, vcmp_ref, *, nch):
    half = CMP_STRIDE * HEAD_DIM

    def mlp(x_ref, w1_ref, w2_ref, pos_ref):
        x = x_ref[...]
        a = _dot(x, w1_ref[:half, :])
        b = _dot(x, w1_ref[half:, :])
        pv = _dot(pos_ref[...], w1_ref[...])[0:1, :]
        hid = a + pltpu.roll(b, nch - 1, axis=0) + pv
        return _dot(_gelu(hid).astype(BF16), w2_ref[...])

    k = mlp(kc_ref, w1k_ref, w2k_ref, posk_ref)
    ms = jnp.mean(k * k, axis=-1, keepdims=True)
    kcmp_ref[...] = (k * lax.rsqrt(ms + EPS) * gkc_ref[...]).astype(BF16)
    v = mlp(vc_ref, w1v_ref, w2v_ref, posv_ref)
    rowi = lax.broadcasted_iota(jnp.int32, v.shape, 0)
    vcmp_ref[...] = jnp.where(rowi < nch - 1, v, 0.0).astype(BF16)


def _compress(kcf, vcf, w1k, w2k, w1v, w2v, posk, posv, gkc, *, bsz, nch):
    kern = functools.partial(_compress_kernel, nch=nch)
    wide = CMP_STRIDE * HEAD_DIM
    xspec = pl.BlockSpec((None, None, nch, wide), lambda b, g: (b, g, 0, 0))
    ospec = pl.BlockSpec((None, None, nch, HEAD_DIM), lambda b, g: (b, g, 0, 0))
    c2 = lambda b, g: (0, 0)
    return pl.pallas_call(
        kern,
        grid=(bsz, N_KV),
        in_specs=[
            xspec, xspec,
            pl.BlockSpec((2 * wide, CMP_HIDDEN), c2), pl.BlockSpec((CMP_HIDDEN, HEAD_DIM), c2),
            pl.BlockSpec((2 * wide, CMP_HIDDEN), c2), pl.BlockSpec((CMP_HIDDEN, HEAD_DIM), c2),
            pl.BlockSpec((8, 2 * wide), c2), pl.BlockSpec((8, 2 * wide), c2),
            pl.BlockSpec((1, HEAD_DIM), c2),
        ],
        out_specs=[ospec, ospec],
        out_shape=[jax.ShapeDtypeStruct((bsz, N_KV, nch, HEAD_DIM), BF16)] * 2,
        compiler_params=_params(("parallel", "parallel")),
        name="compress",
    )(kcf, vcf, w1k, w2k, w1v, w2v, posk, posv, gkc)


def _nsa_kernel(q_ref, kcmp_ref, vcmp_ref, ksa_ref, vs_ref, kw_ref, vw_ref, gate_ref, ovl_ref, o_ref,
                *, tq, tk, nch):
    qi = pl.program_id(2)
    q0 = qi * tq
    rows = HPG * tq
    qt = q_ref[...]
    qs = jnp.concatenate([qt[:, h * HEAD_DIM:(h + 1) * HEAD_DIM] for h in range(HPG)], axis=0)
    tpos = q0 + lax.broadcasted_iota(jnp.int32, (rows, 1), 0) % tq

    s = _dot_nt(qs, kcmp_ref[...])
    cend = lax.broadcasted_iota(jnp.int32, (rows, nch), 1) * CMP_STRIDE + (CMP_BLOCK - 1)
    s = jnp.where(cend <= tpos, s, NEG)
    m = jnp.max(s, axis=1, keepdims=True)
    p = jnp.exp(s - m)
    p = p / jnp.sum(p, axis=1, keepdims=True)
    p = jnp.where(tpos >= CMP_BLOCK - 1, p, 0.0)
    o_cmp = _dot(p.astype(BF16), vcmp_ref[...])

    psum = p[0:tq]
    for h in range(1, HPG):
        psum = psum + p[h * tq:(h + 1) * tq]
    imp = _split_dot(psum, ovl_ref[...])
    blk = lax.broadcasted_iota(jnp.int32, (tq, LANES), 1)
    cur = (q0 + lax.broadcasted_iota(jnp.int32, (tq, LANES), 0)) // SEL_BLOCK
    forced = (blk == 0) | (blk == cur) | (blk == cur - 1)
    imp = jnp.where(forced, FORCE, jnp.where(blk <= cur, imp, NEG))
    ns = SEL_BLOCK
    imp_t = imp.T[:ns]
    jidx = lax.broadcasted_iota(jnp.int32, (ns, tq), 0)
    rank = jnp.zeros((ns, tq), F32)
    for i in range(ns):
        ri = imp_t[i:i + 1, :]
        ahead = (ri > imp_t) | ((ri == imp_t) & (jidx > i))
        rank = rank + jnp.where(ahead, 1.0, 0.0)
    sel_t = jnp.where(rank < N_SELECT, 0.0, NEG)
    selb = jnp.concatenate([sel_t, sel_t], axis=0).T[:, :ns].astype(BF16)
    qa = jnp.concatenate([qs, jnp.concatenate([selb] * HPG, axis=0)], axis=1)

    def online(s, vblk, carry):
        m, l, acc = carry
        m_new = jnp.maximum(m, jnp.max(s, axis=1, keepdims=True))
        alpha = jnp.exp(m - m_new)
        p = jnp.exp(s - m_new)
        l = alpha * l + jnp.sum(p, axis=1, keepdims=True)
        acc = alpha * acc + _dot(p.astype(BF16), vblk)
        return m_new, l, acc

    init = (jnp.full((rows, 1), NEG, F32), jnp.zeros((rows, 1), F32), jnp.zeros((rows, HEAD_DIM), F32))
    kcol = lax.broadcasted_iota(jnp.int32, (rows, tk), 1)

    def sel_body(kt, carry):
        k0 = pl.multiple_of(kt * tk, tk)
        s = _dot_nt(qa, ksa_ref[pl.ds(k0, tk), :])
        s = jnp.where(k0 + kcol <= tpos, s, NEG)
        return online(s, vs_ref[pl.ds(k0, tk), :], carry)

    n_kt = (q0 + tq) // tk
    m, l, acc = lax.fori_loop(0, n_kt, sel_body, init)
    o_sel = acc / l

    def win_body(kt, carry):
        k0 = pl.multiple_of(kt * tk, tk)
        s = _dot_nt(qs, kw_ref[pl.ds(k0, tk), :])
        kpos = k0 + kcol
        s = jnp.where((kpos <= tpos) & (kpos > tpos - WINDOW), s, NEG)
        return online(s, vw_ref[pl.ds(k0, tk), :], carry)

    kt_lo = jnp.maximum(q0 - WINDOW, 0) // tk
    m, l, acc = lax.fori_loop(kt_lo, n_kt, win_body, init)
    o_win = acc / l

    gt = gate_ref[...]
    outs = []
    for h in range(HPG):
        r = slice(h * tq, (h + 1) * tq)
        outs.append(gt[:, 3 * h:3 * h + 1] * o_cmp[r] + gt[:, 3 * h + 1:3 * h + 2] * o_sel[r]
                    + gt[:, 3 * h + 2:3 * h + 3] * o_win[r])
    o_ref[...] = jnp.concatenate(outs, axis=1)


def _nsa(q, kcmp, vcmp, ksa, vs, kw, vw, gate, ovl, *, bsz, seq, nch):
    tq = 256
    tk = 256
    nq = seq // tq
    kern = functools.partial(_nsa_kernel, tq=tq, tk=tk, nch=nch)
    full = lambda b, g, i: (b, g, 0, 0)
    qd = HPG * HEAD_DIM
    return pl.pallas_call(
        kern,
        grid=(bsz, N_KV, nq),
        in_specs=[
            pl.BlockSpec((None, tq, qd), lambda b, g, i: (b, i, g)),
            pl.BlockSpec((None, None, nch, HEAD_DIM), full),
            pl.BlockSpec((None, None, nch, HEAD_DIM), full),
            pl.BlockSpec((None, None, seq, 2 * HEAD_DIM), full),
            pl.BlockSpec((None, None, seq, HEAD_DIM), full),
            pl.BlockSpec((None, None, seq, HEAD_DIM), full),
            pl.BlockSpec((None, None, seq, HEAD_DIM), full),
            pl.BlockSpec((None, tq, LANES), lambda b, g, i: (b, i, g)),
            pl.BlockSpec((nch, LANES), lambda b, g, i: (0, 0)),
        ],
        out_specs=pl.BlockSpec((None, tq, qd), lambda b, g, i: (b, i, g)),
        out_shape=jax.ShapeDtypeStruct((bsz, seq, D_ATT), F32),
        compiler_params=_params(("parallel", "parallel", "arbitrary")),
        name="nsa",
    )(q, kcmp, vcmp, ksa, vs, kw, vw, gate, ovl)


def _out_proj_kernel(ys_ref, ya_ref, x_ref, wglu_ref, bglu_ref, gs_ref, ga_ref, wo_ref, g2_ref, wr_ref, br_ref,
                     x2_ref, h2_ref, comb_ref):
    y = _gelu(ys_ref[...])
    y = y * _sigmoid(_dot(y.astype(BF16), wglu_ref[...]) + bglu_ref[...])
    ysn = y * lax.rsqrt(jnp.mean(y * y, axis=-1, keepdims=True) + EPS) * gs_ref[...]
    ya = ya_ref[...]
    yan = ya * lax.rsqrt(jnp.mean(ya * ya, axis=-1, keepdims=True) + EPS) * ga_ref[...]
    x2 = x_ref[...] + _dot(ysn.astype(BF16), wo_ref[:D_SSM, :]) + _dot(yan.astype(BF16), wo_ref[D_SSM:, :])
    x2_ref[...] = x2
    h2 = (x2 * lax.rsqrt(jnp.mean(x2 * x2, axis=-1, keepdims=True) + EPS) * g2_ref[...]).astype(BF16)
    h2_ref[...] = h2

    logits = _dot(h2, wr_ref[...]) + br_ref[...]
    lane = lax.broadcasted_iota(jnp.int32, logits.shape, 1).astype(F32)
    far = float(LANES)
    is_g = lane < N_EXP_GROUPS
    glog = jnp.where(is_g, logits, -jnp.inf)
    gmax = jnp.max(glog, axis=1, keepdims=True)
    gsum = jnp.sum(jnp.where(is_g, jnp.exp(logits - gmax), 0.0), axis=1, keepdims=True)
    gsel = jnp.min(jnp.where(glog == gmax, lane, far), axis=1, keepdims=True)
    gprob = 1.0 / gsum
    lo = ROUTER_OFF + EXPERTS_PER_GROUP * gsel
    in_e = (lane >= lo) & (lane < lo + EXPERTS_PER_GROUP)
    emax = jnp.max(jnp.where(in_e, logits, -jnp.inf), axis=1, keepdims=True)
    eexp = jnp.where(in_e, jnp.exp(logits - emax), 0.0)
    eprob = jnp.where(in_e, eexp / jnp.sum(eexp, axis=1, keepdims=True), -1.0)
    v1 = jnp.max(eprob, axis=1, keepdims=True)
    i1 = jnp.min(jnp.where(eprob == v1, lane, far), axis=1, keepdims=True)
    rest = jnp.where(lane == i1, -1.0, eprob)
    v2 = jnp.max(rest, axis=1, keepdims=True)
    i2 = jnp.min(jnp.where(rest == v2, lane, far), axis=1, keepdims=True)
    den = v1 + v2
    comb_ref[...] = (jnp.where(lane == i1, v1 / den * gprob, 0.0)
                     + jnp.where(lane == i2, v2 / den * gprob, 0.0))


def _out_proj(ys, ya, x2d, wglu, bglu, gs, ga, wo, g2, wr, br):
    n_tok = x2d.shape[0]
    tm = 512
    row = lambda i: (i, 0)
    const = lambda i: (0, 0)
    return pl.pallas_call(
        _out_proj_kernel,
        grid=(n_tok // tm,),
        in_specs=[
            pl.BlockSpec((tm, D_SSM), row),
            pl.BlockSpec((tm, D_ATT), row),
            pl.BlockSpec((tm, D_MODEL), row),
            pl.BlockSpec((D_SSM, D_SSM), const),
            pl.BlockSpec((1, D_SSM), const),
            pl.BlockSpec((1, D_SSM), const),
            pl.BlockSpec((1, D_ATT), const),
            pl.BlockSpec((D_SSM + D_ATT, D_MODEL), const),
            pl.BlockSpec((1, D_MODEL), const),
            pl.BlockSpec((D_MODEL, LANES), const),
            pl.BlockSpec((1, LANES), const),
        ],
        out_specs=[
            pl.BlockSpec((tm, D_MODEL), row),
            pl.BlockSpec((tm, D_MODEL), row),
            pl.BlockSpec((tm, LANES), row),
        ],
        out_shape=[
            jax.ShapeDtypeStruct((n_tok, D_MODEL), F32),
            jax.ShapeDtypeStruct((n_tok, D_MODEL), BF16),
            jax.ShapeDtypeStruct((n_tok, LANES), F32),
        ],
        compiler_params=_params(("parallel",)),
        name="out_proj",
    )(ys, ya, x2d, wglu, bglu, gs, ga, wo, g2, wr, br)


def _moe_kernel(h_ref, x2_ref, comb_ref, wg_ref, wu_ref, wd_ref, o_ref):
    e = pl.program_id(1)

    @pl.when(e == 0)
    def _():
        o_ref[...] = x2_ref[...]

    h = h_ref[...]
    gate = _dot(h, wg_ref[...])
    up = _dot(h, wu_ref[...])
    comb = comb_ref[...]
    lane = lax.broadcasted_iota(jnp.int32, comb.shape, 1)
    cw = jnp.sum(jnp.where(lane == e + ROUTER_OFF, comb, 0.0), axis=1, keepdims=True)
    a = gate * _sigmoid(gate) * up * cw
    o_ref[...] += _dot(a.astype(BF16), wd_ref[...])


def _moe(h2, x2, comb, wg, wu, wd):
    n_tok = h2.shape[0]
    tm = 1024
    row = lambda i, e: (i, 0)
    return pl.pallas_call(
        _moe_kernel,
        grid=(n_tok // tm, N_EXPERTS),
        in_specs=[
            pl.BlockSpec((tm, D_MODEL), row),
            pl.BlockSpec((tm, D_MODEL), row),
            pl.BlockSpec((tm, LANES), row),
            pl.BlockSpec((None, D_MODEL, D_EXPERT), lambda i, e: (e, 0, 0)),
            pl.BlockSpec((None, D_MODEL, D_EXPERT), lambda i, e: (e, 0, 0)),
            pl.BlockSpec((None, D_EXPERT, D_MODEL), lambda i, e: (e, 0, 0)),
        ],
        out_specs=pl.BlockSpec((tm, D_MODEL), row),
        out_shape=jax.ShapeDtypeStruct((n_tok, D_MODEL), F32),
        compiler_params=_params(("parallel", "arbitrary")),
        name="moe",
    )(h2, x2, comb, wg, wu, wd)


def _block_diag_ones(n, blk):
    i = jnp.arange(n) // blk
    return (i[:, None] == i[None, :]).astype(BF16)


def _layer(x, norm1_g, w_in, lam_re, lam_im, log_step, b_re, b_im, c_re, c_im, d_skip,
           w_glu, b_glu, g_q, g_kc, g_ks, g_kw, pos_k, pos_v, w_ck1, w_ck2, w_cv1, w_cv2,
           out_g_ssm, out_g_att, w_out, norm2_g, w_grp, b_grp, w_exp, b_exp, w_gate, w_up, w_down):
    bsz, seq, _ = x.shape
    n_tok = bsz * seq
    x2d = x.reshape(n_tok, D_MODEL)
    q8 = SSM_Q
    n_sub = seq // q8
    nch = seq // CMP_STRIDE

    o_q = D_SSM
    o_kv = D_SSM + D_ATT
    o_gt = o_kv + 6 * D_KV
    wssm = w_in[:, :o_q].astype(BF16)
    wq = w_in[:, o_q:o_kv].astype(BF16)
    wkv = w_in[:, o_kv:o_gt].astype(BF16)
    per_g = HPG * N_BRANCH
    wgt = jnp.zeros((D_MODEL, N_KV * LANES), F32)
    for g in range(N_KV):
        wgt = wgt.at[:, g * LANES:g * LANES + per_g].set(w_in[:, o_gt + g * per_g:o_gt + (g + 1) * per_g])
    wgt = wgt.astype(BF16)
    gq = (jnp.tile(g_q.astype(F32), N_HEADS) * (HEAD_DIM ** -0.5)).reshape(1, D_ATT)
    gks = jnp.tile(g_ks.astype(F32), N_KV).reshape(1, D_KV)
    gkw = jnp.tile(g_kw.astype(F32), N_KV).reshape(1, D_KV)

    u, q, kc, vc, ksa, vs, kw, vw, gate = _in_proj(
        x2d, norm1_g.reshape(1, D_MODEL), wssm, wq, wkv, wgt, gq, gks, gkw,
        _block_diag_ones(D_ATT, HEAD_DIM), _block_diag_ones(D_KV, HEAD_DIM), bsz=bsz, seq=seq)

    w_loc, t_in, m_st, pw_re, pw_im, dvec = _s5_weights(
        lam_re, lam_im, log_step, b_re, b_im, c_re, c_im, d_skip, n_sub)
    ucat = (u.reshape(bsz, n_sub, q8, SSM_LT, LANES).transpose(0, 1, 3, 2, 4)
            .reshape(bsz, n_sub, SSM_LT * q8 * LANES))
    ycat = _s5(ucat, w_loc, t_in, m_st, pw_re, pw_im, dvec, bsz=bsz, n_sub=n_sub)
    ys = (ycat.reshape(bsz, n_sub, SSM_LT, q8, LANES).transpose(0, 1, 3, 2, 4)
          .reshape(n_tok, D_SSM))

    wide = CMP_STRIDE * HEAD_DIM
    pad8 = lambda p: jnp.zeros((8, 2 * wide), F32).at[0].set(p.reshape(-1)).astype(BF16)
    kcmp, vcmp = _compress(
        kc.reshape(bsz, N_KV, nch, wide), vc.reshape(bsz, N_KV, nch, wide),
        w_ck1.astype(BF16), w_ck2.astype(BF16), w_cv1.astype(BF16), w_cv2.astype(BF16),
        pad8(pos_k), pad8(pos_v), g_kc.astype(F32).reshape(1, HEAD_DIM), bsz=bsz, nch=nch)
    cstart = jnp.arange(nch) * CMP_STRIDE
    sstart = jnp.arange(LANES) * SEL_BLOCK
    ovl = ((cstart[:, None] < sstart[None, :] + SEL_BLOCK) & (cstart[:, None] + CMP_BLOCK > sstart[None, :])
           & (jnp.arange(LANES)[None, :] < seq // SEL_BLOCK) & (jnp.arange(nch)[:, None] < nch - 1)).astype(BF16)
    ya = _nsa(q.reshape(bsz, seq, D_ATT), kcmp, vcmp, ksa, vs, kw, vw,
              gate.reshape(bsz, seq, N_KV * LANES), ovl, bsz=bsz, seq=seq, nch=nch)

    wr = jnp.zeros((D_MODEL, LANES), F32)
    wr = wr.at[:, :N_EXP_GROUPS].set(w_grp).at[:, ROUTER_OFF:ROUTER_OFF + N_EXPERTS].set(w_exp).astype(BF16)
    br = jnp.zeros((1, LANES), F32)
    br = br.at[0, :N_EXP_GROUPS].set(b_grp).at[0, ROUTER_OFF:ROUTER_OFF + N_EXPERTS].set(b_exp)
    x2, h2, comb = _out_proj(
        ys, ya.reshape(n_tok, D_ATT), x2d, w_glu.astype(BF16), b_glu.reshape(1, D_SSM).astype(F32),
        out_g_ssm.reshape(1, D_SSM).astype(F32), out_g_att.reshape(1, D_ATT).astype(F32),
        w_out.astype(BF16), norm2_g.reshape(1, D_MODEL).astype(F32), wr, br)

    out = _moe(h2, x2, comb, w_gate.astype(BF16), w_up.astype(BF16), w_down.astype(BF16))
    return out.reshape(bsz, seq, D_MODEL)


def kernel(x, norm1_g, w_in, lam_re, lam_im, log_step, b_re, b_im, c_re, c_im, d_skip, w_glu, b_glu, g_q, g_kc, g_ks, g_kw, pos_k, pos_v, w_ck1, w_ck2, w_cv1, w_cv2, out_g_ssm, out_g_att, w_out, norm2_g, w_grp, b_grp, w_exp, b_exp, w_gate, w_up, w_down):
    depth = norm1_g.shape[0]
    for l in range(depth):
        x = _layer(x, norm1_g[l], w_in[l], lam_re[l], lam_im[l], log_step[l], b_re[l], b_im[l], c_re[l],
                   c_im[l], d_skip[l], w_glu[l], b_glu[l], g_q[l], g_kc[l], g_ks[l], g_kw[l], pos_k[l],
                   pos_v[l], w_ck1[l], w_ck2[l], w_cv1[l], w_cv2[l], out_g_ssm[l], out_g_att[l], w_out[l],
                   norm2_g[l], w_grp[l], b_grp[l], w_exp[l], b_exp[l], w_gate[l], w_up[l], w_down[l])
    return x
```

```python
import functools
import math

import jax
import jax.numpy as jnp
from jax import lax
from jax.experimental import pallas as pl
from jax.experimental.pallas import tpu as pltpu

D_MODEL = 1024
D_SSM = 512
SSM_CH = 16
SSM_GROUPS = D_SSM // SSM_CH
SSM_STATE = 64
D_ATT = 512
HEAD_DIM = 64
N_HEADS = D_ATT // HEAD_DIM
N_KV = 2
HPG = N_HEADS // N_KV
D_KV = N_KV * HEAD_DIM
N_BRANCH = 3
CMP_STRIDE = 16
CMP_BLOCK = 2 * CMP_STRIDE
CMP_HIDDEN = 256
SEL_BLOCK = 64
N_SELECT = 16
WINDOW = 512
N_EXP_GROUPS = 4
EXPERTS_PER_GROUP = 8
N_EXPERTS = N_EXP_GROUPS * EXPERTS_PER_GROUP
D_EXPERT = 256
EPS = 1e-6
NEG = -1e30
FORCE = 1e9

LANES = 128
SSM_Q = 8
SSM_LT = D_SSM // LANES
ROUTER_OFF = N_EXP_GROUPS
NSA_TQ = 256
NSA_TK = 256
NSA_SPLIT = 1
V_ROWS = HEAD_DIM + 16
MAX_SEL_BLOCKS = 64
GATE_ROWS = 16
VMEM_LIMIT = 56 * 1024 * 1024

F32 = jnp.float32
BF16 = jnp.bfloat16


def _dot(a, b):
    return jnp.dot(a, b, preferred_element_type=F32)


def _dot_nt(a, b):
    return lax.dot_general(a, b, (((1,), (1,)), ((), ())), preferred_element_type=F32)


def _split_dot(x, w):
    hi = x.astype(BF16)
    lo = (x - hi.astype(F32)).astype(BF16)
    return _dot(hi, w) + _dot(lo, w)


def _gelu(x):
    c = math.sqrt(2.0 / math.pi)
    return 0.5 * x * (1.0 + jnp.tanh(c * (x + 0.044715 * (x * x * x))))


def _sigmoid(x):
    return 1.0 / (1.0 + jnp.exp(-x))


def _params(sem):
    return pltpu.CompilerParams(dimension_semantics=sem, vmem_limit_bytes=VMEM_LIMIT)


def _in_proj_kernel(x_ref, g1_ref, wrow_ref, wcol_ref, gq_ref, gks_ref, gkw_ref, bd128_ref,
                    u_ref, qt_ref, kc_ref, vc_ref, ksa_ref, kw_ref, vst_ref, vwt_ref, gate_ref, *, tm, nl):
    x = x_ref[...]
    ms = jnp.mean(x * x, axis=-1, keepdims=True)
    hn = (x * lax.rsqrt(ms + EPS) * g1_ref[...]).astype(BF16)

    pr = _dot(hn, wrow_ref[...])
    u_ref[...] = pr[:, :D_SSM]
    kc, vc, ks, kw = [pr[:, D_SSM + i * D_KV:D_SSM + (i + 1) * D_KV] for i in range(4)]
    kss = _split_dot(ks * ks, bd128_ref[...])
    ksn = ks * lax.rsqrt(kss * (1.0 / HEAD_DIM) + EPS) * gks_ref[...]
    kws = _split_dot(kw * kw, bd128_ref[...])
    kwn = kw * lax.rsqrt(kws * (1.0 / HEAD_DIM) + EPS) * gkw_ref[...]
    t0 = (pl.program_id(0) % nl) * tm
    tpos = t0 + lax.broadcasted_iota(jnp.int32, (tm, MAX_SEL_BLOCKS), 0)
    blk = lax.broadcasted_iota(jnp.int32, (tm, MAX_SEL_BLOCKS), 1)
    onehot = jnp.where(tpos // SEL_BLOCK == blk, 1.0, 0.0).astype(BF16)
    for g in range(N_KV):
        sl = slice(g * HEAD_DIM, (g + 1) * HEAD_DIM)
        kc_ref[g] = kc[:, sl].astype(BF16)
        vc_ref[g] = vc[:, sl].astype(BF16)
        ksa_ref[g] = jnp.concatenate([ksn[:, sl].astype(BF16), onehot], axis=1)
        kw_ref[g] = kwn[:, sl].astype(BF16)

    pc = _dot_nt(wcol_ref[...], hn)
    gq = gq_ref[...]
    for h in range(N_HEADS):
        sl = slice(h * HEAD_DIM, (h + 1) * HEAD_DIM)
        qh = pc[sl]
        ss = jnp.sum(qh * qh, axis=0, keepdims=True)
        qt_ref[h] = (qh * lax.rsqrt(ss * (1.0 / HEAD_DIM) + EPS) * gq[sl]).astype(BF16)
    ones_rows = jnp.where(lax.broadcasted_iota(jnp.int32, (V_ROWS - HEAD_DIM, tm), 0) == 0, 1.0, 0.0)
    for g in range(N_KV):
        for o_ref, base in ((vst_ref, D_ATT), (vwt_ref, D_ATT + D_KV)):
            vt = jnp.concatenate([pc[base + g * HEAD_DIM:base + (g + 1) * HEAD_DIM], ones_rows], axis=0)
            vt = vt.astype(BF16)
            for j in range(tm // NSA_TK):
                o_ref[g, j] = vt[:, j * NSA_TK:(j + 1) * NSA_TK]
        gb = D_ATT + 2 * D_KV + g * GATE_ROWS
        gate_ref[g] = _sigmoid(pc[gb:gb + GATE_ROWS])


def _in_proj(x2d, g1, wrow, wcol, gq, gks, gkw, bd128, *, bsz, seq):
    tm = 512
    nl = seq // tm
    n_tok = bsz * seq
    kern = functools.partial(_in_proj_kernel, tm=tm, nl=nl)
    row = lambda i: (i, 0)
    const = lambda i: (0, 0)
    bgl = lambda i: (i // nl, 0, i % nl, 0)
    n_col = wcol.shape[0]
    jt = tm // NSA_TK

    def kvspec(width):
        return pl.BlockSpec((None, N_KV, tm, width), bgl)

    def kvshape(width):
        return jax.ShapeDtypeStruct((bsz, N_KV, seq, width), BF16)

    vt_spec = pl.BlockSpec((None, N_KV, jt, V_ROWS, NSA_TK), lambda i: (i // nl, 0, i % nl, 0, 0))
    vt_shape = jax.ShapeDtypeStruct((bsz, N_KV, seq // NSA_TK, V_ROWS, NSA_TK), BF16)
    return pl.pallas_call(
        kern,
        grid=(n_tok // tm,),
        in_specs=[
            pl.BlockSpec((tm, D_MODEL), row),
            pl.BlockSpec((1, D_MODEL), const),
            pl.BlockSpec((D_MODEL, D_SSM + 4 * D_KV), const),
            pl.BlockSpec((n_col, D_MODEL), const),
            pl.BlockSpec((D_ATT, 1), const),
            pl.BlockSpec((1, D_KV), const),
            pl.BlockSpec((1, D_KV), const),
            pl.BlockSpec((D_KV, D_KV), const),
        ],
        out_specs=[
            pl.BlockSpec((tm, D_SSM), row),
            pl.BlockSpec((None, N_HEADS, HEAD_DIM, tm), lambda i: (i // nl, 0, 0, i % nl)),
            kvspec(HEAD_DIM), kvspec(HEAD_DIM), kvspec(2 * HEAD_DIM), kvspec(HEAD_DIM),
            vt_spec, vt_spec,
            pl.BlockSpec((None, N_KV, GATE_ROWS, tm), lambda i: (i // nl, 0, 0, i % nl)),
        ],
        out_shape=[
            jax.ShapeDtypeStruct((n_tok, D_SSM), F32),
            jax.ShapeDtypeStruct((bsz, N_HEADS, HEAD_DIM, seq), BF16),
            kvshape(HEAD_DIM), kvshape(HEAD_DIM), kvshape(2 * HEAD_DIM), kvshape(HEAD_DIM),
            vt_shape, vt_shape,
            jax.ShapeDtypeStruct((bsz, N_KV, GATE_ROWS, seq), F32),
        ],
        compiler_params=_params(("parallel",)),
        name="in_proj",
    )(x2d, g1, wrow, wcol, gq, gks, gkw, bd128)


def _s5_weights(lam_re, lam_im, log_step, b_re, b_im, c_re, c_im, d_skip, n_sub):
    q = SSM_Q
    lam = lax.complex(lam_re.astype(F32), lam_im.astype(F32))
    step = jnp.exp(log_step.astype(F32))[:, None]
    lam_bar = jnp.exp(lam * step)
    b_bar = ((lam_bar - 1.0) / lam)[..., None] * lax.complex(b_re.astype(F32), b_im.astype(F32))
    c = lax.complex(c_re.astype(F32), c_im.astype(F32))
    pows = [jnp.ones_like(lam_bar)]
    for _ in range(q):
        pows.append(pows[-1] * lam_bar)
    pw = jnp.stack(pows)
    eye = jnp.eye(LANES // SSM_CH, dtype=F32)
    lt, a8 = SSM_LT, LANES // SSM_CH

    kk = jnp.real(jnp.einsum('ghp,kgp,gpi->kghi', c, pw[:q], b_bar))
    lag = jnp.arange(q)[None, :] - jnp.arange(q)[:, None]
    kt = kk[jnp.clip(lag, 0, q - 1)] * (lag >= 0)[:, :, None, None, None].astype(F32)
    kt = kt.reshape(q, q, lt, a8, SSM_CH, SSM_CH)
    t_in = jnp.einsum('sjlahi,ab->lsaijbh', kt, eye).reshape(lt, q * LANES, q * LANES)

    wc = pw[q - 1 - jnp.arange(q)][..., None] * b_bar[None]
    wr = jnp.stack([jnp.real(wc), jnp.imag(wc)]).reshape(2, q, lt, a8, SSM_STATE, SSM_CH)
    w_loc = jnp.einsum('rslapi,ab->lsairbp', wr, eye).reshape(lt, q * LANES, 2 * a8 * SSM_STATE)

    cl = c[None] * pw[1:q + 1][:, :, None, :]
    mr = jnp.stack([jnp.real(cl), -jnp.imag(cl)]).reshape(2, q, lt, a8, SSM_CH, SSM_STATE)
    m_st = jnp.einsum('rjlahp,ab->lrapjbh', mr, eye).reshape(lt, 2 * a8 * SSM_STATE, q * LANES)

    n_lvl = max(1, (n_sub - 1).bit_length())
    lv = [pw[q]]
    for _ in range(n_lvl - 1):
        lv.append(lv[-1] * lv[-1])
    lvs = jnp.stack(lv).reshape(n_lvl, lt, 1, a8 * SSM_STATE)
    pw_re = jnp.real(lvs).transpose(1, 0, 2, 3)
    pw_im = jnp.imag(lvs).transpose(1, 0, 2, 3)
    dvec = jnp.tile(d_skip.astype(F32).reshape(lt, 1, LANES), (1, 1, q))
    return w_loc.astype(BF16), t_in.astype(BF16), m_st.astype(BF16), pw_re, pw_im, dvec


def _s5_kernel(u_ref, w_ref, t_ref, m_ref, pwr_ref, pwi_ref, d_ref, y_ref, *, n_sub, n_lvl):
    half = (LANES // SSM_CH) * SSM_STATE
    u = u_ref[...]
    ub = u.astype(BF16)
    s_loc = _dot(ub, w_ref[...])
    re = s_loc[:, :half]
    im = s_loc[:, half:]
    rowi = lax.broadcasted_iota(jnp.int32, (n_sub, half), 0)
    for k in range(n_lvl):
        d = 1 << k
        ar = pwr_ref[k]
        ai = pwi_ref[k]
        keep = rowi >= d
        sre = jnp.where(keep, pltpu.roll(re, d, axis=0), 0.0)
        sim = jnp.where(keep, pltpu.roll(im, d, axis=0), 0.0)
        re, im = re + (ar * sre - ai * sim), im + (ar * sim + ai * sre)
    keep = rowi >= 1
    xre = jnp.where(keep, pltpu.roll(re, 1, axis=0), 0.0)
    xim = jnp.where(keep, pltpu.roll(im, 1, axis=0), 0.0)
    xst = jnp.concatenate([xre, xim], axis=1).astype(BF16)
    y = _dot(ub, t_ref[...]) + _dot(xst, m_ref[...])
    y_ref[...] = y + d_ref[...] * u


def _s5(ucat, w_loc, t_in, m_st, pw_re, pw_im, dvec, *, bsz, n_sub):
    q = SSM_Q
    n_lvl = pw_re.shape[1]
    kern = functools.partial(_s5_kernel, n_sub=n_sub, n_lvl=n_lvl)
    wide = q * LANES
    return pl.pallas_call(
        kern,
        grid=(bsz, SSM_LT),
        in_specs=[
            pl.BlockSpec((None, n_sub, wide), lambda b, l: (b, 0, l)),
            pl.BlockSpec((None, wide, wide), lambda b, l: (l, 0, 0)),
            pl.BlockSpec((None, wide, wide), lambda b, l: (l, 0, 0)),
            pl.BlockSpec((None, wide, wide), lambda b, l: (l, 0, 0)),
            pl.BlockSpec((None, n_lvl, 1, wide // 2), lambda b, l: (l, 0, 0, 0)),
            pl.BlockSpec((None, n_lvl, 1, wide // 2), lambda b, l: (l, 0, 0, 0)),
            pl.BlockSpec((None, 1, wide), lambda b, l: (l, 0, 0)),
        ],
        out_specs=pl.BlockSpec((None, n_sub, wide), lambda b, l: (b, 0, l)),
        out_shape=jax.ShapeDtypeStruct((bsz, n_sub, SSM_LT * wide), F32),
        compiler_params=_params(("parallel", "parallel")),
        name="s5",
    )(ucat, w_loc, t_in, m_st, pw_re, pw_im, dvec)


def _compress_kernel(kc_ref, vc_ref, w1k_ref, w2k_ref, w1v_ref, w2vt_ref, posk_ref, posv_ref, gkc_ref,
                     kcmp_ref, vcmpt_ref, *, nch):
    half = CMP_STRIDE * HEAD_DIM

    def hidden(x_ref, w1_ref, pos_ref):
        x = x_ref[...]
        a = _dot(x, w1_ref[:half, :])
        b = _dot(x, w1_ref[half:, :])
        pv = _dot(pos_ref[...], w1_ref[...])[0:1, :]
        hid = a + pltpu.roll(b, nch - 1, axis=0) + pv
        return _gelu(hid).astype(BF16)

    k = _dot(hidden(kc_ref, w1k_ref, posk_ref), w2k_ref[...])
    ms = jnp.mean(k * k, axis=-1, keepdims=True)
    kcmp_ref[...] = (k * lax.rsqrt(ms + EPS) * gkc_ref[...]).astype(BF16)
    vt = _dot_nt(w2vt_ref[...], hidden(vc_ref, w1v_ref, posv_ref))
    coli = lax.broadcasted_iota(jnp.int32, vt.shape, 1)
    vcmpt_ref[...] = jnp.where(coli < nch - 1, vt, 0.0).astype(BF16)


def _compress(kcf, vcf, w1k, w2k, w1v, w2vt, posk, posv, gkc, *, bsz, nch):
    kern = functools.partial(_compress_kernel, nch=nch)
    wide = CMP_STRIDE * HEAD_DIM
    xspec = pl.BlockSpec((None, None, nch, wide), lambda b, g: (b, g, 0, 0))
    c2 = lambda b, g: (0, 0)
    return pl.pallas_call(
        kern,
        grid=(bsz, N_KV),
        in_specs=[
            xspec, xspec,
            pl.BlockSpec((2 * wide, CMP_HIDDEN), c2), pl.BlockSpec((CMP_HIDDEN, HEAD_DIM), c2),
            pl.BlockSpec((2 * wide, CMP_HIDDEN), c2), pl.BlockSpec((HEAD_DIM, CMP_HIDDEN), c2),
            pl.BlockSpec((8, 2 * wide), c2), pl.BlockSpec((8, 2 * wide), c2),
            pl.BlockSpec((1, HEAD_DIM), c2),
        ],
        out_specs=[pl.BlockSpec((None, None, nch, HEAD_DIM), lambda b, g: (b, g, 0, 0)),
                   pl.BlockSpec((None, None, HEAD_DIM, nch), lambda b, g: (b, g, 0, 0))],
        out_shape=[jax.ShapeDtypeStruct((bsz, N_KV, nch, HEAD_DIM), BF16),
                   jax.ShapeDtypeStruct((bsz, N_KV, HEAD_DIM, nch), BF16)],
        compiler_params=_params(("parallel", "parallel")),
        name="compress",
    )(kcf, vcf, w1k, w2k, w1v, w2vt, posk, posv, gkc)


def _nsa_kernel(qt_ref, kcmp_ref, vcmpt_ref, ksa_ref, kw_ref, vst_ref, vwt_ref, gate_ref, ovlt_ref, o_ref,
                accs_ref, accw_ref, *, tq, tk, nch):
    qi = pl.program_id(2)
    q0 = qi * tq
    rows = HPG * tq
    qt = jnp.concatenate([qt_ref[h] for h in range(HPG)], axis=1)
    tpos = q0 + lax.broadcasted_iota(jnp.int32, (1, rows), 1) % tq

    s = _dot(kcmp_ref[...], qt)
    cend = lax.broadcasted_iota(jnp.int32, (nch, rows), 0) * CMP_STRIDE + (CMP_BLOCK - 1)
    s = jnp.where(cend <= tpos, s, NEG)
    m = jnp.max(s, axis=0, keepdims=True)
    p = jnp.exp2(s - m)
    p = p * (1.0 / jnp.sum(p, axis=0, keepdims=True))
    p = jnp.where(tpos >= CMP_BLOCK - 1, p, 0.0)
    o_cmp = _dot(vcmpt_ref[...], p.astype(BF16))

    psum = p[:, 0:tq]
    for h in range(1, HPG):
        psum = psum + p[:, h * tq:(h + 1) * tq]
    hi = psum.astype(BF16)
    lo = (psum - hi.astype(F32)).astype(BF16)
    ovlt = ovlt_ref[...]
    imp = _dot(ovlt, hi) + _dot(ovlt, lo)
    nb = MAX_SEL_BLOCKS
    blk = lax.broadcasted_iota(jnp.int32, (nb, tq), 0)
    cur = (q0 + lax.broadcasted_iota(jnp.int32, (nb, tq), 1)) // SEL_BLOCK
    forced = (blk == 0) | (blk == cur) | (blk == cur - 1)
    imp = jnp.where(forced, FORCE, jnp.where(blk <= cur, imp, NEG))
    rank = jnp.zeros((nb, tq), F32)
    for i in range(nb):
        ri = imp[i:i + 1, :]
        ahead = (ri > imp) | ((ri == imp) & (blk > i))
        rank = rank + jnp.where(ahead, 1.0, 0.0)
    sel = jnp.where(rank < N_SELECT, 0.0, NEG).astype(BF16)
    qa = jnp.concatenate([qt, jnp.concatenate([sel] * HPG, axis=1)], axis=0)

    cw = rows // NSA_SPLIT
    chunks = [slice(c * cw, (c + 1) * cw) for c in range(NSA_SPLIT)]
    krow = lax.broadcasted_iota(jnp.int32, (tk, cw), 0)

    def online(k_blk, q_all, vt_blk, m, acc_ref, mask_fn):
        m_out = []
        for c, cs in enumerate(chunks):
            s = _dot(k_blk, q_all[:, cs])
            if mask_fn is not None:
                s = jnp.where(mask_fn(tpos[:, cs]), s, NEG)
            m_new = jnp.maximum(m[c], jnp.max(s, axis=0, keepdims=True))
            alpha = jnp.exp2(m[c] - m_new)
            p = jnp.exp2(s - m_new).astype(BF16)
            acc_ref[:, cs] = alpha * acc_ref[:, cs] + _dot(vt_blk, p)
            m_out.append(m_new)
        return tuple(m_out)

    m0 = tuple(jnp.full((1, cw), NEG, F32) for _ in chunks)

    accs_ref[...] = jnp.zeros_like(accs_ref)

    def sel_full(kt, m):
        k0 = pl.multiple_of(kt * tk, tk)
        return online(ksa_ref[pl.ds(k0, tk), :], qa, vst_ref[kt], m, accs_ref, None)

    def sel_diag(kt, m):
        k0 = pl.multiple_of(kt * tk, tk)
        return online(ksa_ref[pl.ds(k0, tk), :], qa, vst_ref[kt], m, accs_ref,
                      lambda tp: k0 + krow <= tp)

    n_full = q0 // tk
    n_kt = (q0 + tq) // tk
    m = lax.fori_loop(0, n_full, sel_full, m0)
    lax.fori_loop(n_full, n_kt, sel_diag, m)

    accw_ref[...] = jnp.zeros_like(accw_ref)

    def win_body(kt, m):
        k0 = pl.multiple_of(kt * tk, tk)
        kpos = k0 + krow
        return online(kw_ref[pl.ds(k0, tk), :], qt, vwt_ref[kt], m, accw_ref,
                      lambda tp: (kpos <= tp) & (kpos > tp - WINDOW))

    kt_lo = jnp.maximum(q0 - WINDOW, 0) // tk
    lax.fori_loop(kt_lo, n_kt, win_body, m0)

    accs = accs_ref[...]
    accw = accw_ref[...]
    o_sel = accs[:HEAD_DIM] * (1.0 / accs[HEAD_DIM:HEAD_DIM + 1])
    o_win = accw[:HEAD_DIM] * (1.0 / accw[HEAD_DIM:HEAD_DIM + 1])
    gt = gate_ref[...]
    for h in range(HPG):
        c = slice(h * tq, (h + 1) * tq)
        o_ref[h * HEAD_DIM:(h + 1) * HEAD_DIM, :] = (
            gt[3 * h:3 * h + 1] * o_cmp[:, c] + gt[3 * h + 1:3 * h + 2] * o_sel[:, c]
            + gt[3 * h + 2:3 * h + 3] * o_win[:, c])


def _nsa(qt, kcmp, vcmpt, ksa, kw, vst, vwt, gate, ovlt, *, bsz, seq, nch):
    tq, tk = NSA_TQ, NSA_TK
    nq = seq // tq
    kern = functools.partial(_nsa_kernel, tq=tq, tk=tk, nch=nch)
    full = lambda b, g, i: (b, g, 0, 0)
    full5 = lambda b, g, i: (b, g, 0, 0, 0)
    qd = HPG * HEAD_DIM
    rows = HPG * tq
    return pl.pallas_call(
        kern,
        grid=(bsz, N_KV, nq),
        in_specs=[
            pl.BlockSpec((None, HPG, HEAD_DIM, tq), lambda b, g, i: (b, g, 0, i)),
            pl.BlockSpec((None, None, nch, HEAD_DIM), full),
            pl.BlockSpec((None, None, HEAD_DIM, nch), full),
            pl.BlockSpec((None, None, seq, 2 * HEAD_DIM), full),
            pl.BlockSpec((None, None, seq, HEAD_DIM), full),
            pl.BlockSpec((None, None, seq // tk, V_ROWS, tk), full5),
            pl.BlockSpec((None, None, seq // tk, V_ROWS, tk), full5),
            pl.BlockSpec((None, None, GATE_ROWS, tq), lambda b, g, i: (b, g, 0, i)),
            pl.BlockSpec((MAX_SEL_BLOCKS, nch), lambda b, g, i: (0, 0)),
        ],
        out_specs=pl.BlockSpec((None, qd, tq), lambda b, g, i: (b, g, i)),
        out_shape=jax.ShapeDtypeStruct((bsz, D_ATT, seq), F32),
        scratch_shapes=[pltpu.VMEM((V_ROWS, rows), F32), pltpu.VMEM((V_ROWS, rows), F32)],
        compiler_params=_params(("parallel", "parallel", "arbitrary")),
        name="nsa",
    )(qt, kcmp, vcmpt, ksa, kw, vst, vwt, gate, ovlt)


def _out_proj_kernel(ys_ref, yat_ref, x_ref, wglu_ref, bglu_ref, gs_ref, ga_ref, wo_ref, g2_ref, wr_ref, br_ref,
                     x2_ref, h2_ref, comb_ref):
    y = _gelu(ys_ref[...])
    y = y * _sigmoid(_dot(y.astype(BF16), wglu_ref[...]) + bglu_ref[...])
    ysn = y * lax.rsqrt(jnp.mean(y * y, axis=-1, keepdims=True) + EPS) * gs_ref[...]
    yat = yat_ref[...]
    yant = yat * lax.rsqrt(jnp.mean(yat * yat, axis=0, keepdims=True) + EPS) * ga_ref[...]
    yan = yant.T
    x2 = x_ref[...] + _dot(ysn.astype(BF16), wo_ref[:D_SSM, :]) + _dot(yan.astype(BF16), wo_ref[D_SSM:, :])
    x2_ref[...] = x2
    h2 = (x2 * lax.rsqrt(jnp.mean(x2 * x2, axis=-1, keepdims=True) + EPS) * g2_ref[...]).astype(BF16)
    h2_ref[...] = h2

    logits = _dot(h2, wr_ref[...]) + br_ref[...]
    lane = lax.broadcasted_iota(jnp.int32, logits.shape, 1).astype(F32)
    far = float(LANES)
    is_g = lane < N_EXP_GROUPS
    glog = jnp.where(is_g, logits, -jnp.inf)
    gmax = jnp.max(glog, axis=1, keepdims=True)
    gsum = jnp.sum(jnp.where(is_g, jnp.exp(logits - gmax), 0.0), axis=1, keepdims=True)
    gsel = jnp.min(jnp.where(glog == gmax, lane, far), axis=1, keepdims=True)
    gprob = 1.0 / gsum
    lo = ROUTER_OFF + EXPERTS_PER_GROUP * gsel
    in_e = (lane >= lo) & (lane < lo + EXPERTS_PER_GROUP)
    emax = jnp.max(jnp.where(in_e, logits, -jnp.inf), axis=1, keepdims=True)
    eexp = jnp.where(in_e, jnp.exp(logits - emax), 0.0)
    eprob = jnp.where(in_e, eexp / jnp.sum(eexp, axis=1, keepdims=True), -1.0)
    v1 = jnp.max(eprob, axis=1, keepdims=True)
    i1 = jnp.min(jnp.where(eprob == v1, lane, far), axis=1, keepdims=True)
    rest = jnp.where(lane == i1, -1.0, eprob)
    v2 = jnp.max(rest, axis=1, keepdims=True)
    i2 = jnp.min(jnp.where(rest == v2, lane, far), axis=1, keepdims=True)
    den = v1 + v2
    comb_ref[...] = (jnp.where(lane == i1, v1 / den * gprob, 0.0)
                     + jnp.where(lane == i2, v2 / den * gprob, 0.0))


def _out_proj(ys, yat, x2d, wglu, bglu, gs, ga, wo, g2, wr, br, *, seq):
    n_tok = x2d.shape[0]
    tm = 512
    nl = seq // tm
    row = lambda i: (i, 0)
    const = lambda i: (0, 0)
    return pl.pallas_call(
        _out_proj_kernel,
        grid=(n_tok // tm,),
        in_specs=[
            pl.BlockSpec((tm, D_SSM), row),
            pl.BlockSpec((None, D_ATT, tm), lambda i: (i // nl, 0, i % nl)),
            pl.BlockSpec((tm, D_MODEL), row),
            pl.BlockSpec((D_SSM, D_SSM), const),
            pl.BlockSpec((1, D_SSM), const),
            pl.BlockSpec((1, D_SSM), const),
            pl.BlockSpec((D_ATT, 1), const),
            pl.BlockSpec((D_SSM + D_ATT, D_MODEL), const),
            pl.BlockSpec((1, D_MODEL), const),
            pl.BlockSpec((D_MODEL, LANES), const),
            pl.BlockSpec((1, LANES), const),
        ],
        out_specs=[
            pl.BlockSpec((tm, D_MODEL), row),
            pl.BlockSpec((tm, D_MODEL), row),
            pl.BlockSpec((tm, LANES), row),
        ],
        out_shape=[
            jax.ShapeDtypeStruct((n_tok, D_MODEL), F32),
            jax.ShapeDtypeStruct((n_tok, D_MODEL), BF16),
            jax.ShapeDtypeStruct((n_tok, LANES), F32),
        ],
        compiler_params=_params(("parallel",)),
        name="out_proj",
    )(ys, yat, x2d, wglu, bglu, gs, ga, wo, g2, wr, br)


def _moe_kernel(h_ref, x2_ref, comb_ref, wg_ref, wu_ref, wd_ref, o_ref):
    e = pl.program_id(1)

    @pl.when(e == 0)
    def _():
        o_ref[...] = x2_ref[...]

    h = h_ref[...]
    gate = _dot(h, wg_ref[...])
    up = _dot(h, wu_ref[...])
    comb = comb_ref[...]
    lane = lax.broadcasted_iota(jnp.int32, comb.shape, 1)
    cw = jnp.sum(jnp.where(lane == e + ROUTER_OFF, comb, 0.0), axis=1, keepdims=True)
    a = gate * _sigmoid(gate) * up * cw
    o_ref[...] += _dot(a.astype(BF16), wd_ref[...])


def _moe(h2, x2, comb, wg, wu, wd):
    n_tok = h2.shape[0]
    tm = 1024
    row = lambda i, e: (i, 0)
    return pl.pallas_call(
        _moe_kernel,
        grid=(n_tok // tm, N_EXPERTS),
        in_specs=[
            pl.BlockSpec((tm, D_MODEL), row),
            pl.BlockSpec((tm, D_MODEL), row),
            pl.BlockSpec((tm, LANES), row),
            pl.BlockSpec((None, D_MODEL, D_EXPERT), lambda i, e: (e, 0, 0)),
            pl.BlockSpec((None, D_MODEL, D_EXPERT), lambda i, e: (e, 0, 0)),
            pl.BlockSpec((None, D_EXPERT, D_MODEL), lambda i, e: (e, 0, 0)),
        ],
        out_specs=pl.BlockSpec((tm, D_MODEL), row),
        out_shape=jax.ShapeDtypeStruct((n_tok, D_MODEL), F32),
        compiler_params=_params(("parallel", "arbitrary")),
        name="moe",
    )(h2, x2, comb, wg, wu, wd)


def _block_diag_ones(n, blk):
    i = jnp.arange(n) // blk
    return (i[:, None] == i[None, :]).astype(BF16)


def _layer(x, norm1_g, w_in, lam_re, lam_im, log_step, b_re, b_im, c_re, c_im, d_skip,
           w_glu, b_glu, g_q, g_kc, g_ks, g_kw, pos_k, pos_v, w_ck1, w_ck2, w_cv1, w_cv2,
           out_g_ssm, out_g_att, w_out, norm2_g, w_grp, b_grp, w_exp, b_exp, w_gate, w_up, w_down):
    bsz, seq, _ = x.shape
    assert seq % 512 == 0 and seq // SEL_BLOCK <= MAX_SEL_BLOCKS
    n_tok = bsz * seq
    x2d = x.reshape(n_tok, D_MODEL)
    q8 = SSM_Q
    n_sub = seq // q8
    nch = seq // CMP_STRIDE

    o_q = D_SSM
    o_kv = D_SSM + D_ATT
    o_gt = o_kv + 6 * D_KV
    kv = lambda i: w_in[:, o_kv + i * D_KV:o_kv + (i + 1) * D_KV]
    wrow = jnp.concatenate([w_in[:, :o_q], kv(0), kv(1), kv(2), kv(4)], axis=1).astype(BF16)
    per_g = HPG * N_BRANCH
    wgt = jnp.zeros((D_MODEL, N_KV * GATE_ROWS), F32)
    for g in range(N_KV):
        wgt = wgt.at[:, g * GATE_ROWS:g * GATE_ROWS + per_g].set(w_in[:, o_gt + g * per_g:o_gt + (g + 1) * per_g])
    wcol = jnp.concatenate([w_in[:, o_q:o_kv], kv(3), kv(5), wgt], axis=1).T.astype(BF16)
    qscale = (HEAD_DIM ** -0.5) * math.log2(math.e)
    gq = (jnp.tile(g_q.astype(F32), N_HEADS) * qscale).reshape(D_ATT, 1)
    gks = jnp.tile(g_ks.astype(F32), N_KV).reshape(1, D_KV)
    gkw = jnp.tile(g_kw.astype(F32), N_KV).reshape(1, D_KV)

    u, qt, kc, vc, ksa, kw, vst, vwt, gate = _in_proj(
        x2d, norm1_g.reshape(1, D_MODEL), wrow, wcol, gq, gks, gkw,
        _block_diag_ones(D_KV, HEAD_DIM), bsz=bsz, seq=seq)

    w_loc, t_in, m_st, pw_re, pw_im, dvec = _s5_weights(
        lam_re, lam_im, log_step, b_re, b_im, c_re, c_im, d_skip, n_sub)
    ucat = (u.reshape(bsz, n_sub, q8, SSM_LT, LANES).transpose(0, 1, 3, 2, 4)
            .reshape(bsz, n_sub, SSM_LT * q8 * LANES))
    ycat = _s5(ucat, w_loc, t_in, m_st, pw_re, pw_im, dvec, bsz=bsz, n_sub=n_sub)
    ys = (ycat.reshape(bsz, n_sub, SSM_LT, q8, LANES).transpose(0, 1, 3, 2, 4)
          .reshape(n_tok, D_SSM))

    wide = CMP_STRIDE * HEAD_DIM
    pad8 = lambda p: jnp.zeros((8, 2 * wide), F32).at[0].set(p.reshape(-1)).astype(BF16)
    kcmp, vcmpt = _compress(
        kc.reshape(bsz, N_KV, nch, wide), vc.reshape(bsz, N_KV, nch, wide),
        w_ck1.astype(BF16), w_ck2.astype(BF16), w_cv1.astype(BF16), w_cv2.T.astype(BF16),
        pad8(pos_k), pad8(pos_v), g_kc.astype(F32).reshape(1, HEAD_DIM), bsz=bsz, nch=nch)
    cstart = jnp.arange(nch) * CMP_STRIDE
    sstart = jnp.arange(MAX_SEL_BLOCKS) * SEL_BLOCK
    ovlt = ((cstart[None, :] < sstart[:, None] + SEL_BLOCK) & (cstart[None, :] + CMP_BLOCK > sstart[:, None])
            & (jnp.arange(MAX_SEL_BLOCKS)[:, None] < seq // SEL_BLOCK)
            & (jnp.arange(nch)[None, :] < nch - 1)).astype(BF16)
    yat = _nsa(qt, kcmp, vcmpt, ksa, kw, vst, vwt, gate, ovlt, bsz=bsz, seq=seq, nch=nch)

    wr = jnp.zeros((D_MODEL, LANES), F32)
    wr = wr.at[:, :N_EXP_GROUPS].set(w_grp).at[:, ROUTER_OFF:ROUTER_OFF + N_EXPERTS].set(w_exp).astype(BF16)
    br = jnp.zeros((1, LANES), F32)
    br = br.at[0, :N_EXP_GROUPS].set(b_grp).at[0, ROUTER_OFF:ROUTER_OFF + N_EXPERTS].set(b_exp)
    x2, h2, comb = _out_proj(
        ys, yat, x2d, w_glu.astype(BF16), b_glu.reshape(1, D_SSM).astype(F32),
        out_g_ssm.reshape(1, D_SSM).astype(F32), out_g_att.reshape(D_ATT, 1).astype(F32),
        w_out.astype(BF16), norm2_g.reshape(1, D_MODEL).astype(F32), wr, br, seq=seq)

    out = _moe(h2, x2, comb, w_gate.astype(BF16), w_up.astype(BF16), w_down.astype(BF16))
    return out.reshape(bsz, seq, D_MODEL)


def kernel(x, norm1_g, w_in, lam_re, lam_im, log_step, b_re, b_im, c_re, c_im, d_skip, w_glu, b_glu, g_q, g_kc, g_ks, g_kw, pos_k, pos_v, w_ck1, w_ck2, w_cv1, w_cv2, out_g_ssm, out_g_att, w_out, norm2_g, w_grp, b_grp, w_exp, b_exp, w_gate, w_up, w_down):
    depth = norm1_g.shape[0]
    for l in range(depth):
        x = _layer(x, norm1_g[l], w_in[l], lam_re[l], lam_im[l], log_step[l], b_re[l], b_im[l], c_re[l],
                   c_im[l], d_skip[l], w_glu[l], b_glu[l], g_q[l], g_kc[l], g_ks[l], g_kw[l], pos_k[l],
                   pos_v[l], w_ck1[l], w_ck2[l], w_cv1[l], w_cv2[l], out_g_ssm[l], out_g_att[l], w_out[l],
                   norm2_g[l], w_grp[l], b_grp[l], w_exp[l], b_exp[l], w_gate[l], w_up[l], w_down[l])
    return x
```

```python
import functools
import math

import jax
import jax.numpy as jnp
from jax import lax
from jax.experimental import pallas as pl
from jax.experimental.pallas import tpu as pltpu

D_MODEL = 1024
D_SSM = 512
SSM_CH = 16
SSM_GROUPS = D_SSM // SSM_CH
SSM_STATE = 64
D_ATT = 512
HEAD_DIM = 64
N_HEADS = D_ATT // HEAD_DIM
N_KV = 2
HPG = N_HEADS // N_KV
D_KV = N_KV * HEAD_DIM
N_BRANCH = 3
CMP_STRIDE = 16
CMP_BLOCK = 2 * CMP_STRIDE
CMP_HIDDEN = 256
SEL_BLOCK = 64
N_SELECT = 16
WINDOW = 512
N_EXP_GROUPS = 4
EXPERTS_PER_GROUP = 8
N_EXPERTS = N_EXP_GROUPS * EXPERTS_PER_GROUP
D_EXPERT = 256
EPS = 1e-6
NEG = -1e30
FORCE = 1e9

LANES = 128
SSM_Q = 8
SSM_LT = D_SSM // LANES
ROUTER_OFF = N_EXP_GROUPS
NSA_TQ = 256
NSA_TK = 256
NSA_SPLIT = 1
V_ROWS = HEAD_DIM + 16
MAX_SEL_BLOCKS = 64
MOE_TM = 256
GATE_ROWS = 16
VMEM_LIMIT = 56 * 1024 * 1024

F32 = jnp.float32
BF16 = jnp.bfloat16


def _dot(a, b):
    return jnp.dot(a, b, preferred_element_type=F32)


def _dot_nt(a, b):
    return lax.dot_general(a, b, (((1,), (1,)), ((), ())), preferred_element_type=F32)


def _split_dot(x, w):
    hi = x.astype(BF16)
    lo = (x - hi.astype(F32)).astype(BF16)
    return _dot(hi, w) + _dot(lo, w)


def _gelu(x):
    c = math.sqrt(2.0 / math.pi)
    return 0.5 * x * (1.0 + jnp.tanh(c * (x + 0.044715 * (x * x * x))))


def _sigmoid(x):
    return 1.0 / (1.0 + jnp.exp(-x))


def _params(sem):
    return pltpu.CompilerParams(dimension_semantics=sem, vmem_limit_bytes=VMEM_LIMIT)


def _in_proj_kernel(x_ref, g1_ref, wrow_ref, wcol_ref, gq_ref, gks_ref, gkw_ref, bd128_ref,
                    u_ref, qt_ref, kc_ref, vc_ref, ksa_ref, kw_ref, vst_ref, vwt_ref, gate_ref, *, tm, nl):
    x = x_ref[...]
    ms = jnp.mean(x * x, axis=-1, keepdims=True)
    hn = (x * lax.rsqrt(ms + EPS) * g1_ref[...]).astype(BF16)

    pr = _dot(hn, wrow_ref[...])
    u_ref[...] = pr[:, :D_SSM]
    kc, vc, ks, kw = [pr[:, D_SSM + i * D_KV:D_SSM + (i + 1) * D_KV] for i in range(4)]
    kss = _split_dot(ks * ks, bd128_ref[...])
    ksn = ks * lax.rsqrt(kss * (1.0 / HEAD_DIM) + EPS) * gks_ref[...]
    kws = _split_dot(kw * kw, bd128_ref[...])
    kwn = kw * lax.rsqrt(kws * (1.0 / HEAD_DIM) + EPS) * gkw_ref[...]
    t0 = (pl.program_id(0) % nl) * tm
    tpos = t0 + lax.broadcasted_iota(jnp.int32, (tm, MAX_SEL_BLOCKS), 0)
    blk = lax.broadcasted_iota(jnp.int32, (tm, MAX_SEL_BLOCKS), 1)
    onehot = jnp.where(tpos // SEL_BLOCK == blk, 1.0, 0.0).astype(BF16)
    for g in range(N_KV):
        sl = slice(g * HEAD_DIM, (g + 1) * HEAD_DIM)
        kc_ref[g] = kc[:, sl].astype(BF16)
        vc_ref[g] = vc[:, sl].astype(BF16)
        ksa_ref[g] = jnp.concatenate([ksn[:, sl].astype(BF16), onehot], axis=1)
        kw_ref[g] = kwn[:, sl].astype(BF16)

    pc = _dot_nt(wcol_ref[...], hn)
    gq = gq_ref[...]
    for h in range(N_HEADS):
        sl = slice(h * HEAD_DIM, (h + 1) * HEAD_DIM)
        qh = pc[sl]
        ss = jnp.sum(qh * qh, axis=0, keepdims=True)
        qt_ref[h] = (qh * lax.rsqrt(ss * (1.0 / HEAD_DIM) + EPS) * gq[sl]).astype(BF16)
    ones_rows = jnp.where(lax.broadcasted_iota(jnp.int32, (V_ROWS - HEAD_DIM, tm), 0) == 0, 1.0, 0.0)
    for g in range(N_KV):
        for o_ref, base in ((vst_ref, D_ATT), (vwt_ref, D_ATT + D_KV)):
            vt = jnp.concatenate([pc[base + g * HEAD_DIM:base + (g + 1) * HEAD_DIM], ones_rows], axis=0)
            vt = vt.astype(BF16)
            for j in range(tm // NSA_TK):
                o_ref[g, j] = vt[:, j * NSA_TK:(j + 1) * NSA_TK]
        gb = D_ATT + 2 * D_KV + g * GATE_ROWS
        gate_ref[g] = _sigmoid(pc[gb:gb + GATE_ROWS])


def _in_proj(x2d, g1, wrow, wcol, gq, gks, gkw, bd128, *, bsz, seq):
    tm = 512
    nl = seq // tm
    n_tok = bsz * seq
    kern = functools.partial(_in_proj_kernel, tm=tm, nl=nl)
    row = lambda i: (i, 0)
    const = lambda i: (0, 0)
    bgl = lambda i: (i // nl, 0, i % nl, 0)
    n_col = wcol.shape[0]
    jt = tm // NSA_TK

    def kvspec(width):
        return pl.BlockSpec((None, N_KV, tm, width), bgl)

    def kvshape(width):
        return jax.ShapeDtypeStruct((bsz, N_KV, seq, width), BF16)

    vt_spec = pl.BlockSpec((None, N_KV, jt, V_ROWS, NSA_TK), lambda i: (i // nl, 0, i % nl, 0, 0))
    vt_shape = jax.ShapeDtypeStruct((bsz, N_KV, seq // NSA_TK, V_ROWS, NSA_TK), BF16)
    return pl.pallas_call(
        kern,
        grid=(n_tok // tm,),
        in_specs=[
            pl.BlockSpec((tm, D_MODEL), row),
            pl.BlockSpec((1, D_MODEL), const),
            pl.BlockSpec((D_MODEL, D_SSM + 4 * D_KV), const),
            pl.BlockSpec((n_col, D_MODEL), const),
            pl.BlockSpec((D_ATT, 1), const),
            pl.BlockSpec((1, D_KV), const),
            pl.BlockSpec((1, D_KV), const),
            pl.BlockSpec((D_KV, D_KV), const),
        ],
        out_specs=[
            pl.BlockSpec((tm, D_SSM), row),
            pl.BlockSpec((None, N_HEADS, HEAD_DIM, tm), lambda i: (i // nl, 0, 0, i % nl)),
            kvspec(HEAD_DIM), kvspec(HEAD_DIM), kvspec(2 * HEAD_DIM), kvspec(HEAD_DIM),
            vt_spec, vt_spec,
            pl.BlockSpec((None, N_KV, GATE_ROWS, tm), lambda i: (i // nl, 0, 0, i % nl)),
        ],
        out_shape=[
            jax.ShapeDtypeStruct((n_tok, D_SSM), F32),
            jax.ShapeDtypeStruct((bsz, N_HEADS, HEAD_DIM, seq), BF16),
            kvshape(HEAD_DIM), kvshape(HEAD_DIM), kvshape(2 * HEAD_DIM), kvshape(HEAD_DIM),
            vt_shape, vt_shape,
            jax.ShapeDtypeStruct((bsz, N_KV, GATE_ROWS, seq), F32),
        ],
        compiler_params=_params(("parallel",)),
        name="in_proj",
    )(x2d, g1, wrow, wcol, gq, gks, gkw, bd128)


def _s5_weights(lam_re, lam_im, log_step, b_re, b_im, c_re, c_im, d_skip, n_sub):
    q = SSM_Q
    lam = lax.complex(lam_re.astype(F32), lam_im.astype(F32))
    step = jnp.exp(log_step.astype(F32))[:, None]
    lam_bar = jnp.exp(lam * step)
    b_bar = ((lam_bar - 1.0) / lam)[..., None] * lax.complex(b_re.astype(F32), b_im.astype(F32))
    c = lax.complex(c_re.astype(F32), c_im.astype(F32))
    pows = [jnp.ones_like(lam_bar)]
    for _ in range(q):
        pows.append(pows[-1] * lam_bar)
    pw = jnp.stack(pows)
    eye = jnp.eye(LANES // SSM_CH, dtype=F32)
    lt, a8 = SSM_LT, LANES // SSM_CH

    kk = jnp.real(jnp.einsum('ghp,kgp,gpi->kghi', c, pw[:q], b_bar))
    lag = jnp.arange(q)[None, :] - jnp.arange(q)[:, None]
    kt = kk[jnp.clip(lag, 0, q - 1)] * (lag >= 0)[:, :, None, None, None].astype(F32)
    kt = kt.reshape(q, q, lt, a8, SSM_CH, SSM_CH)
    t_in = jnp.einsum('sjlahi,ab->lsaijbh', kt, eye).reshape(lt, q * LANES, q * LANES)

    wc = pw[q - 1 - jnp.arange(q)][..., None] * b_bar[None]
    wr = jnp.stack([jnp.real(wc), jnp.imag(wc)]).reshape(2, q, lt, a8, SSM_STATE, SSM_CH)
    w_loc = jnp.einsum('rslapi,ab->lsairbp', wr, eye).reshape(lt, q * LANES, 2 * a8 * SSM_STATE)

    cl = c[None] * pw[1:q + 1][:, :, None, :]
    mr = jnp.stack([jnp.real(cl), -jnp.imag(cl)]).reshape(2, q, lt, a8, SSM_CH, SSM_STATE)
    m_st = jnp.einsum('rjlahp,ab->lrapjbh', mr, eye).reshape(lt, 2 * a8 * SSM_STATE, q * LANES)

    n_lvl = max(1, (n_sub - 1).bit_length())
    lv = [pw[q]]
    for _ in range(n_lvl - 1):
        lv.append(lv[-1] * lv[-1])
    lvs = jnp.stack(lv).reshape(n_lvl, lt, 1, a8 * SSM_STATE)
    pw_re = jnp.real(lvs).transpose(1, 0, 2, 3)
    pw_im = jnp.imag(lvs).transpose(1, 0, 2, 3)
    dvec = jnp.tile(d_skip.astype(F32).reshape(lt, 1, LANES), (1, 1, q))
    return w_loc.astype(BF16), t_in.astype(BF16), m_st.astype(BF16), pw_re, pw_im, dvec


def _s5_kernel(u_ref, w_ref, t_ref, m_ref, pwr_ref, pwi_ref, d_ref, y_ref, *, n_sub, n_lvl):
    half = (LANES // SSM_CH) * SSM_STATE
    u = u_ref[...]
    ub = u.astype(BF16)
    s_loc = _dot(ub, w_ref[...])
    re = s_loc[:, :half]
    im = s_loc[:, half:]
    rowi = lax.broadcasted_iota(jnp.int32, (n_sub, half), 0)
    for k in range(n_lvl):
        d = 1 << k
        ar = pwr_ref[k]
        ai = pwi_ref[k]
        keep = rowi >= d
        sre = jnp.where(keep, pltpu.roll(re, d, axis=0), 0.0)
        sim = jnp.where(keep, pltpu.roll(im, d, axis=0), 0.0)
        re, im = re + (ar * sre - ai * sim), im + (ar * sim + ai * sre)
    keep = rowi >= 1
    xre = jnp.where(keep, pltpu.roll(re, 1, axis=0), 0.0)
    xim = jnp.where(keep, pltpu.roll(im, 1, axis=0), 0.0)
    xst = jnp.concatenate([xre, xim], axis=1).astype(BF16)
    y = _dot(ub, t_ref[...]) + _dot(xst, m_ref[...])
    y_ref[...] = y + d_ref[...] * u


def _s5(ucat, w_loc, t_in, m_st, pw_re, pw_im, dvec, *, bsz, n_sub):
    q = SSM_Q
    n_lvl = pw_re.shape[1]
    kern = functools.partial(_s5_kernel, n_sub=n_sub, n_lvl=n_lvl)
    wide = q * LANES
    return pl.pallas_call(
        kern,
        grid=(bsz, SSM_LT),
        in_specs=[
            pl.BlockSpec((None, n_sub, wide), lambda b, l: (b, 0, l)),
            pl.BlockSpec((None, wide, wide), lambda b, l: (l, 0, 0)),
            pl.BlockSpec((None, wide, wide), lambda b, l: (l, 0, 0)),
            pl.BlockSpec((None, wide, wide), lambda b, l: (l, 0, 0)),
            pl.BlockSpec((None, n_lvl, 1, wide // 2), lambda b, l: (l, 0, 0, 0)),
            pl.BlockSpec((None, n_lvl, 1, wide // 2), lambda b, l: (l, 0, 0, 0)),
            pl.BlockSpec((None, 1, wide), lambda b, l: (l, 0, 0)),
        ],
        out_specs=pl.BlockSpec((None, n_sub, wide), lambda b, l: (b, 0, l)),
        out_shape=jax.ShapeDtypeStruct((bsz, n_sub, SSM_LT * wide), F32),
        compiler_params=_params(("parallel", "parallel")),
        name="s5",
    )(ucat, w_loc, t_in, m_st, pw_re, pw_im, dvec)


def _compress_kernel(kc_ref, vc_ref, w1k_ref, w2k_ref, w1v_ref, w2vt_ref, posk_ref, posv_ref, gkc_ref,
                     kcmp_ref, vcmpt_ref, *, nch):
    half = CMP_STRIDE * HEAD_DIM

    def hidden(x_ref, w1_ref, pos_ref):
        x = x_ref[...]
        a = _dot(x, w1_ref[:half, :])
        b = _dot(x, w1_ref[half:, :])
        pv = _dot(pos_ref[...], w1_ref[...])[0:1, :]
        hid = a + pltpu.roll(b, nch - 1, axis=0) + pv
        return _gelu(hid).astype(BF16)

    k = _dot(hidden(kc_ref, w1k_ref, posk_ref), w2k_ref[...])
    ms = jnp.mean(k * k, axis=-1, keepdims=True)
    kcmp_ref[...] = (k * lax.rsqrt(ms + EPS) * gkc_ref[...]).astype(BF16)
    vt = _dot_nt(w2vt_ref[...], hidden(vc_ref, w1v_ref, posv_ref))
    coli = lax.broadcasted_iota(jnp.int32, vt.shape, 1)
    vcmpt_ref[...] = jnp.where(coli < nch - 1, vt, 0.0).astype(BF16)


def _compress(kcf, vcf, w1k, w2k, w1v, w2vt, posk, posv, gkc, *, bsz, nch):
    kern = functools.partial(_compress_kernel, nch=nch)
    wide = CMP_STRIDE * HEAD_DIM
    xspec = pl.BlockSpec((None, None, nch, wide), lambda b, g: (b, g, 0, 0))
    c2 = lambda b, g: (0, 0)
    return pl.pallas_call(
        kern,
        grid=(bsz, N_KV),
        in_specs=[
            xspec, xspec,
            pl.BlockSpec((2 * wide, CMP_HIDDEN), c2), pl.BlockSpec((CMP_HIDDEN, HEAD_DIM), c2),
            pl.BlockSpec((2 * wide, CMP_HIDDEN), c2), pl.BlockSpec((HEAD_DIM, CMP_HIDDEN), c2),
            pl.BlockSpec((8, 2 * wide), c2), pl.BlockSpec((8, 2 * wide), c2),
            pl.BlockSpec((1, HEAD_DIM), c2),
        ],
        out_specs=[pl.BlockSpec((None, None, nch, HEAD_DIM), lambda b, g: (b, g, 0, 0)),
                   pl.BlockSpec((None, None, HEAD_DIM, nch), lambda b, g: (b, g, 0, 0))],
        out_shape=[jax.ShapeDtypeStruct((bsz, N_KV, nch, HEAD_DIM), BF16),
                   jax.ShapeDtypeStruct((bsz, N_KV, HEAD_DIM, nch), BF16)],
        compiler_params=_params(("parallel", "parallel")),
        name="compress",
    )(kcf, vcf, w1k, w2k, w1v, w2vt, posk, posv, gkc)


def _nsa_kernel(qt_ref, kcmp_ref, vcmpt_ref, ksa_ref, kw_ref, vst_ref, vwt_ref, gate_ref, ovlt_ref, o_ref,
                accs_ref, accw_ref, *, tq, tk, nch):
    qi = pl.program_id(2)
    q0 = qi * tq
    rows = HPG * tq
    qt = jnp.concatenate([qt_ref[h] for h in range(HPG)], axis=1)
    tpos = q0 + lax.broadcasted_iota(jnp.int32, (1, rows), 1) % tq

    s = _dot(kcmp_ref[...], qt)
    cend = lax.broadcasted_iota(jnp.int32, (nch, rows), 0) * CMP_STRIDE + (CMP_BLOCK - 1)
    s = jnp.where(cend <= tpos, s, NEG)
    m = jnp.max(s, axis=0, keepdims=True)
    p = jnp.exp2(s - m)
    p = p * (1.0 / jnp.sum(p, axis=0, keepdims=True))
    p = jnp.where(tpos >= CMP_BLOCK - 1, p, 0.0)
    o_cmp = _dot(vcmpt_ref[...], p.astype(BF16))

    psum = p[:, 0:tq]
    for h in range(1, HPG):
        psum = psum + p[:, h * tq:(h + 1) * tq]
    hi = psum.astype(BF16)
    lo = (psum - hi.astype(F32)).astype(BF16)
    ovlt = ovlt_ref[...]
    imp = _dot(ovlt, hi) + _dot(ovlt, lo)
    nb = MAX_SEL_BLOCKS
    blk = lax.broadcasted_iota(jnp.int32, (nb, tq), 0)
    cur = (q0 + lax.broadcasted_iota(jnp.int32, (nb, tq), 1)) // SEL_BLOCK
    forced = (blk == 0) | (blk == cur) | (blk == cur - 1)
    imp = jnp.where(forced, FORCE, jnp.where(blk <= cur, imp, NEG))
    rank = jnp.zeros((nb, tq), F32)
    for i in range(nb):
        ri = imp[i:i + 1, :]
        ahead = (ri > imp) | ((ri == imp) & (blk > i))
        rank = rank + jnp.where(ahead, 1.0, 0.0)
    sel = jnp.where(rank < N_SELECT, 0.0, NEG).astype(BF16)
    qa = jnp.concatenate([qt, jnp.concatenate([sel] * HPG, axis=1)], axis=0)

    cw = rows // NSA_SPLIT
    chunks = [slice(c * cw, (c + 1) * cw) for c in range(NSA_SPLIT)]
    krow = lax.broadcasted_iota(jnp.int32, (tk, cw), 0)

    def online(k_blk, q_all, vt_blk, m, acc_ref, mask_fn):
        m_out = []
        for c, cs in enumerate(chunks):
            s = _dot(k_blk, q_all[:, cs])
            if mask_fn is not None:
                s = jnp.where(mask_fn(tpos[:, cs]), s, NEG)
            m_new = jnp.maximum(m[c], jnp.max(s, axis=0, keepdims=True))
            alpha = jnp.exp2(m[c] - m_new)
            p = jnp.exp2(s - m_new).astype(BF16)
            acc_ref[:, cs] = alpha * acc_ref[:, cs] + _dot(vt_blk, p)
            m_out.append(m_new)
        return tuple(m_out)

    m0 = tuple(jnp.full((1, cw), NEG, F32) for _ in chunks)

    accs_ref[...] = jnp.zeros_like(accs_ref)

    def sel_full(kt, m):
        k0 = pl.multiple_of(kt * tk, tk)
        return online(ksa_ref[pl.ds(k0, tk), :], qa, vst_ref[kt], m, accs_ref, None)

    def sel_diag(kt, m):
        k0 = pl.multiple_of(kt * tk, tk)
        return online(ksa_ref[pl.ds(k0, tk), :], qa, vst_ref[kt], m, accs_ref,
                      lambda tp: k0 + krow <= tp)

    n_full = q0 // tk
    n_kt = (q0 + tq) // tk
    m = lax.fori_loop(0, n_full, sel_full, m0)
    lax.fori_loop(n_full, n_kt, sel_diag, m)

    accw_ref[...] = jnp.zeros_like(accw_ref)

    def win_body(kt, m):
        k0 = pl.multiple_of(kt * tk, tk)
        kpos = k0 + krow
        return online(kw_ref[pl.ds(k0, tk), :], qt, vwt_ref[kt], m, accw_ref,
                      lambda tp: (kpos <= tp) & (kpos > tp - WINDOW))

    kt_lo = jnp.maximum(q0 - WINDOW, 0) // tk
    lax.fori_loop(kt_lo, n_kt, win_body, m0)

    accs = accs_ref[...]
    accw = accw_ref[...]
    o_sel = accs[:HEAD_DIM] * (1.0 / accs[HEAD_DIM:HEAD_DIM + 1])
    o_win = accw[:HEAD_DIM] * (1.0 / accw[HEAD_DIM:HEAD_DIM + 1])
    gt = gate_ref[...]
    for h in range(HPG):
        c = slice(h * tq, (h + 1) * tq)
        o_ref[h * HEAD_DIM:(h + 1) * HEAD_DIM, :] = (
            gt[3 * h:3 * h + 1] * o_cmp[:, c] + gt[3 * h + 1:3 * h + 2] * o_sel[:, c]
            + gt[3 * h + 2:3 * h + 3] * o_win[:, c])


def _nsa(qt, kcmp, vcmpt, ksa, kw, vst, vwt, gate, ovlt, *, bsz, seq, nch):
    tq, tk = NSA_TQ, NSA_TK
    nq = seq // tq
    kern = functools.partial(_nsa_kernel, tq=tq, tk=tk, nch=nch)
    full = lambda b, g, i: (b, g, 0, 0)
    full5 = lambda b, g, i: (b, g, 0, 0, 0)
    qd = HPG * HEAD_DIM
    rows = HPG * tq
    return pl.pallas_call(
        kern,
        grid=(bsz, N_KV, nq),
        in_specs=[
            pl.BlockSpec((None, HPG, HEAD_DIM, tq), lambda b, g, i: (b, g, 0, i)),
            pl.BlockSpec((None, None, nch, HEAD_DIM), full),
            pl.BlockSpec((None, None, HEAD_DIM, nch), full),
            pl.BlockSpec((None, None, seq, 2 * HEAD_DIM), full),
            pl.BlockSpec((None, None, seq, HEAD_DIM), full),
            pl.BlockSpec((None, None, seq // tk, V_ROWS, tk), full5),
            pl.BlockSpec((None, None, seq // tk, V_ROWS, tk), full5),
            pl.BlockSpec((None, None, GATE_ROWS, tq), lambda b, g, i: (b, g, 0, i)),
            pl.BlockSpec((MAX_SEL_BLOCKS, nch), lambda b, g, i: (0, 0)),
        ],
        out_specs=pl.BlockSpec((None, qd, tq), lambda b, g, i: (b, g, i)),
        out_shape=jax.ShapeDtypeStruct((bsz, D_ATT, seq), F32),
        scratch_shapes=[pltpu.VMEM((V_ROWS, rows), F32), pltpu.VMEM((V_ROWS, rows), F32)],
        compiler_params=_params(("parallel", "parallel", "arbitrary")),
        name="nsa",
    )(qt, kcmp, vcmpt, ksa, kw, vst, vwt, gate, ovlt)


def _out_proj_kernel(ys_ref, yat_ref, x_ref, wglu_ref, bglu_ref, gs_ref, ga_ref, wo_ref, g2_ref, wr_ref, br_ref,
                     x2e_ref):
    y = _gelu(ys_ref[...])
    y = y * _sigmoid(_dot(y.astype(BF16), wglu_ref[...]) + bglu_ref[...])
    ysn = y * lax.rsqrt(jnp.mean(y * y, axis=-1, keepdims=True) + EPS) * gs_ref[...]
    yat = yat_ref[...]
    yant = yat * lax.rsqrt(jnp.mean(yat * yat, axis=0, keepdims=True) + EPS) * ga_ref[...]
    yan = yant.T
    x2 = x_ref[...] + _dot(ysn.astype(BF16), wo_ref[:D_SSM, :]) + _dot(yan.astype(BF16), wo_ref[D_SSM:, :])
    x2e_ref[:, :D_MODEL] = x2
    h2 = (x2 * lax.rsqrt(jnp.mean(x2 * x2, axis=-1, keepdims=True) + EPS) * g2_ref[...]).astype(BF16)

    logits = _dot(h2, wr_ref[...]) + br_ref[...]
    lane = lax.broadcasted_iota(jnp.int32, logits.shape, 1).astype(F32)
    far = float(LANES)
    is_g = lane < N_EXP_GROUPS
    glog = jnp.where(is_g, logits, -jnp.inf)
    gmax = jnp.max(glog, axis=1, keepdims=True)
    gsum = jnp.sum(jnp.where(is_g, jnp.exp(logits - gmax), 0.0), axis=1, keepdims=True)
    gsel = jnp.min(jnp.where(glog == gmax, lane, far), axis=1, keepdims=True)
    gprob = 1.0 / gsum
    lo = ROUTER_OFF + EXPERTS_PER_GROUP * gsel
    in_e = (lane >= lo) & (lane < lo + EXPERTS_PER_GROUP)
    emax = jnp.max(jnp.where(in_e, logits, -jnp.inf), axis=1, keepdims=True)
    eexp = jnp.where(in_e, jnp.exp(logits - emax), 0.0)
    eprob = jnp.where(in_e, eexp / jnp.sum(eexp, axis=1, keepdims=True), -1.0)
    v1 = jnp.max(eprob, axis=1, keepdims=True)
    i1 = jnp.min(jnp.where(eprob == v1, lane, far), axis=1, keepdims=True)
    rest = jnp.where(lane == i1, -1.0, eprob)
    v2 = jnp.max(rest, axis=1, keepdims=True)
    i2 = jnp.min(jnp.where(rest == v2, lane, far), axis=1, keepdims=True)
    den = v1 + v2
    x2e_ref[:, D_MODEL:] = (jnp.where(lane == i1, v1 / den * gprob, 0.0)
                            + jnp.where(lane == i2, v2 / den * gprob, 0.0)
                            + jnp.where(lane == 0.0, gsel, 0.0))


def _out_proj(ys, yat, x2d, wglu, bglu, gs, ga, wo, g2, wr, br, *, seq):
    n_tok = x2d.shape[0]
    tm = 512
    nl = seq // tm
    row = lambda i: (i, 0)
    const = lambda i: (0, 0)
    return pl.pallas_call(
        _out_proj_kernel,
        grid=(n_tok // tm,),
        in_specs=[
            pl.BlockSpec((tm, D_SSM), row),
            pl.BlockSpec((None, D_ATT, tm), lambda i: (i // nl, 0, i % nl)),
            pl.BlockSpec((tm, D_MODEL), row),
            pl.BlockSpec((D_SSM, D_SSM), const),
            pl.BlockSpec((1, D_SSM), const),
            pl.BlockSpec((1, D_SSM), const),
            pl.BlockSpec((D_ATT, 1), const),
            pl.BlockSpec((D_SSM + D_ATT, D_MODEL), const),
            pl.BlockSpec((1, D_MODEL), const),
            pl.BlockSpec((D_MODEL, LANES), const),
            pl.BlockSpec((1, LANES), const),
        ],
        out_specs=pl.BlockSpec((tm, D_MODEL + LANES), row),
        out_shape=jax.ShapeDtypeStruct((n_tok, D_MODEL + LANES), F32),
        compiler_params=_params(("parallel",)),
        name="out_proj",
    )(ys, yat, x2d, wglu, bglu, gs, ga, wo, g2, wr, br)


def _moe_plan(gsel, n_tok):
    tmx = MOE_TM
    n_tiles = n_tok // tmx + N_EXP_GROUPS
    oh = (gsel[:, None] == jnp.arange(N_EXP_GROUPS)[None, :]).astype(jnp.int32)
    csum = jnp.cumsum(oh, axis=0)
    counts = csum[-1]
    rank = jnp.sum(csum * oh, axis=1) - 1
    nt = (counts + tmx - 1) // tmx
    tend = jnp.cumsum(nt)
    toff = tend - nt
    pos = jnp.sum(oh * toff[None, :], axis=1) * tmx + rank
    tok = jnp.zeros((n_tiles * tmx,), jnp.int32).at[pos].set(jnp.arange(n_tok, dtype=jnp.int32))
    tile = jnp.arange(n_tiles)
    grp = jnp.minimum(jnp.sum((tile[:, None] >= tend[None, :]).astype(jnp.int32), axis=1), N_EXP_GROUPS - 1)
    nval = jnp.clip(counts[grp] - (tile - toff[grp]) * tmx, 0, tmx)
    return grp.astype(jnp.int32), nval.astype(jnp.int32), tok


def _moe_kernel(grp_ref, nval_ref, tok_ref, x_hbm, g2_ref, wg_ref, wu_ref, wd_ref, o_hbm,
                xbuf, obuf, abuf, gsem, ssem, *, tmx, n_tiles):
    i = pl.program_id(0)
    slot = i % 2

    def gather_start(tile, dst_slot):
        base = tile * tmx

        def body(r, c):
            t = tok_ref[base + r]
            pltpu.make_async_copy(x_hbm.at[pl.ds(t, 1), :], xbuf.at[dst_slot, pl.ds(r, 1), :],
                                  gsem.at[dst_slot]).start()
            return c

        lax.fori_loop(0, tmx, body, 0, unroll=8)

    def scatter_wait(n, src_slot):
        n8 = pl.multiple_of((n // 8) * 8, 8)

        @pl.when(n8 > 0)
        def _():
            pltpu.make_async_copy(obuf.at[src_slot, pl.ds(0, n8), :], o_hbm.at[pl.ds(0, n8), :],
                                  ssem.at[src_slot]).wait()

        def one(r, c):
            pltpu.make_async_copy(obuf.at[src_slot, pl.ds(0, 1), :], o_hbm.at[pl.ds(0, 1), :],
                                  ssem.at[src_slot]).wait()
            return c

        lax.fori_loop(0, n - n8, one, 0)

    @pl.when(i == 0)
    def _():
        gather_start(0, 0)

    @pl.when(i + 1 < n_tiles)
    def _():
        gather_start(i + 1, 1 - slot)

    pltpu.make_async_copy(x_hbm.at[pl.ds(0, tmx), :], xbuf.at[slot], gsem.at[slot]).wait()

    @pl.when(i >= 2)
    def _():
        scatter_wait(nval_ref[i - 2], slot)

    nv = nval_ref[i]

    @pl.when(nv > 0)
    def _():
        xe = xbuf[slot]
        x2 = xe[:, :D_MODEL]
        cw = xe[:, D_MODEL:]
        h = (x2 * lax.rsqrt(jnp.mean(x2 * x2, axis=-1, keepdims=True) + EPS) * g2_ref[...]).astype(BF16)
        lane = lax.broadcasted_iota(jnp.int32, cw.shape, 1)
        first = ROUTER_OFF + EXPERTS_PER_GROUP * grp_ref[i]
        for k in range(EXPERTS_PER_GROUP):
            gate = _dot(h, wg_ref[k])
            up = _dot(h, wu_ref[k])
            ck = jnp.sum(jnp.where(lane == first + k, cw, 0.0), axis=1, keepdims=True)
            abuf[:, k * D_EXPERT:(k + 1) * D_EXPERT] = (gate * _sigmoid(gate) * up * ck).astype(BF16)
        obuf[slot] = x2 + _dot(abuf[...], wd_ref[...])
        base = i * tmx

        def body(r, c):
            t = tok_ref[base + r]
            pltpu.make_async_copy(obuf.at[slot, pl.ds(r, 1), :], o_hbm.at[pl.ds(t, 1), :], ssem.at[slot]).start()
            return c

        lax.fori_loop(0, nv, body, 0)

    @pl.when(i == n_tiles - 1)
    def _():
        scatter_wait(nval_ref[i - 1], 1 - slot)
        scatter_wait(nv, slot)


def _moe(x2e, grp, nval, tok, g2, wg, wu, wd):
    n_tok = x2e.shape[0]
    tmx = MOE_TM
    n_tiles = grp.shape[0]
    kern = functools.partial(_moe_kernel, tmx=tmx, n_tiles=n_tiles)
    gk = EXPERTS_PER_GROUP * D_EXPERT
    grid_spec = pltpu.PrefetchScalarGridSpec(
        num_scalar_prefetch=3,
        grid=(n_tiles,),
        in_specs=[
            pl.BlockSpec(memory_space=pl.ANY),
            pl.BlockSpec((1, D_MODEL), lambda i, g, n, t: (0, 0)),
            pl.BlockSpec((None, EXPERTS_PER_GROUP, D_MODEL, D_EXPERT), lambda i, g, n, t: (g[i], 0, 0, 0)),
            pl.BlockSpec((None, EXPERTS_PER_GROUP, D_MODEL, D_EXPERT), lambda i, g, n, t: (g[i], 0, 0, 0)),
            pl.BlockSpec((None, gk, D_MODEL), lambda i, g, n, t: (g[i], 0, 0)),
        ],
        out_specs=pl.BlockSpec(memory_space=pl.ANY),
        scratch_shapes=[
            pltpu.VMEM((2, tmx, D_MODEL + LANES), F32),
            pltpu.VMEM((2, tmx, D_MODEL), F32),
            pltpu.VMEM((tmx, gk), BF16),
            pltpu.SemaphoreType.DMA((2,)),
            pltpu.SemaphoreType.DMA((2,)),
        ],
    )
    return pl.pallas_call(
        kern,
        grid_spec=grid_spec,
        out_shape=jax.ShapeDtypeStruct((n_tok, D_MODEL), F32),
        compiler_params=_params(("arbitrary",)),
        name="moe",
    )(grp, nval, tok, x2e, g2, wg, wu, wd)


def _block_diag_ones(n, blk):
    i = jnp.arange(n) // blk
    return (i[:, None] == i[None, :]).astype(BF16)


def _layer(x, norm1_g, w_in, lam_re, lam_im, log_step, b_re, b_im, c_re, c_im, d_skip,
           w_glu, b_glu, g_q, g_kc, g_ks, g_kw, pos_k, pos_v, w_ck1, w_ck2, w_cv1, w_cv2,
           out_g_ssm, out_g_att, w_out, norm2_g, w_grp, b_grp, w_exp, b_exp, w_gate, w_up, w_down):
    bsz, seq, _ = x.shape
    assert seq % 512 == 0 and seq // SEL_BLOCK <= MAX_SEL_BLOCKS
    n_tok = bsz * seq
    x2d = x.reshape(n_tok, D_MODEL)
    q8 = SSM_Q
    n_sub = seq // q8
    nch = seq // CMP_STRIDE

    o_q = D_SSM
    o_kv = D_SSM + D_ATT
    o_gt = o_kv + 6 * D_KV
    kv = lambda i: w_in[:, o_kv + i * D_KV:o_kv + (i + 1) * D_KV]
    wrow = jnp.concatenate([w_in[:, :o_q], kv(0), kv(1), kv(2), kv(4)], axis=1).astype(BF16)
    per_g = HPG * N_BRANCH
    wgt = jnp.zeros((D_MODEL, N_KV * GATE_ROWS), F32)
    for g in range(N_KV):
        wgt = wgt.at[:, g * GATE_ROWS:g * GATE_ROWS + per_g].set(w_in[:, o_gt + g * per_g:o_gt + (g + 1) * per_g])
    wcol = jnp.concatenate([w_in[:, o_q:o_kv], kv(3), kv(5), wgt], axis=1).T.astype(BF16)
    qscale = (HEAD_DIM ** -0.5) * math.log2(math.e)
    gq = (jnp.tile(g_q.astype(F32), N_HEADS) * qscale).reshape(D_ATT, 1)
    gks = jnp.tile(g_ks.astype(F32), N_KV).reshape(1, D_KV)
    gkw = jnp.tile(g_kw.astype(F32), N_KV).reshape(1, D_KV)

    u, qt, kc, vc, ksa, kw, vst, vwt, gate = _in_proj(
        x2d, norm1_g.reshape(1, D_MODEL), wrow, wcol, gq, gks, gkw,
        _block_diag_ones(D_KV, HEAD_DIM), bsz=bsz, seq=seq)

    w_loc, t_in, m_st, pw_re, pw_im, dvec = _s5_weights(
        lam_re, lam_im, log_step, b_re, b_im, c_re, c_im, d_skip, n_sub)
    ucat = (u.reshape(bsz, n_sub, q8, SSM_LT, LANES).transpose(0, 1, 3, 2, 4)
            .reshape(bsz, n_sub, SSM_LT * q8 * LANES))
    ycat = _s5(ucat, w_loc, t_in, m_st, pw_re, pw_im, dvec, bsz=bsz, n_sub=n_sub)
    ys = (ycat.reshape(bsz, n_sub, SSM_LT, q8, LANES).transpose(0, 1, 3, 2, 4)
          .reshape(n_tok, D_SSM))

    wide = CMP_STRIDE * HEAD_DIM
    pad8 = lambda p: jnp.zeros((8, 2 * wide), F32).at[0].set(p.reshape(-1)).astype(BF16)
    kcmp, vcmpt = _compress(
        kc.reshape(bsz, N_KV, nch, wide), vc.reshape(bsz, N_KV, nch, wide),
        w_ck1.astype(BF16), w_ck2.astype(BF16), w_cv1.astype(BF16), w_cv2.T.astype(BF16),
        pad8(pos_k), pad8(pos_v), g_kc.astype(F32).reshape(1, HEAD_DIM), bsz=bsz, nch=nch)
    cstart = jnp.arange(nch) * CMP_STRIDE
    sstart = jnp.arange(MAX_SEL_BLOCKS) * SEL_BLOCK
    ovlt = ((cstart[None, :] < sstart[:, None] + SEL_BLOCK) & (cstart[None, :] + CMP_BLOCK > sstart[:, None])
            & (jnp.arange(MAX_SEL_BLOCKS)[:, None] < seq // SEL_BLOCK)
            & (jnp.arange(nch)[None, :] < nch - 1)).astype(BF16)
    yat = _nsa(qt, kcmp, vcmpt, ksa, kw, vst, vwt, gate, ovlt, bsz=bsz, seq=seq, nch=nch)

    wr = jnp.zeros((D_MODEL, LANES), F32)
    wr = wr.at[:, :N_EXP_GROUPS].set(w_grp).at[:, ROUTER_OFF:ROUTER_OFF + N_EXPERTS].set(w_exp).astype(BF16)
    br = jnp.zeros((1, LANES), F32)
    br = br.at[0, :N_EXP_GROUPS].set(b_grp).at[0, ROUTER_OFF:ROUTER_OFF + N_EXPERTS].set(b_exp)
    g2 = norm2_g.reshape(1, D_MODEL).astype(F32)
    x2e = _out_proj(
        ys, yat, x2d, w_glu.astype(BF16), b_glu.reshape(1, D_SSM).astype(F32),
        out_g_ssm.reshape(1, D_SSM).astype(F32), out_g_att.reshape(D_ATT, 1).astype(F32),
        w_out.astype(BF16), g2, wr, br, seq=seq)

    grp, nval, tok = _moe_plan(x2e[:, D_MODEL].astype(jnp.int32), n_tok)
    gshape = (N_EXP_GROUPS, EXPERTS_PER_GROUP, D_MODEL, D_EXPERT)
    out = _moe(x2e, grp, nval, tok, g2, w_gate.astype(BF16).reshape(gshape), w_up.astype(BF16).reshape(gshape),
               w_down.astype(BF16).reshape(N_EXP_GROUPS, EXPERTS_PER_GROUP * D_EXPERT, D_MODEL))
    return out.reshape(bsz, seq, D_MODEL)


def kernel(x, norm1_g, w_in, lam_re, lam_im, log_step, b_re, b_im, c_re, c_im, d_skip, w_glu, b_glu, g_q, g_kc, g_ks, g_kw, pos_k, pos_v, w_ck1, w_ck2, w_cv1, w_cv2, out_g_ssm, out_g_att, w_out, norm2_g, w_grp, b_grp, w_exp, b_exp, w_gate, w_up, w_down):
    depth = norm1_g.shape[0]
    for l in range(depth):
        x = _layer(x, norm1_g[l], w_in[l], lam_re[l], lam_im[l], log_step[l], b_re[l], b_im[l], c_re[l],
                   c_im[l], d_skip[l], w_glu[l], b_glu[l], g_q[l], g_kc[l], g_ks[l], g_kw[l], pos_k[l],
                   pos_v[l], w_ck1[l], w_ck2[l], w_cv1[l], w_cv2[l], out_g_ssm[l], out_g_att[l], w_out[l],
                   norm2_g[l], w_grp[l], b_grp[l], w_exp[l], b_exp[l], w_gate[l], w_up[l], w_down[l])
    return x
```

```python
import functools
import math

import jax
import jax.numpy as jnp
from jax import lax
from jax.experimental import pallas as pl
from jax.experimental.pallas import tpu as pltpu

D_MODEL = 1024
D_SSM = 512
SSM_CH = 16
SSM_GROUPS = D_SSM // SSM_CH
SSM_STATE = 64
D_ATT = 512
HEAD_DIM = 64
N_HEADS = D_ATT // HEAD_DIM
N_KV = 2
HPG = N_HEADS // N_KV
D_KV = N_KV * HEAD_DIM
N_BRANCH = 3
CMP_STRIDE = 16
CMP_BLOCK = 2 * CMP_STRIDE
CMP_HIDDEN = 256
SEL_BLOCK = 64
N_SELECT = 16
WINDOW = 512
N_EXP_GROUPS = 4
EXPERTS_PER_GROUP = 8
N_EXPERTS = N_EXP_GROUPS * EXPERTS_PER_GROUP
D_EXPERT = 256
EPS = 1e-6
NEG = -1e30
FORCE = 1e9

LANES = 128
SSM_Q = 8
SSM_LT = D_SSM // LANES
ROUTER_OFF = N_EXP_GROUPS
NSA_TQ = 256
NSA_TK = 256
NSA_UNROLL = 4
V_ROWS = HEAD_DIM + 16
MAX_SEL_BLOCKS = 64
MOE_TM = 256
GATE_ROWS = 16
VMEM_LIMIT = 56 * 1024 * 1024

F32 = jnp.float32
BF16 = jnp.bfloat16


def _dot(a, b):
    return jnp.dot(a, b, preferred_element_type=F32)


def _dot_nt(a, b):
    return lax.dot_general(a, b, (((1,), (1,)), ((), ())), preferred_element_type=F32)


def _split_dot(x, w):
    hi = x.astype(BF16)
    lo = (x - hi.astype(F32)).astype(BF16)
    return _dot(hi, w) + _dot(lo, w)


def _gelu(x):
    c = math.sqrt(2.0 / math.pi)
    return 0.5 * x * (1.0 + jnp.tanh(c * (x + 0.044715 * (x * x * x))))


def _sigmoid(x):
    return 1.0 / (1.0 + jnp.exp(-x))


def _params(sem):
    return pltpu.CompilerParams(dimension_semantics=sem, vmem_limit_bytes=VMEM_LIMIT)


def _in_proj_kernel(x_ref, g1_ref, wrow_ref, wcol_ref, gq_ref, gks_ref, gkw_ref, bd128_ref,
                    u_ref, qt_ref, kc_ref, vc_ref, ksa_ref, kw_ref, vst_ref, vwt_ref, gate_ref, *, tm, nl):
    x = x_ref[...]
    ms = jnp.mean(x * x, axis=-1, keepdims=True)
    hn = (x * lax.rsqrt(ms + EPS) * g1_ref[...]).astype(BF16)

    pr = _dot(hn, wrow_ref[...])
    u_ref[...] = pr[:, :D_SSM]
    kc, vc, ks, kw = [pr[:, D_SSM + i * D_KV:D_SSM + (i + 1) * D_KV] for i in range(4)]
    kss = _split_dot(ks * ks, bd128_ref[...])
    ksn = ks * lax.rsqrt(kss * (1.0 / HEAD_DIM) + EPS) * gks_ref[...]
    kws = _split_dot(kw * kw, bd128_ref[...])
    kwn = kw * lax.rsqrt(kws * (1.0 / HEAD_DIM) + EPS) * gkw_ref[...]
    t0 = (pl.program_id(0) % nl) * tm
    tpos = t0 + lax.broadcasted_iota(jnp.int32, (tm, MAX_SEL_BLOCKS), 0)
    blk = lax.broadcasted_iota(jnp.int32, (tm, MAX_SEL_BLOCKS), 1)
    onehot = jnp.where(tpos // SEL_BLOCK == blk, 1.0, 0.0).astype(BF16)
    for g in range(N_KV):
        sl = slice(g * HEAD_DIM, (g + 1) * HEAD_DIM)
        kc_ref[g] = kc[:, sl]
        vc_ref[g] = vc[:, sl]
        ksa_ref[g] = jnp.concatenate([ksn[:, sl].astype(BF16), onehot], axis=1)
        kw_ref[g] = kwn[:, sl].astype(BF16)

    pc = _dot_nt(wcol_ref[...], hn)
    gq = gq_ref[...]
    for h in range(N_HEADS):
        sl = slice(h * HEAD_DIM, (h + 1) * HEAD_DIM)
        qh = pc[sl]
        ss = jnp.sum(qh * qh, axis=0, keepdims=True)
        qt_ref[h] = (qh * lax.rsqrt(ss * (1.0 / HEAD_DIM) + EPS) * gq[sl]).astype(BF16)
    ones_rows = jnp.where(lax.broadcasted_iota(jnp.int32, (V_ROWS - HEAD_DIM, tm), 0) == 0, 1.0, 0.0)
    for g in range(N_KV):
        for o_ref, base in ((vst_ref, D_ATT), (vwt_ref, D_ATT + D_KV)):
            vt = jnp.concatenate([pc[base + g * HEAD_DIM:base + (g + 1) * HEAD_DIM], ones_rows], axis=0)
            vt = vt.astype(BF16)
            for j in range(tm // NSA_TK):
                o_ref[g, j] = vt[:, j * NSA_TK:(j + 1) * NSA_TK]
        gb = D_ATT + 2 * D_KV + g * GATE_ROWS
        gate_ref[g] = _sigmoid(pc[gb:gb + GATE_ROWS])


def _in_proj(x2d, g1, wrow, wcol, gq, gks, gkw, bd128, *, bsz, seq):
    tm = 512
    nl = seq // tm
    n_tok = bsz * seq
    kern = functools.partial(_in_proj_kernel, tm=tm, nl=nl)
    row = lambda i: (i, 0)
    const = lambda i: (0, 0)
    bgl = lambda i: (i // nl, 0, i % nl, 0)
    n_col = wcol.shape[0]
    jt = tm // NSA_TK

    def kvspec(width):
        return pl.BlockSpec((None, N_KV, tm, width), bgl)

    def kvshape(width, dtype=BF16):
        return jax.ShapeDtypeStruct((bsz, N_KV, seq, width), dtype)

    vt_spec = pl.BlockSpec((None, N_KV, jt, V_ROWS, NSA_TK), lambda i: (i // nl, 0, i % nl, 0, 0))
    vt_shape = jax.ShapeDtypeStruct((bsz, N_KV, seq // NSA_TK, V_ROWS, NSA_TK), BF16)
    return pl.pallas_call(
        kern,
        grid=(n_tok // tm,),
        in_specs=[
            pl.BlockSpec((tm, D_MODEL), row),
            pl.BlockSpec((1, D_MODEL), const),
            pl.BlockSpec((D_MODEL, D_SSM + 4 * D_KV), const),
            pl.BlockSpec((n_col, D_MODEL), const),
            pl.BlockSpec((D_ATT, 1), const),
            pl.BlockSpec((1, D_KV), const),
            pl.BlockSpec((1, D_KV), const),
            pl.BlockSpec((D_KV, D_KV), const),
        ],
        out_specs=[
            pl.BlockSpec((tm, D_SSM), row),
            pl.BlockSpec((None, N_HEADS, HEAD_DIM, tm), lambda i: (i // nl, 0, 0, i % nl)),
            kvspec(HEAD_DIM), kvspec(HEAD_DIM), kvspec(2 * HEAD_DIM), kvspec(HEAD_DIM),
            vt_spec, vt_spec,
            pl.BlockSpec((None, N_KV, GATE_ROWS, tm), lambda i: (i // nl, 0, 0, i % nl)),
        ],
        out_shape=[
            jax.ShapeDtypeStruct((n_tok, D_SSM), F32),
            jax.ShapeDtypeStruct((bsz, N_HEADS, HEAD_DIM, seq), BF16),
            kvshape(HEAD_DIM, F32), kvshape(HEAD_DIM, F32), kvshape(2 * HEAD_DIM), kvshape(HEAD_DIM),
            vt_shape, vt_shape,
            jax.ShapeDtypeStruct((bsz, N_KV, GATE_ROWS, seq), F32),
        ],
        compiler_params=_params(("parallel",)),
        name="in_proj",
    )(x2d, g1, wrow, wcol, gq, gks, gkw, bd128)


def _s5_weights(lam_re, lam_im, log_step, b_re, b_im, c_re, c_im, d_skip, n_sub):
    q = SSM_Q
    lam = lax.complex(lam_re.astype(F32), lam_im.astype(F32))
    step = jnp.exp(log_step.astype(F32))[:, None]
    lam_bar = jnp.exp(lam * step)
    b_bar = ((lam_bar - 1.0) / lam)[..., None] * lax.complex(b_re.astype(F32), b_im.astype(F32))
    c = lax.complex(c_re.astype(F32), c_im.astype(F32))
    pows = [jnp.ones_like(lam_bar)]
    for _ in range(q):
        pows.append(pows[-1] * lam_bar)
    pw = jnp.stack(pows)
    eye = jnp.eye(LANES // SSM_CH, dtype=F32)
    lt, a8 = SSM_LT, LANES // SSM_CH

    kk = jnp.real(jnp.einsum('ghp,kgp,gpi->kghi', c, pw[:q], b_bar))
    lag = jnp.arange(q)[None, :] - jnp.arange(q)[:, None]
    kt = kk[jnp.clip(lag, 0, q - 1)] * (lag >= 0)[:, :, None, None, None].astype(F32)
    kt = kt.reshape(q, q, lt, a8, SSM_CH, SSM_CH)
    t_in = jnp.einsum('sjlahi,ab->lsaijbh', kt, eye).reshape(lt, q * LANES, q * LANES)

    wc = pw[q - 1 - jnp.arange(q)][..., None] * b_bar[None]
    wr = jnp.stack([jnp.real(wc), jnp.imag(wc)]).reshape(2, q, lt, a8, SSM_STATE, SSM_CH)
    w_loc = jnp.einsum('rslapi,ab->lsairbp', wr, eye).reshape(lt, q * LANES, 2 * a8 * SSM_STATE)

    cl = c[None] * pw[1:q + 1][:, :, None, :]
    mr = jnp.stack([jnp.real(cl), -jnp.imag(cl)]).reshape(2, q, lt, a8, SSM_CH, SSM_STATE)
    m_st = jnp.einsum('rjlahp,ab->lrapjbh', mr, eye).reshape(lt, 2 * a8 * SSM_STATE, q * LANES)

    n_lvl = max(1, (n_sub - 1).bit_length())
    lv = [pw[q]]
    for _ in range(n_lvl - 1):
        lv.append(lv[-1] * lv[-1])
    lvs = jnp.stack(lv).reshape(n_lvl, lt, 1, a8 * SSM_STATE)
    pw_re = jnp.real(lvs).transpose(1, 0, 2, 3)
    pw_im = jnp.imag(lvs).transpose(1, 0, 2, 3)
    dvec = jnp.tile(d_skip.astype(F32).reshape(lt, 1, LANES), (1, 1, q))
    return w_loc.astype(BF16), t_in.astype(BF16), m_st.astype(BF16), pw_re, pw_im, dvec


def _s5_kernel(u_ref, w_ref, t_ref, m_ref, pwr_ref, pwi_ref, d_ref, y_ref, *, n_sub, n_lvl):
    half = (LANES // SSM_CH) * SSM_STATE
    q = SSM_Q
    u = jnp.concatenate([u_ref[pl.ds(s, n_sub, stride=q), :] for s in range(q)], axis=1)
    ub = u.astype(BF16)
    s_loc = _dot(ub, w_ref[...])
    re = s_loc[:, :half]
    im = s_loc[:, half:]
    rowi = lax.broadcasted_iota(jnp.int32, (n_sub, half), 0)
    for k in range(n_lvl):
        d = 1 << k
        ar = pwr_ref[k]
        ai = pwi_ref[k]
        keep = rowi >= d
        sre = jnp.where(keep, pltpu.roll(re, d, axis=0), 0.0)
        sim = jnp.where(keep, pltpu.roll(im, d, axis=0), 0.0)
        re, im = re + (ar * sre - ai * sim), im + (ar * sim + ai * sre)
    keep = rowi >= 1
    xre = jnp.where(keep, pltpu.roll(re, 1, axis=0), 0.0)
    xim = jnp.where(keep, pltpu.roll(im, 1, axis=0), 0.0)
    xst = jnp.concatenate([xre, xim], axis=1).astype(BF16)
    y = _dot(ub, t_ref[...]) + _dot(xst, m_ref[...]) + d_ref[...] * u
    for j in range(q):
        y_ref[pl.ds(j, n_sub, stride=q), :] = y[:, j * LANES:(j + 1) * LANES]


def _s5(u, w_loc, t_in, m_st, pw_re, pw_im, dvec, *, bsz, n_sub):
    q = SSM_Q
    n_lvl = pw_re.shape[1]
    kern = functools.partial(_s5_kernel, n_sub=n_sub, n_lvl=n_lvl)
    wide = q * LANES
    seq = n_sub * q
    return pl.pallas_call(
        kern,
        grid=(bsz, SSM_LT),
        in_specs=[
            pl.BlockSpec((None, seq, LANES), lambda b, l: (b, 0, l)),
            pl.BlockSpec((None, wide, wide), lambda b, l: (l, 0, 0)),
            pl.BlockSpec((None, wide, wide), lambda b, l: (l, 0, 0)),
            pl.BlockSpec((None, wide, wide), lambda b, l: (l, 0, 0)),
            pl.BlockSpec((None, n_lvl, 1, wide // 2), lambda b, l: (l, 0, 0, 0)),
            pl.BlockSpec((None, n_lvl, 1, wide // 2), lambda b, l: (l, 0, 0, 0)),
            pl.BlockSpec((None, 1, wide), lambda b, l: (l, 0, 0)),
        ],
        out_specs=pl.BlockSpec((None, seq, LANES), lambda b, l: (b, 0, l)),
        out_shape=jax.ShapeDtypeStruct((bsz, seq, D_SSM), F32),
        compiler_params=_params(("parallel", "parallel")),
        name="s5",
    )(u, w_loc, t_in, m_st, pw_re, pw_im, dvec)


def _compress_kernel(kc_ref, vc_ref, w1k_ref, w2k_ref, w1v_ref, w2vt_ref, posk_ref, posv_ref, gkc_ref,
                     kcmp_ref, vcmpt_ref, *, nch):
    half = CMP_STRIDE * HEAD_DIM

    def hidden(x_ref, w1_ref, pos_ref):
        a = jnp.zeros((nch, CMP_HIDDEN), F32)
        b = jnp.zeros((nch, CMP_HIDDEN), F32)
        for j in range(CMP_STRIDE):
            xj = x_ref[pl.ds(j, nch, stride=CMP_STRIDE), :].astype(BF16)
            a = a + _dot(xj, w1_ref[j * HEAD_DIM:(j + 1) * HEAD_DIM, :])
            b = b + _dot(xj, w1_ref[half + j * HEAD_DIM:half + (j + 1) * HEAD_DIM, :])
        pv = _dot(pos_ref[...], w1_ref[...])[0:1, :]
        hid = a + pltpu.roll(b, nch - 1, axis=0) + pv
        return _gelu(hid).astype(BF16)

    k = _dot(hidden(kc_ref, w1k_ref, posk_ref), w2k_ref[...])
    ms = jnp.mean(k * k, axis=-1, keepdims=True)
    kcmp_ref[...] = (k * lax.rsqrt(ms + EPS) * gkc_ref[...]).astype(BF16)
    vt = _dot_nt(w2vt_ref[...], hidden(vc_ref, w1v_ref, posv_ref))
    coli = lax.broadcasted_iota(jnp.int32, vt.shape, 1)
    vcmpt_ref[...] = jnp.where(coli < nch - 1, vt, 0.0).astype(BF16)


def _compress(kcf, vcf, w1k, w2k, w1v, w2vt, posk, posv, gkc, *, bsz, nch):
    kern = functools.partial(_compress_kernel, nch=nch)
    wide = CMP_STRIDE * HEAD_DIM
    xspec = pl.BlockSpec((None, None, nch * CMP_STRIDE, HEAD_DIM), lambda b, g: (b, g, 0, 0))
    c2 = lambda b, g: (0, 0)
    return pl.pallas_call(
        kern,
        grid=(bsz, N_KV),
        in_specs=[
            xspec, xspec,
            pl.BlockSpec((2 * wide, CMP_HIDDEN), c2), pl.BlockSpec((CMP_HIDDEN, HEAD_DIM), c2),
            pl.BlockSpec((2 * wide, CMP_HIDDEN), c2), pl.BlockSpec((HEAD_DIM, CMP_HIDDEN), c2),
            pl.BlockSpec((8, 2 * wide), c2), pl.BlockSpec((8, 2 * wide), c2),
            pl.BlockSpec((1, HEAD_DIM), c2),
        ],
        out_specs=[pl.BlockSpec((None, None, nch, HEAD_DIM), lambda b, g: (b, g, 0, 0)),
                   pl.BlockSpec((None, None, HEAD_DIM, nch), lambda b, g: (b, g, 0, 0))],
        out_shape=[jax.ShapeDtypeStruct((bsz, N_KV, nch, HEAD_DIM), BF16),
                   jax.ShapeDtypeStruct((bsz, N_KV, HEAD_DIM, nch), BF16)],
        compiler_params=_params(("parallel", "parallel")),
        name="compress",
    )(kcf, vcf, w1k, w2k, w1v, w2vt, posk, posv, gkc)


def _nsa_kernel(qt_ref, kcmp_ref, vcmpt_ref, ksa_ref, kw_ref, vst_ref, vwt_ref, gate_ref, ovlt_ref, o_ref,
                accs_ref, accw_ref, *, tq, tk, nch):
    qi = pl.program_id(2)
    q0 = qi * tq
    rows = HPG * tq
    qt = jnp.concatenate([qt_ref[h] for h in range(HPG)], axis=1)
    tpos = q0 + lax.broadcasted_iota(jnp.int32, (1, rows), 1) % tq

    s = _dot(kcmp_ref[...], qt)
    cend = lax.broadcasted_iota(jnp.int32, (nch, rows), 0) * CMP_STRIDE + (CMP_BLOCK - 1)
    s = jnp.where(cend <= tpos, s, NEG)
    m = jnp.max(s, axis=0, keepdims=True)
    p = jnp.exp2(s - m)
    p = p * (1.0 / jnp.sum(p, axis=0, keepdims=True))
    p = jnp.where(tpos >= CMP_BLOCK - 1, p, 0.0)
    o_cmp = _dot(vcmpt_ref[...], p.astype(BF16))

    psum = p[:, 0:tq]
    for h in range(1, HPG):
        psum = psum + p[:, h * tq:(h + 1) * tq]
    hi = psum.astype(BF16)
    lo = (psum - hi.astype(F32)).astype(BF16)
    ovlt = ovlt_ref[...]
    imp = _dot(ovlt, hi) + _dot(ovlt, lo)
    nb = MAX_SEL_BLOCKS
    blk = lax.broadcasted_iota(jnp.int32, (nb, tq), 0)
    cur = (q0 + lax.broadcasted_iota(jnp.int32, (nb, tq), 1)) // SEL_BLOCK
    forced = (blk == 0) | (blk == cur) | (blk == cur - 1)
    imp = jnp.where(forced, FORCE, jnp.where(blk <= cur, imp, NEG))
    sub = 8
    groups = [imp[r:r + sub] for r in range(0, nb, sub)]
    ranks = [jnp.zeros((sub, tq), F32) for _ in groups]
    rowl = lax.broadcasted_iota(jnp.int32, (sub, tq), 0)
    for i in range(nb):
        ri = jnp.broadcast_to(imp[i:i + 1, :], (sub, tq))
        for gi, x in enumerate(groups):
            if i < gi * sub:
                ahead = ri >= x
            elif i >= (gi + 1) * sub:
                ahead = ri > x
            else:
                ahead = (ri > x) | ((ri == x) & (rowl > i - gi * sub))
            ranks[gi] = ranks[gi] + jnp.where(ahead, 1.0, 0.0)
    rank = jnp.concatenate(ranks, axis=0)
    sel = jnp.where(rank < N_SELECT, 0.0, NEG).astype(BF16)
    qa = jnp.concatenate([qt, jnp.concatenate([sel] * HPG, axis=1)], axis=0)

    krow = lax.broadcasted_iota(jnp.int32, (tk, rows), 0)
    m0 = jnp.full((1, rows), NEG, F32)

    def run_tiles(k_ref, q_all, vt_ref, acc_ref, m, tiles):
        scores = []
        for kt, kind in tiles:
            k0 = pl.multiple_of(kt * tk, tk)
            s = _dot(k_ref[pl.ds(k0, tk), :], q_all)
            if kind == 'causal':
                s = jnp.where(k0 + krow <= tpos, s, NEG)
            elif kind == 'band':
                s = jnp.where(k0 + krow > tpos - WINDOW, s, NEG)
            scores.append(s)
        for (kt, _), s in zip(tiles, scores):
            m_new = jnp.maximum(m, jnp.max(s, axis=0, keepdims=True))
            alpha = jnp.exp2(m - m_new)
            p = jnp.exp2(s - m_new).astype(BF16)
            acc_ref[...] = alpha * acc_ref[...] + _dot(vt_ref[kt], p)
            m = m_new
        return m

    accs_ref[...] = jnp.zeros_like(accs_ref)
    nu = NSA_UNROLL

    def sel_multi(j, m):
        return run_tiles(ksa_ref, qa, vst_ref, accs_ref, m, [(nu * j + t, None) for t in range(nu)])

    m_sel = lax.fori_loop(0, qi // nu, sel_multi, m0)
    for r in range(nu):
        @pl.when(qi % nu == r)
        def _(r=r):
            base = qi - r
            run_tiles(ksa_ref, qa, vst_ref, accs_ref, m_sel,
                      [(base + t, None) for t in range(r)] + [(qi, 'causal')])

    accw_ref[...] = jnp.zeros_like(accw_ref)
    n_win = WINDOW // tk

    @pl.when(qi >= n_win)
    def _():
        run_tiles(kw_ref, qt, vwt_ref, accw_ref, m0,
                  [(qi - n_win, 'band')] + [(qi - n_win + t, None) for t in range(1, n_win)] + [(qi, 'causal')])

    for r in range(n_win):
        @pl.when(qi == r)
        def _(r=r):
            run_tiles(kw_ref, qt, vwt_ref, accw_ref, m0, [(t, None) for t in range(r)] + [(r, 'causal')])

    accs = accs_ref[...]
    accw = accw_ref[...]
    o_sel = accs[:HEAD_DIM] * (1.0 / accs[HEAD_DIM:HEAD_DIM + 1])
    o_win = accw[:HEAD_DIM] * (1.0 / accw[HEAD_DIM:HEAD_DIM + 1])
    gt = gate_ref[...]
    for h in range(HPG):
        c = slice(h * tq, (h + 1) * tq)
        o_ref[h * HEAD_DIM:(h + 1) * HEAD_DIM, :] = (
            gt[3 * h:3 * h + 1] * o_cmp[:, c] + gt[3 * h + 1:3 * h + 2] * o_sel[:, c]
            + gt[3 * h + 2:3 * h + 3] * o_win[:, c])


def _nsa(qt, kcmp, vcmpt, ksa, kw, vst, vwt, gate, ovlt, *, bsz, seq, nch):
    tq, tk = NSA_TQ, NSA_TK
    assert tq == tk and WINDOW % tk == 0
    nq = seq // tq
    kern = functools.partial(_nsa_kernel, tq=tq, tk=tk, nch=nch)
    full = lambda b, g, i: (b, g, 0, 0)
    full5 = lambda b, g, i: (b, g, 0, 0, 0)
    qd = HPG * HEAD_DIM
    rows = HPG * tq
    return pl.pallas_call(
        kern,
        grid=(bsz, N_KV, nq),
        in_specs=[
            pl.BlockSpec((None, HPG, HEAD_DIM, tq), lambda b, g, i: (b, g, 0, i)),
            pl.BlockSpec((None, None, nch, HEAD_DIM), full),
            pl.BlockSpec((None, None, HEAD_DIM, nch), full),
            pl.BlockSpec((None, None, seq, 2 * HEAD_DIM), full),
            pl.BlockSpec((None, None, seq, HEAD_DIM), full),
            pl.BlockSpec((None, None, seq // tk, V_ROWS, tk), full5),
            pl.BlockSpec((None, None, seq // tk, V_ROWS, tk), full5),
            pl.BlockSpec((None, None, GATE_ROWS, tq), lambda b, g, i: (b, g, 0, i)),
            pl.BlockSpec((MAX_SEL_BLOCKS, nch), lambda b, g, i: (0, 0)),
        ],
        out_specs=pl.BlockSpec((None, qd, tq), lambda b, g, i: (b, g, i)),
        out_shape=jax.ShapeDtypeStruct((bsz, D_ATT, seq), F32),
        scratch_shapes=[pltpu.VMEM((V_ROWS, rows), F32), pltpu.VMEM((V_ROWS, rows), F32)],
        compiler_params=_params(("parallel", "parallel", "arbitrary")),
        name="nsa",
    )(qt, kcmp, vcmpt, ksa, kw, vst, vwt, gate, ovlt)


def _out_proj_kernel(ys_ref, yat_ref, x_ref, wglu_ref, bglu_ref, gs_ref, ga_ref, wo_ref, g2_ref, wr_ref, br_ref,
                     x2e_ref):
    y = _gelu(ys_ref[...])
    y = y * _sigmoid(_dot(y.astype(BF16), wglu_ref[...]) + bglu_ref[...])
    ysn = y * lax.rsqrt(jnp.mean(y * y, axis=-1, keepdims=True) + EPS) * gs_ref[...]
    yat = yat_ref[...]
    yant = yat * lax.rsqrt(jnp.mean(yat * yat, axis=0, keepdims=True) + EPS) * ga_ref[...]
    yan = yant.T
    x2 = x_ref[...] + _dot(ysn.astype(BF16), wo_ref[:D_SSM, :]) + _dot(yan.astype(BF16), wo_ref[D_SSM:, :])
    x2e_ref[:, :D_MODEL] = x2
    h2 = (x2 * lax.rsqrt(jnp.mean(x2 * x2, axis=-1, keepdims=True) + EPS) * g2_ref[...]).astype(BF16)

    logits = _dot(h2, wr_ref[...]) + br_ref[...]
    lane = lax.broadcasted_iota(jnp.int32, logits.shape, 1).astype(F32)
    far = float(LANES)
    is_g = lane < N_EXP_GROUPS
    glog = jnp.where(is_g, logits, -jnp.inf)
    gmax = jnp.max(glog, axis=1, keepdims=True)
    gsum = jnp.sum(jnp.where(is_g, jnp.exp(logits - gmax), 0.0), axis=1, keepdims=True)
    gsel = jnp.min(jnp.where(glog == gmax, lane, far), axis=1, keepdims=True)
    gprob = 1.0 / gsum
    lo = ROUTER_OFF + EXPERTS_PER_GROUP * gsel
    in_e = (lane >= lo) & (lane < lo + EXPERTS_PER_GROUP)
    emax = jnp.max(jnp.where(in_e, logits, -jnp.inf), axis=1, keepdims=True)
    eexp = jnp.where(in_e, jnp.exp(logits - emax), 0.0)
    eprob = jnp.where(in_e, eexp / jnp.sum(eexp, axis=1, keepdims=True), -1.0)
    v1 = jnp.max(eprob, axis=1, keepdims=True)
    i1 = jnp.min(jnp.where(eprob == v1, lane, far), axis=1, keepdims=True)
    rest = jnp.where(lane == i1, -1.0, eprob)
    v2 = jnp.max(rest, axis=1, keepdims=True)
    i2 = jnp.min(jnp.where(rest == v2, lane, far), axis=1, keepdims=True)
    den = v1 + v2
    x2e_ref[:, D_MODEL:] = (jnp.where(lane == i1, v1 / den * gprob, 0.0)
                            + jnp.where(lane == i2, v2 / den * gprob, 0.0)
                            + jnp.where(lane == 0.0, gsel, 0.0))


def _out_proj(ys, yat, x2d, wglu, bglu, gs, ga, wo, g2, wr, br, *, seq):
    n_tok = x2d.shape[0]
    tm = 512
    nl = seq // tm
    row = lambda i: (i, 0)
    const = lambda i: (0, 0)
    return pl.pallas_call(
        _out_proj_kernel,
        grid=(n_tok // tm,),
        in_specs=[
            pl.BlockSpec((tm, D_SSM), row),
            pl.BlockSpec((None, D_ATT, tm), lambda i: (i // nl, 0, i % nl)),
            pl.BlockSpec((tm, D_MODEL), row),
            pl.BlockSpec((D_SSM, D_SSM), const),
            pl.BlockSpec((1, D_SSM), const),
            pl.BlockSpec((1, D_SSM), const),
            pl.BlockSpec((D_ATT, 1), const),
            pl.BlockSpec((D_SSM + D_ATT, D_MODEL), const),
            pl.BlockSpec((1, D_MODEL), const),
            pl.BlockSpec((D_MODEL, LANES), const),
            pl.BlockSpec((1, LANES), const),
        ],
        out_specs=pl.BlockSpec((tm, D_MODEL + LANES), row),
        out_shape=jax.ShapeDtypeStruct((n_tok, D_MODEL + LANES), F32),
        compiler_params=_params(("parallel",)),
        name="out_proj",
    )(ys, yat, x2d, wglu, bglu, gs, ga, wo, g2, wr, br)


def _moe_plan(gsel, n_tok):
    tmx = MOE_TM
    n_tiles = n_tok // tmx + N_EXP_GROUPS
    oh = (gsel[:, None] == jnp.arange(N_EXP_GROUPS)[None, :]).astype(jnp.int32)
    csum = jnp.cumsum(oh, axis=0)
    counts = csum[-1]
    rank = jnp.sum(csum * oh, axis=1) - 1
    nt = (counts + tmx - 1) // tmx
    tend = jnp.cumsum(nt)
    toff = tend - nt
    pos = jnp.sum(oh * toff[None, :], axis=1) * tmx + rank
    tok = jnp.zeros((n_tiles * tmx,), jnp.int32).at[pos].set(jnp.arange(n_tok, dtype=jnp.int32))
    tile = jnp.arange(n_tiles)
    grp = jnp.minimum(jnp.sum((tile[:, None] >= tend[None, :]).astype(jnp.int32), axis=1), N_EXP_GROUPS - 1)
    nval = jnp.clip(counts[grp] - (tile - toff[grp]) * tmx, 0, tmx)
    return grp.astype(jnp.int32), nval.astype(jnp.int32), tok


def _moe_kernel(grp_ref, nval_ref, tok_ref, x_hbm, g2_ref, wg_ref, wu_ref, wd_ref, o_hbm,
                xbuf, obuf, abuf, gsem, ssem, *, tmx, n_tiles):
    i = pl.program_id(0)
    slot = i % 2

    def gather_start(tile, dst_slot):
        base = tile * tmx

        def body(r, c):
            t = tok_ref[base + r]
            pltpu.make_async_copy(x_hbm.at[pl.ds(t, 1), :], xbuf.at[dst_slot, pl.ds(r, 1), :],
                                  gsem.at[dst_slot]).start()
            return c

        lax.fori_loop(0, tmx, body, 0, unroll=8)

    def scatter_wait(n, src_slot):
        n8 = pl.multiple_of((n // 8) * 8, 8)

        @pl.when(n8 > 0)
        def _():
            pltpu.make_async_copy(obuf.at[src_slot, pl.ds(0, n8), :], o_hbm.at[pl.ds(0, n8), :],
                                  ssem.at[src_slot]).wait()

        def one(r, c):
            pltpu.make_async_copy(obuf.at[src_slot, pl.ds(0, 1), :], o_hbm.at[pl.ds(0, 1), :],
                                  ssem.at[src_slot]).wait()
            return c

        lax.fori_loop(0, n - n8, one, 0)

    @pl.when(i == 0)
    def _():
        gather_start(0, 0)

    @pl.when(i + 1 < n_tiles)
    def _():
        gather_start(i + 1, 1 - slot)

    pltpu.make_async_copy(x_hbm.at[pl.ds(0, tmx), :], xbuf.at[slot], gsem.at[slot]).wait()

    @pl.when(i >= 2)
    def _():
        scatter_wait(nval_ref[i - 2], slot)

    nv = nval_ref[i]

    @pl.when(nv > 0)
    def _():
        xe = xbuf[slot]
        x2 = xe[:, :D_MODEL]
        cw = xe[:, D_MODEL:]
        h = (x2 * lax.rsqrt(jnp.mean(x2 * x2, axis=-1, keepdims=True) + EPS) * g2_ref[...]).astype(BF16)
        lane = lax.broadcasted_iota(jnp.int32, cw.shape, 1)
        first = ROUTER_OFF + EXPERTS_PER_GROUP * grp_ref[i]
        for k in range(EXPERTS_PER_GROUP):
            gate = _dot(h, wg_ref[k])
            up = _dot(h, wu_ref[k])
            ck = jnp.sum(jnp.where(lane == first + k, cw, 0.0), axis=1, keepdims=True)
            abuf[:, k * D_EXPERT:(k + 1) * D_EXPERT] = (gate * _sigmoid(gate) * up * ck).astype(BF16)
        obuf[slot] = x2 + _dot(abuf[...], wd_ref[...])
        base = i * tmx

        def body(r, c):
            t = tok_ref[base + r]
            pltpu.make_async_copy(obuf.at[slot, pl.ds(r, 1), :], o_hbm.at[pl.ds(t, 1), :], ssem.at[slot]).start()
            return c

        lax.fori_loop(0, nv, body, 0)

    @pl.when(i == n_tiles - 1)
    def _():
        scatter_wait(nval_ref[i - 1], 1 - slot)
        scatter_wait(nv, slot)


def _moe(x2e, grp, nval, tok, g2, wg, wu, wd):
    n_tok = x2e.shape[0]
    tmx = MOE_TM
    n_tiles = grp.shape[0]
    kern = functools.partial(_moe_kernel, tmx=tmx, n_tiles=n_tiles)
    gk = EXPERTS_PER_GROUP * D_EXPERT
    grid_spec = pltpu.PrefetchScalarGridSpec(
        num_scalar_prefetch=3,
        grid=(n_tiles,),
        in_specs=[
            pl.BlockSpec(memory_space=pl.ANY),
            pl.BlockSpec((1, D_MODEL), lambda i, g, n, t: (0, 0)),
            pl.BlockSpec((None, EXPERTS_PER_GROUP, D_MODEL, D_EXPERT), lambda i, g, n, t: (g[i], 0, 0, 0)),
            pl.BlockSpec((None, EXPERTS_PER_GROUP, D_MODEL, D_EXPERT), lambda i, g, n, t: (g[i], 0, 0, 0)),
            pl.BlockSpec((None, gk, D_MODEL), lambda i, g, n, t: (g[i], 0, 0)),
        ],
        out_specs=pl.BlockSpec(memory_space=pl.ANY),
        scratch_shapes=[
            pltpu.VMEM((2, tmx, D_MODEL + LANES), F32),
            pltpu.VMEM((2, tmx, D_MODEL), F32),
            pltpu.VMEM((tmx, gk), BF16),
            pltpu.SemaphoreType.DMA((2,)),
            pltpu.SemaphoreType.DMA((2,)),
        ],
    )
    return pl.pallas_call(
        kern,
        grid_spec=grid_spec,
        out_shape=jax.ShapeDtypeStruct((n_tok, D_MODEL), F32),
        compiler_params=_params(("arbitrary",)),
        name="moe",
    )(grp, nval, tok, x2e, g2, wg, wu, wd)


def _block_diag_ones(n, blk):
    i = jnp.arange(n) // blk
    return (i[:, None] == i[None, :]).astype(BF16)


def _layer(x, norm1_g, w_in, lam_re, lam_im, log_step, b_re, b_im, c_re, c_im, d_skip,
           w_glu, b_glu, g_q, g_kc, g_ks, g_kw, pos_k, pos_v, w_ck1, w_ck2, w_cv1, w_cv2,
           out_g_ssm, out_g_att, w_out, norm2_g, w_grp, b_grp, w_exp, b_exp, w_gate, w_up, w_down):
    bsz, seq, _ = x.shape
    assert seq % 512 == 0 and seq // SEL_BLOCK <= MAX_SEL_BLOCKS
    n_tok = bsz * seq
    x2d = x.reshape(n_tok, D_MODEL)
    q8 = SSM_Q
    n_sub = seq // q8
    nch = seq // CMP_STRIDE

    o_q = D_SSM
    o_kv = D_SSM + D_ATT
    o_gt = o_kv + 6 * D_KV
    kv = lambda i: w_in[:, o_kv + i * D_KV:o_kv + (i + 1) * D_KV]
    wrow = jnp.concatenate([w_in[:, :o_q], kv(0), kv(1), kv(2), kv(4)], axis=1).astype(BF16)
    per_g = HPG * N_BRANCH
    wgt = jnp.zeros((D_MODEL, N_KV * GATE_ROWS), F32)
    for g in range(N_KV):
        wgt = wgt.at[:, g * GATE_ROWS:g * GATE_ROWS + per_g].set(w_in[:, o_gt + g * per_g:o_gt + (g + 1) * per_g])
    wcol = jnp.concatenate([w_in[:, o_q:o_kv], kv(3), kv(5), wgt], axis=1).T.astype(BF16)
    qscale = (HEAD_DIM ** -0.5) * math.log2(math.e)
    gq = (jnp.tile(g_q.astype(F32), N_HEADS) * qscale).reshape(D_ATT, 1)
    gks = jnp.tile(g_ks.astype(F32), N_KV).reshape(1, D_KV)
    gkw = jnp.tile(g_kw.astype(F32), N_KV).reshape(1, D_KV)

    u, qt, kc, vc, ksa, kw, vst, vwt, gate = _in_proj(
        x2d, norm1_g.reshape(1, D_MODEL), wrow, wcol, gq, gks, gkw,
        _block_diag_ones(D_KV, HEAD_DIM), bsz=bsz, seq=seq)

    w_loc, t_in, m_st, pw_re, pw_im, dvec = _s5_weights(
        lam_re, lam_im, log_step, b_re, b_im, c_re, c_im, d_skip, n_sub)
    ys = _s5(u.reshape(bsz, seq, D_SSM), w_loc, t_in, m_st, pw_re, pw_im, dvec,
             bsz=bsz, n_sub=n_sub).reshape(n_tok, D_SSM)

    wide = CMP_STRIDE * HEAD_DIM
    pad8 = lambda p: jnp.zeros((8, 2 * wide), F32).at[0].set(p.reshape(-1)).astype(BF16)
    kcmp, vcmpt = _compress(
        kc, vc,
        w_ck1.astype(BF16), w_ck2.astype(BF16), w_cv1.astype(BF16), w_cv2.T.astype(BF16),
        pad8(pos_k), pad8(pos_v), g_kc.astype(F32).reshape(1, HEAD_DIM), bsz=bsz, nch=nch)
    cstart = jnp.arange(nch) * CMP_STRIDE
    sstart = jnp.arange(MAX_SEL_BLOCKS) * SEL_BLOCK
    ovlt = ((cstart[None, :] < sstart[:, None] + SEL_BLOCK) & (cstart[None, :] + CMP_BLOCK > sstart[:, None])
            & (jnp.arange(MAX_SEL_BLOCKS)[:, None] < seq // SEL_BLOCK)
            & (jnp.arange(nch)[None, :] < nch - 1)).astype(BF16)
    yat = _nsa(qt, kcmp, vcmpt, ksa, kw, vst, vwt, gate, ovlt, bsz=bsz, seq=seq, nch=nch)

    wr = jnp.zeros((D_MODEL, LANES), F32)
    wr = wr.at[:, :N_EXP_GROUPS].set(w_grp).at[:, ROUTER_OFF:ROUTER_OFF + N_EXPERTS].set(w_exp).astype(BF16)
    br = jnp.zeros((1, LANES), F32)
    br = br.at[0, :N_EXP_GROUPS].set(b_grp).at[0, ROUTER_OFF:ROUTER_OFF + N_EXPERTS].set(b_exp)
    g2 = norm2_g.reshape(1, D_MODEL).astype(F32)
    x2e = _out_proj(
        ys, yat, x2d, w_glu.astype(BF16), b_glu.reshape(1, D_SSM).astype(F32),
        out_g_ssm.reshape(1, D_SSM).astype(F32), out_g_att.reshape(D_ATT, 1).astype(F32),
        w_out.astype(BF16), g2, wr, br, seq=seq)

    grp, nval, tok = _moe_plan(x2e[:, D_MODEL].astype(jnp.int32), n_tok)
    gshape = (N_EXP_GROUPS, EXPERTS_PER_GROUP, D_MODEL, D_EXPERT)
    out = _moe(x2e, grp, nval, tok, g2, w_gate.astype(BF16).reshape(gshape), w_up.astype(BF16).reshape(gshape),
               w_down.astype(BF16).reshape(N_EXP_GROUPS, EXPERTS_PER_GROUP * D_EXPERT, D_MODEL))
    return out.reshape(bsz, seq, D_MODEL)


def kernel(x, norm1_g, w_in, lam_re, lam_im, log_step, b_re, b_im, c_re, c_im, d_skip, w_glu, b_glu, g_q, g_kc, g_ks, g_kw, pos_k, pos_v, w_ck1, w_ck2, w_cv1, w_cv2, out_g_ssm, out_g_att, w_out, norm2_g, w_grp, b_grp, w_exp, b_exp, w_gate, w_up, w_down):
    depth = norm1_g.shape[0]
    for l in range(depth):
        x = _layer(x, norm1_g[l], w_in[l], lam_re[l], lam_im[l], log_step[l], b_re[l], b_im[l], c_re[l],
                   c_im[l], d_skip[l], w_glu[l], b_glu[l], g_q[l], g_kc[l], g_ks[l], g_kw[l], pos_k[l],
                   pos_v[l], w_ck1[l], w_ck2[l], w_cv1[l], w_cv2[l], out_g_ssm[l], out_g_att[l], w_out[l],
                   norm2_g[l], w_grp[l], b_grp[l], w_exp[l], b_exp[l], w_gate[l], w_up[l], w_down[l])
    return x
```

```python
import functools
import math

import jax
import jax.numpy as jnp
from jax import lax
from jax.experimental import pallas as pl
from jax.experimental.pallas import tpu as pltpu

D_MODEL = 1024
D_SSM = 512
SSM_CH = 16
SSM_GROUPS = D_SSM // SSM_CH
SSM_STATE = 64
D_ATT = 512
HEAD_DIM = 64
N_HEADS = D_ATT // HEAD_DIM
N_KV = 2
HPG = N_HEADS // N_KV
D_KV = N_KV * HEAD_DIM
N_BRANCH = 3
CMP_STRIDE = 16
CMP_BLOCK = 2 * CMP_STRIDE
CMP_HIDDEN = 256
SEL_BLOCK = 64
N_SELECT = 16
WINDOW = 512
N_EXP_GROUPS = 4
EXPERTS_PER_GROUP = 8
N_EXPERTS = N_EXP_GROUPS * EXPERTS_PER_GROUP
D_EXPERT = 256
EPS = 1e-6
NEG = -1e30
FORCE = 1e9

LANES = 128
SSM_Q = 8
SSM_LT = D_SSM // LANES
ROUTER_OFF = N_EXP_GROUPS
NSA_TQ = 256
NSA_TK = 256
NSA_UNROLL = 4
V_ROWS = HEAD_DIM + 16
MAX_SEL_BLOCKS = 64
MOE_TM = 256
GATE_ROWS = 16
VMEM_LIMIT = 56 * 1024 * 1024

F32 = jnp.float32
BF16 = jnp.bfloat16


def _dot(a, b):
    return jnp.dot(a, b, preferred_element_type=F32)


def _dot_nt(a, b):
    return lax.dot_general(a, b, (((1,), (1,)), ((), ())), preferred_element_type=F32)


def _split_dot(x, w):
    hi = x.astype(BF16)
    lo = (x - hi.astype(F32)).astype(BF16)
    return _dot(hi, w) + _dot(lo, w)


def _gelu(x):
    c = math.sqrt(2.0 / math.pi)
    return 0.5 * x * (1.0 + jnp.tanh(c * (x + 0.044715 * (x * x * x))))


def _sigmoid(x):
    return 1.0 / (1.0 + jnp.exp(-x))


def _params(sem):
    return pltpu.CompilerParams(dimension_semantics=sem, vmem_limit_bytes=VMEM_LIMIT)


def _in_proj_kernel(x_ref, g1_ref, wrow_ref, wcol_ref, gq_ref, gks_ref, gkw_ref, bd128_ref,
                    u_ref, qt_ref, kc_ref, vc_ref, ksa_ref, kw_ref, vst_ref, vwt_ref, gate_ref, *, tm, nl):
    x = x_ref[...]
    ms = jnp.mean(x * x, axis=-1, keepdims=True)
    hn = (x * lax.rsqrt(ms + EPS) * g1_ref[...]).astype(BF16)

    pr = _dot(hn, wrow_ref[...])
    u_ref[...] = pr[:, :D_SSM]
    kc, vc, ks, kw = [pr[:, D_SSM + i * D_KV:D_SSM + (i + 1) * D_KV] for i in range(4)]
    kss = _split_dot(ks * ks, bd128_ref[...])
    ksn = ks * lax.rsqrt(kss * (1.0 / HEAD_DIM) + EPS) * gks_ref[...]
    kws = _split_dot(kw * kw, bd128_ref[...])
    kwn = kw * lax.rsqrt(kws * (1.0 / HEAD_DIM) + EPS) * gkw_ref[...]
    t0 = (pl.program_id(0) % nl) * tm
    tpos = t0 + lax.broadcasted_iota(jnp.int32, (tm, MAX_SEL_BLOCKS), 0)
    blk = lax.broadcasted_iota(jnp.int32, (tm, MAX_SEL_BLOCKS), 1)
    onehot = jnp.where(tpos // SEL_BLOCK == blk, 1.0, 0.0).astype(BF16)
    for g in range(N_KV):
        sl = slice(g * HEAD_DIM, (g + 1) * HEAD_DIM)
        kc_ref[g] = kc[:, sl]
        vc_ref[g] = vc[:, sl]
        ksa_ref[g] = jnp.concatenate([ksn[:, sl].astype(BF16), onehot], axis=1)
        kw_ref[g] = kwn[:, sl].astype(BF16)

    pc = _dot_nt(wcol_ref[...], hn)
    gq = gq_ref[...]
    for h in range(N_HEADS):
        sl = slice(h * HEAD_DIM, (h + 1) * HEAD_DIM)
        qh = pc[sl]
        ss = jnp.sum(qh * qh, axis=0, keepdims=True)
        qt_ref[h] = (qh * lax.rsqrt(ss * (1.0 / HEAD_DIM) + EPS) * gq[sl]).astype(BF16)
    ones_rows = jnp.where(lax.broadcasted_iota(jnp.int32, (V_ROWS - HEAD_DIM, tm), 0) == 0, 1.0, 0.0)
    for g in range(N_KV):
        for o_ref, base in ((vst_ref, D_ATT), (vwt_ref, D_ATT + D_KV)):
            vt = jnp.concatenate([pc[base + g * HEAD_DIM:base + (g + 1) * HEAD_DIM], ones_rows], axis=0)
            vt = vt.astype(BF16)
            for j in range(tm // NSA_TK):
                o_ref[g, j] = vt[:, j * NSA_TK:(j + 1) * NSA_TK]
        gb = D_ATT + 2 * D_KV + g * GATE_ROWS
        gate_ref[g] = _sigmoid(pc[gb:gb + GATE_ROWS])


def _in_proj(x2d, g1, wrow, wcol, gq, gks, gkw, bd128, *, bsz, seq):
    tm = 512
    nl = seq // tm
    n_tok = bsz * seq
    kern = functools.partial(_in_proj_kernel, tm=tm, nl=nl)
    row = lambda i: (i, 0)
    const = lambda i: (0, 0)
    bgl = lambda i: (i // nl, 0, i % nl, 0)
    n_col = wcol.shape[0]
    jt = tm // NSA_TK

    def kvspec(width):
        return pl.BlockSpec((None, N_KV, tm, width), bgl)

    def kvshape(width, dtype=BF16):
        return jax.ShapeDtypeStruct((bsz, N_KV, seq, width), dtype)

    vt_spec = pl.BlockSpec((None, N_KV, jt, V_ROWS, NSA_TK), lambda i: (i // nl, 0, i % nl, 0, 0))
    vt_shape = jax.ShapeDtypeStruct((bsz, N_KV, seq // NSA_TK, V_ROWS, NSA_TK), BF16)
    return pl.pallas_call(
        kern,
        grid=(n_tok // tm,),
        in_specs=[
            pl.BlockSpec((tm, D_MODEL), row),
            pl.BlockSpec((1, D_MODEL), const),
            pl.BlockSpec((D_MODEL, D_SSM + 4 * D_KV), const),
            pl.BlockSpec((n_col, D_MODEL), const),
            pl.BlockSpec((D_ATT, 1), const),
            pl.BlockSpec((1, D_KV), const),
            pl.BlockSpec((1, D_KV), const),
            pl.BlockSpec((D_KV, D_KV), const),
        ],
        out_specs=[
            pl.BlockSpec((tm, D_SSM), row),
            pl.BlockSpec((None, N_HEADS, HEAD_DIM, tm), lambda i: (i // nl, 0, 0, i % nl)),
            kvspec(HEAD_DIM), kvspec(HEAD_DIM), kvspec(2 * HEAD_DIM), kvspec(HEAD_DIM),
            vt_spec, vt_spec,
            pl.BlockSpec((None, N_KV, GATE_ROWS, tm), lambda i: (i // nl, 0, 0, i % nl)),
        ],
        out_shape=[
            jax.ShapeDtypeStruct((n_tok, D_SSM), F32),
            jax.ShapeDtypeStruct((bsz, N_HEADS, HEAD_DIM, seq), BF16),
            kvshape(HEAD_DIM, F32), kvshape(HEAD_DIM, F32), kvshape(2 * HEAD_DIM), kvshape(HEAD_DIM),
            vt_shape, vt_shape,
            jax.ShapeDtypeStruct((bsz, N_KV, GATE_ROWS, seq), F32),
        ],
        compiler_params=_params(("parallel",)),
        name="in_proj",
    )(x2d, g1, wrow, wcol, gq, gks, gkw, bd128)


def _s5_weights(lam_re, lam_im, log_step, b_re, b_im, c_re, c_im, d_skip, n_sub):
    q = SSM_Q
    lam = lax.complex(lam_re.astype(F32), lam_im.astype(F32))
    step = jnp.exp(log_step.astype(F32))[:, None]
    lam_bar = jnp.exp(lam * step)
    b_bar = ((lam_bar - 1.0) / lam)[..., None] * lax.complex(b_re.astype(F32), b_im.astype(F32))
    c = lax.complex(c_re.astype(F32), c_im.astype(F32))
    pows = [jnp.ones_like(lam_bar)]
    for _ in range(q):
        pows.append(pows[-1] * lam_bar)
    pw = jnp.stack(pows)
    lt, a8 = SSM_LT, LANES // SSM_CH
    hp = a8 * SSM_STATE
    e_lane = (jnp.arange(a8)[:, None] == jnp.arange(LANES)[None, :] // SSM_CH).astype(F32)
    e_state = (jnp.arange(a8)[:, None] == jnp.arange(hp)[None, :] // SSM_STATE).astype(F32)

    kk = jnp.real(jnp.einsum('ghp,kgp,gpi->kghi', c, pw[:q], b_bar))
    km = kk.reshape(q, lt, a8, SSM_CH, SSM_CH).transpose(1, 4, 0, 2, 3).reshape(lt, SSM_CH, q, LANES)
    lag = jnp.arange(q)[None, :] - jnp.arange(q)[:, None]
    kg = km[:, :, jnp.clip(lag, 0, q - 1), :] * (lag >= 0)[None, None, :, :, None].astype(F32)
    kg = kg.transpose(0, 3, 2, 1, 4)
    tj = kg[:, :, :, None, :, :] * e_lane[None, None, None, :, None, :]
    tj = tj.reshape(lt, q, q * LANES, LANES)
    t_in = jnp.concatenate([tj[:, j] for j in range(q)], axis=2)

    wc = pw[q - 1 - jnp.arange(q)][..., None] * b_bar[None]
    halves = []
    for part in (jnp.real(wc), jnp.imag(wc)):
        wm = part.reshape(q, lt, a8, SSM_STATE, SSM_CH).transpose(1, 0, 4, 2, 3).reshape(lt, q, 1, SSM_CH, hp)
        halves.append((wm * e_state[None, None, :, None, :]).reshape(lt, q * LANES, hp))
    w_loc = jnp.concatenate(halves, axis=2)

    cl = c[None] * pw[1:q + 1][:, :, None, :]
    cols = []
    for j in range(q):
        parts = []
        for part in (jnp.real(cl[j]), -jnp.imag(cl[j])):
            mm = part.reshape(lt, a8, SSM_CH, SSM_STATE).transpose(0, 3, 1, 2).reshape(lt, 1, SSM_STATE, LANES)
            parts.append((mm * e_lane[None, :, None, :]).reshape(lt, hp, LANES))
        cols.append(jnp.concatenate(parts, axis=1))
    m_st = jnp.concatenate(cols, axis=2)

    n_lvl = max(1, (n_sub - 1).bit_length())
    lv = [pw[q]]
    for _ in range(n_lvl - 1):
        lv.append(lv[-1] * lv[-1])
    lvs = jnp.stack(lv).reshape(n_lvl, lt, 1, hp)
    pw_re = jnp.real(lvs).transpose(1, 0, 2, 3)
    pw_im = jnp.imag(lvs).transpose(1, 0, 2, 3)
    dvec = jnp.tile(d_skip.astype(F32).reshape(lt, 1, LANES), (1, 1, q))
    return w_loc.astype(BF16), t_in.astype(BF16), m_st.astype(BF16), pw_re, pw_im, dvec


def _s5_kernel(u_ref, w_ref, t_ref, m_ref, pwr_ref, pwi_ref, d_ref, y_ref, *, n_sub, n_lvl):
    half = (LANES // SSM_CH) * SSM_STATE
    q = SSM_Q
    u = jnp.concatenate([u_ref[pl.ds(s, n_sub, stride=q), :] for s in range(q)], axis=1)
    ub = u.astype(BF16)
    s_loc = _dot(ub, w_ref[...])
    re = s_loc[:, :half]
    im = s_loc[:, half:]
    rowi = lax.broadcasted_iota(jnp.int32, (n_sub, half), 0)
    for k in range(n_lvl):
        d = 1 << k
        ar = pwr_ref[k]
        ai = pwi_ref[k]
        keep = rowi >= d
        sre = jnp.where(keep, pltpu.roll(re, d, axis=0), 0.0)
        sim = jnp.where(keep, pltpu.roll(im, d, axis=0), 0.0)
        re, im = re + (ar * sre - ai * sim), im + (ar * sim + ai * sre)
    keep = rowi >= 1
    xre = jnp.where(keep, pltpu.roll(re, 1, axis=0), 0.0)
    xim = jnp.where(keep, pltpu.roll(im, 1, axis=0), 0.0)
    xst = jnp.concatenate([xre, xim], axis=1).astype(BF16)
    y = _dot(ub, t_ref[...]) + _dot(xst, m_ref[...]) + d_ref[...] * u
    for j in range(q):
        y_ref[pl.ds(j, n_sub, stride=q), :] = y[:, j * LANES:(j + 1) * LANES]


def _s5(u, w_loc, t_in, m_st, pw_re, pw_im, dvec, *, bsz, n_sub):
    q = SSM_Q
    n_lvl = pw_re.shape[1]
    kern = functools.partial(_s5_kernel, n_sub=n_sub, n_lvl=n_lvl)
    wide = q * LANES
    seq = n_sub * q
    return pl.pallas_call(
        kern,
        grid=(bsz, SSM_LT),
        in_specs=[
            pl.BlockSpec((None, seq, LANES), lambda b, l: (b, 0, l)),
            pl.BlockSpec((None, wide, wide), lambda b, l: (l, 0, 0)),
            pl.BlockSpec((None, wide, wide), lambda b, l: (l, 0, 0)),
            pl.BlockSpec((None, wide, wide), lambda b, l: (l, 0, 0)),
            pl.BlockSpec((None, n_lvl, 1, wide // 2), lambda b, l: (l, 0, 0, 0)),
            pl.BlockSpec((None, n_lvl, 1, wide // 2), lambda b, l: (l, 0, 0, 0)),
            pl.BlockSpec((None, 1, wide), lambda b, l: (l, 0, 0)),
        ],
        out_specs=pl.BlockSpec((None, seq, LANES), lambda b, l: (b, 0, l)),
        out_shape=jax.ShapeDtypeStruct((bsz, seq, D_SSM), F32),
        compiler_params=_params(("parallel", "parallel")),
        name="s5",
    )(u, w_loc, t_in, m_st, pw_re, pw_im, dvec)


def _compress_kernel(kc_ref, vc_ref, w1k_ref, w2k_ref, w1v_ref, w2vt_ref, posk_ref, posv_ref, gkc_ref,
                     kcmp_ref, vcmpt_ref, *, nch):
    half = CMP_STRIDE * HEAD_DIM

    def hidden(x_ref, w1_ref, pos_ref):
        a = jnp.zeros((nch, CMP_HIDDEN), F32)
        b = jnp.zeros((nch, CMP_HIDDEN), F32)
        for j in range(CMP_STRIDE):
            xj = x_ref[pl.ds(j, nch, stride=CMP_STRIDE), :].astype(BF16)
            a = a + _dot(xj, w1_ref[j * HEAD_DIM:(j + 1) * HEAD_DIM, :])
            b = b + _dot(xj, w1_ref[half + j * HEAD_DIM:half + (j + 1) * HEAD_DIM, :])
        pv = _dot(pos_ref[...], w1_ref[...])[0:1, :]
        hid = a + pltpu.roll(b, nch - 1, axis=0) + pv
        return _gelu(hid).astype(BF16)

    k = _dot(hidden(kc_ref, w1k_ref, posk_ref), w2k_ref[...])
    ms = jnp.mean(k * k, axis=-1, keepdims=True)
    kcmp_ref[...] = (k * lax.rsqrt(ms + EPS) * gkc_ref[...]).astype(BF16)
    vt = _dot_nt(w2vt_ref[...], hidden(vc_ref, w1v_ref, posv_ref))
    coli = lax.broadcasted_iota(jnp.int32, vt.shape, 1)
    vcmpt_ref[...] = jnp.where(coli < nch - 1, vt, 0.0).astype(BF16)


def _compress(kcf, vcf, w1k, w2k, w1v, w2vt, posk, posv, gkc, *, bsz, nch):
    kern = functools.partial(_compress_kernel, nch=nch)
    wide = CMP_STRIDE * HEAD_DIM
    xspec = pl.BlockSpec((None, None, nch * CMP_STRIDE, HEAD_DIM), lambda b, g: (b, g, 0, 0))
    c2 = lambda b, g: (0, 0)
    return pl.pallas_call(
        kern,
        grid=(bsz, N_KV),
        in_specs=[
            xspec, xspec,
            pl.BlockSpec((2 * wide, CMP_HIDDEN), c2), pl.BlockSpec((CMP_HIDDEN, HEAD_DIM), c2),
            pl.BlockSpec((2 * wide, CMP_HIDDEN), c2), pl.BlockSpec((HEAD_DIM, CMP_HIDDEN), c2),
            pl.BlockSpec((8, 2 * wide), c2), pl.BlockSpec((8, 2 * wide), c2),
            pl.BlockSpec((1, HEAD_DIM), c2),
        ],
        out_specs=[pl.BlockSpec((None, None, nch, HEAD_DIM), lambda b, g: (b, g, 0, 0)),
                   pl.BlockSpec((None, None, HEAD_DIM, nch), lambda b, g: (b, g, 0, 0))],
        out_shape=[jax.ShapeDtypeStruct((bsz, N_KV, nch, HEAD_DIM), BF16),
                   jax.ShapeDtypeStruct((bsz, N_KV, HEAD_DIM, nch), BF16)],
        compiler_params=_params(("parallel", "parallel")),
        name="compress",
    )(kcf, vcf, w1k, w2k, w1v, w2vt, posk, posv, gkc)


def _nsa_kernel(qt_ref, kcmp_ref, vcmpt_ref, ksa_ref, kw_ref, vst_ref, vwt_ref, gate_ref, ovlt_ref, o_ref,
                accs_ref, accw_ref, *, tq, tk, nch):
    qi = pl.program_id(2)
    q0 = qi * tq
    rows = HPG * tq
    qt = jnp.concatenate([qt_ref[h] for h in range(HPG)], axis=1)
    tpos = q0 + lax.broadcasted_iota(jnp.int32, (1, rows), 1) % tq

    s = _dot(kcmp_ref[...], qt)
    cend = lax.broadcasted_iota(jnp.int32, (nch, rows), 0) * CMP_STRIDE + (CMP_BLOCK - 1)
    s = jnp.where(cend <= tpos, s, NEG)
    m = jnp.max(s, axis=0, keepdims=True)
    p = jnp.exp2(s - m)
    p = p * (1.0 / jnp.sum(p, axis=0, keepdims=True))
    p = jnp.where(tpos >= CMP_BLOCK - 1, p, 0.0)
    o_cmp = _dot(vcmpt_ref[...], p.astype(BF16))

    psum = p[:, 0:tq]
    for h in range(1, HPG):
        psum = psum + p[:, h * tq:(h + 1) * tq]
    hi = psum.astype(BF16)
    lo = (psum - hi.astype(F32)).astype(BF16)
    ovlt = ovlt_ref[...]
    imp = _dot(ovlt, hi) + _dot(ovlt, lo)
    nb = MAX_SEL_BLOCKS
    blk = lax.broadcasted_iota(jnp.int32, (nb, tq), 0)
    cur = (q0 + lax.broadcasted_iota(jnp.int32, (nb, tq), 1)) // SEL_BLOCK
    forced = (blk == 0) | (blk == cur) | (blk == cur - 1)
    imp = jnp.where(forced, FORCE, jnp.where(blk <= cur, imp, NEG))
    sub = 8
    groups = [imp[r:r + sub] for r in range(0, nb, sub)]
    ranks = [jnp.zeros((sub, tq), F32) for _ in groups]
    rowl = lax.broadcasted_iota(jnp.int32, (sub, tq), 0)
    for i in range(nb):
        ri = jnp.broadcast_to(imp[i:i + 1, :], (sub, tq))
        for gi, x in enumerate(groups):
            if i < gi * sub:
                ahead = ri >= x
            elif i >= (gi + 1) * sub:
                ahead = ri > x
            else:
                ahead = (ri > x) | ((ri == x) & (rowl > i - gi * sub))
            ranks[gi] = ranks[gi] + jnp.where(ahead, 1.0, 0.0)
    rank = jnp.concatenate(ranks, axis=0)
    sel = jnp.where(rank < N_SELECT, 0.0, NEG).astype(BF16)
    qa = jnp.concatenate([qt, jnp.concatenate([sel] * HPG, axis=1)], axis=0)

    krow = lax.broadcasted_iota(jnp.int32, (tk, rows), 0)
    m0 = jnp.full((1, rows), NEG, F32)

    def run_tiles(k_ref, q_all, vt_ref, acc_ref, m, tiles):
        scores = []
        for kt, kind in tiles:
            k0 = pl.multiple_of(kt * tk, tk)
            s = _dot(k_ref[pl.ds(k0, tk), :], q_all)
            if kind == 'causal':
                s = jnp.where(k0 + krow <= tpos, s, NEG)
            elif kind == 'band':
                s = jnp.where(k0 + krow > tpos - WINDOW, s, NEG)
            scores.append(s)
        for (kt, _), s in zip(tiles, scores):
            m_new = jnp.maximum(m, jnp.max(s, axis=0, keepdims=True))
            alpha = jnp.exp2(m - m_new)
            p = jnp.exp2(s - m_new).astype(BF16)
            acc_ref[...] = alpha * acc_ref[...] + _dot(vt_ref[kt], p)
            m = m_new
        return m

    accs_ref[...] = jnp.zeros_like(accs_ref)
    nu = NSA_UNROLL

    def sel_multi(j, m):
        return run_tiles(ksa_ref, qa, vst_ref, accs_ref, m, [(nu * j + t, None) for t in range(nu)])

    m_sel = lax.fori_loop(0, qi // nu, sel_multi, m0)
    for r in range(nu):
        @pl.when(qi % nu == r)
        def _(r=r):
            base = qi - r
            run_tiles(ksa_ref, qa, vst_ref, accs_ref, m_sel,
                      [(base + t, None) for t in range(r)] + [(qi, 'causal')])

    accw_ref[...] = jnp.zeros_like(accw_ref)
    n_win = WINDOW // tk

    @pl.when(qi >= n_win)
    def _():
        run_tiles(kw_ref, qt, vwt_ref, accw_ref, m0,
                  [(qi - n_win, 'band')] + [(qi - n_win + t, None) for t in range(1, n_win)] + [(qi, 'causal')])

    for r in range(n_win):
        @pl.when(qi == r)
        def _(r=r):
            run_tiles(kw_ref, qt, vwt_ref, accw_ref, m0, [(t, None) for t in range(r)] + [(r, 'causal')])

    accs = accs_ref[...]
    accw = accw_ref[...]
    o_sel = accs[:HEAD_DIM] * (1.0 / accs[HEAD_DIM:HEAD_DIM + 1])
    o_win = accw[:HEAD_DIM] * (1.0 / accw[HEAD_DIM:HEAD_DIM + 1])
    gt = gate_ref[...]
    for h in range(HPG):
        c = slice(h * tq, (h + 1) * tq)
        o_ref[h * HEAD_DIM:(h + 1) * HEAD_DIM, :] = (
            gt[3 * h:3 * h + 1] * o_cmp[:, c] + gt[3 * h + 1:3 * h + 2] * o_sel[:, c]
            + gt[3 * h + 2:3 * h + 3] * o_win[:, c])


def _nsa(qt, kcmp, vcmpt, ksa, kw, vst, vwt, gate, ovlt, *, bsz, seq, nch):
    tq, tk = NSA_TQ, NSA_TK
    assert tq == tk and WINDOW % tk == 0
    nq = seq // tq
    kern = functools.partial(_nsa_kernel, tq=tq, tk=tk, nch=nch)
    full = lambda b, g, i: (b, g, 0, 0)
    full5 = lambda b, g, i: (b, g, 0, 0, 0)
    qd = HPG * HEAD_DIM
    rows = HPG * tq
    return pl.pallas_call(
        kern,
        grid=(bsz, N_KV, nq),
        in_specs=[
            pl.BlockSpec((None, HPG, HEAD_DIM, tq), lambda b, g, i: (b, g, 0, i)),
            pl.BlockSpec((None, None, nch, HEAD_DIM), full),
            pl.BlockSpec((None, None, HEAD_DIM, nch), full),
            pl.BlockSpec((None, None, seq, 2 * HEAD_DIM), full),
            pl.BlockSpec((None, None, seq, HEAD_DIM), full),
            pl.BlockSpec((None, None, seq // tk, V_ROWS, tk), full5),
            pl.BlockSpec((None, None, seq // tk, V_ROWS, tk), full5),
            pl.BlockSpec((None, None, GATE_ROWS, tq), lambda b, g, i: (b, g, 0, i)),
            pl.BlockSpec((MAX_SEL_BLOCKS, nch), lambda b, g, i: (0, 0)),
        ],
        out_specs=pl.BlockSpec((None, qd, tq), lambda b, g, i: (b, g, i)),
        out_shape=jax.ShapeDtypeStruct((bsz, D_ATT, seq), F32),
        scratch_shapes=[pltpu.VMEM((V_ROWS, rows), F32), pltpu.VMEM((V_ROWS, rows), F32)],
        compiler_params=_params(("parallel", "parallel", "arbitrary")),
        name="nsa",
    )(qt, kcmp, vcmpt, ksa, kw, vst, vwt, gate, ovlt)


def _out_proj_kernel(ys_ref, yat_ref, x_ref, wglu_ref, bglu_ref, gs_ref, ga_ref, wo_ref, g2_ref, wr_ref, br_ref,
                     x2e_ref):
    y = _gelu(ys_ref[...])
    y = y * _sigmoid(_dot(y.astype(BF16), wglu_ref[...]) + bglu_ref[...])
    ysn = y * lax.rsqrt(jnp.mean(y * y, axis=-1, keepdims=True) + EPS) * gs_ref[...]
    yat = yat_ref[...]
    yant = yat * lax.rsqrt(jnp.mean(yat * yat, axis=0, keepdims=True) + EPS) * ga_ref[...]
    yan = yant.T
    x2 = x_ref[...] + _dot(ysn.astype(BF16), wo_ref[:D_SSM, :]) + _dot(yan.astype(BF16), wo_ref[D_SSM:, :])
    x2e_ref[:, :D_MODEL] = x2
    h2 = (x2 * lax.rsqrt(jnp.mean(x2 * x2, axis=-1, keepdims=True) + EPS) * g2_ref[...]).astype(BF16)

    logits = _dot(h2, wr_ref[...]) + br_ref[...]
    lane = lax.broadcasted_iota(jnp.int32, logits.shape, 1).astype(F32)
    far = float(LANES)
    is_g = lane < N_EXP_GROUPS
    glog = jnp.where(is_g, logits, -jnp.inf)
    gmax = jnp.max(glog, axis=1, keepdims=True)
    gsum = jnp.sum(jnp.where(is_g, jnp.exp(logits - gmax), 0.0), axis=1, keepdims=True)
    gsel = jnp.min(jnp.where(glog == gmax, lane, far), axis=1, keepdims=True)
    gprob = 1.0 / gsum
    lo = ROUTER_OFF + EXPERTS_PER_GROUP * gsel
    in_e = (lane >= lo) & (lane < lo + EXPERTS_PER_GROUP)
    emax = jnp.max(jnp.where(in_e, logits, -jnp.inf), axis=1, keepdims=True)
    eexp = jnp.where(in_e, jnp.exp(logits - emax), 0.0)
    eprob = jnp.where(in_e, eexp / jnp.sum(eexp, axis=1, keepdims=True), -1.0)
    v1 = jnp.max(eprob, axis=1, keepdims=True)
    i1 = jnp.min(jnp.where(eprob == v1, lane, far), axis=1, keepdims=True)
    rest = jnp.where(lane == i1, -1.0, eprob)
    v2 = jnp.max(rest, axis=1, keepdims=True)
    i2 = jnp.min(jnp.where(rest == v2, lane, far), axis=1, keepdims=True)
    den = v1 + v2
    x2e_ref[:, D_MODEL:] = (jnp.where(lane == i1, v1 / den * gprob, 0.0)
                            + jnp.where(lane == i2, v2 / den * gprob, 0.0)
                            + jnp.where(lane == 0.0, gsel, 0.0))


def _out_proj(ys, yat, x2d, wglu, bglu, gs, ga, wo, g2, wr, br, *, seq):
    n_tok = x2d.shape[0]
    tm = 512
    nl = seq // tm
    row = lambda i: (i, 0)
    const = lambda i: (0, 0)
    return pl.pallas_call(
        _out_proj_kernel,
        grid=(n_tok // tm,),
        in_specs=[
            pl.BlockSpec((tm, D_SSM), row),
            pl.BlockSpec((None, D_ATT, tm), lambda i: (i // nl, 0, i % nl)),
            pl.BlockSpec((tm, D_MODEL), row),
            pl.BlockSpec((D_SSM, D_SSM), const),
            pl.BlockSpec((1, D_SSM), const),
            pl.BlockSpec((1, D_SSM), const),
            pl.BlockSpec((D_ATT, 1), const),
            pl.BlockSpec((D_SSM + D_ATT, D_MODEL), const),
            pl.BlockSpec((1, D_MODEL), const),
            pl.BlockSpec((D_MODEL, LANES), const),
            pl.BlockSpec((1, LANES), const),
        ],
        out_specs=pl.BlockSpec((tm, D_MODEL + LANES), row),
        out_shape=jax.ShapeDtypeStruct((n_tok, D_MODEL + LANES), F32),
        compiler_params=_params(("parallel",)),
        name="out_proj",
    )(ys, yat, x2d, wglu, bglu, gs, ga, wo, g2, wr, br)


def _moe_plan(gsel, n_tok):
    tmx = MOE_TM
    n_tiles = n_tok // tmx + N_EXP_GROUPS
    oh = (gsel[:, None] == jnp.arange(N_EXP_GROUPS)[None, :]).astype(jnp.int32)
    csum = jnp.cumsum(oh, axis=0)
    counts = csum[-1]
    rank = jnp.sum(csum * oh, axis=1) - 1
    nt = (counts + tmx - 1) // tmx
    tend = jnp.cumsum(nt)
    toff = tend - nt
    pos = jnp.sum(oh * toff[None, :], axis=1) * tmx + rank
    tok = jnp.zeros((n_tiles * tmx,), jnp.int32).at[pos].set(jnp.arange(n_tok, dtype=jnp.int32))
    tile = jnp.arange(n_tiles)
    grp = jnp.minimum(jnp.sum((tile[:, None] >= tend[None, :]).astype(jnp.int32), axis=1), N_EXP_GROUPS - 1)
    nval = jnp.clip(counts[grp] - (tile - toff[grp]) * tmx, 0, tmx)
    return grp.astype(jnp.int32), nval.astype(jnp.int32), tok


def _moe_kernel(grp_ref, nval_ref, tok_ref, x_hbm, g2_ref, wg_ref, wu_ref, wd_ref, o_hbm,
                xbuf, obuf, abuf, gsem, ssem, *, tmx, n_tiles):
    i = pl.program_id(0)
    slot = i % 2

    def gather_row(tile, r, dst_slot):
        t = tok_ref[tile * tmx + r]
        return pltpu.make_async_copy(x_hbm.at[pl.ds(t, 1), :], xbuf.at[dst_slot, pl.ds(r, 1), :], gsem.at[dst_slot])

    def scatter_row(r):
        t = tok_ref[i * tmx + r]
        return pltpu.make_async_copy(obuf.at[slot, pl.ds(r, 1), :], o_hbm.at[pl.ds(t, 1), :], ssem.at[slot])

    def gather_wait(src_slot):
        pltpu.make_async_copy(x_hbm.at[pl.ds(0, tmx), :], xbuf.at[src_slot], gsem.at[src_slot]).wait()

    def scatter_wait(n, src_slot):
        n8 = pl.multiple_of((n // 8) * 8, 8)

        @pl.when(n8 > 0)
        def _():
            pltpu.make_async_copy(obuf.at[src_slot, pl.ds(0, n8), :], o_hbm.at[pl.ds(0, n8), :],
                                  ssem.at[src_slot]).wait()

        def one(r, c):
            pltpu.make_async_copy(obuf.at[src_slot, pl.ds(0, 1), :], o_hbm.at[pl.ds(0, 1), :],
                                  ssem.at[src_slot]).wait()
            return c

        lax.fori_loop(0, n - n8, one, 0)

    @pl.when(i == 0)
    def _():
        def body(r, c):
            gather_row(0, r, 0).start()
            return c

        lax.fori_loop(0, tmx, body, 0)

    gather_wait(slot)

    @pl.when(i >= 2)
    def _():
        scatter_wait(nval_ref[i - 2], slot)

    nxt = jnp.minimum(i + 1, n_tiles - 1)
    per = tmx // EXPERTS_PER_GROUP
    xe = xbuf[slot]
    x2 = xe[:, :D_MODEL]
    cw = xe[:, D_MODEL:]
    h = (x2 * lax.rsqrt(jnp.mean(x2 * x2, axis=-1, keepdims=True) + EPS) * g2_ref[...]).astype(BF16)
    lane = lax.broadcasted_iota(jnp.int32, cw.shape, 1)
    first = ROUTER_OFF + EXPERTS_PER_GROUP * grp_ref[i]
    for k in range(EXPERTS_PER_GROUP):
        for r in range(k * per, (k + 1) * per):
            gather_row(nxt, r, 1 - slot).start()
        gate = _dot(h, wg_ref[k])
        up = _dot(h, wu_ref[k])
        ck = jnp.sum(jnp.where(lane == first + k, cw, 0.0), axis=1, keepdims=True)
        abuf[:, k * D_EXPERT:(k + 1) * D_EXPERT] = (gate * _sigmoid(gate) * up * ck).astype(BF16)
    obuf[slot] = x2 + _dot(abuf[...], wd_ref[...])

    nv = nval_ref[i]

    @pl.when(nv == tmx)
    def _():
        for r in range(tmx):
            scatter_row(r).start()

    @pl.when(nv < tmx)
    def _():
        def body(r, c):
            scatter_row(r).start()
            return c

        lax.fori_loop(0, nv, body, 0)

    @pl.when(i == n_tiles - 1)
    def _():
        gather_wait(1 - slot)
        scatter_wait(nval_ref[i - 1], 1 - slot)
        scatter_wait(nv, slot)


def _moe(x2e, grp, nval, tok, g2, wg, wu, wd):
    n_tok = x2e.shape[0]
    tmx = MOE_TM
    n_tiles = grp.shape[0]
    kern = functools.partial(_moe_kernel, tmx=tmx, n_tiles=n_tiles)
    gk = EXPERTS_PER_GROUP * D_EXPERT
    grid_spec = pltpu.PrefetchScalarGridSpec(
        num_scalar_prefetch=3,
        grid=(n_tiles,),
        in_specs=[
            pl.BlockSpec(memory_space=pl.ANY),
            pl.BlockSpec((1, D_MODEL), lambda i, g, n, t: (0, 0)),
            pl.BlockSpec((None, EXPERTS_PER_GROUP, D_MODEL, D_EXPERT), lambda i, g, n, t: (g[i], 0, 0, 0)),
            pl.BlockSpec((None, EXPERTS_PER_GROUP, D_MODEL, D_EXPERT), lambda i, g, n, t: (g[i], 0, 0, 0)),
            pl.BlockSpec((None, gk, D_MODEL), lambda i, g, n, t: (g[i], 0, 0)),
        ],
        out_specs=pl.BlockSpec(memory_space=pl.ANY),
        scratch_shapes=[
            pltpu.VMEM((2, tmx, D_MODEL + LANES), F32),
            pltpu.VMEM((2, tmx, D_MODEL), F32),
            pltpu.VMEM((tmx, gk), BF16),
            pltpu.SemaphoreType.DMA((2,)),
            pltpu.SemaphoreType.DMA((2,)),
        ],
    )
    return pl.pallas_call(
        kern,
        grid_spec=grid_spec,
        out_shape=jax.ShapeDtypeStruct((n_tok, D_MODEL), F32),
        compiler_params=_params(("arbitrary",)),
        name="moe",
    )(grp, nval, tok, x2e, g2, wg, wu, wd)


def _block_diag_ones(n, blk):
    i = jnp.arange(n) // blk
    return (i[:, None] == i[None, :]).astype(BF16)


def _layer(x, norm1_g, w_in, lam_re, lam_im, log_step, b_re, b_im, c_re, c_im, d_skip,
           w_glu, b_glu, g_q, g_kc, g_ks, g_kw, pos_k, pos_v, w_ck1, w_ck2, w_cv1, w_cv2,
           out_g_ssm, out_g_att, w_out, norm2_g, w_grp, b_grp, w_exp, b_exp, w_gate, w_up, w_down):
    bsz, seq, _ = x.shape
    assert seq % 512 == 0 and seq // SEL_BLOCK <= MAX_SEL_BLOCKS
    n_tok = bsz * seq
    x2d = x.reshape(n_tok, D_MODEL)
    q8 = SSM_Q
    n_sub = seq // q8
    nch = seq // CMP_STRIDE

    o_q = D_SSM
    o_kv = D_SSM + D_ATT
    o_gt = o_kv + 6 * D_KV
    kv = lambda i: w_in[:, o_kv + i * D_KV:o_kv + (i + 1) * D_KV]
    wrow = jnp.concatenate([w_in[:, :o_q], kv(0), kv(1), kv(2), kv(4)], axis=1).astype(BF16)
    per_g = HPG * N_BRANCH
    wgt = jnp.zeros((D_MODEL, N_KV * GATE_ROWS), F32)
    for g in range(N_KV):
        wgt = wgt.at[:, g * GATE_ROWS:g * GATE_ROWS + per_g].set(w_in[:, o_gt + g * per_g:o_gt + (g + 1) * per_g])
    wcol = jnp.concatenate([w_in[:, o_q:o_kv], kv(3), kv(5), wgt], axis=1).T.astype(BF16)
    qscale = (HEAD_DIM ** -0.5) * math.log2(math.e)
    gq = (jnp.tile(g_q.astype(F32), N_HEADS) * qscale).reshape(D_ATT, 1)
    gks = jnp.tile(g_ks.astype(F32), N_KV).reshape(1, D_KV)
    gkw = jnp.tile(g_kw.astype(F32), N_KV).reshape(1, D_KV)

    u, qt, kc, vc, ksa, kw, vst, vwt, gate = _in_proj(
        x2d, norm1_g.reshape(1, D_MODEL), wrow, wcol, gq, gks, gkw,
        _block_diag_ones(D_KV, HEAD_DIM), bsz=bsz, seq=seq)

    w_loc, t_in, m_st, pw_re, pw_im, dvec = _s5_weights(
        lam_re, lam_im, log_step, b_re, b_im, c_re, c_im, d_skip, n_sub)
    ys = _s5(u.reshape(bsz, seq, D_SSM), w_loc, t_in, m_st, pw_re, pw_im, dvec,
             bsz=bsz, n_sub=n_sub).reshape(n_tok, D_SSM)

    wide = CMP_STRIDE * HEAD_DIM
    pad8 = lambda p: jnp.zeros((8, 2 * wide), F32).at[0].set(p.reshape(-1)).astype(BF16)
    kcmp, vcmpt = _compress(
        kc, vc,
        w_ck1.astype(BF16), w_ck2.astype(BF16), w_cv1.astype(BF16), w_cv2.T.astype(BF16),
        pad8(pos_k), pad8(pos_v), g_kc.astype(F32).reshape(1, HEAD_DIM), bsz=bsz, nch=nch)
    cstart = jnp.arange(nch) * CMP_STRIDE
    sstart = jnp.arange(MAX_SEL_BLOCKS) * SEL_BLOCK
    ovlt = ((cstart[None, :] < sstart[:, None] + SEL_BLOCK) & (cstart[None, :] + CMP_BLOCK > sstart[:, None])
            & (jnp.arange(MAX_SEL_BLOCKS)[:, None] < seq // SEL_BLOCK)
            & (jnp.arange(nch)[None, :] < nch - 1)).astype(BF16)
    yat = _nsa(qt, kcmp, vcmpt, ksa, kw, vst, vwt, gate, ovlt, bsz=bsz, seq=seq, nch=nch)

    wr = jnp.zeros((D_MODEL, LANES), F32)
    wr = wr.at[:, :N_EXP_GROUPS].set(w_grp).at[:, ROUTER_OFF:ROUTER_OFF + N_EXPERTS].set(w_exp).astype(BF16)
    br = jnp.zeros((1, LANES), F32)
    br = br.at[0, :N_EXP_GROUPS].set(b_grp).at[0, ROUTER_OFF:ROUTER_OFF + N_EXPERTS].set(b_exp)
    g2 = norm2_g.reshape(1, D_MODEL).astype(F32)
    x2e = _out_proj(
        ys, yat, x2d, w_glu.astype(BF16), b_glu.reshape(1, D_SSM).astype(F32),
        out_g_ssm.reshape(1, D_SSM).astype(F32), out_g_att.reshape(D_ATT, 1).astype(F32),
        w_out.astype(BF16), g2, wr, br, seq=seq)

    grp, nval, tok = _moe_plan(x2e[:, D_MODEL].astype(jnp.int32), n_tok)
    gshape = (N_EXP_GROUPS, EXPERTS_PER_GROUP, D_MODEL, D_EXPERT)
    out = _moe(x2e, grp, nval, tok, g2, w_gate.astype(BF16).reshape(gshape), w_up.astype(BF16).reshape(gshape),
               w_down.astype(BF16).reshape(N_EXP_GROUPS, EXPERTS_PER_GROUP * D_EXPERT, D_MODEL))
    return out.reshape(bsz, seq, D_MODEL)


def kernel(x, norm1_g, w_in, lam_re, lam_im, log_step, b_re, b_im, c_re, c_im, d_skip, w_glu, b_glu, g_q, g_kc, g_ks, g_kw, pos_k, pos_v, w_ck1, w_ck2, w_cv1, w_cv2, out_g_ssm, out_g_att, w_out, norm2_g, w_grp, b_grp, w_exp, b_exp, w_gate, w_up, w_down):
    depth = norm1_g.shape[0]
    for l in range(depth):
        x = _layer(x, norm1_g[l], w_in[l], lam_re[l], lam_im[l], log_step[l], b_re[l], b_im[l], c_re[l],
                   c_im[l], d_skip[l], w_glu[l], b_glu[l], g_q[l], g_kc[l], g_ks[l], g_kw[l], pos_k[l],
                   pos_v[l], w_ck1[l], w_ck2[l], w_cv1[l], w_cv2[l], out_g_ssm[l], out_g_att[l], w_out[l],
                   norm2_g[l], w_grp[l], b_grp[l], w_exp[l], b_exp[l], w_gate[l], w_up[l], w_down[l])
    return x
```

```python
import functools
import math

import jax
import jax.numpy as jnp
from jax import lax
from jax.experimental import pallas as pl
from jax.experimental.pallas import tpu as pltpu

D_MODEL = 1024
D_SSM = 512
SSM_CH = 16
SSM_GROUPS = D_SSM // SSM_CH
SSM_STATE = 64
D_ATT = 512
HEAD_DIM = 64
N_HEADS = D_ATT // HEAD_DIM
N_KV = 2
HPG = N_HEADS // N_KV
D_KV = N_KV * HEAD_DIM
N_BRANCH = 3
CMP_STRIDE = 16
CMP_BLOCK = 2 * CMP_STRIDE
CMP_HIDDEN = 256
SEL_BLOCK = 64
N_SELECT = 16
WINDOW = 512
N_EXP_GROUPS = 4
EXPERTS_PER_GROUP = 8
N_EXPERTS = N_EXP_GROUPS * EXPERTS_PER_GROUP
D_EXPERT = 256
EPS = 1e-6
NEG = -1e30
FORCE = 1e9

LANES = 128
SSM_Q = 8
SSM_LT = D_SSM // LANES
ROUTER_OFF = N_EXP_GROUPS
NSA_TQ = 256
NSA_TK = 256
NSA_UNROLL = 4
V_ROWS = HEAD_DIM + 16
MAX_SEL_BLOCKS = 64
MOE_TM = 256
GATE_ROWS = 16
VMEM_LIMIT = 56 * 1024 * 1024

F32 = jnp.float32
BF16 = jnp.bfloat16


def _dot(a, b):
    return jnp.dot(a, b, preferred_element_type=F32)


def _dot_nt(a, b):
    return lax.dot_general(a, b, (((1,), (1,)), ((), ())), preferred_element_type=F32)


def _split_dot(x, w):
    hi = x.astype(BF16)
    lo = (x - hi.astype(F32)).astype(BF16)
    return _dot(hi, w) + _dot(lo, w)


def _gelu(x):
    c = math.sqrt(2.0 / math.pi)
    return 0.5 * x * (1.0 + jnp.tanh(c * (x + 0.044715 * (x * x * x))))


def _sigmoid(x):
    return 1.0 / (1.0 + jnp.exp(-x))


def _params(sem):
    return pltpu.CompilerParams(dimension_semantics=sem, vmem_limit_bytes=VMEM_LIMIT)


def _in_proj_kernel(x_ref, g1_ref, wrow_ref, wcol_ref, gq_ref, gks_ref, gkw_ref, bd128_ref,
                    u_ref, qt_ref, kc_ref, vc_ref, ksa_ref, kw_ref, vst_ref, vwt_ref, gate_ref, *, tm, nl):
    x = x_ref[...]
    ms = jnp.mean(x * x, axis=-1, keepdims=True)
    hn = (x * lax.rsqrt(ms + EPS) * g1_ref[...]).astype(BF16)

    pr = _dot(hn, wrow_ref[...])
    u_ref[...] = pr[:, :D_SSM]
    kc, vc, ks, kw = [pr[:, D_SSM + i * D_KV:D_SSM + (i + 1) * D_KV] for i in range(4)]
    kss = _split_dot(ks * ks, bd128_ref[...])
    ksn = ks * lax.rsqrt(kss * (1.0 / HEAD_DIM) + EPS) * gks_ref[...]
    kws = _split_dot(kw * kw, bd128_ref[...])
    kwn = kw * lax.rsqrt(kws * (1.0 / HEAD_DIM) + EPS) * gkw_ref[...]
    t0 = (pl.program_id(0) % nl) * tm
    tpos = t0 + lax.broadcasted_iota(jnp.int32, (tm, MAX_SEL_BLOCKS), 0)
    blk = lax.broadcasted_iota(jnp.int32, (tm, MAX_SEL_BLOCKS), 1)
    onehot = jnp.where(tpos // SEL_BLOCK == blk, 1.0, 0.0).astype(BF16)
    for g in range(N_KV):
        sl = slice(g * HEAD_DIM, (g + 1) * HEAD_DIM)
        kc_ref[g] = kc[:, sl]
        vc_ref[g] = vc[:, sl]
        ksa_ref[g] = jnp.concatenate([ksn[:, sl].astype(BF16), onehot], axis=1)
        kw_ref[g] = kwn[:, sl].astype(BF16)

    pc = _dot_nt(wcol_ref[...], hn)
    gq = gq_ref[...]
    for h in range(N_HEADS):
        sl = slice(h * HEAD_DIM, (h + 1) * HEAD_DIM)
        qh = pc[sl]
        ss = jnp.sum(qh * qh, axis=0, keepdims=True)
        qt_ref[h] = (qh * lax.rsqrt(ss * (1.0 / HEAD_DIM) + EPS) * gq[sl]).astype(BF16)
    ones_rows = jnp.where(lax.broadcasted_iota(jnp.int32, (V_ROWS - HEAD_DIM, tm), 0) == 0, 1.0, 0.0)
    for g in range(N_KV):
        for o_ref, base in ((vst_ref, D_ATT), (vwt_ref, D_ATT + D_KV)):
            vt = jnp.concatenate([pc[base + g * HEAD_DIM:base + (g + 1) * HEAD_DIM], ones_rows], axis=0)
            vt = vt.astype(BF16)
            for j in range(tm // NSA_TK):
                o_ref[g, j] = vt[:, j * NSA_TK:(j + 1) * NSA_TK]
        gb = D_ATT + 2 * D_KV + g * GATE_ROWS
        gate_ref[g] = _sigmoid(pc[gb:gb + GATE_ROWS])


def _in_proj(x2d, g1, wrow, wcol, gq, gks, gkw, bd128, *, bsz, seq):
    tm = 512
    nl = seq // tm
    n_tok = bsz * seq
    kern = functools.partial(_in_proj_kernel, tm=tm, nl=nl)
    row = lambda i: (i, 0)
    const = lambda i: (0, 0)
    bgl = lambda i: (i // nl, 0, i % nl, 0)
    n_col = wcol.shape[0]
    jt = tm // NSA_TK

    def kvspec(width):
        return pl.BlockSpec((None, N_KV, tm, width), bgl)

    def kvshape(width, dtype=BF16):
        return jax.ShapeDtypeStruct((bsz, N_KV, seq, width), dtype)

    vt_spec = pl.BlockSpec((None, N_KV, jt, V_ROWS, NSA_TK), lambda i: (i // nl, 0, i % nl, 0, 0))
    vt_shape = jax.ShapeDtypeStruct((bsz, N_KV, seq // NSA_TK, V_ROWS, NSA_TK), BF16)
    return pl.pallas_call(
        kern,
        grid=(n_tok // tm,),
        in_specs=[
            pl.BlockSpec((tm, D_MODEL), row),
            pl.BlockSpec((1, D_MODEL), const),
            pl.BlockSpec((D_MODEL, D_SSM + 4 * D_KV), const),
            pl.BlockSpec((n_col, D_MODEL), const),
            pl.BlockSpec((D_ATT, 1), const),
            pl.BlockSpec((1, D_KV), const),
            pl.BlockSpec((1, D_KV), const),
            pl.BlockSpec((D_KV, D_KV), const),
        ],
        out_specs=[
            pl.BlockSpec((tm, D_SSM), row),
            pl.BlockSpec((None, N_HEADS, HEAD_DIM, tm), lambda i: (i // nl, 0, 0, i % nl)),
            kvspec(HEAD_DIM), kvspec(HEAD_DIM), kvspec(2 * HEAD_DIM), kvspec(HEAD_DIM),
            vt_spec, vt_spec,
            pl.BlockSpec((None, N_KV, GATE_ROWS, tm), lambda i: (i // nl, 0, 0, i % nl)),
        ],
        out_shape=[
            jax.ShapeDtypeStruct((n_tok, D_SSM), F32),
            jax.ShapeDtypeStruct((bsz, N_HEADS, HEAD_DIM, seq), BF16),
            kvshape(HEAD_DIM, F32), kvshape(HEAD_DIM, F32), kvshape(2 * HEAD_DIM), kvshape(HEAD_DIM),
            vt_shape, vt_shape,
            jax.ShapeDtypeStruct((bsz, N_KV, GATE_ROWS, seq), F32),
        ],
        compiler_params=_params(("parallel",)),
        name="in_proj",
    )(x2d, g1, wrow, wcol, gq, gks, gkw, bd128)


def _s5_weights(lam_re, lam_im, log_step, b_re, b_im, c_re, c_im, d_skip, n_sub):
    q = SSM_Q
    lam = lax.complex(lam_re.astype(F32), lam_im.astype(F32))
    step = jnp.exp(log_step.astype(F32))[:, None]
    lam_bar = jnp.exp(lam * step)
    b_bar = ((lam_bar - 1.0) / lam)[..., None] * lax.complex(b_re.astype(F32), b_im.astype(F32))
    c = lax.complex(c_re.astype(F32), c_im.astype(F32))
    pows = [jnp.ones_like(lam_bar)]
    for _ in range(q):
        pows.append(pows[-1] * lam_bar)
    pw = jnp.stack(pows)
    lt, a8 = SSM_LT, LANES // SSM_CH
    hp = a8 * SSM_STATE
    e_lane = (jnp.arange(a8)[:, None] == jnp.arange(LANES)[None, :] // SSM_CH).astype(F32)
    e_state = (jnp.arange(a8)[:, None] == jnp.arange(hp)[None, :] // SSM_STATE).astype(F32)
    e_lane_t = jnp.tile(e_lane, (1, q))
    e_state_t = jnp.tile(e_state, (1, 2))

    kk = jnp.real(jnp.einsum('ghp,kgp,gpi->kghi', c, pw[:q], b_bar))
    km = kk.reshape(q, lt, a8, SSM_CH, SSM_CH).transpose(1, 4, 0, 2, 3).reshape(lt, SSM_CH, q, LANES)
    lag = jnp.arange(q)[None, :] - jnp.arange(q)[:, None]
    kg = km[:, :, jnp.clip(lag, 0, q - 1), :] * (lag >= 0)[None, None, :, :, None].astype(F32)
    kc = kg.transpose(0, 2, 1, 3, 4).reshape(lt, q, 1, SSM_CH, q * LANES)
    t_in = (kc * e_lane_t[None, None, :, None, :]).reshape(lt, q * LANES, q * LANES)

    wc = pw[q - 1 - jnp.arange(q)][..., None] * b_bar[None]
    wri = jnp.stack([jnp.real(wc), jnp.imag(wc)])
    wm = (wri.reshape(2, q, lt, a8, SSM_STATE, SSM_CH).transpose(2, 1, 5, 0, 3, 4)
          .reshape(lt, q, 1, SSM_CH, 2 * hp))
    w_loc = (wm * e_state_t[None, None, :, None, :]).reshape(lt, q * LANES, 2 * hp)

    cl = c[None] * pw[1:q + 1][:, :, None, :]
    cri = jnp.stack([jnp.real(cl), -jnp.imag(cl)])
    mm = (cri.reshape(2, q, lt, a8, SSM_CH, SSM_STATE).transpose(2, 0, 5, 1, 3, 4)
          .reshape(lt, 2, 1, SSM_STATE, q * LANES))
    m_st = (mm * e_lane_t[None, None, :, None, :]).reshape(lt, 2 * hp, q * LANES)

    n_lvl = max(1, (n_sub - 1).bit_length())
    lv = [pw[q]]
    for _ in range(n_lvl - 1):
        lv.append(lv[-1] * lv[-1])
    lvs = jnp.stack(lv).reshape(n_lvl, lt, 1, hp)
    pw_re = jnp.real(lvs).transpose(1, 0, 2, 3)
    pw_im = jnp.imag(lvs).transpose(1, 0, 2, 3)
    dvec = jnp.tile(d_skip.astype(F32).reshape(lt, 1, LANES), (1, 1, q))
    return w_loc.astype(BF16), t_in.astype(BF16), m_st.astype(BF16), pw_re, pw_im, dvec


def _s5_kernel(u_ref, w_ref, t_ref, m_ref, pwr_ref, pwi_ref, d_ref, y_ref, *, n_sub, n_lvl):
    half = (LANES // SSM_CH) * SSM_STATE
    q = SSM_Q
    u = jnp.concatenate([u_ref[pl.ds(s, n_sub, stride=q), :] for s in range(q)], axis=1)
    ub = u.astype(BF16)
    s_loc = _dot(ub, w_ref[...])
    re = s_loc[:, :half]
    im = s_loc[:, half:]
    rowi = lax.broadcasted_iota(jnp.int32, (n_sub, half), 0)
    for k in range(n_lvl):
        d = 1 << k
        ar = pwr_ref[k]
        ai = pwi_ref[k]
        keep = rowi >= d
        sre = jnp.where(keep, pltpu.roll(re, d, axis=0), 0.0)
        sim = jnp.where(keep, pltpu.roll(im, d, axis=0), 0.0)
        re, im = re + (ar * sre - ai * sim), im + (ar * sim + ai * sre)
    keep = rowi >= 1
    xre = jnp.where(keep, pltpu.roll(re, 1, axis=0), 0.0)
    xim = jnp.where(keep, pltpu.roll(im, 1, axis=0), 0.0)
    xst = jnp.concatenate([xre, xim], axis=1).astype(BF16)
    y = _dot(ub, t_ref[...]) + _dot(xst, m_ref[...]) + d_ref[...] * u
    for j in range(q):
        y_ref[pl.ds(j, n_sub, stride=q), :] = y[:, j * LANES:(j + 1) * LANES]


def _s5(u, w_loc, t_in, m_st, pw_re, pw_im, dvec, *, bsz, n_sub):
    q = SSM_Q
    n_lvl = pw_re.shape[1]
    kern = functools.partial(_s5_kernel, n_sub=n_sub, n_lvl=n_lvl)
    wide = q * LANES
    seq = n_sub * q
    return pl.pallas_call(
        kern,
        grid=(bsz, SSM_LT),
        in_specs=[
            pl.BlockSpec((None, seq, LANES), lambda b, l: (b, 0, l)),
            pl.BlockSpec((None, wide, wide), lambda b, l: (l, 0, 0)),
            pl.BlockSpec((None, wide, wide), lambda b, l: (l, 0, 0)),
            pl.BlockSpec((None, wide, wide), lambda b, l: (l, 0, 0)),
            pl.BlockSpec((None, n_lvl, 1, wide // 2), lambda b, l: (l, 0, 0, 0)),
            pl.BlockSpec((None, n_lvl, 1, wide // 2), lambda b, l: (l, 0, 0, 0)),
            pl.BlockSpec((None, 1, wide), lambda b, l: (l, 0, 0)),
        ],
        out_specs=pl.BlockSpec((None, seq, LANES), lambda b, l: (b, 0, l)),
        out_shape=jax.ShapeDtypeStruct((bsz, seq, D_SSM), F32),
        compiler_params=_params(("parallel", "parallel")),
        name="s5",
    )(u, w_loc, t_in, m_st, pw_re, pw_im, dvec)


def _compress_kernel(kc_ref, vc_ref, w1k_ref, w2k_ref, w1v_ref, w2vt_ref, posk_ref, posv_ref, gkc_ref,
                     kcmp_ref, vcmpt_ref, *, nch):
    half = CMP_STRIDE * HEAD_DIM

    def hidden(x_ref, w1_ref, pos_ref):
        a = jnp.zeros((nch, CMP_HIDDEN), F32)
        b = jnp.zeros((nch, CMP_HIDDEN), F32)
        for j in range(CMP_STRIDE):
            xj = x_ref[pl.ds(j, nch, stride=CMP_STRIDE), :].astype(BF16)
            a = a + _dot(xj, w1_ref[j * HEAD_DIM:(j + 1) * HEAD_DIM, :])
            b = b + _dot(xj, w1_ref[half + j * HEAD_DIM:half + (j + 1) * HEAD_DIM, :])
        pv = _dot(pos_ref[...], w1_ref[...])[0:1, :]
        hid = a + pltpu.roll(b, nch - 1, axis=0) + pv
        return _gelu(hid).astype(BF16)

    k = _dot(hidden(kc_ref, w1k_ref, posk_ref), w2k_ref[...])
    ms = jnp.mean(k * k, axis=-1, keepdims=True)
    kcmp_ref[...] = (k * lax.rsqrt(ms + EPS) * gkc_ref[...]).astype(BF16)
    vt = _dot_nt(w2vt_ref[...], hidden(vc_ref, w1v_ref, posv_ref))
    coli = lax.broadcasted_iota(jnp.int32, vt.shape, 1)
    vcmpt_ref[...] = jnp.where(coli < nch - 1, vt, 0.0).astype(BF16)


def _compress(kcf, vcf, w1k, w2k, w1v, w2vt, posk, posv, gkc, *, bsz, nch):
    kern = functools.partial(_compress_kernel, nch=nch)
    wide = CMP_STRIDE * HEAD_DIM
    xspec = pl.BlockSpec((None, None, nch * CMP_STRIDE, HEAD_DIM), lambda b, g: (b, g, 0, 0))
    c2 = lambda b, g: (0, 0)
    return pl.pallas_call(
        kern,
        grid=(bsz, N_KV),
        in_specs=[
            xspec, xspec,
            pl.BlockSpec((2 * wide, CMP_HIDDEN), c2), pl.BlockSpec((CMP_HIDDEN, HEAD_DIM), c2),
            pl.BlockSpec((2 * wide, CMP_HIDDEN), c2), pl.BlockSpec((HEAD_DIM, CMP_HIDDEN), c2),
            pl.BlockSpec((8, 2 * wide), c2), pl.BlockSpec((8, 2 * wide), c2),
            pl.BlockSpec((1, HEAD_DIM), c2),
        ],
        out_specs=[pl.BlockSpec((None, None, nch, HEAD_DIM), lambda b, g: (b, g, 0, 0)),
                   pl.BlockSpec((None, None, HEAD_DIM, nch), lambda b, g: (b, g, 0, 0))],
        out_shape=[jax.ShapeDtypeStruct((bsz, N_KV, nch, HEAD_DIM), BF16),
                   jax.ShapeDtypeStruct((bsz, N_KV, HEAD_DIM, nch), BF16)],
        compiler_params=_params(("parallel", "parallel")),
        name="compress",
    )(kcf, vcf, w1k, w2k, w1v, w2vt, posk, posv, gkc)


def _nsa_kernel(qt_ref, kcmp_ref, vcmpt_ref, ksa_ref, kw_ref, vst_ref, vwt_ref, gate_ref, ovlt_ref, o_ref,
                accs_ref, accw_ref, *, tq, tk, nch):
    qi = pl.program_id(2)
    q0 = qi * tq
    rows = HPG * tq
    qt = jnp.concatenate([qt_ref[h] for h in range(HPG)], axis=1)
    tpos = q0 + lax.broadcasted_iota(jnp.int32, (1, rows), 1) % tq
    krow = lax.broadcasted_iota(jnp.int32, (tk, rows), 0)
    m0 = jnp.full((1, rows), NEG, F32)

    def run_tiles(k_ref, q_all, vt_ref, acc_ref, m, tiles):
        scores = []
        for kt, kind in tiles:
            kc = jnp.maximum(kt, 0) if kind in ('band', 'valid') else kt
            s = _dot(k_ref[pl.ds(pl.multiple_of(kc * tk, tk), tk), :], q_all)
            if kind == 'causal':
                s = jnp.where(kt * tk + krow <= tpos, s, NEG)
            elif kind == 'band':
                s = jnp.where((kt * tk + krow > tpos - WINDOW) & (kt >= 0), s, NEG)
            elif kind == 'valid':
                s = jnp.where(kt >= 0, s, NEG)
            scores.append((kc, s))
        for kc, s in scores:
            m_new = jnp.maximum(m, jnp.max(s, axis=0, keepdims=True))
            alpha = jnp.exp2(m - m_new)
            p = jnp.exp2(s - m_new).astype(BF16)
            acc_ref[...] = alpha * acc_ref[...] + _dot(vt_ref[kc], p)
            m = m_new
        return m

    accw_ref[...] = jnp.zeros_like(accw_ref)
    n_win = WINDOW // tk
    run_tiles(kw_ref, qt, vwt_ref, accw_ref, m0,
              [(qi - n_win, 'band')] + [(qi - n_win + t, 'valid') for t in range(1, n_win)] + [(qi, 'causal')])

    s = _dot(kcmp_ref[...], qt)
    cend = lax.broadcasted_iota(jnp.int32, (nch, rows), 0) * CMP_STRIDE + (CMP_BLOCK - 1)
    s = jnp.where(cend <= tpos, s, NEG)
    m = jnp.max(s, axis=0, keepdims=True)
    p = jnp.exp2(s - m)
    p = p * (1.0 / jnp.sum(p, axis=0, keepdims=True))
    p = jnp.where(tpos >= CMP_BLOCK - 1, p, 0.0)
    o_cmp = _dot(vcmpt_ref[...], p.astype(BF16))

    psum = p[:, 0:tq]
    for h in range(1, HPG):
        psum = psum + p[:, h * tq:(h + 1) * tq]
    hi = psum.astype(BF16)
    lo = (psum - hi.astype(F32)).astype(BF16)
    ovlt = ovlt_ref[...]
    imp = _dot(ovlt, hi) + _dot(ovlt, lo)
    nb = MAX_SEL_BLOCKS
    blk = lax.broadcasted_iota(jnp.int32, (nb, tq), 0)
    cur = (q0 + lax.broadcasted_iota(jnp.int32, (nb, tq), 1)) // SEL_BLOCK
    forced = (blk == 0) | (blk == cur) | (blk == cur - 1)
    imp = jnp.where(forced, FORCE, jnp.where(blk <= cur, imp, NEG))
    sub = 8
    groups = [imp[r:r + sub] for r in range(0, nb, sub)]
    ranks = [jnp.zeros((sub, tq), F32) for _ in groups]
    rowl = lax.broadcasted_iota(jnp.int32, (sub, tq), 0)
    for i in range(nb):
        ri = jnp.broadcast_to(imp[i:i + 1, :], (sub, tq))
        for gi, x in enumerate(groups):
            if i < gi * sub:
                ahead = ri >= x
            elif i >= (gi + 1) * sub:
                ahead = ri > x
            else:
                ahead = (ri > x) | ((ri == x) & (rowl > i - gi * sub))
            ranks[gi] = ranks[gi] + jnp.where(ahead, 1.0, 0.0)
    rank = jnp.concatenate(ranks, axis=0)
    sel = jnp.where(rank < N_SELECT, 0.0, NEG).astype(BF16)
    qa = jnp.concatenate([qt, jnp.concatenate([sel] * HPG, axis=1)], axis=0)

    accs_ref[...] = jnp.zeros_like(accs_ref)
    nu = NSA_UNROLL

    def sel_multi(j, m):
        return run_tiles(ksa_ref, qa, vst_ref, accs_ref, m, [(nu * j + t, None) for t in range(nu)])

    m_sel = lax.fori_loop(0, qi // nu, sel_multi, m0)
    for r in range(nu):
        @pl.when(qi % nu == r)
        def _(r=r):
            base = qi - r
            run_tiles(ksa_ref, qa, vst_ref, accs_ref, m_sel,
                      [(base + t, None) for t in range(r)] + [(qi, 'causal')])

    accs = accs_ref[...]
    accw = accw_ref[...]
    o_sel = accs[:HEAD_DIM] * (1.0 / accs[HEAD_DIM:HEAD_DIM + 1])
    o_win = accw[:HEAD_DIM] * (1.0 / accw[HEAD_DIM:HEAD_DIM + 1])
    gt = gate_ref[...]
    for h in range(HPG):
        c = slice(h * tq, (h + 1) * tq)
        o_ref[h * HEAD_DIM:(h + 1) * HEAD_DIM, :] = (
            gt[3 * h:3 * h + 1] * o_cmp[:, c] + gt[3 * h + 1:3 * h + 2] * o_sel[:, c]
            + gt[3 * h + 2:3 * h + 3] * o_win[:, c])


def _nsa(qt, kcmp, vcmpt, ksa, kw, vst, vwt, gate, ovlt, *, bsz, seq, nch):
    tq, tk = NSA_TQ, NSA_TK
    assert tq == tk and WINDOW % tk == 0
    nq = seq // tq
    kern = functools.partial(_nsa_kernel, tq=tq, tk=tk, nch=nch)
    full = lambda b, g, i: (b, g, 0, 0)
    full5 = lambda b, g, i: (b, g, 0, 0, 0)
    qd = HPG * HEAD_DIM
    rows = HPG * tq
    return pl.pallas_call(
        kern,
        grid=(bsz, N_KV, nq),
        in_specs=[
            pl.BlockSpec((None, HPG, HEAD_DIM, tq), lambda b, g, i: (b, g, 0, i)),
            pl.BlockSpec((None, None, nch, HEAD_DIM), full),
            pl.BlockSpec((None, None, HEAD_DIM, nch), full),
            pl.BlockSpec((None, None, seq, 2 * HEAD_DIM), full),
            pl.BlockSpec((None, None, seq, HEAD_DIM), full),
            pl.BlockSpec((None, None, seq // tk, V_ROWS, tk), full5),
            pl.BlockSpec((None, None, seq // tk, V_ROWS, tk), full5),
            pl.BlockSpec((None, None, GATE_ROWS, tq), lambda b, g, i: (b, g, 0, i)),
            pl.BlockSpec((MAX_SEL_BLOCKS, nch), lambda b, g, i: (0, 0)),
        ],
        out_specs=pl.BlockSpec((None, qd, tq), lambda b, g, i: (b, g, i)),
        out_shape=jax.ShapeDtypeStruct((bsz, D_ATT, seq), F32),
        scratch_shapes=[pltpu.VMEM((V_ROWS, rows), F32), pltpu.VMEM((V_ROWS, rows), F32)],
        compiler_params=_params(("parallel", "parallel", "arbitrary")),
        name="nsa",
    )(qt, kcmp, vcmpt, ksa, kw, vst, vwt, gate, ovlt)


def _out_proj_kernel(ys_ref, yat_ref, x_ref, wglu_ref, bglu_ref, gs_ref, ga_ref, wo_ref, g2_ref, wr_ref, br_ref,
                     x2e_ref):
    y = _gelu(ys_ref[...])
    y = y * _sigmoid(_dot(y.astype(BF16), wglu_ref[...]) + bglu_ref[...])
    ysn = y * lax.rsqrt(jnp.mean(y * y, axis=-1, keepdims=True) + EPS) * gs_ref[...]
    yat = yat_ref[...]
    yant = yat * lax.rsqrt(jnp.mean(yat * yat, axis=0, keepdims=True) + EPS) * ga_ref[...]
    yan = yant.T
    x2 = x_ref[...] + _dot(ysn.astype(BF16), wo_ref[:D_SSM, :]) + _dot(yan.astype(BF16), wo_ref[D_SSM:, :])
    x2e_ref[:, :D_MODEL] = x2
    h2 = (x2 * lax.rsqrt(jnp.mean(x2 * x2, axis=-1, keepdims=True) + EPS) * g2_ref[...]).astype(BF16)

    logits = _dot(h2, wr_ref[...]) + br_ref[...]
    lane = lax.broadcasted_iota(jnp.int32, logits.shape, 1).astype(F32)
    far = float(LANES)
    is_g = lane < N_EXP_GROUPS
    glog = jnp.where(is_g, logits, -jnp.inf)
    gmax = jnp.max(glog, axis=1, keepdims=True)
    gsum = jnp.sum(jnp.where(is_g, jnp.exp(logits - gmax), 0.0), axis=1, keepdims=True)
    gsel = jnp.min(jnp.where(glog == gmax, lane, far), axis=1, keepdims=True)
    gprob = 1.0 / gsum
    lo = ROUTER_OFF + EXPERTS_PER_GROUP * gsel
    in_e = (lane >= lo) & (lane < lo + EXPERTS_PER_GROUP)
    emax = jnp.max(jnp.where(in_e, logits, -jnp.inf), axis=1, keepdims=True)
    eexp = jnp.where(in_e, jnp.exp(logits - emax), 0.0)
    eprob = jnp.where(in_e, eexp / jnp.sum(eexp, axis=1, keepdims=True), -1.0)
    v1 = jnp.max(eprob, axis=1, keepdims=True)
    i1 = jnp.min(jnp.where(eprob == v1, lane, far), axis=1, keepdims=True)
    rest = jnp.where(lane == i1, -1.0, eprob)
    v2 = jnp.max(rest, axis=1, keepdims=True)
    i2 = jnp.min(jnp.where(rest == v2, lane, far), axis=1, keepdims=True)
    den = v1 + v2
    x2e_ref[:, D_MODEL:] = (jnp.where(lane == i1, v1 / den * gprob, 0.0)
                            + jnp.where(lane == i2, v2 / den * gprob, 0.0)
                            + jnp.where(lane == 0.0, gsel, 0.0))


def _out_proj(ys, yat, x2d, wglu, bglu, gs, ga, wo, g2, wr, br, *, seq):
    n_tok = x2d.shape[0]
    tm = 512
    nl = seq // tm
    row = lambda i: (i, 0)
    const = lambda i: (0, 0)
    return pl.pallas_call(
        _out_proj_kernel,
        grid=(n_tok // tm,),
        in_specs=[
            pl.BlockSpec((tm, D_SSM), row),
            pl.BlockSpec((None, D_ATT, tm), lambda i: (i // nl, 0, i % nl)),
            pl.BlockSpec((tm, D_MODEL), row),
            pl.BlockSpec((D_SSM, D_SSM), const),
            pl.BlockSpec((1, D_SSM), const),
            pl.BlockSpec((1, D_SSM), const),
            pl.BlockSpec((D_ATT, 1), const),
            pl.BlockSpec((D_SSM + D_ATT, D_MODEL), const),
            pl.BlockSpec((1, D_MODEL), const),
            pl.BlockSpec((D_MODEL, LANES), const),
            pl.BlockSpec((1, LANES), const),
        ],
        out_specs=pl.BlockSpec((tm, D_MODEL + LANES), row),
        out_shape=jax.ShapeDtypeStruct((n_tok, D_MODEL + LANES), F32),
        compiler_params=_params(("parallel",)),
        name="out_proj",
    )(ys, yat, x2d, wglu, bglu, gs, ga, wo, g2, wr, br)


def _moe_plan(gsel, n_tok):
    tmx = MOE_TM
    n_tiles = n_tok // tmx + N_EXP_GROUPS
    oh = (gsel[:, None] == jnp.arange(N_EXP_GROUPS)[None, :]).astype(jnp.int32)
    csum = jnp.cumsum(oh, axis=0)
    counts = csum[-1]
    rank = jnp.sum(csum * oh, axis=1) - 1
    nt = (counts + tmx - 1) // tmx
    tend = jnp.cumsum(nt)
    toff = tend - nt
    pos = jnp.sum(oh * toff[None, :], axis=1) * tmx + rank
    tok = jnp.zeros((n_tiles * tmx,), jnp.int32).at[pos].set(jnp.arange(n_tok, dtype=jnp.int32))
    tile = jnp.arange(n_tiles)
    grp = jnp.minimum(jnp.sum((tile[:, None] >= tend[None, :]).astype(jnp.int32), axis=1), N_EXP_GROUPS - 1)
    nval = jnp.clip(counts[grp] - (tile - toff[grp]) * tmx, 0, tmx)
    return grp.astype(jnp.int32), nval.astype(jnp.int32), tok


def _moe_kernel(grp_ref, nval_ref, tok_ref, x_hbm, g2_ref, wg_ref, wu_ref, wd_ref, o_hbm,
                xbuf, obuf, abuf, gsem, ssem, *, tmx, n_tiles):
    i = pl.program_id(0)
    slot = i % 2

    def gather_row(tile, r, dst_slot):
        t = tok_ref[tile * tmx + r]
        return pltpu.make_async_copy(x_hbm.at[pl.ds(t, 1), :], xbuf.at[dst_slot, pl.ds(r, 1), :], gsem.at[dst_slot])

    def scatter_row(r):
        t = tok_ref[i * tmx + r]
        return pltpu.make_async_copy(obuf.at[slot, pl.ds(r, 1), :], o_hbm.at[pl.ds(t, 1), :], ssem.at[slot])

    def gather_wait(src_slot):
        pltpu.make_async_copy(x_hbm.at[pl.ds(0, tmx), :], xbuf.at[src_slot], gsem.at[src_slot]).wait()

    def scatter_wait(n, src_slot):
        n8 = pl.multiple_of((n // 8) * 8, 8)

        @pl.when(n8 > 0)
        def _():
            pltpu.make_async_copy(obuf.at[src_slot, pl.ds(0, n8), :], o_hbm.at[pl.ds(0, n8), :],
                                  ssem.at[src_slot]).wait()

        def one(r, c):
            pltpu.make_async_copy(obuf.at[src_slot, pl.ds(0, 1), :], o_hbm.at[pl.ds(0, 1), :],
                                  ssem.at[src_slot]).wait()
            return c

        lax.fori_loop(0, n - n8, one, 0)

    @pl.when(i == 0)
    def _():
        def body(r, c):
            gather_row(0, r, 0).start()
            return c

        lax.fori_loop(0, tmx, body, 0)

    gather_wait(slot)

    @pl.when(i >= 2)
    def _():
        scatter_wait(nval_ref[i - 2], slot)

    nxt = jnp.minimum(i + 1, n_tiles - 1)
    per = tmx // EXPERTS_PER_GROUP
    xe = xbuf[slot]
    x2 = xe[:, :D_MODEL]
    cw = xe[:, D_MODEL:]
    h = (x2 * lax.rsqrt(jnp.mean(x2 * x2, axis=-1, keepdims=True) + EPS) * g2_ref[...]).astype(BF16)
    lane = lax.broadcasted_iota(jnp.int32, cw.shape, 1)
    first = ROUTER_OFF + EXPERTS_PER_GROUP * grp_ref[i]
    for k in range(EXPERTS_PER_GROUP):
        for r in range(k * per, (k + 1) * per):
            gather_row(nxt, r, 1 - slot).start()
        gate = _dot(h, wg_ref[k])
        up = _dot(h, wu_ref[k])
        ck = jnp.sum(jnp.where(lane == first + k, cw, 0.0), axis=1, keepdims=True)
        abuf[:, k * D_EXPERT:(k + 1) * D_EXPERT] = (gate * _sigmoid(gate) * up * ck).astype(BF16)
    obuf[slot] = x2 + _dot(abuf[...], wd_ref[...])

    nv = nval_ref[i]

    @pl.when(nv == tmx)
    def _():
        for r in range(tmx):
            scatter_row(r).start()

    @pl.when(nv < tmx)
    def _():
        def body(r, c):
            scatter_row(r).start()
            return c

        lax.fori_loop(0, nv, body, 0)

    @pl.when(i == n_tiles - 1)
    def _():
        gather_wait(1 - slot)
        scatter_wait(nval_ref[i - 1], 1 - slot)
        scatter_wait(nv, slot)


def _moe(x2e, grp, nval, tok, g2, wg, wu, wd):
    n_tok = x2e.shape[0]
    tmx = MOE_TM
    n_tiles = grp.shape[0]
    kern = functools.partial(_moe_kernel, tmx=tmx, n_tiles=n_tiles)
    gk = EXPERTS_PER_GROUP * D_EXPERT
    grid_spec = pltpu.PrefetchScalarGridSpec(
        num_scalar_prefetch=3,
        grid=(n_tiles,),
        in_specs=[
            pl.BlockSpec(memory_space=pl.ANY),
            pl.BlockSpec((1, D_MODEL), lambda i, g, n, t: (0, 0)),
            pl.BlockSpec((None, EXPERTS_PER_GROUP, D_MODEL, D_EXPERT), lambda i, g, n, t: (g[i], 0, 0, 0)),
            pl.BlockSpec((None, EXPERTS_PER_GROUP, D_MODEL, D_EXPERT), lambda i, g, n, t: (g[i], 0, 0, 0)),
            pl.BlockSpec((None, gk, D_MODEL), lambda i, g, n, t: (g[i], 0, 0)),
        ],
        out_specs=pl.BlockSpec(memory_space=pl.ANY),
        scratch_shapes=[
            pltpu.VMEM((2, tmx, D_MODEL + LANES), F32),
            pltpu.VMEM((2, tmx, D_MODEL), F32),
            pltpu.VMEM((tmx, gk), BF16),
            pltpu.SemaphoreType.DMA((2,)),
            pltpu.SemaphoreType.DMA((2,)),
        ],
    )
    return pl.pallas_call(
        kern,
        grid_spec=grid_spec,
        out_shape=jax.ShapeDtypeStruct((n_tok, D_MODEL), F32),
        compiler_params=_params(("arbitrary",)),
        name="moe",
    )(grp, nval, tok, x2e, g2, wg, wu, wd)


def _block_diag_ones(n, blk):
    i = jnp.arange(n) // blk
    return (i[:, None] == i[None, :]).astype(BF16)


def _layer(x, norm1_g, w_in, lam_re, lam_im, log_step, b_re, b_im, c_re, c_im, d_skip,
           w_glu, b_glu, g_q, g_kc, g_ks, g_kw, pos_k, pos_v, w_ck1, w_ck2, w_cv1, w_cv2,
           out_g_ssm, out_g_att, w_out, norm2_g, w_grp, b_grp, w_exp, b_exp, w_gate, w_up, w_down):
    bsz, seq, _ = x.shape
    assert seq % 512 == 0 and seq // SEL_BLOCK <= MAX_SEL_BLOCKS
    n_tok = bsz * seq
    x2d = x.reshape(n_tok, D_MODEL)
    q8 = SSM_Q
    n_sub = seq // q8
    nch = seq // CMP_STRIDE

    o_q = D_SSM
    o_kv = D_SSM + D_ATT
    o_gt = o_kv + 6 * D_KV
    kv = lambda i: w_in[:, o_kv + i * D_KV:o_kv + (i + 1) * D_KV]
    wrow = jnp.concatenate([w_in[:, :o_q], kv(0), kv(1), kv(2), kv(4)], axis=1).astype(BF16)
    per_g = HPG * N_BRANCH
    wgt = jnp.zeros((D_MODEL, N_KV * GATE_ROWS), F32)
    for g in range(N_KV):
        wgt = wgt.at[:, g * GATE_ROWS:g * GATE_ROWS + per_g].set(w_in[:, o_gt + g * per_g:o_gt + (g + 1) * per_g])
    wcol = jnp.concatenate([w_in[:, o_q:o_kv], kv(3), kv(5), wgt], axis=1).T.astype(BF16)
    qscale = (HEAD_DIM ** -0.5) * math.log2(math.e)
    gq = (jnp.tile(g_q.astype(F32), N_HEADS) * qscale).reshape(D_ATT, 1)
    gks = jnp.tile(g_ks.astype(F32), N_KV).reshape(1, D_KV)
    gkw = jnp.tile(g_kw.astype(F32), N_KV).reshape(1, D_KV)

    u, qt, kc, vc, ksa, kw, vst, vwt, gate = _in_proj(
        x2d, norm1_g.reshape(1, D_MODEL), wrow, wcol, gq, gks, gkw,
        _block_diag_ones(D_KV, HEAD_DIM), bsz=bsz, seq=seq)

    w_loc, t_in, m_st, pw_re, pw_im, dvec = _s5_weights(
        lam_re, lam_im, log_step, b_re, b_im, c_re, c_im, d_skip, n_sub)
    ys = _s5(u.reshape(bsz, seq, D_SSM), w_loc, t_in, m_st, pw_re, pw_im, dvec,
             bsz=bsz, n_sub=n_sub).reshape(n_tok, D_SSM)

    wide = CMP_STRIDE * HEAD_DIM
    pad8 = lambda p: jnp.zeros((8, 2 * wide), F32).at[0].set(p.reshape(-1)).astype(BF16)
    kcmp, vcmpt = _compress(
        kc, vc,
        w_ck1.astype(BF16), w_ck2.astype(BF16), w_cv1.astype(BF16), w_cv2.T.astype(BF16),
        pad8(pos_k), pad8(pos_v), g_kc.astype(F32).reshape(1, HEAD_DIM), bsz=bsz, nch=nch)
    cstart = jnp.arange(nch) * CMP_STRIDE
    sstart = jnp.arange(MAX_SEL_BLOCKS) * SEL_BLOCK
    ovlt = ((cstart[None, :] < sstart[:, None] + SEL_BLOCK) & (cstart[None, :] + CMP_BLOCK > sstart[:, None])
            & (jnp.arange(MAX_SEL_BLOCKS)[:, None] < seq // SEL_BLOCK)
            & (jnp.arange(nch)[None, :] < nch - 1)).astype(BF16)
    yat = _nsa(qt, kcmp, vcmpt, ksa, kw, vst, vwt, gate, ovlt, bsz=bsz, seq=seq, nch=nch)

    wr = jnp.zeros((D_MODEL, LANES), F32)
    wr = wr.at[:, :N_EXP_GROUPS].set(w_grp).at[:, ROUTER_OFF:ROUTER_OFF + N_EXPERTS].set(w_exp).astype(BF16)
    br = jnp.zeros((1, LANES), F32)
    br = br.at[0, :N_EXP_GROUPS].set(b_grp).at[0, ROUTER_OFF:ROUTER_OFF + N_EXPERTS].set(b_exp)
    g2 = norm2_g.reshape(1, D_MODEL).astype(F32)
    x2e = _out_proj(
        ys, yat, x2d, w_glu.astype(BF16), b_glu.reshape(1, D_SSM).astype(F32),
        out_g_ssm.reshape(1, D_SSM).astype(F32), out_g_att.reshape(D_ATT, 1).astype(F32),
        w_out.astype(BF16), g2, wr, br, seq=seq)

    grp, nval, tok = _moe_plan(x2e[:, D_MODEL].astype(jnp.int32), n_tok)
    gshape = (N_EXP_GROUPS, EXPERTS_PER_GROUP, D_MODEL, D_EXPERT)
    out = _moe(x2e, grp, nval, tok, g2, w_gate.astype(BF16).reshape(gshape), w_up.astype(BF16).reshape(gshape),
               w_down.astype(BF16).reshape(N_EXP_GROUPS, EXPERTS_PER_GROUP * D_EXPERT, D_MODEL))
    return out.reshape(bsz, seq, D_MODEL)


def kernel(x, norm1_g, w_in, lam_re, lam_im, log_step, b_re, b_im, c_re, c_im, d_skip, w_glu, b_glu, g_q, g_kc, g_ks, g_kw, pos_k, pos_v, w_ck1, w_ck2, w_cv1, w_cv2, out_g_ssm, out_g_att, w_out, norm2_g, w_grp, b_grp, w_exp, b_exp, w_gate, w_up, w_down):
    depth = norm1_g.shape[0]
    for l in range(depth):
        x = _layer(x, norm1_g[l], w_in[l], lam_re[l], lam_im[l], log_step[l], b_re[l], b_im[l], c_re[l],
                   c_im[l], d_skip[l], w_glu[l], b_glu[l], g_q[l], g_kc[l], g_ks[l], g_kw[l], pos_k[l],
                   pos_v[l], w_ck1[l], w_ck2[l], w_cv1[l], w_cv2[l], out_g_ssm[l], out_g_att[l], w_out[l],
                   norm2_g[l], w_grp[l], b_grp[l], w_exp[l], b_exp[l], w_gate[l], w_up[l], w_down[l])
    return x
```

```python
import functools
import math

import jax
import jax.numpy as jnp
from jax import lax
from jax.experimental import pallas as pl
from jax.experimental.pallas import tpu as pltpu

D_MODEL = 1024
D_SSM = 512
SSM_CH = 16
SSM_GROUPS = D_SSM // SSM_CH
SSM_STATE = 64
D_ATT = 512
HEAD_DIM = 64
N_HEADS = D_ATT // HEAD_DIM
N_KV = 2
HPG = N_HEADS // N_KV
D_KV = N_KV * HEAD_DIM
N_BRANCH = 3
CMP_STRIDE = 16
CMP_BLOCK = 2 * CMP_STRIDE
CMP_HIDDEN = 256
SEL_BLOCK = 64
N_SELECT = 16
WINDOW = 512
N_EXP_GROUPS = 4
EXPERTS_PER_GROUP = 8
N_EXPERTS = N_EXP_GROUPS * EXPERTS_PER_GROUP
D_EXPERT = 256
EPS = 1e-6
NEG = -1e30
FORCE = 1e9

LANES = 128
SSM_Q = 8
SSM_LT = D_SSM // LANES
ROUTER_OFF = N_EXP_GROUPS
NSA_TQ = 256
NSA_TK = 256
NSA_UNROLL = 4
V_ROWS = HEAD_DIM + 16
MAX_SEL_BLOCKS = 64
MOE_TM = 256
GATE_ROWS = 16
VMEM_LIMIT = 56 * 1024 * 1024

F32 = jnp.float32
BF16 = jnp.bfloat16


def _dot(a, b):
    return jnp.dot(a, b, preferred_element_type=F32)


def _dot_nt(a, b):
    return lax.dot_general(a, b, (((1,), (1,)), ((), ())), preferred_element_type=F32)


def _split_dot(x, w):
    hi = x.astype(BF16)
    lo = (x - hi.astype(F32)).astype(BF16)
    return _dot(hi, w) + _dot(lo, w)


def _gelu(x):
    c = math.sqrt(2.0 / math.pi)
    return 0.5 * x * (1.0 + jnp.tanh(c * (x + 0.044715 * (x * x * x))))


def _sigmoid(x):
    return 1.0 / (1.0 + jnp.exp(-x))


def _params(sem):
    return pltpu.CompilerParams(dimension_semantics=sem, vmem_limit_bytes=VMEM_LIMIT)


def _in_proj_kernel(x_ref, g1_ref, wrow_ref, wcol_ref, gq_ref, gks_ref, gkw_ref, bd128_ref,
                    u_ref, qt_ref, kc_ref, vc_ref, ksa_ref, kw_ref, vst_ref, vwt_ref, gate_ref, *, tm, nl):
    x = x_ref[...]
    ms = jnp.mean(x * x, axis=-1, keepdims=True)
    hn = (x * lax.rsqrt(ms + EPS) * g1_ref[...]).astype(BF16)

    pr = _dot(hn, wrow_ref[...])
    u_ref[...] = pr[:, :D_SSM]
    kc, vc, ks, kw = [pr[:, D_SSM + i * D_KV:D_SSM + (i + 1) * D_KV] for i in range(4)]
    kss = _split_dot(ks * ks, bd128_ref[...])
    ksn = ks * lax.rsqrt(kss * (1.0 / HEAD_DIM) + EPS) * gks_ref[...]
    kws = _split_dot(kw * kw, bd128_ref[...])
    kwn = kw * lax.rsqrt(kws * (1.0 / HEAD_DIM) + EPS) * gkw_ref[...]
    t0 = (pl.program_id(0) % nl) * tm
    tpos = t0 + lax.broadcasted_iota(jnp.int32, (tm, MAX_SEL_BLOCKS), 0)
    blk = lax.broadcasted_iota(jnp.int32, (tm, MAX_SEL_BLOCKS), 1)
    onehot = jnp.where(tpos // SEL_BLOCK == blk, 1.0, 0.0).astype(BF16)
    for g in range(N_KV):
        sl = slice(g * HEAD_DIM, (g + 1) * HEAD_DIM)
        kc_ref[g] = kc[:, sl]
        vc_ref[g] = vc[:, sl]
        ksa_ref[g] = jnp.concatenate([ksn[:, sl].astype(BF16), onehot], axis=1)
        kw_ref[g] = kwn[:, sl].astype(BF16)

    pc = _dot_nt(wcol_ref[...], hn)
    gq = gq_ref[...]
    for h in range(N_HEADS):
        sl = slice(h * HEAD_DIM, (h + 1) * HEAD_DIM)
        qh = pc[sl]
        ss = jnp.sum(qh * qh, axis=0, keepdims=True)
        qt_ref[h] = (qh * lax.rsqrt(ss * (1.0 / HEAD_DIM) + EPS) * gq[sl]).astype(BF16)
    ones_rows = jnp.where(lax.broadcasted_iota(jnp.int32, (V_ROWS - HEAD_DIM, tm), 0) == 0, 1.0, 0.0)
    for g in range(N_KV):
        for o_ref, base in ((vst_ref, D_ATT), (vwt_ref, D_ATT + D_KV)):
            vt = jnp.concatenate([pc[base + g * HEAD_DIM:base + (g + 1) * HEAD_DIM], ones_rows], axis=0)
            vt = vt.astype(BF16)
            for j in range(tm // NSA_TK):
                o_ref[g, j] = vt[:, j * NSA_TK:(j + 1) * NSA_TK]
        gb = D_ATT + 2 * D_KV + g * GATE_ROWS
        gate_ref[g] = _sigmoid(pc[gb:gb + GATE_ROWS])


def _in_proj(x2d, g1, wrow, wcol, gq, gks, gkw, bd128, *, bsz, seq):
    tm = 512
    nl = seq // tm
    n_tok = bsz * seq
    kern = functools.partial(_in_proj_kernel, tm=tm, nl=nl)
    row = lambda i: (i, 0)
    const = lambda i: (0, 0)
    bgl = lambda i: (i // nl, 0, i % nl, 0)
    n_col = wcol.shape[0]
    jt = tm // NSA_TK

    def kvspec(width):
        return pl.BlockSpec((None, N_KV, tm, width), bgl)

    def kvshape(width, dtype=BF16):
        return jax.ShapeDtypeStruct((bsz, N_KV, seq, width), dtype)

    vt_spec = pl.BlockSpec((None, N_KV, jt, V_ROWS, NSA_TK), lambda i: (i // nl, 0, i % nl, 0, 0))
    vt_shape = jax.ShapeDtypeStruct((bsz, N_KV, seq // NSA_TK, V_ROWS, NSA_TK), BF16)
    return pl.pallas_call(
        kern,
        grid=(n_tok // tm,),
        in_specs=[
            pl.BlockSpec((tm, D_MODEL), row),
            pl.BlockSpec((1, D_MODEL), const),
            pl.BlockSpec((D_MODEL, D_SSM + 4 * D_KV), const),
            pl.BlockSpec((n_col, D_MODEL), const),
            pl.BlockSpec((D_ATT, 1), const),
            pl.BlockSpec((1, D_KV), const),
            pl.BlockSpec((1, D_KV), const),
            pl.BlockSpec((D_KV, D_KV), const),
        ],
        out_specs=[
            pl.BlockSpec((tm, D_SSM), row),
            pl.BlockSpec((None, N_HEADS, HEAD_DIM, tm), lambda i: (i // nl, 0, 0, i % nl)),
            kvspec(HEAD_DIM), kvspec(HEAD_DIM), kvspec(2 * HEAD_DIM), kvspec(HEAD_DIM),
            vt_spec, vt_spec,
            pl.BlockSpec((None, N_KV, GATE_ROWS, tm), lambda i: (i // nl, 0, 0, i % nl)),
        ],
        out_shape=[
            jax.ShapeDtypeStruct((n_tok, D_SSM), F32),
            jax.ShapeDtypeStruct((bsz, N_HEADS, HEAD_DIM, seq), BF16),
            kvshape(HEAD_DIM, F32), kvshape(HEAD_DIM, F32), kvshape(2 * HEAD_DIM), kvshape(HEAD_DIM),
            vt_shape, vt_shape,
            jax.ShapeDtypeStruct((bsz, N_KV, GATE_ROWS, seq), F32),
        ],
        compiler_params=_params(("parallel",)),
        name="in_proj",
    )(x2d, g1, wrow, wcol, gq, gks, gkw, bd128)


def _s5_weights(lam_re, lam_im, log_step, b_re, b_im, c_re, c_im, d_skip, n_sub):
    q = SSM_Q
    lam = lax.complex(lam_re.astype(F32), lam_im.astype(F32))
    step = jnp.exp(log_step.astype(F32))[:, None]
    lam_bar = jnp.exp(lam * step)
    b_bar = ((lam_bar - 1.0) / lam)[..., None] * lax.complex(b_re.astype(F32), b_im.astype(F32))
    c = lax.complex(c_re.astype(F32), c_im.astype(F32))
    pows = [jnp.ones_like(lam_bar)]
    for _ in range(q):
        pows.append(pows[-1] * lam_bar)
    pw = jnp.stack(pows)
    lt, a8 = SSM_LT, LANES // SSM_CH
    hp = a8 * SSM_STATE
    e_lane = (jnp.arange(a8)[:, None] == jnp.arange(LANES)[None, :] // SSM_CH).astype(F32)
    e_state = (jnp.arange(a8)[:, None] == jnp.arange(hp)[None, :] // SSM_STATE).astype(F32)
    e_lane_t = jnp.tile(e_lane, (1, q))
    e_state_t = jnp.tile(e_state, (1, 2))

    kk = jnp.real(jnp.einsum('ghp,kgp,gpi->kghi', c, pw[:q], b_bar))
    km = kk.reshape(q, lt, a8, SSM_CH, SSM_CH).transpose(1, 4, 0, 2, 3).reshape(lt, SSM_CH, q, LANES)
    lag = jnp.arange(q)[None, :] - jnp.arange(q)[:, None]
    kg = km[:, :, jnp.clip(lag, 0, q - 1), :] * (lag >= 0)[None, None, :, :, None].astype(F32)
    kc = kg.transpose(0, 2, 1, 3, 4).reshape(lt, q, 1, SSM_CH, q * LANES)
    t_in = (kc * e_lane_t[None, None, :, None, :]).reshape(lt, q * LANES, q * LANES)

    wc = pw[q - 1 - jnp.arange(q)][..., None] * b_bar[None]
    wri = jnp.stack([jnp.real(wc), jnp.imag(wc)])
    wm = (wri.reshape(2, q, lt, a8, SSM_STATE, SSM_CH).transpose(2, 1, 5, 0, 3, 4)
          .reshape(lt, q, 1, SSM_CH, 2 * hp))
    w_loc = (wm * e_state_t[None, None, :, None, :]).reshape(lt, q * LANES, 2 * hp)

    cl = c[None] * pw[1:q + 1][:, :, None, :]
    cri = jnp.stack([jnp.real(cl), -jnp.imag(cl)])
    mm = (cri.reshape(2, q, lt, a8, SSM_CH, SSM_STATE).transpose(2, 0, 5, 1, 3, 4)
          .reshape(lt, 2, 1, SSM_STATE, q * LANES))
    m_st = (mm * e_lane_t[None, None, :, None, :]).reshape(lt, 2 * hp, q * LANES)

    n_lvl = max(1, (n_sub - 1).bit_length())
    lv = [pw[q]]
    for _ in range(n_lvl - 1):
        lv.append(lv[-1] * lv[-1])
    lvs = jnp.stack(lv).reshape(n_lvl, lt, 1, hp)
    pw_re = jnp.real(lvs).transpose(1, 0, 2, 3)
    pw_im = jnp.imag(lvs).transpose(1, 0, 2, 3)
    dvec = jnp.tile(d_skip.astype(F32).reshape(lt, 1, LANES), (1, 1, q))
    return w_loc.astype(BF16), t_in.astype(BF16), m_st.astype(BF16), pw_re, pw_im, dvec


def _s5_kernel(u_ref, w_ref, t_ref, m_ref, pwr_ref, pwi_ref, d_ref, y_ref, *, n_sub, n_lvl):
    half = (LANES // SSM_CH) * SSM_STATE
    q = SSM_Q
    u = jnp.concatenate([u_ref[pl.ds(s, n_sub, stride=q), :] for s in range(q)], axis=1)
    ub = u.astype(BF16)
    s_loc = _dot(ub, w_ref[...])
    re = s_loc[:, :half]
    im = s_loc[:, half:]
    rowi = lax.broadcasted_iota(jnp.int32, (n_sub, half), 0)
    for k in range(n_lvl):
        d = 1 << k
        ar = pwr_ref[k]
        ai = pwi_ref[k]
        keep = rowi >= d
        sre = jnp.where(keep, pltpu.roll(re, d, axis=0), 0.0)
        sim = jnp.where(keep, pltpu.roll(im, d, axis=0), 0.0)
        re, im = re + (ar * sre - ai * sim), im + (ar * sim + ai * sre)
    keep = rowi >= 1
    xre = jnp.where(keep, pltpu.roll(re, 1, axis=0), 0.0)
    xim = jnp.where(keep, pltpu.roll(im, 1, axis=0), 0.0)
    xst = jnp.concatenate([xre, xim], axis=1).astype(BF16)
    y = _dot(ub, t_ref[...]) + _dot(xst, m_ref[...]) + d_ref[...] * u
    for j in range(q):
        y_ref[pl.ds(j, n_sub, stride=q), :] = y[:, j * LANES:(j + 1) * LANES]


def _s5(u, w_loc, t_in, m_st, pw_re, pw_im, dvec, *, bsz, n_sub):
    q = SSM_Q
    n_lvl = pw_re.shape[1]
    kern = functools.partial(_s5_kernel, n_sub=n_sub, n_lvl=n_lvl)
    wide = q * LANES
    seq = n_sub * q
    return pl.pallas_call(
        kern,
        grid=(bsz, SSM_LT),
        in_specs=[
            pl.BlockSpec((None, seq, LANES), lambda b, l: (b, 0, l)),
            pl.BlockSpec((None, wide, wide), lambda b, l: (l, 0, 0)),
            pl.BlockSpec((None, wide, wide), lambda b, l: (l, 0, 0)),
            pl.BlockSpec((None, wide, wide), lambda b, l: (l, 0, 0)),
            pl.BlockSpec((None, n_lvl, 1, wide // 2), lambda b, l: (l, 0, 0, 0)),
            pl.BlockSpec((None, n_lvl, 1, wide // 2), lambda b, l: (l, 0, 0, 0)),
            pl.BlockSpec((None, 1, wide), lambda b, l: (l, 0, 0)),
        ],
        out_specs=pl.BlockSpec((None, seq, LANES), lambda b, l: (b, 0, l)),
        out_shape=jax.ShapeDtypeStruct((bsz, seq, D_SSM), F32),
        compiler_params=_params(("parallel", "parallel")),
        name="s5",
    )(u, w_loc, t_in, m_st, pw_re, pw_im, dvec)


def _compress_kernel(kc_ref, vc_ref, w1k_ref, w2k_ref, w1v_ref, w2vt_ref, posk_ref, posv_ref, gkc_ref,
                     kcmp_ref, vcmpt_ref, *, nch):
    half = CMP_STRIDE * HEAD_DIM

    def hidden(x_ref, w1_ref, pos_ref):
        x = jnp.concatenate([x_ref[pl.ds(j, nch, stride=CMP_STRIDE), :] for j in range(CMP_STRIDE)],
                            axis=1).astype(BF16)
        a = _dot(x, w1_ref[:half, :])
        b = _dot(x, w1_ref[half:, :])
        pv = _dot(pos_ref[...], w1_ref[...])[0:1, :]
        hid = a + pltpu.roll(b, nch - 1, axis=0) + pv
        return _gelu(hid).astype(BF16)

    k = _dot(hidden(kc_ref, w1k_ref, posk_ref), w2k_ref[...])
    ms = jnp.mean(k * k, axis=-1, keepdims=True)
    kcmp_ref[...] = (k * lax.rsqrt(ms + EPS) * gkc_ref[...]).astype(BF16)
    vt = _dot_nt(w2vt_ref[...], hidden(vc_ref, w1v_ref, posv_ref))
    coli = lax.broadcasted_iota(jnp.int32, vt.shape, 1)
    vcmpt_ref[...] = jnp.where(coli < nch - 1, vt, 0.0).astype(BF16)


def _compress(kcf, vcf, w1k, w2k, w1v, w2vt, posk, posv, gkc, *, bsz, nch):
    kern = functools.partial(_compress_kernel, nch=nch)
    wide = CMP_STRIDE * HEAD_DIM
    xspec = pl.BlockSpec((None, None, nch * CMP_STRIDE, HEAD_DIM), lambda b, g: (b, g, 0, 0))
    c2 = lambda b, g: (0, 0)
    return pl.pallas_call(
        kern,
        grid=(bsz, N_KV),
        in_specs=[
            xspec, xspec,
            pl.BlockSpec((2 * wide, CMP_HIDDEN), c2), pl.BlockSpec((CMP_HIDDEN, HEAD_DIM), c2),
            pl.BlockSpec((2 * wide, CMP_HIDDEN), c2), pl.BlockSpec((HEAD_DIM, CMP_HIDDEN), c2),
            pl.BlockSpec((8, 2 * wide), c2), pl.BlockSpec((8, 2 * wide), c2),
            pl.BlockSpec((1, HEAD_DIM), c2),
        ],
        out_specs=[pl.BlockSpec((None, None, nch, HEAD_DIM), lambda b, g: (b, g, 0, 0)),
                   pl.BlockSpec((None, None, HEAD_DIM, nch), lambda b, g: (b, g, 0, 0))],
        out_shape=[jax.ShapeDtypeStruct((bsz, N_KV, nch, HEAD_DIM), BF16),
                   jax.ShapeDtypeStruct((bsz, N_KV, HEAD_DIM, nch), BF16)],
        compiler_params=_params(("parallel", "parallel")),
        name="compress",
    )(kcf, vcf, w1k, w2k, w1v, w2vt, posk, posv, gkc)


def _nsa_kernel(qt_ref, kcmp_ref, vcmpt_ref, ksa_ref, kw_ref, vst_ref, vwt_ref, gate_ref, ovlt_ref, o_ref,
                accs_ref, accw_ref, *, tq, tk, nch):
    qi = pl.program_id(2)
    q0 = qi * tq
    rows = HPG * tq
    qt = jnp.concatenate([qt_ref[h] for h in range(HPG)], axis=1)
    tpos = q0 + lax.broadcasted_iota(jnp.int32, (1, rows), 1) % tq
    krow = lax.broadcasted_iota(jnp.int32, (tk, rows), 0)
    m0 = jnp.full((1, rows), NEG, F32)

    def run_tiles(k_ref, q_all, vt_ref, acc_ref, m, tiles):
        scores = []
        for kt, kind in tiles:
            kc = jnp.maximum(kt, 0) if kind in ('band', 'valid') else kt
            s = _dot(k_ref[pl.ds(pl.multiple_of(kc * tk, tk), tk), :], q_all)
            if kind == 'causal':
                s = jnp.where(kt * tk + krow <= tpos, s, NEG)
            elif kind == 'band':
                s = jnp.where((kt * tk + krow > tpos - WINDOW) & (kt >= 0), s, NEG)
            elif kind == 'valid':
                s = jnp.where(kt >= 0, s, NEG)
            scores.append((kc, s))
        for kc, s in scores:
            m_new = jnp.maximum(m, jnp.max(s, axis=0, keepdims=True))
            alpha = jnp.exp2(m - m_new)
            p = jnp.exp2(s - m_new).astype(BF16)
            acc_ref[...] = alpha * acc_ref[...] + _dot(vt_ref[kc], p)
            m = m_new
        return m

    accw_ref[...] = jnp.zeros_like(accw_ref)
    n_win = WINDOW // tk
    run_tiles(kw_ref, qt, vwt_ref, accw_ref, m0,
              [(qi - n_win, 'band')] + [(qi - n_win + t, 'valid') for t in range(1, n_win)] + [(qi, 'causal')])

    s = _dot(kcmp_ref[...], qt)
    cend = lax.broadcasted_iota(jnp.int32, (nch, rows), 0) * CMP_STRIDE + (CMP_BLOCK - 1)
    s = jnp.where(cend <= tpos, s, NEG)
    m = jnp.max(s, axis=0, keepdims=True)
    p = jnp.exp2(s - m)
    p = p * (1.0 / jnp.sum(p, axis=0, keepdims=True))
    p = jnp.where(tpos >= CMP_BLOCK - 1, p, 0.0)
    o_cmp = _dot(vcmpt_ref[...], p.astype(BF16))

    psum = p[:, 0:tq]
    for h in range(1, HPG):
        psum = psum + p[:, h * tq:(h + 1) * tq]
    hi = psum.astype(BF16)
    lo = (psum - hi.astype(F32)).astype(BF16)
    ovlt = ovlt_ref[...]
    imp = _dot(ovlt, hi) + _dot(ovlt, lo)
    nb = MAX_SEL_BLOCKS
    blk = lax.broadcasted_iota(jnp.int32, (nb, tq), 0)
    cur = (q0 + lax.broadcasted_iota(jnp.int32, (nb, tq), 1)) // SEL_BLOCK
    forced = (blk == 0) | (blk == cur) | (blk == cur - 1)
    imp = jnp.where(forced, FORCE, jnp.where(blk <= cur, imp, NEG))
    sub = 8
    groups = [imp[r:r + sub] for r in range(0, nb, sub)]
    ranks = [jnp.zeros((sub, tq), F32) for _ in groups]
    rowl = lax.broadcasted_iota(jnp.int32, (sub, tq), 0)
    for i in range(nb):
        ri = jnp.broadcast_to(imp[i:i + 1, :], (sub, tq))
        for gi, x in enumerate(groups):
            if i < gi * sub:
                ahead = ri >= x
            elif i >= (gi + 1) * sub:
                ahead = ri > x
            else:
                ahead = (ri > x) | ((ri == x) & (rowl > i - gi * sub))
            ranks[gi] = ranks[gi] + jnp.where(ahead, 1.0, 0.0)
    rank = jnp.concatenate(ranks, axis=0)
    sel = jnp.where(rank < N_SELECT, 0.0, NEG).astype(BF16)
    qa = jnp.concatenate([qt, jnp.concatenate([sel] * HPG, axis=1)], axis=0)

    accs_ref[...] = jnp.zeros_like(accs_ref)
    nu = NSA_UNROLL

    def sel_multi(j, m):
        return run_tiles(ksa_ref, qa, vst_ref, accs_ref, m, [(nu * j + t, None) for t in range(nu)])

    m_sel = lax.fori_loop(0, qi // nu, sel_multi, m0)
    for r in range(nu):
        @pl.when(qi % nu == r)
        def _(r=r):
            base = qi - r
            run_tiles(ksa_ref, qa, vst_ref, accs_ref, m_sel,
                      [(base + t, None) for t in range(r)] + [(qi, 'causal')])

    accs = accs_ref[...]
    accw = accw_ref[...]
    o_sel = accs[:HEAD_DIM] * (1.0 / accs[HEAD_DIM:HEAD_DIM + 1])
    o_win = accw[:HEAD_DIM] * (1.0 / accw[HEAD_DIM:HEAD_DIM + 1])
    gt = gate_ref[...]
    for h in range(HPG):
        c = slice(h * tq, (h + 1) * tq)
        o_ref[h * HEAD_DIM:(h + 1) * HEAD_DIM, :] = (
            gt[3 * h:3 * h + 1] * o_cmp[:, c] + gt[3 * h + 1:3 * h + 2] * o_sel[:, c]
            + gt[3 * h + 2:3 * h + 3] * o_win[:, c])


def _nsa(qt, kcmp, vcmpt, ksa, kw, vst, vwt, gate, ovlt, *, bsz, seq, nch):
    tq, tk = NSA_TQ, NSA_TK
    assert tq == tk and WINDOW % tk == 0
    nq = seq // tq
    kern = functools.partial(_nsa_kernel, tq=tq, tk=tk, nch=nch)
    full = lambda b, g, i: (b, g, 0, 0)
    full5 = lambda b, g, i: (b, g, 0, 0, 0)
    qd = HPG * HEAD_DIM
    rows = HPG * tq
    return pl.pallas_call(
        kern,
        grid=(bsz, N_KV, nq),
        in_specs=[
            pl.BlockSpec((None, HPG, HEAD_DIM, tq), lambda b, g, i: (b, g, 0, i)),
            pl.BlockSpec((None, None, nch, HEAD_DIM), full),
            pl.BlockSpec((None, None, HEAD_DIM, nch), full),
            pl.BlockSpec((None, None, seq, 2 * HEAD_DIM), full),
            pl.BlockSpec((None, None, seq, HEAD_DIM), full),
            pl.BlockSpec((None, None, seq // tk, V_ROWS, tk), full5),
            pl.BlockSpec((None, None, seq // tk, V_ROWS, tk), full5),
            pl.BlockSpec((None, None, GATE_ROWS, tq), lambda b, g, i: (b, g, 0, i)),
            pl.BlockSpec((MAX_SEL_BLOCKS, nch), lambda b, g, i: (0, 0)),
        ],
        out_specs=pl.BlockSpec((None, qd, tq), lambda b, g, i: (b, g, i)),
        out_shape=jax.ShapeDtypeStruct((bsz, D_ATT, seq), F32),
        scratch_shapes=[pltpu.VMEM((V_ROWS, rows), F32), pltpu.VMEM((V_ROWS, rows), F32)],
        compiler_params=_params(("parallel", "parallel", "arbitrary")),
        name="nsa",
    )(qt, kcmp, vcmpt, ksa, kw, vst, vwt, gate, ovlt)


def _out_proj_kernel(ys_ref, yat_ref, x_ref, wglu_ref, bglu_ref, gs_ref, ga_ref, wo_ref, g2_ref, wr_ref, br_ref,
                     x2e_ref):
    y = _gelu(ys_ref[...])
    y = y * _sigmoid(_dot(y.astype(BF16), wglu_ref[...]) + bglu_ref[...])
    ysn = y * lax.rsqrt(jnp.mean(y * y, axis=-1, keepdims=True) + EPS) * gs_ref[...]
    yat = yat_ref[...]
    yant = yat * lax.rsqrt(jnp.mean(yat * yat, axis=0, keepdims=True) + EPS) * ga_ref[...]
    yan = yant.T
    x2 = x_ref[...] + _dot(ysn.astype(BF16), wo_ref[:D_SSM, :]) + _dot(yan.astype(BF16), wo_ref[D_SSM:, :])
    x2e_ref[:, :D_MODEL] = x2
    h2 = (x2 * lax.rsqrt(jnp.mean(x2 * x2, axis=-1, keepdims=True) + EPS) * g2_ref[...]).astype(BF16)

    logits = _dot(h2, wr_ref[...]) + br_ref[...]
    lane = lax.broadcasted_iota(jnp.int32, logits.shape, 1).astype(F32)
    far = float(LANES)
    is_g = lane < N_EXP_GROUPS
    glog = jnp.where(is_g, logits, -jnp.inf)
    gmax = jnp.max(glog, axis=1, keepdims=True)
    gsum = jnp.sum(jnp.where(is_g, jnp.exp(logits - gmax), 0.0), axis=1, keepdims=True)
    gsel = jnp.min(jnp.where(glog == gmax, lane, far), axis=1, keepdims=True)
    gprob = 1.0 / gsum
    lo = ROUTER_OFF + EXPERTS_PER_GROUP * gsel
    in_e = (lane >= lo) & (lane < lo + EXPERTS_PER_GROUP)
    emax = jnp.max(jnp.where(in_e, logits, -jnp.inf), axis=1, keepdims=True)
    eexp = jnp.where(in_e, jnp.exp(logits - emax), 0.0)
    eprob = jnp.where(in_e, eexp / jnp.sum(eexp, axis=1, keepdims=True), -1.0)
    v1 = jnp.max(eprob, axis=1, keepdims=True)
    i1 = jnp.min(jnp.where(eprob == v1, lane, far), axis=1, keepdims=True)
    rest = jnp.where(lane == i1, -1.0, eprob)
    v2 = jnp.max(rest, axis=1, keepdims=True)
    i2 = jnp.min(jnp.where(rest == v2, lane, far), axis=1, keepdims=True)
    den = v1 + v2
    x2e_ref[:, D_MODEL:] = (jnp.where(lane == i1, v1 / den * gprob, 0.0)
                            + jnp.where(lane == i2, v2 / den * gprob, 0.0)
                            + jnp.where(lane == 0.0, gsel, 0.0))


def _out_proj(ys, yat, x2d, wglu, bglu, gs, ga, wo, g2, wr, br, *, seq):
    n_tok = x2d.shape[0]
    tm = 512
    nl = seq // tm
    row = lambda i: (i, 0)
    const = lambda i: (0, 0)
    return pl.pallas_call(
        _out_proj_kernel,
        grid=(n_tok // tm,),
        in_specs=[
            pl.BlockSpec((tm, D_SSM), row),
            pl.BlockSpec((None, D_ATT, tm), lambda i: (i // nl, 0, i % nl)),
            pl.BlockSpec((tm, D_MODEL), row),
            pl.BlockSpec((D_SSM, D_SSM), const),
            pl.BlockSpec((1, D_SSM), const),
            pl.BlockSpec((1, D_SSM), const),
            pl.BlockSpec((D_ATT, 1), const),
            pl.BlockSpec((D_SSM + D_ATT, D_MODEL), const),
            pl.BlockSpec((1, D_MODEL), const),
            pl.BlockSpec((D_MODEL, LANES), const),
            pl.BlockSpec((1, LANES), const),
        ],
        out_specs=pl.BlockSpec((tm, D_MODEL + LANES), row),
        out_shape=jax.ShapeDtypeStruct((n_tok, D_MODEL + LANES), F32),
        compiler_params=_params(("parallel",)),
        name="out_proj",
    )(ys, yat, x2d, wglu, bglu, gs, ga, wo, g2, wr, br)


def _moe_plan(gsel, n_tok):
    tmx = MOE_TM
    n_tiles = n_tok // tmx + N_EXP_GROUPS
    oh = (gsel[:, None] == jnp.arange(N_EXP_GROUPS)[None, :]).astype(jnp.int32)
    csum = jnp.cumsum(oh, axis=0)
    counts = csum[-1]
    rank = jnp.sum(csum * oh, axis=1) - 1
    nt = (counts + tmx - 1) // tmx
    tend = jnp.cumsum(nt)
    toff = tend - nt
    pos = jnp.sum(oh * toff[None, :], axis=1) * tmx + rank
    tok = jnp.zeros((n_tiles * tmx,), jnp.int32).at[pos].set(jnp.arange(n_tok, dtype=jnp.int32))
    tile = jnp.arange(n_tiles)
    grp = jnp.minimum(jnp.sum((tile[:, None] >= tend[None, :]).astype(jnp.int32), axis=1), N_EXP_GROUPS - 1)
    nval = jnp.clip(counts[grp] - (tile - toff[grp]) * tmx, 0, tmx)
    return grp.astype(jnp.int32), nval.astype(jnp.int32), tok


def _moe_kernel(grp_ref, nval_ref, tok_ref, x_hbm, g2_ref, wg_ref, wu_ref, wd_ref, o_hbm,
                xbuf, obuf, abuf, gsem, ssem, *, tmx, n_tiles):
    i = pl.program_id(0)
    slot = i % 2

    def gather_row(tile, r, dst_slot):
        t = tok_ref[tile * tmx + r]
        return pltpu.make_async_copy(x_hbm.at[pl.ds(t, 1), :], xbuf.at[dst_slot, pl.ds(r, 1), :], gsem.at[dst_slot])

    def scatter_row(tile, r, src_slot):
        t = tok_ref[tile * tmx + r]
        return pltpu.make_async_copy(obuf.at[src_slot, pl.ds(r, 1), :], o_hbm.at[pl.ds(t, 1), :], ssem.at[src_slot])

    def gather_wait(src_slot):
        pltpu.make_async_copy(x_hbm.at[pl.ds(0, tmx), :], xbuf.at[src_slot], gsem.at[src_slot]).wait()

    def scatter_wait(n, src_slot):
        n8 = pl.multiple_of((n // 8) * 8, 8)

        @pl.when(n8 > 0)
        def _():
            pltpu.make_async_copy(obuf.at[src_slot, pl.ds(0, n8), :], o_hbm.at[pl.ds(0, n8), :],
                                  ssem.at[src_slot]).wait()

        def one(r, c):
            pltpu.make_async_copy(obuf.at[src_slot, pl.ds(0, 1), :], o_hbm.at[pl.ds(0, 1), :],
                                  ssem.at[src_slot]).wait()
            return c

        lax.fori_loop(0, n - n8, one, 0)

    @pl.when(i == 0)
    def _():
        def body(r, c):
            gather_row(0, r, 0).start()
            return c

        lax.fori_loop(0, tmx, body, 0)

    gather_wait(slot)

    @pl.when(i >= 2)
    def _():
        scatter_wait(nval_ref[i - 2], slot)

    nxt = jnp.minimum(i + 1, n_tiles - 1)
    per = tmx // EXPERTS_PER_GROUP
    prev_full = (i >= 1) & (nval_ref[jnp.maximum(i - 1, 0)] == tmx)

    def experts(with_scatter):
        xe = xbuf[slot]
        x2 = xe[:, :D_MODEL]
        cw = xe[:, D_MODEL:]
        h = (x2 * lax.rsqrt(jnp.mean(x2 * x2, axis=-1, keepdims=True) + EPS) * g2_ref[...]).astype(BF16)
        lane = lax.broadcasted_iota(jnp.int32, cw.shape, 1)
        first = ROUTER_OFF + EXPERTS_PER_GROUP * grp_ref[i]
        for k in range(EXPERTS_PER_GROUP):
            for r in range(k * per, (k + 1) * per):
                gather_row(nxt, r, 1 - slot).start()
                if with_scatter:
                    scatter_row(i - 1, r, 1 - slot).start()
            gate = _dot(h, wg_ref[k])
            up = _dot(h, wu_ref[k])
            ck = jnp.sum(jnp.where(lane == first + k, cw, 0.0), axis=1, keepdims=True)
            abuf[:, k * D_EXPERT:(k + 1) * D_EXPERT] = (gate * _sigmoid(gate) * up * ck).astype(BF16)
        obuf[slot] = x2 + _dot(abuf[...], wd_ref[...])

    @pl.when(prev_full)
    def _():
        experts(True)

    @pl.when(jnp.logical_not(prev_full))
    def _():
        experts(False)

    nv = nval_ref[i]

    @pl.when(nv < tmx)
    def _():
        def body(r, c):
            scatter_row(i, r, slot).start()
            return c

        lax.fori_loop(0, nv, body, 0)

    @pl.when(i == n_tiles - 1)
    def _():
        @pl.when(nv == tmx)
        def _():
            def body(r, c):
                scatter_row(i, r, slot).start()
                return c

            lax.fori_loop(0, tmx, body, 0)

        gather_wait(1 - slot)
        scatter_wait(nval_ref[i - 1], 1 - slot)
        scatter_wait(nv, slot)


def _moe(x2e, grp, nval, tok, g2, wg, wu, wd):
    n_tok = x2e.shape[0]
    tmx = MOE_TM
    n_tiles = grp.shape[0]
    kern = functools.partial(_moe_kernel, tmx=tmx, n_tiles=n_tiles)
    gk = EXPERTS_PER_GROUP * D_EXPERT
    grid_spec = pltpu.PrefetchScalarGridSpec(
        num_scalar_prefetch=3,
        grid=(n_tiles,),
        in_specs=[
            pl.BlockSpec(memory_space=pl.ANY),
            pl.BlockSpec((1, D_MODEL), lambda i, g, n, t: (0, 0)),
            pl.BlockSpec((None, EXPERTS_PER_GROUP, D_MODEL, D_EXPERT), lambda i, g, n, t: (g[i], 0, 0, 0)),
            pl.BlockSpec((None, EXPERTS_PER_GROUP, D_MODEL, D_EXPERT), lambda i, g, n, t: (g[i], 0, 0, 0)),
            pl.BlockSpec((None, gk, D_MODEL), lambda i, g, n, t: (g[i], 0, 0)),
        ],
        out_specs=pl.BlockSpec(memory_space=pl.ANY),
        scratch_shapes=[
            pltpu.VMEM((2, tmx, D_MODEL + LANES), F32),
            pltpu.VMEM((2, tmx, D_MODEL), F32),
            pltpu.VMEM((tmx, gk), BF16),
            pltpu.SemaphoreType.DMA((2,)),
            pltpu.SemaphoreType.DMA((2,)),
        ],
    )
    return pl.pallas_call(
        kern,
        grid_spec=grid_spec,
        out_shape=jax.ShapeDtypeStruct((n_tok, D_MODEL), F32),
        compiler_params=_params(("arbitrary",)),
        name="moe",
    )(grp, nval, tok, x2e, g2, wg, wu, wd)


def _block_diag_ones(n, blk):
    i = jnp.arange(n) // blk
    return (i[:, None] == i[None, :]).astype(BF16)


def _layer(x, norm1_g, w_in, lam_re, lam_im, log_step, b_re, b_im, c_re, c_im, d_skip,
           w_glu, b_glu, g_q, g_kc, g_ks, g_kw, pos_k, pos_v, w_ck1, w_ck2, w_cv1, w_cv2,
           out_g_ssm, out_g_att, w_out, norm2_g, w_grp, b_grp, w_exp, b_exp, w_gate, w_up, w_down):
    bsz, seq, _ = x.shape
    assert seq % 512 == 0 and seq // SEL_BLOCK <= MAX_SEL_BLOCKS
    n_tok = bsz * seq
    x2d = x.reshape(n_tok, D_MODEL)
    q8 = SSM_Q
    n_sub = seq // q8
    nch = seq // CMP_STRIDE

    o_q = D_SSM
    o_kv = D_SSM + D_ATT
    o_gt = o_kv + 6 * D_KV
    kv = lambda i: w_in[:, o_kv + i * D_KV:o_kv + (i + 1) * D_KV]
    wrow = jnp.concatenate([w_in[:, :o_q], kv(0), kv(1), kv(2), kv(4)], axis=1).astype(BF16)
    per_g = HPG * N_BRANCH
    wgt = jnp.zeros((D_MODEL, N_KV * GATE_ROWS), F32)
    for g in range(N_KV):
        wgt = wgt.at[:, g * GATE_ROWS:g * GATE_ROWS + per_g].set(w_in[:, o_gt + g * per_g:o_gt + (g + 1) * per_g])
    wcol = jnp.concatenate([w_in[:, o_q:o_kv], kv(3), kv(5), wgt], axis=1).T.astype(BF16)
    qscale = (HEAD_DIM ** -0.5) * math.log2(math.e)
    gq = (jnp.tile(g_q.astype(F32), N_HEADS) * qscale).reshape(D_ATT, 1)
    gks = jnp.tile(g_ks.astype(F32), N_KV).reshape(1, D_KV)
    gkw = jnp.tile(g_kw.astype(F32), N_KV).reshape(1, D_KV)

    u, qt, kc, vc, ksa, kw, vst, vwt, gate = _in_proj(
        x2d, norm1_g.reshape(1, D_MODEL), wrow, wcol, gq, gks, gkw,
        _block_diag_ones(D_KV, HEAD_DIM), bsz=bsz, seq=seq)

    w_loc, t_in, m_st, pw_re, pw_im, dvec = _s5_weights(
        lam_re, lam_im, log_step, b_re, b_im, c_re, c_im, d_skip, n_sub)
    ys = _s5(u.reshape(bsz, seq, D_SSM), w_loc, t_in, m_st, pw_re, pw_im, dvec,
             bsz=bsz, n_sub=n_sub).reshape(n_tok, D_SSM)

    wide = CMP_STRIDE * HEAD_DIM
    pad8 = lambda p: jnp.zeros((8, 2 * wide), F32).at[0].set(p.reshape(-1)).astype(BF16)
    kcmp, vcmpt = _compress(
        kc, vc,
        w_ck1.astype(BF16), w_ck2.astype(BF16), w_cv1.astype(BF16), w_cv2.T.astype(BF16),
        pad8(pos_k), pad8(pos_v), g_kc.astype(F32).reshape(1, HEAD_DIM), bsz=bsz, nch=nch)
    cstart = jnp.arange(nch) * CMP_STRIDE
    sstart = jnp.arange(MAX_SEL_BLOCKS) * SEL_BLOCK
    ovlt = ((cstart[None, :] < sstart[:, None] + SEL_BLOCK) & (cstart[None, :] + CMP_BLOCK > sstart[:, None])
            & (jnp.arange(MAX_SEL_BLOCKS)[:, None] < seq // SEL_BLOCK)
            & (jnp.arange(nch)[None, :] < nch - 1)).astype(BF16)
    yat = _nsa(qt, kcmp, vcmpt, ksa, kw, vst, vwt, gate, ovlt, bsz=bsz, seq=seq, nch=nch)

    wr = jnp.zeros((D_MODEL, LANES), F32)
    wr = wr.at[:, :N_EXP_GROUPS].set(w_grp).at[:, ROUTER_OFF:ROUTER_OFF + N_EXPERTS].set(w_exp).astype(BF16)
    br = jnp.zeros((1, LANES), F32)
    br = br.at[0, :N_EXP_GROUPS].set(b_grp).at[0, ROUTER_OFF:ROUTER_OFF + N_EXPERTS].set(b_exp)
    g2 = norm2_g.reshape(1, D_MODEL).astype(F32)
    x2e = _out_proj(
        ys, yat, x2d, w_glu.astype(BF16), b_glu.reshape(1, D_SSM).astype(F32),
        out_g_ssm.reshape(1, D_SSM).astype(F32), out_g_att.reshape(D_ATT, 1).astype(F32),
        w_out.astype(BF16), g2, wr, br, seq=seq)

    grp, nval, tok = _moe_plan(x2e[:, D_MODEL].astype(jnp.int32), n_tok)
    gshape = (N_EXP_GROUPS, EXPERTS_PER_GROUP, D_MODEL, D_EXPERT)
    out = _moe(x2e, grp, nval, tok, g2, w_gate.astype(BF16).reshape(gshape), w_up.astype(BF16).reshape(gshape),
               w_down.astype(BF16).reshape(N_EXP_GROUPS, EXPERTS_PER_GROUP * D_EXPERT, D_MODEL))
    return out.reshape(bsz, seq, D_MODEL)


def kernel(x, norm1_g, w_in, lam_re, lam_im, log_step, b_re, b_im, c_re, c_im, d_skip, w_glu, b_glu, g_q, g_kc, g_ks, g_kw, pos_k, pos_v, w_ck1, w_ck2, w_cv1, w_cv2, out_g_ssm, out_g_att, w_out, norm2_g, w_grp, b_grp, w_exp, b_exp, w_gate, w_up, w_down):
    depth = norm1_g.shape[0]
    for l in range(depth):
        x = _layer(x, norm1_g[l], w_in[l], lam_re[l], lam_im[l], log_step[l], b_re[l], b_im[l], c_re[l],
                   c_im[l], d_skip[l], w_glu[l], b_glu[l], g_q[l], g_kc[l], g_ks[l], g_kw[l], pos_k[l],
                   pos_v[l], w_ck1[l], w_ck2[l], w_cv1[l], w_cv2[l], out_g_ssm[l], out_g_att[l], w_out[l],
                   norm2_g[l], w_grp[l], b_grp[l], w_exp[l], b_exp[l], w_gate[l], w_up[l], w_down[l])
    return x
```

```python
import functools
import math

import jax
import jax.numpy as jnp
from jax import lax
from jax.experimental import pallas as pl
from jax.experimental.pallas import tpu as pltpu

D_MODEL = 1024
D_SSM = 512
SSM_CH = 16
SSM_GROUPS = D_SSM // SSM_CH
SSM_STATE = 64
D_ATT = 512
HEAD_DIM = 64
N_HEADS = D_ATT // HEAD_DIM
N_KV = 2
HPG = N_HEADS // N_KV
D_KV = N_KV * HEAD_DIM
N_BRANCH = 3
CMP_STRIDE = 16
CMP_BLOCK = 2 * CMP_STRIDE
CMP_HIDDEN = 256
SEL_BLOCK = 64
N_SELECT = 16
WINDOW = 512
N_EXP_GROUPS = 4
EXPERTS_PER_GROUP = 8
N_EXPERTS = N_EXP_GROUPS * EXPERTS_PER_GROUP
D_EXPERT = 256
EPS = 1e-6
NEG = -1e30
FORCE = 1e9

LANES = 128
SSM_Q = 8
SSM_LT = D_SSM // LANES
ROUTER_OFF = N_EXP_GROUPS
NSA_TQ = 256
NSA_TK = 256
NSA_UNROLL = 4
V_ROWS = HEAD_DIM + 16
MAX_SEL_BLOCKS = 64
MOE_TM = 256
GATE_ROWS = 16
VMEM_LIMIT = 56 * 1024 * 1024

F32 = jnp.float32
BF16 = jnp.bfloat16


def _dot(a, b):
    return jnp.dot(a, b, preferred_element_type=F32)


def _dot_nt(a, b):
    return lax.dot_general(a, b, (((1,), (1,)), ((), ())), preferred_element_type=F32)


def _split_dot(x, w):
    hi = x.astype(BF16)
    lo = (x - hi.astype(F32)).astype(BF16)
    return _dot(hi, w) + _dot(lo, w)


def _gelu(x):
    c = math.sqrt(2.0 / math.pi)
    return 0.5 * x * (1.0 + jnp.tanh(c * (x + 0.044715 * (x * x * x))))


def _sigmoid(x):
    return 1.0 / (1.0 + jnp.exp(-x))


def _params(sem):
    return pltpu.CompilerParams(dimension_semantics=sem, vmem_limit_bytes=VMEM_LIMIT)


def _in_proj_kernel(x_ref, g1_ref, wrow_ref, wcol_ref, gq_ref, gks_ref, gkw_ref, bd128_ref,
                    u_ref, qt_ref, kc_ref, vc_ref, ksa_ref, kw_ref, vst_ref, vwt_ref, gate_ref, *, tm, nl):
    x = x_ref[...]
    ms = jnp.mean(x * x, axis=-1, keepdims=True)
    hn = (x * lax.rsqrt(ms + EPS) * g1_ref[...]).astype(BF16)

    pr = _dot(hn, wrow_ref[...])
    u_ref[...] = pr[:, :D_SSM]
    kc, vc, ks, kw = [pr[:, D_SSM + i * D_KV:D_SSM + (i + 1) * D_KV] for i in range(4)]
    kss = _split_dot(ks * ks, bd128_ref[...])
    ksn = ks * lax.rsqrt(kss * (1.0 / HEAD_DIM) + EPS) * gks_ref[...]
    kws = _split_dot(kw * kw, bd128_ref[...])
    kwn = kw * lax.rsqrt(kws * (1.0 / HEAD_DIM) + EPS) * gkw_ref[...]
    t0 = (pl.program_id(0) % nl) * tm
    tpos = t0 + lax.broadcasted_iota(jnp.int32, (tm, MAX_SEL_BLOCKS), 0)
    blk = lax.broadcasted_iota(jnp.int32, (tm, MAX_SEL_BLOCKS), 1)
    onehot = jnp.where(tpos // SEL_BLOCK == blk, 1.0, 0.0).astype(BF16)
    for g in range(N_KV):
        sl = slice(g * HEAD_DIM, (g + 1) * HEAD_DIM)
        kc_ref[g] = kc[:, sl]
        vc_ref[g] = vc[:, sl]
        ksa_ref[g] = jnp.concatenate([ksn[:, sl].astype(BF16), onehot], axis=1)
        kw_ref[g] = kwn[:, sl].astype(BF16)

    pc = _dot_nt(wcol_ref[...], hn)
    gq = gq_ref[...]
    for h in range(N_HEADS):
        sl = slice(h * HEAD_DIM, (h + 1) * HEAD_DIM)
        qh = pc[sl]
        ss = jnp.sum(qh * qh, axis=0, keepdims=True)
        qt_ref[h] = (qh * lax.rsqrt(ss * (1.0 / HEAD_DIM) + EPS) * gq[sl]).astype(BF16)
    ones_rows = jnp.where(lax.broadcasted_iota(jnp.int32, (V_ROWS - HEAD_DIM, tm), 0) == 0, 1.0, 0.0)
    for g in range(N_KV):
        for o_ref, base in ((vst_ref, D_ATT), (vwt_ref, D_ATT + D_KV)):
            vt = jnp.concatenate([pc[base + g * HEAD_DIM:base + (g + 1) * HEAD_DIM], ones_rows], axis=0)
            vt = vt.astype(BF16)
            for j in range(tm // NSA_TK):
                o_ref[g, j] = vt[:, j * NSA_TK:(j + 1) * NSA_TK]
        gb = D_ATT + 2 * D_KV + g * GATE_ROWS
        gate_ref[g] = _sigmoid(pc[gb:gb + GATE_ROWS])


def _in_proj(x2d, g1, wrow, wcol, gq, gks, gkw, bd128, *, bsz, seq):
    tm = 512
    nl = seq // tm
    n_tok = bsz * seq
    kern = functools.partial(_in_proj_kernel, tm=tm, nl=nl)
    row = lambda i: (i, 0)
    const = lambda i: (0, 0)
    bgl = lambda i: (i // nl, 0, i % nl, 0)
    n_col = wcol.shape[0]
    jt = tm // NSA_TK

    def kvspec(width):
        return pl.BlockSpec((None, N_KV, tm, width), bgl)

    def kvshape(width, dtype=BF16):
        return jax.ShapeDtypeStruct((bsz, N_KV, seq, width), dtype)

    vt_spec = pl.BlockSpec((None, N_KV, jt, V_ROWS, NSA_TK), lambda i: (i // nl, 0, i % nl, 0, 0))
    vt_shape = jax.ShapeDtypeStruct((bsz, N_KV, seq // NSA_TK, V_ROWS, NSA_TK), BF16)
    return pl.pallas_call(
        kern,
        grid=(n_tok // tm,),
        in_specs=[
            pl.BlockSpec((tm, D_MODEL), row),
            pl.BlockSpec((1, D_MODEL), const),
            pl.BlockSpec((D_MODEL, D_SSM + 4 * D_KV), const),
            pl.BlockSpec((n_col, D_MODEL), const),
            pl.BlockSpec((D_ATT, 1), const),
            pl.BlockSpec((1, D_KV), const),
            pl.BlockSpec((1, D_KV), const),
            pl.BlockSpec((D_KV, D_KV), const),
        ],
        out_specs=[
            pl.BlockSpec((tm, D_SSM), row),
            pl.BlockSpec((None, N_HEADS, HEAD_DIM, tm), lambda i: (i // nl, 0, 0, i % nl)),
            kvspec(HEAD_DIM), kvspec(HEAD_DIM), kvspec(2 * HEAD_DIM), kvspec(HEAD_DIM),
            vt_spec, vt_spec,
            pl.BlockSpec((None, N_KV, GATE_ROWS, tm), lambda i: (i // nl, 0, 0, i % nl)),
        ],
        out_shape=[
            jax.ShapeDtypeStruct((n_tok, D_SSM), F32),
            jax.ShapeDtypeStruct((bsz, N_HEADS, HEAD_DIM, seq), BF16),
            kvshape(HEAD_DIM, F32), kvshape(HEAD_DIM, F32), kvshape(2 * HEAD_DIM), kvshape(HEAD_DIM),
            vt_shape, vt_shape,
            jax.ShapeDtypeStruct((bsz, N_KV, GATE_ROWS, seq), F32),
        ],
        compiler_params=_params(("parallel",)),
        name="in_proj",
    )(x2d, g1, wrow, wcol, gq, gks, gkw, bd128)


def _s5_weights(lam_re, lam_im, log_step, b_re, b_im, c_re, c_im, d_skip, n_sub):
    q = SSM_Q
    lam = lax.complex(lam_re.astype(F32), lam_im.astype(F32))
    step = jnp.exp(log_step.astype(F32))[:, None]
    lam_bar = jnp.exp(lam * step)
    b_bar = ((lam_bar - 1.0) / lam)[..., None] * lax.complex(b_re.astype(F32), b_im.astype(F32))
    c = lax.complex(c_re.astype(F32), c_im.astype(F32))
    pows = [jnp.ones_like(lam_bar)]
    for _ in range(q):
        pows.append(pows[-1] * lam_bar)
    pw = jnp.stack(pows)
    lt, a8 = SSM_LT, LANES // SSM_CH
    hp = a8 * SSM_STATE
    e_lane = (jnp.arange(a8)[:, None] == jnp.arange(LANES)[None, :] // SSM_CH).astype(F32)
    e_state = (jnp.arange(a8)[:, None] == jnp.arange(hp)[None, :] // SSM_STATE).astype(F32)
    e_lane_t = jnp.tile(e_lane, (1, q))
    e_state_t = jnp.tile(e_state, (1, 2))

    kk = jnp.real(jnp.einsum('ghp,kgp,gpi->kghi', c, pw[:q], b_bar))
    km = kk.reshape(q, lt, a8, SSM_CH, SSM_CH).transpose(1, 4, 0, 2, 3).reshape(lt, SSM_CH, q, LANES)
    lag = jnp.arange(q)[None, :] - jnp.arange(q)[:, None]
    kg = km[:, :, jnp.clip(lag, 0, q - 1), :] * (lag >= 0)[None, None, :, :, None].astype(F32)
    kc = kg.transpose(0, 2, 1, 3, 4).reshape(lt, q, 1, SSM_CH, q * LANES)
    t_in = (kc * e_lane_t[None, None, :, None, :]).reshape(lt, q * LANES, q * LANES)

    wc = pw[q - 1 - jnp.arange(q)][..., None] * b_bar[None]
    wri = jnp.stack([jnp.real(wc), jnp.imag(wc)])
    wm = (wri.reshape(2, q, lt, a8, SSM_STATE, SSM_CH).transpose(2, 1, 5, 0, 3, 4)
          .reshape(lt, q, 1, SSM_CH, 2 * hp))
    w_loc = (wm * e_state_t[None, None, :, None, :]).reshape(lt, q * LANES, 2 * hp)

    cl = c[None] * pw[1:q + 1][:, :, None, :]
    cri = jnp.stack([jnp.real(cl), -jnp.imag(cl)])
    mm = (cri.reshape(2, q, lt, a8, SSM_CH, SSM_STATE).transpose(2, 0, 5, 1, 3, 4)
          .reshape(lt, 2, 1, SSM_STATE, q * LANES))
    m_st = (mm * e_lane_t[None, None, :, None, :]).reshape(lt, 2 * hp, q * LANES)

    n_lvl = max(1, (n_sub - 1).bit_length())
    lv = [pw[q]]
    for _ in range(n_lvl - 1):
        lv.append(lv[-1] * lv[-1])
    lvs = jnp.stack(lv).reshape(n_lvl, lt, 1, hp)
    pw_re = jnp.real(lvs).transpose(1, 0, 2, 3)
    pw_im = jnp.imag(lvs).transpose(1, 0, 2, 3)
    dvec = jnp.tile(d_skip.astype(F32).reshape(lt, 1, LANES), (1, 1, q))
    return w_loc.astype(BF16), t_in.astype(BF16), m_st.astype(BF16), pw_re, pw_im, dvec


def _s5_kernel(u_ref, w_ref, t_ref, m_ref, pwr_ref, pwi_ref, d_ref, y_ref, *, n_sub, n_lvl):
    half = (LANES // SSM_CH) * SSM_STATE
    q = SSM_Q
    u = jnp.concatenate([u_ref[pl.ds(s, n_sub, stride=q), :] for s in range(q)], axis=1)
    ub = u.astype(BF16)
    s_loc = _dot(ub, w_ref[...])
    re = s_loc[:, :half]
    im = s_loc[:, half:]
    rowi = lax.broadcasted_iota(jnp.int32, (n_sub, half), 0)
    for k in range(n_lvl):
        d = 1 << k
        ar = pwr_ref[k]
        ai = pwi_ref[k]
        keep = rowi >= d
        sre = jnp.where(keep, pltpu.roll(re, d, axis=0), 0.0)
        sim = jnp.where(keep, pltpu.roll(im, d, axis=0), 0.0)
        re, im = re + (ar * sre - ai * sim), im + (ar * sim + ai * sre)
    keep = rowi >= 1
    xre = jnp.where(keep, pltpu.roll(re, 1, axis=0), 0.0)
    xim = jnp.where(keep, pltpu.roll(im, 1, axis=0), 0.0)
    xst = jnp.concatenate([xre, xim], axis=1).astype(BF16)
    y = _dot(ub, t_ref[...]) + _dot(xst, m_ref[...]) + d_ref[...] * u
    for j in range(q):
        y_ref[pl.ds(j, n_sub, stride=q), :] = y[:, j * LANES:(j + 1) * LANES]


def _s5(u, w_loc, t_in, m_st, pw_re, pw_im, dvec, *, bsz, n_sub):
    q = SSM_Q
    n_lvl = pw_re.shape[1]
    kern = functools.partial(_s5_kernel, n_sub=n_sub, n_lvl=n_lvl)
    wide = q * LANES
    seq = n_sub * q
    return pl.pallas_call(
        kern,
        grid=(bsz, SSM_LT),
        in_specs=[
            pl.BlockSpec((None, seq, LANES), lambda b, l: (b, 0, l)),
            pl.BlockSpec((None, wide, wide), lambda b, l: (l, 0, 0)),
            pl.BlockSpec((None, wide, wide), lambda b, l: (l, 0, 0)),
            pl.BlockSpec((None, wide, wide), lambda b, l: (l, 0, 0)),
            pl.BlockSpec((None, n_lvl, 1, wide // 2), lambda b, l: (l, 0, 0, 0)),
            pl.BlockSpec((None, n_lvl, 1, wide // 2), lambda b, l: (l, 0, 0, 0)),
            pl.BlockSpec((None, 1, wide), lambda b, l: (l, 0, 0)),
        ],
        out_specs=pl.BlockSpec((None, seq, LANES), lambda b, l: (b, 0, l)),
        out_shape=jax.ShapeDtypeStruct((bsz, seq, D_SSM), F32),
        compiler_params=_params(("parallel", "parallel")),
        name="s5",
    )(u, w_loc, t_in, m_st, pw_re, pw_im, dvec)


def _compress_kernel(kc_ref, vc_ref, w1k_ref, w2k_ref, w1v_ref, w2vt_ref, posk_ref, posv_ref, gkc_ref,
                     kcmp_ref, vcmpt_ref, *, nch):
    half = CMP_STRIDE * HEAD_DIM

    def hidden(x_ref, w1_ref, pos_ref):
        x = jnp.concatenate([x_ref[pl.ds(j, nch, stride=CMP_STRIDE), :] for j in range(CMP_STRIDE)],
                            axis=1).astype(BF16)
        a = _dot(x, w1_ref[:half, :])
        b = _dot(x, w1_ref[half:, :])
        pv = _dot(pos_ref[...], w1_ref[...])[0:1, :]
        hid = a + pltpu.roll(b, nch - 1, axis=0) + pv
        return _gelu(hid).astype(BF16)

    k = _dot(hidden(kc_ref, w1k_ref, posk_ref), w2k_ref[...])
    ms = jnp.mean(k * k, axis=-1, keepdims=True)
    kcmp_ref[...] = (k * lax.rsqrt(ms + EPS) * gkc_ref[...]).astype(BF16)
    vt = _dot_nt(w2vt_ref[...], hidden(vc_ref, w1v_ref, posv_ref))
    coli = lax.broadcasted_iota(jnp.int32, vt.shape, 1)
    vcmpt_ref[...] = jnp.where(coli < nch - 1, vt, 0.0).astype(BF16)


def _compress(kcf, vcf, w1k, w2k, w1v, w2vt, posk, posv, gkc, *, bsz, nch):
    kern = functools.partial(_compress_kernel, nch=nch)
    wide = CMP_STRIDE * HEAD_DIM
    xspec = pl.BlockSpec((None, None, nch * CMP_STRIDE, HEAD_DIM), lambda b, g: (b, g, 0, 0))
    c2 = lambda b, g: (0, 0)
    return pl.pallas_call(
        kern,
        grid=(bsz, N_KV),
        in_specs=[
            xspec, xspec,
            pl.BlockSpec((2 * wide, CMP_HIDDEN), c2), pl.BlockSpec((CMP_HIDDEN, HEAD_DIM), c2),
            pl.BlockSpec((2 * wide, CMP_HIDDEN), c2), pl.BlockSpec((HEAD_DIM, CMP_HIDDEN), c2),
            pl.BlockSpec((8, 2 * wide), c2), pl.BlockSpec((8, 2 * wide), c2),
            pl.BlockSpec((1, HEAD_DIM), c2),
        ],
        out_specs=[pl.BlockSpec((None, None, nch, HEAD_DIM), lambda b, g: (b, g, 0, 0)),
                   pl.BlockSpec((None, None, HEAD_DIM, nch), lambda b, g: (b, g, 0, 0))],
        out_shape=[jax.ShapeDtypeStruct((bsz, N_KV, nch, HEAD_DIM), BF16),
                   jax.ShapeDtypeStruct((bsz, N_KV, HEAD_DIM, nch), BF16)],
        compiler_params=_params(("parallel", "parallel")),
        name="compress",
    )(kcf, vcf, w1k, w2k, w1v, w2vt, posk, posv, gkc)


def _nsa_kernel(qt_ref, kcmp_ref, vcmpt_ref, ksa_ref, kw_ref, vst_ref, vwt_ref, gate_ref, ovlt_ref, o_ref,
                accs_ref, accw_ref, *, tq, tk, nch):
    qi = pl.program_id(2)
    q0 = qi * tq
    rows = HPG * tq
    qt = jnp.concatenate([qt_ref[h] for h in range(HPG)], axis=1)
    tpos = q0 + lax.broadcasted_iota(jnp.int32, (1, rows), 1) % tq
    krow = lax.broadcasted_iota(jnp.int32, (tk, rows), 0)
    m0 = jnp.full((1, rows), NEG, F32)

    def run_tiles(k_ref, q_all, vt_ref, acc_ref, m, tiles):
        scores = []
        for kt, kind in tiles:
            kc = jnp.maximum(kt, 0) if kind in ('band', 'valid') else kt
            s = _dot(k_ref[pl.ds(pl.multiple_of(kc * tk, tk), tk), :], q_all)
            if kind == 'causal':
                s = jnp.where(kt * tk + krow <= tpos, s, NEG)
            elif kind == 'band':
                s = jnp.where((kt * tk + krow > tpos - WINDOW) & (kt >= 0), s, NEG)
            elif kind == 'valid':
                s = jnp.where(kt >= 0, s, NEG)
            scores.append((kc, s))
        for kc, s in scores:
            m_new = jnp.maximum(m, jnp.max(s, axis=0, keepdims=True))
            alpha = jnp.exp2(m - m_new)
            p = jnp.exp2(s - m_new).astype(BF16)
            acc_ref[...] = alpha * acc_ref[...] + _dot(vt_ref[kc], p)
            m = m_new
        return m

    accw_ref[...] = jnp.zeros_like(accw_ref)
    n_win = WINDOW // tk
    run_tiles(kw_ref, qt, vwt_ref, accw_ref, m0,
              [(qi - n_win, 'band')] + [(qi - n_win + t, 'valid') for t in range(1, n_win)] + [(qi, 'causal')])

    s = _dot(kcmp_ref[...], qt)
    cend = lax.broadcasted_iota(jnp.int32, (nch, rows), 0) * CMP_STRIDE + (CMP_BLOCK - 1)
    s = jnp.where(cend <= tpos, s, NEG)
    m = jnp.max(s, axis=0, keepdims=True)
    p = jnp.exp2(s - m)
    p = p * (1.0 / jnp.sum(p, axis=0, keepdims=True))
    p = jnp.where(tpos >= CMP_BLOCK - 1, p, 0.0)
    o_cmp = _dot(vcmpt_ref[...], p.astype(BF16))

    psum = p[:, 0:tq]
    for h in range(1, HPG):
        psum = psum + p[:, h * tq:(h + 1) * tq]
    hi = psum.astype(BF16)
    lo = (psum - hi.astype(F32)).astype(BF16)
    ovlt = ovlt_ref[...]
    imp = _dot(ovlt, hi) + _dot(ovlt, lo)
    nb = MAX_SEL_BLOCKS
    blk = lax.broadcasted_iota(jnp.int32, (nb, tq), 0)
    cur = (q0 + lax.broadcasted_iota(jnp.int32, (nb, tq), 1)) // SEL_BLOCK
    forced = (blk == 0) | (blk == cur) | (blk == cur - 1)
    imp = jnp.where(forced, FORCE, jnp.where(blk <= cur, imp, NEG))
    sub = 8
    groups = [imp[r:r + sub] for r in range(0, nb, sub)]
    ranks = [jnp.zeros((sub, tq), F32) for _ in groups]
    rowl = lax.broadcasted_iota(jnp.int32, (sub, tq), 0)
    for i in range(nb):
        ri = jnp.broadcast_to(imp[i:i + 1, :], (sub, tq))
        for gi, x in enumerate(groups):
            if i < gi * sub:
                ahead = ri >= x
            elif i >= (gi + 1) * sub:
                ahead = ri > x
            else:
                ahead = (ri > x) | ((ri == x) & (rowl > i - gi * sub))
            ranks[gi] = ranks[gi] + jnp.where(ahead, 1.0, 0.0)
    rank = jnp.concatenate(ranks, axis=0)
    sel = jnp.where(rank < N_SELECT, 0.0, NEG).astype(BF16)
    qa = jnp.concatenate([qt, jnp.concatenate([sel] * HPG, axis=1)], axis=0)

    accs_ref[...] = jnp.zeros_like(accs_ref)
    nu = NSA_UNROLL

    def sel_multi(j, m):
        return run_tiles(ksa_ref, qa, vst_ref, accs_ref, m, [(nu * j + t, None) for t in range(nu)])

    m_sel = lax.fori_loop(0, qi // nu, sel_multi, m0)
    for r in range(nu):
        @pl.when(qi % nu == r)
        def _(r=r):
            base = qi - r
            run_tiles(ksa_ref, qa, vst_ref, accs_ref, m_sel,
                      [(base + t, None) for t in range(r)] + [(qi, 'causal')])

    accs = accs_ref[...]
    accw = accw_ref[...]
    o_sel = accs[:HEAD_DIM] * (1.0 / accs[HEAD_DIM:HEAD_DIM + 1])
    o_win = accw[:HEAD_DIM] * (1.0 / accw[HEAD_DIM:HEAD_DIM + 1])
    gt = gate_ref[...]
    for h in range(HPG):
        c = slice(h * tq, (h + 1) * tq)
        o_ref[h * HEAD_DIM:(h + 1) * HEAD_DIM, :] = (
            gt[3 * h:3 * h + 1] * o_cmp[:, c] + gt[3 * h + 1:3 * h + 2] * o_sel[:, c]
            + gt[3 * h + 2:3 * h + 3] * o_win[:, c])


def _nsa(qt, kcmp, vcmpt, ksa, kw, vst, vwt, gate, ovlt, *, bsz, seq, nch):
    tq, tk = NSA_TQ, NSA_TK
    assert tq == tk and WINDOW % tk == 0
    nq = seq // tq
    kern = functools.partial(_nsa_kernel, tq=tq, tk=tk, nch=nch)
    full = lambda b, g, i: (b, g, 0, 0)
    full5 = lambda b, g, i: (b, g, 0, 0, 0)
    qd = HPG * HEAD_DIM
    rows = HPG * tq
    return pl.pallas_call(
        kern,
        grid=(bsz, N_KV, nq),
        in_specs=[
            pl.BlockSpec((None, HPG, HEAD_DIM, tq), lambda b, g, i: (b, g, 0, i)),
            pl.BlockSpec((None, None, nch, HEAD_DIM), full),
            pl.BlockSpec((None, None, HEAD_DIM, nch), full),
            pl.BlockSpec((None, None, seq, 2 * HEAD_DIM), full),
            pl.BlockSpec((None, None, seq, HEAD_DIM), full),
            pl.BlockSpec((None, None, seq // tk, V_ROWS, tk), full5),
            pl.BlockSpec((None, None, seq // tk, V_ROWS, tk), full5),
            pl.BlockSpec((None, None, GATE_ROWS, tq), lambda b, g, i: (b, g, 0, i)),
            pl.BlockSpec((MAX_SEL_BLOCKS, nch), lambda b, g, i: (0, 0)),
        ],
        out_specs=pl.BlockSpec((None, qd, tq), lambda b, g, i: (b, g, i)),
        out_shape=jax.ShapeDtypeStruct((bsz, D_ATT, seq), F32),
        scratch_shapes=[pltpu.VMEM((V_ROWS, rows), F32), pltpu.VMEM((V_ROWS, rows), F32)],
        compiler_params=_params(("parallel", "parallel", "arbitrary")),
        name="nsa",
    )(qt, kcmp, vcmpt, ksa, kw, vst, vwt, gate, ovlt)


def _out_proj_kernel(tm, ys_ref, yat_ref, x_ref, wglu_ref, bglu_ref, gs_ref, ga_ref, wo_ref, g2_ref, wr_ref, br_ref,
                     x2t_ref, route_ref):
    y = _gelu(ys_ref[...])
    y = y * _sigmoid(_dot(y.astype(BF16), wglu_ref[...]) + bglu_ref[...])
    ysn = y * lax.rsqrt(jnp.mean(y * y, axis=-1, keepdims=True) + EPS) * gs_ref[...]
    yat = yat_ref[...]
    yant = yat * lax.rsqrt(jnp.mean(yat * yat, axis=0, keepdims=True) + EPS) * ga_ref[...]
    yan = yant.T
    x2 = x_ref[...] + _dot(ysn.astype(BF16), wo_ref[:D_SSM, :]) + _dot(yan.astype(BF16), wo_ref[D_SSM:, :])
    for c in range(D_MODEL // LANES):
        x2t_ref[pl.ds(c, tm, stride=D_MODEL // LANES), :] = x2[:, c * LANES:(c + 1) * LANES]
    h2 = (x2 * lax.rsqrt(jnp.mean(x2 * x2, axis=-1, keepdims=True) + EPS) * g2_ref[...]).astype(BF16)

    logits = _dot(h2, wr_ref[...]) + br_ref[...]
    lane = lax.broadcasted_iota(jnp.int32, logits.shape, 1).astype(F32)
    far = float(LANES)
    is_g = lane < N_EXP_GROUPS
    glog = jnp.where(is_g, logits, -jnp.inf)
    gmax = jnp.max(glog, axis=1, keepdims=True)
    gsum = jnp.sum(jnp.where(is_g, jnp.exp(logits - gmax), 0.0), axis=1, keepdims=True)
    gsel = jnp.min(jnp.where(glog == gmax, lane, far), axis=1, keepdims=True)
    gprob = 1.0 / gsum
    lo = ROUTER_OFF + EXPERTS_PER_GROUP * gsel
    in_e = (lane >= lo) & (lane < lo + EXPERTS_PER_GROUP)
    emax = jnp.max(jnp.where(in_e, logits, -jnp.inf), axis=1, keepdims=True)
    eexp = jnp.where(in_e, jnp.exp(logits - emax), 0.0)
    eprob = jnp.where(in_e, eexp / jnp.sum(eexp, axis=1, keepdims=True), -1.0)
    v1 = jnp.max(eprob, axis=1, keepdims=True)
    i1 = jnp.min(jnp.where(eprob == v1, lane, far), axis=1, keepdims=True)
    rest = jnp.where(lane == i1, -1.0, eprob)
    v2 = jnp.max(rest, axis=1, keepdims=True)
    i2 = jnp.min(jnp.where(rest == v2, lane, far), axis=1, keepdims=True)
    den = v1 + v2
    route_ref[...] = (jnp.where(lane == i1, v1 / den * gprob, 0.0)
                            + jnp.where(lane == i2, v2 / den * gprob, 0.0)
                            + jnp.where(lane == 0.0, gsel, 0.0))


def _out_proj(ys, yat, x2d, wglu, bglu, gs, ga, wo, g2, wr, br, *, seq):
    n_tok = x2d.shape[0]
    tm = 512
    nl = seq // tm
    row = lambda i: (i, 0)
    const = lambda i: (0, 0)
    return pl.pallas_call(
        functools.partial(_out_proj_kernel, tm),
        grid=(n_tok // tm,),
        in_specs=[
            pl.BlockSpec((tm, D_SSM), row),
            pl.BlockSpec((None, D_ATT, tm), lambda i: (i // nl, 0, i % nl)),
            pl.BlockSpec((tm, D_MODEL), row),
            pl.BlockSpec((D_SSM, D_SSM), const),
            pl.BlockSpec((1, D_SSM), const),
            pl.BlockSpec((1, D_SSM), const),
            pl.BlockSpec((D_ATT, 1), const),
            pl.BlockSpec((D_SSM + D_ATT, D_MODEL), const),
            pl.BlockSpec((1, D_MODEL), const),
            pl.BlockSpec((D_MODEL, LANES), const),
            pl.BlockSpec((1, LANES), const),
        ],
        out_specs=[pl.BlockSpec((tm * (D_MODEL // LANES), LANES), row), pl.BlockSpec((tm, LANES), row)],
        out_shape=[jax.ShapeDtypeStruct((n_tok * (D_MODEL // LANES), LANES), F32),
                   jax.ShapeDtypeStruct((n_tok, LANES), F32)],
        compiler_params=_params(("parallel",)),
        name="out_proj",
    )(ys, yat, x2d, wglu, bglu, gs, ga, wo, g2, wr, br)


def _moe_plan(gsel, n_tok):
    tmx = MOE_TM
    n_tiles = n_tok // tmx + N_EXP_GROUPS
    oh = (gsel[:, None] == jnp.arange(N_EXP_GROUPS)[None, :]).astype(jnp.int32)
    csum = jnp.cumsum(oh, axis=0)
    counts = csum[-1]
    rank = jnp.sum(csum * oh, axis=1) - 1
    nt = (counts + tmx - 1) // tmx
    tend = jnp.cumsum(nt)
    toff = tend - nt
    pos = jnp.sum(oh * toff[None, :], axis=1) * tmx + rank
    tok = jnp.zeros((n_tiles * tmx,), jnp.int32).at[pos].set(jnp.arange(n_tok, dtype=jnp.int32))
    tile = jnp.arange(n_tiles)
    grp = jnp.minimum(jnp.sum((tile[:, None] >= tend[None, :]).astype(jnp.int32), axis=1), N_EXP_GROUPS - 1)
    nval = jnp.clip(counts[grp] - (tile - toff[grp]) * tmx, 0, tmx)
    return grp.astype(jnp.int32), nval.astype(jnp.int32), tok


def _moe_kernel(grp_ref, nval_ref, tok_ref, x_hbm, r_hbm, g2_ref, wg_ref, wu_ref, wd_ref, o_hbm,
                xbuf, rbuf, obuf, abuf, gsem, ssem, *, tmx, n_tiles):
    i = pl.program_id(0)
    slot = i % 2

    npc = D_MODEL // LANES

    def gather_row(tile, r, dst_slot):
        t = tok_ref[tile * tmx + r]
        pltpu.make_async_copy(x_hbm.at[pl.ds(pl.multiple_of(t * npc, npc), npc), :],
                              xbuf.at[dst_slot, pl.ds(r * npc, npc), :], gsem.at[dst_slot]).start()
        pltpu.make_async_copy(r_hbm.at[pl.ds(t, 1), :], rbuf.at[dst_slot, pl.ds(r, 1), :], gsem.at[dst_slot]).start()

    def scatter_row(tile, r, src_slot):
        t = tok_ref[tile * tmx + r]
        return pltpu.make_async_copy(obuf.at[src_slot, pl.ds(r, 1), :], o_hbm.at[pl.ds(t, 1), :], ssem.at[src_slot])

    def gather_wait(src_slot):
        pltpu.make_async_copy(x_hbm.at[pl.ds(0, tmx * npc), :], xbuf.at[src_slot], gsem.at[src_slot]).wait()
        pltpu.make_async_copy(r_hbm.at[pl.ds(0, tmx), :], rbuf.at[src_slot], gsem.at[src_slot]).wait()

    def scatter_wait(n, src_slot):
        n8 = pl.multiple_of((n // 8) * 8, 8)

        @pl.when(n8 > 0)
        def _():
            pltpu.make_async_copy(obuf.at[src_slot, pl.ds(0, n8), :], o_hbm.at[pl.ds(0, n8), :],
                                  ssem.at[src_slot]).wait()

        def one(r, c):
            pltpu.make_async_copy(obuf.at[src_slot, pl.ds(0, 1), :], o_hbm.at[pl.ds(0, 1), :],
                                  ssem.at[src_slot]).wait()
            return c

        lax.fori_loop(0, n - n8, one, 0)

    @pl.when(i == 0)
    def _():
        def body(r, c):
            gather_row(0, r, 0)
            return c

        lax.fori_loop(0, tmx, body, 0)

    gather_wait(slot)

    @pl.when(i >= 2)
    def _():
        scatter_wait(nval_ref[i - 2], slot)

    nxt = jnp.minimum(i + 1, n_tiles - 1)
    per = tmx // EXPERTS_PER_GROUP
    prev_full = (i >= 1) & (nval_ref[jnp.maximum(i - 1, 0)] == tmx)

    def experts(with_scatter):
        x2 = jnp.concatenate([xbuf[slot, pl.ds(c, tmx, stride=npc), :] for c in range(npc)], axis=1)
        cw = rbuf[slot]
        h = (x2 * lax.rsqrt(jnp.mean(x2 * x2, axis=-1, keepdims=True) + EPS) * g2_ref[...]).astype(BF16)
        lane = lax.broadcasted_iota(jnp.int32, cw.shape, 1)
        first = ROUTER_OFF + EXPERTS_PER_GROUP * grp_ref[i]
        for k in range(EXPERTS_PER_GROUP):
            for r in range(k * per, (k + 1) * per):
                gather_row(nxt, r, 1 - slot)
                if with_scatter:
                    scatter_row(i - 1, r, 1 - slot).start()
            gate = _dot(h, wg_ref[k])
            up = _dot(h, wu_ref[k])
            ck = jnp.sum(jnp.where(lane == first + k, cw, 0.0), axis=1, keepdims=True)
            abuf[:, k * D_EXPERT:(k + 1) * D_EXPERT] = (gate * _sigmoid(gate) * up * ck).astype(BF16)
        obuf[slot] = x2 + _dot(abuf[...], wd_ref[...])

    @pl.when(prev_full)
    def _():
        experts(True)

    @pl.when(jnp.logical_not(prev_full))
    def _():
        experts(False)

    nv = nval_ref[i]

    @pl.when(nv < tmx)
    def _():
        def body(r, c):
            scatter_row(i, r, slot).start()
            return c

        lax.fori_loop(0, nv, body, 0)

    @pl.when(i == n_tiles - 1)
    def _():
        @pl.when(nv == tmx)
        def _():
            def body(r, c):
                scatter_row(i, r, slot).start()
                return c

            lax.fori_loop(0, tmx, body, 0)

        gather_wait(1 - slot)
        scatter_wait(nval_ref[i - 1], 1 - slot)
        scatter_wait(nv, slot)


def _moe(x2t, route, grp, nval, tok, g2, wg, wu, wd):
    n_tok = route.shape[0]
    tmx = MOE_TM
    n_tiles = grp.shape[0]
    kern = functools.partial(_moe_kernel, tmx=tmx, n_tiles=n_tiles)
    gk = EXPERTS_PER_GROUP * D_EXPERT
    grid_spec = pltpu.PrefetchScalarGridSpec(
        num_scalar_prefetch=3,
        grid=(n_tiles,),
        in_specs=[
            pl.BlockSpec(memory_space=pl.ANY),
            pl.BlockSpec(memory_space=pl.ANY),
            pl.BlockSpec((1, D_MODEL), lambda i, g, n, t: (0, 0)),
            pl.BlockSpec((None, EXPERTS_PER_GROUP, D_MODEL, D_EXPERT), lambda i, g, n, t: (g[i], 0, 0, 0)),
            pl.BlockSpec((None, EXPERTS_PER_GROUP, D_MODEL, D_EXPERT), lambda i, g, n, t: (g[i], 0, 0, 0)),
            pl.BlockSpec((None, gk, D_MODEL), lambda i, g, n, t: (g[i], 0, 0)),
        ],
        out_specs=pl.BlockSpec(memory_space=pl.ANY),
        scratch_shapes=[
            pltpu.VMEM((2, tmx * (D_MODEL // LANES), LANES), F32),
            pltpu.VMEM((2, tmx, LANES), F32),
            pltpu.VMEM((2, tmx, D_MODEL), F32),
            pltpu.VMEM((tmx, gk), BF16),
            pltpu.SemaphoreType.DMA((2,)),
            pltpu.SemaphoreType.DMA((2,)),
        ],
    )
    return pl.pallas_call(
        kern,
        grid_spec=grid_spec,
        out_shape=jax.ShapeDtypeStruct((n_tok, D_MODEL), F32),
        compiler_params=_params(("arbitrary",)),
        name="moe",
    )(grp, nval, tok, x2t, route, g2, wg, wu, wd)


def _block_diag_ones(n, blk):
    i = jnp.arange(n) // blk
    return (i[:, None] == i[None, :]).astype(BF16)


def _layer(x, norm1_g, w_in, lam_re, lam_im, log_step, b_re, b_im, c_re, c_im, d_skip,
           w_glu, b_glu, g_q, g_kc, g_ks, g_kw, pos_k, pos_v, w_ck1, w_ck2, w_cv1, w_cv2,
           out_g_ssm, out_g_att, w_out, norm2_g, w_grp, b_grp, w_exp, b_exp, w_gate, w_up, w_down):
    bsz, seq, _ = x.shape
    assert seq % 512 == 0 and seq // SEL_BLOCK <= MAX_SEL_BLOCKS
    n_tok = bsz * seq
    x2d = x.reshape(n_tok, D_MODEL)
    q8 = SSM_Q
    n_sub = seq // q8
    nch = seq // CMP_STRIDE

    o_q = D_SSM
    o_kv = D_SSM + D_ATT
    o_gt = o_kv + 6 * D_KV
    kv = lambda i: w_in[:, o_kv + i * D_KV:o_kv + (i + 1) * D_KV]
    wrow = jnp.concatenate([w_in[:, :o_q], kv(0), kv(1), kv(2), kv(4)], axis=1).astype(BF16)
    per_g = HPG * N_BRANCH
    wgt = jnp.zeros((D_MODEL, N_KV * GATE_ROWS), F32)
    for g in range(N_KV):
        wgt = wgt.at[:, g * GATE_ROWS:g * GATE_ROWS + per_g].set(w_in[:, o_gt + g * per_g:o_gt + (g + 1) * per_g])
    wcol = jnp.concatenate([w_in[:, o_q:o_kv], kv(3), kv(5), wgt], axis=1).T.astype(BF16)
    qscale = (HEAD_DIM ** -0.5) * math.log2(math.e)
    gq = (jnp.tile(g_q.astype(F32), N_HEADS) * qscale).reshape(D_ATT, 1)
    gks = jnp.tile(g_ks.astype(F32), N_KV).reshape(1, D_KV)
    gkw = jnp.tile(g_kw.astype(F32), N_KV).reshape(1, D_KV)

    u, qt, kc, vc, ksa, kw, vst, vwt, gate = _in_proj(
        x2d, norm1_g.reshape(1, D_MODEL), wrow, wcol, gq, gks, gkw,
        _block_diag_ones(D_KV, HEAD_DIM), bsz=bsz, seq=seq)

    w_loc, t_in, m_st, pw_re, pw_im, dvec = _s5_weights(
        lam_re, lam_im, log_step, b_re, b_im, c_re, c_im, d_skip, n_sub)
    ys = _s5(u.reshape(bsz, seq, D_SSM), w_loc, t_in, m_st, pw_re, pw_im, dvec,
             bsz=bsz, n_sub=n_sub).reshape(n_tok, D_SSM)

    wide = CMP_STRIDE * HEAD_DIM
    pad8 = lambda p: jnp.zeros((8, 2 * wide), F32).at[0].set(p.reshape(-1)).astype(BF16)
    kcmp, vcmpt = _compress(
        kc, vc,
        w_ck1.astype(BF16), w_ck2.astype(BF16), w_cv1.astype(BF16), w_cv2.T.astype(BF16),
        pad8(pos_k), pad8(pos_v), g_kc.astype(F32).reshape(1, HEAD_DIM), bsz=bsz, nch=nch)
    cstart = jnp.arange(nch) * CMP_STRIDE
    sstart = jnp.arange(MAX_SEL_BLOCKS) * SEL_BLOCK
    ovlt = ((cstart[None, :] < sstart[:, None] + SEL_BLOCK) & (cstart[None, :] + CMP_BLOCK > sstart[:, None])
            & (jnp.arange(MAX_SEL_BLOCKS)[:, None] < seq // SEL_BLOCK)
            & (jnp.arange(nch)[None, :] < nch - 1)).astype(BF16)
    yat = _nsa(qt, kcmp, vcmpt, ksa, kw, vst, vwt, gate, ovlt, bsz=bsz, seq=seq, nch=nch)

    wr = jnp.zeros((D_MODEL, LANES), F32)
    wr = wr.at[:, :N_EXP_GROUPS].set(w_grp).at[:, ROUTER_OFF:ROUTER_OFF + N_EXPERTS].set(w_exp).astype(BF16)
    br = jnp.zeros((1, LANES), F32)
    br = br.at[0, :N_EXP_GROUPS].set(b_grp).at[0, ROUTER_OFF:ROUTER_OFF + N_EXPERTS].set(b_exp)
    g2 = norm2_g.reshape(1, D_MODEL).astype(F32)
    x2t, route = _out_proj(
        ys, yat, x2d, w_glu.astype(BF16), b_glu.reshape(1, D_SSM).astype(F32),
        out_g_ssm.reshape(1, D_SSM).astype(F32), out_g_att.reshape(D_ATT, 1).astype(F32),
        w_out.astype(BF16), g2, wr, br, seq=seq)

    grp, nval, tok = _moe_plan(route[:, 0].astype(jnp.int32), n_tok)
    gshape = (N_EXP_GROUPS, EXPERTS_PER_GROUP, D_MODEL, D_EXPERT)
    out = _moe(x2t, route, grp, nval, tok, g2, w_gate.astype(BF16).reshape(gshape), w_up.astype(BF16).reshape(gshape),
               w_down.astype(BF16).reshape(N_EXP_GROUPS, EXPERTS_PER_GROUP * D_EXPERT, D_MODEL))
    return out.reshape(bsz, seq, D_MODEL)


def kernel(x, norm1_g, w_in, lam_re, lam_im, log_step, b_re, b_im, c_re, c_im, d_skip, w_glu, b_glu, g_q, g_kc, g_ks, g_kw, pos_k, pos_v, w_ck1, w_ck2, w_cv1, w_cv2, out_g_ssm, out_g_att, w_out, norm2_g, w_grp, b_grp, w_exp, b_exp, w_gate, w_up, w_down):
    depth = norm1_g.shape[0]
    for l in range(depth):
        x = _layer(x, norm1_g[l], w_in[l], lam_re[l], lam_im[l], log_step[l], b_re[l], b_im[l], c_re[l],
                   c_im[l], d_skip[l], w_glu[l], b_glu[l], g_q[l], g_kc[l], g_ks[l], g_kw[l], pos_k[l],
                   pos_v[l], w_ck1[l], w_ck2[l], w_cv1[l], w_cv2[l], out_g_ssm[l], out_g_att[l], w_out[l],
                   norm2_g[l], w_grp[l], b_grp[l], w_exp[l], b_exp[l], w_gate[l], w_up[l], w_down[l])
    return x
```

```python
import functools
import math

import jax
import jax.numpy as jnp
from jax import lax
from jax.experimental import pallas as pl
from jax.experimental.pallas import tpu as pltpu

D_MODEL = 1024
D_SSM = 512
SSM_CH = 16
SSM_GROUPS = D_SSM // SSM_CH
SSM_STATE = 64
D_ATT = 512
HEAD_DIM = 64
N_HEADS = D_ATT // HEAD_DIM
N_KV = 2
HPG = N_HEADS // N_KV
D_KV = N_KV * HEAD_DIM
N_BRANCH = 3
CMP_STRIDE = 16
CMP_BLOCK = 2 * CMP_STRIDE
CMP_HIDDEN = 256
SEL_BLOCK = 64
N_SELECT = 16
WINDOW = 512
N_EXP_GROUPS = 4
EXPERTS_PER_GROUP = 8
N_EXPERTS = N_EXP_GROUPS * EXPERTS_PER_GROUP
D_EXPERT = 256
EPS = 1e-6
NEG = -1e30
FORCE = 1e9

LANES = 128
SSM_Q = 8
SSM_LT = D_SSM // LANES
ROUTER_OFF = N_EXP_GROUPS
NSA_TQ = 256
NSA_TK = 256
NSA_UNROLL = 4
V_ROWS = HEAD_DIM + 16
MAX_SEL_BLOCKS = 64
MOE_TM = 256
GATE_ROWS = 16
VMEM_LIMIT = 56 * 1024 * 1024

F32 = jnp.float32
BF16 = jnp.bfloat16


def _dot(a, b):
    return jnp.dot(a, b, preferred_element_type=F32)


def _dot_nt(a, b):
    return lax.dot_general(a, b, (((1,), (1,)), ((), ())), preferred_element_type=F32)


def _split_dot(x, w):
    hi = x.astype(BF16)
    lo = (x - hi.astype(F32)).astype(BF16)
    return _dot(hi, w) + _dot(lo, w)


def _gelu(x):
    c = math.sqrt(2.0 / math.pi)
    return 0.5 * x * (1.0 + jnp.tanh(c * (x + 0.044715 * (x * x * x))))


def _sigmoid(x):
    return 1.0 / (1.0 + jnp.exp(-x))


def _params(sem):
    return pltpu.CompilerParams(dimension_semantics=sem, vmem_limit_bytes=VMEM_LIMIT)


def _in_proj_kernel(x_ref, g1_ref, wrow_ref, wcol_ref, gq_ref, gks_ref, gkw_ref, bd128_ref,
                    u_ref, qt_ref, kc_ref, vc_ref, ksa_ref, kw_ref, vst_ref, vwt_ref, gate_ref, *, tm, nl):
    x = x_ref[...]
    ms = jnp.mean(x * x, axis=-1, keepdims=True)
    hn = (x * lax.rsqrt(ms + EPS) * g1_ref[...]).astype(BF16)

    pr = _dot(hn, wrow_ref[...])
    u_ref[...] = pr[:, :D_SSM]
    kc, vc, ks, kw = [pr[:, D_SSM + i * D_KV:D_SSM + (i + 1) * D_KV] for i in range(4)]
    kss = _split_dot(ks * ks, bd128_ref[...])
    ksn = ks * lax.rsqrt(kss * (1.0 / HEAD_DIM) + EPS) * gks_ref[...]
    kws = _split_dot(kw * kw, bd128_ref[...])
    kwn = kw * lax.rsqrt(kws * (1.0 / HEAD_DIM) + EPS) * gkw_ref[...]
    t0 = (pl.program_id(0) % nl) * tm
    tpos = t0 + lax.broadcasted_iota(jnp.int32, (tm, MAX_SEL_BLOCKS), 0)
    blk = lax.broadcasted_iota(jnp.int32, (tm, MAX_SEL_BLOCKS), 1)
    onehot = jnp.where(tpos // SEL_BLOCK == blk, 1.0, 0.0).astype(BF16)
    for g in range(N_KV):
        sl = slice(g * HEAD_DIM, (g + 1) * HEAD_DIM)
        kc_ref[g] = kc[:, sl]
        vc_ref[g] = vc[:, sl]
        ksa_ref[g] = jnp.concatenate([ksn[:, sl].astype(BF16), onehot], axis=1)
        kw_ref[g] = kwn[:, sl].astype(BF16)

    pc = _dot_nt(wcol_ref[...], hn)
    gq = gq_ref[...]
    for h in range(N_HEADS):
        sl = slice(h * HEAD_DIM, (h + 1) * HEAD_DIM)
        qh = pc[sl]
        ss = jnp.sum(qh * qh, axis=0, keepdims=True)
        qt_ref[h] = (qh * lax.rsqrt(ss * (1.0 / HEAD_DIM) + EPS) * gq[sl]).astype(BF16)
    ones_rows = jnp.where(lax.broadcasted_iota(jnp.int32, (V_ROWS - HEAD_DIM, tm), 0) == 0, 1.0, 0.0)
    for g in range(N_KV):
        for o_ref, base in ((vst_ref, D_ATT), (vwt_ref, D_ATT + D_KV)):
            vt = jnp.concatenate([pc[base + g * HEAD_DIM:base + (g + 1) * HEAD_DIM], ones_rows], axis=0)
            vt = vt.astype(BF16)
            for j in range(tm // NSA_TK):
                o_ref[g, j] = vt[:, j * NSA_TK:(j + 1) * NSA_TK]
        gb = D_ATT + 2 * D_KV + g * GATE_ROWS
        gate_ref[g] = _sigmoid(pc[gb:gb + GATE_ROWS])


def _in_proj(x2d, g1, wrow, wcol, gq, gks, gkw, bd128, *, bsz, seq):
    tm = 512
    nl = seq // tm
    n_tok = bsz * seq
    kern = functools.partial(_in_proj_kernel, tm=tm, nl=nl)
    row = lambda i: (i, 0)
    const = lambda i: (0, 0)
    bgl = lambda i: (i // nl, 0, i % nl, 0)
    n_col = wcol.shape[0]
    jt = tm // NSA_TK

    def kvspec(width):
        return pl.BlockSpec((None, N_KV, tm, width), bgl)

    def kvshape(width, dtype=BF16):
        return jax.ShapeDtypeStruct((bsz, N_KV, seq, width), dtype)

    vt_spec = pl.BlockSpec((None, N_KV, jt, V_ROWS, NSA_TK), lambda i: (i // nl, 0, i % nl, 0, 0))
    vt_shape = jax.ShapeDtypeStruct((bsz, N_KV, seq // NSA_TK, V_ROWS, NSA_TK), BF16)
    return pl.pallas_call(
        kern,
        grid=(n_tok // tm,),
        in_specs=[
            pl.BlockSpec((tm, D_MODEL), row),
            pl.BlockSpec((1, D_MODEL), const),
            pl.BlockSpec((D_MODEL, D_SSM + 4 * D_KV), const),
            pl.BlockSpec((n_col, D_MODEL), const),
            pl.BlockSpec((D_ATT, 1), const),
            pl.BlockSpec((1, D_KV), const),
            pl.BlockSpec((1, D_KV), const),
            pl.BlockSpec((D_KV, D_KV), const),
        ],
        out_specs=[
            pl.BlockSpec((tm, D_SSM), row),
            pl.BlockSpec((None, N_HEADS, HEAD_DIM, tm), lambda i: (i // nl, 0, 0, i % nl)),
            kvspec(HEAD_DIM), kvspec(HEAD_DIM), kvspec(2 * HEAD_DIM), kvspec(HEAD_DIM),
            vt_spec, vt_spec,
            pl.BlockSpec((None, N_KV, GATE_ROWS, tm), lambda i: (i // nl, 0, 0, i % nl)),
        ],
        out_shape=[
            jax.ShapeDtypeStruct((n_tok, D_SSM), F32),
            jax.ShapeDtypeStruct((bsz, N_HEADS, HEAD_DIM, seq), BF16),
            kvshape(HEAD_DIM, F32), kvshape(HEAD_DIM, F32), kvshape(2 * HEAD_DIM), kvshape(HEAD_DIM),
            vt_shape, vt_shape,
            jax.ShapeDtypeStruct((bsz, N_KV, GATE_ROWS, seq), F32),
        ],
        compiler_params=_params(("parallel",)),
        name="in_proj",
    )(x2d, g1, wrow, wcol, gq, gks, gkw, bd128)


def _s5_weights(lam_re, lam_im, log_step, b_re, b_im, c_re, c_im, d_skip, n_sub):
    q = SSM_Q
    lam = lax.complex(lam_re.astype(F32), lam_im.astype(F32))
    step = jnp.exp(log_step.astype(F32))[:, None]
    lam_bar = jnp.exp(lam * step)
    b_bar = ((lam_bar - 1.0) / lam)[..., None] * lax.complex(b_re.astype(F32), b_im.astype(F32))
    c = lax.complex(c_re.astype(F32), c_im.astype(F32))
    pows = [jnp.ones_like(lam_bar)]
    for _ in range(q):
        pows.append(pows[-1] * lam_bar)
    pw = jnp.stack(pows)
    lt, a8 = SSM_LT, LANES // SSM_CH
    hp = a8 * SSM_STATE
    e_lane = (jnp.arange(a8)[:, None] == jnp.arange(LANES)[None, :] // SSM_CH).astype(F32)
    e_state = (jnp.arange(a8)[:, None] == jnp.arange(hp)[None, :] // SSM_STATE).astype(F32)
    e_lane_t = jnp.tile(e_lane, (1, q))
    e_state_t = jnp.tile(e_state, (1, 2))

    kk = jnp.real(jnp.einsum('ghp,kgp,gpi->kghi', c, pw[:q], b_bar))
    km = kk.reshape(q, lt, a8, SSM_CH, SSM_CH).transpose(1, 4, 0, 2, 3).reshape(lt, SSM_CH, q, LANES)
    lag = jnp.arange(q)[None, :] - jnp.arange(q)[:, None]
    kg = km[:, :, jnp.clip(lag, 0, q - 1), :] * (lag >= 0)[None, None, :, :, None].astype(F32)
    kc = kg.transpose(0, 2, 1, 3, 4).reshape(lt, q, 1, SSM_CH, q * LANES)
    t_in = (kc * e_lane_t[None, None, :, None, :]).reshape(lt, q * LANES, q * LANES)

    wc = pw[q - 1 - jnp.arange(q)][..., None] * b_bar[None]
    wri = jnp.stack([jnp.real(wc), jnp.imag(wc)])
    wm = (wri.reshape(2, q, lt, a8, SSM_STATE, SSM_CH).transpose(2, 1, 5, 0, 3, 4)
          .reshape(lt, q, 1, SSM_CH, 2 * hp))
    w_loc = (wm * e_state_t[None, None, :, None, :]).reshape(lt, q * LANES, 2 * hp)

    cl = c[None] * pw[1:q + 1][:, :, None, :]
    cri = jnp.stack([jnp.real(cl), -jnp.imag(cl)])
    mm = (cri.reshape(2, q, lt, a8, SSM_CH, SSM_STATE).transpose(2, 0, 5, 1, 3, 4)
          .reshape(lt, 2, 1, SSM_STATE, q * LANES))
    m_st = (mm * e_lane_t[None, None, :, None, :]).reshape(lt, 2 * hp, q * LANES)

    n_lvl = max(1, (n_sub - 1).bit_length())
    lv = [pw[q]]
    for _ in range(n_lvl - 1):
        lv.append(lv[-1] * lv[-1])
    lvs = jnp.stack(lv).reshape(n_lvl, lt, 1, hp)
    pw_re = jnp.real(lvs).transpose(1, 0, 2, 3)
    pw_im = jnp.imag(lvs).transpose(1, 0, 2, 3)
    dvec = jnp.tile(d_skip.astype(F32).reshape(lt, 1, LANES), (1, 1, q))
    return w_loc.astype(BF16), t_in.astype(BF16), m_st.astype(BF16), pw_re, pw_im, dvec


def _s5_kernel(u_ref, w_ref, t_ref, m_ref, pwr_ref, pwi_ref, d_ref, y_ref, *, n_sub, n_lvl):
    half = (LANES // SSM_CH) * SSM_STATE
    q = SSM_Q
    u = jnp.concatenate([u_ref[pl.ds(s, n_sub, stride=q), :] for s in range(q)], axis=1)
    ub = u.astype(BF16)
    s_loc = _dot(ub, w_ref[...])
    re = s_loc[:, :half]
    im = s_loc[:, half:]
    rowi = lax.broadcasted_iota(jnp.int32, (n_sub, half), 0)
    for k in range(n_lvl):
        d = 1 << k
        ar = pwr_ref[k]
        ai = pwi_ref[k]
        keep = rowi >= d
        sre = jnp.where(keep, pltpu.roll(re, d, axis=0), 0.0)
        sim = jnp.where(keep, pltpu.roll(im, d, axis=0), 0.0)
        re, im = re + (ar * sre - ai * sim), im + (ar * sim + ai * sre)
    keep = rowi >= 1
    xre = jnp.where(keep, pltpu.roll(re, 1, axis=0), 0.0)
    xim = jnp.where(keep, pltpu.roll(im, 1, axis=0), 0.0)
    xst = jnp.concatenate([xre, xim], axis=1).astype(BF16)
    y = _dot(ub, t_ref[...]) + _dot(xst, m_ref[...]) + d_ref[...] * u
    for j in range(q):
        y_ref[pl.ds(j, n_sub, stride=q), :] = y[:, j * LANES:(j + 1) * LANES]


def _s5(u, w_loc, t_in, m_st, pw_re, pw_im, dvec, *, bsz, n_sub):
    q = SSM_Q
    n_lvl = pw_re.shape[1]
    kern = functools.partial(_s5_kernel, n_sub=n_sub, n_lvl=n_lvl)
    wide = q * LANES
    seq = n_sub * q
    return pl.pallas_call(
        kern,
        grid=(bsz, SSM_LT),
        in_specs=[
            pl.BlockSpec((None, seq, LANES), lambda b, l: (b, 0, l)),
            pl.BlockSpec((None, wide, wide), lambda b, l: (l, 0, 0)),
            pl.BlockSpec((None, wide, wide), lambda b, l: (l, 0, 0)),
            pl.BlockSpec((None, wide, wide), lambda b, l: (l, 0, 0)),
            pl.BlockSpec((None, n_lvl, 1, wide // 2), lambda b, l: (l, 0, 0, 0)),
            pl.BlockSpec((None, n_lvl, 1, wide // 2), lambda b, l: (l, 0, 0, 0)),
            pl.BlockSpec((None, 1, wide), lambda b, l: (l, 0, 0)),
        ],
        out_specs=pl.BlockSpec((None, seq, LANES), lambda b, l: (b, 0, l)),
        out_shape=jax.ShapeDtypeStruct((bsz, seq, D_SSM), F32),
        compiler_params=_params(("parallel", "parallel")),
        name="s5",
    )(u, w_loc, t_in, m_st, pw_re, pw_im, dvec)


def _compress_kernel(kc_ref, vc_ref, w1k_ref, w2k_ref, w1v_ref, w2vt_ref, posk_ref, posv_ref, gkc_ref,
                     kcmp_ref, vcmpt_ref, *, nch):
    half = CMP_STRIDE * HEAD_DIM

    def hidden(x_ref, w1_ref, pos_ref):
        x = jnp.concatenate([x_ref[pl.ds(j, nch, stride=CMP_STRIDE), :] for j in range(CMP_STRIDE)],
                            axis=1).astype(BF16)
        a = _dot(x, w1_ref[:half, :])
        b = _dot(x, w1_ref[half:, :])
        pv = _dot(pos_ref[...], w1_ref[...])[0:1, :]
        hid = a + pltpu.roll(b, nch - 1, axis=0) + pv
        return _gelu(hid).astype(BF16)

    k = _dot(hidden(kc_ref, w1k_ref, posk_ref), w2k_ref[...])
    ms = jnp.mean(k * k, axis=-1, keepdims=True)
    kcmp_ref[...] = (k * lax.rsqrt(ms + EPS) * gkc_ref[...]).astype(BF16)
    vt = _dot_nt(w2vt_ref[...], hidden(vc_ref, w1v_ref, posv_ref))
    coli = lax.broadcasted_iota(jnp.int32, vt.shape, 1)
    vcmpt_ref[...] = jnp.where(coli < nch - 1, vt, 0.0).astype(BF16)


def _compress(kcf, vcf, w1k, w2k, w1v, w2vt, posk, posv, gkc, *, bsz, nch):
    kern = functools.partial(_compress_kernel, nch=nch)
    wide = CMP_STRIDE * HEAD_DIM
    xspec = pl.BlockSpec((None, None, nch * CMP_STRIDE, HEAD_DIM), lambda b, g: (b, g, 0, 0))
    c2 = lambda b, g: (0, 0)
    return pl.pallas_call(
        kern,
        grid=(bsz, N_KV),
        in_specs=[
            xspec, xspec,
            pl.BlockSpec((2 * wide, CMP_HIDDEN), c2), pl.BlockSpec((CMP_HIDDEN, HEAD_DIM), c2),
            pl.BlockSpec((2 * wide, CMP_HIDDEN), c2), pl.BlockSpec((HEAD_DIM, CMP_HIDDEN), c2),
            pl.BlockSpec((8, 2 * wide), c2), pl.BlockSpec((8, 2 * wide), c2),
            pl.BlockSpec((1, HEAD_DIM), c2),
        ],
        out_specs=[pl.BlockSpec((None, None, nch, HEAD_DIM), lambda b, g: (b, g, 0, 0)),
                   pl.BlockSpec((None, None, HEAD_DIM, nch), lambda b, g: (b, g, 0, 0))],
        out_shape=[jax.ShapeDtypeStruct((bsz, N_KV, nch, HEAD_DIM), BF16),
                   jax.ShapeDtypeStruct((bsz, N_KV, HEAD_DIM, nch), BF16)],
        compiler_params=_params(("parallel", "parallel")),
        name="compress",
    )(kcf, vcf, w1k, w2k, w1v, w2vt, posk, posv, gkc)


def _nsa_kernel(qt_ref, kcmp_ref, vcmpt_ref, ksa_ref, kw_ref, vst_ref, vwt_ref, gate_ref, ovlt_ref, o_ref,
                accs_ref, accw_ref, *, tq, tk, nch):
    qi = pl.program_id(2)
    q0 = qi * tq
    rows = HPG * tq
    qt = jnp.concatenate([qt_ref[h] for h in range(HPG)], axis=1)
    tpos = q0 + lax.broadcasted_iota(jnp.int32, (1, rows), 1) % tq
    krow = lax.broadcasted_iota(jnp.int32, (tk, rows), 0)
    m0 = jnp.full((1, rows), NEG, F32)

    def run_tiles(k_ref, q_all, vt_ref, acc_ref, m, tiles):
        scores = []
        for kt, kind in tiles:
            kc = jnp.maximum(kt, 0) if kind in ('band', 'valid') else kt
            s = _dot(k_ref[pl.ds(pl.multiple_of(kc * tk, tk), tk), :], q_all)
            if kind == 'causal':
                s = jnp.where(kt * tk + krow <= tpos, s, NEG)
            elif kind == 'band':
                s = jnp.where((kt * tk + krow > tpos - WINDOW) & (kt >= 0), s, NEG)
            elif kind == 'valid':
                s = jnp.where(kt >= 0, s, NEG)
            scores.append((kc, s))
        for kc, s in scores:
            m_new = jnp.maximum(m, jnp.max(s, axis=0, keepdims=True))
            alpha = jnp.exp2(m - m_new)
            p = jnp.exp2(s - m_new).astype(BF16)
            acc_ref[...] = alpha * acc_ref[...] + _dot(vt_ref[kc], p)
            m = m_new
        return m

    accw_ref[...] = jnp.zeros_like(accw_ref)
    n_win = WINDOW // tk
    run_tiles(kw_ref, qt, vwt_ref, accw_ref, m0,
              [(qi - n_win, 'band')] + [(qi - n_win + t, 'valid') for t in range(1, n_win)] + [(qi, 'causal')])

    s = _dot(kcmp_ref[...], qt)
    cend = lax.broadcasted_iota(jnp.int32, (nch, rows), 0) * CMP_STRIDE + (CMP_BLOCK - 1)
    s = jnp.where(cend <= tpos, s, NEG)
    m = jnp.max(s, axis=0, keepdims=True)
    p = jnp.exp2(s - m)
    p = p * (1.0 / jnp.sum(p, axis=0, keepdims=True))
    p = jnp.where(tpos >= CMP_BLOCK - 1, p, 0.0)
    o_cmp = _dot(vcmpt_ref[...], p.astype(BF16))

    psum = p[:, 0:tq]
    for h in range(1, HPG):
        psum = psum + p[:, h * tq:(h + 1) * tq]
    hi = psum.astype(BF16)
    lo = (psum - hi.astype(F32)).astype(BF16)
    ovlt = ovlt_ref[...]
    imp = _dot(ovlt, hi) + _dot(ovlt, lo)
    nb = MAX_SEL_BLOCKS
    blk = lax.broadcasted_iota(jnp.int32, (nb, tq), 0)
    cur = (q0 + lax.broadcasted_iota(jnp.int32, (nb, tq), 1)) // SEL_BLOCK
    forced = (blk == 0) | (blk == cur) | (blk == cur - 1)
    imp = jnp.where(forced, FORCE, jnp.where(blk <= cur, imp, NEG))
    sub = 8
    groups = [imp[r:r + sub] for r in range(0, nb, sub)]
    ranks = [jnp.zeros((sub, tq), F32) for _ in groups]
    rowl = lax.broadcasted_iota(jnp.int32, (sub, tq), 0)
    for i in range(nb):
        ri = jnp.broadcast_to(imp[i:i + 1, :], (sub, tq))
        for gi, x in enumerate(groups):
            if i < gi * sub:
                ahead = ri >= x
            elif i >= (gi + 1) * sub:
                ahead = ri > x
            else:
                ahead = (ri > x) | ((ri == x) & (rowl > i - gi * sub))
            ranks[gi] = ranks[gi] + jnp.where(ahead, 1.0, 0.0)
    rank = jnp.concatenate(ranks, axis=0)
    sel = jnp.where(rank < N_SELECT, 0.0, NEG).astype(BF16)
    qa = jnp.concatenate([qt, jnp.concatenate([sel] * HPG, axis=1)], axis=0)

    accs_ref[...] = jnp.zeros_like(accs_ref)
    nu = NSA_UNROLL

    def sel_multi(j, m):
        return run_tiles(ksa_ref, qa, vst_ref, accs_ref, m, [(nu * j + t, None) for t in range(nu)])

    m_sel = lax.fori_loop(0, qi // nu, sel_multi, m0)
    for r in range(nu):
        @pl.when(qi % nu == r)
        def _(r=r):
            base = qi - r
            run_tiles(ksa_ref, qa, vst_ref, accs_ref, m_sel,
                      [(base + t, None) for t in range(r)] + [(qi, 'causal')])

    accs = accs_ref[...]
    accw = accw_ref[...]
    o_sel = accs[:HEAD_DIM] * (1.0 / accs[HEAD_DIM:HEAD_DIM + 1])
    o_win = accw[:HEAD_DIM] * (1.0 / accw[HEAD_DIM:HEAD_DIM + 1])
    gt = gate_ref[...]
    for h in range(HPG):
        c = slice(h * tq, (h + 1) * tq)
        o_ref[h * HEAD_DIM:(h + 1) * HEAD_DIM, :] = (
            gt[3 * h:3 * h + 1] * o_cmp[:, c] + gt[3 * h + 1:3 * h + 2] * o_sel[:, c]
            + gt[3 * h + 2:3 * h + 3] * o_win[:, c])


def _nsa(qt, kcmp, vcmpt, ksa, kw, vst, vwt, gate, ovlt, *, bsz, seq, nch):
    tq, tk = NSA_TQ, NSA_TK
    assert tq == tk and WINDOW % tk == 0
    nq = seq // tq
    kern = functools.partial(_nsa_kernel, tq=tq, tk=tk, nch=nch)
    full = lambda b, g, i: (b, g, 0, 0)
    full5 = lambda b, g, i: (b, g, 0, 0, 0)
    qd = HPG * HEAD_DIM
    rows = HPG * tq
    return pl.pallas_call(
        kern,
        grid=(bsz, N_KV, nq),
        in_specs=[
            pl.BlockSpec((None, HPG, HEAD_DIM, tq), lambda b, g, i: (b, g, 0, i)),
            pl.BlockSpec((None, None, nch, HEAD_DIM), full),
            pl.BlockSpec((None, None, HEAD_DIM, nch), full),
            pl.BlockSpec((None, None, seq, 2 * HEAD_DIM), full),
            pl.BlockSpec((None, None, seq, HEAD_DIM), full),
            pl.BlockSpec((None, None, seq // tk, V_ROWS, tk), full5),
            pl.BlockSpec((None, None, seq // tk, V_ROWS, tk), full5),
            pl.BlockSpec((None, None, GATE_ROWS, tq), lambda b, g, i: (b, g, 0, i)),
            pl.BlockSpec((MAX_SEL_BLOCKS, nch), lambda b, g, i: (0, 0)),
        ],
        out_specs=pl.BlockSpec((None, qd, tq), lambda b, g, i: (b, g, i)),
        out_shape=jax.ShapeDtypeStruct((bsz, D_ATT, seq), F32),
        scratch_shapes=[pltpu.VMEM((V_ROWS, rows), F32), pltpu.VMEM((V_ROWS, rows), F32)],
        compiler_params=_params(("parallel", "parallel", "arbitrary")),
        name="nsa",
    )(qt, kcmp, vcmpt, ksa, kw, vst, vwt, gate, ovlt)


def _out_proj_kernel(ys_ref, yat_ref, x_ref, wglu_ref, bglu_ref, gs_ref, ga_ref, wo_ref, g2_ref, wr_ref, br_ref,
                     x2e_ref):
    y = _gelu(ys_ref[...])
    y = y * _sigmoid(_dot(y.astype(BF16), wglu_ref[...]) + bglu_ref[...])
    ysn = y * lax.rsqrt(jnp.mean(y * y, axis=-1, keepdims=True) + EPS) * gs_ref[...]
    yat = yat_ref[...]
    yant = yat * lax.rsqrt(jnp.mean(yat * yat, axis=0, keepdims=True) + EPS) * ga_ref[...]
    yan = yant.T
    x2 = x_ref[...] + _dot(ysn.astype(BF16), wo_ref[:D_SSM, :]) + _dot(yan.astype(BF16), wo_ref[D_SSM:, :])
    x2e_ref[:, :D_MODEL] = x2
    h2 = (x2 * lax.rsqrt(jnp.mean(x2 * x2, axis=-1, keepdims=True) + EPS) * g2_ref[...]).astype(BF16)

    logits = _dot(h2, wr_ref[...]) + br_ref[...]
    lane = lax.broadcasted_iota(jnp.int32, logits.shape, 1).astype(F32)
    far = float(LANES)
    is_g = lane < N_EXP_GROUPS
    glog = jnp.where(is_g, logits, -jnp.inf)
    gmax = jnp.max(glog, axis=1, keepdims=True)
    gsum = jnp.sum(jnp.where(is_g, jnp.exp(logits - gmax), 0.0), axis=1, keepdims=True)
    gsel = jnp.min(jnp.where(glog == gmax, lane, far), axis=1, keepdims=True)
    gprob = 1.0 / gsum
    lo = ROUTER_OFF + EXPERTS_PER_GROUP * gsel
    in_e = (lane >= lo) & (lane < lo + EXPERTS_PER_GROUP)
    emax = jnp.max(jnp.where(in_e, logits, -jnp.inf), axis=1, keepdims=True)
    eexp = jnp.where(in_e, jnp.exp(logits - emax), 0.0)
    eprob = jnp.where(in_e, eexp / jnp.sum(eexp, axis=1, keepdims=True), -1.0)
    v1 = jnp.max(eprob, axis=1, keepdims=True)
    i1 = jnp.min(jnp.where(eprob == v1, lane, far), axis=1, keepdims=True)
    rest = jnp.where(lane == i1, -1.0, eprob)
    v2 = jnp.max(rest, axis=1, keepdims=True)
    i2 = jnp.min(jnp.where(rest == v2, lane, far), axis=1, keepdims=True)
    den = v1 + v2
    x2e_ref[:, D_MODEL:] = (jnp.where(lane == i1, v1 / den * gprob, 0.0)
                            + jnp.where(lane == i2, v2 / den * gprob, 0.0)
                            + jnp.where(lane == 0.0, gsel, 0.0))


def _out_proj(ys, yat, x2d, wglu, bglu, gs, ga, wo, g2, wr, br, *, seq):
    n_tok = x2d.shape[0]
    tm = 512
    nl = seq // tm
    row = lambda i: (i, 0)
    const = lambda i: (0, 0)
    return pl.pallas_call(
        _out_proj_kernel,
        grid=(n_tok // tm,),
        in_specs=[
            pl.BlockSpec((tm, D_SSM), row),
            pl.BlockSpec((None, D_ATT, tm), lambda i: (i // nl, 0, i % nl)),
            pl.BlockSpec((tm, D_MODEL), row),
            pl.BlockSpec((D_SSM, D_SSM), const),
            pl.BlockSpec((1, D_SSM), const),
            pl.BlockSpec((1, D_SSM), const),
            pl.BlockSpec((D_ATT, 1), const),
            pl.BlockSpec((D_SSM + D_ATT, D_MODEL), const),
            pl.BlockSpec((1, D_MODEL), const),
            pl.BlockSpec((D_MODEL, LANES), const),
            pl.BlockSpec((1, LANES), const),
        ],
        out_specs=pl.BlockSpec((tm, D_MODEL + LANES), row),
        out_shape=jax.ShapeDtypeStruct((n_tok, D_MODEL + LANES), F32),
        compiler_params=_params(("parallel",)),
        name="out_proj",
    )(ys, yat, x2d, wglu, bglu, gs, ga, wo, g2, wr, br)


def _moe_plan(gsel, n_tok):
    tmx = MOE_TM
    n_tiles = n_tok // tmx + N_EXP_GROUPS
    oh = (gsel[:, None] == jnp.arange(N_EXP_GROUPS)[None, :]).astype(jnp.int32)
    csum = jnp.cumsum(oh, axis=0)
    counts = csum[-1]
    rank = jnp.sum(csum * oh, axis=1) - 1
    nt = (counts + tmx - 1) // tmx
    tend = jnp.cumsum(nt)
    toff = tend - nt
    pos = jnp.sum(oh * toff[None, :], axis=1) * tmx + rank
    tok = jnp.zeros((n_tiles * tmx,), jnp.int32).at[pos].set(jnp.arange(n_tok, dtype=jnp.int32))
    tile = jnp.arange(n_tiles)
    grp = jnp.minimum(jnp.sum((tile[:, None] >= tend[None, :]).astype(jnp.int32), axis=1), N_EXP_GROUPS - 1)
    nval = jnp.clip(counts[grp] - (tile - toff[grp]) * tmx, 0, tmx)
    return grp.astype(jnp.int32), nval.astype(jnp.int32), tok


def _moe_kernel(grp_ref, nval_ref, tok_ref, x_hbm, g2_ref, wg_ref, wu_ref, wd_ref, o_hbm,
                xbuf, obuf, abuf, gsem, ssem, *, tmx, n_tiles):
    i = pl.program_id(0)
    slot = i % 2

    def gather_row(tile, r, dst_slot):
        t = tok_ref[tile * tmx + r]
        return pltpu.make_async_copy(x_hbm.at[pl.ds(t, 1), :], xbuf.at[dst_slot, pl.ds(r, 1), :], gsem.at[dst_slot])

    def scatter_row(tile, r, src_slot):
        t = tok_ref[tile * tmx + r]
        return pltpu.make_async_copy(obuf.at[src_slot, pl.ds(r, 1), :], o_hbm.at[pl.ds(t, 1), :], ssem.at[src_slot])

    def gather_wait(src_slot):
        pltpu.make_async_copy(x_hbm.at[pl.ds(0, tmx), :], xbuf.at[src_slot], gsem.at[src_slot]).wait()

    def scatter_wait(n, src_slot):
        n8 = pl.multiple_of((n // 8) * 8, 8)

        @pl.when(n8 > 0)
        def _():
            pltpu.make_async_copy(obuf.at[src_slot, pl.ds(0, n8), :], o_hbm.at[pl.ds(0, n8), :],
                                  ssem.at[src_slot]).wait()

        def one(r, c):
            pltpu.make_async_copy(obuf.at[src_slot, pl.ds(0, 1), :], o_hbm.at[pl.ds(0, 1), :],
                                  ssem.at[src_slot]).wait()
            return c

        lax.fori_loop(0, n - n8, one, 0)

    @pl.when(i == 0)
    def _():
        def body(r, c):
            gather_row(0, r, 0).start()
            return c

        lax.fori_loop(0, tmx, body, 0)

    gather_wait(slot)

    @pl.when(i >= 2)
    def _():
        scatter_wait(nval_ref[i - 2], slot)

    nxt = jnp.minimum(i + 1, n_tiles - 1)
    per = tmx // EXPERTS_PER_GROUP
    prev_full = (i >= 1) & (nval_ref[jnp.maximum(i - 1, 0)] == tmx)

    def experts(with_scatter):
        xe = xbuf[slot]
        x2 = xe[:, :D_MODEL]
        cw = xe[:, D_MODEL:]
        h = (x2 * lax.rsqrt(jnp.mean(x2 * x2, axis=-1, keepdims=True) + EPS) * g2_ref[...]).astype(BF16)
        lane = lax.broadcasted_iota(jnp.int32, cw.shape, 1)
        first = ROUTER_OFF + EXPERTS_PER_GROUP * grp_ref[i]
        for k in range(EXPERTS_PER_GROUP):
            for r in range(k * per, (k + 1) * per):
                gather_row(nxt, r, 1 - slot).start()
                if with_scatter:
                    scatter_row(i - 1, r, 1 - slot).start(priority=1)
            gate = _dot(h, wg_ref[k])
            up = _dot(h, wu_ref[k])
            ck = jnp.sum(jnp.where(lane == first + k, cw, 0.0), axis=1, keepdims=True)
            abuf[:, k * D_EXPERT:(k + 1) * D_EXPERT] = (gate * _sigmoid(gate) * up * ck).astype(BF16)
        obuf[slot] = x2 + _dot(abuf[...], wd_ref[...])

    @pl.when(prev_full)
    def _():
        experts(True)

    @pl.when(jnp.logical_not(prev_full))
    def _():
        experts(False)

    nv = nval_ref[i]

    @pl.when(nv < tmx)
    def _():
        def body(r, c):
            scatter_row(i, r, slot).start()
            return c

        lax.fori_loop(0, nv, body, 0)

    @pl.when(i == n_tiles - 1)
    def _():
        @pl.when(nv == tmx)
        def _():
            def body(r, c):
                scatter_row(i, r, slot).start()
                return c

            lax.fori_loop(0, tmx, body, 0)

        gather_wait(1 - slot)
        scatter_wait(nval_ref[i - 1], 1 - slot)
        scatter_wait(nv, slot)


def _moe(x2e, grp, nval, tok, g2, wg, wu, wd):
    n_tok = x2e.shape[0]
    tmx = MOE_TM
    n_tiles = grp.shape[0]
    kern = functools.partial(_moe_kernel, tmx=tmx, n_tiles=n_tiles)
    gk = EXPERTS_PER_GROUP * D_EXPERT
    grid_spec = pltpu.PrefetchScalarGridSpec(
        num_scalar_prefetch=3,
        grid=(n_tiles,),
        in_specs=[
            pl.BlockSpec(memory_space=pl.ANY),
            pl.BlockSpec((1, D_MODEL), lambda i, g, n, t: (0, 0)),
            pl.BlockSpec((None, EXPERTS_PER_GROUP, D_MODEL, D_EXPERT), lambda i, g, n, t: (g[i], 0, 0, 0)),
            pl.BlockSpec((None, EXPERTS_PER_GROUP, D_MODEL, D_EXPERT), lambda i, g, n, t: (g[i], 0, 0, 0)),
            pl.BlockSpec((None, gk, D_MODEL), lambda i, g, n, t: (g[i], 0, 0)),
        ],
        out_specs=pl.BlockSpec(memory_space=pl.ANY),
        scratch_shapes=[
            pltpu.VMEM((2, tmx, D_MODEL + LANES), F32),
            pltpu.VMEM((2, tmx, D_MODEL), F32),
            pltpu.VMEM((tmx, gk), BF16),
            pltpu.SemaphoreType.DMA((2,)),
            pltpu.SemaphoreType.DMA((2,)),
        ],
    )
    return pl.pallas_call(
        kern,
        grid_spec=grid_spec,
        out_shape=jax.ShapeDtypeStruct((n_tok, D_MODEL), F32),
        compiler_params=_params(("arbitrary",)),
        name="moe",
    )(grp, nval, tok, x2e, g2, wg, wu, wd)


def _block_diag_ones(n, blk):
    i = jnp.arange(n) // blk
    return (i[:, None] == i[None, :]).astype(BF16)


def _layer(x, norm1_g, w_in, lam_re, lam_im, log_step, b_re, b_im, c_re, c_im, d_skip,
           w_glu, b_glu, g_q, g_kc, g_ks, g_kw, pos_k, pos_v, w_ck1, w_ck2, w_cv1, w_cv2,
           out_g_ssm, out_g_att, w_out, norm2_g, w_grp, b_grp, w_exp, b_exp, w_gate, w_up, w_down):
    bsz, seq, _ = x.shape
    assert seq % 512 == 0 and seq // SEL_BLOCK <= MAX_SEL_BLOCKS
    n_tok = bsz * seq
    x2d = x.reshape(n_tok, D_MODEL)
    q8 = SSM_Q
    n_sub = seq // q8
    nch = seq // CMP_STRIDE

    o_q = D_SSM
    o_kv = D_SSM + D_ATT
    o_gt = o_kv + 6 * D_KV
    kv = lambda i: w_in[:, o_kv + i * D_KV:o_kv + (i + 1) * D_KV]
    wrow = jnp.concatenate([w_in[:, :o_q], kv(0), kv(1), kv(2), kv(4)], axis=1).astype(BF16)
    per_g = HPG * N_BRANCH
    wgt = jnp.zeros((D_MODEL, N_KV * GATE_ROWS), F32)
    for g in range(N_KV):
        wgt = wgt.at[:, g * GATE_ROWS:g * GATE_ROWS + per_g].set(w_in[:, o_gt + g * per_g:o_gt + (g + 1) * per_g])
    wcol = jnp.concatenate([w_in[:, o_q:o_kv], kv(3), kv(5), wgt], axis=1).T.astype(BF16)
    qscale = (HEAD_DIM ** -0.5) * math.log2(math.e)
    gq = (jnp.tile(g_q.astype(F32), N_HEADS) * qscale).reshape(D_ATT, 1)
    gks = jnp.tile(g_ks.astype(F32), N_KV).reshape(1, D_KV)
    gkw = jnp.tile(g_kw.astype(F32), N_KV).reshape(1, D_KV)

    u, qt, kc, vc, ksa, kw, vst, vwt, gate = _in_proj(
        x2d, norm1_g.reshape(1, D_MODEL), wrow, wcol, gq, gks, gkw,
        _block_diag_ones(D_KV, HEAD_DIM), bsz=bsz, seq=seq)

    w_loc, t_in, m_st, pw_re, pw_im, dvec = _s5_weights(
        lam_re, lam_im, log_step, b_re, b_im, c_re, c_im, d_skip, n_sub)
    ys = _s5(u.reshape(bsz, seq, D_SSM), w_loc, t_in, m_st, pw_re, pw_im, dvec,
             bsz=bsz, n_sub=n_sub).reshape(n_tok, D_SSM)

    wide = CMP_STRIDE * HEAD_DIM
    pad8 = lambda p: jnp.zeros((8, 2 * wide), F32).at[0].set(p.reshape(-1)).astype(BF16)
    kcmp, vcmpt = _compress(
        kc, vc,
        w_ck1.astype(BF16), w_ck2.astype(BF16), w_cv1.astype(BF16), w_cv2.T.astype(BF16),
        pad8(pos_k), pad8(pos_v), g_kc.astype(F32).reshape(1, HEAD_DIM), bsz=bsz, nch=nch)
    cstart = jnp.arange(nch) * CMP_STRIDE
    sstart = jnp.arange(MAX_SEL_BLOCKS) * SEL_BLOCK
    ovlt = ((cstart[None, :] < sstart[:, None] + SEL_BLOCK) & (cstart[None, :] + CMP_BLOCK > sstart[:, None])
            & (jnp.arange(MAX_SEL_BLOCKS)[:, None] < seq // SEL_BLOCK)
            & (jnp.arange(nch)[None, :] < nch - 1)).astype(BF16)
    yat = _nsa(qt, kcmp, vcmpt, ksa, kw, vst, vwt, gate, ovlt, bsz=bsz, seq=seq, nch=nch)

    wr = jnp.zeros((D_MODEL, LANES), F32)
    wr = wr.at[:, :N_EXP_GROUPS].set(w_grp).at[:, ROUTER_OFF:ROUTER_OFF + N_EXPERTS].set(w_exp).astype(BF16)
    br = jnp.zeros((1, LANES), F32)
    br = br.at[0, :N_EXP_GROUPS].set(b_grp).at[0, ROUTER_OFF:ROUTER_OFF + N_EXPERTS].set(b_exp)
    g2 = norm2_g.reshape(1, D_MODEL).astype(F32)
    x2e = _out_proj(
        ys, yat, x2d, w_glu.astype(BF16), b_glu.reshape(1, D_SSM).astype(F32),
        out_g_ssm.reshape(1, D_SSM).astype(F32), out_g_att.reshape(D_ATT, 1).astype(F32),
        w_out.astype(BF16), g2, wr, br, seq=seq)

    grp, nval, tok = _moe_plan(x2e[:, D_MODEL].astype(jnp.int32), n_tok)
    gshape = (N_EXP_GROUPS, EXPERTS_PER_GROUP, D_MODEL, D_EXPERT)
    out = _moe(x2e, grp, nval, tok, g2, w_gate.astype(BF16).reshape(gshape), w_up.astype(BF16).reshape(gshape),
               w_down.astype(BF16).reshape(N_EXP_GROUPS, EXPERTS_PER_GROUP * D_EXPERT, D_MODEL))
    return out.reshape(bsz, seq, D_MODEL)


def kernel(x, norm1_g, w_in, lam_re, lam_im, log_step, b_re, b_im, c_re, c_im, d_skip, w_glu, b_glu, g_q, g_kc, g_ks, g_kw, pos_k, pos_v, w_ck1, w_ck2, w_cv1, w_cv2, out_g_ssm, out_g_att, w_out, norm2_g, w_grp, b_grp, w_exp, b_exp, w_gate, w_up, w_down):
    depth = norm1_g.shape[0]
    for l in range(depth):
        x = _layer(x, norm1_g[l], w_in[l], lam_re[l], lam_im[l], log_step[l], b_re[l], b_im[l], c_re[l],
                   c_im[l], d_skip[l], w_glu[l], b_glu[l], g_q[l], g_kc[l], g_ks[l], g_kw[l], pos_k[l],
                   pos_v[l], w_ck1[l], w_ck2[l], w_cv1[l], w_cv2[l], out_g_ssm[l], out_g_att[l], w_out[l],
                   norm2_g[l], w_grp[l], b_grp[l], w_exp[l], b_exp[l], w_gate[l], w_up[l], w_down[l])
    return x
```

```python
import functools
import math

import jax
import jax.numpy as jnp
from jax import lax
from jax.experimental import pallas as pl
from jax.experimental.pallas import tpu as pltpu

D_MODEL = 1024
D_SSM = 512
SSM_CH = 16
SSM_GROUPS = D_SSM // SSM_CH
SSM_STATE = 64
D_ATT = 512
HEAD_DIM = 64
N_HEADS = D_ATT // HEAD_DIM
N_KV = 2
HPG = N_HEADS // N_KV
D_KV = N_KV * HEAD_DIM
N_BRANCH = 3
CMP_STRIDE = 16
CMP_BLOCK = 2 * CMP_STRIDE
CMP_HIDDEN = 256
SEL_BLOCK = 64
N_SELECT = 16
WINDOW = 512
N_EXP_GROUPS = 4
EXPERTS_PER_GROUP = 8
N_EXPERTS = N_EXP_GROUPS * EXPERTS_PER_GROUP
D_EXPERT = 256
EPS = 1e-6
NEG = -1e30
FORCE = 1e9

LANES = 128
SSM_Q = 8
SSM_LT = D_SSM // LANES
ROUTER_OFF = N_EXP_GROUPS
NSA_TQ = 256
NSA_TK = 256
NSA_UNROLL = 8
V_ROWS = HEAD_DIM + 16
MAX_SEL_BLOCKS = 64
MOE_TM = 256
GATE_ROWS = 16
VMEM_LIMIT = 56 * 1024 * 1024

F32 = jnp.float32
BF16 = jnp.bfloat16


def _dot(a, b):
    return jnp.dot(a, b, preferred_element_type=F32)


def _dot_nt(a, b):
    return lax.dot_general(a, b, (((1,), (1,)), ((), ())), preferred_element_type=F32)


def _split_dot(x, w):
    hi = x.astype(BF16)
    lo = (x - hi.astype(F32)).astype(BF16)
    return _dot(hi, w) + _dot(lo, w)


def _gelu(x):
    c = math.sqrt(2.0 / math.pi)
    return 0.5 * x * (1.0 + jnp.tanh(c * (x + 0.044715 * (x * x * x))))


def _sigmoid(x):
    return 1.0 / (1.0 + jnp.exp(-x))


def _params(sem):
    return pltpu.CompilerParams(dimension_semantics=sem, vmem_limit_bytes=VMEM_LIMIT)


def _in_proj_kernel(x_ref, g1_ref, wrow_ref, wcol_ref, gq_ref, gks_ref, gkw_ref, bd128_ref,
                    u_ref, qt_ref, kc_ref, vc_ref, ksa_ref, kw_ref, vst_ref, vwt_ref, gate_ref, *, tm, nl):
    x = x_ref[...]
    ms = jnp.mean(x * x, axis=-1, keepdims=True)
    hn = (x * lax.rsqrt(ms + EPS) * g1_ref[...]).astype(BF16)

    pr = _dot(hn, wrow_ref[...])
    u_ref[...] = pr[:, :D_SSM]
    kc, vc, ks, kw = [pr[:, D_SSM + i * D_KV:D_SSM + (i + 1) * D_KV] for i in range(4)]
    kss = _split_dot(ks * ks, bd128_ref[...])
    ksn = ks * lax.rsqrt(kss * (1.0 / HEAD_DIM) + EPS) * gks_ref[...]
    kws = _split_dot(kw * kw, bd128_ref[...])
    kwn = kw * lax.rsqrt(kws * (1.0 / HEAD_DIM) + EPS) * gkw_ref[...]
    t0 = (pl.program_id(0) % nl) * tm
    tpos = t0 + lax.broadcasted_iota(jnp.int32, (tm, MAX_SEL_BLOCKS), 0)
    blk = lax.broadcasted_iota(jnp.int32, (tm, MAX_SEL_BLOCKS), 1)
    onehot = jnp.where(tpos // SEL_BLOCK == blk, 1.0, 0.0).astype(BF16)
    for g in range(N_KV):
        sl = slice(g * HEAD_DIM, (g + 1) * HEAD_DIM)
        kc_ref[g] = kc[:, sl]
        vc_ref[g] = vc[:, sl]
        ksa_ref[g] = jnp.concatenate([ksn[:, sl].astype(BF16), onehot], axis=1)
        kw_ref[g] = kwn[:, sl].astype(BF16)

    pc = _dot_nt(wcol_ref[...], hn)
    gq = gq_ref[...]
    for h in range(N_HEADS):
        sl = slice(h * HEAD_DIM, (h + 1) * HEAD_DIM)
        qh = pc[sl]
        ss = jnp.sum(qh * qh, axis=0, keepdims=True)
        qt_ref[h] = (qh * lax.rsqrt(ss * (1.0 / HEAD_DIM) + EPS) * gq[sl]).astype(BF16)
    ones_rows = jnp.where(lax.broadcasted_iota(jnp.int32, (V_ROWS - HEAD_DIM, tm), 0) == 0, 1.0, 0.0)
    for g in range(N_KV):
        for o_ref, base in ((vst_ref, D_ATT), (vwt_ref, D_ATT + D_KV)):
            vt = jnp.concatenate([pc[base + g * HEAD_DIM:base + (g + 1) * HEAD_DIM], ones_rows], axis=0)
            vt = vt.astype(BF16)
            for j in range(tm // NSA_TK):
                o_ref[g, j] = vt[:, j * NSA_TK:(j + 1) * NSA_TK]
        gb = D_ATT + 2 * D_KV + g * GATE_ROWS
        gate_ref[g] = _sigmoid(pc[gb:gb + GATE_ROWS])


def _in_proj(x2d, g1, wrow, wcol, gq, gks, gkw, bd128, *, bsz, seq):
    tm = 512
    nl = seq // tm
    n_tok = bsz * seq
    kern = functools.partial(_in_proj_kernel, tm=tm, nl=nl)
    row = lambda i: (i, 0)
    const = lambda i: (0, 0)
    bgl = lambda i: (i // nl, 0, i % nl, 0)
    n_col = wcol.shape[0]
    jt = tm // NSA_TK

    def kvspec(width):
        return pl.BlockSpec((None, N_KV, tm, width), bgl)

    def kvshape(width, dtype=BF16):
        return jax.ShapeDtypeStruct((bsz, N_KV, seq, width), dtype)

    vt_spec = pl.BlockSpec((None, N_KV, jt, V_ROWS, NSA_TK), lambda i: (i // nl, 0, i % nl, 0, 0))
    vt_shape = jax.ShapeDtypeStruct((bsz, N_KV, seq // NSA_TK, V_ROWS, NSA_TK), BF16)
    return pl.pallas_call(
        kern,
        grid=(n_tok // tm,),
        in_specs=[
            pl.BlockSpec((tm, D_MODEL), row),
            pl.BlockSpec((1, D_MODEL), const),
            pl.BlockSpec((D_MODEL, D_SSM + 4 * D_KV), const),
            pl.BlockSpec((n_col, D_MODEL), const),
            pl.BlockSpec((D_ATT, 1), const),
            pl.BlockSpec((1, D_KV), const),
            pl.BlockSpec((1, D_KV), const),
            pl.BlockSpec((D_KV, D_KV), const),
        ],
        out_specs=[
            pl.BlockSpec((tm, D_SSM), row),
            pl.BlockSpec((None, N_HEADS, HEAD_DIM, tm), lambda i: (i // nl, 0, 0, i % nl)),
            kvspec(HEAD_DIM), kvspec(HEAD_DIM), kvspec(2 * HEAD_DIM), kvspec(HEAD_DIM),
            vt_spec, vt_spec,
            pl.BlockSpec((None, N_KV, GATE_ROWS, tm), lambda i: (i // nl, 0, 0, i % nl)),
        ],
        out_shape=[
            jax.ShapeDtypeStruct((n_tok, D_SSM), F32),
            jax.ShapeDtypeStruct((bsz, N_HEADS, HEAD_DIM, seq), BF16),
            kvshape(HEAD_DIM, F32), kvshape(HEAD_DIM, F32), kvshape(2 * HEAD_DIM), kvshape(HEAD_DIM),
            vt_shape, vt_shape,
            jax.ShapeDtypeStruct((bsz, N_KV, GATE_ROWS, seq), F32),
        ],
        compiler_params=_params(("parallel",)),
        name="in_proj",
    )(x2d, g1, wrow, wcol, gq, gks, gkw, bd128)


def _s5_weights(lam_re, lam_im, log_step, b_re, b_im, c_re, c_im, d_skip, n_sub):
    q = SSM_Q
    lam = lax.complex(lam_re.astype(F32), lam_im.astype(F32))
    step = jnp.exp(log_step.astype(F32))[:, None]
    lam_bar = jnp.exp(lam * step)
    b_bar = ((lam_bar - 1.0) / lam)[..., None] * lax.complex(b_re.astype(F32), b_im.astype(F32))
    c = lax.complex(c_re.astype(F32), c_im.astype(F32))
    pows = [jnp.ones_like(lam_bar)]
    for _ in range(q):
        pows.append(pows[-1] * lam_bar)
    pw = jnp.stack(pows)
    lt, a8 = SSM_LT, LANES // SSM_CH
    hp = a8 * SSM_STATE
    e_lane = (jnp.arange(a8)[:, None] == jnp.arange(LANES)[None, :] // SSM_CH).astype(F32)
    e_state = (jnp.arange(a8)[:, None] == jnp.arange(hp)[None, :] // SSM_STATE).astype(F32)
    e_lane_t = jnp.tile(e_lane, (1, q))
    e_state_t = jnp.tile(e_state, (1, 2))

    kk = jnp.real(jnp.einsum('ghp,kgp,gpi->kghi', c, pw[:q], b_bar))
    km = kk.reshape(q, lt, a8, SSM_CH, SSM_CH).transpose(1, 4, 0, 2, 3).reshape(lt, SSM_CH, q, LANES)
    lag = jnp.arange(q)[None, :] - jnp.arange(q)[:, None]
    kg = km[:, :, jnp.clip(lag, 0, q - 1), :] * (lag >= 0)[None, None, :, :, None].astype(F32)
    kc = kg.transpose(0, 2, 1, 3, 4).reshape(lt, q, 1, SSM_CH, q * LANES)
    t_in = (kc * e_lane_t[None, None, :, None, :]).reshape(lt, q * LANES, q * LANES)

    wc = pw[q - 1 - jnp.arange(q)][..., None] * b_bar[None]
    wri = jnp.stack([jnp.real(wc), jnp.imag(wc)])
    wm = (wri.reshape(2, q, lt, a8, SSM_STATE, SSM_CH).transpose(2, 1, 5, 0, 3, 4)
          .reshape(lt, q, 1, SSM_CH, 2 * hp))
    w_loc = (wm * e_state_t[None, None, :, None, :]).reshape(lt, q * LANES, 2 * hp)

    cl = c[None] * pw[1:q + 1][:, :, None, :]
    cri = jnp.stack([jnp.real(cl), -jnp.imag(cl)])
    mm = (cri.reshape(2, q, lt, a8, SSM_CH, SSM_STATE).transpose(2, 0, 5, 1, 3, 4)
          .reshape(lt, 2, 1, SSM_STATE, q * LANES))
    m_st = (mm * e_lane_t[None, None, :, None, :]).reshape(lt, 2 * hp, q * LANES)

    n_lvl = max(1, (n_sub - 1).bit_length())
    lv = [pw[q]]
    for _ in range(n_lvl - 1):
        lv.append(lv[-1] * lv[-1])
    lvs = jnp.stack(lv).reshape(n_lvl, lt, 1, hp)
    pw_re = jnp.real(lvs).transpose(1, 0, 2, 3)
    pw_im = jnp.imag(lvs).transpose(1, 0, 2, 3)
    dvec = jnp.tile(d_skip.astype(F32).reshape(lt, 1, LANES), (1, 1, q))
    return w_loc.astype(BF16), t_in.astype(BF16), m_st.astype(BF16), pw_re, pw_im, dvec


def _s5_kernel(u_ref, w_ref, t_ref, m_ref, pwr_ref, pwi_ref, d_ref, y_ref, *, n_sub, n_lvl):
    half = (LANES // SSM_CH) * SSM_STATE
    q = SSM_Q
    u = jnp.concatenate([u_ref[pl.ds(s, n_sub, stride=q), :] for s in range(q)], axis=1)
    ub = u.astype(BF16)
    s_loc = _dot(ub, w_ref[...])
    re = s_loc[:, :half]
    im = s_loc[:, half:]
    rowi = lax.broadcasted_iota(jnp.int32, (n_sub, half), 0)
    for k in range(n_lvl):
        d = 1 << k
        ar = pwr_ref[k]
        ai = pwi_ref[k]
        keep = rowi >= d
        sre = jnp.where(keep, pltpu.roll(re, d, axis=0), 0.0)
        sim = jnp.where(keep, pltpu.roll(im, d, axis=0), 0.0)
        re, im = re + (ar * sre - ai * sim), im + (ar * sim + ai * sre)
    keep = rowi >= 1
    xre = jnp.where(keep, pltpu.roll(re, 1, axis=0), 0.0)
    xim = jnp.where(keep, pltpu.roll(im, 1, axis=0), 0.0)
    xst = jnp.concatenate([xre, xim], axis=1).astype(BF16)
    y = _dot(ub, t_ref[...]) + _dot(xst, m_ref[...]) + d_ref[...] * u
    for j in range(q):
        y_ref[pl.ds(j, n_sub, stride=q), :] = y[:, j * LANES:(j + 1) * LANES]


def _s5(u, w_loc, t_in, m_st, pw_re, pw_im, dvec, *, bsz, n_sub):
    q = SSM_Q
    n_lvl = pw_re.shape[1]
    kern = functools.partial(_s5_kernel, n_sub=n_sub, n_lvl=n_lvl)
    wide = q * LANES
    seq = n_sub * q
    return pl.pallas_call(
        kern,
        grid=(bsz, SSM_LT),
        in_specs=[
            pl.BlockSpec((None, seq, LANES), lambda b, l: (b, 0, l)),
            pl.BlockSpec((None, wide, wide), lambda b, l: (l, 0, 0)),
            pl.BlockSpec((None, wide, wide), lambda b, l: (l, 0, 0)),
            pl.BlockSpec((None, wide, wide), lambda b, l: (l, 0, 0)),
            pl.BlockSpec((None, n_lvl, 1, wide // 2), lambda b, l: (l, 0, 0, 0)),
            pl.BlockSpec((None, n_lvl, 1, wide // 2), lambda b, l: (l, 0, 0, 0)),
            pl.BlockSpec((None, 1, wide), lambda b, l: (l, 0, 0)),
        ],
        out_specs=pl.BlockSpec((None, seq, LANES), lambda b, l: (b, 0, l)),
        out_shape=jax.ShapeDtypeStruct((bsz, seq, D_SSM), F32),
        compiler_params=_params(("parallel", "parallel")),
        name="s5",
    )(u, w_loc, t_in, m_st, pw_re, pw_im, dvec)


def _compress_kernel(kc_ref, vc_ref, w1k_ref, w2k_ref, w1v_ref, w2vt_ref, posk_ref, posv_ref, gkc_ref,
                     kcmp_ref, vcmpt_ref, *, nch):
    half = CMP_STRIDE * HEAD_DIM

    def hidden(x_ref, w1_ref, pos_ref):
        x = jnp.concatenate([x_ref[pl.ds(j, nch, stride=CMP_STRIDE), :] for j in range(CMP_STRIDE)],
                            axis=1).astype(BF16)
        a = _dot(x, w1_ref[:half, :])
        b = _dot(x, w1_ref[half:, :])
        pv = _dot(pos_ref[...], w1_ref[...])[0:1, :]
        hid = a + pltpu.roll(b, nch - 1, axis=0) + pv
        return _gelu(hid).astype(BF16)

    k = _dot(hidden(kc_ref, w1k_ref, posk_ref), w2k_ref[...])
    ms = jnp.mean(k * k, axis=-1, keepdims=True)
    kcmp_ref[...] = (k * lax.rsqrt(ms + EPS) * gkc_ref[...]).astype(BF16)
    vt = _dot_nt(w2vt_ref[...], hidden(vc_ref, w1v_ref, posv_ref))
    coli = lax.broadcasted_iota(jnp.int32, vt.shape, 1)
    vcmpt_ref[...] = jnp.where(coli < nch - 1, vt, 0.0).astype(BF16)


def _compress(kcf, vcf, w1k, w2k, w1v, w2vt, posk, posv, gkc, *, bsz, nch):
    kern = functools.partial(_compress_kernel, nch=nch)
    wide = CMP_STRIDE * HEAD_DIM
    xspec = pl.BlockSpec((None, None, nch * CMP_STRIDE, HEAD_DIM), lambda b, g: (b, g, 0, 0))
    c2 = lambda b, g: (0, 0)
    return pl.pallas_call(
        kern,
        grid=(bsz, N_KV),
        in_specs=[
            xspec, xspec,
            pl.BlockSpec((2 * wide, CMP_HIDDEN), c2), pl.BlockSpec((CMP_HIDDEN, HEAD_DIM), c2),
            pl.BlockSpec((2 * wide, CMP_HIDDEN), c2), pl.BlockSpec((HEAD_DIM, CMP_HIDDEN), c2),
            pl.BlockSpec((8, 2 * wide), c2), pl.BlockSpec((8, 2 * wide), c2),
            pl.BlockSpec((1, HEAD_DIM), c2),
        ],
        out_specs=[pl.BlockSpec((None, None, nch, HEAD_DIM), lambda b, g: (b, g, 0, 0)),
                   pl.BlockSpec((None, None, HEAD_DIM, nch), lambda b, g: (b, g, 0, 0))],
        out_shape=[jax.ShapeDtypeStruct((bsz, N_KV, nch, HEAD_DIM), BF16),
                   jax.ShapeDtypeStruct((bsz, N_KV, HEAD_DIM, nch), BF16)],
        compiler_params=_params(("parallel", "parallel")),
        name="compress",
    )(kcf, vcf, w1k, w2k, w1v, w2vt, posk, posv, gkc)


def _nsa_kernel(qt_ref, kcmp_ref, vcmpt_ref, ksa_ref, kw_ref, vst_ref, vwt_ref, gate_ref, ovlt_ref, o_ref,
                accs_ref, accw_ref, sel_ref, *, tq, tk, nch):
    qi = pl.program_id(2)
    q0 = qi * tq
    rows = HPG * tq
    qt = jnp.concatenate([qt_ref[h] for h in range(HPG)], axis=1)
    tpos = q0 + lax.broadcasted_iota(jnp.int32, (1, rows), 1) % tq
    krow = lax.broadcasted_iota(jnp.int32, (tk, rows), 0)
    m0 = jnp.full((1, rows), NEG, F32)

    def run_tiles(k_ref, q_all, vt_ref, acc_ref, m, tiles):
        scores = []
        for kt, kind in tiles:
            kc = jnp.maximum(kt, 0) if kind in ('band', 'valid') else kt
            s = _dot(k_ref[pl.ds(pl.multiple_of(kc * tk, tk), tk), :], q_all)
            if kind == 'causal':
                s = jnp.where(kt * tk + krow <= tpos, s, NEG)
            elif kind == 'band':
                s = jnp.where((kt * tk + krow > tpos - WINDOW) & (kt >= 0), s, NEG)
            elif kind == 'valid':
                s = jnp.where(kt >= 0, s, NEG)
            scores.append((kc, s))
        for kc, s in scores:
            m_new = jnp.maximum(m, jnp.max(s, axis=0, keepdims=True))
            alpha = jnp.exp2(m - m_new)
            p = jnp.exp2(s - m_new).astype(BF16)
            acc_ref[...] = alpha * acc_ref[...] + _dot(vt_ref[kc], p)
            m = m_new
        return m

    accw_ref[...] = jnp.zeros_like(accw_ref)
    n_win = WINDOW // tk
    run_tiles(kw_ref, qt, vwt_ref, accw_ref, m0,
              [(qi - n_win, 'band')] + [(qi - n_win + t, 'valid') for t in range(1, n_win)] + [(qi, 'causal')])

    s = _dot(kcmp_ref[...], qt)
    cend = lax.broadcasted_iota(jnp.int32, (nch, rows), 0) * CMP_STRIDE + (CMP_BLOCK - 1)
    s = jnp.where(cend <= tpos, s, NEG)
    m = jnp.max(s, axis=0, keepdims=True)
    p = jnp.exp2(s - m)
    p = p * jnp.where(tpos >= CMP_BLOCK - 1, 1.0 / jnp.sum(p, axis=0, keepdims=True), 0.0)
    o_cmp = _dot(vcmpt_ref[...], p.astype(BF16))

    psum = p[:, 0:tq]
    for h in range(1, HPG):
        psum = psum + p[:, h * tq:(h + 1) * tq]
    hi = psum.astype(BF16)
    lo = (psum - hi.astype(F32)).astype(BF16)
    ovlt = ovlt_ref[...]
    imp = _dot(ovlt, hi) + _dot(ovlt, lo)
    nb = MAX_SEL_BLOCKS
    blk = lax.broadcasted_iota(jnp.int32, (nb, tq), 0)
    cur = (q0 + lax.broadcasted_iota(jnp.int32, (nb, tq), 1)) // SEL_BLOCK
    forced = (blk == 0) | (blk == cur) | (blk == cur - 1)
    imp = jnp.where(forced, FORCE, jnp.where(blk <= cur, imp, NEG))
    sub = 8
    rowl = lax.broadcasted_iota(jnp.int32, (sub, tq), 0)
    n_seen = (q0 + tq - 1) // SEL_BLOCK + 1
    sel_ref[...] = jnp.zeros_like(sel_ref)
    for nbv in range(2 * N_SELECT, nb + 1, N_SELECT):
        @pl.when((n_seen > nbv - N_SELECT) & (n_seen <= nbv))
        def _(nbv=nbv):
            groups = [imp[r:r + sub] for r in range(0, nbv, sub)]
            ranks = [jnp.zeros((sub, tq), F32) for _ in groups]
            for i in range(nbv):
                ri = jnp.broadcast_to(imp[i:i + 1, :], (sub, tq))
                for gi, x in enumerate(groups):
                    if i < gi * sub:
                        ahead = ri >= x
                    elif i >= (gi + 1) * sub:
                        ahead = ri > x
                    else:
                        ahead = (ri > x) | ((ri == x) & (rowl > i - gi * sub))
                    ranks[gi] = ranks[gi] + jnp.where(ahead, 1.0, 0.0)
            rank = jnp.concatenate(ranks, axis=0)
            sel_ref[0:nbv, :] = jnp.where(rank < N_SELECT, 0.0, NEG).astype(BF16)
    sel = sel_ref[...]
    qa = jnp.concatenate([qt, jnp.concatenate([sel] * HPG, axis=1)], axis=0)

    accs_ref[...] = jnp.zeros_like(accs_ref)
    nu = NSA_UNROLL

    def sel_multi(j, m):
        return run_tiles(ksa_ref, qa, vst_ref, accs_ref, m, [(nu * j + t, None) for t in range(nu)])

    m_sel = lax.fori_loop(0, qi // nu, sel_multi, m0)
    for r in range(nu):
        @pl.when(qi % nu == r)
        def _(r=r):
            base = qi - r
            run_tiles(ksa_ref, qa, vst_ref, accs_ref, m_sel,
                      [(base + t, None) for t in range(r)] + [(qi, 'causal')])

    accs = accs_ref[...]
    accw = accw_ref[...]
    o_sel = accs[:HEAD_DIM] * (1.0 / accs[HEAD_DIM:HEAD_DIM + 1])
    o_win = accw[:HEAD_DIM] * (1.0 / accw[HEAD_DIM:HEAD_DIM + 1])
    gt = gate_ref[...]
    for h in range(HPG):
        c = slice(h * tq, (h + 1) * tq)
        o_ref[h * HEAD_DIM:(h + 1) * HEAD_DIM, :] = (
            gt[3 * h:3 * h + 1] * o_cmp[:, c] + gt[3 * h + 1:3 * h + 2] * o_sel[:, c]
            + gt[3 * h + 2:3 * h + 3] * o_win[:, c])


def _nsa(qt, kcmp, vcmpt, ksa, kw, vst, vwt, gate, ovlt, *, bsz, seq, nch):
    tq, tk = NSA_TQ, NSA_TK
    assert tq == tk and WINDOW % tk == 0
    nq = seq // tq
    kern = functools.partial(_nsa_kernel, tq=tq, tk=tk, nch=nch)
    full = lambda b, g, i: (b, g, 0, 0)
    full5 = lambda b, g, i: (b, g, 0, 0, 0)
    qd = HPG * HEAD_DIM
    rows = HPG * tq
    return pl.pallas_call(
        kern,
        grid=(bsz, N_KV, nq),
        in_specs=[
            pl.BlockSpec((None, HPG, HEAD_DIM, tq), lambda b, g, i: (b, g, 0, i)),
            pl.BlockSpec((None, None, nch, HEAD_DIM), full),
            pl.BlockSpec((None, None, HEAD_DIM, nch), full),
            pl.BlockSpec((None, None, seq, 2 * HEAD_DIM), full),
            pl.BlockSpec((None, None, seq, HEAD_DIM), full),
            pl.BlockSpec((None, None, seq // tk, V_ROWS, tk), full5),
            pl.BlockSpec((None, None, seq // tk, V_ROWS, tk), full5),
            pl.BlockSpec((None, None, GATE_ROWS, tq), lambda b, g, i: (b, g, 0, i)),
            pl.BlockSpec((MAX_SEL_BLOCKS, nch), lambda b, g, i: (0, 0)),
        ],
        out_specs=pl.BlockSpec((None, qd, tq), lambda b, g, i: (b, g, i)),
        out_shape=jax.ShapeDtypeStruct((bsz, D_ATT, seq), F32),
        scratch_shapes=[pltpu.VMEM((V_ROWS, rows), F32), pltpu.VMEM((V_ROWS, rows), F32),
                        pltpu.VMEM((MAX_SEL_BLOCKS, tq), BF16)],
        compiler_params=_params(("parallel", "parallel", "arbitrary")),
        name="nsa",
    )(qt, kcmp, vcmpt, ksa, kw, vst, vwt, gate, ovlt)


def _out_proj_kernel(ys_ref, yat_ref, x_ref, wglu_ref, bglu_ref, gs_ref, ga_ref, wo_ref, g2_ref, wr_ref, br_ref,
                     x2e_ref):
    y = _gelu(ys_ref[...])
    y = y * _sigmoid(_dot(y.astype(BF16), wglu_ref[...]) + bglu_ref[...])
    ysn = y * lax.rsqrt(jnp.mean(y * y, axis=-1, keepdims=True) + EPS) * gs_ref[...]
    yat = yat_ref[...]
    yant = yat * lax.rsqrt(jnp.mean(yat * yat, axis=0, keepdims=True) + EPS) * ga_ref[...]
    yan = yant.T
    x2 = x_ref[...] + _dot(ysn.astype(BF16), wo_ref[:D_SSM, :]) + _dot(yan.astype(BF16), wo_ref[D_SSM:, :])
    x2e_ref[:, :D_MODEL] = x2
    h2 = (x2 * lax.rsqrt(jnp.mean(x2 * x2, axis=-1, keepdims=True) + EPS) * g2_ref[...]).astype(BF16)

    logits = _dot(h2, wr_ref[...]) + br_ref[...]
    lane = lax.broadcasted_iota(jnp.int32, logits.shape, 1).astype(F32)
    far = float(LANES)
    is_g = lane < N_EXP_GROUPS
    glog = jnp.where(is_g, logits, -jnp.inf)
    gmax = jnp.max(glog, axis=1, keepdims=True)
    gsum = jnp.sum(jnp.where(is_g, jnp.exp(logits - gmax), 0.0), axis=1, keepdims=True)
    gsel = jnp.min(jnp.where(glog == gmax, lane, far), axis=1, keepdims=True)
    gprob = 1.0 / gsum
    lo = ROUTER_OFF + EXPERTS_PER_GROUP * gsel
    in_e = (lane >= lo) & (lane < lo + EXPERTS_PER_GROUP)
    emax = jnp.max(jnp.where(in_e, logits, -jnp.inf), axis=1, keepdims=True)
    eexp = jnp.where(in_e, jnp.exp(logits - emax), 0.0)
    eprob = jnp.where(in_e, eexp / jnp.sum(eexp, axis=1, keepdims=True), -1.0)
    v1 = jnp.max(eprob, axis=1, keepdims=True)
    i1 = jnp.min(jnp.where(eprob == v1, lane, far), axis=1, keepdims=True)
    rest = jnp.where(lane == i1, -1.0, eprob)
    v2 = jnp.max(rest, axis=1, keepdims=True)
    i2 = jnp.min(jnp.where(rest == v2, lane, far), axis=1, keepdims=True)
    den = v1 + v2
    x2e_ref[:, D_MODEL:] = (jnp.where(lane == i1, v1 / den * gprob, 0.0)
                            + jnp.where(lane == i2, v2 / den * gprob, 0.0)
                            + jnp.where(lane == 0.0, gsel, 0.0))


def _out_proj(ys, yat, x2d, wglu, bglu, gs, ga, wo, g2, wr, br, *, seq):
    n_tok = x2d.shape[0]
    tm = 512
    nl = seq // tm
    row = lambda i: (i, 0)
    const = lambda i: (0, 0)
    return pl.pallas_call(
        _out_proj_kernel,
        grid=(n_tok // tm,),
        in_specs=[
            pl.BlockSpec((tm, D_SSM), row),
            pl.BlockSpec((None, D_ATT, tm), lambda i: (i // nl, 0, i % nl)),
            pl.BlockSpec((tm, D_MODEL), row),
            pl.BlockSpec((D_SSM, D_SSM), const),
            pl.BlockSpec((1, D_SSM), const),
            pl.BlockSpec((1, D_SSM), const),
            pl.BlockSpec((D_ATT, 1), const),
            pl.BlockSpec((D_SSM + D_ATT, D_MODEL), const),
            pl.BlockSpec((1, D_MODEL), const),
            pl.BlockSpec((D_MODEL, LANES), const),
            pl.BlockSpec((1, LANES), const),
        ],
        out_specs=pl.BlockSpec((tm, D_MODEL + LANES), row),
        out_shape=jax.ShapeDtypeStruct((n_tok, D_MODEL + LANES), F32),
        compiler_params=_params(("parallel",)),
        name="out_proj",
    )(ys, yat, x2d, wglu, bglu, gs, ga, wo, g2, wr, br)


def _moe_plan(gsel, n_tok):
    tmx = MOE_TM
    n_tiles = n_tok // tmx + N_EXP_GROUPS
    oh = (gsel[:, None] == jnp.arange(N_EXP_GROUPS)[None, :]).astype(jnp.int32)
    csum = jnp.cumsum(oh, axis=0)
    counts = csum[-1]
    rank = jnp.sum(csum * oh, axis=1) - 1
    nt = (counts + tmx - 1) // tmx
    tend = jnp.cumsum(nt)
    toff = tend - nt
    pos = jnp.sum(oh * toff[None, :], axis=1) * tmx + rank
    tok = jnp.zeros((n_tiles * tmx,), jnp.int32).at[pos].set(jnp.arange(n_tok, dtype=jnp.int32))
    tile = jnp.arange(n_tiles)
    grp = jnp.minimum(jnp.sum((tile[:, None] >= tend[None, :]).astype(jnp.int32), axis=1), N_EXP_GROUPS - 1)
    nval = jnp.clip(counts[grp] - (tile - toff[grp]) * tmx, 0, tmx)
    return grp.astype(jnp.int32), nval.astype(jnp.int32), tok


def _moe_kernel(grp_ref, nval_ref, tok_ref, x_hbm, g2_ref, wg_ref, wu_ref, wd_ref, o_hbm,
                xbuf, obuf, abuf, gsem, ssem, *, tmx, n_tiles):
    i = pl.program_id(0)
    slot = i % 2

    def gather_row(tile, r, dst_slot):
        t = tok_ref[tile * tmx + r]
        return pltpu.make_async_copy(x_hbm.at[pl.ds(t, 1), :], xbuf.at[dst_slot, pl.ds(r, 1), :], gsem.at[dst_slot])

    def scatter_row(tile, r, src_slot):
        t = tok_ref[tile * tmx + r]
        return pltpu.make_async_copy(obuf.at[src_slot, pl.ds(r, 1), :], o_hbm.at[pl.ds(t, 1), :], ssem.at[src_slot])

    def gather_wait(src_slot):
        pltpu.make_async_copy(x_hbm.at[pl.ds(0, tmx), :], xbuf.at[src_slot], gsem.at[src_slot]).wait()

    def scatter_wait(n, src_slot):
        n8 = pl.multiple_of((n // 8) * 8, 8)

        @pl.when(n8 > 0)
        def _():
            pltpu.make_async_copy(obuf.at[src_slot, pl.ds(0, n8), :], o_hbm.at[pl.ds(0, n8), :],
                                  ssem.at[src_slot]).wait()

        def one(r, c):
            pltpu.make_async_copy(obuf.at[src_slot, pl.ds(0, 1), :], o_hbm.at[pl.ds(0, 1), :],
                                  ssem.at[src_slot]).wait()
            return c

        lax.fori_loop(0, n - n8, one, 0)

    @pl.when(i == 0)
    def _():
        def body(r, c):
            gather_row(0, r, 0).start()
            return c

        lax.fori_loop(0, tmx, body, 0)

    gather_wait(slot)

    @pl.when(i >= 2)
    def _():
        scatter_wait(nval_ref[i - 2], slot)

    nxt = jnp.minimum(i + 1, n_tiles - 1)
    per = tmx // EXPERTS_PER_GROUP
    prev_full = (i >= 1) & (nval_ref[jnp.maximum(i - 1, 0)] == tmx)

    def experts(with_scatter):
        xe = xbuf[slot]
        x2 = xe[:, :D_MODEL]
        cw = xe[:, D_MODEL:]
        h = (x2 * lax.rsqrt(jnp.mean(x2 * x2, axis=-1, keepdims=True) + EPS) * g2_ref[...]).astype(BF16)
        lane = lax.broadcasted_iota(jnp.int32, cw.shape, 1)
        first = ROUTER_OFF + EXPERTS_PER_GROUP * grp_ref[i]
        for k in range(EXPERTS_PER_GROUP):
            for r in range(k * per, (k + 1) * per):
                gather_row(nxt, r, 1 - slot).start()
                if with_scatter:
                    scatter_row(i - 1, r, 1 - slot).start(priority=1)
            gate = _dot(h, wg_ref[k])
            up = _dot(h, wu_ref[k])
            ck = jnp.sum(jnp.where(lane == first + k, cw, 0.0), axis=1, keepdims=True)
            abuf[:, k * D_EXPERT:(k + 1) * D_EXPERT] = (gate * _sigmoid(gate) * up * ck).astype(BF16)
        obuf[slot] = x2 + _dot(abuf[...], wd_ref[...])

    @pl.when(prev_full)
    def _():
        experts(True)

    @pl.when(jnp.logical_not(prev_full))
    def _():
        experts(False)

    nv = nval_ref[i]

    @pl.when(nv < tmx)
    def _():
        def body(r, c):
            scatter_row(i, r, slot).start()
            return c

        lax.fori_loop(0, nv, body, 0)

    @pl.when(i == n_tiles - 1)
    def _():
        @pl.when(nv == tmx)
        def _():
            def body(r, c):
                scatter_row(i, r, slot).start()
                return c

            lax.fori_loop(0, tmx, body, 0)

        gather_wait(1 - slot)
        scatter_wait(nval_ref[i - 1], 1 - slot)
        scatter_wait(nv, slot)


def _moe(x2e, grp, nval, tok, g2, wg, wu, wd):
    n_tok = x2e.shape[0]
    tmx = MOE_TM
    n_tiles = grp.shape[0]
    kern = functools.partial(_moe_kernel, tmx=tmx, n_tiles=n_tiles)
    gk = EXPERTS_PER_GROUP * D_EXPERT
    grid_spec = pltpu.PrefetchScalarGridSpec(
        num_scalar_prefetch=3,
        grid=(n_tiles,),
        in_specs=[
            pl.BlockSpec(memory_space=pl.ANY),
            pl.BlockSpec((1, D_MODEL), lambda i, g, n, t: (0, 0)),
            pl.BlockSpec((None, EXPERTS_PER_GROUP, D_MODEL, D_EXPERT), lambda i, g, n, t: (g[i], 0, 0, 0)),
            pl.BlockSpec((None, EXPERTS_PER_GROUP, D_MODEL, D_EXPERT), lambda i, g, n, t: (g[i], 0, 0, 0)),
            pl.BlockSpec((None, gk, D_MODEL), lambda i, g, n, t: (g[i], 0, 0)),
        ],
        out_specs=pl.BlockSpec(memory_space=pl.ANY),
        scratch_shapes=[
            pltpu.VMEM((2, tmx, D_MODEL + LANES), F32),
            pltpu.VMEM((2, tmx, D_MODEL), F32),
            pltpu.VMEM((tmx, gk), BF16),
            pltpu.SemaphoreType.DMA((2,)),
            pltpu.SemaphoreType.DMA((2,)),
        ],
    )
    return pl.pallas_call(
        kern,
        grid_spec=grid_spec,
        out_shape=jax.ShapeDtypeStruct((n_tok, D_MODEL), F32),
        compiler_params=_params(("arbitrary",)),
        name="moe",
    )(grp, nval, tok, x2e, g2, wg, wu, wd)


def _block_diag_ones(n, blk):
    i = jnp.arange(n) // blk
    return (i[:, None] == i[None, :]).astype(BF16)


def _layer(x, norm1_g, w_in, lam_re, lam_im, log_step, b_re, b_im, c_re, c_im, d_skip,
           w_glu, b_glu, g_q, g_kc, g_ks, g_kw, pos_k, pos_v, w_ck1, w_ck2, w_cv1, w_cv2,
           out_g_ssm, out_g_att, w_out, norm2_g, w_grp, b_grp, w_exp, b_exp, w_gate, w_up, w_down):
    bsz, seq, _ = x.shape
    assert seq % 512 == 0 and seq // SEL_BLOCK <= MAX_SEL_BLOCKS
    n_tok = bsz * seq
    x2d = x.reshape(n_tok, D_MODEL)
    q8 = SSM_Q
    n_sub = seq // q8
    nch = seq // CMP_STRIDE

    o_q = D_SSM
    o_kv = D_SSM + D_ATT
    o_gt = o_kv + 6 * D_KV
    kv = lambda i: w_in[:, o_kv + i * D_KV:o_kv + (i + 1) * D_KV]
    wrow = jnp.concatenate([w_in[:, :o_q], kv(0), kv(1), kv(2), kv(4)], axis=1).astype(BF16)
    per_g = HPG * N_BRANCH
    wgt = jnp.zeros((D_MODEL, N_KV * GATE_ROWS), F32)
    for g in range(N_KV):
        wgt = wgt.at[:, g * GATE_ROWS:g * GATE_ROWS + per_g].set(w_in[:, o_gt + g * per_g:o_gt + (g + 1) * per_g])
    wcol = jnp.concatenate([w_in[:, o_q:o_kv], kv(3), kv(5), wgt], axis=1).T.astype(BF16)
    qscale = (HEAD_DIM ** -0.5) * math.log2(math.e)
    gq = (jnp.tile(g_q.astype(F32), N_HEADS) * qscale).reshape(D_ATT, 1)
    gks = jnp.tile(g_ks.astype(F32), N_KV).reshape(1, D_KV)
    gkw = jnp.tile(g_kw.astype(F32), N_KV).reshape(1, D_KV)

    u, qt, kc, vc, ksa, kw, vst, vwt, gate = _in_proj(
        x2d, norm1_g.reshape(1, D_MODEL), wrow, wcol, gq, gks, gkw,
        _block_diag_ones(D_KV, HEAD_DIM), bsz=bsz, seq=seq)

    w_loc, t_in, m_st, pw_re, pw_im, dvec = _s5_weights(
        lam_re, lam_im, log_step, b_re, b_im, c_re, c_im, d_skip, n_sub)
    ys = _s5(u.reshape(bsz, seq, D_SSM), w_loc, t_in, m_st, pw_re, pw_im, dvec,
             bsz=bsz, n_sub=n_sub).reshape(n_tok, D_SSM)

    wide = CMP_STRIDE * HEAD_DIM
    pad8 = lambda p: jnp.zeros((8, 2 * wide), F32).at[0].set(p.reshape(-1)).astype(BF16)
    kcmp, vcmpt = _compress(
        kc, vc,
        w_ck1.astype(BF16), w_ck2.astype(BF16), w_cv1.astype(BF16), w_cv2.T.astype(BF16),
        pad8(pos_k), pad8(pos_v), g_kc.astype(F32).reshape(1, HEAD_DIM), bsz=bsz, nch=nch)
    cstart = jnp.arange(nch) * CMP_STRIDE
    sstart = jnp.arange(MAX_SEL_BLOCKS) * SEL_BLOCK
    ovlt = ((cstart[None, :] < sstart[:, None] + SEL_BLOCK) & (cstart[None, :] + CMP_BLOCK > sstart[:, None])
            & (jnp.arange(MAX_SEL_BLOCKS)[:, None] < seq // SEL_BLOCK)
            & (jnp.arange(nch)[None, :] < nch - 1)).astype(BF16)
    yat = _nsa(qt, kcmp, vcmpt, ksa, kw, vst, vwt, gate, ovlt, bsz=bsz, seq=seq, nch=nch)

    wr = jnp.zeros((D_MODEL, LANES), F32)
    wr = wr.at[:, :N_EXP_GROUPS].set(w_grp).at[:, ROUTER_OFF:ROUTER_OFF + N_EXPERTS].set(w_exp).astype(BF16)
    br = jnp.zeros((1, LANES), F32)
    br = br.at[0, :N_EXP_GROUPS].set(b_grp).at[0, ROUTER_OFF:ROUTER_OFF + N_EXPERTS].set(b_exp)
    g2 = norm2_g.reshape(1, D_MODEL).astype(F32)
    x2e = _out_proj(
        ys, yat, x2d, w_glu.astype(BF16), b_glu.reshape(1, D_SSM).astype(F32),
        out_g_ssm.reshape(1, D_SSM).astype(F32), out_g_att.reshape(D_ATT, 1).astype(F32),
        w_out.astype(BF16), g2, wr, br, seq=seq)

    grp, nval, tok = _moe_plan(x2e[:, D_MODEL].astype(jnp.int32), n_tok)
    gshape = (N_EXP_GROUPS, EXPERTS_PER_GROUP, D_MODEL, D_EXPERT)
    out = _moe(x2e, grp, nval, tok, g2, w_gate.astype(BF16).reshape(gshape), w_up.astype(BF16).reshape(gshape),
               w_down.astype(BF16).reshape(N_EXP_GROUPS, EXPERTS_PER_GROUP * D_EXPERT, D_MODEL))
    return out.reshape(bsz, seq, D_MODEL)


def kernel(x, norm1_g, w_in, lam_re, lam_im, log_step, b_re, b_im, c_re, c_im, d_skip, w_glu, b_glu, g_q, g_kc, g_ks, g_kw, pos_k, pos_v, w_ck1, w_ck2, w_cv1, w_cv2, out_g_ssm, out_g_att, w_out, norm2_g, w_grp, b_grp, w_exp, b_exp, w_gate, w_up, w_down):
    depth = norm1_g.shape[0]
    for l in range(depth):
        x = _layer(x, norm1_g[l], w_in[l], lam_re[l], lam_im[l], log_step[l], b_re[l], b_im[l], c_re[l],
                   c_im[l], d_skip[l], w_glu[l], b_glu[l], g_q[l], g_kc[l], g_ks[l], g_kw[l], pos_k[l],
                   pos_v[l], w_ck1[l], w_ck2[l], w_cv1[l], w_cv2[l], out_g_ssm[l], out_g_att[l], w_out[l],
                   norm2_g[l], w_grp[l], b_grp[l], w_exp[l], b_exp[l], w_gate[l], w_up[l], w_down[l])
    return x
```

```python
import functools
import math

import jax
import jax.numpy as jnp
from jax import lax
from jax.experimental import pallas as pl
from jax.experimental.pallas import tpu as pltpu

D_MODEL = 1024
D_SSM = 512
SSM_CH = 16
SSM_GROUPS = D_SSM // SSM_CH
SSM_STATE = 64
D_ATT = 512
HEAD_DIM = 64
N_HEADS = D_ATT // HEAD_DIM
N_KV = 2
HPG = N_HEADS // N_KV
D_KV = N_KV * HEAD_DIM
N_BRANCH = 3
CMP_STRIDE = 16
CMP_BLOCK = 2 * CMP_STRIDE
CMP_HIDDEN = 256
SEL_BLOCK = 64
N_SELECT = 16
WINDOW = 512
N_EXP_GROUPS = 4
EXPERTS_PER_GROUP = 8
N_EXPERTS = N_EXP_GROUPS * EXPERTS_PER_GROUP
D_EXPERT = 256
EPS = 1e-6
NEG = -1e30
FORCE = 1e9

LANES = 128
SSM_Q = 8
SSM_LT = D_SSM // LANES
ROUTER_OFF = N_EXP_GROUPS
NSA_TQ = 256
NSA_TK = 256
NSA_UNROLL = 8
V_ROWS = HEAD_DIM + 16
MAX_SEL_BLOCKS = 64
MOE_TM = 256
GATE_ROWS = 16
VMEM_LIMIT = 56 * 1024 * 1024

F32 = jnp.float32
BF16 = jnp.bfloat16


def _dot(a, b):
    return jnp.dot(a, b, preferred_element_type=F32)


def _dot_nt(a, b):
    return lax.dot_general(a, b, (((1,), (1,)), ((), ())), preferred_element_type=F32)


def _split_dot(x, w):
    hi = x.astype(BF16)
    lo = (x - hi.astype(F32)).astype(BF16)
    return _dot(hi, w) + _dot(lo, w)


def _gelu(x):
    c = math.sqrt(2.0 / math.pi)
    return 0.5 * x * (1.0 + jnp.tanh(c * (x + 0.044715 * (x * x * x))))


def _sigmoid(x):
    return 1.0 / (1.0 + jnp.exp(-x))


def _params(sem):
    return pltpu.CompilerParams(dimension_semantics=sem, vmem_limit_bytes=VMEM_LIMIT)


def _in_proj_kernel(x_ref, g1_ref, wrow_ref, wcol_ref, gq_ref, gks_ref, gkw_ref, bd128_ref,
                    u_ref, qt_ref, kc_ref, vc_ref, ksa_ref, kw_ref, vst_ref, vwt_ref, gate_ref, *, tm, nl):
    x = x_ref[...]
    ms = jnp.mean(x * x, axis=-1, keepdims=True)
    hn = (x * lax.rsqrt(ms + EPS) * g1_ref[...]).astype(BF16)

    pr = _dot(hn, wrow_ref[...])
    u_ref[...] = pr[:, :D_SSM]
    kc, vc, ks, kw = [pr[:, D_SSM + i * D_KV:D_SSM + (i + 1) * D_KV] for i in range(4)]
    kss = _split_dot(ks * ks, bd128_ref[...])
    ksn = ks * lax.rsqrt(kss * (1.0 / HEAD_DIM) + EPS) * gks_ref[...]
    kws = _split_dot(kw * kw, bd128_ref[...])
    kwn = kw * lax.rsqrt(kws * (1.0 / HEAD_DIM) + EPS) * gkw_ref[...]
    t0 = (pl.program_id(0) % nl) * tm
    tpos = t0 + lax.broadcasted_iota(jnp.int32, (tm, MAX_SEL_BLOCKS), 0)
    blk = lax.broadcasted_iota(jnp.int32, (tm, MAX_SEL_BLOCKS), 1)
    onehot = jnp.where(tpos // SEL_BLOCK == blk, 1.0, 0.0).astype(BF16)
    for g in range(N_KV):
        sl = slice(g * HEAD_DIM, (g + 1) * HEAD_DIM)
        kc_ref[g] = kc[:, sl]
        vc_ref[g] = vc[:, sl]
        ksa_ref[g] = jnp.concatenate([ksn[:, sl].astype(BF16), onehot], axis=1)
        kw_ref[g] = kwn[:, sl].astype(BF16)

    pc = _dot_nt(wcol_ref[...], hn)
    gq = gq_ref[...]
    for h in range(N_HEADS):
        sl = slice(h * HEAD_DIM, (h + 1) * HEAD_DIM)
        qh = pc[sl]
        ss = jnp.sum(qh * qh, axis=0, keepdims=True)
        qt_ref[h] = (qh * lax.rsqrt(ss * (1.0 / HEAD_DIM) + EPS) * gq[sl]).astype(BF16)
    ones_rows = jnp.where(lax.broadcasted_iota(jnp.int32, (V_ROWS - HEAD_DIM, tm), 0) == 0, 1.0, 0.0)
    for g in range(N_KV):
        for o_ref, base in ((vst_ref, D_ATT), (vwt_ref, D_ATT + D_KV)):
            vt = jnp.concatenate([pc[base + g * HEAD_DIM:base + (g + 1) * HEAD_DIM], ones_rows], axis=0)
            vt = vt.astype(BF16)
            for j in range(tm // NSA_TK):
                o_ref[g, j] = vt[:, j * NSA_TK:(j + 1) * NSA_TK]
        gb = D_ATT + 2 * D_KV + g * GATE_ROWS
        gate_ref[g] = _sigmoid(pc[gb:gb + GATE_ROWS])


def _in_proj(x2d, g1, wrow, wcol, gq, gks, gkw, bd128, *, bsz, seq):
    tm = 512
    nl = seq // tm
    n_tok = bsz * seq
    kern = functools.partial(_in_proj_kernel, tm=tm, nl=nl)
    row = lambda i: (i, 0)
    const = lambda i: (0, 0)
    bgl = lambda i: (i // nl, 0, i % nl, 0)
    n_col = wcol.shape[0]
    jt = tm // NSA_TK

    def kvspec(width):
        return pl.BlockSpec((None, N_KV, tm, width), bgl)

    def kvshape(width, dtype=BF16):
        return jax.ShapeDtypeStruct((bsz, N_KV, seq, width), dtype)

    vt_spec = pl.BlockSpec((None, N_KV, jt, V_ROWS, NSA_TK), lambda i: (i // nl, 0, i % nl, 0, 0))
    vt_shape = jax.ShapeDtypeStruct((bsz, N_KV, seq // NSA_TK, V_ROWS, NSA_TK), BF16)
    return pl.pallas_call(
        kern,
        grid=(n_tok // tm,),
        in_specs=[
            pl.BlockSpec((tm, D_MODEL), row),
            pl.BlockSpec((1, D_MODEL), const),
            pl.BlockSpec((D_MODEL, D_SSM + 4 * D_KV), const),
            pl.BlockSpec((n_col, D_MODEL), const),
            pl.BlockSpec((D_ATT, 1), const),
            pl.BlockSpec((1, D_KV), const),
            pl.BlockSpec((1, D_KV), const),
            pl.BlockSpec((D_KV, D_KV), const),
        ],
        out_specs=[
            pl.BlockSpec((tm, D_SSM), row),
            pl.BlockSpec((None, N_HEADS, HEAD_DIM, tm), lambda i: (i // nl, 0, 0, i % nl)),
            kvspec(HEAD_DIM), kvspec(HEAD_DIM), kvspec(2 * HEAD_DIM), kvspec(HEAD_DIM),
            vt_spec, vt_spec,
            pl.BlockSpec((None, N_KV, GATE_ROWS, tm), lambda i: (i // nl, 0, 0, i % nl)),
        ],
        out_shape=[
            jax.ShapeDtypeStruct((n_tok, D_SSM), F32),
            jax.ShapeDtypeStruct((bsz, N_HEADS, HEAD_DIM, seq), BF16),
            kvshape(HEAD_DIM, F32), kvshape(HEAD_DIM, F32), kvshape(2 * HEAD_DIM), kvshape(HEAD_DIM),
            vt_shape, vt_shape,
            jax.ShapeDtypeStruct((bsz, N_KV, GATE_ROWS, seq), F32),
        ],
        compiler_params=_params(("parallel",)),
        name="in_proj",
    )(x2d, g1, wrow, wcol, gq, gks, gkw, bd128)


def _s5_weights(lam_re, lam_im, log_step, b_re, b_im, c_re, c_im, d_skip, n_sub):
    q = SSM_Q
    lam = lax.complex(lam_re.astype(F32), lam_im.astype(F32))
    step = jnp.exp(log_step.astype(F32))[:, None]
    lam_bar = jnp.exp(lam * step)
    b_bar = ((lam_bar - 1.0) / lam)[..., None] * lax.complex(b_re.astype(F32), b_im.astype(F32))
    c = lax.complex(c_re.astype(F32), c_im.astype(F32))
    pows = [jnp.ones_like(lam_bar)]
    for _ in range(q):
        pows.append(pows[-1] * lam_bar)
    pw = jnp.stack(pows)
    lt, a8 = SSM_LT, LANES // SSM_CH
    hp = a8 * SSM_STATE
    e_lane = (jnp.arange(a8)[:, None] == jnp.arange(LANES)[None, :] // SSM_CH).astype(F32)
    e_state = (jnp.arange(a8)[:, None] == jnp.arange(hp)[None, :] // SSM_STATE).astype(F32)
    e_lane_t = jnp.tile(e_lane, (1, q))
    e_state_t = jnp.tile(e_state, (1, 2))

    kk = jnp.real(jnp.einsum('ghp,kgp,gpi->kghi', c, pw[:q], b_bar))
    km = kk.reshape(q, lt, a8, SSM_CH, SSM_CH).transpose(1, 4, 0, 2, 3).reshape(lt, SSM_CH, q, LANES)
    lag = jnp.arange(q)[None, :] - jnp.arange(q)[:, None]
    kg = km[:, :, jnp.clip(lag, 0, q - 1), :] * (lag >= 0)[None, None, :, :, None].astype(F32)
    kc = kg.transpose(0, 2, 1, 3, 4).reshape(lt, q, 1, SSM_CH, q * LANES)
    t_in = (kc * e_lane_t[None, None, :, None, :]).reshape(lt, q * LANES, q * LANES)

    wc = pw[q - 1 - jnp.arange(q)][..., None] * b_bar[None]
    wri = jnp.stack([jnp.real(wc), jnp.imag(wc)])
    wm = (wri.reshape(2, q, lt, a8, SSM_STATE, SSM_CH).transpose(2, 1, 5, 0, 3, 4)
          .reshape(lt, q, 1, SSM_CH, 2 * hp))
    w_loc = (wm * e_state_t[None, None, :, None, :]).reshape(lt, q * LANES, 2 * hp)

    cl = c[None] * pw[1:q + 1][:, :, None, :]
    cri = jnp.stack([jnp.real(cl), -jnp.imag(cl)])
    mm = (cri.reshape(2, q, lt, a8, SSM_CH, SSM_STATE).transpose(2, 0, 5, 1, 3, 4)
          .reshape(lt, 2, 1, SSM_STATE, q * LANES))
    m_st = (mm * e_lane_t[None, None, :, None, :]).reshape(lt, 2 * hp, q * LANES)

    n_lvl = max(1, (n_sub - 1).bit_length())
    lv = [pw[q]]
    for _ in range(n_lvl - 1):
        lv.append(lv[-1] * lv[-1])
    lvs = jnp.stack(lv).reshape(n_lvl, lt, 1, hp)
    pw_re = jnp.real(lvs).transpose(1, 0, 2, 3)
    pw_im = jnp.imag(lvs).transpose(1, 0, 2, 3)
    dvec = jnp.tile(d_skip.astype(F32).reshape(lt, 1, LANES), (1, 1, q))
    return w_loc.astype(BF16), t_in.astype(BF16), m_st.astype(BF16), pw_re, pw_im, dvec


def _s5_kernel(u_ref, w_ref, t_ref, m_ref, pwr_ref, pwi_ref, d_ref, y_ref, *, n_sub, n_lvl):
    half = (LANES // SSM_CH) * SSM_STATE
    q = SSM_Q
    u = jnp.concatenate([u_ref[pl.ds(s, n_sub, stride=q), :] for s in range(q)], axis=1)
    ub = u.astype(BF16)
    s_loc = _dot(ub, w_ref[...])
    re = s_loc[:, :half]
    im = s_loc[:, half:]
    rowi = lax.broadcasted_iota(jnp.int32, (n_sub, half), 0)
    for k in range(n_lvl):
        d = 1 << k
        ar = pwr_ref[k]
        ai = pwi_ref[k]
        keep = rowi >= d
        sre = jnp.where(keep, pltpu.roll(re, d, axis=0), 0.0)
        sim = jnp.where(keep, pltpu.roll(im, d, axis=0), 0.0)
        re, im = re + (ar * sre - ai * sim), im + (ar * sim + ai * sre)
    keep = rowi >= 1
    xre = jnp.where(keep, pltpu.roll(re, 1, axis=0), 0.0)
    xim = jnp.where(keep, pltpu.roll(im, 1, axis=0), 0.0)
    xst = jnp.concatenate([xre, xim], axis=1).astype(BF16)
    y = _dot(ub, t_ref[...]) + _dot(xst, m_ref[...]) + d_ref[...] * u
    for j in range(q):
        y_ref[pl.ds(j, n_sub, stride=q), :] = y[:, j * LANES:(j + 1) * LANES]


def _s5(u, w_loc, t_in, m_st, pw_re, pw_im, dvec, *, bsz, n_sub):
    q = SSM_Q
    n_lvl = pw_re.shape[1]
    kern = functools.partial(_s5_kernel, n_sub=n_sub, n_lvl=n_lvl)
    wide = q * LANES
    seq = n_sub * q
    return pl.pallas_call(
        kern,
        grid=(bsz, SSM_LT),
        in_specs=[
            pl.BlockSpec((None, seq, LANES), lambda b, l: (b, 0, l)),
            pl.BlockSpec((None, wide, wide), lambda b, l: (l, 0, 0)),
            pl.BlockSpec((None, wide, wide), lambda b, l: (l, 0, 0)),
            pl.BlockSpec((None, wide, wide), lambda b, l: (l, 0, 0)),
            pl.BlockSpec((None, n_lvl, 1, wide // 2), lambda b, l: (l, 0, 0, 0)),
            pl.BlockSpec((None, n_lvl, 1, wide // 2), lambda b, l: (l, 0, 0, 0)),
            pl.BlockSpec((None, 1, wide), lambda b, l: (l, 0, 0)),
        ],
        out_specs=pl.BlockSpec((None, seq, LANES), lambda b, l: (b, 0, l)),
        out_shape=jax.ShapeDtypeStruct((bsz, seq, D_SSM), F32),
        compiler_params=_params(("parallel", "parallel")),
        name="s5",
    )(u, w_loc, t_in, m_st, pw_re, pw_im, dvec)


def _compress_kernel(kc_ref, vc_ref, w1k_ref, w2k_ref, w1v_ref, w2vt_ref, posk_ref, posv_ref, gkc_ref,
                     kcmp_ref, vcmpt_ref, *, nch):
    half = CMP_STRIDE * HEAD_DIM

    def hidden(x_ref, w1_ref, pos_ref):
        x = jnp.concatenate([x_ref[pl.ds(j, nch, stride=CMP_STRIDE), :] for j in range(CMP_STRIDE)],
                            axis=1).astype(BF16)
        a = _dot(x, w1_ref[:half, :])
        b = _dot(x, w1_ref[half:, :])
        pv = _dot(pos_ref[...], w1_ref[...])[0:1, :]
        hid = a + pltpu.roll(b, nch - 1, axis=0) + pv
        return _gelu(hid).astype(BF16)

    k = _dot(hidden(kc_ref, w1k_ref, posk_ref), w2k_ref[...])
    ms = jnp.mean(k * k, axis=-1, keepdims=True)
    kcmp_ref[...] = (k * lax.rsqrt(ms + EPS) * gkc_ref[...]).astype(BF16)
    vt = _dot_nt(w2vt_ref[...], hidden(vc_ref, w1v_ref, posv_ref))
    coli = lax.broadcasted_iota(jnp.int32, vt.shape, 1)
    vcmpt_ref[...] = jnp.where(coli < nch - 1, vt, 0.0).astype(BF16)


def _compress(kcf, vcf, w1k, w2k, w1v, w2vt, posk, posv, gkc, *, bsz, nch):
    kern = functools.partial(_compress_kernel, nch=nch)
    wide = CMP_STRIDE * HEAD_DIM
    xspec = pl.BlockSpec((None, None, nch * CMP_STRIDE, HEAD_DIM), lambda b, g: (b, g, 0, 0))
    c2 = lambda b, g: (0, 0)
    return pl.pallas_call(
        kern,
        grid=(bsz, N_KV),
        in_specs=[
            xspec, xspec,
            pl.BlockSpec((2 * wide, CMP_HIDDEN), c2), pl.BlockSpec((CMP_HIDDEN, HEAD_DIM), c2),
            pl.BlockSpec((2 * wide, CMP_HIDDEN), c2), pl.BlockSpec((HEAD_DIM, CMP_HIDDEN), c2),
            pl.BlockSpec((8, 2 * wide), c2), pl.BlockSpec((8, 2 * wide), c2),
            pl.BlockSpec((1, HEAD_DIM), c2),
        ],
        out_specs=[pl.BlockSpec((None, None, nch, HEAD_DIM), lambda b, g: (b, g, 0, 0)),
                   pl.BlockSpec((None, None, HEAD_DIM, nch), lambda b, g: (b, g, 0, 0))],
        out_shape=[jax.ShapeDtypeStruct((bsz, N_KV, nch, HEAD_DIM), BF16),
                   jax.ShapeDtypeStruct((bsz, N_KV, HEAD_DIM, nch), BF16)],
        compiler_params=_params(("parallel", "parallel")),
        name="compress",
    )(kcf, vcf, w1k, w2k, w1v, w2vt, posk, posv, gkc)


def _nsa_kernel(qt_ref, kcmp_ref, vcmpt_ref, ksa_ref, kw_ref, vst_ref, vwt_ref, gate_ref, ovlt_ref, o_ref,
                accs_ref, accw_ref, sel_ref, ocmp_ref, *, tq, tk, nch):
    qi = pl.program_id(2)
    q0 = qi * tq
    rows = HPG * tq
    qt = jnp.concatenate([qt_ref[h] for h in range(HPG)], axis=1)
    tpos = q0 + lax.broadcasted_iota(jnp.int32, (1, rows), 1) % tq
    krow = lax.broadcasted_iota(jnp.int32, (tk, rows), 0)
    m0 = jnp.full((1, rows), NEG, F32)

    def run_tiles(k_ref, q_all, vt_ref, acc_ref, m, tiles):
        scores = []
        for kt, kind in tiles:
            kc = jnp.maximum(kt, 0) if kind in ('band', 'valid') else kt
            s = _dot(k_ref[pl.ds(pl.multiple_of(kc * tk, tk), tk), :], q_all)
            if kind == 'causal':
                s = jnp.where(kt * tk + krow <= tpos, s, NEG)
            elif kind == 'band':
                s = jnp.where((kt * tk + krow > tpos - WINDOW) & (kt >= 0), s, NEG)
            elif kind == 'valid':
                s = jnp.where(kt >= 0, s, NEG)
            scores.append((kc, s))
        for kc, s in scores:
            m_new = jnp.maximum(m, jnp.max(s, axis=0, keepdims=True))
            alpha = jnp.exp2(m - m_new)
            p = jnp.exp2(s - m_new).astype(BF16)
            acc_ref[...] = alpha * acc_ref[...] + _dot(vt_ref[kc], p)
            m = m_new
        return m

    accw_ref[...] = jnp.zeros_like(accw_ref)
    n_win = WINDOW // tk
    run_tiles(kw_ref, qt, vwt_ref, accw_ref, m0,
              [(qi - n_win, 'band')] + [(qi - n_win + t, 'valid') for t in range(1, n_win)] + [(qi, 'causal')])

    def compressed_and_select(nbv):
        ncv = min(nch, nbv * (SEL_BLOCK // CMP_STRIDE))
        s = _dot(kcmp_ref[0:ncv, :], qt)
        cend = lax.broadcasted_iota(jnp.int32, (ncv, rows), 0) * CMP_STRIDE + (CMP_BLOCK - 1)
        s = jnp.where(cend <= tpos, s, NEG)
        m = jnp.max(s, axis=0, keepdims=True)
        p = jnp.exp2(s - m)
        p = p * jnp.where(tpos >= CMP_BLOCK - 1, 1.0 / jnp.sum(p, axis=0, keepdims=True), 0.0)
        ocmp_ref[...] = _dot(vcmpt_ref[:, 0:ncv], p.astype(BF16))
        if nbv <= N_SELECT:
            return
        psum = p[:, 0:tq]
        for h in range(1, HPG):
            psum = psum + p[:, h * tq:(h + 1) * tq]
        hi = psum.astype(BF16)
        lo = (psum - hi.astype(F32)).astype(BF16)
        ovlt = ovlt_ref[0:nbv, 0:ncv]
        imp = _dot(ovlt, hi) + _dot(ovlt, lo)
        blk = lax.broadcasted_iota(jnp.int32, (nbv, tq), 0)
        cur = (q0 + lax.broadcasted_iota(jnp.int32, (nbv, tq), 1)) // SEL_BLOCK
        forced = (blk == 0) | (blk == cur) | (blk == cur - 1)
        imp = jnp.where(forced, FORCE, jnp.where(blk <= cur, imp, NEG))
        sub = 8
        rowl = lax.broadcasted_iota(jnp.int32, (sub, tq), 0)
        groups = [imp[r:r + sub] for r in range(0, nbv, sub)]
        ranks = [jnp.zeros((sub, tq), F32) for _ in groups]
        for i in range(nbv):
            ri = jnp.broadcast_to(imp[i:i + 1, :], (sub, tq))
            for gi, x in enumerate(groups):
                if i < gi * sub:
                    ahead = ri >= x
                elif i >= (gi + 1) * sub:
                    ahead = ri > x
                else:
                    ahead = (ri > x) | ((ri == x) & (rowl > i - gi * sub))
                ranks[gi] = ranks[gi] + jnp.where(ahead, 1.0, 0.0)
        rank = jnp.concatenate(ranks, axis=0)
        sel_ref[0:nbv, :] = jnp.where(rank < N_SELECT, 0.0, NEG).astype(BF16)

    n_seen = (q0 + tq - 1) // SEL_BLOCK + 1
    sel_ref[...] = jnp.zeros_like(sel_ref)
    for nbv in range(N_SELECT, MAX_SEL_BLOCKS + 1, N_SELECT):
        @pl.when((n_seen > nbv - N_SELECT) & (n_seen <= nbv))
        def _(nbv=nbv):
            compressed_and_select(nbv)
    o_cmp = ocmp_ref[...]
    sel = sel_ref[...]
    qa = jnp.concatenate([qt, jnp.concatenate([sel] * HPG, axis=1)], axis=0)

    accs_ref[...] = jnp.zeros_like(accs_ref)
    nu = NSA_UNROLL

    def sel_multi(j, m):
        return run_tiles(ksa_ref, qa, vst_ref, accs_ref, m, [(nu * j + t, None) for t in range(nu)])

    m_sel = lax.fori_loop(0, qi // nu, sel_multi, m0)
    for r in range(nu):
        @pl.when(qi % nu == r)
        def _(r=r):
            base = qi - r
            run_tiles(ksa_ref, qa, vst_ref, accs_ref, m_sel,
                      [(base + t, None) for t in range(r)] + [(qi, 'causal')])

    accs = accs_ref[...]
    accw = accw_ref[...]
    o_sel = accs[:HEAD_DIM] * (1.0 / accs[HEAD_DIM:HEAD_DIM + 1])
    o_win = accw[:HEAD_DIM] * (1.0 / accw[HEAD_DIM:HEAD_DIM + 1])
    gt = gate_ref[...]
    for h in range(HPG):
        c = slice(h * tq, (h + 1) * tq)
        o_ref[h * HEAD_DIM:(h + 1) * HEAD_DIM, :] = (
            gt[3 * h:3 * h + 1] * o_cmp[:, c] + gt[3 * h + 1:3 * h + 2] * o_sel[:, c]
            + gt[3 * h + 2:3 * h + 3] * o_win[:, c])


def _nsa(qt, kcmp, vcmpt, ksa, kw, vst, vwt, gate, ovlt, *, bsz, seq, nch):
    tq, tk = NSA_TQ, NSA_TK
    assert tq == tk and WINDOW % tk == 0
    nq = seq // tq
    kern = functools.partial(_nsa_kernel, tq=tq, tk=tk, nch=nch)
    full = lambda b, g, i: (b, g, 0, 0)
    full5 = lambda b, g, i: (b, g, 0, 0, 0)
    qd = HPG * HEAD_DIM
    rows = HPG * tq
    return pl.pallas_call(
        kern,
        grid=(bsz, N_KV, nq),
        in_specs=[
            pl.BlockSpec((None, HPG, HEAD_DIM, tq), lambda b, g, i: (b, g, 0, i)),
            pl.BlockSpec((None, None, nch, HEAD_DIM), full),
            pl.BlockSpec((None, None, HEAD_DIM, nch), full),
            pl.BlockSpec((None, None, seq, 2 * HEAD_DIM), full),
            pl.BlockSpec((None, None, seq, HEAD_DIM), full),
            pl.BlockSpec((None, None, seq // tk, V_ROWS, tk), full5),
            pl.BlockSpec((None, None, seq // tk, V_ROWS, tk), full5),
            pl.BlockSpec((None, None, GATE_ROWS, tq), lambda b, g, i: (b, g, 0, i)),
            pl.BlockSpec((MAX_SEL_BLOCKS, nch), lambda b, g, i: (0, 0)),
        ],
        out_specs=pl.BlockSpec((None, qd, tq), lambda b, g, i: (b, g, i)),
        out_shape=jax.ShapeDtypeStruct((bsz, D_ATT, seq), F32),
        scratch_shapes=[pltpu.VMEM((V_ROWS, rows), F32), pltpu.VMEM((V_ROWS, rows), F32),
                        pltpu.VMEM((MAX_SEL_BLOCKS, tq), BF16), pltpu.VMEM((HEAD_DIM, rows), F32)],
        compiler_params=_params(("parallel", "parallel", "arbitrary")),
        name="nsa",
    )(qt, kcmp, vcmpt, ksa, kw, vst, vwt, gate, ovlt)


def _out_proj_kernel(ys_ref, yat_ref, x_ref, wglu_ref, bglu_ref, gs_ref, ga_ref, wo_ref, g2_ref, wr_ref, br_ref,
                     x2e_ref):
    y = _gelu(ys_ref[...])
    y = y * _sigmoid(_dot(y.astype(BF16), wglu_ref[...]) + bglu_ref[...])
    ysn = y * lax.rsqrt(jnp.mean(y * y, axis=-1, keepdims=True) + EPS) * gs_ref[...]
    yat = yat_ref[...]
    yant = yat * lax.rsqrt(jnp.mean(yat * yat, axis=0, keepdims=True) + EPS) * ga_ref[...]
    yan = yant.T
    x2 = x_ref[...] + _dot(ysn.astype(BF16), wo_ref[:D_SSM, :]) + _dot(yan.astype(BF16), wo_ref[D_SSM:, :])
    x2e_ref[:, :D_MODEL] = x2
    h2 = (x2 * lax.rsqrt(jnp.mean(x2 * x2, axis=-1, keepdims=True) + EPS) * g2_ref[...]).astype(BF16)

    logits = _dot(h2, wr_ref[...]) + br_ref[...]
    lane = lax.broadcasted_iota(jnp.int32, logits.shape, 1).astype(F32)
    far = float(LANES)
    is_g = lane < N_EXP_GROUPS
    glog = jnp.where(is_g, logits, -jnp.inf)
    gmax = jnp.max(glog, axis=1, keepdims=True)
    gsum = jnp.sum(jnp.where(is_g, jnp.exp(logits - gmax), 0.0), axis=1, keepdims=True)
    gsel = jnp.min(jnp.where(glog == gmax, lane, far), axis=1, keepdims=True)
    gprob = 1.0 / gsum
    lo = ROUTER_OFF + EXPERTS_PER_GROUP * gsel
    in_e = (lane >= lo) & (lane < lo + EXPERTS_PER_GROUP)
    emax = jnp.max(jnp.where(in_e, logits, -jnp.inf), axis=1, keepdims=True)
    eexp = jnp.where(in_e, jnp.exp(logits - emax), 0.0)
    eprob = jnp.where(in_e, eexp / jnp.sum(eexp, axis=1, keepdims=True), -1.0)
    v1 = jnp.max(eprob, axis=1, keepdims=True)
    i1 = jnp.min(jnp.where(eprob == v1, lane, far), axis=1, keepdims=True)
    rest = jnp.where(lane == i1, -1.0, eprob)
    v2 = jnp.max(rest, axis=1, keepdims=True)
    i2 = jnp.min(jnp.where(rest == v2, lane, far), axis=1, keepdims=True)
    den = v1 + v2
    x2e_ref[:, D_MODEL:] = (jnp.where(lane == i1, v1 / den * gprob, 0.0)
                            + jnp.where(lane == i2, v2 / den * gprob, 0.0)
                            + jnp.where(lane == 0.0, gsel, 0.0))


def _out_proj(ys, yat, x2d, wglu, bglu, gs, ga, wo, g2, wr, br, *, seq):
    n_tok = x2d.shape[0]
    tm = 512
    nl = seq // tm
    row = lambda i: (i, 0)
    const = lambda i: (0, 0)
    return pl.pallas_call(
        _out_proj_kernel,
        grid=(n_tok // tm,),
        in_specs=[
            pl.BlockSpec((tm, D_SSM), row),
            pl.BlockSpec((None, D_ATT, tm), lambda i: (i // nl, 0, i % nl)),
            pl.BlockSpec((tm, D_MODEL), row),
            pl.BlockSpec((D_SSM, D_SSM), const),
            pl.BlockSpec((1, D_SSM), const),
            pl.BlockSpec((1, D_SSM), const),
            pl.BlockSpec((D_ATT, 1), const),
            pl.BlockSpec((D_SSM + D_ATT, D_MODEL), const),
            pl.BlockSpec((1, D_MODEL), const),
            pl.BlockSpec((D_MODEL, LANES), const),
            pl.BlockSpec((1, LANES), const),
        ],
        out_specs=pl.BlockSpec((tm, D_MODEL + LANES), row),
        out_shape=jax.ShapeDtypeStruct((n_tok, D_MODEL + LANES), F32),
        compiler_params=_params(("parallel",)),
        name="out_proj",
    )(ys, yat, x2d, wglu, bglu, gs, ga, wo, g2, wr, br)


def _moe_plan(gsel, n_tok):
    tmx = MOE_TM
    n_tiles = n_tok // tmx + N_EXP_GROUPS
    oh = (gsel[:, None] == jnp.arange(N_EXP_GROUPS)[None, :]).astype(jnp.int32)
    csum = jnp.cumsum(oh, axis=0)
    counts = csum[-1]
    rank = jnp.sum(csum * oh, axis=1) - 1
    nt = (counts + tmx - 1) // tmx
    tend = jnp.cumsum(nt)
    toff = tend - nt
    pos = jnp.sum(oh * toff[None, :], axis=1) * tmx + rank
    tok = jnp.zeros((n_tiles * tmx,), jnp.int32).at[pos].set(jnp.arange(n_tok, dtype=jnp.int32))
    tile = jnp.arange(n_tiles)
    grp = jnp.minimum(jnp.sum((tile[:, None] >= tend[None, :]).astype(jnp.int32), axis=1), N_EXP_GROUPS - 1)
    nval = jnp.clip(counts[grp] - (tile - toff[grp]) * tmx, 0, tmx)
    return grp.astype(jnp.int32), nval.astype(jnp.int32), tok


def _moe_kernel(grp_ref, nval_ref, tok_ref, x_hbm, g2_ref, wg_ref, wu_ref, wd_ref, o_hbm,
                xbuf, obuf, abuf, gsem, ssem, *, tmx, n_tiles):
    i = pl.program_id(0)
    slot = i % 2

    def gather_row(tile, r, dst_slot):
        t = tok_ref[tile * tmx + r]
        return pltpu.make_async_copy(x_hbm.at[pl.ds(t, 1), :], xbuf.at[dst_slot, pl.ds(r, 1), :], gsem.at[dst_slot])

    def scatter_row(tile, r, src_slot):
        t = tok_ref[tile * tmx + r]
        return pltpu.make_async_copy(obuf.at[src_slot, pl.ds(r, 1), :], o_hbm.at[pl.ds(t, 1), :], ssem.at[src_slot])

    def gather_wait(src_slot):
        pltpu.make_async_copy(x_hbm.at[pl.ds(0, tmx), :], xbuf.at[src_slot], gsem.at[src_slot]).wait()

    def scatter_wait(n, src_slot):
        n8 = pl.multiple_of((n // 8) * 8, 8)

        @pl.when(n8 > 0)
        def _():
            pltpu.make_async_copy(obuf.at[src_slot, pl.ds(0, n8), :], o_hbm.at[pl.ds(0, n8), :],
                                  ssem.at[src_slot]).wait()

        def one(r, c):
            pltpu.make_async_copy(obuf.at[src_slot, pl.ds(0, 1), :], o_hbm.at[pl.ds(0, 1), :],
                                  ssem.at[src_slot]).wait()
            return c

        lax.fori_loop(0, n - n8, one, 0)

    @pl.when(i == 0)
    def _():
        def body(r, c):
            gather_row(0, r, 0).start()
            return c

        lax.fori_loop(0, tmx, body, 0)

    gather_wait(slot)

    @pl.when(i >= 2)
    def _():
        scatter_wait(nval_ref[i - 2], slot)

    nxt = jnp.minimum(i + 1, n_tiles - 1)
    per = tmx // EXPERTS_PER_GROUP
    prev_full = (i >= 1) & (nval_ref[jnp.maximum(i - 1, 0)] == tmx)

    def experts(with_scatter):
        xe = xbuf[slot]
        x2 = xe[:, :D_MODEL]
        cw = xe[:, D_MODEL:]
        h = (x2 * lax.rsqrt(jnp.mean(x2 * x2, axis=-1, keepdims=True) + EPS) * g2_ref[...]).astype(BF16)
        lane = lax.broadcasted_iota(jnp.int32, cw.shape, 1)
        first = ROUTER_OFF + EXPERTS_PER_GROUP * grp_ref[i]
        for k in range(EXPERTS_PER_GROUP):
            for r in range(k * per, (k + 1) * per):
                gather_row(nxt, r, 1 - slot).start()
                if with_scatter:
                    scatter_row(i - 1, r, 1 - slot).start(priority=1)
            gate = _dot(h, wg_ref[k])
            up = _dot(h, wu_ref[k])
            ck = jnp.sum(jnp.where(lane == first + k, cw, 0.0), axis=1, keepdims=True)
            abuf[:, k * D_EXPERT:(k + 1) * D_EXPERT] = (gate * _sigmoid(gate) * up * ck).astype(BF16)
        obuf[slot] = x2 + _dot(abuf[...], wd_ref[...])

    @pl.when(prev_full)
    def _():
        experts(True)

    @pl.when(jnp.logical_not(prev_full))
    def _():
        experts(False)

    nv = nval_ref[i]

    @pl.when(nv < tmx)
    def _():
        def body(r, c):
            scatter_row(i, r, slot).start()
            return c

        lax.fori_loop(0, nv, body, 0)

    @pl.when(i == n_tiles - 1)
    def _():
        @pl.when(nv == tmx)
        def _():
            def body(r, c):
                scatter_row(i, r, slot).start()
                return c

            lax.fori_loop(0, tmx, body, 0)

        gather_wait(1 - slot)
        scatter_wait(nval_ref[i - 1], 1 - slot)
        scatter_wait(nv, slot)


def _moe(x2e, grp, nval, tok, g2, wg, wu, wd):
    n_tok = x2e.shape[0]
    tmx = MOE_TM
    n_tiles = grp.shape[0]
    kern = functools.partial(_moe_kernel, tmx=tmx, n_tiles=n_tiles)
    gk = EXPERTS_PER_GROUP * D_EXPERT
    grid_spec = pltpu.PrefetchScalarGridSpec(
        num_scalar_prefetch=3,
        grid=(n_tiles,),
        in_specs=[
            pl.BlockSpec(memory_space=pl.ANY),
            pl.BlockSpec((1, D_MODEL), lambda i, g, n, t: (0, 0)),
            pl.BlockSpec((None, EXPERTS_PER_GROUP, D_MODEL, D_EXPERT), lambda i, g, n, t: (g[i], 0, 0, 0)),
            pl.BlockSpec((None, EXPERTS_PER_GROUP, D_MODEL, D_EXPERT), lambda i, g, n, t: (g[i], 0, 0, 0)),
            pl.BlockSpec((None, gk, D_MODEL), lambda i, g, n, t: (g[i], 0, 0)),
        ],
        out_specs=pl.BlockSpec(memory_space=pl.ANY),
        scratch_shapes=[
            pltpu.VMEM((2, tmx, D_MODEL + LANES), F32),
            pltpu.VMEM((2, tmx, D_MODEL), F32),
            pltpu.VMEM((tmx, gk), BF16),
            pltpu.SemaphoreType.DMA((2,)),
            pltpu.SemaphoreType.DMA((2,)),
        ],
    )
    return pl.pallas_call(
        kern,
        grid_spec=grid_spec,
        out_shape=jax.ShapeDtypeStruct((n_tok, D_MODEL), F32),
        compiler_params=_params(("arbitrary",)),
        name="moe",
    )(grp, nval, tok, x2e, g2, wg, wu, wd)


def _block_diag_ones(n, blk):
    i = jnp.arange(n) // blk
    return (i[:, None] == i[None, :]).astype(BF16)


def _layer(x, norm1_g, w_in, lam_re, lam_im, log_step, b_re, b_im, c_re, c_im, d_skip,
           w_glu, b_glu, g_q, g_kc, g_ks, g_kw, pos_k, pos_v, w_ck1, w_ck2, w_cv1, w_cv2,
           out_g_ssm, out_g_att, w_out, norm2_g, w_grp, b_grp, w_exp, b_exp, w_gate, w_up, w_down):
    bsz, seq, _ = x.shape
    assert seq % 512 == 0 and seq // SEL_BLOCK <= MAX_SEL_BLOCKS
    n_tok = bsz * seq
    x2d = x.reshape(n_tok, D_MODEL)
    q8 = SSM_Q
    n_sub = seq // q8
    nch = seq // CMP_STRIDE

    o_q = D_SSM
    o_kv = D_SSM + D_ATT
    o_gt = o_kv + 6 * D_KV
    kv = lambda i: w_in[:, o_kv + i * D_KV:o_kv + (i + 1) * D_KV]
    wrow = jnp.concatenate([w_in[:, :o_q], kv(0), kv(1), kv(2), kv(4)], axis=1).astype(BF16)
    per_g = HPG * N_BRANCH
    wgt = jnp.zeros((D_MODEL, N_KV * GATE_ROWS), F32)
    for g in range(N_KV):
        wgt = wgt.at[:, g * GATE_ROWS:g * GATE_ROWS + per_g].set(w_in[:, o_gt + g * per_g:o_gt + (g + 1) * per_g])
    wcol = jnp.concatenate([w_in[:, o_q:o_kv], kv(3), kv(5), wgt], axis=1).T.astype(BF16)
    qscale = (HEAD_DIM ** -0.5) * math.log2(math.e)
    gq = (jnp.tile(g_q.astype(F32), N_HEADS) * qscale).reshape(D_ATT, 1)
    gks = jnp.tile(g_ks.astype(F32), N_KV).reshape(1, D_KV)
    gkw = jnp.tile(g_kw.astype(F32), N_KV).reshape(1, D_KV)

    u, qt, kc, vc, ksa, kw, vst, vwt, gate = _in_proj(
        x2d, norm1_g.reshape(1, D_MODEL), wrow, wcol, gq, gks, gkw,
        _block_diag_ones(D_KV, HEAD_DIM), bsz=bsz, seq=seq)

    w_loc, t_in, m_st, pw_re, pw_im, dvec = _s5_weights(
        lam_re, lam_im, log_step, b_re, b_im, c_re, c_im, d_skip, n_sub)
    ys = _s5(u.reshape(bsz, seq, D_SSM), w_loc, t_in, m_st, pw_re, pw_im, dvec,
             bsz=bsz, n_sub=n_sub).reshape(n_tok, D_SSM)

    wide = CMP_STRIDE * HEAD_DIM
    pad8 = lambda p: jnp.zeros((8, 2 * wide), F32).at[0].set(p.reshape(-1)).astype(BF16)
    kcmp, vcmpt = _compress(
        kc, vc,
        w_ck1.astype(BF16), w_ck2.astype(BF16), w_cv1.astype(BF16), w_cv2.T.astype(BF16),
        pad8(pos_k), pad8(pos_v), g_kc.astype(F32).reshape(1, HEAD_DIM), bsz=bsz, nch=nch)
    cstart = jnp.arange(nch) * CMP_STRIDE
    sstart = jnp.arange(MAX_SEL_BLOCKS) * SEL_BLOCK
    ovlt = ((cstart[None, :] < sstart[:, None] + SEL_BLOCK) & (cstart[None, :] + CMP_BLOCK > sstart[:, None])
            & (jnp.arange(MAX_SEL_BLOCKS)[:, None] < seq // SEL_BLOCK)
            & (jnp.arange(nch)[None, :] < nch - 1)).astype(BF16)
    yat = _nsa(qt, kcmp, vcmpt, ksa, kw, vst, vwt, gate, ovlt, bsz=bsz, seq=seq, nch=nch)

    wr = jnp.zeros((D_MODEL, LANES), F32)
    wr = wr.at[:, :N_EXP_GROUPS].set(w_grp).at[:, ROUTER_OFF:ROUTER_OFF + N_EXPERTS].set(w_exp).astype(BF16)
    br = jnp.zeros((1, LANES), F32)
    br = br.at[0, :N_EXP_GROUPS].set(b_grp).at[0, ROUTER_OFF:ROUTER_OFF + N_EXPERTS].set(b_exp)
    g2 = norm2_g.reshape(1, D_MODEL).astype(F32)
    x2e = _out_proj(
        ys, yat, x2d, w_glu.astype(BF16), b_glu.reshape(1, D_SSM).astype(F32),
        out_g_ssm.reshape(1, D_SSM).astype(F32), out_g_att.reshape(D_ATT, 1).astype(F32),
        w_out.astype(BF16), g2, wr, br, seq=seq)

    grp, nval, tok = _moe_plan(x2e[:, D_MODEL].astype(jnp.int32), n_tok)
    gshape = (N_EXP_GROUPS, EXPERTS_PER_GROUP, D_MODEL, D_EXPERT)
    out = _moe(x2e, grp, nval, tok, g2, w_gate.astype(BF16).reshape(gshape), w_up.astype(BF16).reshape(gshape),
               w_down.astype(BF16).reshape(N_EXP_GROUPS, EXPERTS_PER_GROUP * D_EXPERT, D_MODEL))
    return out.reshape(bsz, seq, D_MODEL)


def kernel(x, norm1_g, w_in, lam_re, lam_im, log_step, b_re, b_im, c_re, c_im, d_skip, w_glu, b_glu, g_q, g_kc, g_ks, g_kw, pos_k, pos_v, w_ck1, w_ck2, w_cv1, w_cv2, out_g_ssm, out_g_att, w_out, norm2_g, w_grp, b_grp, w_exp, b_exp, w_gate, w_up, w_down):
    depth = norm1_g.shape[0]
    for l in range(depth):
        x = _layer(x, norm1_g[l], w_in[l], lam_re[l], lam_im[l], log_step[l], b_re[l], b_im[l], c_re[l],
                   c_im[l], d_skip[l], w_glu[l], b_glu[l], g_q[l], g_kc[l], g_ks[l], g_kw[l], pos_k[l],
                   pos_v[l], w_ck1[l], w_ck2[l], w_cv1[l], w_cv2[l], out_g_ssm[l], out_g_att[l], w_out[l],
                   norm2_g[l], w_grp[l], b_grp[l], w_exp[l], b_exp[l], w_gate[l], w_up[l], w_down[l])
    return x
```

```python
import functools
import math

import jax
import jax.numpy as jnp
from jax import lax
from jax.experimental import pallas as pl
from jax.experimental.pallas import tpu as pltpu

D_MODEL = 1024
D_SSM = 512
SSM_CH = 16
SSM_GROUPS = D_SSM // SSM_CH
SSM_STATE = 64
D_ATT = 512
HEAD_DIM = 64
N_HEADS = D_ATT // HEAD_DIM
N_KV = 2
HPG = N_HEADS // N_KV
D_KV = N_KV * HEAD_DIM
N_BRANCH = 3
CMP_STRIDE = 16
CMP_BLOCK = 2 * CMP_STRIDE
CMP_HIDDEN = 256
SEL_BLOCK = 64
N_SELECT = 16
WINDOW = 512
N_EXP_GROUPS = 4
EXPERTS_PER_GROUP = 8
N_EXPERTS = N_EXP_GROUPS * EXPERTS_PER_GROUP
D_EXPERT = 256
EPS = 1e-6
NEG = -1e30
FORCE = 1e9

LANES = 128
SSM_Q = 8
SSM_LT = D_SSM // LANES
ROUTER_OFF = N_EXP_GROUPS
NSA_TQ = 256
NSA_TK = 256
NSA_UNROLL = 8
V_ROWS = HEAD_DIM + 16
MAX_SEL_BLOCKS = 64
MOE_TM = 256
PROJ_TM = 1024
GATE_ROWS = 16
V7X_VMEM_BYTES = 64 * 1024 * 1024
VMEM_LIMIT = V7X_VMEM_BYTES - 8 * 1024 * 1024

F32 = jnp.float32
BF16 = jnp.bfloat16


def _dot(a, b):
    return jnp.dot(a, b, preferred_element_type=F32)


def _dot_nt(a, b):
    return lax.dot_general(a, b, (((1,), (1,)), ((), ())), preferred_element_type=F32)


def _split_dot(x, w):
    hi = x.astype(BF16)
    lo = (x - hi.astype(F32)).astype(BF16)
    return _dot(hi, w) + _dot(lo, w)


def _gelu(x):
    c = math.sqrt(2.0 / math.pi)
    return 0.5 * x * (1.0 + jnp.tanh(c * (x + 0.044715 * (x * x * x))))


def _sigmoid(x):
    return 1.0 / (1.0 + jnp.exp(-x))


def _params(sem):
    return pltpu.CompilerParams(dimension_semantics=sem, vmem_limit_bytes=VMEM_LIMIT)


def _in_proj_kernel(x_ref, g1_ref, wrow_ref, wcol_ref, gq_ref, gks_ref, gkw_ref, bd128_ref,
                    u_ref, qt_ref, kc_ref, vc_ref, ksa_ref, kw_ref, vst_ref, vwt_ref, gate_ref, *, tm, nl):
    x = x_ref[...]
    ms = jnp.mean(x * x, axis=-1, keepdims=True)
    hn = (x * lax.rsqrt(ms + EPS) * g1_ref[...]).astype(BF16)

    pr = _dot(hn, wrow_ref[...])
    u_ref[...] = pr[:, :D_SSM]
    kc, vc, ks, kw = [pr[:, D_SSM + i * D_KV:D_SSM + (i + 1) * D_KV] for i in range(4)]
    kss = _split_dot(ks * ks, bd128_ref[...])
    ksn = ks * lax.rsqrt(kss * (1.0 / HEAD_DIM) + EPS) * gks_ref[...]
    kws = _split_dot(kw * kw, bd128_ref[...])
    kwn = kw * lax.rsqrt(kws * (1.0 / HEAD_DIM) + EPS) * gkw_ref[...]
    t0 = (pl.program_id(0) % nl) * tm
    tpos = t0 + lax.broadcasted_iota(jnp.int32, (tm, MAX_SEL_BLOCKS), 0)
    blk = lax.broadcasted_iota(jnp.int32, (tm, MAX_SEL_BLOCKS), 1)
    onehot = jnp.where(tpos // SEL_BLOCK == blk, 1.0, 0.0).astype(BF16)
    for g in range(N_KV):
        sl = slice(g * HEAD_DIM, (g + 1) * HEAD_DIM)
        kc_ref[g] = kc[:, sl]
        vc_ref[g] = vc[:, sl]
        ksa_ref[g] = jnp.concatenate([ksn[:, sl].astype(BF16), onehot], axis=1)
        kw_ref[g] = kwn[:, sl].astype(BF16)

    pc = _dot_nt(wcol_ref[...], hn)
    gq = gq_ref[...]
    for h in range(N_HEADS):
        sl = slice(h * HEAD_DIM, (h + 1) * HEAD_DIM)
        qh = pc[sl]
        ss = jnp.sum(qh * qh, axis=0, keepdims=True)
        qt_ref[h] = (qh * lax.rsqrt(ss * (1.0 / HEAD_DIM) + EPS) * gq[sl]).astype(BF16)
    ones_rows = jnp.where(lax.broadcasted_iota(jnp.int32, (V_ROWS - HEAD_DIM, tm), 0) == 0, 1.0, 0.0)
    for g in range(N_KV):
        for o_ref, base in ((vst_ref, D_ATT), (vwt_ref, D_ATT + D_KV)):
            vt = jnp.concatenate([pc[base + g * HEAD_DIM:base + (g + 1) * HEAD_DIM], ones_rows], axis=0)
            vt = vt.astype(BF16)
            for j in range(tm // NSA_TK):
                o_ref[g, j] = vt[:, j * NSA_TK:(j + 1) * NSA_TK]
        gb = D_ATT + 2 * D_KV + g * GATE_ROWS
        gate_ref[g] = _sigmoid(pc[gb:gb + GATE_ROWS])


def _in_proj(x2d, g1, wrow, wcol, gq, gks, gkw, bd128, *, bsz, seq):
    tm = PROJ_TM
    nl = seq // tm
    n_tok = bsz * seq
    kern = functools.partial(_in_proj_kernel, tm=tm, nl=nl)
    row = lambda i: (i, 0)
    const = lambda i: (0, 0)
    bgl = lambda i: (i // nl, 0, i % nl, 0)
    n_col = wcol.shape[0]
    jt = tm // NSA_TK

    def kvspec(width):
        return pl.BlockSpec((None, N_KV, tm, width), bgl)

    def kvshape(width, dtype=BF16):
        return jax.ShapeDtypeStruct((bsz, N_KV, seq, width), dtype)

    vt_spec = pl.BlockSpec((None, N_KV, jt, V_ROWS, NSA_TK), lambda i: (i // nl, 0, i % nl, 0, 0))
    vt_shape = jax.ShapeDtypeStruct((bsz, N_KV, seq // NSA_TK, V_ROWS, NSA_TK), BF16)
    return pl.pallas_call(
        kern,
        grid=(n_tok // tm,),
        in_specs=[
            pl.BlockSpec((tm, D_MODEL), row),
            pl.BlockSpec((1, D_MODEL), const),
            pl.BlockSpec((D_MODEL, D_SSM + 4 * D_KV), const),
            pl.BlockSpec((n_col, D_MODEL), const),
            pl.BlockSpec((D_ATT, 1), const),
            pl.BlockSpec((1, D_KV), const),
            pl.BlockSpec((1, D_KV), const),
            pl.BlockSpec((D_KV, D_KV), const),
        ],
        out_specs=[
            pl.BlockSpec((tm, D_SSM), row),
            pl.BlockSpec((None, N_HEADS, HEAD_DIM, tm), lambda i: (i // nl, 0, 0, i % nl)),
            kvspec(HEAD_DIM), kvspec(HEAD_DIM), kvspec(2 * HEAD_DIM), kvspec(HEAD_DIM),
            vt_spec, vt_spec,
            pl.BlockSpec((None, N_KV, GATE_ROWS, tm), lambda i: (i // nl, 0, 0, i % nl)),
        ],
        out_shape=[
            jax.ShapeDtypeStruct((n_tok, D_SSM), F32),
            jax.ShapeDtypeStruct((bsz, N_HEADS, HEAD_DIM, seq), BF16),
            kvshape(HEAD_DIM, F32), kvshape(HEAD_DIM, F32), kvshape(2 * HEAD_DIM), kvshape(HEAD_DIM),
            vt_shape, vt_shape,
            jax.ShapeDtypeStruct((bsz, N_KV, GATE_ROWS, seq), F32),
        ],
        compiler_params=_params(("parallel",)),
        name="in_proj",
    )(x2d, g1, wrow, wcol, gq, gks, gkw, bd128)


def _s5_weights(lam_re, lam_im, log_step, b_re, b_im, c_re, c_im, d_skip, n_sub):
    q = SSM_Q
    lam = lax.complex(lam_re.astype(F32), lam_im.astype(F32))
    step = jnp.exp(log_step.astype(F32))[:, None]
    lam_bar = jnp.exp(lam * step)
    b_bar = ((lam_bar - 1.0) / lam)[..., None] * lax.complex(b_re.astype(F32), b_im.astype(F32))
    c = lax.complex(c_re.astype(F32), c_im.astype(F32))
    pows = [jnp.ones_like(lam_bar)]
    for _ in range(q):
        pows.append(pows[-1] * lam_bar)
    pw = jnp.stack(pows)
    lt, a8 = SSM_LT, LANES // SSM_CH
    hp = a8 * SSM_STATE
    e_lane = (jnp.arange(a8)[:, None] == jnp.arange(LANES)[None, :] // SSM_CH).astype(F32)
    e_state = (jnp.arange(a8)[:, None] == jnp.arange(hp)[None, :] // SSM_STATE).astype(F32)
    e_lane_t = jnp.tile(e_lane, (1, q))
    e_state_t = jnp.tile(e_state, (1, 2))

    kk = jnp.real(jnp.einsum('ghp,kgp,gpi->kghi', c, pw[:q], b_bar))
    km = kk.reshape(q, lt, a8, SSM_CH, SSM_CH).transpose(1, 4, 0, 2, 3).reshape(lt, SSM_CH, q, LANES)
    lag = jnp.arange(q)[None, :] - jnp.arange(q)[:, None]
    kg = km[:, :, jnp.clip(lag, 0, q - 1), :] * (lag >= 0)[None, None, :, :, None].astype(F32)
    kc = kg.transpose(0, 2, 1, 3, 4).reshape(lt, q, 1, SSM_CH, q * LANES)
    t_in = (kc * e_lane_t[None, None, :, None, :]).reshape(lt, q * LANES, q * LANES)

    wc = pw[q - 1 - jnp.arange(q)][..., None] * b_bar[None]
    wri = jnp.stack([jnp.real(wc), jnp.imag(wc)])
    wm = (wri.reshape(2, q, lt, a8, SSM_STATE, SSM_CH).transpose(2, 1, 5, 0, 3, 4)
          .reshape(lt, q, 1, SSM_CH, 2 * hp))
    w_loc = (wm * e_state_t[None, None, :, None, :]).reshape(lt, q * LANES, 2 * hp)

    cl = c[None] * pw[1:q + 1][:, :, None, :]
    cri = jnp.stack([jnp.real(cl), -jnp.imag(cl)])
    mm = (cri.reshape(2, q, lt, a8, SSM_CH, SSM_STATE).transpose(2, 0, 5, 1, 3, 4)
          .reshape(lt, 2, 1, SSM_STATE, q * LANES))
    m_st = (mm * e_lane_t[None, None, :, None, :]).reshape(lt, 2 * hp, q * LANES)

    n_lvl = max(1, (n_sub - 1).bit_length())
    lv = [pw[q]]
    for _ in range(n_lvl - 1):
        lv.append(lv[-1] * lv[-1])
    lvs = jnp.stack(lv).reshape(n_lvl, lt, 1, hp)
    pw_re = jnp.real(lvs).transpose(1, 0, 2, 3)
    pw_im = jnp.imag(lvs).transpose(1, 0, 2, 3)
    dvec = jnp.tile(d_skip.astype(F32).reshape(lt, 1, LANES), (1, 1, q))
    return w_loc.astype(BF16), t_in.astype(BF16), m_st.astype(BF16), pw_re, pw_im, dvec


def _s5_kernel(u_ref, w_ref, t_ref, m_ref, pwr_ref, pwi_ref, d_ref, y_ref, *, n_sub, n_lvl):
    half = (LANES // SSM_CH) * SSM_STATE
    q = SSM_Q
    u = jnp.concatenate([u_ref[pl.ds(s, n_sub, stride=q), :] for s in range(q)], axis=1)
    ub = u.astype(BF16)
    s_loc = _dot(ub, w_ref[...])
    re = s_loc[:, :half]
    im = s_loc[:, half:]
    rowi = lax.broadcasted_iota(jnp.int32, (n_sub, half), 0)
    for k in range(n_lvl):
        d = 1 << k
        ar = pwr_ref[k]
        ai = pwi_ref[k]
        keep = rowi >= d
        sre = jnp.where(keep, pltpu.roll(re, d, axis=0), 0.0)
        sim = jnp.where(keep, pltpu.roll(im, d, axis=0), 0.0)
        re, im = re + (ar * sre - ai * sim), im + (ar * sim + ai * sre)
    keep = rowi >= 1
    xre = jnp.where(keep, pltpu.roll(re, 1, axis=0), 0.0)
    xim = jnp.where(keep, pltpu.roll(im, 1, axis=0), 0.0)
    xst = jnp.concatenate([xre, xim], axis=1).astype(BF16)
    y = _dot(ub, t_ref[...]) + _dot(xst, m_ref[...]) + d_ref[...] * u
    for j in range(q):
        y_ref[pl.ds(j, n_sub, stride=q), :] = y[:, j * LANES:(j + 1) * LANES]


def _s5(u, w_loc, t_in, m_st, pw_re, pw_im, dvec, *, bsz, n_sub):
    q = SSM_Q
    n_lvl = pw_re.shape[1]
    kern = functools.partial(_s5_kernel, n_sub=n_sub, n_lvl=n_lvl)
    wide = q * LANES
    seq = n_sub * q
    return pl.pallas_call(
        kern,
        grid=(bsz, SSM_LT),
        in_specs=[
            pl.BlockSpec((None, seq, LANES), lambda b, l: (b, 0, l)),
            pl.BlockSpec((None, wide, wide), lambda b, l: (l, 0, 0)),
            pl.BlockSpec((None, wide, wide), lambda b, l: (l, 0, 0)),
            pl.BlockSpec((None, wide, wide), lambda b, l: (l, 0, 0)),
            pl.BlockSpec((None, n_lvl, 1, wide // 2), lambda b, l: (l, 0, 0, 0)),
            pl.BlockSpec((None, n_lvl, 1, wide // 2), lambda b, l: (l, 0, 0, 0)),
            pl.BlockSpec((None, 1, wide), lambda b, l: (l, 0, 0)),
        ],
        out_specs=pl.BlockSpec((None, seq, LANES), lambda b, l: (b, 0, l)),
        out_shape=jax.ShapeDtypeStruct((bsz, seq, D_SSM), F32),
        compiler_params=_params(("parallel", "parallel")),
        name="s5",
    )(u, w_loc, t_in, m_st, pw_re, pw_im, dvec)


def _compress_kernel(kc_ref, vc_ref, w1k_ref, w2k_ref, w1v_ref, w2vt_ref, posk_ref, posv_ref, gkc_ref,
                     kcmp_ref, vcmpt_ref, *, nch):
    half = CMP_STRIDE * HEAD_DIM

    def hidden(x_ref, w1_ref, pos_ref):
        x = jnp.concatenate([x_ref[pl.ds(j, nch, stride=CMP_STRIDE), :] for j in range(CMP_STRIDE)],
                            axis=1).astype(BF16)
        a = _dot(x, w1_ref[:half, :])
        b = _dot(x, w1_ref[half:, :])
        pv = _dot(pos_ref[...], w1_ref[...])[0:1, :]
        hid = a + pltpu.roll(b, nch - 1, axis=0) + pv
        return _gelu(hid).astype(BF16)

    k = _dot(hidden(kc_ref, w1k_ref, posk_ref), w2k_ref[...])
    ms = jnp.mean(k * k, axis=-1, keepdims=True)
    kcmp_ref[...] = (k * lax.rsqrt(ms + EPS) * gkc_ref[...]).astype(BF16)
    vt = _dot_nt(w2vt_ref[...], hidden(vc_ref, w1v_ref, posv_ref))
    coli = lax.broadcasted_iota(jnp.int32, vt.shape, 1)
    vcmpt_ref[...] = jnp.where(coli < nch - 1, vt, 0.0).astype(BF16)


def _compress(kcf, vcf, w1k, w2k, w1v, w2vt, posk, posv, gkc, *, bsz, nch):
    kern = functools.partial(_compress_kernel, nch=nch)
    wide = CMP_STRIDE * HEAD_DIM
    xspec = pl.BlockSpec((None, None, nch * CMP_STRIDE, HEAD_DIM), lambda b, g: (b, g, 0, 0))
    c2 = lambda b, g: (0, 0)
    return pl.pallas_call(
        kern,
        grid=(bsz, N_KV),
        in_specs=[
            xspec, xspec,
            pl.BlockSpec((2 * wide, CMP_HIDDEN), c2), pl.BlockSpec((CMP_HIDDEN, HEAD_DIM), c2),
            pl.BlockSpec((2 * wide, CMP_HIDDEN), c2), pl.BlockSpec((HEAD_DIM, CMP_HIDDEN), c2),
            pl.BlockSpec((8, 2 * wide), c2), pl.BlockSpec((8, 2 * wide), c2),
            pl.BlockSpec((1, HEAD_DIM), c2),
        ],
        out_specs=[pl.BlockSpec((None, None, nch, HEAD_DIM), lambda b, g: (b, g, 0, 0)),
                   pl.BlockSpec((None, None, HEAD_DIM, nch), lambda b, g: (b, g, 0, 0))],
        out_shape=[jax.ShapeDtypeStruct((bsz, N_KV, nch, HEAD_DIM), BF16),
                   jax.ShapeDtypeStruct((bsz, N_KV, HEAD_DIM, nch), BF16)],
        compiler_params=_params(("parallel", "parallel")),
        name="compress",
    )(kcf, vcf, w1k, w2k, w1v, w2vt, posk, posv, gkc)


def _nsa_kernel(qt_ref, kcmp_ref, vcmpt_ref, ksa_ref, kw_ref, vst_ref, vwt_ref, gate_ref, ovlt_ref, o_ref,
                accs_ref, accw_ref, sel_ref, *, tq, tk, nch):
    qi = pl.program_id(2)
    q0 = qi * tq
    rows = HPG * tq
    qt = jnp.concatenate([qt_ref[h] for h in range(HPG)], axis=1)
    tpos = q0 + lax.broadcasted_iota(jnp.int32, (1, rows), 1) % tq
    krow = lax.broadcasted_iota(jnp.int32, (tk, rows), 0)
    m0 = jnp.full((1, rows), NEG, F32)

    def run_tiles(k_ref, q_all, vt_ref, acc_ref, m, tiles):
        scores = []
        for kt, kind in tiles:
            kc = jnp.maximum(kt, 0) if kind in ('band', 'valid') else kt
            s = _dot(k_ref[pl.ds(pl.multiple_of(kc * tk, tk), tk), :], q_all)
            if kind == 'causal':
                s = jnp.where(kt * tk + krow <= tpos, s, NEG)
            elif kind == 'band':
                s = jnp.where((kt * tk + krow > tpos - WINDOW) & (kt >= 0), s, NEG)
            elif kind == 'valid':
                s = jnp.where(kt >= 0, s, NEG)
            scores.append((kc, s))
        for kc, s in scores:
            m_new = jnp.maximum(m, jnp.max(s, axis=0, keepdims=True))
            alpha = jnp.exp2(m - m_new)
            p = jnp.exp2(s - m_new).astype(BF16)
            acc_ref[...] = alpha * acc_ref[...] + _dot(vt_ref[kc], p)
            m = m_new
        return m

    accw_ref[...] = jnp.zeros_like(accw_ref)
    n_win = WINDOW // tk
    run_tiles(kw_ref, qt, vwt_ref, accw_ref, m0,
              [(qi - n_win, 'band')] + [(qi - n_win + t, 'valid') for t in range(1, n_win)] + [(qi, 'causal')])

    s = _dot(kcmp_ref[...], qt)
    cend = lax.broadcasted_iota(jnp.int32, (nch, rows), 0) * CMP_STRIDE + (CMP_BLOCK - 1)
    s = jnp.where(cend <= tpos, s, NEG)
    m = jnp.max(s, axis=0, keepdims=True)
    p = jnp.exp2(s - m)
    p = p * jnp.where(tpos >= CMP_BLOCK - 1, 1.0 / jnp.sum(p, axis=0, keepdims=True), 0.0)
    o_cmp = _dot(vcmpt_ref[...], p.astype(BF16))

    psum = p[:, 0:tq]
    for h in range(1, HPG):
        psum = psum + p[:, h * tq:(h + 1) * tq]
    hi = psum.astype(BF16)
    lo = (psum - hi.astype(F32)).astype(BF16)
    ovlt = ovlt_ref[...]
    imp = _dot(ovlt, hi) + _dot(ovlt, lo)
    nb = MAX_SEL_BLOCKS
    blk = lax.broadcasted_iota(jnp.int32, (nb, tq), 0)
    cur = (q0 + lax.broadcasted_iota(jnp.int32, (nb, tq), 1)) // SEL_BLOCK
    forced = (blk == 0) | (blk == cur) | (blk == cur - 1)
    imp = jnp.where(forced, FORCE, jnp.where(blk <= cur, imp, NEG))
    sub = 8
    rowl = lax.broadcasted_iota(jnp.int32, (sub, tq), 0)
    n_seen = (q0 + tq - 1) // SEL_BLOCK + 1
    sel_ref[...] = jnp.zeros_like(sel_ref)
    for nbv in range(2 * N_SELECT, nb + 1, N_SELECT):
        @pl.when((n_seen > nbv - N_SELECT) & (n_seen <= nbv))
        def _(nbv=nbv):
            groups = [imp[r:r + sub] for r in range(0, nbv, sub)]
            ranks = [jnp.zeros((sub, tq), F32) for _ in groups]
            for i in range(nbv):
                ri = jnp.broadcast_to(imp[i:i + 1, :], (sub, tq))
                for gi, x in enumerate(groups):
                    if i < gi * sub:
                        ahead = ri >= x
                    elif i >= (gi + 1) * sub:
                        ahead = ri > x
                    else:
                        ahead = (ri > x) | ((ri == x) & (rowl > i - gi * sub))
                    ranks[gi] = ranks[gi] + jnp.where(ahead, 1.0, 0.0)
            rank = jnp.concatenate(ranks, axis=0)
            sel_ref[0:nbv, :] = jnp.where(rank < N_SELECT, 0.0, NEG).astype(BF16)
    sel = sel_ref[...]
    qa = jnp.concatenate([qt, jnp.concatenate([sel] * HPG, axis=1)], axis=0)

    accs_ref[...] = jnp.zeros_like(accs_ref)
    nu = NSA_UNROLL

    def sel_multi(j, m):
        return run_tiles(ksa_ref, qa, vst_ref, accs_ref, m, [(nu * j + t, None) for t in range(nu)])

    m_sel = lax.fori_loop(0, qi // nu, sel_multi, m0)
    for r in range(nu):
        @pl.when(qi % nu == r)
        def _(r=r):
            base = qi - r
            run_tiles(ksa_ref, qa, vst_ref, accs_ref, m_sel,
                      [(base + t, None) for t in range(r)] + [(qi, 'causal')])

    accs = accs_ref[...]
    accw = accw_ref[...]
    o_sel = accs[:HEAD_DIM] * (1.0 / accs[HEAD_DIM:HEAD_DIM + 1])
    o_win = accw[:HEAD_DIM] * (1.0 / accw[HEAD_DIM:HEAD_DIM + 1])
    gt = gate_ref[...]
    for h in range(HPG):
        c = slice(h * tq, (h + 1) * tq)
        o_ref[h * HEAD_DIM:(h + 1) * HEAD_DIM, :] = (
            gt[3 * h:3 * h + 1] * o_cmp[:, c] + gt[3 * h + 1:3 * h + 2] * o_sel[:, c]
            + gt[3 * h + 2:3 * h + 3] * o_win[:, c])


def _nsa(qt, kcmp, vcmpt, ksa, kw, vst, vwt, gate, ovlt, *, bsz, seq, nch):
    tq, tk = NSA_TQ, NSA_TK
    assert tq == tk and WINDOW % tk == 0
    nq = seq // tq
    kern = functools.partial(_nsa_kernel, tq=tq, tk=tk, nch=nch)
    full = lambda b, g, i: (b, g, 0, 0)
    full5 = lambda b, g, i: (b, g, 0, 0, 0)
    qd = HPG * HEAD_DIM
    rows = HPG * tq
    return pl.pallas_call(
        kern,
        grid=(bsz, N_KV, nq),
        in_specs=[
            pl.BlockSpec((None, HPG, HEAD_DIM, tq), lambda b, g, i: (b, g, 0, i)),
            pl.BlockSpec((None, None, nch, HEAD_DIM), full),
            pl.BlockSpec((None, None, HEAD_DIM, nch), full),
            pl.BlockSpec((None, None, seq, 2 * HEAD_DIM), full),
            pl.BlockSpec((None, None, seq, HEAD_DIM), full),
            pl.BlockSpec((None, None, seq // tk, V_ROWS, tk), full5),
            pl.BlockSpec((None, None, seq // tk, V_ROWS, tk), full5),
            pl.BlockSpec((None, None, GATE_ROWS, tq), lambda b, g, i: (b, g, 0, i)),
            pl.BlockSpec((MAX_SEL_BLOCKS, nch), lambda b, g, i: (0, 0)),
        ],
        out_specs=pl.BlockSpec((None, qd, tq), lambda b, g, i: (b, g, i)),
        out_shape=jax.ShapeDtypeStruct((bsz, D_ATT, seq), F32),
        scratch_shapes=[pltpu.VMEM((V_ROWS, rows), F32), pltpu.VMEM((V_ROWS, rows), F32),
                        pltpu.VMEM((MAX_SEL_BLOCKS, tq), BF16)],
        compiler_params=_params(("parallel", "parallel", "arbitrary")),
        name="nsa",
    )(qt, kcmp, vcmpt, ksa, kw, vst, vwt, gate, ovlt)


def _out_proj_kernel(ys_ref, yat_ref, x_ref, wglu_ref, bglu_ref, gs_ref, ga_ref, wo_ref, g2_ref, wr_ref, br_ref,
                     x2e_ref):
    y = _gelu(ys_ref[...])
    y = y * _sigmoid(_dot(y.astype(BF16), wglu_ref[...]) + bglu_ref[...])
    ysn = y * lax.rsqrt(jnp.mean(y * y, axis=-1, keepdims=True) + EPS) * gs_ref[...]
    yat = yat_ref[...]
    yant = yat * lax.rsqrt(jnp.mean(yat * yat, axis=0, keepdims=True) + EPS) * ga_ref[...]
    yan = yant.T
    x2 = x_ref[...] + _dot(ysn.astype(BF16), wo_ref[:D_SSM, :]) + _dot(yan.astype(BF16), wo_ref[D_SSM:, :])
    x2e_ref[:, :D_MODEL] = x2
    h2 = (x2 * lax.rsqrt(jnp.mean(x2 * x2, axis=-1, keepdims=True) + EPS) * g2_ref[...]).astype(BF16)

    logits = _dot(h2, wr_ref[...]) + br_ref[...]
    lane = lax.broadcasted_iota(jnp.int32, logits.shape, 1).astype(F32)
    far = float(LANES)
    is_g = lane < N_EXP_GROUPS
    glog = jnp.where(is_g, logits, -jnp.inf)
    gmax = jnp.max(glog, axis=1, keepdims=True)
    gsum = jnp.sum(jnp.where(is_g, jnp.exp(logits - gmax), 0.0), axis=1, keepdims=True)
    gsel = jnp.min(jnp.where(glog == gmax, lane, far), axis=1, keepdims=True)
    gprob = 1.0 / gsum
    lo = ROUTER_OFF + EXPERTS_PER_GROUP * gsel
    in_e = (lane >= lo) & (lane < lo + EXPERTS_PER_GROUP)
    emax = jnp.max(jnp.where(in_e, logits, -jnp.inf), axis=1, keepdims=True)
    eexp = jnp.where(in_e, jnp.exp(logits - emax), 0.0)
    eprob = jnp.where(in_e, eexp / jnp.sum(eexp, axis=1, keepdims=True), -1.0)
    v1 = jnp.max(eprob, axis=1, keepdims=True)
    i1 = jnp.min(jnp.where(eprob == v1, lane, far), axis=1, keepdims=True)
    rest = jnp.where(lane == i1, -1.0, eprob)
    v2 = jnp.max(rest, axis=1, keepdims=True)
    i2 = jnp.min(jnp.where(rest == v2, lane, far), axis=1, keepdims=True)
    den = v1 + v2
    x2e_ref[:, D_MODEL:] = (jnp.where(lane == i1, v1 / den * gprob, 0.0)
                            + jnp.where(lane == i2, v2 / den * gprob, 0.0)
                            + jnp.where(lane == 0.0, gsel, 0.0))


def _out_proj(ys, yat, x2d, wglu, bglu, gs, ga, wo, g2, wr, br, *, seq):
    n_tok = x2d.shape[0]
    tm = PROJ_TM
    nl = seq // tm
    row = lambda i: (i, 0)
    const = lambda i: (0, 0)
    return pl.pallas_call(
        _out_proj_kernel,
        grid=(n_tok // tm,),
        in_specs=[
            pl.BlockSpec((tm, D_SSM), row),
            pl.BlockSpec((None, D_ATT, tm), lambda i: (i // nl, 0, i % nl)),
            pl.BlockSpec((tm, D_MODEL), row),
            pl.BlockSpec((D_SSM, D_SSM), const),
            pl.BlockSpec((1, D_SSM), const),
            pl.BlockSpec((1, D_SSM), const),
            pl.BlockSpec((D_ATT, 1), const),
            pl.BlockSpec((D_SSM + D_ATT, D_MODEL), const),
            pl.BlockSpec((1, D_MODEL), const),
            pl.BlockSpec((D_MODEL, LANES), const),
            pl.BlockSpec((1, LANES), const),
        ],
        out_specs=pl.BlockSpec((tm, D_MODEL + LANES), row),
        out_shape=jax.ShapeDtypeStruct((n_tok, D_MODEL + LANES), F32),
        compiler_params=_params(("parallel",)),
        name="out_proj",
    )(ys, yat, x2d, wglu, bglu, gs, ga, wo, g2, wr, br)


def _moe_plan(gsel, n_tok):
    tmx = MOE_TM
    n_tiles = n_tok // tmx + N_EXP_GROUPS
    oh = (gsel[:, None] == jnp.arange(N_EXP_GROUPS)[None, :]).astype(jnp.int32)
    csum = jnp.cumsum(oh, axis=0)
    counts = csum[-1]
    rank = jnp.sum(csum * oh, axis=1) - 1
    nt = (counts + tmx - 1) // tmx
    tend = jnp.cumsum(nt)
    toff = tend - nt
    pos = jnp.sum(oh * toff[None, :], axis=1) * tmx + rank
    tile = jnp.arange(n_tiles)
    grp = jnp.minimum(jnp.sum((tile[:, None] >= tend[None, :]).astype(jnp.int32), axis=1), N_EXP_GROUPS - 1)
    nval = jnp.clip(counts[grp] - (tile - toff[grp]) * tmx, 0, tmx)
    return grp.astype(jnp.int32), nval.astype(jnp.int32), pos.astype(jnp.int32)


def _moe_kernel(grp_ref, nval_ref, pos_ref, x_hbm, g2_ref, wg_ref, wu_ref, wd_ref, o_hbm,
                xbuf, obuf, abuf, tok_ref, gsem, ssem, *, tmx, n_tiles, n_tok):
    i = pl.program_id(0)
    slot = i % 2

    @pl.when(i == 0)
    def _():
        def place(t, c):
            tok_ref[pos_ref[t]] = t
            return c

        lax.fori_loop(0, n_tok, place, 0, unroll=8)

        def pad_tile(tile, c):
            def pad_row(r, c2):
                tok_ref[tile * tmx + r] = 0
                return c2

            return lax.fori_loop(nval_ref[tile], tmx, pad_row, c)

        lax.fori_loop(0, n_tiles, pad_tile, 0)

    def gather_row(tile, r, dst_slot):
        t = tok_ref[tile * tmx + r]
        return pltpu.make_async_copy(x_hbm.at[pl.ds(t, 1), :], xbuf.at[dst_slot, pl.ds(r, 1), :], gsem.at[dst_slot])

    def scatter_row(tile, r, src_slot):
        t = tok_ref[tile * tmx + r]
        return pltpu.make_async_copy(obuf.at[src_slot, pl.ds(r, 1), :], o_hbm.at[pl.ds(t, 1), :], ssem.at[src_slot])

    def gather_wait(src_slot):
        pltpu.make_async_copy(x_hbm.at[pl.ds(0, tmx), :], xbuf.at[src_slot], gsem.at[src_slot]).wait()

    def scatter_wait(n, src_slot):
        n8 = pl.multiple_of((n // 8) * 8, 8)

        @pl.when(n8 > 0)
        def _():
            pltpu.make_async_copy(obuf.at[src_slot, pl.ds(0, n8), :], o_hbm.at[pl.ds(0, n8), :],
                                  ssem.at[src_slot]).wait()

        def one(r, c):
            pltpu.make_async_copy(obuf.at[src_slot, pl.ds(0, 1), :], o_hbm.at[pl.ds(0, 1), :],
                                  ssem.at[src_slot]).wait()
            return c

        lax.fori_loop(0, n - n8, one, 0)

    @pl.when(i == 0)
    def _():
        def body(r, c):
            gather_row(0, r, 0).start()
            return c

        lax.fori_loop(0, tmx, body, 0)

    gather_wait(slot)

    @pl.when(i >= 2)
    def _():
        scatter_wait(nval_ref[i - 2], slot)

    nxt = jnp.minimum(i + 1, n_tiles - 1)
    per = tmx // EXPERTS_PER_GROUP
    prev_full = (i >= 1) & (nval_ref[jnp.maximum(i - 1, 0)] == tmx)

    def experts(with_scatter):
        xe = xbuf[slot]
        x2 = xe[:, :D_MODEL]
        cw = xe[:, D_MODEL:]
        h = (x2 * lax.rsqrt(jnp.mean(x2 * x2, axis=-1, keepdims=True) + EPS) * g2_ref[...]).astype(BF16)
        lane = lax.broadcasted_iota(jnp.int32, cw.shape, 1)
        first = ROUTER_OFF + EXPERTS_PER_GROUP * grp_ref[i]
        for k in range(EXPERTS_PER_GROUP):
            for r in range(k * per, (k + 1) * per):
                gather_row(nxt, r, 1 - slot).start()
                if with_scatter:
                    scatter_row(i - 1, r, 1 - slot).start(priority=1)
            gate = _dot(h, wg_ref[k])
            up = _dot(h, wu_ref[k])
            ck = jnp.sum(jnp.where(lane == first + k, cw, 0.0), axis=1, keepdims=True)
            abuf[:, k * D_EXPERT:(k + 1) * D_EXPERT] = (gate * _sigmoid(gate) * up * ck).astype(BF16)
        obuf[slot] = x2 + _dot(abuf[...], wd_ref[...])

    @pl.when(prev_full)
    def _():
        experts(True)

    @pl.when(jnp.logical_not(prev_full))
    def _():
        experts(False)

    nv = nval_ref[i]

    @pl.when(nv < tmx)
    def _():
        def body(r, c):
            scatter_row(i, r, slot).start()
            return c

        lax.fori_loop(0, nv, body, 0)

    @pl.when(i == n_tiles - 1)
    def _():
        @pl.when(nv == tmx)
        def _():
            def body(r, c):
                scatter_row(i, r, slot).start()
                return c

            lax.fori_loop(0, tmx, body, 0)

        gather_wait(1 - slot)
        scatter_wait(nval_ref[i - 1], 1 - slot)
        scatter_wait(nv, slot)


def _moe(x2e, grp, nval, pos, g2, wg, wu, wd):
    n_tok = x2e.shape[0]
    tmx = MOE_TM
    n_tiles = grp.shape[0]
    kern = functools.partial(_moe_kernel, tmx=tmx, n_tiles=n_tiles, n_tok=n_tok)
    gk = EXPERTS_PER_GROUP * D_EXPERT
    grid_spec = pltpu.PrefetchScalarGridSpec(
        num_scalar_prefetch=3,
        grid=(n_tiles,),
        in_specs=[
            pl.BlockSpec(memory_space=pl.ANY),
            pl.BlockSpec((1, D_MODEL), lambda i, g, n, t: (0, 0)),
            pl.BlockSpec((None, EXPERTS_PER_GROUP, D_MODEL, D_EXPERT), lambda i, g, n, t: (g[i], 0, 0, 0)),
            pl.BlockSpec((None, EXPERTS_PER_GROUP, D_MODEL, D_EXPERT), lambda i, g, n, t: (g[i], 0, 0, 0)),
            pl.BlockSpec((None, gk, D_MODEL), lambda i, g, n, t: (g[i], 0, 0)),
        ],
        out_specs=pl.BlockSpec(memory_space=pl.ANY),
        scratch_shapes=[
            pltpu.VMEM((2, tmx, D_MODEL + LANES), F32),
            pltpu.VMEM((2, tmx, D_MODEL), F32),
            pltpu.VMEM((tmx, gk), BF16),
            pltpu.SMEM((n_tiles * tmx,), jnp.int32),
            pltpu.SemaphoreType.DMA((2,)),
            pltpu.SemaphoreType.DMA((2,)),
        ],
    )
    return pl.pallas_call(
        kern,
        grid_spec=grid_spec,
        out_shape=jax.ShapeDtypeStruct((n_tok, D_MODEL), F32),
        compiler_params=_params(("arbitrary",)),
        name="moe",
    )(grp, nval, pos, x2e, g2, wg, wu, wd)


def _block_diag_ones(n, blk):
    i = jnp.arange(n) // blk
    return (i[:, None] == i[None, :]).astype(BF16)


def _layer(x, norm1_g, w_in, lam_re, lam_im, log_step, b_re, b_im, c_re, c_im, d_skip,
           w_glu, b_glu, g_q, g_kc, g_ks, g_kw, pos_k, pos_v, w_ck1, w_ck2, w_cv1, w_cv2,
           out_g_ssm, out_g_att, w_out, norm2_g, w_grp, b_grp, w_exp, b_exp, w_gate, w_up, w_down):
    bsz, seq, _ = x.shape
    assert seq % PROJ_TM == 0 and seq // SEL_BLOCK <= MAX_SEL_BLOCKS
    n_tok = bsz * seq
    x2d = x.reshape(n_tok, D_MODEL)
    q8 = SSM_Q
    n_sub = seq // q8
    nch = seq // CMP_STRIDE

    o_q = D_SSM
    o_kv = D_SSM + D_ATT
    o_gt = o_kv + 6 * D_KV
    kv = lambda i: w_in[:, o_kv + i * D_KV:o_kv + (i + 1) * D_KV]
    wrow = jnp.concatenate([w_in[:, :o_q], kv(0), kv(1), kv(2), kv(4)], axis=1).astype(BF16)
    per_g = HPG * N_BRANCH
    wgt = jnp.zeros((D_MODEL, N_KV * GATE_ROWS), F32)
    for g in range(N_KV):
        wgt = wgt.at[:, g * GATE_ROWS:g * GATE_ROWS + per_g].set(w_in[:, o_gt + g * per_g:o_gt + (g + 1) * per_g])
    wcol = jnp.concatenate([w_in[:, o_q:o_kv], kv(3), kv(5), wgt], axis=1).T.astype(BF16)
    qscale = (HEAD_DIM ** -0.5) * math.log2(math.e)
    gq = (jnp.tile(g_q.astype(F32), N_HEADS) * qscale).reshape(D_ATT, 1)
    gks = jnp.tile(g_ks.astype(F32), N_KV).reshape(1, D_KV)
    gkw = jnp.tile(g_kw.astype(F32), N_KV).reshape(1, D_KV)

    u, qt, kc, vc, ksa, kw, vst, vwt, gate = _in_proj(
        x2d, norm1_g.reshape(1, D_MODEL), wrow, wcol, gq, gks, gkw,
        _block_diag_ones(D_KV, HEAD_DIM), bsz=bsz, seq=seq)

    w_loc, t_in, m_st, pw_re, pw_im, dvec = _s5_weights(
        lam_re, lam_im, log_step, b_re, b_im, c_re, c_im, d_skip, n_sub)
    ys = _s5(u.reshape(bsz, seq, D_SSM), w_loc, t_in, m_st, pw_re, pw_im, dvec,
             bsz=bsz, n_sub=n_sub).reshape(n_tok, D_SSM)

    wide = CMP_STRIDE * HEAD_DIM
    pad8 = lambda p: jnp.zeros((8, 2 * wide), F32).at[0].set(p.reshape(-1)).astype(BF16)
    kcmp, vcmpt = _compress(
        kc, vc,
        w_ck1.astype(BF16), w_ck2.astype(BF16), w_cv1.astype(BF16), w_cv2.T.astype(BF16),
        pad8(pos_k), pad8(pos_v), g_kc.astype(F32).reshape(1, HEAD_DIM), bsz=bsz, nch=nch)
    cstart = jnp.arange(nch) * CMP_STRIDE
    sstart = jnp.arange(MAX_SEL_BLOCKS) * SEL_BLOCK
    ovlt = ((cstart[None, :] < sstart[:, None] + SEL_BLOCK) & (cstart[None, :] + CMP_BLOCK > sstart[:, None])
            & (jnp.arange(MAX_SEL_BLOCKS)[:, None] < seq // SEL_BLOCK)
            & (jnp.arange(nch)[None, :] < nch - 1)).astype(BF16)
    yat = _nsa(qt, kcmp, vcmpt, ksa, kw, vst, vwt, gate, ovlt, bsz=bsz, seq=seq, nch=nch)

    wr = jnp.zeros((D_MODEL, LANES), F32)
    wr = wr.at[:, :N_EXP_GROUPS].set(w_grp).at[:, ROUTER_OFF:ROUTER_OFF + N_EXPERTS].set(w_exp).astype(BF16)
    br = jnp.zeros((1, LANES), F32)
    br = br.at[0, :N_EXP_GROUPS].set(b_grp).at[0, ROUTER_OFF:ROUTER_OFF + N_EXPERTS].set(b_exp)
    g2 = norm2_g.reshape(1, D_MODEL).astype(F32)
    x2e = _out_proj(
        ys, yat, x2d, w_glu.astype(BF16), b_glu.reshape(1, D_SSM).astype(F32),
        out_g_ssm.reshape(1, D_SSM).astype(F32), out_g_att.reshape(D_ATT, 1).astype(F32),
        w_out.astype(BF16), g2, wr, br, seq=seq)

    grp, nval, pos = _moe_plan(x2e[:, D_MODEL].astype(jnp.int32), n_tok)
    gshape = (N_EXP_GROUPS, EXPERTS_PER_GROUP, D_MODEL, D_EXPERT)
    out = _moe(x2e, grp, nval, pos, g2, w_gate.astype(BF16).reshape(gshape), w_up.astype(BF16).reshape(gshape),
               w_down.astype(BF16).reshape(N_EXP_GROUPS, EXPERTS_PER_GROUP * D_EXPERT, D_MODEL))
    return out.reshape(bsz, seq, D_MODEL)


def kernel(x, norm1_g, w_in, lam_re, lam_im, log_step, b_re, b_im, c_re, c_im, d_skip, w_glu, b_glu, g_q, g_kc, g_ks, g_kw, pos_k, pos_v, w_ck1, w_ck2, w_cv1, w_cv2, out_g_ssm, out_g_att, w_out, norm2_g, w_grp, b_grp, w_exp, b_exp, w_gate, w_up, w_down):
    depth = norm1_g.shape[0]
    for l in range(depth):
        x = _layer(x, norm1_g[l], w_in[l], lam_re[l], lam_im[l], log_step[l], b_re[l], b_im[l], c_re[l],
                   c_im[l], d_skip[l], w_glu[l], b_glu[l], g_q[l], g_kc[l], g_ks[l], g_kw[l], pos_k[l],
                   pos_v[l], w_ck1[l], w_ck2[l], w_cv1[l], w_cv2[l], out_g_ssm[l], out_g_att[l], w_out[l],
                   norm2_g[l], w_grp[l], b_grp[l], w_exp[l], b_exp[l], w_gate[l], w_up[l], w_down[l])
    return x
```

```python
import functools
import math

import jax
import jax.numpy as jnp
from jax import lax
from jax.experimental import pallas as pl
from jax.experimental.pallas import tpu as pltpu

D_MODEL = 1024
D_SSM = 512
SSM_CH = 16
SSM_GROUPS = D_SSM // SSM_CH
SSM_STATE = 64
D_ATT = 512
HEAD_DIM = 64
N_HEADS = D_ATT // HEAD_DIM
N_KV = 2
HPG = N_HEADS // N_KV
D_KV = N_KV * HEAD_DIM
N_BRANCH = 3
CMP_STRIDE = 16
CMP_BLOCK = 2 * CMP_STRIDE
CMP_HIDDEN = 256
SEL_BLOCK = 64
N_SELECT = 16
WINDOW = 512
N_EXP_GROUPS = 4
EXPERTS_PER_GROUP = 8
N_EXPERTS = N_EXP_GROUPS * EXPERTS_PER_GROUP
D_EXPERT = 256
EPS = 1e-6
NEG = -1e30
FORCE = 1e9

LANES = 128
SSM_Q = 8
SSM_LT = D_SSM // LANES
ROUTER_OFF = N_EXP_GROUPS
NSA_TQ = 256
NSA_TK = 256
NSA_UNROLL = 8
V_ROWS = HEAD_DIM + 16
MAX_SEL_BLOCKS = 64
MOE_TM = 256
PROJ_TM = 1024
GATE_ROWS = 16
V7X_VMEM_BYTES = 64 * 1024 * 1024
VMEM_LIMIT = V7X_VMEM_BYTES - 8 * 1024 * 1024

F32 = jnp.float32
BF16 = jnp.bfloat16


def _dot(a, b):
    return jnp.dot(a, b, preferred_element_type=F32)


def _dot_nt(a, b):
    return lax.dot_general(a, b, (((1,), (1,)), ((), ())), preferred_element_type=F32)


def _split_dot(x, w):
    hi = x.astype(BF16)
    lo = (x - hi.astype(F32)).astype(BF16)
    return _dot(hi, w) + _dot(lo, w)


def _gelu(x):
    c = math.sqrt(2.0 / math.pi)
    return 0.5 * x * (1.0 + jnp.tanh(c * (x + 0.044715 * (x * x * x))))


def _sigmoid(x):
    return 1.0 / (1.0 + jnp.exp(-x))


def _params(sem, vmem_limit=VMEM_LIMIT):
    return pltpu.CompilerParams(dimension_semantics=sem, vmem_limit_bytes=vmem_limit)


def _in_proj_kernel(x_ref, g1_ref, wrow_ref, wcol_ref, gq_ref, gks_ref, gkw_ref, bd128_ref,
                    u_ref, qt_ref, kc_ref, vc_ref, ksa_ref, kw_ref, vst_ref, vwt_ref, gate_ref, *, tm, nl):
    x = x_ref[...]
    ms = jnp.mean(x * x, axis=-1, keepdims=True)
    hn = (x * lax.rsqrt(ms + EPS) * g1_ref[...]).astype(BF16)

    pr = _dot(hn, wrow_ref[...])
    u_ref[...] = pr[:, :D_SSM]
    kc, vc, ks, kw = [pr[:, D_SSM + i * D_KV:D_SSM + (i + 1) * D_KV] for i in range(4)]
    kss = _split_dot(ks * ks, bd128_ref[...])
    ksn = ks * lax.rsqrt(kss * (1.0 / HEAD_DIM) + EPS) * gks_ref[...]
    kws = _split_dot(kw * kw, bd128_ref[...])
    kwn = kw * lax.rsqrt(kws * (1.0 / HEAD_DIM) + EPS) * gkw_ref[...]
    t0 = (pl.program_id(0) % nl) * tm
    tpos = t0 + lax.broadcasted_iota(jnp.int32, (tm, MAX_SEL_BLOCKS), 0)
    blk = lax.broadcasted_iota(jnp.int32, (tm, MAX_SEL_BLOCKS), 1)
    onehot = jnp.where(tpos // SEL_BLOCK == blk, 1.0, 0.0).astype(BF16)
    for g in range(N_KV):
        sl = slice(g * HEAD_DIM, (g + 1) * HEAD_DIM)
        kc_ref[g] = kc[:, sl]
        vc_ref[g] = vc[:, sl]
        ksa_ref[g] = jnp.concatenate([ksn[:, sl].astype(BF16), onehot], axis=1)
        kw_ref[g] = kwn[:, sl].astype(BF16)

    pc = _dot_nt(wcol_ref[...], hn)
    gq = gq_ref[...]
    for h in range(N_HEADS):
        sl = slice(h * HEAD_DIM, (h + 1) * HEAD_DIM)
        qh = pc[sl]
        ss = jnp.sum(qh * qh, axis=0, keepdims=True)
        qt_ref[h] = (qh * lax.rsqrt(ss * (1.0 / HEAD_DIM) + EPS) * gq[sl]).astype(BF16)
    ones_rows = jnp.where(lax.broadcasted_iota(jnp.int32, (V_ROWS - HEAD_DIM, tm), 0) == 0, 1.0, 0.0)
    for g in range(N_KV):
        for o_ref, base in ((vst_ref, D_ATT), (vwt_ref, D_ATT + D_KV)):
            vt = jnp.concatenate([pc[base + g * HEAD_DIM:base + (g + 1) * HEAD_DIM], ones_rows], axis=0)
            vt = vt.astype(BF16)
            for j in range(tm // NSA_TK):
                o_ref[g, j] = vt[:, j * NSA_TK:(j + 1) * NSA_TK]
        gb = D_ATT + 2 * D_KV + g * GATE_ROWS
        gate_ref[g] = _sigmoid(pc[gb:gb + GATE_ROWS])


def _in_proj(x2d, g1, wrow, wcol, gq, gks, gkw, bd128, *, bsz, seq):
    tm = PROJ_TM
    nl = seq // tm
    n_tok = bsz * seq
    kern = functools.partial(_in_proj_kernel, tm=tm, nl=nl)
    row = lambda i: (i, 0)
    const = lambda i: (0, 0)
    bgl = lambda i: (i // nl, 0, i % nl, 0)
    n_col = wcol.shape[0]
    jt = tm // NSA_TK

    def kvspec(width):
        return pl.BlockSpec((None, N_KV, tm, width), bgl)

    def kvshape(width, dtype=BF16):
        return jax.ShapeDtypeStruct((bsz, N_KV, seq, width), dtype)

    vt_spec = pl.BlockSpec((None, N_KV, jt, V_ROWS, NSA_TK), lambda i: (i // nl, 0, i % nl, 0, 0))
    vt_shape = jax.ShapeDtypeStruct((bsz, N_KV, seq // NSA_TK, V_ROWS, NSA_TK), BF16)
    return pl.pallas_call(
        kern,
        grid=(n_tok // tm,),
        in_specs=[
            pl.BlockSpec((tm, D_MODEL), row),
            pl.BlockSpec((1, D_MODEL), const),
            pl.BlockSpec((D_MODEL, D_SSM + 4 * D_KV), const),
            pl.BlockSpec((n_col, D_MODEL), const),
            pl.BlockSpec((D_ATT, 1), const),
            pl.BlockSpec((1, D_KV), const),
            pl.BlockSpec((1, D_KV), const),
            pl.BlockSpec((D_KV, D_KV), const),
        ],
        out_specs=[
            pl.BlockSpec((tm, D_SSM), row),
            pl.BlockSpec((None, N_HEADS, HEAD_DIM, tm), lambda i: (i // nl, 0, 0, i % nl)),
            kvspec(HEAD_DIM), kvspec(HEAD_DIM), kvspec(2 * HEAD_DIM), kvspec(HEAD_DIM),
            vt_spec, vt_spec,
            pl.BlockSpec((None, N_KV, GATE_ROWS, tm), lambda i: (i // nl, 0, 0, i % nl)),
        ],
        out_shape=[
            jax.ShapeDtypeStruct((n_tok, D_SSM), F32),
            jax.ShapeDtypeStruct((bsz, N_HEADS, HEAD_DIM, seq), BF16),
            kvshape(HEAD_DIM, F32), kvshape(HEAD_DIM, F32), kvshape(2 * HEAD_DIM), kvshape(HEAD_DIM),
            vt_shape, vt_shape,
            jax.ShapeDtypeStruct((bsz, N_KV, GATE_ROWS, seq), F32),
        ],
        compiler_params=_params(("parallel",)),
        name="in_proj",
    )(x2d, g1, wrow, wcol, gq, gks, gkw, bd128)


def _s5_weights(lam_re, lam_im, log_step, b_re, b_im, c_re, c_im, d_skip, n_sub):
    q = SSM_Q
    lam = lax.complex(lam_re.astype(F32), lam_im.astype(F32))
    step = jnp.exp(log_step.astype(F32))[:, None]
    lam_bar = jnp.exp(lam * step)
    b_bar = ((lam_bar - 1.0) / lam)[..., None] * lax.complex(b_re.astype(F32), b_im.astype(F32))
    c = lax.complex(c_re.astype(F32), c_im.astype(F32))
    pows = [jnp.ones_like(lam_bar)]
    for _ in range(q):
        pows.append(pows[-1] * lam_bar)
    pw = jnp.stack(pows)
    lt, a8 = SSM_LT, LANES // SSM_CH
    hp = a8 * SSM_STATE
    e_lane = (jnp.arange(a8)[:, None] == jnp.arange(LANES)[None, :] // SSM_CH).astype(F32)
    e_state = (jnp.arange(a8)[:, None] == jnp.arange(hp)[None, :] // SSM_STATE).astype(F32)
    e_lane_t = jnp.tile(e_lane, (1, q))
    e_state_t = jnp.tile(e_state, (1, 2))

    kk = jnp.real(jnp.einsum('ghp,kgp,gpi->kghi', c, pw[:q], b_bar))
    km = kk.reshape(q, lt, a8, SSM_CH, SSM_CH).transpose(1, 4, 0, 2, 3).reshape(lt, SSM_CH, q, LANES)
    lag = jnp.arange(q)[None, :] - jnp.arange(q)[:, None]
    kg = km[:, :, jnp.clip(lag, 0, q - 1), :] * (lag >= 0)[None, None, :, :, None].astype(F32)
    kc = kg.transpose(0, 2, 1, 3, 4).reshape(lt, q, 1, SSM_CH, q * LANES)
    t_in = (kc * e_lane_t[None, None, :, None, :]).reshape(lt, q * LANES, q * LANES)

    wc = pw[q - 1 - jnp.arange(q)][..., None] * b_bar[None]
    wri = jnp.stack([jnp.real(wc), jnp.imag(wc)])
    wm = (wri.reshape(2, q, lt, a8, SSM_STATE, SSM_CH).transpose(2, 1, 5, 0, 3, 4)
          .reshape(lt, q, 1, SSM_CH, 2 * hp))
    w_loc = (wm * e_state_t[None, None, :, None, :]).reshape(lt, q * LANES, 2 * hp)

    cl = c[None] * pw[1:q + 1][:, :, None, :]
    cri = jnp.stack([jnp.real(cl), -jnp.imag(cl)])
    mm = (cri.reshape(2, q, lt, a8, SSM_CH, SSM_STATE).transpose(2, 0, 5, 1, 3, 4)
          .reshape(lt, 2, 1, SSM_STATE, q * LANES))
    m_st = (mm * e_lane_t[None, None, :, None, :]).reshape(lt, 2 * hp, q * LANES)

    n_lvl = max(1, (n_sub - 1).bit_length())
    lv = [pw[q]]
    for _ in range(n_lvl - 1):
        lv.append(lv[-1] * lv[-1])
    lvs = jnp.stack(lv).reshape(n_lvl, lt, 1, hp)
    pw_re = jnp.real(lvs).transpose(1, 0, 2, 3)
    pw_im = jnp.imag(lvs).transpose(1, 0, 2, 3)
    dvec = jnp.tile(d_skip.astype(F32).reshape(lt, 1, LANES), (1, 1, q))
    return w_loc.astype(BF16), t_in.astype(BF16), m_st.astype(BF16), pw_re, pw_im, dvec


def _s5_kernel(u_ref, w_ref, t_ref, m_ref, pwr_ref, pwi_ref, d_ref, y_ref, *, n_sub, n_lvl):
    half = (LANES // SSM_CH) * SSM_STATE
    q = SSM_Q
    u = jnp.concatenate([u_ref[pl.ds(s, n_sub, stride=q), :] for s in range(q)], axis=1)
    ub = u.astype(BF16)
    s_loc = _dot(ub, w_ref[...])
    re = s_loc[:, :half]
    im = s_loc[:, half:]
    rowi = lax.broadcasted_iota(jnp.int32, (n_sub, half), 0)
    for k in range(n_lvl):
        d = 1 << k
        ar = pwr_ref[k]
        ai = pwi_ref[k]
        keep = rowi >= d
        sre = jnp.where(keep, pltpu.roll(re, d, axis=0), 0.0)
        sim = jnp.where(keep, pltpu.roll(im, d, axis=0), 0.0)
        re, im = re + (ar * sre - ai * sim), im + (ar * sim + ai * sre)
    keep = rowi >= 1
    xre = jnp.where(keep, pltpu.roll(re, 1, axis=0), 0.0)
    xim = jnp.where(keep, pltpu.roll(im, 1, axis=0), 0.0)
    xst = jnp.concatenate([xre, xim], axis=1).astype(BF16)
    y = _dot(ub, t_ref[...]) + _dot(xst, m_ref[...]) + d_ref[...] * u
    for j in range(q):
        y_ref[pl.ds(j, n_sub, stride=q), :] = y[:, j * LANES:(j + 1) * LANES]


def _s5(u, w_loc, t_in, m_st, pw_re, pw_im, dvec, *, bsz, n_sub):
    q = SSM_Q
    n_lvl = pw_re.shape[1]
    kern = functools.partial(_s5_kernel, n_sub=n_sub, n_lvl=n_lvl)
    wide = q * LANES
    seq = n_sub * q
    return pl.pallas_call(
        kern,
        grid=(bsz, SSM_LT),
        in_specs=[
            pl.BlockSpec((None, seq, LANES), lambda b, l: (b, 0, l)),
            pl.BlockSpec((None, wide, wide), lambda b, l: (l, 0, 0)),
            pl.BlockSpec((None, wide, wide), lambda b, l: (l, 0, 0)),
            pl.BlockSpec((None, wide, wide), lambda b, l: (l, 0, 0)),
            pl.BlockSpec((None, n_lvl, 1, wide // 2), lambda b, l: (l, 0, 0, 0)),
            pl.BlockSpec((None, n_lvl, 1, wide // 2), lambda b, l: (l, 0, 0, 0)),
            pl.BlockSpec((None, 1, wide), lambda b, l: (l, 0, 0)),
        ],
        out_specs=pl.BlockSpec((None, seq, LANES), lambda b, l: (b, 0, l)),
        out_shape=jax.ShapeDtypeStruct((bsz, seq, D_SSM), F32),
        compiler_params=_params(("parallel", "parallel")),
        name="s5",
    )(u, w_loc, t_in, m_st, pw_re, pw_im, dvec)


def _compress_kernel(kc_ref, vc_ref, w1k_ref, w2k_ref, w1v_ref, w2vt_ref, posk_ref, posv_ref, gkc_ref,
                     kcmp_ref, vcmpt_ref, *, nch):
    half = CMP_STRIDE * HEAD_DIM

    def hidden(x_ref, w1_ref, pos_ref):
        x = jnp.concatenate([x_ref[pl.ds(j, nch, stride=CMP_STRIDE), :] for j in range(CMP_STRIDE)],
                            axis=1).astype(BF16)
        a = _dot(x, w1_ref[:half, :])
        b = _dot(x, w1_ref[half:, :])
        pv = _dot(pos_ref[...], w1_ref[...])[0:1, :]
        hid = a + pltpu.roll(b, nch - 1, axis=0) + pv
        return _gelu(hid).astype(BF16)

    k = _dot(hidden(kc_ref, w1k_ref, posk_ref), w2k_ref[...])
    ms = jnp.mean(k * k, axis=-1, keepdims=True)
    kcmp_ref[...] = (k * lax.rsqrt(ms + EPS) * gkc_ref[...]).astype(BF16)
    vt = _dot_nt(w2vt_ref[...], hidden(vc_ref, w1v_ref, posv_ref))
    coli = lax.broadcasted_iota(jnp.int32, vt.shape, 1)
    vcmpt_ref[...] = jnp.where(coli < nch - 1, vt, 0.0).astype(BF16)


def _compress(kcf, vcf, w1k, w2k, w1v, w2vt, posk, posv, gkc, *, bsz, nch):
    kern = functools.partial(_compress_kernel, nch=nch)
    wide = CMP_STRIDE * HEAD_DIM
    xspec = pl.BlockSpec((None, None, nch * CMP_STRIDE, HEAD_DIM), lambda b, g: (b, g, 0, 0))
    c2 = lambda b, g: (0, 0)
    return pl.pallas_call(
        kern,
        grid=(bsz, N_KV),
        in_specs=[
            xspec, xspec,
            pl.BlockSpec((2 * wide, CMP_HIDDEN), c2), pl.BlockSpec((CMP_HIDDEN, HEAD_DIM), c2),
            pl.BlockSpec((2 * wide, CMP_HIDDEN), c2), pl.BlockSpec((HEAD_DIM, CMP_HIDDEN), c2),
            pl.BlockSpec((8, 2 * wide), c2), pl.BlockSpec((8, 2 * wide), c2),
            pl.BlockSpec((1, HEAD_DIM), c2),
        ],
        out_specs=[pl.BlockSpec((None, None, nch, HEAD_DIM), lambda b, g: (b, g, 0, 0)),
                   pl.BlockSpec((None, None, HEAD_DIM, nch), lambda b, g: (b, g, 0, 0))],
        out_shape=[jax.ShapeDtypeStruct((bsz, N_KV, nch, HEAD_DIM), BF16),
                   jax.ShapeDtypeStruct((bsz, N_KV, HEAD_DIM, nch), BF16)],
        compiler_params=_params(("parallel", "parallel")),
        name="compress",
    )(kcf, vcf, w1k, w2k, w1v, w2vt, posk, posv, gkc)


def _nsa_kernel(qt_ref, kcmp_ref, vcmpt_ref, ksa_ref, kw_ref, vst_ref, vwt_ref, gate_ref, ovlt_ref, o_ref,
                accs_ref, accw_ref, sel_ref, *, tq, tk, nch):
    qi = pl.program_id(2)
    q0 = qi * tq
    rows = HPG * tq
    qt = jnp.concatenate([qt_ref[h] for h in range(HPG)], axis=1)
    tpos = q0 + lax.broadcasted_iota(jnp.int32, (1, rows), 1) % tq
    krow = lax.broadcasted_iota(jnp.int32, (tk, rows), 0)
    m0 = jnp.full((1, rows), NEG, F32)

    def run_tiles(k_ref, q_all, vt_ref, acc_ref, m, tiles):
        scores = []
        for kt, kind in tiles:
            kc = jnp.maximum(kt, 0) if kind in ('band', 'valid') else kt
            s = _dot(k_ref[pl.ds(pl.multiple_of(kc * tk, tk), tk), :], q_all)
            if kind == 'causal':
                s = jnp.where(kt * tk + krow <= tpos, s, NEG)
            elif kind == 'band':
                s = jnp.where((kt * tk + krow > tpos - WINDOW) & (kt >= 0), s, NEG)
            elif kind == 'valid':
                s = jnp.where(kt >= 0, s, NEG)
            scores.append((kc, s))
        for kc, s in scores:
            m_new = jnp.maximum(m, jnp.max(s, axis=0, keepdims=True))
            alpha = jnp.exp2(m - m_new)
            p = jnp.exp2(s - m_new).astype(BF16)
            acc_ref[...] = alpha * acc_ref[...] + _dot(vt_ref[kc], p)
            m = m_new
        return m

    accw_ref[...] = jnp.zeros_like(accw_ref)
    n_win = WINDOW // tk
    run_tiles(kw_ref, qt, vwt_ref, accw_ref, m0,
              [(qi - n_win, 'band')] + [(qi - n_win + t, 'valid') for t in range(1, n_win)] + [(qi, 'causal')])

    s = _dot(kcmp_ref[...], qt)
    cend = lax.broadcasted_iota(jnp.int32, (nch, rows), 0) * CMP_STRIDE + (CMP_BLOCK - 1)
    s = jnp.where(cend <= tpos, s, NEG)
    m = jnp.max(s, axis=0, keepdims=True)
    p = jnp.exp2(s - m)
    p = p * jnp.where(tpos >= CMP_BLOCK - 1, 1.0 / jnp.sum(p, axis=0, keepdims=True), 0.0)
    o_cmp = _dot(vcmpt_ref[...], p.astype(BF16))

    psum = p[:, 0:tq]
    for h in range(1, HPG):
        psum = psum + p[:, h * tq:(h + 1) * tq]
    hi = psum.astype(BF16)
    lo = (psum - hi.astype(F32)).astype(BF16)
    ovlt = ovlt_ref[...]
    imp = _dot(ovlt, hi) + _dot(ovlt, lo)
    nb = MAX_SEL_BLOCKS
    blk = lax.broadcasted_iota(jnp.int32, (nb, tq), 0)
    cur = (q0 + lax.broadcasted_iota(jnp.int32, (nb, tq), 1)) // SEL_BLOCK
    forced = (blk == 0) | (blk == cur) | (blk == cur - 1)
    imp = jnp.where(forced, FORCE, jnp.where(blk <= cur, imp, NEG))
    sub = 8
    rowl = lax.broadcasted_iota(jnp.int32, (sub, tq), 0)
    n_seen = (q0 + tq - 1) // SEL_BLOCK + 1
    sel_ref[...] = jnp.zeros_like(sel_ref)
    for nbv in range(2 * N_SELECT, nb + 1, N_SELECT):
        @pl.when((n_seen > nbv - N_SELECT) & (n_seen <= nbv))
        def _(nbv=nbv):
            groups = [imp[r:r + sub] for r in range(0, nbv, sub)]
            ranks = [jnp.zeros((sub, tq), F32) for _ in groups]
            for i in range(nbv):
                ri = jnp.broadcast_to(imp[i:i + 1, :], (sub, tq))
                for gi, x in enumerate(groups):
                    if i < gi * sub:
                        ahead = ri >= x
                    elif i >= (gi + 1) * sub:
                        ahead = ri > x
                    else:
                        ahead = (ri > x) | ((ri == x) & (rowl > i - gi * sub))
                    ranks[gi] = ranks[gi] + jnp.where(ahead, 1.0, 0.0)
            rank = jnp.concatenate(ranks, axis=0)
            sel_ref[0:nbv, :] = jnp.where(rank < N_SELECT, 0.0, NEG).astype(BF16)
    sel = sel_ref[...]
    qa = jnp.concatenate([qt, jnp.concatenate([sel] * HPG, axis=1)], axis=0)

    accs_ref[...] = jnp.zeros_like(accs_ref)
    nu = NSA_UNROLL

    def sel_multi(j, m):
        return run_tiles(ksa_ref, qa, vst_ref, accs_ref, m, [(nu * j + t, None) for t in range(nu)])

    m_sel = lax.fori_loop(0, qi // nu, sel_multi, m0)
    for r in range(nu):
        @pl.when(qi % nu == r)
        def _(r=r):
            base = qi - r
            run_tiles(ksa_ref, qa, vst_ref, accs_ref, m_sel,
                      [(base + t, None) for t in range(r)] + [(qi, 'causal')])

    accs = accs_ref[...]
    accw = accw_ref[...]
    o_sel = accs[:HEAD_DIM] * (1.0 / accs[HEAD_DIM:HEAD_DIM + 1])
    o_win = accw[:HEAD_DIM] * (1.0 / accw[HEAD_DIM:HEAD_DIM + 1])
    gt = gate_ref[...]
    for h in range(HPG):
        c = slice(h * tq, (h + 1) * tq)
        o_ref[h * HEAD_DIM:(h + 1) * HEAD_DIM, :] = (
            gt[3 * h:3 * h + 1] * o_cmp[:, c] + gt[3 * h + 1:3 * h + 2] * o_sel[:, c]
            + gt[3 * h + 2:3 * h + 3] * o_win[:, c])


def _nsa(qt, kcmp, vcmpt, ksa, kw, vst, vwt, gate, ovlt, *, bsz, seq, nch):
    tq, tk = NSA_TQ, NSA_TK
    assert tq == tk and WINDOW % tk == 0
    nq = seq // tq
    kern = functools.partial(_nsa_kernel, tq=tq, tk=tk, nch=nch)
    full = lambda b, g, i: (b, g, 0, 0)
    full5 = lambda b, g, i: (b, g, 0, 0, 0)
    qd = HPG * HEAD_DIM
    rows = HPG * tq
    return pl.pallas_call(
        kern,
        grid=(bsz, N_KV, nq),
        in_specs=[
            pl.BlockSpec((None, HPG, HEAD_DIM, tq), lambda b, g, i: (b, g, 0, i)),
            pl.BlockSpec((None, None, nch, HEAD_DIM), full),
            pl.BlockSpec((None, None, HEAD_DIM, nch), full),
            pl.BlockSpec((None, None, seq, 2 * HEAD_DIM), full),
            pl.BlockSpec((None, None, seq, HEAD_DIM), full),
            pl.BlockSpec((None, None, seq // tk, V_ROWS, tk), full5),
            pl.BlockSpec((None, None, seq // tk, V_ROWS, tk), full5),
            pl.BlockSpec((None, None, GATE_ROWS, tq), lambda b, g, i: (b, g, 0, i)),
            pl.BlockSpec((MAX_SEL_BLOCKS, nch), lambda b, g, i: (0, 0)),
        ],
        out_specs=pl.BlockSpec((None, qd, tq), lambda b, g, i: (b, g, i)),
        out_shape=jax.ShapeDtypeStruct((bsz, D_ATT, seq), F32),
        scratch_shapes=[pltpu.VMEM((V_ROWS, rows), F32), pltpu.VMEM((V_ROWS, rows), F32),
                        pltpu.VMEM((MAX_SEL_BLOCKS, tq), BF16)],
        compiler_params=_params(("parallel", "parallel", "arbitrary")),
        name="nsa",
    )(qt, kcmp, vcmpt, ksa, kw, vst, vwt, gate, ovlt)


def _out_proj_kernel(ys_ref, yat_ref, x_ref, wglu_ref, bglu_ref, gs_ref, ga_ref, wo_ref, g2_ref, wr_ref, br_ref,
                     x2e_ref):
    y = _gelu(ys_ref[...])
    y = y * _sigmoid(_dot(y.astype(BF16), wglu_ref[...]) + bglu_ref[...])
    ysn = y * lax.rsqrt(jnp.mean(y * y, axis=-1, keepdims=True) + EPS) * gs_ref[...]
    yat = yat_ref[...]
    yant = yat * lax.rsqrt(jnp.mean(yat * yat, axis=0, keepdims=True) + EPS) * ga_ref[...]
    yan = yant.T
    x2 = x_ref[...] + _dot(ysn.astype(BF16), wo_ref[:D_SSM, :]) + _dot(yan.astype(BF16), wo_ref[D_SSM:, :])
    x2e_ref[:, :D_MODEL] = x2
    h2 = (x2 * lax.rsqrt(jnp.mean(x2 * x2, axis=-1, keepdims=True) + EPS) * g2_ref[...]).astype(BF16)

    logits = _dot(h2, wr_ref[...]) + br_ref[...]
    lane = lax.broadcasted_iota(jnp.int32, logits.shape, 1).astype(F32)
    far = float(LANES)
    is_g = lane < N_EXP_GROUPS
    glog = jnp.where(is_g, logits, -jnp.inf)
    gmax = jnp.max(glog, axis=1, keepdims=True)
    gsum = jnp.sum(jnp.where(is_g, jnp.exp(logits - gmax), 0.0), axis=1, keepdims=True)
    gsel = jnp.min(jnp.where(glog == gmax, lane, far), axis=1, keepdims=True)
    gprob = 1.0 / gsum
    lo = ROUTER_OFF + EXPERTS_PER_GROUP * gsel
    in_e = (lane >= lo) & (lane < lo + EXPERTS_PER_GROUP)
    emax = jnp.max(jnp.where(in_e, logits, -jnp.inf), axis=1, keepdims=True)
    eexp = jnp.where(in_e, jnp.exp(logits - emax), 0.0)
    eprob = jnp.where(in_e, eexp / jnp.sum(eexp, axis=1, keepdims=True), -1.0)
    v1 = jnp.max(eprob, axis=1, keepdims=True)
    i1 = jnp.min(jnp.where(eprob == v1, lane, far), axis=1, keepdims=True)
    rest = jnp.where(lane == i1, -1.0, eprob)
    v2 = jnp.max(rest, axis=1, keepdims=True)
    i2 = jnp.min(jnp.where(rest == v2, lane, far), axis=1, keepdims=True)
    den = v1 + v2
    x2e_ref[:, D_MODEL:] = (jnp.where(lane == i1, v1 / den * gprob, 0.0)
                            + jnp.where(lane == i2, v2 / den * gprob, 0.0)
                            + jnp.where(lane == 0.0, gsel, 0.0))


def _out_proj(ys, yat, x2d, wglu, bglu, gs, ga, wo, g2, wr, br, *, seq):
    n_tok = x2d.shape[0]
    tm = PROJ_TM
    nl = seq // tm
    row = lambda i: (i, 0)
    const = lambda i: (0, 0)
    return pl.pallas_call(
        _out_proj_kernel,
        grid=(n_tok // tm,),
        in_specs=[
            pl.BlockSpec((tm, D_SSM), row),
            pl.BlockSpec((None, D_ATT, tm), lambda i: (i // nl, 0, i % nl)),
            pl.BlockSpec((tm, D_MODEL), row),
            pl.BlockSpec((D_SSM, D_SSM), const),
            pl.BlockSpec((1, D_SSM), const),
            pl.BlockSpec((1, D_SSM), const),
            pl.BlockSpec((D_ATT, 1), const),
            pl.BlockSpec((D_SSM + D_ATT, D_MODEL), const),
            pl.BlockSpec((1, D_MODEL), const),
            pl.BlockSpec((D_MODEL, LANES), const),
            pl.BlockSpec((1, LANES), const),
        ],
        out_specs=pl.BlockSpec((tm, D_MODEL + LANES), row),
        out_shape=jax.ShapeDtypeStruct((n_tok, D_MODEL + LANES), F32),
        compiler_params=_params(("parallel",)),
        name="out_proj",
    )(ys, yat, x2d, wglu, bglu, gs, ga, wo, g2, wr, br)


def _moe_plan(gsel, n_tok):
    tmx = MOE_TM
    n_tiles = n_tok // tmx + N_EXP_GROUPS
    oh = (gsel[:, None] == jnp.arange(N_EXP_GROUPS)[None, :]).astype(jnp.int32)
    csum = jnp.cumsum(oh, axis=0)
    counts = csum[-1]
    rank = jnp.sum(csum * oh, axis=1) - 1
    nt = (counts + tmx - 1) // tmx
    tend = jnp.cumsum(nt)
    toff = tend - nt
    pos = jnp.sum(oh * toff[None, :], axis=1) * tmx + rank
    tile = jnp.arange(n_tiles)
    grp = jnp.minimum(jnp.sum((tile[:, None] >= tend[None, :]).astype(jnp.int32), axis=1), N_EXP_GROUPS - 1)
    nval = jnp.clip(counts[grp] - (tile - toff[grp]) * tmx, 0, tmx)
    return grp.astype(jnp.int32), nval.astype(jnp.int32), pos.astype(jnp.int32)


def _moe_kernel(grp_ref, nval_ref, pos_ref, x_hbm, g2_ref, wg_ref, wu_ref, wd_ref, o_hbm,
                xbuf, obuf, abuf, tok_ref, gsem, ssem, *, tmx, n_tiles, n_tok):
    i = pl.program_id(0)
    slot = i % 2

    @pl.when(i == 0)
    def _():
        def place(t, c):
            tok_ref[pos_ref[t]] = t
            return c

        lax.fori_loop(0, n_tok, place, 0, unroll=8)

        def pad_tile(tile, c):
            def pad_row(r, c2):
                tok_ref[tile * tmx + r] = 0
                return c2

            return lax.fori_loop(nval_ref[tile], tmx, pad_row, c)

        lax.fori_loop(0, n_tiles, pad_tile, 0)

    def gather_row(tile, r, dst_slot):
        t = tok_ref[tile * tmx + r]
        return pltpu.make_async_copy(x_hbm.at[pl.ds(t, 1), :], xbuf.at[dst_slot, pl.ds(r, 1), :], gsem.at[dst_slot])

    def scatter_row(tile, r, src_slot):
        t = tok_ref[tile * tmx + r]
        return pltpu.make_async_copy(obuf.at[src_slot, pl.ds(r, 1), :], o_hbm.at[pl.ds(t, 1), :], ssem.at[src_slot])

    def gather_wait(src_slot):
        pltpu.make_async_copy(x_hbm.at[pl.ds(0, tmx), :], xbuf.at[src_slot], gsem.at[src_slot]).wait()

    def scatter_wait(n, src_slot):
        n8 = pl.multiple_of((n // 8) * 8, 8)

        @pl.when(n8 > 0)
        def _():
            pltpu.make_async_copy(obuf.at[src_slot, pl.ds(0, n8), :], o_hbm.at[pl.ds(0, n8), :],
                                  ssem.at[src_slot]).wait()

        def one(r, c):
            pltpu.make_async_copy(obuf.at[src_slot, pl.ds(0, 1), :], o_hbm.at[pl.ds(0, 1), :],
                                  ssem.at[src_slot]).wait()
            return c

        lax.fori_loop(0, n - n8, one, 0)

    @pl.when(i == 0)
    def _():
        def body(r, c):
            gather_row(0, r, 0).start()
            return c

        lax.fori_loop(0, tmx, body, 0)

    gather_wait(slot)

    @pl.when(i >= 2)
    def _():
        scatter_wait(nval_ref[i - 2], slot)

    nxt = jnp.minimum(i + 1, n_tiles - 1)
    per = tmx // EXPERTS_PER_GROUP
    prev_full = (i >= 1) & (nval_ref[jnp.maximum(i - 1, 0)] == tmx)

    def experts(with_scatter):
        xe = xbuf[slot]
        x2 = xe[:, :D_MODEL]
        cw = xe[:, D_MODEL:]
        h = (x2 * lax.rsqrt(jnp.mean(x2 * x2, axis=-1, keepdims=True) + EPS) * g2_ref[...]).astype(BF16)
        lane = lax.broadcasted_iota(jnp.int32, cw.shape, 1)
        first = ROUTER_OFF + EXPERTS_PER_GROUP * grp_ref[i]
        for k in range(EXPERTS_PER_GROUP):
            for r in range(k * per, (k + 1) * per):
                gather_row(nxt, r, 1 - slot).start()
                if with_scatter:
                    scatter_row(i - 1, r, 1 - slot).start(priority=1)
            gate = _dot(h, wg_ref[k].astype(BF16))
            up = _dot(h, wu_ref[k].astype(BF16))
            ck = jnp.sum(jnp.where(lane == first + k, cw, 0.0), axis=1, keepdims=True)
            abuf[:, k * D_EXPERT:(k + 1) * D_EXPERT] = (gate * _sigmoid(gate) * up * ck).astype(BF16)
        obuf[slot] = x2 + _dot(abuf[...], wd_ref[...].astype(BF16))

    @pl.when(prev_full)
    def _():
        experts(True)

    @pl.when(jnp.logical_not(prev_full))
    def _():
        experts(False)

    nv = nval_ref[i]

    @pl.when(nv < tmx)
    def _():
        def body(r, c):
            scatter_row(i, r, slot).start()
            return c

        lax.fori_loop(0, nv, body, 0)

    @pl.when(i == n_tiles - 1)
    def _():
        @pl.when(nv == tmx)
        def _():
            def body(r, c):
                scatter_row(i, r, slot).start()
                return c

            lax.fori_loop(0, tmx, body, 0)

        gather_wait(1 - slot)
        scatter_wait(nval_ref[i - 1], 1 - slot)
        scatter_wait(nv, slot)


def _moe(x2e, grp, nval, pos, g2, wg, wu, wd):
    n_tok = x2e.shape[0]
    tmx = MOE_TM
    n_tiles = grp.shape[0]
    kern = functools.partial(_moe_kernel, tmx=tmx, n_tiles=n_tiles, n_tok=n_tok)
    gk = EXPERTS_PER_GROUP * D_EXPERT
    w_bytes = 2 * 3 * EXPERTS_PER_GROUP * D_MODEL * D_EXPERT * 4
    io_bytes = 2 * tmx * (2 * D_MODEL + LANES) * 4 + tmx * gk * 2
    vmem_limit = w_bytes + io_bytes + 6 * 1024 * 1024
    assert vmem_limit < V7X_VMEM_BYTES
    grid_spec = pltpu.PrefetchScalarGridSpec(
        num_scalar_prefetch=3,
        grid=(n_tiles,),
        in_specs=[
            pl.BlockSpec(memory_space=pl.ANY),
            pl.BlockSpec((1, D_MODEL), lambda i, g, n, t: (0, 0)),
            pl.BlockSpec((None, EXPERTS_PER_GROUP, D_MODEL, D_EXPERT), lambda i, g, n, t: (g[i], 0, 0, 0)),
            pl.BlockSpec((None, EXPERTS_PER_GROUP, D_MODEL, D_EXPERT), lambda i, g, n, t: (g[i], 0, 0, 0)),
            pl.BlockSpec((None, gk, D_MODEL), lambda i, g, n, t: (g[i], 0, 0)),
        ],
        out_specs=pl.BlockSpec(memory_space=pl.ANY),
        scratch_shapes=[
            pltpu.VMEM((2, tmx, D_MODEL + LANES), F32),
            pltpu.VMEM((2, tmx, D_MODEL), F32),
            pltpu.VMEM((tmx, gk), BF16),
            pltpu.SMEM((n_tiles * tmx,), jnp.int32),
            pltpu.SemaphoreType.DMA((2,)),
            pltpu.SemaphoreType.DMA((2,)),
        ],
    )
    return pl.pallas_call(
        kern,
        grid_spec=grid_spec,
        out_shape=jax.ShapeDtypeStruct((n_tok, D_MODEL), F32),
        compiler_params=_params(("arbitrary",), vmem_limit),
        name="moe",
    )(grp, nval, pos, x2e, g2, wg, wu, wd)


def _block_diag_ones(n, blk):
    i = jnp.arange(n) // blk
    return (i[:, None] == i[None, :]).astype(BF16)


def _layer(x, norm1_g, w_in, lam_re, lam_im, log_step, b_re, b_im, c_re, c_im, d_skip,
           w_glu, b_glu, g_q, g_kc, g_ks, g_kw, pos_k, pos_v, w_ck1, w_ck2, w_cv1, w_cv2,
           out_g_ssm, out_g_att, w_out, norm2_g, w_grp, b_grp, w_exp, b_exp, w_gate, w_up, w_down):
    bsz, seq, _ = x.shape
    assert seq % PROJ_TM == 0 and seq // SEL_BLOCK <= MAX_SEL_BLOCKS
    n_tok = bsz * seq
    x2d = x.reshape(n_tok, D_MODEL)
    q8 = SSM_Q
    n_sub = seq // q8
    nch = seq // CMP_STRIDE

    o_q = D_SSM
    o_kv = D_SSM + D_ATT
    o_gt = o_kv + 6 * D_KV
    kv = lambda i: w_in[:, o_kv + i * D_KV:o_kv + (i + 1) * D_KV]
    wrow = jnp.concatenate([w_in[:, :o_q], kv(0), kv(1), kv(2), kv(4)], axis=1).astype(BF16)
    per_g = HPG * N_BRANCH
    wgt = jnp.zeros((D_MODEL, N_KV * GATE_ROWS), F32)
    for g in range(N_KV):
        wgt = wgt.at[:, g * GATE_ROWS:g * GATE_ROWS + per_g].set(w_in[:, o_gt + g * per_g:o_gt + (g + 1) * per_g])
    wcol = jnp.concatenate([w_in[:, o_q:o_kv], kv(3), kv(5), wgt], axis=1).T.astype(BF16)
    qscale = (HEAD_DIM ** -0.5) * math.log2(math.e)
    gq = (jnp.tile(g_q.astype(F32), N_HEADS) * qscale).reshape(D_ATT, 1)
    gks = jnp.tile(g_ks.astype(F32), N_KV).reshape(1, D_KV)
    gkw = jnp.tile(g_kw.astype(F32), N_KV).reshape(1, D_KV)

    u, qt, kc, vc, ksa, kw, vst, vwt, gate = _in_proj(
        x2d, norm1_g.reshape(1, D_MODEL), wrow, wcol, gq, gks, gkw,
        _block_diag_ones(D_KV, HEAD_DIM), bsz=bsz, seq=seq)

    w_loc, t_in, m_st, pw_re, pw_im, dvec = _s5_weights(
        lam_re, lam_im, log_step, b_re, b_im, c_re, c_im, d_skip, n_sub)
    ys = _s5(u.reshape(bsz, seq, D_SSM), w_loc, t_in, m_st, pw_re, pw_im, dvec,
             bsz=bsz, n_sub=n_sub).reshape(n_tok, D_SSM)

    wide = CMP_STRIDE * HEAD_DIM
    pad8 = lambda p: jnp.zeros((8, 2 * wide), F32).at[0].set(p.reshape(-1)).astype(BF16)
    kcmp, vcmpt = _compress(
        kc, vc,
        w_ck1.astype(BF16), w_ck2.astype(BF16), w_cv1.astype(BF16), w_cv2.T.astype(BF16),
        pad8(pos_k), pad8(pos_v), g_kc.astype(F32).reshape(1, HEAD_DIM), bsz=bsz, nch=nch)
    cstart = jnp.arange(nch) * CMP_STRIDE
    sstart = jnp.arange(MAX_SEL_BLOCKS) * SEL_BLOCK
    ovlt = ((cstart[None, :] < sstart[:, None] + SEL_BLOCK) & (cstart[None, :] + CMP_BLOCK > sstart[:, None])
            & (jnp.arange(MAX_SEL_BLOCKS)[:, None] < seq // SEL_BLOCK)
            & (jnp.arange(nch)[None, :] < nch - 1)).astype(BF16)
    yat = _nsa(qt, kcmp, vcmpt, ksa, kw, vst, vwt, gate, ovlt, bsz=bsz, seq=seq, nch=nch)

    wr = jnp.zeros((D_MODEL, LANES), F32)
    wr = wr.at[:, :N_EXP_GROUPS].set(w_grp).at[:, ROUTER_OFF:ROUTER_OFF + N_EXPERTS].set(w_exp).astype(BF16)
    br = jnp.zeros((1, LANES), F32)
    br = br.at[0, :N_EXP_GROUPS].set(b_grp).at[0, ROUTER_OFF:ROUTER_OFF + N_EXPERTS].set(b_exp)
    g2 = norm2_g.reshape(1, D_MODEL).astype(F32)
    x2e = _out_proj(
        ys, yat, x2d, w_glu.astype(BF16), b_glu.reshape(1, D_SSM).astype(F32),
        out_g_ssm.reshape(1, D_SSM).astype(F32), out_g_att.reshape(D_ATT, 1).astype(F32),
        w_out.astype(BF16), g2, wr, br, seq=seq)

    grp, nval, pos = _moe_plan(x2e[:, D_MODEL].astype(jnp.int32), n_tok)
    gshape = (N_EXP_GROUPS, EXPERTS_PER_GROUP, D_MODEL, D_EXPERT)
    out = _moe(x2e, grp, nval, pos, g2, w_gate.reshape(gshape), w_up.reshape(gshape),
               w_down.reshape(N_EXP_GROUPS, EXPERTS_PER_GROUP * D_EXPERT, D_MODEL))
    return out.reshape(bsz, seq, D_MODEL)


def kernel(x, norm1_g, w_in, lam_re, lam_im, log_step, b_re, b_im, c_re, c_im, d_skip, w_glu, b_glu, g_q, g_kc, g_ks, g_kw, pos_k, pos_v, w_ck1, w_ck2, w_cv1, w_cv2, out_g_ssm, out_g_att, w_out, norm2_g, w_grp, b_grp, w_exp, b_exp, w_gate, w_up, w_down):
    depth = norm1_g.shape[0]
    for l in range(depth):
        x = _layer(x, norm1_g[l], w_in[l], lam_re[l], lam_im[l], log_step[l], b_re[l], b_im[l], c_re[l],
                   c_im[l], d_skip[l], w_glu[l], b_glu[l], g_q[l], g_kc[l], g_ks[l], g_kw[l], pos_k[l],
                   pos_v[l], w_ck1[l], w_ck2[l], w_cv1[l], w_cv2[l], out_g_ssm[l], out_g_att[l], w_out[l],
                   norm2_g[l], w_grp[l], b_grp[l], w_exp[l], b_exp[l], w_gate[l], w_up[l], w_down[l])
    return x
```

```python
import functools
import math

import jax
import jax.numpy as jnp
from jax import lax
from jax.experimental import pallas as pl
from jax.experimental.pallas import tpu as pltpu

D_MODEL = 1024
D_SSM = 512
SSM_CH = 16
SSM_GROUPS = D_SSM // SSM_CH
SSM_STATE = 64
D_ATT = 512
HEAD_DIM = 64
N_HEADS = D_ATT // HEAD_DIM
N_KV = 2
HPG = N_HEADS // N_KV
D_KV = N_KV * HEAD_DIM
N_BRANCH = 3
CMP_STRIDE = 16
CMP_BLOCK = 2 * CMP_STRIDE
CMP_HIDDEN = 256
SEL_BLOCK = 64
N_SELECT = 16
WINDOW = 512
N_EXP_GROUPS = 4
EXPERTS_PER_GROUP = 8
N_EXPERTS = N_EXP_GROUPS * EXPERTS_PER_GROUP
D_EXPERT = 256
EPS = 1e-6
NEG = -1e30
FORCE = 1e9

LANES = 128
SSM_Q = 8
SSM_LT = D_SSM // LANES
ROUTER_OFF = N_EXP_GROUPS
NSA_TQ = 256
NSA_TK = 256
NSA_UNROLL = 8
V_ROWS = HEAD_DIM + 16
MAX_SEL_BLOCKS = 64
MOE_TM = 256
PROJ_TM = 1024
GATE_ROWS = 16
V7X_VMEM_BYTES = 64 * 1024 * 1024
VMEM_LIMIT = V7X_VMEM_BYTES - 8 * 1024 * 1024

F32 = jnp.float32
BF16 = jnp.bfloat16


def _dot(a, b):
    return jnp.dot(a, b, preferred_element_type=F32)


def _dot_nt(a, b):
    return lax.dot_general(a, b, (((1,), (1,)), ((), ())), preferred_element_type=F32)


def _split_dot(x, w):
    hi = x.astype(BF16)
    lo = (x - hi.astype(F32)).astype(BF16)
    return _dot(hi, w) + _dot(lo, w)


def _gelu(x):
    c = math.sqrt(2.0 / math.pi)
    return 0.5 * x * (1.0 + jnp.tanh(c * (x + 0.044715 * (x * x * x))))


def _sigmoid(x):
    return 1.0 / (1.0 + jnp.exp(-x))


def _params(sem, vmem_limit=VMEM_LIMIT):
    return pltpu.CompilerParams(dimension_semantics=sem, vmem_limit_bytes=vmem_limit)


def _in_proj_kernel(x_ref, g1_ref, wrow_ref, wcol_ref, gq_ref, gks_ref, gkw_ref, bd128_ref,
                    u_ref, qt_ref, kc_ref, vc_ref, ksa_ref, kw_ref, vst_ref, vwt_ref, gate_ref, *, tm, nl):
    x = x_ref[...]
    ms = jnp.mean(x * x, axis=-1, keepdims=True)
    hn = (x * lax.rsqrt(ms + EPS) * g1_ref[...]).astype(BF16)

    pr = _dot(hn, wrow_ref[...])
    u_ref[...] = pr[:, :D_SSM]
    kc, vc, ks, kw = [pr[:, D_SSM + i * D_KV:D_SSM + (i + 1) * D_KV] for i in range(4)]
    kss = _split_dot(ks * ks, bd128_ref[...])
    ksn = ks * lax.rsqrt(kss * (1.0 / HEAD_DIM) + EPS) * gks_ref[...]
    kws = _split_dot(kw * kw, bd128_ref[...])
    kwn = kw * lax.rsqrt(kws * (1.0 / HEAD_DIM) + EPS) * gkw_ref[...]
    t0 = (pl.program_id(0) % nl) * tm
    tpos = t0 + lax.broadcasted_iota(jnp.int32, (tm, MAX_SEL_BLOCKS), 0)
    blk = lax.broadcasted_iota(jnp.int32, (tm, MAX_SEL_BLOCKS), 1)
    onehot = jnp.where(tpos // SEL_BLOCK == blk, 1.0, 0.0).astype(BF16)
    for g in range(N_KV):
        sl = slice(g * HEAD_DIM, (g + 1) * HEAD_DIM)
        kc_ref[g] = kc[:, sl]
        vc_ref[g] = vc[:, sl]
        ksa_ref[g] = jnp.concatenate([ksn[:, sl].astype(BF16), onehot], axis=1)
        kw_ref[g] = kwn[:, sl].astype(BF16)

    pc = _dot_nt(wcol_ref[...], hn)
    gq = gq_ref[...]
    for h in range(N_HEADS):
        sl = slice(h * HEAD_DIM, (h + 1) * HEAD_DIM)
        qh = pc[sl]
        ss = jnp.sum(qh * qh, axis=0, keepdims=True)
        qt_ref[h] = (qh * lax.rsqrt(ss * (1.0 / HEAD_DIM) + EPS) * gq[sl]).astype(BF16)
    ones_rows = jnp.where(lax.broadcasted_iota(jnp.int32, (V_ROWS - HEAD_DIM, tm), 0) == 0, 1.0, 0.0)
    for g in range(N_KV):
        for o_ref, base in ((vst_ref, D_ATT), (vwt_ref, D_ATT + D_KV)):
            vt = jnp.concatenate([pc[base + g * HEAD_DIM:base + (g + 1) * HEAD_DIM], ones_rows], axis=0)
            vt = vt.astype(BF16)
            for j in range(tm // NSA_TK):
                o_ref[g, j] = vt[:, j * NSA_TK:(j + 1) * NSA_TK]
        gb = D_ATT + 2 * D_KV + g * GATE_ROWS
        gate_ref[g] = _sigmoid(pc[gb:gb + GATE_ROWS])


def _in_proj(x2d, g1, wrow, wcol, gq, gks, gkw, bd128, *, bsz, seq):
    tm = PROJ_TM
    nl = seq // tm
    n_tok = bsz * seq
    kern = functools.partial(_in_proj_kernel, tm=tm, nl=nl)
    row = lambda i: (i, 0)
    const = lambda i: (0, 0)
    bgl = lambda i: (i // nl, 0, i % nl, 0)
    n_col = wcol.shape[0]
    jt = tm // NSA_TK

    def kvspec(width):
        return pl.BlockSpec((None, N_KV, tm, width), bgl)

    def kvshape(width, dtype=BF16):
        return jax.ShapeDtypeStruct((bsz, N_KV, seq, width), dtype)

    vt_spec = pl.BlockSpec((None, N_KV, jt, V_ROWS, NSA_TK), lambda i: (i // nl, 0, i % nl, 0, 0))
    vt_shape = jax.ShapeDtypeStruct((bsz, N_KV, seq // NSA_TK, V_ROWS, NSA_TK), BF16)
    return pl.pallas_call(
        kern,
        grid=(n_tok // tm,),
        in_specs=[
            pl.BlockSpec((tm, D_MODEL), row),
            pl.BlockSpec((1, D_MODEL), const),
            pl.BlockSpec((D_MODEL, D_SSM + 4 * D_KV), const),
            pl.BlockSpec((n_col, D_MODEL), const),
            pl.BlockSpec((D_ATT, 1), const),
            pl.BlockSpec((1, D_KV), const),
            pl.BlockSpec((1, D_KV), const),
            pl.BlockSpec((D_KV, D_KV), const),
        ],
        out_specs=[
            pl.BlockSpec((tm, D_SSM), row),
            pl.BlockSpec((None, N_HEADS, HEAD_DIM, tm), lambda i: (i // nl, 0, 0, i % nl)),
            kvspec(HEAD_DIM), kvspec(HEAD_DIM), kvspec(2 * HEAD_DIM), kvspec(HEAD_DIM),
            vt_spec, vt_spec,
            pl.BlockSpec((None, N_KV, GATE_ROWS, tm), lambda i: (i // nl, 0, 0, i % nl)),
        ],
        out_shape=[
            jax.ShapeDtypeStruct((n_tok, D_SSM), F32),
            jax.ShapeDtypeStruct((bsz, N_HEADS, HEAD_DIM, seq), BF16),
            kvshape(HEAD_DIM, F32), kvshape(HEAD_DIM, F32), kvshape(2 * HEAD_DIM), kvshape(HEAD_DIM),
            vt_shape, vt_shape,
            jax.ShapeDtypeStruct((bsz, N_KV, GATE_ROWS, seq), F32),
        ],
        compiler_params=_params(("parallel",)),
        name="in_proj",
    )(x2d, g1, wrow, wcol, gq, gks, gkw, bd128)


def _s5_weights(lam_re, lam_im, log_step, b_re, b_im, c_re, c_im, d_skip, n_sub):
    q = SSM_Q
    lam = lax.complex(lam_re.astype(F32), lam_im.astype(F32))
    step = jnp.exp(log_step.astype(F32))[:, None]
    lam_bar = jnp.exp(lam * step)
    b_bar = ((lam_bar - 1.0) / lam)[..., None] * lax.complex(b_re.astype(F32), b_im.astype(F32))
    c = lax.complex(c_re.astype(F32), c_im.astype(F32))
    pows = [jnp.ones_like(lam_bar)]
    for _ in range(q):
        pows.append(pows[-1] * lam_bar)
    pw = jnp.stack(pows)
    lt, a8 = SSM_LT, LANES // SSM_CH
    hp = a8 * SSM_STATE
    e_lane = (jnp.arange(a8)[:, None] == jnp.arange(LANES)[None, :] // SSM_CH).astype(F32)
    e_state = (jnp.arange(a8)[:, None] == jnp.arange(hp)[None, :] // SSM_STATE).astype(F32)
    e_lane_t = jnp.tile(e_lane, (1, q))
    e_state_t = jnp.tile(e_state, (1, 2))

    kk = jnp.real(jnp.einsum('ghp,kgp,gpi->kghi', c, pw[:q], b_bar))
    km = kk.reshape(q, lt, a8, SSM_CH, SSM_CH).transpose(1, 4, 0, 2, 3).reshape(lt, SSM_CH, q, LANES)
    lag = jnp.arange(q)[None, :] - jnp.arange(q)[:, None]
    kg = km[:, :, jnp.clip(lag, 0, q - 1), :] * (lag >= 0)[None, None, :, :, None].astype(F32)
    kc = kg.transpose(0, 2, 1, 3, 4).reshape(lt, q, 1, SSM_CH, q * LANES)
    t_in = (kc * e_lane_t[None, None, :, None, :]).reshape(lt, q * LANES, q * LANES)

    wc = pw[q - 1 - jnp.arange(q)][..., None] * b_bar[None]
    wri = jnp.stack([jnp.real(wc), jnp.imag(wc)])
    wm = (wri.reshape(2, q, lt, a8, SSM_STATE, SSM_CH).transpose(2, 1, 5, 0, 3, 4)
          .reshape(lt, q, 1, SSM_CH, 2 * hp))
    w_loc = (wm * e_state_t[None, None, :, None, :]).reshape(lt, q * LANES, 2 * hp)

    cl = c[None] * pw[1:q + 1][:, :, None, :]
    cri = jnp.stack([jnp.real(cl), -jnp.imag(cl)])
    mm = (cri.reshape(2, q, lt, a8, SSM_CH, SSM_STATE).transpose(2, 0, 5, 1, 3, 4)
          .reshape(lt, 2, 1, SSM_STATE, q * LANES))
    m_st = (mm * e_lane_t[None, None, :, None, :]).reshape(lt, 2 * hp, q * LANES)

    n_lvl = max(1, (n_sub - 1).bit_length())
    lv = [pw[q]]
    for _ in range(n_lvl - 1):
        lv.append(lv[-1] * lv[-1])
    lvs = jnp.stack(lv).reshape(n_lvl, lt, 1, hp)
    pw_re = jnp.real(lvs).transpose(1, 0, 2, 3)
    pw_im = jnp.imag(lvs).transpose(1, 0, 2, 3)
    dvec = jnp.tile(d_skip.astype(F32).reshape(lt, 1, LANES), (1, 1, q))
    return w_loc.astype(BF16), t_in.astype(BF16), m_st.astype(BF16), pw_re, pw_im, dvec


def _s5_kernel(u_ref, w_ref, t_ref, m_ref, pwr_ref, pwi_ref, d_ref, y_ref, *, n_sub, n_lvl):
    half = (LANES // SSM_CH) * SSM_STATE
    q = SSM_Q
    u = jnp.concatenate([u_ref[pl.ds(s, n_sub, stride=q), :] for s in range(q)], axis=1)
    ub = u.astype(BF16)
    s_loc = _dot(ub, w_ref[...])
    re = s_loc[:, :half]
    im = s_loc[:, half:]
    rowi = lax.broadcasted_iota(jnp.int32, (n_sub, half), 0)
    for k in range(n_lvl):
        d = 1 << k
        ar = pwr_ref[k]
        ai = pwi_ref[k]
        keep = rowi >= d
        sre = jnp.where(keep, pltpu.roll(re, d, axis=0), 0.0)
        sim = jnp.where(keep, pltpu.roll(im, d, axis=0), 0.0)
        re, im = re + (ar * sre - ai * sim), im + (ar * sim + ai * sre)
    keep = rowi >= 1
    xre = jnp.where(keep, pltpu.roll(re, 1, axis=0), 0.0)
    xim = jnp.where(keep, pltpu.roll(im, 1, axis=0), 0.0)
    xst = jnp.concatenate([xre, xim], axis=1).astype(BF16)
    y = _dot(ub, t_ref[...]) + _dot(xst, m_ref[...]) + d_ref[...] * u
    for j in range(q):
        y_ref[pl.ds(j, n_sub, stride=q), :] = y[:, j * LANES:(j + 1) * LANES]


def _s5(u, w_loc, t_in, m_st, pw_re, pw_im, dvec, *, bsz, n_sub):
    q = SSM_Q
    n_lvl = pw_re.shape[1]
    kern = functools.partial(_s5_kernel, n_sub=n_sub, n_lvl=n_lvl)
    wide = q * LANES
    seq = n_sub * q
    return pl.pallas_call(
        kern,
        grid=(bsz, SSM_LT),
        in_specs=[
            pl.BlockSpec((None, seq, LANES), lambda b, l: (b, 0, l)),
            pl.BlockSpec((None, wide, wide), lambda b, l: (l, 0, 0)),
            pl.BlockSpec((None, wide, wide), lambda b, l: (l, 0, 0)),
            pl.BlockSpec((None, wide, wide), lambda b, l: (l, 0, 0)),
            pl.BlockSpec((None, n_lvl, 1, wide // 2), lambda b, l: (l, 0, 0, 0)),
            pl.BlockSpec((None, n_lvl, 1, wide // 2), lambda b, l: (l, 0, 0, 0)),
            pl.BlockSpec((None, 1, wide), lambda b, l: (l, 0, 0)),
        ],
        out_specs=pl.BlockSpec((None, seq, LANES), lambda b, l: (b, 0, l)),
        out_shape=jax.ShapeDtypeStruct((bsz, seq, D_SSM), F32),
        compiler_params=_params(("parallel", "parallel")),
        name="s5",
    )(u, w_loc, t_in, m_st, pw_re, pw_im, dvec)


def _compress_kernel(kc_ref, vc_ref, w1k_ref, w2k_ref, w1v_ref, w2vt_ref, posk_ref, posv_ref, gkc_ref,
                     kcmp_ref, vcmpt_ref, *, nch):
    half = CMP_STRIDE * HEAD_DIM

    def hidden(x_ref, w1_ref, pos_ref):
        x = jnp.concatenate([x_ref[pl.ds(j, nch, stride=CMP_STRIDE), :] for j in range(CMP_STRIDE)],
                            axis=1).astype(BF16)
        a = _dot(x, w1_ref[:half, :])
        b = _dot(x, w1_ref[half:, :])
        pv = _dot(pos_ref[...], w1_ref[...])[0:1, :]
        hid = a + pltpu.roll(b, nch - 1, axis=0) + pv
        return _gelu(hid).astype(BF16)

    k = _dot(hidden(kc_ref, w1k_ref, posk_ref), w2k_ref[...])
    ms = jnp.mean(k * k, axis=-1, keepdims=True)
    kcmp_ref[...] = (k * lax.rsqrt(ms + EPS) * gkc_ref[...]).astype(BF16)
    vt = _dot_nt(w2vt_ref[...], hidden(vc_ref, w1v_ref, posv_ref))
    coli = lax.broadcasted_iota(jnp.int32, vt.shape, 1)
    vcmpt_ref[...] = jnp.where(coli < nch - 1, vt, 0.0).astype(BF16)


def _compress(kcf, vcf, w1k, w2k, w1v, w2vt, posk, posv, gkc, *, bsz, nch):
    kern = functools.partial(_compress_kernel, nch=nch)
    wide = CMP_STRIDE * HEAD_DIM
    xspec = pl.BlockSpec((None, None, nch * CMP_STRIDE, HEAD_DIM), lambda b, g: (b, g, 0, 0))
    c2 = lambda b, g: (0, 0)
    return pl.pallas_call(
        kern,
        grid=(bsz, N_KV),
        in_specs=[
            xspec, xspec,
            pl.BlockSpec((2 * wide, CMP_HIDDEN), c2), pl.BlockSpec((CMP_HIDDEN, HEAD_DIM), c2),
            pl.BlockSpec((2 * wide, CMP_HIDDEN), c2), pl.BlockSpec((HEAD_DIM, CMP_HIDDEN), c2),
            pl.BlockSpec((8, 2 * wide), c2), pl.BlockSpec((8, 2 * wide), c2),
            pl.BlockSpec((1, HEAD_DIM), c2),
        ],
        out_specs=[pl.BlockSpec((None, None, nch, HEAD_DIM), lambda b, g: (b, g, 0, 0)),
                   pl.BlockSpec((None, None, HEAD_DIM, nch), lambda b, g: (b, g, 0, 0))],
        out_shape=[jax.ShapeDtypeStruct((bsz, N_KV, nch, HEAD_DIM), BF16),
                   jax.ShapeDtypeStruct((bsz, N_KV, HEAD_DIM, nch), BF16)],
        compiler_params=_params(("parallel", "parallel")),
        name="compress",
    )(kcf, vcf, w1k, w2k, w1v, w2vt, posk, posv, gkc)


def _nsa_kernel(qt_ref, kcmp_ref, vcmpt_ref, ksa_ref, kw_ref, vst_ref, vwt_ref, gate_ref, ovlt_ref, o_ref,
                accs_ref, accw_ref, sel_ref, *, tq, tk, nch):
    qi = pl.program_id(2)
    q0 = qi * tq
    rows = HPG * tq
    qt = jnp.concatenate([qt_ref[h] for h in range(HPG)], axis=1)
    tpos = q0 + lax.broadcasted_iota(jnp.int32, (1, rows), 1) % tq
    krow = lax.broadcasted_iota(jnp.int32, (tk, rows), 0)
    m0 = jnp.full((1, rows), NEG, F32)

    def run_tiles(k_ref, q_all, vt_ref, acc_ref, m, tiles):
        scores = []
        for kt, kind in tiles:
            kc = jnp.maximum(kt, 0) if kind in ('band', 'valid') else kt
            s = _dot(k_ref[pl.ds(pl.multiple_of(kc * tk, tk), tk), :], q_all)
            if kind == 'causal':
                s = jnp.where(kt * tk + krow <= tpos, s, NEG)
            elif kind == 'band':
                s = jnp.where((kt * tk + krow > tpos - WINDOW) & (kt >= 0), s, NEG)
            elif kind == 'valid':
                s = jnp.where(kt >= 0, s, NEG)
            scores.append((kc, s))
        for kc, s in scores:
            m_new = jnp.maximum(m, jnp.max(s, axis=0, keepdims=True))
            alpha = jnp.exp2(m - m_new)
            p = jnp.exp2(s - m_new).astype(BF16)
            acc_ref[...] = alpha * acc_ref[...] + _dot(vt_ref[kc], p)
            m = m_new
        return m

    accw_ref[...] = jnp.zeros_like(accw_ref)
    n_win = WINDOW // tk
    run_tiles(kw_ref, qt, vwt_ref, accw_ref, m0,
              [(qi - n_win, 'band')] + [(qi - n_win + t, 'valid') for t in range(1, n_win)] + [(qi, 'causal')])

    s = _dot(kcmp_ref[...], qt)
    cend = lax.broadcasted_iota(jnp.int32, (nch, rows), 0) * CMP_STRIDE + (CMP_BLOCK - 1)
    s = jnp.where(cend <= tpos, s, NEG)
    m = jnp.max(s, axis=0, keepdims=True)
    p = jnp.exp2(s - m)
    p = p * jnp.where(tpos >= CMP_BLOCK - 1, 1.0 / jnp.sum(p, axis=0, keepdims=True), 0.0)
    o_cmp = _dot(vcmpt_ref[...], p.astype(BF16))

    psum = p[:, 0:tq]
    for h in range(1, HPG):
        psum = psum + p[:, h * tq:(h + 1) * tq]
    hi = psum.astype(BF16)
    lo = (psum - hi.astype(F32)).astype(BF16)
    ovlt = ovlt_ref[...]
    imp = _dot(ovlt, hi) + _dot(ovlt, lo)
    nb = MAX_SEL_BLOCKS
    blk = lax.broadcasted_iota(jnp.int32, (nb, tq), 0)
    cur = (q0 + lax.broadcasted_iota(jnp.int32, (nb, tq), 1)) // SEL_BLOCK
    forced = (blk == 0) | (blk == cur) | (blk == cur - 1)
    imp = jnp.where(forced, FORCE, jnp.where(blk <= cur, imp, NEG))
    sub = 8
    rowl = lax.broadcasted_iota(jnp.int32, (sub, tq), 0)
    n_seen = (q0 + tq - 1) // SEL_BLOCK + 1
    sel_ref[...] = jnp.zeros_like(sel_ref)
    for nbv in range(2 * N_SELECT, nb + 1, N_SELECT):
        @pl.when((n_seen > nbv - N_SELECT) & (n_seen <= nbv))
        def _(nbv=nbv):
            groups = [imp[r:r + sub] for r in range(0, nbv, sub)]
            ranks = [jnp.zeros((sub, tq), F32) for _ in groups]
            for i in range(nbv):
                ri = jnp.broadcast_to(imp[i:i + 1, :], (sub, tq))
                for gi, x in enumerate(groups):
                    if i < gi * sub:
                        ahead = ri >= x
                    elif i >= (gi + 1) * sub:
                        ahead = ri > x
                    else:
                        ahead = (ri > x) | ((ri == x) & (rowl > i - gi * sub))
                    ranks[gi] = ranks[gi] + jnp.where(ahead, 1.0, 0.0)
            rank = jnp.concatenate(ranks, axis=0)
            sel_ref[0:nbv, :] = jnp.where(rank < N_SELECT, 0.0, NEG).astype(BF16)
    sel = sel_ref[...]
    qa = jnp.concatenate([qt, jnp.concatenate([sel] * HPG, axis=1)], axis=0)

    accs_ref[...] = jnp.zeros_like(accs_ref)
    nu = NSA_UNROLL

    def sel_multi(j, m):
        return run_tiles(ksa_ref, qa, vst_ref, accs_ref, m, [(nu * j + t, None) for t in range(nu)])

    m_sel = lax.fori_loop(0, qi // nu, sel_multi, m0)
    for r in range(nu):
        @pl.when(qi % nu == r)
        def _(r=r):
            base = qi - r
            run_tiles(ksa_ref, qa, vst_ref, accs_ref, m_sel,
                      [(base + t, None) for t in range(r)] + [(qi, 'causal')])

    accs = accs_ref[...]
    accw = accw_ref[...]
    o_sel = accs[:HEAD_DIM] * (1.0 / accs[HEAD_DIM:HEAD_DIM + 1])
    o_win = accw[:HEAD_DIM] * (1.0 / accw[HEAD_DIM:HEAD_DIM + 1])
    gt = gate_ref[...]
    for h in range(HPG):
        c = slice(h * tq, (h + 1) * tq)
        o_ref[h * HEAD_DIM:(h + 1) * HEAD_DIM, :] = (
            gt[3 * h:3 * h + 1] * o_cmp[:, c] + gt[3 * h + 1:3 * h + 2] * o_sel[:, c]
            + gt[3 * h + 2:3 * h + 3] * o_win[:, c])


def _nsa(qt, kcmp, vcmpt, ksa, kw, vst, vwt, gate, ovlt, *, bsz, seq, nch):
    tq, tk = NSA_TQ, NSA_TK
    assert tq == tk and WINDOW % tk == 0
    nq = seq // tq
    kern = functools.partial(_nsa_kernel, tq=tq, tk=tk, nch=nch)
    full = lambda b, g, i: (b, g, 0, 0)
    full5 = lambda b, g, i: (b, g, 0, 0, 0)
    qd = HPG * HEAD_DIM
    rows = HPG * tq
    return pl.pallas_call(
        kern,
        grid=(bsz, N_KV, nq),
        in_specs=[
            pl.BlockSpec((None, HPG, HEAD_DIM, tq), lambda b, g, i: (b, g, 0, i)),
            pl.BlockSpec((None, None, nch, HEAD_DIM), full),
            pl.BlockSpec((None, None, HEAD_DIM, nch), full),
            pl.BlockSpec((None, None, seq, 2 * HEAD_DIM), full),
            pl.BlockSpec((None, None, seq, HEAD_DIM), full),
            pl.BlockSpec((None, None, seq // tk, V_ROWS, tk), full5),
            pl.BlockSpec((None, None, seq // tk, V_ROWS, tk), full5),
            pl.BlockSpec((None, None, GATE_ROWS, tq), lambda b, g, i: (b, g, 0, i)),
            pl.BlockSpec((MAX_SEL_BLOCKS, nch), lambda b, g, i: (0, 0)),
        ],
        out_specs=pl.BlockSpec((None, qd, tq), lambda b, g, i: (b, g, i)),
        out_shape=jax.ShapeDtypeStruct((bsz, D_ATT, seq), F32),
        scratch_shapes=[pltpu.VMEM((V_ROWS, rows), F32), pltpu.VMEM((V_ROWS, rows), F32),
                        pltpu.VMEM((MAX_SEL_BLOCKS, tq), BF16)],
        compiler_params=_params(("parallel", "parallel", "arbitrary")),
        name="nsa",
    )(qt, kcmp, vcmpt, ksa, kw, vst, vwt, gate, ovlt)


def _out_proj_kernel(ys_ref, yat_ref, x_ref, wglu_ref, bglu_ref, gs_ref, ga_ref, wo_ref, g2_ref, wr_ref, br_ref,
                     x2e_ref):
    y = _gelu(ys_ref[...])
    y = y * _sigmoid(_dot(y.astype(BF16), wglu_ref[...]) + bglu_ref[...])
    ysn = y * lax.rsqrt(jnp.mean(y * y, axis=-1, keepdims=True) + EPS) * gs_ref[...]
    yat = yat_ref[...]
    yant = yat * lax.rsqrt(jnp.mean(yat * yat, axis=0, keepdims=True) + EPS) * ga_ref[...]
    yan = yant.T
    x2 = x_ref[...] + _dot(ysn.astype(BF16), wo_ref[:D_SSM, :]) + _dot(yan.astype(BF16), wo_ref[D_SSM:, :])
    x2e_ref[:, :D_MODEL] = x2
    h2 = (x2 * lax.rsqrt(jnp.mean(x2 * x2, axis=-1, keepdims=True) + EPS) * g2_ref[...]).astype(BF16)

    logits = _dot(h2, wr_ref[...]) + br_ref[...]
    lane = lax.broadcasted_iota(jnp.int32, logits.shape, 1).astype(F32)
    far = float(LANES)
    is_g = lane < N_EXP_GROUPS
    glog = jnp.where(is_g, logits, -jnp.inf)
    gmax = jnp.max(glog, axis=1, keepdims=True)
    gsum = jnp.sum(jnp.where(is_g, jnp.exp(logits - gmax), 0.0), axis=1, keepdims=True)
    gsel = jnp.min(jnp.where(glog == gmax, lane, far), axis=1, keepdims=True)
    gprob = 1.0 / gsum
    lo = ROUTER_OFF + EXPERTS_PER_GROUP * gsel
    in_e = (lane >= lo) & (lane < lo + EXPERTS_PER_GROUP)
    emax = jnp.max(jnp.where(in_e, logits, -jnp.inf), axis=1, keepdims=True)
    eexp = jnp.where(in_e, jnp.exp(logits - emax), 0.0)
    eprob = jnp.where(in_e, eexp / jnp.sum(eexp, axis=1, keepdims=True), -1.0)
    v1 = jnp.max(eprob, axis=1, keepdims=True)
    i1 = jnp.min(jnp.where(eprob == v1, lane, far), axis=1, keepdims=True)
    rest = jnp.where(lane == i1, -1.0, eprob)
    v2 = jnp.max(rest, axis=1, keepdims=True)
    i2 = jnp.min(jnp.where(rest == v2, lane, far), axis=1, keepdims=True)
    den = v1 + v2
    x2e_ref[:, D_MODEL:] = (jnp.where(lane == i1, v1 / den * gprob, 0.0)
                            + jnp.where(lane == i2, v2 / den * gprob, 0.0)
                            + jnp.where(lane == 0.0, gsel, 0.0))


def _out_proj(ys, yat, x2d, wglu, bglu, gs, ga, wo, g2, wr, br, *, seq):
    n_tok = x2d.shape[0]
    tm = PROJ_TM
    nl = seq // tm
    row = lambda i: (i, 0)
    const = lambda i: (0, 0)
    return pl.pallas_call(
        _out_proj_kernel,
        grid=(n_tok // tm,),
        in_specs=[
            pl.BlockSpec((tm, D_SSM), row),
            pl.BlockSpec((None, D_ATT, tm), lambda i: (i // nl, 0, i % nl)),
            pl.BlockSpec((tm, D_MODEL), row),
            pl.BlockSpec((D_SSM, D_SSM), const),
            pl.BlockSpec((1, D_SSM), const),
            pl.BlockSpec((1, D_SSM), const),
            pl.BlockSpec((D_ATT, 1), const),
            pl.BlockSpec((D_SSM + D_ATT, D_MODEL), const),
            pl.BlockSpec((1, D_MODEL), const),
            pl.BlockSpec((D_MODEL, LANES), const),
            pl.BlockSpec((1, LANES), const),
        ],
        out_specs=pl.BlockSpec((tm, D_MODEL + LANES), row),
        out_shape=jax.ShapeDtypeStruct((n_tok, D_MODEL + LANES), F32),
        compiler_params=_params(("parallel",)),
        name="out_proj",
    )(ys, yat, x2d, wglu, bglu, gs, ga, wo, g2, wr, br)


def _moe_plan(gsel, n_tok):
    tmx = MOE_TM
    n_tiles = n_tok // tmx + N_EXP_GROUPS
    oh = (gsel[:, None] == jnp.arange(N_EXP_GROUPS)[None, :]).astype(jnp.int32)
    csum = jnp.cumsum(oh, axis=0)
    counts = csum[-1]
    rank = jnp.sum(csum * oh, axis=1) - 1
    nt = (counts + tmx - 1) // tmx
    tend = jnp.cumsum(nt)
    toff = tend - nt
    pos = jnp.sum(oh * toff[None, :], axis=1) * tmx + rank
    tile = jnp.arange(n_tiles)
    grp = jnp.minimum(jnp.sum((tile[:, None] >= tend[None, :]).astype(jnp.int32), axis=1), N_EXP_GROUPS - 1)
    nval = jnp.clip(counts[grp] - (tile - toff[grp]) * tmx, 0, tmx)
    return grp.astype(jnp.int32), nval.astype(jnp.int32), pos.astype(jnp.int32)


def _moe_kernel(grp_ref, nval_ref, pos_ref, x_hbm, g2_ref, wg_ref, wu_ref, wd_ref, o_hbm,
                xbuf, obuf, abuf, tok_ref, gsem, ssem, *, tmx, n_tiles, n_tok):
    i = pl.program_id(0)
    slot = i % 2

    @pl.when(i == 0)
    def _():
        def place(t, c):
            tok_ref[pos_ref[t]] = t
            return c

        lax.fori_loop(0, n_tok, place, 0, unroll=8)

        def pad_tile(tile, c):
            def pad_row(r, c2):
                tok_ref[tile * tmx + r] = 0
                return c2

            return lax.fori_loop(nval_ref[tile], tmx, pad_row, c)

        lax.fori_loop(0, n_tiles, pad_tile, 0)

    def gather_row(tile, r, dst_slot):
        t = tok_ref[tile * tmx + r]
        return pltpu.make_async_copy(x_hbm.at[pl.ds(t, 1), :], xbuf.at[dst_slot, pl.ds(r, 1), :], gsem.at[dst_slot])

    def scatter_row(tile, r, src_slot):
        t = tok_ref[tile * tmx + r]
        return pltpu.make_async_copy(obuf.at[src_slot, pl.ds(r, 1), :], o_hbm.at[pl.ds(t, 1), :], ssem.at[src_slot])

    def gather_wait(src_slot):
        pltpu.make_async_copy(x_hbm.at[pl.ds(0, tmx), :], xbuf.at[src_slot], gsem.at[src_slot]).wait()

    def scatter_wait(n, src_slot):
        n8 = pl.multiple_of((n // 8) * 8, 8)

        @pl.when(n8 > 0)
        def _():
            pltpu.make_async_copy(obuf.at[src_slot, pl.ds(0, n8), :], o_hbm.at[pl.ds(0, n8), :],
                                  ssem.at[src_slot]).wait()

        def one(r, c):
            pltpu.make_async_copy(obuf.at[src_slot, pl.ds(0, 1), :], o_hbm.at[pl.ds(0, 1), :],
                                  ssem.at[src_slot]).wait()
            return c

        lax.fori_loop(0, n - n8, one, 0)

    @pl.when(i == 0)
    def _():
        def body(r, c):
            gather_row(0, r, 0).start()
            return c

        lax.fori_loop(0, tmx, body, 0)

    nv = nval_ref[i]
    nv_prev = nval_ref[jnp.maximum(i - 1, 0)]

    @pl.when((i == 0) | (nv_prev > 0))
    def _():
        gather_wait(slot)

    @pl.when(i >= 2)
    def _():
        scatter_wait(nval_ref[i - 2], slot)

    nxt = jnp.minimum(i + 1, n_tiles - 1)
    per = tmx // EXPERTS_PER_GROUP
    prev_full = (i >= 1) & (nv_prev == tmx)

    def experts(with_scatter):
        xe = xbuf[slot]
        x2 = xe[:, :D_MODEL]
        cw = xe[:, D_MODEL:]
        h = (x2 * lax.rsqrt(jnp.mean(x2 * x2, axis=-1, keepdims=True) + EPS) * g2_ref[...]).astype(BF16)
        lane = lax.broadcasted_iota(jnp.int32, cw.shape, 1)
        first = ROUTER_OFF + EXPERTS_PER_GROUP * grp_ref[i]
        for k in range(EXPERTS_PER_GROUP):
            for r in range(k * per, (k + 1) * per):
                gather_row(nxt, r, 1 - slot).start()
                if with_scatter:
                    scatter_row(i - 1, r, 1 - slot).start(priority=1)
            gate = _dot(h, wg_ref[k].astype(BF16))
            up = _dot(h, wu_ref[k].astype(BF16))
            ck = jnp.sum(jnp.where(lane == first + k, cw, 0.0), axis=1, keepdims=True)
            abuf[:, k * D_EXPERT:(k + 1) * D_EXPERT] = (gate * _sigmoid(gate) * up * ck).astype(BF16)
        obuf[slot] = x2 + _dot(abuf[...], wd_ref[...].astype(BF16))

    @pl.when(prev_full & (nv > 0))
    def _():
        experts(True)

    @pl.when(jnp.logical_not(prev_full) & (nv > 0))
    def _():
        experts(False)

    @pl.when(prev_full & (nv == 0))
    def _():
        def body(r, c):
            scatter_row(i - 1, r, 1 - slot).start()
            return c

        lax.fori_loop(0, tmx, body, 0)

    @pl.when(nv < tmx)
    def _():
        def body(r, c):
            scatter_row(i, r, slot).start()
            return c

        lax.fori_loop(0, nv, body, 0)

    @pl.when(i == n_tiles - 1)
    def _():
        @pl.when(nv == tmx)
        def _():
            def body(r, c):
                scatter_row(i, r, slot).start()
                return c

            lax.fori_loop(0, tmx, body, 0)

        @pl.when(nv > 0)
        def _():
            gather_wait(1 - slot)

        scatter_wait(nv_prev, 1 - slot)
        scatter_wait(nv, slot)


def _moe(x2e, grp, nval, pos, g2, wg, wu, wd):
    n_tok = x2e.shape[0]
    tmx = MOE_TM
    n_tiles = grp.shape[0]
    kern = functools.partial(_moe_kernel, tmx=tmx, n_tiles=n_tiles, n_tok=n_tok)
    gk = EXPERTS_PER_GROUP * D_EXPERT
    w_bytes = 2 * 3 * EXPERTS_PER_GROUP * D_MODEL * D_EXPERT * 4
    io_bytes = 2 * tmx * (2 * D_MODEL + LANES) * 4 + tmx * gk * 2
    vmem_limit = w_bytes + io_bytes + 6 * 1024 * 1024
    assert vmem_limit < V7X_VMEM_BYTES
    grid_spec = pltpu.PrefetchScalarGridSpec(
        num_scalar_prefetch=3,
        grid=(n_tiles,),
        in_specs=[
            pl.BlockSpec(memory_space=pl.ANY),
            pl.BlockSpec((1, D_MODEL), lambda i, g, n, t: (0, 0)),
            pl.BlockSpec((None, EXPERTS_PER_GROUP, D_MODEL, D_EXPERT), lambda i, g, n, t: (g[i], 0, 0, 0)),
            pl.BlockSpec((None, EXPERTS_PER_GROUP, D_MODEL, D_EXPERT), lambda i, g, n, t: (g[i], 0, 0, 0)),
            pl.BlockSpec((None, gk, D_MODEL), lambda i, g, n, t: (g[i], 0, 0)),
        ],
        out_specs=pl.BlockSpec(memory_space=pl.ANY),
        scratch_shapes=[
            pltpu.VMEM((2, tmx, D_MODEL + LANES), F32),
            pltpu.VMEM((2, tmx, D_MODEL), F32),
            pltpu.VMEM((tmx, gk), BF16),
            pltpu.SMEM((n_tiles * tmx,), jnp.int32),
            pltpu.SemaphoreType.DMA((2,)),
            pltpu.SemaphoreType.DMA((2,)),
        ],
    )
    return pl.pallas_call(
        kern,
        grid_spec=grid_spec,
        out_shape=jax.ShapeDtypeStruct((n_tok, D_MODEL), F32),
        compiler_params=_params(("arbitrary",), vmem_limit),
        name="moe",
    )(grp, nval, pos, x2e, g2, wg, wu, wd)


def _block_diag_ones(n, blk):
    i = jnp.arange(n) // blk
    return (i[:, None] == i[None, :]).astype(BF16)


def _layer(x, norm1_g, w_in, lam_re, lam_im, log_step, b_re, b_im, c_re, c_im, d_skip,
           w_glu, b_glu, g_q, g_kc, g_ks, g_kw, pos_k, pos_v, w_ck1, w_ck2, w_cv1, w_cv2,
           out_g_ssm, out_g_att, w_out, norm2_g, w_grp, b_grp, w_exp, b_exp, w_gate, w_up, w_down):
    bsz, seq, _ = x.shape
    assert seq % PROJ_TM == 0 and seq // SEL_BLOCK <= MAX_SEL_BLOCKS
    n_tok = bsz * seq
    x2d = x.reshape(n_tok, D_MODEL)
    q8 = SSM_Q
    n_sub = seq // q8
    nch = seq // CMP_STRIDE

    o_q = D_SSM
    o_kv = D_SSM + D_ATT
    o_gt = o_kv + 6 * D_KV
    kv = lambda i: w_in[:, o_kv + i * D_KV:o_kv + (i + 1) * D_KV]
    wrow = jnp.concatenate([w_in[:, :o_q], kv(0), kv(1), kv(2), kv(4)], axis=1).astype(BF16)
    per_g = HPG * N_BRANCH
    wgt = jnp.zeros((D_MODEL, N_KV * GATE_ROWS), F32)
    for g in range(N_KV):
        wgt = wgt.at[:, g * GATE_ROWS:g * GATE_ROWS + per_g].set(w_in[:, o_gt + g * per_g:o_gt + (g + 1) * per_g])
    wcol = jnp.concatenate([w_in[:, o_q:o_kv], kv(3), kv(5), wgt], axis=1).T.astype(BF16)
    qscale = (HEAD_DIM ** -0.5) * math.log2(math.e)
    gq = (jnp.tile(g_q.astype(F32), N_HEADS) * qscale).reshape(D_ATT, 1)
    gks = jnp.tile(g_ks.astype(F32), N_KV).reshape(1, D_KV)
    gkw = jnp.tile(g_kw.astype(F32), N_KV).reshape(1, D_KV)

    u, qt, kc, vc, ksa, kw, vst, vwt, gate = _in_proj(
        x2d, norm1_g.reshape(1, D_MODEL), wrow, wcol, gq, gks, gkw,
        _block_diag_ones(D_KV, HEAD_DIM), bsz=bsz, seq=seq)

    w_loc, t_in, m_st, pw_re, pw_im, dvec = _s5_weights(
        lam_re, lam_im, log_step, b_re, b_im, c_re, c_im, d_skip, n_sub)
    ys = _s5(u.reshape(bsz, seq, D_SSM), w_loc, t_in, m_st, pw_re, pw_im, dvec,
             bsz=bsz, n_sub=n_sub).reshape(n_tok, D_SSM)

    wide = CMP_STRIDE * HEAD_DIM
    pad8 = lambda p: jnp.zeros((8, 2 * wide), F32).at[0].set(p.reshape(-1)).astype(BF16)
    kcmp, vcmpt = _compress(
        kc, vc,
        w_ck1.astype(BF16), w_ck2.astype(BF16), w_cv1.astype(BF16), w_cv2.T.astype(BF16),
        pad8(pos_k), pad8(pos_v), g_kc.astype(F32).reshape(1, HEAD_DIM), bsz=bsz, nch=nch)
    cstart = jnp.arange(nch) * CMP_STRIDE
    sstart = jnp.arange(MAX_SEL_BLOCKS) * SEL_BLOCK
    ovlt = ((cstart[None, :] < sstart[:, None] + SEL_BLOCK) & (cstart[None, :] + CMP_BLOCK > sstart[:, None])
            & (jnp.arange(MAX_SEL_BLOCKS)[:, None] < seq // SEL_BLOCK)
            & (jnp.arange(nch)[None, :] < nch - 1)).astype(BF16)
    yat = _nsa(qt, kcmp, vcmpt, ksa, kw, vst, vwt, gate, ovlt, bsz=bsz, seq=seq, nch=nch)

    wr = jnp.zeros((D_MODEL, LANES), F32)
    wr = wr.at[:, :N_EXP_GROUPS].set(w_grp).at[:, ROUTER_OFF:ROUTER_OFF + N_EXPERTS].set(w_exp).astype(BF16)
    br = jnp.zeros((1, LANES), F32)
    br = br.at[0, :N_EXP_GROUPS].set(b_grp).at[0, ROUTER_OFF:ROUTER_OFF + N_EXPERTS].set(b_exp)
    g2 = norm2_g.reshape(1, D_MODEL).astype(F32)
    x2e = _out_proj(
        ys, yat, x2d, w_glu.astype(BF16), b_glu.reshape(1, D_SSM).astype(F32),
        out_g_ssm.reshape(1, D_SSM).astype(F32), out_g_att.reshape(D_ATT, 1).astype(F32),
        w_out.astype(BF16), g2, wr, br, seq=seq)

    grp, nval, pos = _moe_plan(x2e[:, D_MODEL].astype(jnp.int32), n_tok)
    gshape = (N_EXP_GROUPS, EXPERTS_PER_GROUP, D_MODEL, D_EXPERT)
    out = _moe(x2e, grp, nval, pos, g2, w_gate.reshape(gshape), w_up.reshape(gshape),
               w_down.reshape(N_EXP_GROUPS, EXPERTS_PER_GROUP * D_EXPERT, D_MODEL))
    return out.reshape(bsz, seq, D_MODEL)


def kernel(x, norm1_g, w_in, lam_re, lam_im, log_step, b_re, b_im, c_re, c_im, d_skip, w_glu, b_glu, g_q, g_kc, g_ks, g_kw, pos_k, pos_v, w_ck1, w_ck2, w_cv1, w_cv2, out_g_ssm, out_g_att, w_out, norm2_g, w_grp, b_grp, w_exp, b_exp, w_gate, w_up, w_down):
    depth = norm1_g.shape[0]
    for l in range(depth):
        x = _layer(x, norm1_g[l], w_in[l], lam_re[l], lam_im[l], log_step[l], b_re[l], b_im[l], c_re[l],
                   c_im[l], d_skip[l], w_glu[l], b_glu[l], g_q[l], g_kc[l], g_ks[l], g_kw[l], pos_k[l],
                   pos_v[l], w_ck1[l], w_ck2[l], w_cv1[l], w_cv2[l], out_g_ssm[l], out_g_att[l], w_out[l],
                   norm2_g[l], w_grp[l], b_grp[l], w_exp[l], b_exp[l], w_gate[l], w_up[l], w_down[l])
    return x
```

```python
import functools
import math

import jax
import jax.numpy as jnp
from jax import lax
from jax.experimental import pallas as pl
from jax.experimental.pallas import tpu as pltpu

D_MODEL = 1024
D_SSM = 512
SSM_CH = 16
SSM_GROUPS = D_SSM // SSM_CH
SSM_STATE = 64
D_ATT = 512
HEAD_DIM = 64
N_HEADS = D_ATT // HEAD_DIM
N_KV = 2
HPG = N_HEADS // N_KV
D_KV = N_KV * HEAD_DIM
N_BRANCH = 3
CMP_STRIDE = 16
CMP_BLOCK = 2 * CMP_STRIDE
CMP_HIDDEN = 256
SEL_BLOCK = 64
N_SELECT = 16
WINDOW = 512
N_EXP_GROUPS = 4
EXPERTS_PER_GROUP = 8
N_EXPERTS = N_EXP_GROUPS * EXPERTS_PER_GROUP
D_EXPERT = 256
EPS = 1e-6
NEG = -1e30
FORCE = 1e9

LANES = 128
SSM_Q = 8
SSM_LT = D_SSM // LANES
ROUTER_OFF = N_EXP_GROUPS
NSA_TQ = 256
NSA_TK = 256
NSA_UNROLL = 8
V_ROWS = HEAD_DIM + 16
MAX_SEL_BLOCKS = 64
MOE_TM = 256
PROJ_TM = 1024
GATE_ROWS = 16
V7X_VMEM_BYTES = 64 * 1024 * 1024
VMEM_LIMIT = V7X_VMEM_BYTES - 8 * 1024 * 1024

F32 = jnp.float32
BF16 = jnp.bfloat16


def _dot(a, b):
    return jnp.dot(a, b, preferred_element_type=F32)


def _dot_nt(a, b):
    return lax.dot_general(a, b, (((1,), (1,)), ((), ())), preferred_element_type=F32)


def _split_dot(x, w):
    hi = x.astype(BF16)
    lo = (x - hi.astype(F32)).astype(BF16)
    return _dot(hi, w) + _dot(lo, w)


def _gelu(x):
    c = math.sqrt(2.0 / math.pi)
    return 0.5 * x * (1.0 + jnp.tanh(c * (x + 0.044715 * (x * x * x))))


def _sigmoid(x):
    return 1.0 / (1.0 + jnp.exp(-x))


def _params(sem, vmem_limit=VMEM_LIMIT):
    return pltpu.CompilerParams(dimension_semantics=sem, vmem_limit_bytes=vmem_limit)


def _in_proj_kernel(x_ref, g1_ref, wrow_ref, wcol_ref, gq_ref, gks_ref, gkw_ref, bd128_ref,
                    u_ref, qt_ref, kc_ref, vc_ref, ksa_ref, kw_ref, vst_ref, vwt_ref, gate_ref, *, tm, nl):
    x = x_ref[...]
    ms = jnp.mean(x * x, axis=-1, keepdims=True)
    hn = (x * lax.rsqrt(ms + EPS) * g1_ref[...]).astype(BF16)

    pr = _dot(hn, wrow_ref[...])
    u_ref[...] = pr[:, :D_SSM]
    kc, vc, ks, kw = [pr[:, D_SSM + i * D_KV:D_SSM + (i + 1) * D_KV] for i in range(4)]
    kss = _split_dot(ks * ks, bd128_ref[...])
    ksn = ks * lax.rsqrt(kss * (1.0 / HEAD_DIM) + EPS) * gks_ref[...]
    kws = _split_dot(kw * kw, bd128_ref[...])
    kwn = kw * lax.rsqrt(kws * (1.0 / HEAD_DIM) + EPS) * gkw_ref[...]
    t0 = (pl.program_id(0) % nl) * tm
    tpos = t0 + lax.broadcasted_iota(jnp.int32, (tm, MAX_SEL_BLOCKS), 0)
    blk = lax.broadcasted_iota(jnp.int32, (tm, MAX_SEL_BLOCKS), 1)
    onehot = jnp.where(tpos // SEL_BLOCK == blk, 1.0, 0.0).astype(BF16)
    for g in range(N_KV):
        sl = slice(g * HEAD_DIM, (g + 1) * HEAD_DIM)
        kc_ref[g] = kc[:, sl]
        vc_ref[g] = vc[:, sl]
        ksa_ref[g] = jnp.concatenate([ksn[:, sl].astype(BF16), onehot], axis=1)
        kw_ref[g] = kwn[:, sl].astype(BF16)

    pc = _dot_nt(wcol_ref[...], hn)
    gq = gq_ref[...]
    for h in range(N_HEADS):
        sl = slice(h * HEAD_DIM, (h + 1) * HEAD_DIM)
        qh = pc[sl]
        ss = jnp.sum(qh * qh, axis=0, keepdims=True)
        qt_ref[h] = (qh * lax.rsqrt(ss * (1.0 / HEAD_DIM) + EPS) * gq[sl]).astype(BF16)
    ones_rows = jnp.where(lax.broadcasted_iota(jnp.int32, (V_ROWS - HEAD_DIM, tm), 0) == 0, 1.0, 0.0)
    for g in range(N_KV):
        for o_ref, base in ((vst_ref, D_ATT), (vwt_ref, D_ATT + D_KV)):
            vt = jnp.concatenate([pc[base + g * HEAD_DIM:base + (g + 1) * HEAD_DIM], ones_rows], axis=0)
            vt = vt.astype(BF16)
            for j in range(tm // NSA_TK):
                o_ref[g, j] = vt[:, j * NSA_TK:(j + 1) * NSA_TK]
        gb = D_ATT + 2 * D_KV + g * GATE_ROWS
        gate_ref[g] = _sigmoid(pc[gb:gb + GATE_ROWS])


def _in_proj(x2d, g1, wrow, wcol, gq, gks, gkw, bd128, *, bsz, seq):
    tm = PROJ_TM
    nl = seq // tm
    n_tok = bsz * seq
    kern = functools.partial(_in_proj_kernel, tm=tm, nl=nl)
    row = lambda i: (i, 0)
    const = lambda i: (0, 0)
    bgl = lambda i: (i // nl, 0, i % nl, 0)
    n_col = wcol.shape[0]
    jt = tm // NSA_TK

    def kvspec(width):
        return pl.BlockSpec((None, N_KV, tm, width), bgl)

    def kvshape(width, dtype=BF16):
        return jax.ShapeDtypeStruct((bsz, N_KV, seq, width), dtype)

    vt_spec = pl.BlockSpec((None, N_KV, jt, V_ROWS, NSA_TK), lambda i: (i // nl, 0, i % nl, 0, 0))
    vt_shape = jax.ShapeDtypeStruct((bsz, N_KV, seq // NSA_TK, V_ROWS, NSA_TK), BF16)
    return pl.pallas_call(
        kern,
        grid=(n_tok // tm,),
        in_specs=[
            pl.BlockSpec((tm, D_MODEL), row),
            pl.BlockSpec((1, D_MODEL), const),
            pl.BlockSpec((D_MODEL, D_SSM + 4 * D_KV), const),
            pl.BlockSpec((n_col, D_MODEL), const),
            pl.BlockSpec((D_ATT, 1), const),
            pl.BlockSpec((1, D_KV), const),
            pl.BlockSpec((1, D_KV), const),
            pl.BlockSpec((D_KV, D_KV), const),
        ],
        out_specs=[
            pl.BlockSpec((tm, D_SSM), row),
            pl.BlockSpec((None, N_HEADS, HEAD_DIM, tm), lambda i: (i // nl, 0, 0, i % nl)),
            kvspec(HEAD_DIM), kvspec(HEAD_DIM), kvspec(2 * HEAD_DIM), kvspec(HEAD_DIM),
            vt_spec, vt_spec,
            pl.BlockSpec((None, N_KV, GATE_ROWS, tm), lambda i: (i // nl, 0, 0, i % nl)),
        ],
        out_shape=[
            jax.ShapeDtypeStruct((n_tok, D_SSM), F32),
            jax.ShapeDtypeStruct((bsz, N_HEADS, HEAD_DIM, seq), BF16),
            kvshape(HEAD_DIM, F32), kvshape(HEAD_DIM, F32), kvshape(2 * HEAD_DIM), kvshape(HEAD_DIM),
            vt_shape, vt_shape,
            jax.ShapeDtypeStruct((bsz, N_KV, GATE_ROWS, seq), F32),
        ],
        compiler_params=_params(("parallel",)),
        name="in_proj",
    )(x2d, g1, wrow, wcol, gq, gks, gkw, bd128)


def _s5_weights(lam_re, lam_im, log_step, b_re, b_im, c_re, c_im, d_skip, n_sub):
    q = SSM_Q
    lam = lax.complex(lam_re.astype(F32), lam_im.astype(F32))
    step = jnp.exp(log_step.astype(F32))[:, None]
    lam_bar = jnp.exp(lam * step)
    b_bar = ((lam_bar - 1.0) / lam)[..., None] * lax.complex(b_re.astype(F32), b_im.astype(F32))
    c = lax.complex(c_re.astype(F32), c_im.astype(F32))
    pows = [jnp.ones_like(lam_bar)]
    for _ in range(q):
        pows.append(pows[-1] * lam_bar)
    pw = jnp.stack(pows)
    lt, a8 = SSM_LT, LANES // SSM_CH
    hp = a8 * SSM_STATE
    e_lane = (jnp.arange(a8)[:, None] == jnp.arange(LANES)[None, :] // SSM_CH).astype(F32)
    e_state = (jnp.arange(a8)[:, None] == jnp.arange(hp)[None, :] // SSM_STATE).astype(F32)
    e_lane_t = jnp.tile(e_lane, (1, q))
    e_state_t = jnp.tile(e_state, (1, 2))

    kk = jnp.real(jnp.einsum('ghp,kgp,gpi->kghi', c, pw[:q], b_bar))
    km = kk.reshape(q, lt, a8, SSM_CH, SSM_CH).transpose(1, 4, 0, 2, 3).reshape(lt, SSM_CH, q, LANES)
    lag = jnp.arange(q)[None, :] - jnp.arange(q)[:, None]
    kg = km[:, :, jnp.clip(lag, 0, q - 1), :] * (lag >= 0)[None, None, :, :, None].astype(F32)
    kc = kg.transpose(0, 2, 1, 3, 4).reshape(lt, q, 1, SSM_CH, q * LANES)
    t_c = kc.reshape(lt, q * SSM_CH, q * LANES)

    wc = pw[q - 1 - jnp.arange(q)][..., None] * b_bar[None]
    wri = jnp.stack([jnp.real(wc), jnp.imag(wc)])
    wm = (wri.reshape(2, q, lt, a8, SSM_STATE, SSM_CH).transpose(2, 1, 5, 0, 3, 4)
          .reshape(lt, q, 1, SSM_CH, 2 * hp))
    w_c = wm.reshape(lt, q * SSM_CH, 2 * hp)

    cl = c[None] * pw[1:q + 1][:, :, None, :]
    cri = jnp.stack([jnp.real(cl), -jnp.imag(cl)])
    mm = (cri.reshape(2, q, lt, a8, SSM_CH, SSM_STATE).transpose(2, 0, 5, 1, 3, 4)
          .reshape(lt, 2, 1, SSM_STATE, q * LANES))
    m_c = mm.reshape(lt, 2 * SSM_STATE, q * LANES)

    n_lvl = max(1, (n_sub - 1).bit_length())
    lv = [pw[q]]
    for _ in range(n_lvl - 1):
        lv.append(lv[-1] * lv[-1])
    lvs = jnp.stack(lv).reshape(n_lvl, lt, 1, hp)
    pw_re = jnp.real(lvs).transpose(1, 0, 2, 3)
    pw_im = jnp.imag(lvs).transpose(1, 0, 2, 3)
    dvec = jnp.tile(d_skip.astype(F32).reshape(lt, 1, LANES), (1, 1, q))
    return w_c, t_c, m_c, e_state_t, e_lane_t, pw_re, pw_im, dvec


def _s5_kernel(u_ref, wc_ref, tc_ref, mc_ref, es_ref, el_ref, pwr_ref, pwi_ref, d_ref, y_ref, w_ref, t_ref, m_ref,
               *, n_sub, n_lvl):
    half = (LANES // SSM_CH) * SSM_STATE
    q = SSM_Q
    a8 = LANES // SSM_CH

    @pl.when(pl.program_id(1) == 0)
    def _():
        for a in range(a8):
            el = el_ref[a:a + 1, :]
            es = es_ref[a:a + 1, :]
            for s_ in range(q):
                rows = pl.ds((s_ * a8 + a) * SSM_CH, SSM_CH)
                src = pl.ds(s_ * SSM_CH, SSM_CH)
                t_ref[rows, :] = (tc_ref[src, :] * el).astype(BF16)
                w_ref[rows, :] = (wc_ref[src, :] * es).astype(BF16)
            for r in range(2):
                rows = pl.ds((r * a8 + a) * SSM_STATE, SSM_STATE)
                m_ref[rows, :] = (mc_ref[pl.ds(r * SSM_STATE, SSM_STATE), :] * el).astype(BF16)

    u = jnp.concatenate([u_ref[pl.ds(s, n_sub, stride=q), :] for s in range(q)], axis=1)
    ub = u.astype(BF16)
    s_loc = _dot(ub, w_ref[...])
    re = s_loc[:, :half]
    im = s_loc[:, half:]
    rowi = lax.broadcasted_iota(jnp.int32, (n_sub, half), 0)
    for k in range(n_lvl):
        d = 1 << k
        ar = pwr_ref[k]
        ai = pwi_ref[k]
        keep = rowi >= d
        sre = jnp.where(keep, pltpu.roll(re, d, axis=0), 0.0)
        sim = jnp.where(keep, pltpu.roll(im, d, axis=0), 0.0)
        re, im = re + (ar * sre - ai * sim), im + (ar * sim + ai * sre)
    keep = rowi >= 1
    xre = jnp.where(keep, pltpu.roll(re, 1, axis=0), 0.0)
    xim = jnp.where(keep, pltpu.roll(im, 1, axis=0), 0.0)
    xst = jnp.concatenate([xre, xim], axis=1).astype(BF16)
    y = _dot(ub, t_ref[...]) + _dot(xst, m_ref[...]) + d_ref[...] * u
    for j in range(q):
        y_ref[pl.ds(j, n_sub, stride=q), :] = y[:, j * LANES:(j + 1) * LANES]


def _s5(u, w_c, t_c, m_c, e_state_t, e_lane_t, pw_re, pw_im, dvec, *, bsz, n_sub):
    q = SSM_Q
    n_lvl = pw_re.shape[1]
    kern = functools.partial(_s5_kernel, n_sub=n_sub, n_lvl=n_lvl)
    wide = q * LANES
    seq = n_sub * q
    return pl.pallas_call(
        kern,
        grid=(SSM_LT, bsz),
        in_specs=[
            pl.BlockSpec((None, seq, LANES), lambda l, b: (b, 0, l)),
            pl.BlockSpec((None, q * SSM_CH, wide), lambda l, b: (l, 0, 0)),
            pl.BlockSpec((None, q * SSM_CH, wide), lambda l, b: (l, 0, 0)),
            pl.BlockSpec((None, 2 * SSM_STATE, wide), lambda l, b: (l, 0, 0)),
            pl.BlockSpec((LANES // SSM_CH, wide), lambda l, b: (0, 0)),
            pl.BlockSpec((LANES // SSM_CH, wide), lambda l, b: (0, 0)),
            pl.BlockSpec((None, n_lvl, 1, wide // 2), lambda l, b: (l, 0, 0, 0)),
            pl.BlockSpec((None, n_lvl, 1, wide // 2), lambda l, b: (l, 0, 0, 0)),
            pl.BlockSpec((None, 1, wide), lambda l, b: (l, 0, 0)),
        ],
        out_specs=pl.BlockSpec((None, seq, LANES), lambda l, b: (b, 0, l)),
        out_shape=jax.ShapeDtypeStruct((bsz, seq, D_SSM), F32),
        scratch_shapes=[pltpu.VMEM((wide, wide), BF16)] * 3,
        compiler_params=_params(("arbitrary", "arbitrary")),
        name="s5",
    )(u, w_c, t_c, m_c, e_state_t, e_lane_t, pw_re, pw_im, dvec)


def _compress_kernel(kc_ref, vc_ref, w1k_ref, w2k_ref, w1v_ref, w2vt_ref, posk_ref, posv_ref, gkc_ref,
                     kcmp_ref, vcmpt_ref, *, nch):
    half = CMP_STRIDE * HEAD_DIM

    def hidden(x_ref, w1_ref, pos_ref):
        x = jnp.concatenate([x_ref[pl.ds(j, nch, stride=CMP_STRIDE), :] for j in range(CMP_STRIDE)],
                            axis=1).astype(BF16)
        a = _dot(x, w1_ref[:half, :])
        b = _dot(x, w1_ref[half:, :])
        pv = _dot(pos_ref[...], w1_ref[...])[0:1, :]
        hid = a + pltpu.roll(b, nch - 1, axis=0) + pv
        return _gelu(hid).astype(BF16)

    k = _dot(hidden(kc_ref, w1k_ref, posk_ref), w2k_ref[...])
    ms = jnp.mean(k * k, axis=-1, keepdims=True)
    kcmp_ref[...] = (k * lax.rsqrt(ms + EPS) * gkc_ref[...]).astype(BF16)
    vt = _dot_nt(w2vt_ref[...], hidden(vc_ref, w1v_ref, posv_ref))
    coli = lax.broadcasted_iota(jnp.int32, vt.shape, 1)
    vcmpt_ref[...] = jnp.where(coli < nch - 1, vt, 0.0).astype(BF16)


def _compress(kcf, vcf, w1k, w2k, w1v, w2vt, posk, posv, gkc, *, bsz, nch):
    kern = functools.partial(_compress_kernel, nch=nch)
    wide = CMP_STRIDE * HEAD_DIM
    xspec = pl.BlockSpec((None, None, nch * CMP_STRIDE, HEAD_DIM), lambda b, g: (b, g, 0, 0))
    c2 = lambda b, g: (0, 0)
    return pl.pallas_call(
        kern,
        grid=(bsz, N_KV),
        in_specs=[
            xspec, xspec,
            pl.BlockSpec((2 * wide, CMP_HIDDEN), c2), pl.BlockSpec((CMP_HIDDEN, HEAD_DIM), c2),
            pl.BlockSpec((2 * wide, CMP_HIDDEN), c2), pl.BlockSpec((HEAD_DIM, CMP_HIDDEN), c2),
            pl.BlockSpec((8, 2 * wide), c2), pl.BlockSpec((8, 2 * wide), c2),
            pl.BlockSpec((1, HEAD_DIM), c2),
        ],
        out_specs=[pl.BlockSpec((None, None, nch, HEAD_DIM), lambda b, g: (b, g, 0, 0)),
                   pl.BlockSpec((None, None, HEAD_DIM, nch), lambda b, g: (b, g, 0, 0))],
        out_shape=[jax.ShapeDtypeStruct((bsz, N_KV, nch, HEAD_DIM), BF16),
                   jax.ShapeDtypeStruct((bsz, N_KV, HEAD_DIM, nch), BF16)],
        compiler_params=_params(("parallel", "parallel")),
        name="compress",
    )(kcf, vcf, w1k, w2k, w1v, w2vt, posk, posv, gkc)


def _nsa_kernel(qt_ref, kcmp_ref, vcmpt_ref, ksa_ref, kw_ref, vst_ref, vwt_ref, gate_ref, ovlt_ref, o_ref,
                accs_ref, accw_ref, sel_ref, *, tq, tk, nch):
    qi = pl.program_id(2)
    q0 = qi * tq
    rows = HPG * tq
    qt = jnp.concatenate([qt_ref[h] for h in range(HPG)], axis=1)
    tpos = q0 + lax.broadcasted_iota(jnp.int32, (1, rows), 1) % tq
    krow = lax.broadcasted_iota(jnp.int32, (tk, rows), 0)
    m0 = jnp.full((1, rows), NEG, F32)

    def run_tiles(k_ref, q_all, vt_ref, acc_ref, m, tiles):
        scores = []
        for kt, kind in tiles:
            kc = jnp.maximum(kt, 0) if kind in ('band', 'valid') else kt
            s = _dot(k_ref[pl.ds(pl.multiple_of(kc * tk, tk), tk), :], q_all)
            if kind == 'causal':
                s = jnp.where(kt * tk + krow <= tpos, s, NEG)
            elif kind == 'band':
                s = jnp.where((kt * tk + krow > tpos - WINDOW) & (kt >= 0), s, NEG)
            elif kind == 'valid':
                s = jnp.where(kt >= 0, s, NEG)
            scores.append((kc, s))
        for kc, s in scores:
            m_new = jnp.maximum(m, jnp.max(s, axis=0, keepdims=True))
            alpha = jnp.exp2(m - m_new)
            p = jnp.exp2(s - m_new).astype(BF16)
            acc_ref[...] = alpha * acc_ref[...] + _dot(vt_ref[kc], p)
            m = m_new
        return m

    accw_ref[...] = jnp.zeros_like(accw_ref)
    n_win = WINDOW // tk
    run_tiles(kw_ref, qt, vwt_ref, accw_ref, m0,
              [(qi - n_win, 'band')] + [(qi - n_win + t, 'valid') for t in range(1, n_win)] + [(qi, 'causal')])

    s = _dot(kcmp_ref[...], qt)
    cend = lax.broadcasted_iota(jnp.int32, (nch, rows), 0) * CMP_STRIDE + (CMP_BLOCK - 1)
    s = jnp.where(cend <= tpos, s, NEG)
    m = jnp.max(s, axis=0, keepdims=True)
    p = jnp.exp2(s - m)
    p = p * jnp.where(tpos >= CMP_BLOCK - 1, 1.0 / jnp.sum(p, axis=0, keepdims=True), 0.0)
    o_cmp = _dot(vcmpt_ref[...], p.astype(BF16))

    psum = p[:, 0:tq]
    for h in range(1, HPG):
        psum = psum + p[:, h * tq:(h + 1) * tq]
    hi = psum.astype(BF16)
    lo = (psum - hi.astype(F32)).astype(BF16)
    ovlt = ovlt_ref[...]
    imp = _dot(ovlt, hi) + _dot(ovlt, lo)
    nb = MAX_SEL_BLOCKS
    blk = lax.broadcasted_iota(jnp.int32, (nb, tq), 0)
    cur = (q0 + lax.broadcasted_iota(jnp.int32, (nb, tq), 1)) // SEL_BLOCK
    forced = (blk == 0) | (blk == cur) | (blk == cur - 1)
    imp = jnp.where(forced, FORCE, jnp.where(blk <= cur, imp, NEG))
    sub = 8
    rowl = lax.broadcasted_iota(jnp.int32, (sub, tq), 0)
    n_seen = (q0 + tq - 1) // SEL_BLOCK + 1
    sel_ref[...] = jnp.zeros_like(sel_ref)
    for nbv in range(2 * N_SELECT, nb + 1, N_SELECT):
        @pl.when((n_seen > nbv - N_SELECT) & (n_seen <= nbv))
        def _(nbv=nbv):
            groups = [imp[r:r + sub] for r in range(0, nbv, sub)]
            ranks = [jnp.zeros((sub, tq), F32) for _ in groups]
            for i in range(nbv):
                ri = jnp.broadcast_to(imp[i:i + 1, :], (sub, tq))
                for gi, x in enumerate(groups):
                    if i < gi * sub:
                        ahead = ri >= x
                    elif i >= (gi + 1) * sub:
                        ahead = ri > x
                    else:
                        ahead = (ri > x) | ((ri == x) & (rowl > i - gi * sub))
                    ranks[gi] = ranks[gi] + jnp.where(ahead, 1.0, 0.0)
            rank = jnp.concatenate(ranks, axis=0)
            sel_ref[0:nbv, :] = jnp.where(rank < N_SELECT, 0.0, NEG).astype(BF16)
    sel = sel_ref[...]
    qa = jnp.concatenate([qt, jnp.concatenate([sel] * HPG, axis=1)], axis=0)

    accs_ref[...] = jnp.zeros_like(accs_ref)
    nu = NSA_UNROLL

    def sel_multi(j, m):
        return run_tiles(ksa_ref, qa, vst_ref, accs_ref, m, [(nu * j + t, None) for t in range(nu)])

    m_sel = lax.fori_loop(0, qi // nu, sel_multi, m0)
    for r in range(nu):
        @pl.when(qi % nu == r)
        def _(r=r):
            base = qi - r
            run_tiles(ksa_ref, qa, vst_ref, accs_ref, m_sel,
                      [(base + t, None) for t in range(r)] + [(qi, 'causal')])

    accs = accs_ref[...]
    accw = accw_ref[...]
    o_sel = accs[:HEAD_DIM] * (1.0 / accs[HEAD_DIM:HEAD_DIM + 1])
    o_win = accw[:HEAD_DIM] * (1.0 / accw[HEAD_DIM:HEAD_DIM + 1])
    gt = gate_ref[...]
    for h in range(HPG):
        c = slice(h * tq, (h + 1) * tq)
        o_ref[h * HEAD_DIM:(h + 1) * HEAD_DIM, :] = (
            gt[3 * h:3 * h + 1] * o_cmp[:, c] + gt[3 * h + 1:3 * h + 2] * o_sel[:, c]
            + gt[3 * h + 2:3 * h + 3] * o_win[:, c])


def _nsa(qt, kcmp, vcmpt, ksa, kw, vst, vwt, gate, ovlt, *, bsz, seq, nch):
    tq, tk = NSA_TQ, NSA_TK
    assert tq == tk and WINDOW % tk == 0
    nq = seq // tq
    kern = functools.partial(_nsa_kernel, tq=tq, tk=tk, nch=nch)
    full = lambda b, g, i: (b, g, 0, 0)
    full5 = lambda b, g, i: (b, g, 0, 0, 0)
    qd = HPG * HEAD_DIM
    rows = HPG * tq
    return pl.pallas_call(
        kern,
        grid=(bsz, N_KV, nq),
        in_specs=[
            pl.BlockSpec((None, HPG, HEAD_DIM, tq), lambda b, g, i: (b, g, 0, i)),
            pl.BlockSpec((None, None, nch, HEAD_DIM), full),
            pl.BlockSpec((None, None, HEAD_DIM, nch), full),
            pl.BlockSpec((None, None, seq, 2 * HEAD_DIM), full),
            pl.BlockSpec((None, None, seq, HEAD_DIM), full),
            pl.BlockSpec((None, None, seq // tk, V_ROWS, tk), full5),
            pl.BlockSpec((None, None, seq // tk, V_ROWS, tk), full5),
            pl.BlockSpec((None, None, GATE_ROWS, tq), lambda b, g, i: (b, g, 0, i)),
            pl.BlockSpec((MAX_SEL_BLOCKS, nch), lambda b, g, i: (0, 0)),
        ],
        out_specs=pl.BlockSpec((None, qd, tq), lambda b, g, i: (b, g, i)),
        out_shape=jax.ShapeDtypeStruct((bsz, D_ATT, seq), F32),
        scratch_shapes=[pltpu.VMEM((V_ROWS, rows), F32), pltpu.VMEM((V_ROWS, rows), F32),
                        pltpu.VMEM((MAX_SEL_BLOCKS, tq), BF16)],
        compiler_params=_params(("parallel", "parallel", "arbitrary")),
        name="nsa",
    )(qt, kcmp, vcmpt, ksa, kw, vst, vwt, gate, ovlt)


def _out_proj_kernel(ys_ref, yat_ref, x_ref, wglu_ref, bglu_ref, gs_ref, ga_ref, wo_ref, g2_ref, wr_ref, br_ref,
                     x2e_ref):
    y = _gelu(ys_ref[...])
    y = y * _sigmoid(_dot(y.astype(BF16), wglu_ref[...]) + bglu_ref[...])
    ysn = y * lax.rsqrt(jnp.mean(y * y, axis=-1, keepdims=True) + EPS) * gs_ref[...]
    yat = yat_ref[...]
    yant = yat * lax.rsqrt(jnp.mean(yat * yat, axis=0, keepdims=True) + EPS) * ga_ref[...]
    yan = yant.T
    x2 = x_ref[...] + _dot(ysn.astype(BF16), wo_ref[:D_SSM, :]) + _dot(yan.astype(BF16), wo_ref[D_SSM:, :])
    x2e_ref[:, :D_MODEL] = x2
    h2 = (x2 * lax.rsqrt(jnp.mean(x2 * x2, axis=-1, keepdims=True) + EPS) * g2_ref[...]).astype(BF16)

    logits = _dot(h2, wr_ref[...]) + br_ref[...]
    lane = lax.broadcasted_iota(jnp.int32, logits.shape, 1).astype(F32)
    far = float(LANES)
    is_g = lane < N_EXP_GROUPS
    glog = jnp.where(is_g, logits, -jnp.inf)
    gmax = jnp.max(glog, axis=1, keepdims=True)
    gsum = jnp.sum(jnp.where(is_g, jnp.exp(logits - gmax), 0.0), axis=1, keepdims=True)
    gsel = jnp.min(jnp.where(glog == gmax, lane, far), axis=1, keepdims=True)
    gprob = 1.0 / gsum
    lo = ROUTER_OFF + EXPERTS_PER_GROUP * gsel
    in_e = (lane >= lo) & (lane < lo + EXPERTS_PER_GROUP)
    emax = jnp.max(jnp.where(in_e, logits, -jnp.inf), axis=1, keepdims=True)
    eexp = jnp.where(in_e, jnp.exp(logits - emax), 0.0)
    eprob = jnp.where(in_e, eexp / jnp.sum(eexp, axis=1, keepdims=True), -1.0)
    v1 = jnp.max(eprob, axis=1, keepdims=True)
    i1 = jnp.min(jnp.where(eprob == v1, lane, far), axis=1, keepdims=True)
    rest = jnp.where(lane == i1, -1.0, eprob)
    v2 = jnp.max(rest, axis=1, keepdims=True)
    i2 = jnp.min(jnp.where(rest == v2, lane, far), axis=1, keepdims=True)
    den = v1 + v2
    x2e_ref[:, D_MODEL:] = (jnp.where(lane == i1, v1 / den * gprob, 0.0)
                            + jnp.where(lane == i2, v2 / den * gprob, 0.0)
                            + jnp.where(lane == 0.0, gsel, 0.0))


def _out_proj(ys, yat, x2d, wglu, bglu, gs, ga, wo, g2, wr, br, *, seq):
    n_tok = x2d.shape[0]
    tm = PROJ_TM
    nl = seq // tm
    row = lambda i: (i, 0)
    const = lambda i: (0, 0)
    return pl.pallas_call(
        _out_proj_kernel,
        grid=(n_tok // tm,),
        in_specs=[
            pl.BlockSpec((tm, D_SSM), row),
            pl.BlockSpec((None, D_ATT, tm), lambda i: (i // nl, 0, i % nl)),
            pl.BlockSpec((tm, D_MODEL), row),
            pl.BlockSpec((D_SSM, D_SSM), const),
            pl.BlockSpec((1, D_SSM), const),
            pl.BlockSpec((1, D_SSM), const),
            pl.BlockSpec((D_ATT, 1), const),
            pl.BlockSpec((D_SSM + D_ATT, D_MODEL), const),
            pl.BlockSpec((1, D_MODEL), const),
            pl.BlockSpec((D_MODEL, LANES), const),
            pl.BlockSpec((1, LANES), const),
        ],
        out_specs=pl.BlockSpec((tm, D_MODEL + LANES), row),
        out_shape=jax.ShapeDtypeStruct((n_tok, D_MODEL + LANES), F32),
        compiler_params=_params(("parallel",)),
        name="out_proj",
    )(ys, yat, x2d, wglu, bglu, gs, ga, wo, g2, wr, br)


def _moe_plan(gsel, n_tok):
    tmx = MOE_TM
    n_tiles = n_tok // tmx + N_EXP_GROUPS
    oh = (gsel[:, None] == jnp.arange(N_EXP_GROUPS)[None, :]).astype(jnp.int32)
    csum = jnp.cumsum(oh, axis=0)
    counts = csum[-1]
    rank = jnp.sum(csum * oh, axis=1) - 1
    nt = (counts + tmx - 1) // tmx
    tend = jnp.cumsum(nt)
    toff = tend - nt
    pos = jnp.sum(oh * toff[None, :], axis=1) * tmx + rank
    tile = jnp.arange(n_tiles)
    grp = jnp.minimum(jnp.sum((tile[:, None] >= tend[None, :]).astype(jnp.int32), axis=1), N_EXP_GROUPS - 1)
    nval = jnp.clip(counts[grp] - (tile - toff[grp]) * tmx, 0, tmx)
    return grp.astype(jnp.int32), nval.astype(jnp.int32), pos.astype(jnp.int32)


def _moe_kernel(grp_ref, nval_ref, pos_ref, x_hbm, g2_ref, wg_ref, wu_ref, wd_ref, o_hbm,
                xbuf, obuf, abuf, tok_ref, gsem, ssem, *, tmx, n_tiles, n_tok):
    i = pl.program_id(0)
    slot = i % 2

    @pl.when(i == 0)
    def _():
        def place(t, c):
            tok_ref[pos_ref[t]] = t
            return c

        lax.fori_loop(0, n_tok, place, 0, unroll=8)

        def pad_tile(tile, c):
            def pad_row(r, c2):
                tok_ref[tile * tmx + r] = 0
                return c2

            return lax.fori_loop(nval_ref[tile], tmx, pad_row, c)

        lax.fori_loop(0, n_tiles, pad_tile, 0)

    def gather_row(tile, r, dst_slot):
        t = tok_ref[tile * tmx + r]
        return pltpu.make_async_copy(x_hbm.at[pl.ds(t, 1), :], xbuf.at[dst_slot, pl.ds(r, 1), :], gsem.at[dst_slot])

    def scatter_row(tile, r, src_slot):
        t = tok_ref[tile * tmx + r]
        return pltpu.make_async_copy(obuf.at[src_slot, pl.ds(r, 1), :], o_hbm.at[pl.ds(t, 1), :], ssem.at[src_slot])

    def gather_wait(src_slot):
        pltpu.make_async_copy(x_hbm.at[pl.ds(0, tmx), :], xbuf.at[src_slot], gsem.at[src_slot]).wait()

    def scatter_wait(n, src_slot):
        n8 = pl.multiple_of((n // 8) * 8, 8)

        @pl.when(n8 > 0)
        def _():
            pltpu.make_async_copy(obuf.at[src_slot, pl.ds(0, n8), :], o_hbm.at[pl.ds(0, n8), :],
                                  ssem.at[src_slot]).wait()

        def one(r, c):
            pltpu.make_async_copy(obuf.at[src_slot, pl.ds(0, 1), :], o_hbm.at[pl.ds(0, 1), :],
                                  ssem.at[src_slot]).wait()
            return c

        lax.fori_loop(0, n - n8, one, 0)

    @pl.when(i == 0)
    def _():
        def body(r, c):
            gather_row(0, r, 0).start()
            return c

        lax.fori_loop(0, tmx, body, 0)

    nv = nval_ref[i]
    nv_prev = nval_ref[jnp.maximum(i - 1, 0)]

    @pl.when((i == 0) | (nv_prev > 0))
    def _():
        gather_wait(slot)

    @pl.when(i >= 2)
    def _():
        scatter_wait(nval_ref[i - 2], slot)

    nxt = jnp.minimum(i + 1, n_tiles - 1)
    per = tmx // EXPERTS_PER_GROUP
    prev_full = (i >= 1) & (nv_prev == tmx)

    def experts(with_scatter):
        xe = xbuf[slot]
        x2 = xe[:, :D_MODEL]
        cw = xe[:, D_MODEL:]
        h = (x2 * lax.rsqrt(jnp.mean(x2 * x2, axis=-1, keepdims=True) + EPS) * g2_ref[...]).astype(BF16)
        lane = lax.broadcasted_iota(jnp.int32, cw.shape, 1)
        first = ROUTER_OFF + EXPERTS_PER_GROUP * grp_ref[i]
        for k in range(EXPERTS_PER_GROUP):
            for r in range(k * per, (k + 1) * per):
                gather_row(nxt, r, 1 - slot).start()
                if with_scatter:
                    scatter_row(i - 1, r, 1 - slot).start(priority=1)
            gate = _dot(h, wg_ref[k].astype(BF16))
            up = _dot(h, wu_ref[k].astype(BF16))
            ck = jnp.sum(jnp.where(lane == first + k, cw, 0.0), axis=1, keepdims=True)
            abuf[:, k * D_EXPERT:(k + 1) * D_EXPERT] = (gate * _sigmoid(gate) * up * ck).astype(BF16)
        obuf[slot] = x2 + _dot(abuf[...], wd_ref[...].astype(BF16))

    @pl.when(prev_full & (nv > 0))
    def _():
        experts(True)

    @pl.when(jnp.logical_not(prev_full) & (nv > 0))
    def _():
        experts(False)

    @pl.when(prev_full & (nv == 0))
    def _():
        def body(r, c):
            scatter_row(i - 1, r, 1 - slot).start()
            return c

        lax.fori_loop(0, tmx, body, 0)

    @pl.when(nv < tmx)
    def _():
        def body(r, c):
            scatter_row(i, r, slot).start()
            return c

        lax.fori_loop(0, nv, body, 0)

    @pl.when(i == n_tiles - 1)
    def _():
        @pl.when(nv == tmx)
        def _():
            def body(r, c):
                scatter_row(i, r, slot).start()
                return c

            lax.fori_loop(0, tmx, body, 0)

        @pl.when(nv > 0)
        def _():
            gather_wait(1 - slot)

        scatter_wait(nv_prev, 1 - slot)
        scatter_wait(nv, slot)


def _moe(x2e, grp, nval, pos, g2, wg, wu, wd):
    n_tok = x2e.shape[0]
    tmx = MOE_TM
    n_tiles = grp.shape[0]
    kern = functools.partial(_moe_kernel, tmx=tmx, n_tiles=n_tiles, n_tok=n_tok)
    gk = EXPERTS_PER_GROUP * D_EXPERT
    w_bytes = 2 * 3 * EXPERTS_PER_GROUP * D_MODEL * D_EXPERT * 4
    io_bytes = 2 * tmx * (2 * D_MODEL + LANES) * 4 + tmx * gk * 2
    vmem_limit = w_bytes + io_bytes + 6 * 1024 * 1024
    assert vmem_limit < V7X_VMEM_BYTES
    grid_spec = pltpu.PrefetchScalarGridSpec(
        num_scalar_prefetch=3,
        grid=(n_tiles,),
        in_specs=[
            pl.BlockSpec(memory_space=pl.ANY),
            pl.BlockSpec((1, D_MODEL), lambda i, g, n, t: (0, 0)),
            pl.BlockSpec((None, EXPERTS_PER_GROUP, D_MODEL, D_EXPERT), lambda i, g, n, t: (g[i], 0, 0, 0)),
            pl.BlockSpec((None, EXPERTS_PER_GROUP, D_MODEL, D_EXPERT), lambda i, g, n, t: (g[i], 0, 0, 0)),
            pl.BlockSpec((None, gk, D_MODEL), lambda i, g, n, t: (g[i], 0, 0)),
        ],
        out_specs=pl.BlockSpec(memory_space=pl.ANY),
        scratch_shapes=[
            pltpu.VMEM((2, tmx, D_MODEL + LANES), F32),
            pltpu.VMEM((2, tmx, D_MODEL), F32),
            pltpu.VMEM((tmx, gk), BF16),
            pltpu.SMEM((n_tiles * tmx,), jnp.int32),
            pltpu.SemaphoreType.DMA((2,)),
            pltpu.SemaphoreType.DMA((2,)),
        ],
    )
    return pl.pallas_call(
        kern,
        grid_spec=grid_spec,
        out_shape=jax.ShapeDtypeStruct((n_tok, D_MODEL), F32),
        compiler_params=_params(("arbitrary",), vmem_limit),
        name="moe",
    )(grp, nval, pos, x2e, g2, wg, wu, wd)


def _block_diag_ones(n, blk):
    i = jnp.arange(n) // blk
    return (i[:, None] == i[None, :]).astype(BF16)


def _layer(x, norm1_g, w_in, lam_re, lam_im, log_step, b_re, b_im, c_re, c_im, d_skip,
           w_glu, b_glu, g_q, g_kc, g_ks, g_kw, pos_k, pos_v, w_ck1, w_ck2, w_cv1, w_cv2,
           out_g_ssm, out_g_att, w_out, norm2_g, w_grp, b_grp, w_exp, b_exp, w_gate, w_up, w_down):
    bsz, seq, _ = x.shape
    assert seq % PROJ_TM == 0 and seq // SEL_BLOCK <= MAX_SEL_BLOCKS
    n_tok = bsz * seq
    x2d = x.reshape(n_tok, D_MODEL)
    q8 = SSM_Q
    n_sub = seq // q8
    nch = seq // CMP_STRIDE

    o_q = D_SSM
    o_kv = D_SSM + D_ATT
    o_gt = o_kv + 6 * D_KV
    kv = lambda i: w_in[:, o_kv + i * D_KV:o_kv + (i + 1) * D_KV]
    wrow = jnp.concatenate([w_in[:, :o_q], kv(0), kv(1), kv(2), kv(4)], axis=1).astype(BF16)
    per_g = HPG * N_BRANCH
    wgt = jnp.zeros((D_MODEL, N_KV * GATE_ROWS), F32)
    for g in range(N_KV):
        wgt = wgt.at[:, g * GATE_ROWS:g * GATE_ROWS + per_g].set(w_in[:, o_gt + g * per_g:o_gt + (g + 1) * per_g])
    wcol = jnp.concatenate([w_in[:, o_q:o_kv], kv(3), kv(5), wgt], axis=1).T.astype(BF16)
    qscale = (HEAD_DIM ** -0.5) * math.log2(math.e)
    gq = (jnp.tile(g_q.astype(F32), N_HEADS) * qscale).reshape(D_ATT, 1)
    gks = jnp.tile(g_ks.astype(F32), N_KV).reshape(1, D_KV)
    gkw = jnp.tile(g_kw.astype(F32), N_KV).reshape(1, D_KV)

    u, qt, kc, vc, ksa, kw, vst, vwt, gate = _in_proj(
        x2d, norm1_g.reshape(1, D_MODEL), wrow, wcol, gq, gks, gkw,
        _block_diag_ones(D_KV, HEAD_DIM), bsz=bsz, seq=seq)

    w_c, t_c, m_c, e_state_t, e_lane_t, pw_re, pw_im, dvec = _s5_weights(
        lam_re, lam_im, log_step, b_re, b_im, c_re, c_im, d_skip, n_sub)
    ys = _s5(u.reshape(bsz, seq, D_SSM), w_c, t_c, m_c, e_state_t, e_lane_t, pw_re, pw_im, dvec,
             bsz=bsz, n_sub=n_sub).reshape(n_tok, D_SSM)

    wide = CMP_STRIDE * HEAD_DIM
    pad8 = lambda p: jnp.zeros((8, 2 * wide), F32).at[0].set(p.reshape(-1)).astype(BF16)
    kcmp, vcmpt = _compress(
        kc, vc,
        w_ck1.astype(BF16), w_ck2.astype(BF16), w_cv1.astype(BF16), w_cv2.T.astype(BF16),
        pad8(pos_k), pad8(pos_v), g_kc.astype(F32).reshape(1, HEAD_DIM), bsz=bsz, nch=nch)
    cstart = jnp.arange(nch) * CMP_STRIDE
    sstart = jnp.arange(MAX_SEL_BLOCKS) * SEL_BLOCK
    ovlt = ((cstart[None, :] < sstart[:, None] + SEL_BLOCK) & (cstart[None, :] + CMP_BLOCK > sstart[:, None])
            & (jnp.arange(MAX_SEL_BLOCKS)[:, None] < seq // SEL_BLOCK)
            & (jnp.arange(nch)[None, :] < nch - 1)).astype(BF16)
    yat = _nsa(qt, kcmp, vcmpt, ksa, kw, vst, vwt, gate, ovlt, bsz=bsz, seq=seq, nch=nch)

    wr = jnp.zeros((D_MODEL, LANES), F32)
    wr = wr.at[:, :N_EXP_GROUPS].set(w_grp).at[:, ROUTER_OFF:ROUTER_OFF + N_EXPERTS].set(w_exp).astype(BF16)
    br = jnp.zeros((1, LANES), F32)
    br = br.at[0, :N_EXP_GROUPS].set(b_grp).at[0, ROUTER_OFF:ROUTER_OFF + N_EXPERTS].set(b_exp)
    g2 = norm2_g.reshape(1, D_MODEL).astype(F32)
    x2e = _out_proj(
        ys, yat, x2d, w_glu.astype(BF16), b_glu.reshape(1, D_SSM).astype(F32),
        out_g_ssm.reshape(1, D_SSM).astype(F32), out_g_att.reshape(D_ATT, 1).astype(F32),
        w_out.astype(BF16), g2, wr, br, seq=seq)

    grp, nval, pos = _moe_plan(x2e[:, D_MODEL].astype(jnp.int32), n_tok)
    gshape = (N_EXP_GROUPS, EXPERTS_PER_GROUP, D_MODEL, D_EXPERT)
    out = _moe(x2e, grp, nval, pos, g2, w_gate.reshape(gshape), w_up.reshape(gshape),
               w_down.reshape(N_EXP_GROUPS, EXPERTS_PER_GROUP * D_EXPERT, D_MODEL))
    return out.reshape(bsz, seq, D_MODEL)


def kernel(x, norm1_g, w_in, lam_re, lam_im, log_step, b_re, b_im, c_re, c_im, d_skip, w_glu, b_glu, g_q, g_kc, g_ks, g_kw, pos_k, pos_v, w_ck1, w_ck2, w_cv1, w_cv2, out_g_ssm, out_g_att, w_out, norm2_g, w_grp, b_grp, w_exp, b_exp, w_gate, w_up, w_down):
    depth = norm1_g.shape[0]
    for l in range(depth):
        x = _layer(x, norm1_g[l], w_in[l], lam_re[l], lam_im[l], log_step[l], b_re[l], b_im[l], c_re[l],
                   c_im[l], d_skip[l], w_glu[l], b_glu[l], g_q[l], g_kc[l], g_ks[l], g_kw[l], pos_k[l],
                   pos_v[l], w_ck1[l], w_ck2[l], w_cv1[l], w_cv2[l], out_g_ssm[l], out_g_att[l], w_out[l],
                   norm2_g[l], w_grp[l], b_grp[l], w_exp[l], b_exp[l], w_gate[l], w_up[l], w_down[l])
    return x
```

```python
import functools
import math

import jax
import jax.numpy as jnp
from jax import lax
from jax.experimental import pallas as pl
from jax.experimental.pallas import tpu as pltpu

D_MODEL = 1024
D_SSM = 512
SSM_CH = 16
SSM_GROUPS = D_SSM // SSM_CH
SSM_STATE = 64
D_ATT = 512
HEAD_DIM = 64
N_HEADS = D_ATT // HEAD_DIM
N_KV = 2
HPG = N_HEADS // N_KV
D_KV = N_KV * HEAD_DIM
N_BRANCH = 3
CMP_STRIDE = 16
CMP_BLOCK = 2 * CMP_STRIDE
CMP_HIDDEN = 256
SEL_BLOCK = 64
N_SELECT = 16
WINDOW = 512
N_EXP_GROUPS = 4
EXPERTS_PER_GROUP = 8
N_EXPERTS = N_EXP_GROUPS * EXPERTS_PER_GROUP
D_EXPERT = 256
EPS = 1e-6
NEG = -1e30
FORCE = 1e9

LANES = 128
SUBLANES = 8
SSM_Q = 8
SSM_LT = D_SSM // LANES
ROUTER_OFF = N_EXP_GROUPS
NSA_TQ = 256
NSA_TK = 256
NSA_UNROLL = 8
V_ROWS = HEAD_DIM + 16
MAX_SEL_BLOCKS = 64
MOE_TM = 256
PROJ_TM = 1024
GATE_ROWS = 16
V7X_VMEM_BYTES = 64 * 1024 * 1024
VMEM_LIMIT = V7X_VMEM_BYTES - 8 * 1024 * 1024

F32 = jnp.float32
BF16 = jnp.bfloat16


def _dot(a, b):
    return jnp.dot(a, b, preferred_element_type=F32)


def _dot_nt(a, b):
    return lax.dot_general(a, b, (((1,), (1,)), ((), ())), preferred_element_type=F32)


def _split_dot(x, w):
    hi = x.astype(BF16)
    lo = (x - hi.astype(F32)).astype(BF16)
    return _dot(hi, w) + _dot(lo, w)


def _gelu(x):
    c = math.sqrt(2.0 / math.pi)
    return 0.5 * x * (1.0 + jnp.tanh(c * (x + 0.044715 * (x * x * x))))


def _sigmoid(x):
    return 1.0 / (1.0 + jnp.exp(-x))


def _params(sem, vmem_limit=VMEM_LIMIT):
    return pltpu.CompilerParams(dimension_semantics=sem, vmem_limit_bytes=vmem_limit)


def _in_proj_kernel(x_ref, g1_ref, wrow_ref, wcol_ref, gq_ref, gks_ref, gkw_ref, bd128_ref,
                    u_ref, qt_ref, kc_ref, vc_ref, ksa_ref, kw_ref, vst_ref, vwt_ref, gate_ref, *, tm, nl):
    x = x_ref[...]
    ms = jnp.mean(x * x, axis=-1, keepdims=True)
    hn = (x * lax.rsqrt(ms + EPS) * g1_ref[...]).astype(BF16)

    pr = _dot(hn, wrow_ref[...])
    u_ref[...] = pr[:, :D_SSM]
    kc, vc, ks, kw = [pr[:, D_SSM + i * D_KV:D_SSM + (i + 1) * D_KV] for i in range(4)]
    kss = _split_dot(ks * ks, bd128_ref[...])
    ksn = ks * lax.rsqrt(kss * (1.0 / HEAD_DIM) + EPS) * gks_ref[...]
    kws = _split_dot(kw * kw, bd128_ref[...])
    kwn = kw * lax.rsqrt(kws * (1.0 / HEAD_DIM) + EPS) * gkw_ref[...]
    t0 = (pl.program_id(0) % nl) * tm
    tpos = t0 + lax.broadcasted_iota(jnp.int32, (tm, MAX_SEL_BLOCKS), 0)
    blk = lax.broadcasted_iota(jnp.int32, (tm, MAX_SEL_BLOCKS), 1)
    onehot = jnp.where(tpos // SEL_BLOCK == blk, 1.0, 0.0).astype(BF16)
    for g in range(N_KV):
        sl = slice(g * HEAD_DIM, (g + 1) * HEAD_DIM)
        kc_ref[g] = kc[:, sl]
        vc_ref[g] = vc[:, sl]
        ksa_ref[g] = jnp.concatenate([ksn[:, sl].astype(BF16), onehot], axis=1)
        kw_ref[g] = kwn[:, sl].astype(BF16)

    pc = _dot_nt(wcol_ref[...], hn)
    gq = gq_ref[...]
    for h in range(N_HEADS):
        sl = slice(h * HEAD_DIM, (h + 1) * HEAD_DIM)
        qh = pc[sl]
        ss = jnp.sum(qh * qh, axis=0, keepdims=True)
        qt_ref[h] = (qh * lax.rsqrt(ss * (1.0 / HEAD_DIM) + EPS) * gq[sl]).astype(BF16)
    ones_rows = jnp.where(lax.broadcasted_iota(jnp.int32, (V_ROWS - HEAD_DIM, tm), 0) == 0, 1.0, 0.0)
    for g in range(N_KV):
        for o_ref, base in ((vst_ref, D_ATT), (vwt_ref, D_ATT + D_KV)):
            vt = jnp.concatenate([pc[base + g * HEAD_DIM:base + (g + 1) * HEAD_DIM], ones_rows], axis=0)
            vt = vt.astype(BF16)
            for j in range(tm // NSA_TK):
                o_ref[g, j] = vt[:, j * NSA_TK:(j + 1) * NSA_TK]
        gb = D_ATT + 2 * D_KV + g * GATE_ROWS
        gate_ref[g] = _sigmoid(pc[gb:gb + GATE_ROWS])


def _in_proj(x2d, g1, wrow, wcol, gq, gks, gkw, bd128, *, bsz, seq):
    tm = PROJ_TM
    nl = seq // tm
    n_tok = bsz * seq
    kern = functools.partial(_in_proj_kernel, tm=tm, nl=nl)
    row = lambda i: (i, 0)
    const = lambda i: (0, 0)
    bgl = lambda i: (i // nl, 0, i % nl, 0)
    n_col = wcol.shape[0]
    jt = tm // NSA_TK

    def kvspec(width):
        return pl.BlockSpec((None, N_KV, tm, width), bgl)

    def kvshape(width, dtype=BF16):
        return jax.ShapeDtypeStruct((bsz, N_KV, seq, width), dtype)

    vt_spec = pl.BlockSpec((None, N_KV, jt, V_ROWS, NSA_TK), lambda i: (i // nl, 0, i % nl, 0, 0))
    vt_shape = jax.ShapeDtypeStruct((bsz, N_KV, seq // NSA_TK, V_ROWS, NSA_TK), BF16)
    return pl.pallas_call(
        kern,
        grid=(n_tok // tm,),
        in_specs=[
            pl.BlockSpec((tm, D_MODEL), row),
            pl.BlockSpec((1, D_MODEL), const),
            pl.BlockSpec((D_MODEL, D_SSM + 4 * D_KV), const),
            pl.BlockSpec((n_col, D_MODEL), const),
            pl.BlockSpec((D_ATT, 1), const),
            pl.BlockSpec((1, D_KV), const),
            pl.BlockSpec((1, D_KV), const),
            pl.BlockSpec((D_KV, D_KV), const),
        ],
        out_specs=[
            pl.BlockSpec((tm, D_SSM), row),
            pl.BlockSpec((None, N_HEADS, HEAD_DIM, tm), lambda i: (i // nl, 0, 0, i % nl)),
            kvspec(HEAD_DIM), kvspec(HEAD_DIM), kvspec(2 * HEAD_DIM), kvspec(HEAD_DIM),
            vt_spec, vt_spec,
            pl.BlockSpec((None, N_KV, GATE_ROWS, tm), lambda i: (i // nl, 0, 0, i % nl)),
        ],
        out_shape=[
            jax.ShapeDtypeStruct((n_tok, D_SSM), F32),
            jax.ShapeDtypeStruct((bsz, N_HEADS, HEAD_DIM, seq), BF16),
            kvshape(HEAD_DIM, F32), kvshape(HEAD_DIM, F32), kvshape(2 * HEAD_DIM), kvshape(HEAD_DIM),
            vt_shape, vt_shape,
            jax.ShapeDtypeStruct((bsz, N_KV, GATE_ROWS, seq), F32),
        ],
        compiler_params=_params(("parallel",)),
        name="in_proj",
    )(x2d, g1, wrow, wcol, gq, gks, gkw, bd128)


def _s5_weights(lam_re, lam_im, log_step, b_re, b_im, c_re, c_im, d_skip, n_sub):
    q = SSM_Q
    lam = lax.complex(lam_re.astype(F32), lam_im.astype(F32))
    step = jnp.exp(log_step.astype(F32))[:, None]
    lam_bar = jnp.exp(lam * step)
    b_bar = ((lam_bar - 1.0) / lam)[..., None] * lax.complex(b_re.astype(F32), b_im.astype(F32))
    c = lax.complex(c_re.astype(F32), c_im.astype(F32))
    pows = [jnp.ones_like(lam_bar)]
    for _ in range(q):
        pows.append(pows[-1] * lam_bar)
    pw = jnp.stack(pows)
    lt, a8 = SSM_LT, LANES // SSM_CH
    hp = a8 * SSM_STATE
    e_lane = (jnp.arange(a8)[:, None] == jnp.arange(LANES)[None, :] // SSM_CH).astype(F32)
    e_state = (jnp.arange(a8)[:, None] == jnp.arange(hp)[None, :] // SSM_STATE).astype(F32)
    e_lane_t = jnp.tile(e_lane, (1, q))
    e_state_t = jnp.tile(e_state, (1, 2))

    kk = jnp.real(jnp.einsum('ghp,kgp,gpi->kghi', c, pw[:q], b_bar))
    km = kk.reshape(q, lt, a8, SSM_CH, SSM_CH).transpose(1, 4, 0, 2, 3).reshape(lt, SSM_CH, q, LANES)
    lag = jnp.arange(q)[None, :] - jnp.arange(q)[:, None]
    kg = km[:, :, jnp.clip(lag, 0, q - 1), :] * (lag >= 0)[None, None, :, :, None].astype(F32)
    kc = kg.transpose(0, 2, 1, 3, 4).reshape(lt, q, 1, SSM_CH, q * LANES)
    t_c = kc.reshape(lt, q * SSM_CH, q * LANES)

    wc = pw[q - 1 - jnp.arange(q)][..., None] * b_bar[None]
    wri = jnp.stack([jnp.real(wc), jnp.imag(wc)])
    wm = (wri.reshape(2, q, lt, a8, SSM_STATE, SSM_CH).transpose(2, 1, 5, 0, 3, 4)
          .reshape(lt, q, 1, SSM_CH, 2 * hp))
    w_c = wm.reshape(lt, q * SSM_CH, 2 * hp)

    cl = c[None] * pw[1:q + 1][:, :, None, :]
    cri = jnp.stack([jnp.real(cl), -jnp.imag(cl)])
    mm = (cri.reshape(2, q, lt, a8, SSM_CH, SSM_STATE).transpose(2, 0, 5, 1, 3, 4)
          .reshape(lt, 2, 1, SSM_STATE, q * LANES))
    m_c = mm.reshape(lt, 2 * SSM_STATE, q * LANES)

    n_lvl = max(1, (n_sub - 1).bit_length())
    lv = [pw[q]]
    for _ in range(n_lvl - 1):
        lv.append(lv[-1] * lv[-1])
    lvs = jnp.stack(lv).reshape(n_lvl, lt, 1, hp)
    pw_re = jnp.real(lvs).transpose(1, 0, 2, 3)
    pw_im = jnp.imag(lvs).transpose(1, 0, 2, 3)
    dvec = jnp.tile(d_skip.astype(F32).reshape(lt, 1, LANES), (1, 1, q))
    return w_c, t_c, m_c, e_state_t, e_lane_t, pw_re, pw_im, dvec


def _s5_kernel(u_ref, wc_ref, tc_ref, mc_ref, es_ref, el_ref, pwr_ref, pwi_ref, d_ref, y_ref, w_ref, t_ref, m_ref,
               *, n_sub, n_lvl):
    half = (LANES // SSM_CH) * SSM_STATE
    q = SSM_Q
    a8 = LANES // SSM_CH

    @pl.when(pl.program_id(1) == 0)
    def _():
        for a in range(a8):
            el = el_ref[a:a + 1, :]
            es = es_ref[a:a + 1, :]
            for s_ in range(q):
                rows = pl.ds((s_ * a8 + a) * SSM_CH, SSM_CH)
                src = pl.ds(s_ * SSM_CH, SSM_CH)
                t_ref[rows, :] = (tc_ref[src, :] * el).astype(BF16)
                w_ref[rows, :] = (wc_ref[src, :] * es).astype(BF16)
            for r in range(2):
                rows = pl.ds((r * a8 + a) * SSM_STATE, SSM_STATE)
                m_ref[rows, :] = (mc_ref[pl.ds(r * SSM_STATE, SSM_STATE), :] * el).astype(BF16)

    u = jnp.concatenate([u_ref[pl.ds(s, n_sub, stride=q), :] for s in range(q)], axis=1)
    ub = u.astype(BF16)
    s_loc = _dot(ub, w_ref[...])
    re = s_loc[:, :half]
    im = s_loc[:, half:]
    rowi = lax.broadcasted_iota(jnp.int32, (n_sub, half), 0)
    for k in range(n_lvl):
        d = 1 << k
        ar = pwr_ref[k]
        ai = pwi_ref[k]
        keep = rowi >= d
        sre = jnp.where(keep, pltpu.roll(re, d, axis=0), 0.0)
        sim = jnp.where(keep, pltpu.roll(im, d, axis=0), 0.0)
        re, im = re + (ar * sre - ai * sim), im + (ar * sim + ai * sre)
    keep = rowi >= 1
    xre = jnp.where(keep, pltpu.roll(re, 1, axis=0), 0.0)
    xim = jnp.where(keep, pltpu.roll(im, 1, axis=0), 0.0)
    xst = jnp.concatenate([xre, xim], axis=1).astype(BF16)
    y = _dot(ub, t_ref[...]) + _dot(xst, m_ref[...]) + d_ref[...] * u
    for j in range(q):
        y_ref[pl.ds(j, n_sub, stride=q), :] = y[:, j * LANES:(j + 1) * LANES]


def _s5(u, w_c, t_c, m_c, e_state_t, e_lane_t, pw_re, pw_im, dvec, *, bsz, n_sub):
    q = SSM_Q
    n_lvl = pw_re.shape[1]
    kern = functools.partial(_s5_kernel, n_sub=n_sub, n_lvl=n_lvl)
    wide = q * LANES
    seq = n_sub * q
    return pl.pallas_call(
        kern,
        grid=(SSM_LT, bsz),
        in_specs=[
            pl.BlockSpec((None, seq, LANES), lambda l, b: (b, 0, l)),
            pl.BlockSpec((None, q * SSM_CH, wide), lambda l, b: (l, 0, 0)),
            pl.BlockSpec((None, q * SSM_CH, wide), lambda l, b: (l, 0, 0)),
            pl.BlockSpec((None, 2 * SSM_STATE, wide), lambda l, b: (l, 0, 0)),
            pl.BlockSpec((LANES // SSM_CH, wide), lambda l, b: (0, 0)),
            pl.BlockSpec((LANES // SSM_CH, wide), lambda l, b: (0, 0)),
            pl.BlockSpec((None, n_lvl, 1, wide // 2), lambda l, b: (l, 0, 0, 0)),
            pl.BlockSpec((None, n_lvl, 1, wide // 2), lambda l, b: (l, 0, 0, 0)),
            pl.BlockSpec((None, 1, wide), lambda l, b: (l, 0, 0)),
        ],
        out_specs=pl.BlockSpec((None, seq, LANES), lambda l, b: (b, 0, l)),
        out_shape=jax.ShapeDtypeStruct((bsz, seq, D_SSM), F32),
        scratch_shapes=[pltpu.VMEM((wide, wide), BF16)] * 3,
        compiler_params=_params(("arbitrary", "arbitrary")),
        name="s5",
    )(u, w_c, t_c, m_c, e_state_t, e_lane_t, pw_re, pw_im, dvec)


def _compress_kernel(kc_ref, vc_ref, w1k_ref, w2k_ref, w1v_ref, w2vt_ref, posk_ref, posv_ref, gkc_ref,
                     kcmp_ref, vcmpt_ref, *, nch):
    half = CMP_STRIDE * HEAD_DIM

    def hidden(x_ref, w1_ref, pos_ref):
        x = jnp.concatenate([x_ref[pl.ds(j, nch, stride=CMP_STRIDE), :] for j in range(CMP_STRIDE)],
                            axis=1).astype(BF16)
        a = _dot(x, w1_ref[:half, :])
        b = _dot(x, w1_ref[half:, :])
        pv = _dot(pos_ref[...], w1_ref[...])[0:1, :]
        hid = a + pltpu.roll(b, nch - 1, axis=0) + pv
        return _gelu(hid).astype(BF16)

    k = _dot(hidden(kc_ref, w1k_ref, posk_ref), w2k_ref[...])
    ms = jnp.mean(k * k, axis=-1, keepdims=True)
    kcmp_ref[...] = (k * lax.rsqrt(ms + EPS) * gkc_ref[...]).astype(BF16)
    vt = _dot_nt(w2vt_ref[...], hidden(vc_ref, w1v_ref, posv_ref))
    coli = lax.broadcasted_iota(jnp.int32, vt.shape, 1)
    vcmpt_ref[...] = jnp.where(coli < nch - 1, vt, 0.0).astype(BF16)


def _compress(kcf, vcf, w1k, w2k, w1v, w2vt, posk, posv, gkc, *, bsz, nch):
    kern = functools.partial(_compress_kernel, nch=nch)
    wide = CMP_STRIDE * HEAD_DIM
    xspec = pl.BlockSpec((None, None, nch * CMP_STRIDE, HEAD_DIM), lambda b, g: (b, g, 0, 0))
    c2 = lambda b, g: (0, 0)
    return pl.pallas_call(
        kern,
        grid=(bsz, N_KV),
        in_specs=[
            xspec, xspec,
            pl.BlockSpec((2 * wide, CMP_HIDDEN), c2), pl.BlockSpec((CMP_HIDDEN, HEAD_DIM), c2),
            pl.BlockSpec((2 * wide, CMP_HIDDEN), c2), pl.BlockSpec((HEAD_DIM, CMP_HIDDEN), c2),
            pl.BlockSpec((SUBLANES, 2 * wide), c2), pl.BlockSpec((SUBLANES, 2 * wide), c2),
            pl.BlockSpec((1, HEAD_DIM), c2),
        ],
        out_specs=[pl.BlockSpec((None, None, nch, HEAD_DIM), lambda b, g: (b, g, 0, 0)),
                   pl.BlockSpec((None, None, HEAD_DIM, nch), lambda b, g: (b, g, 0, 0))],
        out_shape=[jax.ShapeDtypeStruct((bsz, N_KV, nch, HEAD_DIM), BF16),
                   jax.ShapeDtypeStruct((bsz, N_KV, HEAD_DIM, nch), BF16)],
        compiler_params=_params(("parallel", "parallel")),
        name="compress",
    )(kcf, vcf, w1k, w2k, w1v, w2vt, posk, posv, gkc)


def _nsa_kernel(qt_ref, kcmp_ref, vcmpt_ref, ksa_ref, kw_ref, vst_ref, vwt_ref, gate_ref, ovlt_ref, o_ref,
                accs_ref, accw_ref, sel_ref, *, tq, tk, nch):
    qi = pl.program_id(2)
    q0 = qi * tq
    rows = HPG * tq
    qt = jnp.concatenate([qt_ref[h] for h in range(HPG)], axis=1)
    tpos = q0 + lax.broadcasted_iota(jnp.int32, (1, rows), 1) % tq
    krow = lax.broadcasted_iota(jnp.int32, (tk, rows), 0)
    m0 = jnp.full((1, rows), NEG, F32)

    def run_tiles(k_ref, q_all, vt_ref, acc_ref, m, tiles):
        scores = []
        for kt, kind in tiles:
            kc = jnp.maximum(kt, 0) if kind in ('band', 'valid') else kt
            s = _dot(k_ref[pl.ds(pl.multiple_of(kc * tk, tk), tk), :], q_all)
            if kind == 'causal':
                s = jnp.where(kt * tk + krow <= tpos, s, NEG)
            elif kind == 'band':
                s = jnp.where((kt * tk + krow > tpos - WINDOW) & (kt >= 0), s, NEG)
            elif kind == 'valid':
                s = jnp.where(kt >= 0, s, NEG)
            scores.append((kc, s))
        for kc, s in scores:
            m_new = jnp.maximum(m, jnp.max(s, axis=0, keepdims=True))
            alpha = jnp.exp2(m - m_new)
            p = jnp.exp2(s - m_new).astype(BF16)
            acc_ref[...] = alpha * acc_ref[...] + _dot(vt_ref[kc], p)
            m = m_new
        return m

    accw_ref[...] = jnp.zeros_like(accw_ref)
    n_win = WINDOW // tk
    run_tiles(kw_ref, qt, vwt_ref, accw_ref, m0,
              [(qi - n_win, 'band')] + [(qi - n_win + t, 'valid') for t in range(1, n_win)] + [(qi, 'causal')])

    s = _dot(kcmp_ref[...], qt)
    cend = lax.broadcasted_iota(jnp.int32, (nch, rows), 0) * CMP_STRIDE + (CMP_BLOCK - 1)
    s = jnp.where(cend <= tpos, s, NEG)
    m = jnp.max(s, axis=0, keepdims=True)
    p = jnp.exp2(s - m)
    p = p * jnp.where(tpos >= CMP_BLOCK - 1, 1.0 / jnp.sum(p, axis=0, keepdims=True), 0.0)
    o_cmp = _dot(vcmpt_ref[...], p.astype(BF16))

    psum = p[:, 0:tq]
    for h in range(1, HPG):
        psum = psum + p[:, h * tq:(h + 1) * tq]
    hi = psum.astype(BF16)
    lo = (psum - hi.astype(F32)).astype(BF16)
    ovlt = ovlt_ref[...]
    imp = _dot(ovlt, hi) + _dot(ovlt, lo)
    nb = MAX_SEL_BLOCKS
    blk = lax.broadcasted_iota(jnp.int32, (nb, tq), 0)
    cur = (q0 + lax.broadcasted_iota(jnp.int32, (nb, tq), 1)) // SEL_BLOCK
    forced = (blk == 0) | (blk == cur) | (blk == cur - 1)
    imp = jnp.where(forced, FORCE, jnp.where(blk <= cur, imp, NEG))
    sub = SUBLANES
    rowl = lax.broadcasted_iota(jnp.int32, (sub, tq), 0)
    n_seen = (q0 + tq - 1) // SEL_BLOCK + 1
    sel_ref[...] = jnp.zeros_like(sel_ref)
    for nbv in range(2 * N_SELECT, nb + 1, N_SELECT):
        @pl.when((n_seen > nbv - N_SELECT) & (n_seen <= nbv))
        def _(nbv=nbv):
            groups = [imp[r:r + sub] for r in range(0, nbv, sub)]
            ranks = [jnp.zeros((sub, tq), F32) for _ in groups]
            for i in range(nbv):
                ri = jnp.broadcast_to(imp[i:i + 1, :], (sub, tq))
                for gi, x in enumerate(groups):
                    if i < gi * sub:
                        ahead = ri >= x
                    elif i >= (gi + 1) * sub:
                        ahead = ri > x
                    else:
                        ahead = (ri > x) | ((ri == x) & (rowl > i - gi * sub))
                    ranks[gi] = ranks[gi] + jnp.where(ahead, 1.0, 0.0)
            rank = jnp.concatenate(ranks, axis=0)
            sel_ref[0:nbv, :] = jnp.where(rank < N_SELECT, 0.0, NEG).astype(BF16)
    sel = sel_ref[...]
    qa = jnp.concatenate([qt, jnp.concatenate([sel] * HPG, axis=1)], axis=0)

    accs_ref[...] = jnp.zeros_like(accs_ref)
    nu = NSA_UNROLL

    def sel_multi(j, m):
        return run_tiles(ksa_ref, qa, vst_ref, accs_ref, m, [(nu * j + t, None) for t in range(nu)])

    m_sel = lax.fori_loop(0, qi // nu, sel_multi, m0)
    for r in range(nu):
        @pl.when(qi % nu == r)
        def _(r=r):
            base = qi - r
            run_tiles(ksa_ref, qa, vst_ref, accs_ref, m_sel,
                      [(base + t, None) for t in range(r)] + [(qi, 'causal')])

    accs = accs_ref[...]
    accw = accw_ref[...]
    o_sel = accs[:HEAD_DIM] * (1.0 / accs[HEAD_DIM:HEAD_DIM + 1])
    o_win = accw[:HEAD_DIM] * (1.0 / accw[HEAD_DIM:HEAD_DIM + 1])
    gt = gate_ref[...]
    for h in range(HPG):
        c = slice(h * tq, (h + 1) * tq)
        o_ref[h * HEAD_DIM:(h + 1) * HEAD_DIM, :] = (
            gt[3 * h:3 * h + 1] * o_cmp[:, c] + gt[3 * h + 1:3 * h + 2] * o_sel[:, c]
            + gt[3 * h + 2:3 * h + 3] * o_win[:, c])


def _nsa(qt, kcmp, vcmpt, ksa, kw, vst, vwt, gate, ovlt, *, bsz, seq, nch):
    tq, tk = NSA_TQ, NSA_TK
    assert tq == tk and WINDOW % tk == 0
    nq = seq // tq
    kern = functools.partial(_nsa_kernel, tq=tq, tk=tk, nch=nch)
    full = lambda b, g, i: (b, g, 0, 0)
    full5 = lambda b, g, i: (b, g, 0, 0, 0)
    qd = HPG * HEAD_DIM
    rows = HPG * tq
    return pl.pallas_call(
        kern,
        grid=(bsz, N_KV, nq),
        in_specs=[
            pl.BlockSpec((None, HPG, HEAD_DIM, tq), lambda b, g, i: (b, g, 0, i)),
            pl.BlockSpec((None, None, nch, HEAD_DIM), full),
            pl.BlockSpec((None, None, HEAD_DIM, nch), full),
            pl.BlockSpec((None, None, seq, 2 * HEAD_DIM), full),
            pl.BlockSpec((None, None, seq, HEAD_DIM), full),
            pl.BlockSpec((None, None, seq // tk, V_ROWS, tk), full5),
            pl.BlockSpec((None, None, seq // tk, V_ROWS, tk), full5),
            pl.BlockSpec((None, None, GATE_ROWS, tq), lambda b, g, i: (b, g, 0, i)),
            pl.BlockSpec((MAX_SEL_BLOCKS, nch), lambda b, g, i: (0, 0)),
        ],
        out_specs=pl.BlockSpec((None, qd, tq), lambda b, g, i: (b, g, i)),
        out_shape=jax.ShapeDtypeStruct((bsz, D_ATT, seq), F32),
        scratch_shapes=[pltpu.VMEM((V_ROWS, rows), F32), pltpu.VMEM((V_ROWS, rows), F32),
                        pltpu.VMEM((MAX_SEL_BLOCKS, tq), BF16)],
        compiler_params=_params(("parallel", "parallel", "arbitrary")),
        name="nsa",
    )(qt, kcmp, vcmpt, ksa, kw, vst, vwt, gate, ovlt)


def _out_proj_kernel(ys_ref, yat_ref, x_ref, wglu_ref, bglu_ref, gs_ref, ga_ref, wo_ref, g2_ref, wr_ref, br_ref,
                     x2e_ref):
    y = _gelu(ys_ref[...])
    y = y * _sigmoid(_dot(y.astype(BF16), wglu_ref[...]) + bglu_ref[...])
    ysn = y * lax.rsqrt(jnp.mean(y * y, axis=-1, keepdims=True) + EPS) * gs_ref[...]
    yat = yat_ref[...]
    yant = yat * lax.rsqrt(jnp.mean(yat * yat, axis=0, keepdims=True) + EPS) * ga_ref[...]
    yan = yant.T
    x2 = x_ref[...] + _dot(ysn.astype(BF16), wo_ref[:D_SSM, :]) + _dot(yan.astype(BF16), wo_ref[D_SSM:, :])
    x2e_ref[:, :D_MODEL] = x2
    h2 = (x2 * lax.rsqrt(jnp.mean(x2 * x2, axis=-1, keepdims=True) + EPS) * g2_ref[...]).astype(BF16)

    logits = _dot(h2, wr_ref[...]) + br_ref[...]
    lane = lax.broadcasted_iota(jnp.int32, logits.shape, 1).astype(F32)
    far = float(LANES)
    is_g = lane < N_EXP_GROUPS
    glog = jnp.where(is_g, logits, -jnp.inf)
    gmax = jnp.max(glog, axis=1, keepdims=True)
    gsum = jnp.sum(jnp.where(is_g, jnp.exp(logits - gmax), 0.0), axis=1, keepdims=True)
    gsel = jnp.min(jnp.where(glog == gmax, lane, far), axis=1, keepdims=True)
    gprob = 1.0 / gsum
    lo = ROUTER_OFF + EXPERTS_PER_GROUP * gsel
    in_e = (lane >= lo) & (lane < lo + EXPERTS_PER_GROUP)
    emax = jnp.max(jnp.where(in_e, logits, -jnp.inf), axis=1, keepdims=True)
    eexp = jnp.where(in_e, jnp.exp(logits - emax), 0.0)
    eprob = jnp.where(in_e, eexp / jnp.sum(eexp, axis=1, keepdims=True), -1.0)
    v1 = jnp.max(eprob, axis=1, keepdims=True)
    i1 = jnp.min(jnp.where(eprob == v1, lane, far), axis=1, keepdims=True)
    rest = jnp.where(lane == i1, -1.0, eprob)
    v2 = jnp.max(rest, axis=1, keepdims=True)
    i2 = jnp.min(jnp.where(rest == v2, lane, far), axis=1, keepdims=True)
    den = v1 + v2
    x2e_ref[:, D_MODEL:] = (jnp.where(lane == i1, v1 / den * gprob, 0.0)
                            + jnp.where(lane == i2, v2 / den * gprob, 0.0)
                            + jnp.where(lane == 0.0, gsel, 0.0))


def _out_proj(ys, yat, x2d, wglu, bglu, gs, ga, wo, g2, wr, br, *, seq):
    n_tok = x2d.shape[0]
    tm = PROJ_TM
    nl = seq // tm
    row = lambda i: (i, 0)
    const = lambda i: (0, 0)
    return pl.pallas_call(
        _out_proj_kernel,
        grid=(n_tok // tm,),
        in_specs=[
            pl.BlockSpec((tm, D_SSM), row),
            pl.BlockSpec((None, D_ATT, tm), lambda i: (i // nl, 0, i % nl)),
            pl.BlockSpec((tm, D_MODEL), row),
            pl.BlockSpec((D_SSM, D_SSM), const),
            pl.BlockSpec((1, D_SSM), const),
            pl.BlockSpec((1, D_SSM), const),
            pl.BlockSpec((D_ATT, 1), const),
            pl.BlockSpec((D_SSM + D_ATT, D_MODEL), const),
            pl.BlockSpec((1, D_MODEL), const),
            pl.BlockSpec((D_MODEL, LANES), const),
            pl.BlockSpec((1, LANES), const),
        ],
        out_specs=pl.BlockSpec((tm, D_MODEL + LANES), row),
        out_shape=jax.ShapeDtypeStruct((n_tok, D_MODEL + LANES), F32),
        compiler_params=_params(("parallel",)),
        name="out_proj",
    )(ys, yat, x2d, wglu, bglu, gs, ga, wo, g2, wr, br)


def _moe_plan(gsel, n_tok):
    tmx = MOE_TM
    n_tiles = n_tok // tmx + N_EXP_GROUPS
    oh = (gsel[:, None] == jnp.arange(N_EXP_GROUPS)[None, :]).astype(jnp.int32)
    csum = jnp.cumsum(oh, axis=0)
    counts = csum[-1]
    rank = jnp.sum(csum * oh, axis=1) - 1
    nt = (counts + tmx - 1) // tmx
    tend = jnp.cumsum(nt)
    toff = tend - nt
    pos = jnp.sum(oh * toff[None, :], axis=1) * tmx + rank
    tile = jnp.arange(n_tiles)
    grp = jnp.minimum(jnp.sum((tile[:, None] >= tend[None, :]).astype(jnp.int32), axis=1), N_EXP_GROUPS - 1)
    nval = jnp.clip(counts[grp] - (tile - toff[grp]) * tmx, 0, tmx)
    return grp.astype(jnp.int32), nval.astype(jnp.int32), pos.astype(jnp.int32)


def _moe_kernel(grp_ref, nval_ref, pos_ref, x_hbm, g2_ref, wg_ref, wu_ref, wd_ref, o_hbm,
                xbuf, obuf, abuf, tok_ref, gsem, ssem, *, tmx, n_tiles, n_tok):
    i = pl.program_id(0)
    slot = i % 2

    @pl.when(i == 0)
    def _():
        def place(t, c):
            tok_ref[pos_ref[t]] = t
            return c

        lax.fori_loop(0, n_tok, place, 0, unroll=8)

        def pad_tile(tile, c):
            def pad_row(r, c2):
                tok_ref[tile * tmx + r] = 0
                return c2

            return lax.fori_loop(nval_ref[tile], tmx, pad_row, c)

        lax.fori_loop(0, n_tiles, pad_tile, 0)

    def gather_row(tile, r, dst_slot):
        t = tok_ref[tile * tmx + r]
        return pltpu.make_async_copy(x_hbm.at[pl.ds(t, 1), :], xbuf.at[dst_slot, pl.ds(r, 1), :], gsem.at[dst_slot])

    def scatter_row(tile, r, src_slot):
        t = tok_ref[tile * tmx + r]
        return pltpu.make_async_copy(obuf.at[src_slot, pl.ds(r, 1), :], o_hbm.at[pl.ds(t, 1), :], ssem.at[src_slot])

    def gather_wait(src_slot):
        pltpu.make_async_copy(x_hbm.at[pl.ds(0, tmx), :], xbuf.at[src_slot], gsem.at[src_slot]).wait()

    def scatter_wait(n, src_slot):
        n8 = pl.multiple_of((n // SUBLANES) * SUBLANES, SUBLANES)

        @pl.when(n8 > 0)
        def _():
            pltpu.make_async_copy(obuf.at[src_slot, pl.ds(0, n8), :], o_hbm.at[pl.ds(0, n8), :],
                                  ssem.at[src_slot]).wait()

        def one(r, c):
            pltpu.make_async_copy(obuf.at[src_slot, pl.ds(0, 1), :], o_hbm.at[pl.ds(0, 1), :],
                                  ssem.at[src_slot]).wait()
            return c

        lax.fori_loop(0, n - n8, one, 0)

    @pl.when(i == 0)
    def _():
        def body(r, c):
            gather_row(0, r, 0).start()
            return c

        lax.fori_loop(0, tmx, body, 0)

    nv = nval_ref[i]
    nv_prev = nval_ref[jnp.maximum(i - 1, 0)]

    @pl.when((i == 0) | (nv_prev > 0))
    def _():
        gather_wait(slot)

    @pl.when(i >= 2)
    def _():
        scatter_wait(nval_ref[i - 2], slot)

    nxt = jnp.minimum(i + 1, n_tiles - 1)
    per = tmx // EXPERTS_PER_GROUP
    prev_full = (i >= 1) & (nv_prev == tmx)

    def experts(with_scatter):
        xe = xbuf[slot]
        x2 = xe[:, :D_MODEL]
        cw = xe[:, D_MODEL:]
        h = (x2 * lax.rsqrt(jnp.mean(x2 * x2, axis=-1, keepdims=True) + EPS) * g2_ref[...]).astype(BF16)
        lane = lax.broadcasted_iota(jnp.int32, cw.shape, 1)
        first = ROUTER_OFF + EXPERTS_PER_GROUP * grp_ref[i]
        for k in range(EXPERTS_PER_GROUP):
            for r in range(k * per, (k + 1) * per):
                gather_row(nxt, r, 1 - slot).start()
                if with_scatter:
                    scatter_row(i - 1, r, 1 - slot).start()
            gate = _dot(h, wg_ref[k].astype(BF16))
            up = _dot(h, wu_ref[k].astype(BF16))
            ck = jnp.sum(jnp.where(lane == first + k, cw, 0.0), axis=1, keepdims=True)
            abuf[:, k * D_EXPERT:(k + 1) * D_EXPERT] = (gate * _sigmoid(gate) * up * ck).astype(BF16)
        obuf[slot] = x2 + _dot(abuf[...], wd_ref[...].astype(BF16))

    @pl.when(prev_full & (nv > 0))
    def _():
        experts(True)

    @pl.when(jnp.logical_not(prev_full) & (nv > 0))
    def _():
        experts(False)

    @pl.when(prev_full & (nv == 0))
    def _():
        def body(r, c):
            scatter_row(i - 1, r, 1 - slot).start()
            return c

        lax.fori_loop(0, tmx, body, 0)

    @pl.when(nv < tmx)
    def _():
        def body(r, c):
            scatter_row(i, r, slot).start()
            return c

        lax.fori_loop(0, nv, body, 0)

    @pl.when(i == n_tiles - 1)
    def _():
        @pl.when(nv == tmx)
        def _():
            def body(r, c):
                scatter_row(i, r, slot).start()
                return c

            lax.fori_loop(0, tmx, body, 0)

        @pl.when(nv > 0)
        def _():
            gather_wait(1 - slot)

        scatter_wait(nv_prev, 1 - slot)
        scatter_wait(nv, slot)


def _moe(x2e, grp, nval, pos, g2, wg, wu, wd):
    n_tok = x2e.shape[0]
    tmx = MOE_TM
    n_tiles = grp.shape[0]
    kern = functools.partial(_moe_kernel, tmx=tmx, n_tiles=n_tiles, n_tok=n_tok)
    gk = EXPERTS_PER_GROUP * D_EXPERT
    w_bytes = 2 * 3 * EXPERTS_PER_GROUP * D_MODEL * D_EXPERT * 4
    io_bytes = 2 * tmx * (2 * D_MODEL + LANES) * 4 + tmx * gk * 2
    vmem_limit = w_bytes + io_bytes + 6 * 1024 * 1024
    assert vmem_limit < V7X_VMEM_BYTES
    grid_spec = pltpu.PrefetchScalarGridSpec(
        num_scalar_prefetch=3,
        grid=(n_tiles,),
        in_specs=[
            pl.BlockSpec(memory_space=pl.ANY),
            pl.BlockSpec((1, D_MODEL), lambda i, g, n, t: (0, 0)),
            pl.BlockSpec((None, EXPERTS_PER_GROUP, D_MODEL, D_EXPERT), lambda i, g, n, t: (g[i], 0, 0, 0)),
            pl.BlockSpec((None, EXPERTS_PER_GROUP, D_MODEL, D_EXPERT), lambda i, g, n, t: (g[i], 0, 0, 0)),
            pl.BlockSpec((None, gk, D_MODEL), lambda i, g, n, t: (g[i], 0, 0)),
        ],
        out_specs=pl.BlockSpec(memory_space=pl.ANY),
        scratch_shapes=[
            pltpu.VMEM((2, tmx, D_MODEL + LANES), F32),
            pltpu.VMEM((2, tmx, D_MODEL), F32),
            pltpu.VMEM((tmx, gk), BF16),
            pltpu.SMEM((n_tiles * tmx,), jnp.int32),
            pltpu.SemaphoreType.DMA((2,)),
            pltpu.SemaphoreType.DMA((2,)),
        ],
    )
    return pl.pallas_call(
        kern,
        grid_spec=grid_spec,
        out_shape=jax.ShapeDtypeStruct((n_tok, D_MODEL), F32),
        compiler_params=_params(("arbitrary",), vmem_limit),
        name="moe",
    )(grp, nval, pos, x2e, g2, wg, wu, wd)


def _block_diag_ones(n, blk):
    i = jnp.arange(n) // blk
    return (i[:, None] == i[None, :]).astype(BF16)


def _layer(x, norm1_g, w_in, lam_re, lam_im, log_step, b_re, b_im, c_re, c_im, d_skip,
           w_glu, b_glu, g_q, g_kc, g_ks, g_kw, pos_k, pos_v, w_ck1, w_ck2, w_cv1, w_cv2,
           out_g_ssm, out_g_att, w_out, norm2_g, w_grp, b_grp, w_exp, b_exp, w_gate, w_up, w_down):
    bsz, seq, _ = x.shape
    assert seq % PROJ_TM == 0 and seq // SEL_BLOCK <= MAX_SEL_BLOCKS
    n_tok = bsz * seq
    x2d = x.reshape(n_tok, D_MODEL)
    q8 = SSM_Q
    n_sub = seq // q8
    nch = seq // CMP_STRIDE

    o_q = D_SSM
    o_kv = D_SSM + D_ATT
    o_gt = o_kv + 6 * D_KV
    kv = lambda i: w_in[:, o_kv + i * D_KV:o_kv + (i + 1) * D_KV]
    wrow = jnp.concatenate([w_in[:, :o_q], kv(0), kv(1), kv(2), kv(4)], axis=1).astype(BF16)
    per_g = HPG * N_BRANCH
    wgt = jnp.zeros((D_MODEL, N_KV * GATE_ROWS), F32)
    for g in range(N_KV):
        wgt = wgt.at[:, g * GATE_ROWS:g * GATE_ROWS + per_g].set(w_in[:, o_gt + g * per_g:o_gt + (g + 1) * per_g])
    wcol = jnp.concatenate([w_in[:, o_q:o_kv], kv(3), kv(5), wgt], axis=1).T.astype(BF16)
    qscale = (HEAD_DIM ** -0.5) * math.log2(math.e)
    gq = (jnp.tile(g_q.astype(F32), N_HEADS) * qscale).reshape(D_ATT, 1)
    gks = jnp.tile(g_ks.astype(F32), N_KV).reshape(1, D_KV)
    gkw = jnp.tile(g_kw.astype(F32), N_KV).reshape(1, D_KV)

    u, qt, kc, vc, ksa, kw, vst, vwt, gate = _in_proj(
        x2d, norm1_g.reshape(1, D_MODEL), wrow, wcol, gq, gks, gkw,
        _block_diag_ones(D_KV, HEAD_DIM), bsz=bsz, seq=seq)

    w_c, t_c, m_c, e_state_t, e_lane_t, pw_re, pw_im, dvec = _s5_weights(
        lam_re, lam_im, log_step, b_re, b_im, c_re, c_im, d_skip, n_sub)
    ys = _s5(u.reshape(bsz, seq, D_SSM), w_c, t_c, m_c, e_state_t, e_lane_t, pw_re, pw_im, dvec,
             bsz=bsz, n_sub=n_sub).reshape(n_tok, D_SSM)

    wide = CMP_STRIDE * HEAD_DIM
    pad8 = lambda p: jnp.zeros((SUBLANES, 2 * wide), F32).at[0].set(p.reshape(-1)).astype(BF16)
    kcmp, vcmpt = _compress(
        kc, vc,
        w_ck1.astype(BF16), w_ck2.astype(BF16), w_cv1.astype(BF16), w_cv2.T.astype(BF16),
        pad8(pos_k), pad8(pos_v), g_kc.astype(F32).reshape(1, HEAD_DIM), bsz=bsz, nch=nch)
    cstart = jnp.arange(nch) * CMP_STRIDE
    sstart = jnp.arange(MAX_SEL_BLOCKS) * SEL_BLOCK
    ovlt = ((cstart[None, :] < sstart[:, None] + SEL_BLOCK) & (cstart[None, :] + CMP_BLOCK > sstart[:, None])
            & (jnp.arange(MAX_SEL_BLOCKS)[:, None] < seq // SEL_BLOCK)
            & (jnp.arange(nch)[None, :] < nch - 1)).astype(BF16)
    yat = _nsa(qt, kcmp, vcmpt, ksa, kw, vst, vwt, gate, ovlt, bsz=bsz, seq=seq, nch=nch)

    wr = jnp.zeros((D_MODEL, LANES), F32)
    wr = wr.at[:, :N_EXP_GROUPS].set(w_grp).at[:, ROUTER_OFF:ROUTER_OFF + N_EXPERTS].set(w_exp).astype(BF16)
    br = jnp.zeros((1, LANES), F32)
    br = br.at[0, :N_EXP_GROUPS].set(b_grp).at[0, ROUTER_OFF:ROUTER_OFF + N_EXPERTS].set(b_exp)
    g2 = norm2_g.reshape(1, D_MODEL).astype(F32)
    x2e = _out_proj(
        ys, yat, x2d, w_glu.astype(BF16), b_glu.reshape(1, D_SSM).astype(F32),
        out_g_ssm.reshape(1, D_SSM).astype(F32), out_g_att.reshape(D_ATT, 1).astype(F32),
        w_out.astype(BF16), g2, wr, br, seq=seq)

    grp, nval, pos = _moe_plan(x2e[:, D_MODEL].astype(jnp.int32), n_tok)
    gshape = (N_EXP_GROUPS, EXPERTS_PER_GROUP, D_MODEL, D_EXPERT)
    out = _moe(x2e, grp, nval, pos, g2, w_gate.reshape(gshape), w_up.reshape(gshape),
               w_down.reshape(N_EXP_GROUPS, EXPERTS_PER_GROUP * D_EXPERT, D_MODEL))
    return out.reshape(bsz, seq, D_MODEL)


def kernel(x, norm1_g, w_in, lam_re, lam_im, log_step, b_re, b_im, c_re, c_im, d_skip, w_glu, b_glu, g_q, g_kc, g_ks, g_kw, pos_k, pos_v, w_ck1, w_ck2, w_cv1, w_cv2, out_g_ssm, out_g_att, w_out, norm2_g, w_grp, b_grp, w_exp, b_exp, w_gate, w_up, w_down):
    depth = norm1_g.shape[0]
    for l in range(depth):
        x = _layer(x, norm1_g[l], w_in[l], lam_re[l], lam_im[l], log_step[l], b_re[l], b_im[l], c_re[l],
                   c_im[l], d_skip[l], w_glu[l], b_glu[l], g_q[l], g_kc[l], g_ks[l], g_kw[l], pos_k[l],
                   pos_v[l], w_ck1[l], w_ck2[l], w_cv1[l], w_cv2[l], out_g_ssm[l], out_g_att[l], w_out[l],
                   norm2_g[l], w_grp[l], b_grp[l], w_exp[l], b_exp[l], w_gate[l], w_up[l], w_down[l])
    return x
```

```python
import functools
import math

import jax
import jax.numpy as jnp
from jax import lax
from jax.experimental import pallas as pl
from jax.experimental.pallas import tpu as pltpu

D_MODEL = 1024
D_SSM = 512
SSM_CH = 16
SSM_GROUPS = D_SSM // SSM_CH
SSM_STATE = 64
D_ATT = 512
HEAD_DIM = 64
N_HEADS = D_ATT // HEAD_DIM
N_KV = 2
HPG = N_HEADS // N_KV
D_KV = N_KV * HEAD_DIM
N_BRANCH = 3
CMP_STRIDE = 16
CMP_BLOCK = 2 * CMP_STRIDE
CMP_HIDDEN = 256
SEL_BLOCK = 64
N_SELECT = 16
WINDOW = 512
N_EXP_GROUPS = 4
EXPERTS_PER_GROUP = 8
N_EXPERTS = N_EXP_GROUPS * EXPERTS_PER_GROUP
D_EXPERT = 256
EPS = 1e-6
NEG = -1e30
FORCE = 1e9

LANES = 128
SUBLANES = 8
SSM_Q = 8
SSM_LT = D_SSM // LANES
ROUTER_OFF = N_EXP_GROUPS
NSA_TQ = 256
NSA_TK = 256
NSA_UNROLL = 8
V_ROWS = HEAD_DIM + 16
MAX_SEL_BLOCKS = 64
MOE_TM = 256
TT_ROWS = D_MODEL // LANES
PROJ_TM = 1024
GATE_ROWS = 16
V7X_VMEM_BYTES = 64 * 1024 * 1024
VMEM_LIMIT = V7X_VMEM_BYTES - 8 * 1024 * 1024

F32 = jnp.float32
BF16 = jnp.bfloat16


def _dot(a, b):
    return jnp.dot(a, b, preferred_element_type=F32)


def _dot_nt(a, b):
    return lax.dot_general(a, b, (((1,), (1,)), ((), ())), preferred_element_type=F32)


def _split_dot(x, w):
    hi = x.astype(BF16)
    lo = (x - hi.astype(F32)).astype(BF16)
    return _dot(hi, w) + _dot(lo, w)


def _gelu(x):
    c = math.sqrt(2.0 / math.pi)
    return 0.5 * x * (1.0 + jnp.tanh(c * (x + 0.044715 * (x * x * x))))


def _sigmoid(x):
    return 1.0 / (1.0 + jnp.exp(-x))


def _params(sem, vmem_limit=VMEM_LIMIT):
    return pltpu.CompilerParams(dimension_semantics=sem, vmem_limit_bytes=vmem_limit)


def _in_proj_kernel(x_ref, g1_ref, wrow_ref, wcol_ref, gq_ref, gks_ref, gkw_ref, bd128_ref,
                    u_ref, qt_ref, kc_ref, vc_ref, ksa_ref, kw_ref, vst_ref, vwt_ref, gate_ref, *, tm, nl):
    x = x_ref[...]
    ms = jnp.mean(x * x, axis=-1, keepdims=True)
    hn = (x * lax.rsqrt(ms + EPS) * g1_ref[...]).astype(BF16)

    pr = _dot(hn, wrow_ref[...])
    u_ref[...] = pr[:, :D_SSM]
    kc, vc, ks, kw = [pr[:, D_SSM + i * D_KV:D_SSM + (i + 1) * D_KV] for i in range(4)]
    kss = _split_dot(ks * ks, bd128_ref[...])
    ksn = ks * lax.rsqrt(kss * (1.0 / HEAD_DIM) + EPS) * gks_ref[...]
    kws = _split_dot(kw * kw, bd128_ref[...])
    kwn = kw * lax.rsqrt(kws * (1.0 / HEAD_DIM) + EPS) * gkw_ref[...]
    t0 = (pl.program_id(0) % nl) * tm
    tpos = t0 + lax.broadcasted_iota(jnp.int32, (tm, MAX_SEL_BLOCKS), 0)
    blk = lax.broadcasted_iota(jnp.int32, (tm, MAX_SEL_BLOCKS), 1)
    onehot = jnp.where(tpos // SEL_BLOCK == blk, 1.0, 0.0).astype(BF16)
    for g in range(N_KV):
        sl = slice(g * HEAD_DIM, (g + 1) * HEAD_DIM)
        kc_ref[g] = kc[:, sl]
        vc_ref[g] = vc[:, sl]
        ksa_ref[g] = jnp.concatenate([ksn[:, sl].astype(BF16), onehot], axis=1)
        kw_ref[g] = kwn[:, sl].astype(BF16)

    pc = _dot_nt(wcol_ref[...], hn)
    gq = gq_ref[...]
    for h in range(N_HEADS):
        sl = slice(h * HEAD_DIM, (h + 1) * HEAD_DIM)
        qh = pc[sl]
        ss = jnp.sum(qh * qh, axis=0, keepdims=True)
        qt_ref[h] = (qh * lax.rsqrt(ss * (1.0 / HEAD_DIM) + EPS) * gq[sl]).astype(BF16)
    ones_rows = jnp.where(lax.broadcasted_iota(jnp.int32, (V_ROWS - HEAD_DIM, tm), 0) == 0, 1.0, 0.0)
    for g in range(N_KV):
        for o_ref, base in ((vst_ref, D_ATT), (vwt_ref, D_ATT + D_KV)):
            vt = jnp.concatenate([pc[base + g * HEAD_DIM:base + (g + 1) * HEAD_DIM], ones_rows], axis=0)
            vt = vt.astype(BF16)
            for j in range(tm // NSA_TK):
                o_ref[g, j] = vt[:, j * NSA_TK:(j + 1) * NSA_TK]
        gb = D_ATT + 2 * D_KV + g * GATE_ROWS
        gate_ref[g] = _sigmoid(pc[gb:gb + GATE_ROWS])


def _in_proj(x2d, g1, wrow, wcol, gq, gks, gkw, bd128, *, bsz, seq):
    tm = PROJ_TM
    nl = seq // tm
    n_tok = bsz * seq
    kern = functools.partial(_in_proj_kernel, tm=tm, nl=nl)
    row = lambda i: (i, 0)
    const = lambda i: (0, 0)
    bgl = lambda i: (i // nl, 0, i % nl, 0)
    n_col = wcol.shape[0]
    jt = tm // NSA_TK

    def kvspec(width):
        return pl.BlockSpec((None, N_KV, tm, width), bgl)

    def kvshape(width, dtype=BF16):
        return jax.ShapeDtypeStruct((bsz, N_KV, seq, width), dtype)

    vt_spec = pl.BlockSpec((None, N_KV, jt, V_ROWS, NSA_TK), lambda i: (i // nl, 0, i % nl, 0, 0))
    vt_shape = jax.ShapeDtypeStruct((bsz, N_KV, seq // NSA_TK, V_ROWS, NSA_TK), BF16)
    return pl.pallas_call(
        kern,
        grid=(n_tok // tm,),
        in_specs=[
            pl.BlockSpec((tm, D_MODEL), row),
            pl.BlockSpec((1, D_MODEL), const),
            pl.BlockSpec((D_MODEL, D_SSM + 4 * D_KV), const),
            pl.BlockSpec((n_col, D_MODEL), const),
            pl.BlockSpec((D_ATT, 1), const),
            pl.BlockSpec((1, D_KV), const),
            pl.BlockSpec((1, D_KV), const),
            pl.BlockSpec((D_KV, D_KV), const),
        ],
        out_specs=[
            pl.BlockSpec((tm, D_SSM), row),
            pl.BlockSpec((None, N_HEADS, HEAD_DIM, tm), lambda i: (i // nl, 0, 0, i % nl)),
            kvspec(HEAD_DIM), kvspec(HEAD_DIM), kvspec(2 * HEAD_DIM), kvspec(HEAD_DIM),
            vt_spec, vt_spec,
            pl.BlockSpec((None, N_KV, GATE_ROWS, tm), lambda i: (i // nl, 0, 0, i % nl)),
        ],
        out_shape=[
            jax.ShapeDtypeStruct((n_tok, D_SSM), F32),
            jax.ShapeDtypeStruct((bsz, N_HEADS, HEAD_DIM, seq), BF16),
            kvshape(HEAD_DIM, F32), kvshape(HEAD_DIM, F32), kvshape(2 * HEAD_DIM), kvshape(HEAD_DIM),
            vt_shape, vt_shape,
            jax.ShapeDtypeStruct((bsz, N_KV, GATE_ROWS, seq), F32),
        ],
        compiler_params=_params(("parallel",)),
        name="in_proj",
    )(x2d, g1, wrow, wcol, gq, gks, gkw, bd128)


def _s5_weights(lam_re, lam_im, log_step, b_re, b_im, c_re, c_im, d_skip, n_sub):
    q = SSM_Q
    lam = lax.complex(lam_re.astype(F32), lam_im.astype(F32))
    step = jnp.exp(log_step.astype(F32))[:, None]
    lam_bar = jnp.exp(lam * step)
    b_bar = ((lam_bar - 1.0) / lam)[..., None] * lax.complex(b_re.astype(F32), b_im.astype(F32))
    c = lax.complex(c_re.astype(F32), c_im.astype(F32))
    pows = [jnp.ones_like(lam_bar)]
    for _ in range(q):
        pows.append(pows[-1] * lam_bar)
    pw = jnp.stack(pows)
    lt, a8 = SSM_LT, LANES // SSM_CH
    hp = a8 * SSM_STATE
    e_lane = (jnp.arange(a8)[:, None] == jnp.arange(LANES)[None, :] // SSM_CH).astype(F32)
    e_state = (jnp.arange(a8)[:, None] == jnp.arange(hp)[None, :] // SSM_STATE).astype(F32)
    e_lane_t = jnp.tile(e_lane, (1, q))
    e_state_t = jnp.tile(e_state, (1, 2))

    kk = jnp.real(jnp.einsum('ghp,kgp,gpi->kghi', c, pw[:q], b_bar))
    km = kk.reshape(q, lt, a8, SSM_CH, SSM_CH).transpose(1, 4, 0, 2, 3).reshape(lt, SSM_CH, q, LANES)
    lag = jnp.arange(q)[None, :] - jnp.arange(q)[:, None]
    kg = km[:, :, jnp.clip(lag, 0, q - 1), :] * (lag >= 0)[None, None, :, :, None].astype(F32)
    kc = kg.transpose(0, 2, 1, 3, 4).reshape(lt, q, 1, SSM_CH, q * LANES)
    t_c = kc.reshape(lt, q * SSM_CH, q * LANES)

    wc = pw[q - 1 - jnp.arange(q)][..., None] * b_bar[None]
    wri = jnp.stack([jnp.real(wc), jnp.imag(wc)])
    wm = (wri.reshape(2, q, lt, a8, SSM_STATE, SSM_CH).transpose(2, 1, 5, 0, 3, 4)
          .reshape(lt, q, 1, SSM_CH, 2 * hp))
    w_c = wm.reshape(lt, q * SSM_CH, 2 * hp)

    cl = c[None] * pw[1:q + 1][:, :, None, :]
    cri = jnp.stack([jnp.real(cl), -jnp.imag(cl)])
    mm = (cri.reshape(2, q, lt, a8, SSM_CH, SSM_STATE).transpose(2, 0, 5, 1, 3, 4)
          .reshape(lt, 2, 1, SSM_STATE, q * LANES))
    m_c = mm.reshape(lt, 2 * SSM_STATE, q * LANES)

    n_lvl = max(1, (n_sub - 1).bit_length())
    lv = [pw[q]]
    for _ in range(n_lvl - 1):
        lv.append(lv[-1] * lv[-1])
    lvs = jnp.stack(lv).reshape(n_lvl, lt, 1, hp)
    pw_re = jnp.real(lvs).transpose(1, 0, 2, 3)
    pw_im = jnp.imag(lvs).transpose(1, 0, 2, 3)
    dvec = jnp.tile(d_skip.astype(F32).reshape(lt, 1, LANES), (1, 1, q))
    return w_c, t_c, m_c, e_state_t, e_lane_t, pw_re, pw_im, dvec


def _s5_kernel(u_ref, wc_ref, tc_ref, mc_ref, es_ref, el_ref, pwr_ref, pwi_ref, d_ref, y_ref, w_ref, t_ref, m_ref,
               *, n_sub, n_lvl):
    half = (LANES // SSM_CH) * SSM_STATE
    q = SSM_Q
    a8 = LANES // SSM_CH

    @pl.when(pl.program_id(1) == 0)
    def _():
        for a in range(a8):
            el = el_ref[a:a + 1, :]
            es = es_ref[a:a + 1, :]
            for s_ in range(q):
                rows = pl.ds((s_ * a8 + a) * SSM_CH, SSM_CH)
                src = pl.ds(s_ * SSM_CH, SSM_CH)
                t_ref[rows, :] = (tc_ref[src, :] * el).astype(BF16)
                w_ref[rows, :] = (wc_ref[src, :] * es).astype(BF16)
            for r in range(2):
                rows = pl.ds((r * a8 + a) * SSM_STATE, SSM_STATE)
                m_ref[rows, :] = (mc_ref[pl.ds(r * SSM_STATE, SSM_STATE), :] * el).astype(BF16)

    u = jnp.concatenate([u_ref[pl.ds(s, n_sub, stride=q), :] for s in range(q)], axis=1)
    ub = u.astype(BF16)
    s_loc = _dot(ub, w_ref[...])
    re = s_loc[:, :half]
    im = s_loc[:, half:]
    rowi = lax.broadcasted_iota(jnp.int32, (n_sub, half), 0)
    for k in range(n_lvl):
        d = 1 << k
        ar = pwr_ref[k]
        ai = pwi_ref[k]
        keep = rowi >= d
        sre = jnp.where(keep, pltpu.roll(re, d, axis=0), 0.0)
        sim = jnp.where(keep, pltpu.roll(im, d, axis=0), 0.0)
        re, im = re + (ar * sre - ai * sim), im + (ar * sim + ai * sre)
    keep = rowi >= 1
    xre = jnp.where(keep, pltpu.roll(re, 1, axis=0), 0.0)
    xim = jnp.where(keep, pltpu.roll(im, 1, axis=0), 0.0)
    xst = jnp.concatenate([xre, xim], axis=1).astype(BF16)
    y = _dot(ub, t_ref[...]) + _dot(xst, m_ref[...]) + d_ref[...] * u
    for j in range(q):
        y_ref[pl.ds(j, n_sub, stride=q), :] = y[:, j * LANES:(j + 1) * LANES]


def _s5(u, w_c, t_c, m_c, e_state_t, e_lane_t, pw_re, pw_im, dvec, *, bsz, n_sub):
    q = SSM_Q
    n_lvl = pw_re.shape[1]
    kern = functools.partial(_s5_kernel, n_sub=n_sub, n_lvl=n_lvl)
    wide = q * LANES
    seq = n_sub * q
    return pl.pallas_call(
        kern,
        grid=(SSM_LT, bsz),
        in_specs=[
            pl.BlockSpec((None, seq, LANES), lambda l, b: (b, 0, l)),
            pl.BlockSpec((None, q * SSM_CH, wide), lambda l, b: (l, 0, 0)),
            pl.BlockSpec((None, q * SSM_CH, wide), lambda l, b: (l, 0, 0)),
            pl.BlockSpec((None, 2 * SSM_STATE, wide), lambda l, b: (l, 0, 0)),
            pl.BlockSpec((LANES // SSM_CH, wide), lambda l, b: (0, 0)),
            pl.BlockSpec((LANES // SSM_CH, wide), lambda l, b: (0, 0)),
            pl.BlockSpec((None, n_lvl, 1, wide // 2), lambda l, b: (l, 0, 0, 0)),
            pl.BlockSpec((None, n_lvl, 1, wide // 2), lambda l, b: (l, 0, 0, 0)),
            pl.BlockSpec((None, 1, wide), lambda l, b: (l, 0, 0)),
        ],
        out_specs=pl.BlockSpec((None, seq, LANES), lambda l, b: (b, 0, l)),
        out_shape=jax.ShapeDtypeStruct((bsz, seq, D_SSM), F32),
        scratch_shapes=[pltpu.VMEM((wide, wide), BF16)] * 3,
        compiler_params=_params(("arbitrary", "arbitrary")),
        name="s5",
    )(u, w_c, t_c, m_c, e_state_t, e_lane_t, pw_re, pw_im, dvec)


def _compress_kernel(kc_ref, vc_ref, w1k_ref, w2k_ref, w1v_ref, w2vt_ref, posk_ref, posv_ref, gkc_ref,
                     kcmp_ref, vcmpt_ref, *, nch):
    half = CMP_STRIDE * HEAD_DIM

    def hidden(x_ref, w1_ref, pos_ref):
        x = jnp.concatenate([x_ref[pl.ds(j, nch, stride=CMP_STRIDE), :] for j in range(CMP_STRIDE)],
                            axis=1).astype(BF16)
        a = _dot(x, w1_ref[:half, :])
        b = _dot(x, w1_ref[half:, :])
        pv = _dot(pos_ref[...], w1_ref[...])[0:1, :]
        hid = a + pltpu.roll(b, nch - 1, axis=0) + pv
        return _gelu(hid).astype(BF16)

    k = _dot(hidden(kc_ref, w1k_ref, posk_ref), w2k_ref[...])
    ms = jnp.mean(k * k, axis=-1, keepdims=True)
    kcmp_ref[...] = (k * lax.rsqrt(ms + EPS) * gkc_ref[...]).astype(BF16)
    vt = _dot_nt(w2vt_ref[...], hidden(vc_ref, w1v_ref, posv_ref))
    coli = lax.broadcasted_iota(jnp.int32, vt.shape, 1)
    vcmpt_ref[...] = jnp.where(coli < nch - 1, vt, 0.0).astype(BF16)


def _compress(kcf, vcf, w1k, w2k, w1v, w2vt, posk, posv, gkc, *, bsz, nch):
    kern = functools.partial(_compress_kernel, nch=nch)
    wide = CMP_STRIDE * HEAD_DIM
    xspec = pl.BlockSpec((None, None, nch * CMP_STRIDE, HEAD_DIM), lambda b, g: (b, g, 0, 0))
    c2 = lambda b, g: (0, 0)
    return pl.pallas_call(
        kern,
        grid=(bsz, N_KV),
        in_specs=[
            xspec, xspec,
            pl.BlockSpec((2 * wide, CMP_HIDDEN), c2), pl.BlockSpec((CMP_HIDDEN, HEAD_DIM), c2),
            pl.BlockSpec((2 * wide, CMP_HIDDEN), c2), pl.BlockSpec((HEAD_DIM, CMP_HIDDEN), c2),
            pl.BlockSpec((SUBLANES, 2 * wide), c2), pl.BlockSpec((SUBLANES, 2 * wide), c2),
            pl.BlockSpec((1, HEAD_DIM), c2),
        ],
        out_specs=[pl.BlockSpec((None, None, nch, HEAD_DIM), lambda b, g: (b, g, 0, 0)),
                   pl.BlockSpec((None, None, HEAD_DIM, nch), lambda b, g: (b, g, 0, 0))],
        out_shape=[jax.ShapeDtypeStruct((bsz, N_KV, nch, HEAD_DIM), BF16),
                   jax.ShapeDtypeStruct((bsz, N_KV, HEAD_DIM, nch), BF16)],
        compiler_params=_params(("parallel", "parallel")),
        name="compress",
    )(kcf, vcf, w1k, w2k, w1v, w2vt, posk, posv, gkc)


def _nsa_kernel(qt_ref, kcmp_ref, vcmpt_ref, ksa_ref, kw_ref, vst_ref, vwt_ref, gate_ref, ovlt_ref, o_ref,
                accs_ref, accw_ref, sel_ref, *, tq, tk, nch):
    qi = pl.program_id(2)
    q0 = qi * tq
    rows = HPG * tq
    qt = jnp.concatenate([qt_ref[h] for h in range(HPG)], axis=1)
    tpos = q0 + lax.broadcasted_iota(jnp.int32, (1, rows), 1) % tq
    krow = lax.broadcasted_iota(jnp.int32, (tk, rows), 0)
    m0 = jnp.full((1, rows), NEG, F32)

    def run_tiles(k_ref, q_all, vt_ref, acc_ref, m, tiles):
        scores = []
        for kt, kind in tiles:
            kc = jnp.maximum(kt, 0) if kind in ('band', 'valid') else kt
            s = _dot(k_ref[pl.ds(pl.multiple_of(kc * tk, tk), tk), :], q_all)
            if kind == 'causal':
                s = jnp.where(kt * tk + krow <= tpos, s, NEG)
            elif kind == 'band':
                s = jnp.where((kt * tk + krow > tpos - WINDOW) & (kt >= 0), s, NEG)
            elif kind == 'valid':
                s = jnp.where(kt >= 0, s, NEG)
            scores.append((kc, s))
        for kc, s in scores:
            m_new = jnp.maximum(m, jnp.max(s, axis=0, keepdims=True))
            alpha = jnp.exp2(m - m_new)
            p = jnp.exp2(s - m_new).astype(BF16)
            acc_ref[...] = alpha * acc_ref[...] + _dot(vt_ref[kc], p)
            m = m_new
        return m

    accw_ref[...] = jnp.zeros_like(accw_ref)
    n_win = WINDOW // tk
    run_tiles(kw_ref, qt, vwt_ref, accw_ref, m0,
              [(qi - n_win, 'band')] + [(qi - n_win + t, 'valid') for t in range(1, n_win)] + [(qi, 'causal')])

    s = _dot(kcmp_ref[...], qt)
    cend = lax.broadcasted_iota(jnp.int32, (nch, rows), 0) * CMP_STRIDE + (CMP_BLOCK - 1)
    s = jnp.where(cend <= tpos, s, NEG)
    m = jnp.max(s, axis=0, keepdims=True)
    p = jnp.exp2(s - m)
    p = p * jnp.where(tpos >= CMP_BLOCK - 1, 1.0 / jnp.sum(p, axis=0, keepdims=True), 0.0)
    o_cmp = _dot(vcmpt_ref[...], p.astype(BF16))

    psum = p[:, 0:tq]
    for h in range(1, HPG):
        psum = psum + p[:, h * tq:(h + 1) * tq]
    hi = psum.astype(BF16)
    lo = (psum - hi.astype(F32)).astype(BF16)
    ovlt = ovlt_ref[...]
    imp = _dot(ovlt, hi) + _dot(ovlt, lo)
    nb = MAX_SEL_BLOCKS
    blk = lax.broadcasted_iota(jnp.int32, (nb, tq), 0)
    cur = (q0 + lax.broadcasted_iota(jnp.int32, (nb, tq), 1)) // SEL_BLOCK
    forced = (blk == 0) | (blk == cur) | (blk == cur - 1)
    imp = jnp.where(forced, FORCE, jnp.where(blk <= cur, imp, NEG))
    sub = SUBLANES
    rowl = lax.broadcasted_iota(jnp.int32, (sub, tq), 0)
    n_seen = (q0 + tq - 1) // SEL_BLOCK + 1
    sel_ref[...] = jnp.zeros_like(sel_ref)
    for nbv in range(2 * N_SELECT, nb + 1, N_SELECT):
        @pl.when((n_seen > nbv - N_SELECT) & (n_seen <= nbv))
        def _(nbv=nbv):
            groups = [imp[r:r + sub] for r in range(0, nbv, sub)]
            ranks = [jnp.zeros((sub, tq), F32) for _ in groups]
            for i in range(nbv):
                ri = jnp.broadcast_to(imp[i:i + 1, :], (sub, tq))
                for gi, x in enumerate(groups):
                    if i < gi * sub:
                        ahead = ri >= x
                    elif i >= (gi + 1) * sub:
                        ahead = ri > x
                    else:
                        ahead = (ri > x) | ((ri == x) & (rowl > i - gi * sub))
                    ranks[gi] = ranks[gi] + jnp.where(ahead, 1.0, 0.0)
            rank = jnp.concatenate(ranks, axis=0)
            sel_ref[0:nbv, :] = jnp.where(rank < N_SELECT, 0.0, NEG).astype(BF16)
    sel = sel_ref[...]
    qa = jnp.concatenate([qt, jnp.concatenate([sel] * HPG, axis=1)], axis=0)

    accs_ref[...] = jnp.zeros_like(accs_ref)
    nu = NSA_UNROLL

    def sel_multi(j, m):
        return run_tiles(ksa_ref, qa, vst_ref, accs_ref, m, [(nu * j + t, None) for t in range(nu)])

    m_sel = lax.fori_loop(0, qi // nu, sel_multi, m0)
    for r in range(nu):
        @pl.when(qi % nu == r)
        def _(r=r):
            base = qi - r
            run_tiles(ksa_ref, qa, vst_ref, accs_ref, m_sel,
                      [(base + t, None) for t in range(r)] + [(qi, 'causal')])

    accs = accs_ref[...]
    accw = accw_ref[...]
    o_sel = accs[:HEAD_DIM] * (1.0 / accs[HEAD_DIM:HEAD_DIM + 1])
    o_win = accw[:HEAD_DIM] * (1.0 / accw[HEAD_DIM:HEAD_DIM + 1])
    gt = gate_ref[...]
    for h in range(HPG):
        c = slice(h * tq, (h + 1) * tq)
        o_ref[h * HEAD_DIM:(h + 1) * HEAD_DIM, :] = (
            gt[3 * h:3 * h + 1] * o_cmp[:, c] + gt[3 * h + 1:3 * h + 2] * o_sel[:, c]
            + gt[3 * h + 2:3 * h + 3] * o_win[:, c])


def _nsa(qt, kcmp, vcmpt, ksa, kw, vst, vwt, gate, ovlt, *, bsz, seq, nch):
    tq, tk = NSA_TQ, NSA_TK
    assert tq == tk and WINDOW % tk == 0
    nq = seq // tq
    kern = functools.partial(_nsa_kernel, tq=tq, tk=tk, nch=nch)
    full = lambda b, g, i: (b, g, 0, 0)
    full5 = lambda b, g, i: (b, g, 0, 0, 0)
    qd = HPG * HEAD_DIM
    rows = HPG * tq
    return pl.pallas_call(
        kern,
        grid=(bsz, N_KV, nq),
        in_specs=[
            pl.BlockSpec((None, HPG, HEAD_DIM, tq), lambda b, g, i: (b, g, 0, i)),
            pl.BlockSpec((None, None, nch, HEAD_DIM), full),
            pl.BlockSpec((None, None, HEAD_DIM, nch), full),
            pl.BlockSpec((None, None, seq, 2 * HEAD_DIM), full),
            pl.BlockSpec((None, None, seq, HEAD_DIM), full),
            pl.BlockSpec((None, None, seq // tk, V_ROWS, tk), full5),
            pl.BlockSpec((None, None, seq // tk, V_ROWS, tk), full5),
            pl.BlockSpec((None, None, GATE_ROWS, tq), lambda b, g, i: (b, g, 0, i)),
            pl.BlockSpec((MAX_SEL_BLOCKS, nch), lambda b, g, i: (0, 0)),
        ],
        out_specs=pl.BlockSpec((None, qd, tq), lambda b, g, i: (b, g, i)),
        out_shape=jax.ShapeDtypeStruct((bsz, D_ATT, seq), F32),
        scratch_shapes=[pltpu.VMEM((V_ROWS, rows), F32), pltpu.VMEM((V_ROWS, rows), F32),
                        pltpu.VMEM((MAX_SEL_BLOCKS, tq), BF16)],
        compiler_params=_params(("parallel", "parallel", "arbitrary")),
        name="nsa",
    )(qt, kcmp, vcmpt, ksa, kw, vst, vwt, gate, ovlt)


def _route(logits, group=None):
    lane = lax.broadcasted_iota(jnp.int32, logits.shape, 1).astype(F32)
    far = float(LANES)
    is_g = lane < N_EXP_GROUPS
    glog = jnp.where(is_g, logits, -jnp.inf)
    gmax = jnp.max(glog, axis=1, keepdims=True)
    gsum = jnp.sum(jnp.where(is_g, jnp.exp(logits - gmax), 0.0), axis=1, keepdims=True)
    gsel = jnp.min(jnp.where(glog == gmax, lane, far), axis=1, keepdims=True)
    gprob = 1.0 / gsum
    lo = ROUTER_OFF + EXPERTS_PER_GROUP * (gsel if group is None else group)
    in_e = (lane >= lo) & (lane < lo + EXPERTS_PER_GROUP)
    emax = jnp.max(jnp.where(in_e, logits, -jnp.inf), axis=1, keepdims=True)
    eexp = jnp.where(in_e, jnp.exp(logits - emax), 0.0)
    eprob = jnp.where(in_e, eexp / jnp.sum(eexp, axis=1, keepdims=True), -1.0)
    v1 = jnp.max(eprob, axis=1, keepdims=True)
    i1 = jnp.min(jnp.where(eprob == v1, lane, far), axis=1, keepdims=True)
    rest = jnp.where(lane == i1, -1.0, eprob)
    v2 = jnp.max(rest, axis=1, keepdims=True)
    i2 = jnp.min(jnp.where(rest == v2, lane, far), axis=1, keepdims=True)
    den = v1 + v2
    comb = jnp.where(lane == i1, v1 / den * gprob, 0.0) + jnp.where(lane == i2, v2 / den * gprob, 0.0)
    return gsel, comb


def _out_proj_kernel(tm, ys_ref, yat_ref, x_ref, wglu_ref, bglu_ref, gs_ref, ga_ref, wo_ref, g2_ref, wr_ref, br_ref,
                     xt_ref, gsel_ref):
    y = _gelu(ys_ref[...])
    y = y * _sigmoid(_dot(y.astype(BF16), wglu_ref[...]) + bglu_ref[...])
    ysn = y * lax.rsqrt(jnp.mean(y * y, axis=-1, keepdims=True) + EPS) * gs_ref[...]
    yat = yat_ref[...]
    yant = yat * lax.rsqrt(jnp.mean(yat * yat, axis=0, keepdims=True) + EPS) * ga_ref[...]
    yan = yant.T
    x2 = x_ref[...] + _dot(ysn.astype(BF16), wo_ref[:D_SSM, :]) + _dot(yan.astype(BF16), wo_ref[D_SSM:, :])
    for c in range(TT_ROWS):
        xt_ref[pl.ds(c, tm, stride=TT_ROWS), :] = x2[:, c * LANES:(c + 1) * LANES]
    h2 = (x2 * lax.rsqrt(jnp.mean(x2 * x2, axis=-1, keepdims=True) + EPS) * g2_ref[...]).astype(BF16)
    gsel, _ = _route(_dot(h2, wr_ref[...]) + br_ref[...])
    gsel_ref[...] = jnp.broadcast_to(gsel, gsel_ref.shape)


def _out_proj(ys, yat, x2d, wglu, bglu, gs, ga, wo, g2, wr, br, *, seq):
    n_tok = x2d.shape[0]
    tm = PROJ_TM
    nl = seq // tm
    row = lambda i: (i, 0)
    const = lambda i: (0, 0)
    return pl.pallas_call(
        functools.partial(_out_proj_kernel, tm),
        grid=(n_tok // tm,),
        in_specs=[
            pl.BlockSpec((tm, D_SSM), row),
            pl.BlockSpec((None, D_ATT, tm), lambda i: (i // nl, 0, i % nl)),
            pl.BlockSpec((tm, D_MODEL), row),
            pl.BlockSpec((D_SSM, D_SSM), const),
            pl.BlockSpec((1, D_SSM), const),
            pl.BlockSpec((1, D_SSM), const),
            pl.BlockSpec((D_ATT, 1), const),
            pl.BlockSpec((D_SSM + D_ATT, D_MODEL), const),
            pl.BlockSpec((1, D_MODEL), const),
            pl.BlockSpec((D_MODEL, LANES), const),
            pl.BlockSpec((1, LANES), const),
        ],
        out_specs=[pl.BlockSpec((tm * TT_ROWS, LANES), row), pl.BlockSpec((tm, LANES), row)],
        out_shape=[jax.ShapeDtypeStruct((n_tok * TT_ROWS, LANES), F32), jax.ShapeDtypeStruct((n_tok, LANES), F32)],
        compiler_params=_params(("parallel",)),
        name="out_proj",
    )(ys, yat, x2d, wglu, bglu, gs, ga, wo, g2, wr, br)


def _moe_plan(gsel, n_tok):
    tmx = MOE_TM
    n_tiles = n_tok // tmx + N_EXP_GROUPS
    oh = (gsel[:, None] == jnp.arange(N_EXP_GROUPS)[None, :]).astype(jnp.int32)
    csum = jnp.cumsum(oh, axis=0)
    counts = csum[-1]
    rank = jnp.sum(csum * oh, axis=1) - 1
    nt = (counts + tmx - 1) // tmx
    tend = jnp.cumsum(nt)
    toff = tend - nt
    pos = jnp.sum(oh * toff[None, :], axis=1) * tmx + rank
    tile = jnp.arange(n_tiles)
    grp = jnp.minimum(jnp.sum((tile[:, None] >= tend[None, :]).astype(jnp.int32), axis=1), N_EXP_GROUPS - 1)
    nval = jnp.clip(counts[grp] - (tile - toff[grp]) * tmx, 0, tmx)
    return grp.astype(jnp.int32), nval.astype(jnp.int32), pos.astype(jnp.int32)


def _moe_kernel(grp_ref, nval_ref, pos_ref, x_hbm, g2_ref, wr_ref, br_ref, wg_ref, wu_ref, wd_ref, o_hbm,
                xbuf, obuf, abuf, tok_ref, gsem, ssem, *, tmx, n_tiles, n_tok):
    i = pl.program_id(0)
    slot = i % 2

    @pl.when(i == 0)
    def _():
        def place(t, c):
            tok_ref[pos_ref[t]] = t
            return c

        lax.fori_loop(0, n_tok, place, 0, unroll=8)

        def pad_tile(tile, c):
            def pad_row(r, c2):
                tok_ref[tile * tmx + r] = 0
                return c2

            return lax.fori_loop(nval_ref[tile], tmx, pad_row, c)

        lax.fori_loop(0, n_tiles, pad_tile, 0)

    def gather_row(tile, r, dst_slot):
        t = tok_ref[tile * tmx + r]
        return pltpu.make_async_copy(x_hbm.at[pl.ds(pl.multiple_of(t * TT_ROWS, TT_ROWS), TT_ROWS), :],
                                     xbuf.at[dst_slot, pl.ds(r * TT_ROWS, TT_ROWS), :], gsem.at[dst_slot])

    def scatter_row(tile, r, src_slot):
        t = tok_ref[tile * tmx + r]
        return pltpu.make_async_copy(obuf.at[src_slot, pl.ds(r, 1), :], o_hbm.at[pl.ds(t, 1), :], ssem.at[src_slot])

    def gather_wait(src_slot):
        pltpu.make_async_copy(x_hbm.at[pl.ds(0, tmx * TT_ROWS), :], xbuf.at[src_slot], gsem.at[src_slot]).wait()

    def scatter_wait(n, src_slot):
        n8 = pl.multiple_of((n // SUBLANES) * SUBLANES, SUBLANES)

        @pl.when(n8 > 0)
        def _():
            pltpu.make_async_copy(obuf.at[src_slot, pl.ds(0, n8), :], o_hbm.at[pl.ds(0, n8), :],
                                  ssem.at[src_slot]).wait()

        def one(r, c):
            pltpu.make_async_copy(obuf.at[src_slot, pl.ds(0, 1), :], o_hbm.at[pl.ds(0, 1), :],
                                  ssem.at[src_slot]).wait()
            return c

        lax.fori_loop(0, n - n8, one, 0)

    @pl.when(i == 0)
    def _():
        def body(r, c):
            gather_row(0, r, 0).start()
            return c

        lax.fori_loop(0, tmx, body, 0)

    nv = nval_ref[i]
    nv_prev = nval_ref[jnp.maximum(i - 1, 0)]

    @pl.when((i == 0) | (nv_prev > 0))
    def _():
        gather_wait(slot)

    @pl.when(i >= 2)
    def _():
        scatter_wait(nval_ref[i - 2], slot)

    nxt = jnp.minimum(i + 1, n_tiles - 1)
    per = tmx // EXPERTS_PER_GROUP
    prev_full = (i >= 1) & (nv_prev == tmx)

    def experts(with_scatter):
        x2 = jnp.concatenate([xbuf[slot, pl.ds(c, tmx, stride=TT_ROWS), :] for c in range(TT_ROWS)], axis=1)
        h = (x2 * lax.rsqrt(jnp.mean(x2 * x2, axis=-1, keepdims=True) + EPS) * g2_ref[...]).astype(BF16)
        _, cw = _route(_dot(h, wr_ref[...]) + br_ref[...], group=grp_ref[i].astype(F32))
        lane = lax.broadcasted_iota(jnp.int32, cw.shape, 1)
        first = ROUTER_OFF + EXPERTS_PER_GROUP * grp_ref[i]
        for k in range(EXPERTS_PER_GROUP):
            for r in range(k * per, (k + 1) * per):
                gather_row(nxt, r, 1 - slot).start()
                if with_scatter:
                    scatter_row(i - 1, r, 1 - slot).start()
            gate = _dot(h, wg_ref[k].astype(BF16))
            up = _dot(h, wu_ref[k].astype(BF16))
            ck = jnp.sum(jnp.where(lane == first + k, cw, 0.0), axis=1, keepdims=True)
            abuf[:, k * D_EXPERT:(k + 1) * D_EXPERT] = (gate * _sigmoid(gate) * up * ck).astype(BF16)
        obuf[slot] = x2 + _dot(abuf[...], wd_ref[...].astype(BF16))

    @pl.when(prev_full & (nv > 0))
    def _():
        experts(True)

    @pl.when(jnp.logical_not(prev_full) & (nv > 0))
    def _():
        experts(False)

    @pl.when(prev_full & (nv == 0))
    def _():
        def body(r, c):
            scatter_row(i - 1, r, 1 - slot).start()
            return c

        lax.fori_loop(0, tmx, body, 0)

    @pl.when(nv < tmx)
    def _():
        def body(r, c):
            scatter_row(i, r, slot).start()
            return c

        lax.fori_loop(0, nv, body, 0)

    @pl.when(i == n_tiles - 1)
    def _():
        @pl.when(nv == tmx)
        def _():
            def body(r, c):
                scatter_row(i, r, slot).start()
                return c

            lax.fori_loop(0, tmx, body, 0)

        @pl.when(nv > 0)
        def _():
            gather_wait(1 - slot)

        scatter_wait(nv_prev, 1 - slot)
        scatter_wait(nv, slot)


def _moe(x2t, grp, nval, pos, g2, wr, br, wg, wu, wd):
    n_tok = x2t.shape[0] // TT_ROWS
    tmx = MOE_TM
    n_tiles = grp.shape[0]
    kern = functools.partial(_moe_kernel, tmx=tmx, n_tiles=n_tiles, n_tok=n_tok)
    gk = EXPERTS_PER_GROUP * D_EXPERT
    w_bytes = 2 * 3 * EXPERTS_PER_GROUP * D_MODEL * D_EXPERT * 4
    io_bytes = 2 * tmx * (TT_ROWS * LANES + D_MODEL) * 4 + tmx * gk * 2
    vmem_limit = w_bytes + io_bytes + 6 * 1024 * 1024
    assert vmem_limit < V7X_VMEM_BYTES
    grid_spec = pltpu.PrefetchScalarGridSpec(
        num_scalar_prefetch=3,
        grid=(n_tiles,),
        in_specs=[
            pl.BlockSpec(memory_space=pl.ANY),
            pl.BlockSpec((1, D_MODEL), lambda i, g, n, t: (0, 0)),
            pl.BlockSpec((D_MODEL, LANES), lambda i, g, n, t: (0, 0)),
            pl.BlockSpec((1, LANES), lambda i, g, n, t: (0, 0)),
            pl.BlockSpec((None, EXPERTS_PER_GROUP, D_MODEL, D_EXPERT), lambda i, g, n, t: (g[i], 0, 0, 0)),
            pl.BlockSpec((None, EXPERTS_PER_GROUP, D_MODEL, D_EXPERT), lambda i, g, n, t: (g[i], 0, 0, 0)),
            pl.BlockSpec((None, gk, D_MODEL), lambda i, g, n, t: (g[i], 0, 0)),
        ],
        out_specs=pl.BlockSpec(memory_space=pl.ANY),
        scratch_shapes=[
            pltpu.VMEM((2, tmx * TT_ROWS, LANES), F32),
            pltpu.VMEM((2, tmx, D_MODEL), F32),
            pltpu.VMEM((tmx, gk), BF16),
            pltpu.SMEM((n_tiles * tmx,), jnp.int32),
            pltpu.SemaphoreType.DMA((2,)),
            pltpu.SemaphoreType.DMA((2,)),
        ],
    )
    return pl.pallas_call(
        kern,
        grid_spec=grid_spec,
        out_shape=jax.ShapeDtypeStruct((n_tok, D_MODEL), F32),
        compiler_params=_params(("arbitrary",), vmem_limit),
        name="moe",
    )(grp, nval, pos, x2t, g2, wr, br, wg, wu, wd)


def _block_diag_ones(n, blk):
    i = jnp.arange(n) // blk
    return (i[:, None] == i[None, :]).astype(BF16)


def _layer(x, norm1_g, w_in, lam_re, lam_im, log_step, b_re, b_im, c_re, c_im, d_skip,
           w_glu, b_glu, g_q, g_kc, g_ks, g_kw, pos_k, pos_v, w_ck1, w_ck2, w_cv1, w_cv2,
           out_g_ssm, out_g_att, w_out, norm2_g, w_grp, b_grp, w_exp, b_exp, w_gate, w_up, w_down):
    bsz, seq, _ = x.shape
    assert seq % PROJ_TM == 0 and seq // SEL_BLOCK <= MAX_SEL_BLOCKS
    n_tok = bsz * seq
    x2d = x.reshape(n_tok, D_MODEL)
    q8 = SSM_Q
    n_sub = seq // q8
    nch = seq // CMP_STRIDE

    o_q = D_SSM
    o_kv = D_SSM + D_ATT
    o_gt = o_kv + 6 * D_KV
    kv = lambda i: w_in[:, o_kv + i * D_KV:o_kv + (i + 1) * D_KV]
    wrow = jnp.concatenate([w_in[:, :o_q], kv(0), kv(1), kv(2), kv(4)], axis=1).astype(BF16)
    per_g = HPG * N_BRANCH
    wgt = jnp.zeros((D_MODEL, N_KV * GATE_ROWS), F32)
    for g in range(N_KV):
        wgt = wgt.at[:, g * GATE_ROWS:g * GATE_ROWS + per_g].set(w_in[:, o_gt + g * per_g:o_gt + (g + 1) * per_g])
    wcol = jnp.concatenate([w_in[:, o_q:o_kv], kv(3), kv(5), wgt], axis=1).T.astype(BF16)
    qscale = (HEAD_DIM ** -0.5) * math.log2(math.e)
    gq = (jnp.tile(g_q.astype(F32), N_HEADS) * qscale).reshape(D_ATT, 1)
    gks = jnp.tile(g_ks.astype(F32), N_KV).reshape(1, D_KV)
    gkw = jnp.tile(g_kw.astype(F32), N_KV).reshape(1, D_KV)

    u, qt, kc, vc, ksa, kw, vst, vwt, gate = _in_proj(
        x2d, norm1_g.reshape(1, D_MODEL), wrow, wcol, gq, gks, gkw,
        _block_diag_ones(D_KV, HEAD_DIM), bsz=bsz, seq=seq)

    w_c, t_c, m_c, e_state_t, e_lane_t, pw_re, pw_im, dvec = _s5_weights(
        lam_re, lam_im, log_step, b_re, b_im, c_re, c_im, d_skip, n_sub)
    ys = _s5(u.reshape(bsz, seq, D_SSM), w_c, t_c, m_c, e_state_t, e_lane_t, pw_re, pw_im, dvec,
             bsz=bsz, n_sub=n_sub).reshape(n_tok, D_SSM)

    wide = CMP_STRIDE * HEAD_DIM
    pad8 = lambda p: jnp.zeros((SUBLANES, 2 * wide), F32).at[0].set(p.reshape(-1)).astype(BF16)
    kcmp, vcmpt = _compress(
        kc, vc,
        w_ck1.astype(BF16), w_ck2.astype(BF16), w_cv1.astype(BF16), w_cv2.T.astype(BF16),
        pad8(pos_k), pad8(pos_v), g_kc.astype(F32).reshape(1, HEAD_DIM), bsz=bsz, nch=nch)
    cstart = jnp.arange(nch) * CMP_STRIDE
    sstart = jnp.arange(MAX_SEL_BLOCKS) * SEL_BLOCK
    ovlt = ((cstart[None, :] < sstart[:, None] + SEL_BLOCK) & (cstart[None, :] + CMP_BLOCK > sstart[:, None])
            & (jnp.arange(MAX_SEL_BLOCKS)[:, None] < seq // SEL_BLOCK)
            & (jnp.arange(nch)[None, :] < nch - 1)).astype(BF16)
    yat = _nsa(qt, kcmp, vcmpt, ksa, kw, vst, vwt, gate, ovlt, bsz=bsz, seq=seq, nch=nch)

    wr = jnp.zeros((D_MODEL, LANES), F32)
    wr = wr.at[:, :N_EXP_GROUPS].set(w_grp).at[:, ROUTER_OFF:ROUTER_OFF + N_EXPERTS].set(w_exp).astype(BF16)
    br = jnp.zeros((1, LANES), F32)
    br = br.at[0, :N_EXP_GROUPS].set(b_grp).at[0, ROUTER_OFF:ROUTER_OFF + N_EXPERTS].set(b_exp)
    g2 = norm2_g.reshape(1, D_MODEL).astype(F32)
    x2t, gsel = _out_proj(
        ys, yat, x2d, w_glu.astype(BF16), b_glu.reshape(1, D_SSM).astype(F32),
        out_g_ssm.reshape(1, D_SSM).astype(F32), out_g_att.reshape(D_ATT, 1).astype(F32),
        w_out.astype(BF16), g2, wr, br, seq=seq)

    grp, nval, pos = _moe_plan(gsel[:, 0].astype(jnp.int32), n_tok)
    gshape = (N_EXP_GROUPS, EXPERTS_PER_GROUP, D_MODEL, D_EXPERT)
    out = _moe(x2t, grp, nval, pos, g2, wr, br, w_gate.reshape(gshape), w_up.reshape(gshape),
               w_down.reshape(N_EXP_GROUPS, EXPERTS_PER_GROUP * D_EXPERT, D_MODEL))
    return out.reshape(bsz, seq, D_MODEL)


def kernel(x, norm1_g, w_in, lam_re, lam_im, log_step, b_re, b_im, c_re, c_im, d_skip, w_glu, b_glu, g_q, g_kc, g_ks, g_kw, pos_k, pos_v, w_ck1, w_ck2, w_cv1, w_cv2, out_g_ssm, out_g_att, w_out, norm2_g, w_grp, b_grp, w_exp, b_exp, w_gate, w_up, w_down):
    depth = norm1_g.shape[0]
    for l in range(depth):
        x = _layer(x, norm1_g[l], w_in[l], lam_re[l], lam_im[l], log_step[l], b_re[l], b_im[l], c_re[l],
                   c_im[l], d_skip[l], w_glu[l], b_glu[l], g_q[l], g_kc[l], g_ks[l], g_kw[l], pos_k[l],
                   pos_v[l], w_ck1[l], w_ck2[l], w_cv1[l], w_cv2[l], out_g_ssm[l], out_g_att[l], w_out[l],
                   norm2_g[l], w_grp[l], b_grp[l], w_exp[l], b_exp[l], w_gate[l], w_up[l], w_down[l])
    return x
```

```python
import functools
import math

import jax
import jax.numpy as jnp
from jax import lax
from jax.experimental import pallas as pl
from jax.experimental.pallas import tpu as pltpu

D_MODEL = 1024
D_SSM = 512
SSM_CH = 16
SSM_STATE = 64
D_ATT = 512
HEAD_DIM = 64
N_HEADS = D_ATT // HEAD_DIM
N_KV = 2
HPG = N_HEADS // N_KV
D_KV = N_KV * HEAD_DIM
N_BRANCH = 3
CMP_STRIDE = 16
CMP_BLOCK = 2 * CMP_STRIDE
CMP_HIDDEN = 256
SEL_BLOCK = 64
N_SELECT = 16
WINDOW = 512
N_EXP_GROUPS = 4
EXPERTS_PER_GROUP = 8
N_EXPERTS = N_EXP_GROUPS * EXPERTS_PER_GROUP
D_EXPERT = 256
EPS = 1e-6
NEG = -1e30
FORCE = 1e9

LANES = 128
SUBLANES = 8
SSM_Q = 8
SSM_LT = D_SSM // LANES
ROUTER_OFF = N_EXP_GROUPS
NSA_TQ = 256
NSA_TK = 256
NSA_UNROLL = 8
V_ROWS = HEAD_DIM + 16
MAX_SEL_BLOCKS = 64
MOE_TM = 256
TT_ROWS = D_MODEL // LANES
PROJ_TM = 1024
GATE_ROWS = 16
V7X_VMEM_BYTES = 64 * 1024 * 1024
VMEM_LIMIT = V7X_VMEM_BYTES - 8 * 1024 * 1024

F32 = jnp.float32
BF16 = jnp.bfloat16


def _dot(a, b):
    return jnp.dot(a, b, preferred_element_type=F32)


def _dot_nt(a, b):
    return lax.dot_general(a, b, (((1,), (1,)), ((), ())), preferred_element_type=F32)


def _split_dot(x, w):
    hi = x.astype(BF16)
    lo = (x - hi.astype(F32)).astype(BF16)
    return _dot(hi, w) + _dot(lo, w)


def _gelu(x):
    c = math.sqrt(2.0 / math.pi)
    return 0.5 * x * (1.0 + jnp.tanh(c * (x + 0.044715 * (x * x * x))))


def _sigmoid(x):
    return 1.0 / (1.0 + jnp.exp(-x))


def _params(sem, vmem_limit=VMEM_LIMIT):
    return pltpu.CompilerParams(dimension_semantics=sem, vmem_limit_bytes=vmem_limit)


def _in_proj_kernel(x_ref, g1_ref, wrow_ref, wcol_ref, gq_ref, gks_ref, gkw_ref, bd128_ref,
                    u_ref, qt_ref, kc_ref, vc_ref, ksa_ref, kw_ref, vst_ref, vwt_ref, gate_ref, *, tm, nl):
    x = x_ref[...]
    ms = jnp.mean(x * x, axis=-1, keepdims=True)
    hn = (x * lax.rsqrt(ms + EPS) * g1_ref[...]).astype(BF16)

    pr = _dot(hn, wrow_ref[...])
    u_ref[...] = pr[:, :D_SSM]
    kc, vc, ks, kw = [pr[:, D_SSM + i * D_KV:D_SSM + (i + 1) * D_KV] for i in range(4)]
    kss = _split_dot(ks * ks, bd128_ref[...])
    ksn = ks * lax.rsqrt(kss * (1.0 / HEAD_DIM) + EPS) * gks_ref[...]
    kws = _split_dot(kw * kw, bd128_ref[...])
    kwn = kw * lax.rsqrt(kws * (1.0 / HEAD_DIM) + EPS) * gkw_ref[...]
    t0 = (pl.program_id(0) % nl) * tm
    tpos = t0 + lax.broadcasted_iota(jnp.int32, (tm, MAX_SEL_BLOCKS), 0)
    blk = lax.broadcasted_iota(jnp.int32, (tm, MAX_SEL_BLOCKS), 1)
    onehot = jnp.where(tpos // SEL_BLOCK == blk, 1.0, 0.0).astype(BF16)
    for g in range(N_KV):
        sl = slice(g * HEAD_DIM, (g + 1) * HEAD_DIM)
        kc_ref[g] = kc[:, sl]
        vc_ref[g] = vc[:, sl]
        ksa_ref[g] = jnp.concatenate([ksn[:, sl].astype(BF16), onehot], axis=1)
        kw_ref[g] = kwn[:, sl].astype(BF16)

    pc = _dot_nt(wcol_ref[...], hn)
    gq = gq_ref[...]
    for h in range(N_HEADS):
        sl = slice(h * HEAD_DIM, (h + 1) * HEAD_DIM)
        qh = pc[sl]
        ss = jnp.sum(qh * qh, axis=0, keepdims=True)
        qt_ref[h] = (qh * lax.rsqrt(ss * (1.0 / HEAD_DIM) + EPS) * gq[sl]).astype(BF16)
    ones_rows = jnp.where(lax.broadcasted_iota(jnp.int32, (V_ROWS - HEAD_DIM, tm), 0) == 0, 1.0, 0.0)
    for g in range(N_KV):
        for o_ref, base in ((vst_ref, D_ATT), (vwt_ref, D_ATT + D_KV)):
            vt = jnp.concatenate([pc[base + g * HEAD_DIM:base + (g + 1) * HEAD_DIM], ones_rows], axis=0)
            vt = vt.astype(BF16)
            for j in range(tm // NSA_TK):
                o_ref[g, j] = vt[:, j * NSA_TK:(j + 1) * NSA_TK]
        gb = D_ATT + 2 * D_KV + g * GATE_ROWS
        gate_ref[g] = _sigmoid(pc[gb:gb + GATE_ROWS])


def _in_proj(x2d, g1, wrow, wcol, gq, gks, gkw, bd128, *, bsz, seq):
    tm = PROJ_TM
    nl = seq // tm
    n_tok = bsz * seq
    kern = functools.partial(_in_proj_kernel, tm=tm, nl=nl)
    row = lambda i: (i, 0)
    const = lambda i: (0, 0)
    bgl = lambda i: (i // nl, 0, i % nl, 0)
    n_col = wcol.shape[0]
    jt = tm // NSA_TK

    def kvspec(width):
        return pl.BlockSpec((None, N_KV, tm, width), bgl)

    def kvshape(width, dtype=BF16):
        return jax.ShapeDtypeStruct((bsz, N_KV, seq, width), dtype)

    vt_spec = pl.BlockSpec((None, N_KV, jt, V_ROWS, NSA_TK), lambda i: (i // nl, 0, i % nl, 0, 0))
    vt_shape = jax.ShapeDtypeStruct((bsz, N_KV, seq // NSA_TK, V_ROWS, NSA_TK), BF16)
    return pl.pallas_call(
        kern,
        grid=(n_tok // tm,),
        in_specs=[
            pl.BlockSpec((tm, D_MODEL), row),
            pl.BlockSpec((1, D_MODEL), const),
            pl.BlockSpec((D_MODEL, D_SSM + 4 * D_KV), const),
            pl.BlockSpec((n_col, D_MODEL), const),
            pl.BlockSpec((D_ATT, 1), const),
            pl.BlockSpec((1, D_KV), const),
            pl.BlockSpec((1, D_KV), const),
            pl.BlockSpec((D_KV, D_KV), const),
        ],
        out_specs=[
            pl.BlockSpec((tm, D_SSM), row),
            pl.BlockSpec((None, N_HEADS, HEAD_DIM, tm), lambda i: (i // nl, 0, 0, i % nl)),
            kvspec(HEAD_DIM), kvspec(HEAD_DIM), kvspec(2 * HEAD_DIM), kvspec(HEAD_DIM),
            vt_spec, vt_spec,
            pl.BlockSpec((None, N_KV, GATE_ROWS, tm), lambda i: (i // nl, 0, 0, i % nl)),
        ],
        out_shape=[
            jax.ShapeDtypeStruct((n_tok, D_SSM), F32),
            jax.ShapeDtypeStruct((bsz, N_HEADS, HEAD_DIM, seq), BF16),
            kvshape(HEAD_DIM, F32), kvshape(HEAD_DIM, F32), kvshape(2 * HEAD_DIM), kvshape(HEAD_DIM),
            vt_shape, vt_shape,
            jax.ShapeDtypeStruct((bsz, N_KV, GATE_ROWS, seq), F32),
        ],
        compiler_params=_params(("parallel",)),
        name="in_proj",
    )(x2d, g1, wrow, wcol, gq, gks, gkw, bd128)


def _s5_weights(lam_re, lam_im, log_step, b_re, b_im, c_re, c_im, d_skip, n_sub):
    q = SSM_Q
    lam = lax.complex(lam_re.astype(F32), lam_im.astype(F32))
    step = jnp.exp(log_step.astype(F32))[:, None]
    lam_bar = jnp.exp(lam * step)
    b_bar = ((lam_bar - 1.0) / lam)[..., None] * lax.complex(b_re.astype(F32), b_im.astype(F32))
    c = lax.complex(c_re.astype(F32), c_im.astype(F32))
    pows = [jnp.ones_like(lam_bar)]
    for _ in range(q):
        pows.append(pows[-1] * lam_bar)
    pw = jnp.stack(pows)
    lt, a8 = SSM_LT, LANES // SSM_CH
    hp = a8 * SSM_STATE
    e_lane = (jnp.arange(a8)[:, None] == jnp.arange(LANES)[None, :] // SSM_CH).astype(F32)
    e_state = (jnp.arange(a8)[:, None] == jnp.arange(hp)[None, :] // SSM_STATE).astype(F32)
    e_lane_t = jnp.tile(e_lane, (1, q))
    e_state_t = jnp.tile(e_state, (1, 2))

    kk = jnp.real(jnp.einsum('ghp,kgp,gpi->kghi', c, pw[:q], b_bar))
    km = kk.reshape(q, lt, a8, SSM_CH, SSM_CH).transpose(1, 4, 0, 2, 3).reshape(lt, SSM_CH, q, LANES)
    lag = jnp.arange(q)[None, :] - jnp.arange(q)[:, None]
    kg = km[:, :, jnp.clip(lag, 0, q - 1), :] * (lag >= 0)[None, None, :, :, None].astype(F32)
    kc = kg.transpose(0, 2, 1, 3, 4).reshape(lt, q, 1, SSM_CH, q * LANES)
    t_c = kc.reshape(lt, q * SSM_CH, q * LANES)

    wc = pw[q - 1 - jnp.arange(q)][..., None] * b_bar[None]
    wri = jnp.stack([jnp.real(wc), jnp.imag(wc)])
    wm = (wri.reshape(2, q, lt, a8, SSM_STATE, SSM_CH).transpose(2, 1, 5, 0, 3, 4)
          .reshape(lt, q, 1, SSM_CH, 2 * hp))
    w_c = wm.reshape(lt, q * SSM_CH, 2 * hp)

    cl = c[None] * pw[1:q + 1][:, :, None, :]
    cri = jnp.stack([jnp.real(cl), -jnp.imag(cl)])
    mm = (cri.reshape(2, q, lt, a8, SSM_CH, SSM_STATE).transpose(2, 0, 5, 1, 3, 4)
          .reshape(lt, 2, 1, SSM_STATE, q * LANES))
    m_c = mm.reshape(lt, 2 * SSM_STATE, q * LANES)

    n_lvl = max(1, (n_sub - 1).bit_length())
    lv = [pw[q]]
    for _ in range(n_lvl - 1):
        lv.append(lv[-1] * lv[-1])
    lvs = jnp.stack(lv).reshape(n_lvl, lt, 1, hp)
    pw_re = jnp.real(lvs).transpose(1, 0, 2, 3)
    pw_im = jnp.imag(lvs).transpose(1, 0, 2, 3)
    dvec = jnp.tile(d_skip.astype(F32).reshape(lt, 1, LANES), (1, 1, q))
    return w_c, t_c, m_c, e_state_t, e_lane_t, pw_re, pw_im, dvec


def _s5_kernel(u_ref, wc_ref, tc_ref, mc_ref, es_ref, el_ref, pwr_ref, pwi_ref, d_ref, y_ref, w_ref, t_ref, m_ref,
               *, n_sub, n_lvl):
    half = (LANES // SSM_CH) * SSM_STATE
    q = SSM_Q
    a8 = LANES // SSM_CH

    @pl.when(pl.program_id(1) == 0)
    def _():
        for a in range(a8):
            el = el_ref[a:a + 1, :]
            es = es_ref[a:a + 1, :]
            for s_ in range(q):
                rows = pl.ds((s_ * a8 + a) * SSM_CH, SSM_CH)
                src = pl.ds(s_ * SSM_CH, SSM_CH)
                t_ref[rows, :] = (tc_ref[src, :] * el).astype(BF16)
                w_ref[rows, :] = (wc_ref[src, :] * es).astype(BF16)
            for r in range(2):
                rows = pl.ds((r * a8 + a) * SSM_STATE, SSM_STATE)
                m_ref[rows, :] = (mc_ref[pl.ds(r * SSM_STATE, SSM_STATE), :] * el).astype(BF16)

    u = jnp.concatenate([u_ref[pl.ds(s, n_sub, stride=q), :] for s in range(q)], axis=1)
    ub = u.astype(BF16)
    s_loc = _dot(ub, w_ref[...])
    re = s_loc[:, :half]
    im = s_loc[:, half:]
    rowi = lax.broadcasted_iota(jnp.int32, (n_sub, half), 0)
    for k in range(n_lvl):
        d = 1 << k
        ar = pwr_ref[k]
        ai = pwi_ref[k]
        keep = rowi >= d
        sre = jnp.where(keep, pltpu.roll(re, d, axis=0), 0.0)
        sim = jnp.where(keep, pltpu.roll(im, d, axis=0), 0.0)
        re, im = re + (ar * sre - ai * sim), im + (ar * sim + ai * sre)
    keep = rowi >= 1
    xre = jnp.where(keep, pltpu.roll(re, 1, axis=0), 0.0)
    xim = jnp.where(keep, pltpu.roll(im, 1, axis=0), 0.0)
    xst = jnp.concatenate([xre, xim], axis=1).astype(BF16)
    y = _dot(ub, t_ref[...]) + _dot(xst, m_ref[...]) + d_ref[...] * u
    for j in range(q):
        y_ref[pl.ds(j, n_sub, stride=q), :] = y[:, j * LANES:(j + 1) * LANES]


def _s5(u, w_c, t_c, m_c, e_state_t, e_lane_t, pw_re, pw_im, dvec, *, bsz, n_sub):
    q = SSM_Q
    n_lvl = pw_re.shape[1]
    kern = functools.partial(_s5_kernel, n_sub=n_sub, n_lvl=n_lvl)
    wide = q * LANES
    seq = n_sub * q
    return pl.pallas_call(
        kern,
        grid=(SSM_LT, bsz),
        in_specs=[
            pl.BlockSpec((None, seq, LANES), lambda l, b: (b, 0, l)),
            pl.BlockSpec((None, q * SSM_CH, wide), lambda l, b: (l, 0, 0)),
            pl.BlockSpec((None, q * SSM_CH, wide), lambda l, b: (l, 0, 0)),
            pl.BlockSpec((None, 2 * SSM_STATE, wide), lambda l, b: (l, 0, 0)),
            pl.BlockSpec((LANES // SSM_CH, wide), lambda l, b: (0, 0)),
            pl.BlockSpec((LANES // SSM_CH, wide), lambda l, b: (0, 0)),
            pl.BlockSpec((None, n_lvl, 1, wide // 2), lambda l, b: (l, 0, 0, 0)),
            pl.BlockSpec((None, n_lvl, 1, wide // 2), lambda l, b: (l, 0, 0, 0)),
            pl.BlockSpec((None, 1, wide), lambda l, b: (l, 0, 0)),
        ],
        out_specs=pl.BlockSpec((None, seq, LANES), lambda l, b: (b, 0, l)),
        out_shape=jax.ShapeDtypeStruct((bsz, seq, D_SSM), F32),
        scratch_shapes=[pltpu.VMEM((wide, wide), BF16)] * 3,
        compiler_params=_params(("arbitrary", "arbitrary")),
        name="s5",
    )(u, w_c, t_c, m_c, e_state_t, e_lane_t, pw_re, pw_im, dvec)


def _compress_kernel(kc_ref, vc_ref, w1k_ref, w2k_ref, w1v_ref, w2vt_ref, posk_ref, posv_ref, gkc_ref,
                     kcmp_ref, vcmpt_ref, *, nch):
    half = CMP_STRIDE * HEAD_DIM

    def hidden(x_ref, w1_ref, pos_ref):
        x = jnp.concatenate([x_ref[pl.ds(j, nch, stride=CMP_STRIDE), :] for j in range(CMP_STRIDE)],
                            axis=1).astype(BF16)
        a = _dot(x, w1_ref[:half, :])
        b = _dot(x, w1_ref[half:, :])
        pv = _dot(pos_ref[...], w1_ref[...])[0:1, :]
        hid = a + pltpu.roll(b, nch - 1, axis=0) + pv
        return _gelu(hid).astype(BF16)

    k = _dot(hidden(kc_ref, w1k_ref, posk_ref), w2k_ref[...])
    ms = jnp.mean(k * k, axis=-1, keepdims=True)
    kcmp_ref[...] = (k * lax.rsqrt(ms + EPS) * gkc_ref[...]).astype(BF16)
    vt = _dot_nt(w2vt_ref[...], hidden(vc_ref, w1v_ref, posv_ref))
    coli = lax.broadcasted_iota(jnp.int32, vt.shape, 1)
    vcmpt_ref[...] = jnp.where(coli < nch - 1, vt, 0.0).astype(BF16)


def _compress(kcf, vcf, w1k, w2k, w1v, w2vt, posk, posv, gkc, *, bsz, nch):
    kern = functools.partial(_compress_kernel, nch=nch)
    wide = CMP_STRIDE * HEAD_DIM
    xspec = pl.BlockSpec((None, None, nch * CMP_STRIDE, HEAD_DIM), lambda b, g: (b, g, 0, 0))
    c2 = lambda b, g: (0, 0)
    return pl.pallas_call(
        kern,
        grid=(bsz, N_KV),
        in_specs=[
            xspec, xspec,
            pl.BlockSpec((2 * wide, CMP_HIDDEN), c2), pl.BlockSpec((CMP_HIDDEN, HEAD_DIM), c2),
            pl.BlockSpec((2 * wide, CMP_HIDDEN), c2), pl.BlockSpec((HEAD_DIM, CMP_HIDDEN), c2),
            pl.BlockSpec((SUBLANES, 2 * wide), c2), pl.BlockSpec((SUBLANES, 2 * wide), c2),
            pl.BlockSpec((1, HEAD_DIM), c2),
        ],
        out_specs=[pl.BlockSpec((None, None, nch, HEAD_DIM), lambda b, g: (b, g, 0, 0)),
                   pl.BlockSpec((None, None, HEAD_DIM, nch), lambda b, g: (b, g, 0, 0))],
        out_shape=[jax.ShapeDtypeStruct((bsz, N_KV, nch, HEAD_DIM), BF16),
                   jax.ShapeDtypeStruct((bsz, N_KV, HEAD_DIM, nch), BF16)],
        compiler_params=_params(("parallel", "parallel")),
        name="compress",
    )(kcf, vcf, w1k, w2k, w1v, w2vt, posk, posv, gkc)


def _nsa_kernel(qt_ref, kcmp_ref, vcmpt_ref, ksa_ref, kw_ref, vst_ref, vwt_ref, gate_ref, ovlt_ref, o_ref,
                accs_ref, accw_ref, sel_ref, *, tq, tk, nch):
    qi = pl.program_id(2)
    q0 = qi * tq
    rows = HPG * tq
    qt = jnp.concatenate([qt_ref[h] for h in range(HPG)], axis=1)
    tpos = q0 + lax.broadcasted_iota(jnp.int32, (1, rows), 1) % tq
    krow = lax.broadcasted_iota(jnp.int32, (tk, rows), 0)
    m0 = jnp.full((1, rows), NEG, F32)

    def run_tiles(k_ref, q_all, vt_ref, acc_ref, m, tiles):
        scores = []
        for kt, kind in tiles:
            kc = jnp.maximum(kt, 0) if kind in ('band', 'valid') else kt
            s = _dot(k_ref[pl.ds(pl.multiple_of(kc * tk, tk), tk), :], q_all)
            if kind == 'causal':
                s = jnp.where(kt * tk + krow <= tpos, s, NEG)
            elif kind == 'band':
                s = jnp.where((kt * tk + krow > tpos - WINDOW) & (kt >= 0), s, NEG)
            elif kind == 'valid':
                s = jnp.where(kt >= 0, s, NEG)
            scores.append((kc, s))
        for kc, s in scores:
            m_new = jnp.maximum(m, jnp.max(s, axis=0, keepdims=True))
            alpha = jnp.exp2(m - m_new)
            p = jnp.exp2(s - m_new).astype(BF16)
            acc_ref[...] = alpha * acc_ref[...] + _dot(vt_ref[kc], p)
            m = m_new
        return m

    accw_ref[...] = jnp.zeros_like(accw_ref)
    n_win = WINDOW // tk
    run_tiles(kw_ref, qt, vwt_ref, accw_ref, m0,
              [(qi - n_win, 'band')] + [(qi - n_win + t, 'valid') for t in range(1, n_win)] + [(qi, 'causal')])

    s = _dot(kcmp_ref[...], qt)
    cend = lax.broadcasted_iota(jnp.int32, (nch, rows), 0) * CMP_STRIDE + (CMP_BLOCK - 1)
    s = jnp.where(cend <= tpos, s, NEG)
    m = jnp.max(s, axis=0, keepdims=True)
    p = jnp.exp2(s - m)
    p = p * jnp.where(tpos >= CMP_BLOCK - 1, 1.0 / jnp.sum(p, axis=0, keepdims=True), 0.0)
    o_cmp = _dot(vcmpt_ref[...], p.astype(BF16))

    psum = p[:, 0:tq]
    for h in range(1, HPG):
        psum = psum + p[:, h * tq:(h + 1) * tq]
    hi = psum.astype(BF16)
    lo = (psum - hi.astype(F32)).astype(BF16)
    ovlt = ovlt_ref[...]
    imp = _dot(ovlt, hi) + _dot(ovlt, lo)
    nb = MAX_SEL_BLOCKS
    blk = lax.broadcasted_iota(jnp.int32, (nb, tq), 0)
    cur = (q0 + lax.broadcasted_iota(jnp.int32, (nb, tq), 1)) // SEL_BLOCK
    forced = (blk == 0) | (blk == cur) | (blk == cur - 1)
    imp = jnp.where(forced, FORCE, jnp.where(blk <= cur, imp, NEG))
    sub = SUBLANES
    rowl = lax.broadcasted_iota(jnp.int32, (sub, tq), 0)
    n_seen = (q0 + tq - 1) // SEL_BLOCK + 1
    sel_ref[...] = jnp.zeros_like(sel_ref)
    for nbv in range(2 * N_SELECT, nb + 1, N_SELECT):
        @pl.when((n_seen > nbv - N_SELECT) & (n_seen <= nbv))
        def _(nbv=nbv):
            groups = [imp[r:r + sub] for r in range(0, nbv, sub)]
            ranks = [jnp.zeros((sub, tq), F32) for _ in groups]
            for i in range(nbv):
                ri = jnp.broadcast_to(imp[i:i + 1, :], (sub, tq))
                for gi, x in enumerate(groups):
                    if i < gi * sub:
                        ahead = ri >= x
                    elif i >= (gi + 1) * sub:
                        ahead = ri > x
                    else:
                        ahead = (ri > x) | ((ri == x) & (rowl > i - gi * sub))
                    ranks[gi] = ranks[gi] + jnp.where(ahead, 1.0, 0.0)
            rank = jnp.concatenate(ranks, axis=0)
            sel_ref[0:nbv, :] = jnp.where(rank < N_SELECT, 0.0, NEG).astype(BF16)
    sel = sel_ref[...]
    qa = jnp.concatenate([qt, jnp.concatenate([sel] * HPG, axis=1)], axis=0)

    accs_ref[...] = jnp.zeros_like(accs_ref)
    nu = NSA_UNROLL

    def sel_multi(j, m):
        return run_tiles(ksa_ref, qa, vst_ref, accs_ref, m, [(nu * j + t, None) for t in range(nu)])

    m_sel = lax.fori_loop(0, qi // nu, sel_multi, m0)
    for r in range(nu):
        @pl.when(qi % nu == r)
        def _(r=r):
            base = qi - r
            run_tiles(ksa_ref, qa, vst_ref, accs_ref, m_sel,
                      [(base + t, None) for t in range(r)] + [(qi, 'causal')])

    accs = accs_ref[...]
    accw = accw_ref[...]
    o_sel = accs[:HEAD_DIM] * (1.0 / accs[HEAD_DIM:HEAD_DIM + 1])
    o_win = accw[:HEAD_DIM] * (1.0 / accw[HEAD_DIM:HEAD_DIM + 1])
    gt = gate_ref[...]
    for h in range(HPG):
        c = slice(h * tq, (h + 1) * tq)
        o_ref[h * HEAD_DIM:(h + 1) * HEAD_DIM, :] = (
            gt[3 * h:3 * h + 1] * o_cmp[:, c] + gt[3 * h + 1:3 * h + 2] * o_sel[:, c]
            + gt[3 * h + 2:3 * h + 3] * o_win[:, c])


def _nsa(qt, kcmp, vcmpt, ksa, kw, vst, vwt, gate, ovlt, *, bsz, seq, nch):
    tq, tk = NSA_TQ, NSA_TK
    assert tq == tk and WINDOW % tk == 0
    nq = seq // tq
    kern = functools.partial(_nsa_kernel, tq=tq, tk=tk, nch=nch)
    full = lambda b, g, i: (b, g, 0, 0)
    full5 = lambda b, g, i: (b, g, 0, 0, 0)
    qd = HPG * HEAD_DIM
    rows = HPG * tq
    return pl.pallas_call(
        kern,
        grid=(bsz, N_KV, nq),
        in_specs=[
            pl.BlockSpec((None, HPG, HEAD_DIM, tq), lambda b, g, i: (b, g, 0, i)),
            pl.BlockSpec((None, None, nch, HEAD_DIM), full),
            pl.BlockSpec((None, None, HEAD_DIM, nch), full),
            pl.BlockSpec((None, None, seq, 2 * HEAD_DIM), full),
            pl.BlockSpec((None, None, seq, HEAD_DIM), full),
            pl.BlockSpec((None, None, seq // tk, V_ROWS, tk), full5),
            pl.BlockSpec((None, None, seq // tk, V_ROWS, tk), full5),
            pl.BlockSpec((None, None, GATE_ROWS, tq), lambda b, g, i: (b, g, 0, i)),
            pl.BlockSpec((MAX_SEL_BLOCKS, nch), lambda b, g, i: (0, 0)),
        ],
        out_specs=pl.BlockSpec((None, qd, tq), lambda b, g, i: (b, g, i)),
        out_shape=jax.ShapeDtypeStruct((bsz, D_ATT, seq), F32),
        scratch_shapes=[pltpu.VMEM((V_ROWS, rows), F32), pltpu.VMEM((V_ROWS, rows), F32),
                        pltpu.VMEM((MAX_SEL_BLOCKS, tq), BF16)],
        compiler_params=_params(("parallel", "parallel", "arbitrary")),
        name="nsa",
    )(qt, kcmp, vcmpt, ksa, kw, vst, vwt, gate, ovlt)


def _route(logits, group=None):
    lane = lax.broadcasted_iota(jnp.int32, logits.shape, 1).astype(F32)
    far = float(LANES)
    is_g = lane < N_EXP_GROUPS
    glog = jnp.where(is_g, logits, -jnp.inf)
    gmax = jnp.max(glog, axis=1, keepdims=True)
    gsum = jnp.sum(jnp.where(is_g, jnp.exp(logits - gmax), 0.0), axis=1, keepdims=True)
    gsel = jnp.min(jnp.where(glog == gmax, lane, far), axis=1, keepdims=True)
    gprob = 1.0 / gsum
    lo = ROUTER_OFF + EXPERTS_PER_GROUP * (gsel if group is None else group)
    in_e = (lane >= lo) & (lane < lo + EXPERTS_PER_GROUP)
    emax = jnp.max(jnp.where(in_e, logits, -jnp.inf), axis=1, keepdims=True)
    eexp = jnp.where(in_e, jnp.exp(logits - emax), 0.0)
    eprob = jnp.where(in_e, eexp / jnp.sum(eexp, axis=1, keepdims=True), -1.0)
    v1 = jnp.max(eprob, axis=1, keepdims=True)
    i1 = jnp.min(jnp.where(eprob == v1, lane, far), axis=1, keepdims=True)
    rest = jnp.where(lane == i1, -1.0, eprob)
    v2 = jnp.max(rest, axis=1, keepdims=True)
    i2 = jnp.min(jnp.where(rest == v2, lane, far), axis=1, keepdims=True)
    den = v1 + v2
    comb = jnp.where(lane == i1, v1 / den * gprob, 0.0) + jnp.where(lane == i2, v2 / den * gprob, 0.0)
    return gsel, comb


def _out_proj_kernel(tm, ys_ref, yat_ref, x_ref, wglu_ref, bglu_ref, gs_ref, ga_ref, wo_ref, g2_ref, wr_ref, br_ref,
                     xt_ref, gsel_ref):
    y = _gelu(ys_ref[...])
    y = y * _sigmoid(_dot(y.astype(BF16), wglu_ref[...]) + bglu_ref[...])
    ysn = y * lax.rsqrt(jnp.mean(y * y, axis=-1, keepdims=True) + EPS) * gs_ref[...]
    yat = yat_ref[...]
    yant = yat * lax.rsqrt(jnp.mean(yat * yat, axis=0, keepdims=True) + EPS) * ga_ref[...]
    yan = yant.T
    x2 = x_ref[...] + _dot(ysn.astype(BF16), wo_ref[:D_SSM, :]) + _dot(yan.astype(BF16), wo_ref[D_SSM:, :])
    for c in range(TT_ROWS):
        xt_ref[pl.ds(c, tm, stride=TT_ROWS), :] = x2[:, c * LANES:(c + 1) * LANES]
    h2 = (x2 * lax.rsqrt(jnp.mean(x2 * x2, axis=-1, keepdims=True) + EPS) * g2_ref[...]).astype(BF16)
    gsel, _ = _route(_dot(h2, wr_ref[...]) + br_ref[...])
    gsel_ref[...] = jnp.broadcast_to(gsel, gsel_ref.shape)


def _out_proj(ys, yat, x2d, wglu, bglu, gs, ga, wo, g2, wr, br, *, seq):
    n_tok = x2d.shape[0]
    tm = PROJ_TM
    nl = seq // tm
    row = lambda i: (i, 0)
    const = lambda i: (0, 0)
    return pl.pallas_call(
        functools.partial(_out_proj_kernel, tm),
        grid=(n_tok // tm,),
        in_specs=[
            pl.BlockSpec((tm, D_SSM), row),
            pl.BlockSpec((None, D_ATT, tm), lambda i: (i // nl, 0, i % nl)),
            pl.BlockSpec((tm, D_MODEL), row),
            pl.BlockSpec((D_SSM, D_SSM), const),
            pl.BlockSpec((1, D_SSM), const),
            pl.BlockSpec((1, D_SSM), const),
            pl.BlockSpec((D_ATT, 1), const),
            pl.BlockSpec((D_SSM + D_ATT, D_MODEL), const),
            pl.BlockSpec((1, D_MODEL), const),
            pl.BlockSpec((D_MODEL, LANES), const),
            pl.BlockSpec((1, LANES), const),
        ],
        out_specs=[pl.BlockSpec((tm * TT_ROWS, LANES), row), pl.BlockSpec((tm, LANES), row)],
        out_shape=[jax.ShapeDtypeStruct((n_tok * TT_ROWS, LANES), F32), jax.ShapeDtypeStruct((n_tok, LANES), F32)],
        compiler_params=_params(("parallel",)),
        name="out_proj",
    )(ys, yat, x2d, wglu, bglu, gs, ga, wo, g2, wr, br)


def _moe_plan(gsel, n_tok):
    tmx = MOE_TM
    n_tiles = n_tok // tmx + N_EXP_GROUPS
    oh = (gsel[:, None] == jnp.arange(N_EXP_GROUPS)[None, :]).astype(jnp.int32)
    csum = jnp.cumsum(oh, axis=0)
    counts = csum[-1]
    rank = jnp.sum(csum * oh, axis=1) - 1
    nt = (counts + tmx - 1) // tmx
    tend = jnp.cumsum(nt)
    toff = tend - nt
    pos = jnp.sum(oh * toff[None, :], axis=1) * tmx + rank
    tile = jnp.arange(n_tiles)
    grp = jnp.minimum(jnp.sum((tile[:, None] >= tend[None, :]).astype(jnp.int32), axis=1), N_EXP_GROUPS - 1)
    nval = jnp.clip(counts[grp] - (tile - toff[grp]) * tmx, 0, tmx)
    return grp.astype(jnp.int32), nval.astype(jnp.int32), pos.astype(jnp.int32)


def _moe_kernel(grp_ref, nval_ref, pos_ref, x_hbm, g2_ref, wr_ref, br_ref, wg_ref, wu_ref, wd_ref, o_hbm,
                xbuf, obuf, abuf, tok_ref, gsem, ssem, *, tmx, n_tiles, n_tok):
    i = pl.program_id(0)
    slot = i % 2

    @pl.when(i == 0)
    def _():
        def place(t, c):
            tok_ref[pos_ref[t]] = t
            return c

        lax.fori_loop(0, n_tok, place, 0, unroll=8)

        def pad_tile(tile, c):
            def pad_row(r, c2):
                tok_ref[tile * tmx + r] = 0
                return c2

            return lax.fori_loop(nval_ref[tile], tmx, pad_row, c)

        lax.fori_loop(0, n_tiles, pad_tile, 0)

    def gather_row(tile, r, dst_slot):
        t = tok_ref[tile * tmx + r]
        return pltpu.make_async_copy(x_hbm.at[pl.ds(pl.multiple_of(t * TT_ROWS, TT_ROWS), TT_ROWS), :],
                                     xbuf.at[dst_slot, pl.ds(r * TT_ROWS, TT_ROWS), :], gsem.at[dst_slot])

    def scatter_row(tile, r, src_slot):
        t = tok_ref[tile * tmx + r]
        return pltpu.make_async_copy(obuf.at[src_slot, pl.ds(r, 1), :], o_hbm.at[pl.ds(t, 1), :], ssem.at[src_slot])

    def gather_wait(src_slot):
        pltpu.make_async_copy(x_hbm.at[pl.ds(0, tmx * TT_ROWS), :], xbuf.at[src_slot], gsem.at[src_slot]).wait()

    def scatter_wait(n, src_slot):
        n8 = pl.multiple_of((n // SUBLANES) * SUBLANES, SUBLANES)

        @pl.when(n8 > 0)
        def _():
            pltpu.make_async_copy(obuf.at[src_slot, pl.ds(0, n8), :], o_hbm.at[pl.ds(0, n8), :],
                                  ssem.at[src_slot]).wait()

        def one(r, c):
            pltpu.make_async_copy(obuf.at[src_slot, pl.ds(0, 1), :], o_hbm.at[pl.ds(0, 1), :],
                                  ssem.at[src_slot]).wait()
            return c

        lax.fori_loop(0, n - n8, one, 0)

    @pl.when(i == 0)
    def _():
        def body(r, c):
            gather_row(0, r, 0).start()
            return c

        lax.fori_loop(0, tmx, body, 0)

    nv = nval_ref[i]
    nv_prev = nval_ref[jnp.maximum(i - 1, 0)]

    @pl.when((i == 0) | (nv_prev > 0))
    def _():
        gather_wait(slot)

    @pl.when(i >= 2)
    def _():
        scatter_wait(nval_ref[i - 2], slot)

    nxt = jnp.minimum(i + 1, n_tiles - 1)
    per = tmx // EXPERTS_PER_GROUP
    prev_full = (i >= 1) & (nv_prev == tmx)

    def experts(with_scatter):
        x2 = jnp.concatenate([xbuf[slot, pl.ds(c, tmx, stride=TT_ROWS), :] for c in range(TT_ROWS)], axis=1)
        h = (x2 * lax.rsqrt(jnp.mean(x2 * x2, axis=-1, keepdims=True) + EPS) * g2_ref[...]).astype(BF16)
        _, cw = _route(_dot(h, wr_ref[...]) + br_ref[...], group=grp_ref[i].astype(F32))
        lane = lax.broadcasted_iota(jnp.int32, cw.shape, 1)
        first = ROUTER_OFF + EXPERTS_PER_GROUP * grp_ref[i]
        for k in range(EXPERTS_PER_GROUP):
            for r in range(k * per, (k + 1) * per):
                gather_row(nxt, r, 1 - slot).start()
                if with_scatter:
                    scatter_row(i - 1, r, 1 - slot).start()
            gate = _dot(h, wg_ref[k].astype(BF16))
            up = _dot(h, wu_ref[k].astype(BF16))
            ck = jnp.sum(jnp.where(lane == first + k, cw, 0.0), axis=1, keepdims=True)
            abuf[:, k * D_EXPERT:(k + 1) * D_EXPERT] = (gate * _sigmoid(gate) * up * ck).astype(BF16)
        obuf[slot] = x2 + _dot(abuf[...], wd_ref[...].astype(BF16))

    @pl.when(prev_full & (nv > 0))
    def _():
        experts(True)

    @pl.when(jnp.logical_not(prev_full) & (nv > 0))
    def _():
        experts(False)

    @pl.when(prev_full & (nv == 0))
    def _():
        def body(r, c):
            scatter_row(i - 1, r, 1 - slot).start()
            return c

        lax.fori_loop(0, tmx, body, 0)

    @pl.when(nv < tmx)
    def _():
        def body(r, c):
            scatter_row(i, r, slot).start()
            return c

        lax.fori_loop(0, nv, body, 0)

    @pl.when(i == n_tiles - 1)
    def _():
        @pl.when(nv == tmx)
        def _():
            def body(r, c):
                scatter_row(i, r, slot).start()
                return c

            lax.fori_loop(0, tmx, body, 0)

        @pl.when(nv > 0)
        def _():
            gather_wait(1 - slot)

        scatter_wait(nv_prev, 1 - slot)
        scatter_wait(nv, slot)


def _moe(x2t, grp, nval, pos, g2, wr, br, wg, wu, wd):
    n_tok = x2t.shape[0] // TT_ROWS
    tmx = MOE_TM
    n_tiles = grp.shape[0]
    kern = functools.partial(_moe_kernel, tmx=tmx, n_tiles=n_tiles, n_tok=n_tok)
    gk = EXPERTS_PER_GROUP * D_EXPERT
    w_bytes = 2 * 3 * EXPERTS_PER_GROUP * D_MODEL * D_EXPERT * 4
    io_bytes = 2 * tmx * (TT_ROWS * LANES + D_MODEL) * 4 + tmx * gk * 2
    vmem_limit = w_bytes + io_bytes + 6 * 1024 * 1024
    assert vmem_limit < V7X_VMEM_BYTES
    grid_spec = pltpu.PrefetchScalarGridSpec(
        num_scalar_prefetch=3,
        grid=(n_tiles,),
        in_specs=[
            pl.BlockSpec(memory_space=pl.ANY),
            pl.BlockSpec((1, D_MODEL), lambda i, g, n, t: (0, 0)),
            pl.BlockSpec((D_MODEL, LANES), lambda i, g, n, t: (0, 0)),
            pl.BlockSpec((1, LANES), lambda i, g, n, t: (0, 0)),
            pl.BlockSpec((None, EXPERTS_PER_GROUP, D_MODEL, D_EXPERT), lambda i, g, n, t: (g[i], 0, 0, 0)),
            pl.BlockSpec((None, EXPERTS_PER_GROUP, D_MODEL, D_EXPERT), lambda i, g, n, t: (g[i], 0, 0, 0)),
            pl.BlockSpec((None, gk, D_MODEL), lambda i, g, n, t: (g[i], 0, 0)),
        ],
        out_specs=pl.BlockSpec(memory_space=pl.ANY),
        scratch_shapes=[
            pltpu.VMEM((2, tmx * TT_ROWS, LANES), F32),
            pltpu.VMEM((2, tmx, D_MODEL), F32),
            pltpu.VMEM((tmx, gk), BF16),
            pltpu.SMEM((n_tiles * tmx,), jnp.int32),
            pltpu.SemaphoreType.DMA((2,)),
            pltpu.SemaphoreType.DMA((2,)),
        ],
    )
    return pl.pallas_call(
        kern,
        grid_spec=grid_spec,
        out_shape=jax.ShapeDtypeStruct((n_tok, D_MODEL), F32),
        compiler_params=_params(("arbitrary",), vmem_limit),
        name="moe",
    )(grp, nval, pos, x2t, g2, wr, br, wg, wu, wd)


def _block_diag_ones(n, blk):
    i = jnp.arange(n) // blk
    return (i[:, None] == i[None, :]).astype(BF16)


def _layer(x, norm1_g, w_in, lam_re, lam_im, log_step, b_re, b_im, c_re, c_im, d_skip,
           w_glu, b_glu, g_q, g_kc, g_ks, g_kw, pos_k, pos_v, w_ck1, w_ck2, w_cv1, w_cv2,
           out_g_ssm, out_g_att, w_out, norm2_g, w_grp, b_grp, w_exp, b_exp, w_gate, w_up, w_down):
    bsz, seq, _ = x.shape
    assert seq % PROJ_TM == 0 and seq // SEL_BLOCK <= MAX_SEL_BLOCKS
    n_tok = bsz * seq
    x2d = x.reshape(n_tok, D_MODEL)
    q8 = SSM_Q
    n_sub = seq // q8
    nch = seq // CMP_STRIDE

    o_q = D_SSM
    o_kv = D_SSM + D_ATT
    o_gt = o_kv + 6 * D_KV
    kv = lambda i: w_in[:, o_kv + i * D_KV:o_kv + (i + 1) * D_KV]
    wrow = jnp.concatenate([w_in[:, :o_q], kv(0), kv(1), kv(2), kv(4)], axis=1).astype(BF16)
    per_g = HPG * N_BRANCH
    wgt = jnp.zeros((D_MODEL, N_KV * GATE_ROWS), F32)
    for g in range(N_KV):
        wgt = wgt.at[:, g * GATE_ROWS:g * GATE_ROWS + per_g].set(w_in[:, o_gt + g * per_g:o_gt + (g + 1) * per_g])
    wcol = jnp.concatenate([w_in[:, o_q:o_kv], kv(3), kv(5), wgt], axis=1).T.astype(BF16)
    qscale = (HEAD_DIM ** -0.5) * math.log2(math.e)
    gq = (jnp.tile(g_q.astype(F32), N_HEADS) * qscale).reshape(D_ATT, 1)
    gks = jnp.tile(g_ks.astype(F32), N_KV).reshape(1, D_KV)
    gkw = jnp.tile(g_kw.astype(F32), N_KV).reshape(1, D_KV)

    u, qt, kc, vc, ksa, kw, vst, vwt, gate = _in_proj(
        x2d, norm1_g.reshape(1, D_MODEL), wrow, wcol, gq, gks, gkw,
        _block_diag_ones(D_KV, HEAD_DIM), bsz=bsz, seq=seq)

    w_c, t_c, m_c, e_state_t, e_lane_t, pw_re, pw_im, dvec = _s5_weights(
        lam_re, lam_im, log_step, b_re, b_im, c_re, c_im, d_skip, n_sub)
    ys = _s5(u.reshape(bsz, seq, D_SSM), w_c, t_c, m_c, e_state_t, e_lane_t, pw_re, pw_im, dvec,
             bsz=bsz, n_sub=n_sub).reshape(n_tok, D_SSM)

    wide = CMP_STRIDE * HEAD_DIM
    pad8 = lambda p: jnp.zeros((SUBLANES, 2 * wide), F32).at[0].set(p.reshape(-1)).astype(BF16)
    kcmp, vcmpt = _compress(
        kc, vc,
        w_ck1.astype(BF16), w_ck2.astype(BF16), w_cv1.astype(BF16), w_cv2.T.astype(BF16),
        pad8(pos_k), pad8(pos_v), g_kc.astype(F32).reshape(1, HEAD_DIM), bsz=bsz, nch=nch)
    cstart = jnp.arange(nch) * CMP_STRIDE
    sstart = jnp.arange(MAX_SEL_BLOCKS) * SEL_BLOCK
    ovlt = ((cstart[None, :] < sstart[:, None] + SEL_BLOCK) & (cstart[None, :] + CMP_BLOCK > sstart[:, None])
            & (jnp.arange(MAX_SEL_BLOCKS)[:, None] < seq // SEL_BLOCK)
            & (jnp.arange(nch)[None, :] < nch - 1)).astype(BF16)
    yat = _nsa(qt, kcmp, vcmpt, ksa, kw, vst, vwt, gate, ovlt, bsz=bsz, seq=seq, nch=nch)

    wr = jnp.zeros((D_MODEL, LANES), F32)
    wr = wr.at[:, :N_EXP_GROUPS].set(w_grp).at[:, ROUTER_OFF:ROUTER_OFF + N_EXPERTS].set(w_exp).astype(BF16)
    br = jnp.zeros((1, LANES), F32)
    br = br.at[0, :N_EXP_GROUPS].set(b_grp).at[0, ROUTER_OFF:ROUTER_OFF + N_EXPERTS].set(b_exp)
    g2 = norm2_g.reshape(1, D_MODEL).astype(F32)
    x2t, gsel = _out_proj(
        ys, yat, x2d, w_glu.astype(BF16), b_glu.reshape(1, D_SSM).astype(F32),
        out_g_ssm.reshape(1, D_SSM).astype(F32), out_g_att.reshape(D_ATT, 1).astype(F32),
        w_out.astype(BF16), g2, wr, br, seq=seq)

    grp, nval, pos = _moe_plan(gsel[:, 0].astype(jnp.int32), n_tok)
    gshape = (N_EXP_GROUPS, EXPERTS_PER_GROUP, D_MODEL, D_EXPERT)
    out = _moe(x2t, grp, nval, pos, g2, wr, br, w_gate.reshape(gshape), w_up.reshape(gshape),
               w_down.reshape(N_EXP_GROUPS, EXPERTS_PER_GROUP * D_EXPERT, D_MODEL))
    return out.reshape(bsz, seq, D_MODEL)


def kernel(x, norm1_g, w_in, lam_re, lam_im, log_step, b_re, b_im, c_re, c_im, d_skip, w_glu, b_glu, g_q, g_kc, g_ks, g_kw, pos_k, pos_v, w_ck1, w_ck2, w_cv1, w_cv2, out_g_ssm, out_g_att, w_out, norm2_g, w_grp, b_grp, w_exp, b_exp, w_gate, w_up, w_down):
    depth = norm1_g.shape[0]
    for l in range(depth):
        x = _layer(x, norm1_g[l], w_in[l], lam_re[l], lam_im[l], log_step[l], b_re[l], b_im[l], c_re[l],
                   c_im[l], d_skip[l], w_glu[l], b_glu[l], g_q[l], g_kc[l], g_ks[l], g_kw[l], pos_k[l],
                   pos_v[l], w_ck1[l], w_ck2[l], w_cv1[l], w_cv2[l], out_g_ssm[l], out_g_att[l], w_out[l],
                   norm2_g[l], w_grp[l], b_grp[l], w_exp[l], b_exp[l], w_gate[l], w_up[l], w_down[l])
    return x
```

```python
import functools
import math

import jax
import jax.numpy as jnp
from jax import lax
from jax.experimental import pallas as pl
from jax.experimental.pallas import tpu as pltpu

D_MODEL = 1024
D_SSM = 512
SSM_CH = 16
SSM_STATE = 64
D_ATT = 512
HEAD_DIM = 64
N_HEADS = D_ATT // HEAD_DIM
N_KV = 2
HPG = N_HEADS // N_KV
D_KV = N_KV * HEAD_DIM
N_BRANCH = 3
CMP_STRIDE = 16
CMP_BLOCK = 2 * CMP_STRIDE
CMP_HIDDEN = 256
SEL_BLOCK = 64
N_SELECT = 16
WINDOW = 512
N_EXP_GROUPS = 4
EXPERTS_PER_GROUP = 8
N_EXPERTS = N_EXP_GROUPS * EXPERTS_PER_GROUP
D_EXPERT = 256
EPS = 1e-6
NEG = -1e30
FORCE = 1e9

LANES = 128
SUBLANES = 8
SSM_Q = 8
SSM_LT = D_SSM // LANES
ROUTER_OFF = N_EXP_GROUPS
NSA_TQ = 256
NSA_TK = 256
NSA_UNROLL = 8
V_ROWS = HEAD_DIM + 16
MAX_SEL_BLOCKS = 64
MOE_TM = 256
TT_ROWS = D_MODEL // LANES
PROJ_TM = 1024
GATE_ROWS = 16
V7X_VMEM_BYTES = 64 * 1024 * 1024
VMEM_LIMIT = V7X_VMEM_BYTES - 8 * 1024 * 1024

F32 = jnp.float32
BF16 = jnp.bfloat16


def _dot(a, b):
    return jnp.dot(a, b, preferred_element_type=F32)


def _dot_nt(a, b):
    return lax.dot_general(a, b, (((1,), (1,)), ((), ())), preferred_element_type=F32)


def _split_dot(x, w):
    hi = x.astype(BF16)
    lo = (x - hi.astype(F32)).astype(BF16)
    return _dot(hi, w) + _dot(lo, w)


def _gelu(x):
    c = math.sqrt(2.0 / math.pi)
    return 0.5 * x * (1.0 + jnp.tanh(c * (x + 0.044715 * (x * x * x))))


def _sigmoid(x):
    return 1.0 / (1.0 + jnp.exp(-x))


def _params(sem, vmem_limit=VMEM_LIMIT):
    return pltpu.CompilerParams(dimension_semantics=sem, vmem_limit_bytes=vmem_limit)


def _in_proj_kernel(x_ref, g1_ref, wrow_ref, wcol_ref, gq_ref, gks_ref, gkw_ref, bd128_ref,
                    u_ref, qt_ref, kc_ref, vc_ref, ksa_ref, kw_ref, vst_ref, vwt_ref, gate_ref, *, tm, nl):
    x = x_ref[...]
    ms = jnp.mean(x * x, axis=-1, keepdims=True)
    hn = (x * lax.rsqrt(ms + EPS) * g1_ref[...]).astype(BF16)

    pr = _dot(hn, wrow_ref[...])
    u_ref[...] = pr[:, :D_SSM]
    kc, vc, ks, kw = [pr[:, D_SSM + i * D_KV:D_SSM + (i + 1) * D_KV] for i in range(4)]
    kss = _split_dot(ks * ks, bd128_ref[...])
    ksn = ks * lax.rsqrt(kss * (1.0 / HEAD_DIM) + EPS) * gks_ref[...]
    kws = _split_dot(kw * kw, bd128_ref[...])
    kwn = kw * lax.rsqrt(kws * (1.0 / HEAD_DIM) + EPS) * gkw_ref[...]
    t0 = (pl.program_id(0) % nl) * tm
    tpos = t0 + lax.broadcasted_iota(jnp.int32, (tm, MAX_SEL_BLOCKS), 0)
    blk = lax.broadcasted_iota(jnp.int32, (tm, MAX_SEL_BLOCKS), 1)
    onehot = jnp.where(tpos // SEL_BLOCK == blk, 1.0, 0.0).astype(BF16)
    for g in range(N_KV):
        sl = slice(g * HEAD_DIM, (g + 1) * HEAD_DIM)
        kc_ref[g] = kc[:, sl]
        vc_ref[g] = vc[:, sl]
        ksa_ref[g] = jnp.concatenate([ksn[:, sl].astype(BF16), onehot], axis=1)
        kw_ref[g] = kwn[:, sl].astype(BF16)

    pc = _dot_nt(wcol_ref[...], hn)
    gq = gq_ref[...]
    for h in range(N_HEADS):
        sl = slice(h * HEAD_DIM, (h + 1) * HEAD_DIM)
        qh = pc[sl]
        ss = jnp.sum(qh * qh, axis=0, keepdims=True)
        qt_ref[h] = (qh * lax.rsqrt(ss * (1.0 / HEAD_DIM) + EPS) * gq[sl]).astype(BF16)
    ones_rows = jnp.where(lax.broadcasted_iota(jnp.int32, (V_ROWS - HEAD_DIM, tm), 0) == 0, 1.0, 0.0)
    for g in range(N_KV):
        for o_ref, base in ((vst_ref, D_ATT), (vwt_ref, D_ATT + D_KV)):
            vt = jnp.concatenate([pc[base + g * HEAD_DIM:base + (g + 1) * HEAD_DIM], ones_rows], axis=0)
            vt = vt.astype(BF16)
            for j in range(tm // NSA_TK):
                o_ref[g, j] = vt[:, j * NSA_TK:(j + 1) * NSA_TK]
        gb = D_ATT + 2 * D_KV + g * GATE_ROWS
        gate_ref[g] = _sigmoid(pc[gb:gb + GATE_ROWS])


def _in_proj(x2d, g1, wrow, wcol, gq, gks, gkw, bd128, *, bsz, seq):
    tm = PROJ_TM
    nl = seq // tm
    n_tok = bsz * seq
    kern = functools.partial(_in_proj_kernel, tm=tm, nl=nl)
    row = lambda i: (i, 0)
    const = lambda i: (0, 0)
    bgl = lambda i: (i // nl, 0, i % nl, 0)
    n_col = wcol.shape[0]
    jt = tm // NSA_TK

    def kvspec(width):
        return pl.BlockSpec((None, N_KV, tm, width), bgl)

    def kvshape(width, dtype=BF16):
        return jax.ShapeDtypeStruct((bsz, N_KV, seq, width), dtype)

    vt_spec = pl.BlockSpec((None, N_KV, jt, V_ROWS, NSA_TK), lambda i: (i // nl, 0, i % nl, 0, 0))
    vt_shape = jax.ShapeDtypeStruct((bsz, N_KV, seq // NSA_TK, V_ROWS, NSA_TK), BF16)
    return pl.pallas_call(
        kern,
        grid=(n_tok // tm,),
        in_specs=[
            pl.BlockSpec((tm, D_MODEL), row),
            pl.BlockSpec((1, D_MODEL), const),
            pl.BlockSpec((D_MODEL, D_SSM + 4 * D_KV), const),
            pl.BlockSpec((n_col, D_MODEL), const),
            pl.BlockSpec((D_ATT, 1), const),
            pl.BlockSpec((1, D_KV), const),
            pl.BlockSpec((1, D_KV), const),
            pl.BlockSpec((D_KV, D_KV), const),
        ],
        out_specs=[
            pl.BlockSpec((tm, D_SSM), row),
            pl.BlockSpec((None, N_HEADS, HEAD_DIM, tm), lambda i: (i // nl, 0, 0, i % nl)),
            kvspec(HEAD_DIM), kvspec(HEAD_DIM), kvspec(2 * HEAD_DIM), kvspec(HEAD_DIM),
            vt_spec, vt_spec,
            pl.BlockSpec((None, N_KV, GATE_ROWS, tm), lambda i: (i // nl, 0, 0, i % nl)),
        ],
        out_shape=[
            jax.ShapeDtypeStruct((n_tok, D_SSM), F32),
            jax.ShapeDtypeStruct((bsz, N_HEADS, HEAD_DIM, seq), BF16),
            kvshape(HEAD_DIM, F32), kvshape(HEAD_DIM, F32), kvshape(2 * HEAD_DIM), kvshape(HEAD_DIM),
            vt_shape, vt_shape,
            jax.ShapeDtypeStruct((bsz, N_KV, GATE_ROWS, seq), F32),
        ],
        compiler_params=_params(("parallel",)),
        name="in_proj",
    )(x2d, g1, wrow, wcol, gq, gks, gkw, bd128)


def _s5_weights(lam_re, lam_im, log_step, b_re, b_im, c_re, c_im, d_skip, n_sub):
    q = SSM_Q
    lam = lax.complex(lam_re.astype(F32), lam_im.astype(F32))
    step = jnp.exp(log_step.astype(F32))[:, None]
    lam_bar = jnp.exp(lam * step)
    b_bar = ((lam_bar - 1.0) / lam)[..., None] * lax.complex(b_re.astype(F32), b_im.astype(F32))
    c = lax.complex(c_re.astype(F32), c_im.astype(F32))
    pows = [jnp.ones_like(lam_bar)]
    for _ in range(q):
        pows.append(pows[-1] * lam_bar)
    pw = jnp.stack(pows)
    lt, a8 = SSM_LT, LANES // SSM_CH
    hp = a8 * SSM_STATE
    e_lane = (jnp.arange(a8)[:, None] == jnp.arange(LANES)[None, :] // SSM_CH).astype(F32)
    e_state = (jnp.arange(a8)[:, None] == jnp.arange(hp)[None, :] // SSM_STATE).astype(F32)
    e_lane_t = jnp.tile(e_lane, (1, q))
    e_state_t = jnp.tile(e_state, (1, 2))

    kk = jnp.real(jnp.einsum('ghp,kgp,gpi->kghi', c, pw[:q], b_bar))
    km = kk.reshape(q, lt, a8, SSM_CH, SSM_CH).transpose(1, 4, 0, 2, 3).reshape(lt, SSM_CH, q, LANES)
    lag = jnp.arange(q)[None, :] - jnp.arange(q)[:, None]
    kg = km[:, :, jnp.clip(lag, 0, q - 1), :] * (lag >= 0)[None, None, :, :, None].astype(F32)
    kc = kg.transpose(0, 2, 1, 3, 4).reshape(lt, q, 1, SSM_CH, q * LANES)
    t_c = kc.reshape(lt, q * SSM_CH, q * LANES)

    wc = pw[q - 1 - jnp.arange(q)][..., None] * b_bar[None]
    wri = jnp.stack([jnp.real(wc), jnp.imag(wc)])
    wm = (wri.reshape(2, q, lt, a8, SSM_STATE, SSM_CH).transpose(2, 1, 5, 0, 3, 4)
          .reshape(lt, q, 1, SSM_CH, 2 * hp))
    w_c = wm.reshape(lt, q * SSM_CH, 2 * hp)

    cl = c[None] * pw[1:q + 1][:, :, None, :]
    cri = jnp.stack([jnp.real(cl), -jnp.imag(cl)])
    mm = (cri.reshape(2, q, lt, a8, SSM_CH, SSM_STATE).transpose(2, 0, 5, 1, 3, 4)
          .reshape(lt, 2, 1, SSM_STATE, q * LANES))
    m_c = mm.reshape(lt, 2 * SSM_STATE, q * LANES)

    n_lvl = max(1, (n_sub - 1).bit_length())
    lv = [pw[q]]
    for _ in range(n_lvl - 1):
        lv.append(lv[-1] * lv[-1])
    lvs = jnp.stack(lv).reshape(n_lvl, lt, 1, hp)
    pw_re = jnp.real(lvs).transpose(1, 0, 2, 3)
    pw_im = jnp.imag(lvs).transpose(1, 0, 2, 3)
    dvec = jnp.tile(d_skip.astype(F32).reshape(lt, 1, LANES), (1, 1, q))
    return w_c, t_c, m_c, e_state_t, e_lane_t, pw_re, pw_im, dvec


def _s5_kernel(u_ref, wc_ref, tc_ref, mc_ref, es_ref, el_ref, pwr_ref, pwi_ref, d_ref, y_ref, w_ref, t_ref, m_ref,
               *, n_sub, n_lvl):
    half = (LANES // SSM_CH) * SSM_STATE
    q = SSM_Q
    a8 = LANES // SSM_CH

    @pl.when(pl.program_id(1) == 0)
    def _():
        for a in range(a8):
            el = el_ref[a:a + 1, :]
            es = es_ref[a:a + 1, :]
            for s_ in range(q):
                rows = pl.ds((s_ * a8 + a) * SSM_CH, SSM_CH)
                src = pl.ds(s_ * SSM_CH, SSM_CH)
                t_ref[rows, :] = (tc_ref[src, :] * el).astype(BF16)
                w_ref[rows, :] = (wc_ref[src, :] * es).astype(BF16)
            for r in range(2):
                rows = pl.ds((r * a8 + a) * SSM_STATE, SSM_STATE)
                m_ref[rows, :] = (mc_ref[pl.ds(r * SSM_STATE, SSM_STATE), :] * el).astype(BF16)

    u = jnp.concatenate([u_ref[pl.ds(s, n_sub, stride=q), :] for s in range(q)], axis=1)
    ub = u.astype(BF16)
    s_loc = _dot(ub, w_ref[...])
    re = s_loc[:, :half]
    im = s_loc[:, half:]
    rowi = lax.broadcasted_iota(jnp.int32, (n_sub, half), 0)
    for k in range(n_lvl):
        d = 1 << k
        ar = pwr_ref[k]
        ai = pwi_ref[k]
        keep = rowi >= d
        sre = jnp.where(keep, pltpu.roll(re, d, axis=0), 0.0)
        sim = jnp.where(keep, pltpu.roll(im, d, axis=0), 0.0)
        re, im = re + (ar * sre - ai * sim), im + (ar * sim + ai * sre)
    keep = rowi >= 1
    xre = jnp.where(keep, pltpu.roll(re, 1, axis=0), 0.0)
    xim = jnp.where(keep, pltpu.roll(im, 1, axis=0), 0.0)
    xst = jnp.concatenate([xre, xim], axis=1).astype(BF16)
    y = _dot(ub, t_ref[...]) + _dot(xst, m_ref[...]) + d_ref[...] * u
    for j in range(q):
        y_ref[pl.ds(j, n_sub, stride=q), :] = y[:, j * LANES:(j + 1) * LANES]


def _s5(u, w_c, t_c, m_c, e_state_t, e_lane_t, pw_re, pw_im, dvec, *, bsz, n_sub):
    q = SSM_Q
    n_lvl = pw_re.shape[1]
    kern = functools.partial(_s5_kernel, n_sub=n_sub, n_lvl=n_lvl)
    wide = q * LANES
    seq = n_sub * q
    return pl.pallas_call(
        kern,
        grid=(SSM_LT, bsz),
        in_specs=[
            pl.BlockSpec((None, seq, LANES), lambda l, b: (b, 0, l)),
            pl.BlockSpec((None, q * SSM_CH, wide), lambda l, b: (l, 0, 0)),
            pl.BlockSpec((None, q * SSM_CH, wide), lambda l, b: (l, 0, 0)),
            pl.BlockSpec((None, 2 * SSM_STATE, wide), lambda l, b: (l, 0, 0)),
            pl.BlockSpec((LANES // SSM_CH, wide), lambda l, b: (0, 0)),
            pl.BlockSpec((LANES // SSM_CH, wide), lambda l, b: (0, 0)),
            pl.BlockSpec((None, n_lvl, 1, wide // 2), lambda l, b: (l, 0, 0, 0)),
            pl.BlockSpec((None, n_lvl, 1, wide // 2), lambda l, b: (l, 0, 0, 0)),
            pl.BlockSpec((None, 1, wide), lambda l, b: (l, 0, 0)),
        ],
        out_specs=pl.BlockSpec((None, seq, LANES), lambda l, b: (b, 0, l)),
        out_shape=jax.ShapeDtypeStruct((bsz, seq, D_SSM), F32),
        scratch_shapes=[pltpu.VMEM((wide, wide), BF16)] * 3,
        compiler_params=_params(("arbitrary", "arbitrary")),
        name="s5",
    )(u, w_c, t_c, m_c, e_state_t, e_lane_t, pw_re, pw_im, dvec)


def _compress_kernel(kc_ref, vc_ref, w1k_ref, w2k_ref, w1v_ref, w2vt_ref, posk_ref, posv_ref, gkc_ref,
                     kcmp_ref, vcmpt_ref, *, nch):
    half = CMP_STRIDE * HEAD_DIM

    def hidden(x_ref, w1_ref, pos_ref):
        x = jnp.concatenate([x_ref[pl.ds(j, nch, stride=CMP_STRIDE), :] for j in range(CMP_STRIDE)],
                            axis=1).astype(BF16)
        a = _dot(x, w1_ref[:half, :])
        b = _dot(x, w1_ref[half:, :])
        pv = _dot(pos_ref[...], w1_ref[...])[0:1, :]
        hid = a + pltpu.roll(b, nch - 1, axis=0) + pv
        return _gelu(hid).astype(BF16)

    k = _dot(hidden(kc_ref, w1k_ref, posk_ref), w2k_ref[...])
    ms = jnp.mean(k * k, axis=-1, keepdims=True)
    kcmp_ref[...] = (k * lax.rsqrt(ms + EPS) * gkc_ref[...]).astype(BF16)
    vt = _dot_nt(w2vt_ref[...], hidden(vc_ref, w1v_ref, posv_ref))
    coli = lax.broadcasted_iota(jnp.int32, vt.shape, 1)
    vcmpt_ref[...] = jnp.where(coli < nch - 1, vt, 0.0).astype(BF16)


def _compress(kcf, vcf, w1k, w2k, w1v, w2vt, posk, posv, gkc, *, bsz, nch):
    kern = functools.partial(_compress_kernel, nch=nch)
    wide = CMP_STRIDE * HEAD_DIM
    xspec = pl.BlockSpec((None, None, nch * CMP_STRIDE, HEAD_DIM), lambda b, g: (b, g, 0, 0))
    c2 = lambda b, g: (0, 0)
    return pl.pallas_call(
        kern,
        grid=(bsz, N_KV),
        in_specs=[
            xspec, xspec,
            pl.BlockSpec((2 * wide, CMP_HIDDEN), c2), pl.BlockSpec((CMP_HIDDEN, HEAD_DIM), c2),
            pl.BlockSpec((2 * wide, CMP_HIDDEN), c2), pl.BlockSpec((HEAD_DIM, CMP_HIDDEN), c2),
            pl.BlockSpec((SUBLANES, 2 * wide), c2), pl.BlockSpec((SUBLANES, 2 * wide), c2),
            pl.BlockSpec((1, HEAD_DIM), c2),
        ],
        out_specs=[pl.BlockSpec((None, None, nch, HEAD_DIM), lambda b, g: (b, g, 0, 0)),
                   pl.BlockSpec((None, None, HEAD_DIM, nch), lambda b, g: (b, g, 0, 0))],
        out_shape=[jax.ShapeDtypeStruct((bsz, N_KV, nch, HEAD_DIM), BF16),
                   jax.ShapeDtypeStruct((bsz, N_KV, HEAD_DIM, nch), BF16)],
        compiler_params=_params(("parallel", "parallel")),
        name="compress",
    )(kcf, vcf, w1k, w2k, w1v, w2vt, posk, posv, gkc)


def _nsa_kernel(qt_ref, kcmp_ref, vcmpt_ref, ksa_ref, kw_ref, vst_ref, vwt_ref, gate_ref, ovlt_ref, o_ref,
                accs_ref, accw_ref, sel_ref, *, tq, tk, nch):
    qi = pl.program_id(2)
    q0 = qi * tq
    rows = HPG * tq
    qt = jnp.concatenate([qt_ref[h] for h in range(HPG)], axis=1)
    tpos = q0 + lax.broadcasted_iota(jnp.int32, (1, rows), 1) % tq
    krow = lax.broadcasted_iota(jnp.int32, (tk, rows), 0)
    m0 = jnp.full((1, rows), NEG, F32)

    def run_tiles(k_ref, q_all, vt_ref, acc_ref, m, tiles):
        scores = []
        for kt, kind in tiles:
            kc = jnp.maximum(kt, 0) if kind in ('band', 'valid') else kt
            s = _dot(k_ref[pl.ds(pl.multiple_of(kc * tk, tk), tk), :], q_all)
            if kind == 'causal':
                s = jnp.where(kt * tk + krow <= tpos, s, NEG)
            elif kind == 'band':
                s = jnp.where((kt * tk + krow > tpos - WINDOW) & (kt >= 0), s, NEG)
            elif kind == 'valid':
                s = jnp.where(kt >= 0, s, NEG)
            scores.append((kc, s))
        for kc, s in scores:
            m_new = jnp.maximum(m, jnp.max(s, axis=0, keepdims=True))
            alpha = jnp.exp2(m - m_new)
            p = jnp.exp2(s - m_new).astype(BF16)
            acc_ref[...] = alpha * acc_ref[...] + _dot(vt_ref[kc], p)
            m = m_new
        return m

    accw_ref[...] = jnp.zeros_like(accw_ref)
    n_win = WINDOW // tk
    run_tiles(kw_ref, qt, vwt_ref, accw_ref, m0,
              [(qi - n_win, 'band')] + [(qi - n_win + t, 'valid') for t in range(1, n_win)] + [(qi, 'causal')])

    s = _dot(kcmp_ref[...], qt)
    cend = lax.broadcasted_iota(jnp.int32, (nch, rows), 0) * CMP_STRIDE + (CMP_BLOCK - 1)
    s = jnp.where(cend <= tpos, s, NEG)
    m = jnp.max(s, axis=0, keepdims=True)
    p = jnp.exp2(s - m)
    p = p * jnp.where(tpos >= CMP_BLOCK - 1, 1.0 / jnp.sum(p, axis=0, keepdims=True), 0.0)
    o_cmp = _dot(vcmpt_ref[...], p.astype(BF16))

    psum = p[:, 0:tq]
    for h in range(1, HPG):
        psum = psum + p[:, h * tq:(h + 1) * tq]
    hi = psum.astype(BF16)
    lo = (psum - hi.astype(F32)).astype(BF16)
    ovlt = ovlt_ref[...]
    imp = _dot(ovlt, hi) + _dot(ovlt, lo)
    nb = MAX_SEL_BLOCKS
    blk = lax.broadcasted_iota(jnp.int32, (nb, tq), 0)
    cur = (q0 + lax.broadcasted_iota(jnp.int32, (nb, tq), 1)) // SEL_BLOCK
    forced = (blk == 0) | (blk == cur) | (blk == cur - 1)
    imp = jnp.where(forced, FORCE, jnp.where(blk <= cur, imp, NEG))
    sub = SUBLANES
    rowl = lax.broadcasted_iota(jnp.int32, (sub, tq), 0)
    n_seen = (q0 + tq - 1) // SEL_BLOCK + 1
    sel_ref[...] = jnp.zeros_like(sel_ref)
    for nbv in range(2 * N_SELECT, nb + 1, N_SELECT):
        @pl.when((n_seen > nbv - N_SELECT) & (n_seen <= nbv))
        def _(nbv=nbv):
            groups = [imp[r:r + sub] for r in range(0, nbv, sub)]
            ranks = [jnp.zeros((sub, tq), F32) for _ in groups]
            for i in range(nbv):
                ri = jnp.broadcast_to(imp[i:i + 1, :], (sub, tq))
                for gi, x in enumerate(groups):
                    if i < gi * sub:
                        ahead = ri >= x
                    elif i >= (gi + 1) * sub:
                        ahead = ri > x
                    else:
                        ahead = (ri > x) | ((ri == x) & (rowl > i - gi * sub))
                    ranks[gi] = ranks[gi] + jnp.where(ahead, 1.0, 0.0)
            rank = jnp.concatenate(ranks, axis=0)
            sel_ref[0:nbv, :] = jnp.where(rank < N_SELECT, 0.0, NEG).astype(BF16)
    sel = sel_ref[...]
    qa = jnp.concatenate([qt, jnp.concatenate([sel] * HPG, axis=1)], axis=0)

    accs_ref[...] = jnp.zeros_like(accs_ref)
    nu = NSA_UNROLL

    def sel_multi(j, m):
        return run_tiles(ksa_ref, qa, vst_ref, accs_ref, m, [(nu * j + t, None) for t in range(nu)])

    m_sel = lax.fori_loop(0, qi // nu, sel_multi, m0)
    for r in range(nu):
        @pl.when(qi % nu == r)
        def _(r=r):
            base = qi - r
            run_tiles(ksa_ref, qa, vst_ref, accs_ref, m_sel,
                      [(base + t, None) for t in range(r)] + [(qi, 'causal')])

    accs = accs_ref[...]
    accw = accw_ref[...]
    o_sel = accs[:HEAD_DIM] * (1.0 / accs[HEAD_DIM:HEAD_DIM + 1])
    o_win = accw[:HEAD_DIM] * (1.0 / accw[HEAD_DIM:HEAD_DIM + 1])
    gt = gate_ref[...]
    for h in range(HPG):
        c = slice(h * tq, (h + 1) * tq)
        o_ref[h * HEAD_DIM:(h + 1) * HEAD_DIM, :] = (
            gt[3 * h:3 * h + 1] * o_cmp[:, c] + gt[3 * h + 1:3 * h + 2] * o_sel[:, c]
            + gt[3 * h + 2:3 * h + 3] * o_win[:, c])


def _nsa(qt, kcmp, vcmpt, ksa, kw, vst, vwt, gate, ovlt, *, bsz, seq, nch):
    tq, tk = NSA_TQ, NSA_TK
    assert tq == tk and WINDOW % tk == 0
    nq = seq // tq
    kern = functools.partial(_nsa_kernel, tq=tq, tk=tk, nch=nch)
    full = lambda b, g, i: (b, g, 0, 0)
    full5 = lambda b, g, i: (b, g, 0, 0, 0)
    qd = HPG * HEAD_DIM
    rows = HPG * tq
    return pl.pallas_call(
        kern,
        grid=(bsz, N_KV, nq),
        in_specs=[
            pl.BlockSpec((None, HPG, HEAD_DIM, tq), lambda b, g, i: (b, g, 0, i)),
            pl.BlockSpec((None, None, nch, HEAD_DIM), full),
            pl.BlockSpec((None, None, HEAD_DIM, nch), full),
            pl.BlockSpec((None, None, seq, 2 * HEAD_DIM), full),
            pl.BlockSpec((None, None, seq, HEAD_DIM), full),
            pl.BlockSpec((None, None, seq // tk, V_ROWS, tk), full5),
            pl.BlockSpec((None, None, seq // tk, V_ROWS, tk), full5),
            pl.BlockSpec((None, None, GATE_ROWS, tq), lambda b, g, i: (b, g, 0, i)),
            pl.BlockSpec((MAX_SEL_BLOCKS, nch), lambda b, g, i: (0, 0)),
        ],
        out_specs=pl.BlockSpec((None, qd, tq), lambda b, g, i: (b, g, i)),
        out_shape=jax.ShapeDtypeStruct((bsz, D_ATT, seq), F32),
        scratch_shapes=[pltpu.VMEM((V_ROWS, rows), F32), pltpu.VMEM((V_ROWS, rows), F32),
                        pltpu.VMEM((MAX_SEL_BLOCKS, tq), BF16)],
        compiler_params=_params(("parallel", "parallel", "arbitrary")),
        name="nsa",
    )(qt, kcmp, vcmpt, ksa, kw, vst, vwt, gate, ovlt)


def _route(logits, group=None):
    lane = lax.broadcasted_iota(jnp.int32, logits.shape, 1).astype(F32)
    far = float(LANES)
    is_g = lane < N_EXP_GROUPS
    glog = jnp.where(is_g, logits, -jnp.inf)
    gmax = jnp.max(glog, axis=1, keepdims=True)
    gsum = jnp.sum(jnp.where(is_g, jnp.exp(logits - gmax), 0.0), axis=1, keepdims=True)
    gsel = jnp.min(jnp.where(glog == gmax, lane, far), axis=1, keepdims=True)
    gprob = 1.0 / gsum
    lo = ROUTER_OFF + EXPERTS_PER_GROUP * (gsel if group is None else group)
    in_e = (lane >= lo) & (lane < lo + EXPERTS_PER_GROUP)
    emax = jnp.max(jnp.where(in_e, logits, -jnp.inf), axis=1, keepdims=True)
    eexp = jnp.where(in_e, jnp.exp(logits - emax), 0.0)
    eprob = jnp.where(in_e, eexp / jnp.sum(eexp, axis=1, keepdims=True), -1.0)
    v1 = jnp.max(eprob, axis=1, keepdims=True)
    i1 = jnp.min(jnp.where(eprob == v1, lane, far), axis=1, keepdims=True)
    rest = jnp.where(lane == i1, -1.0, eprob)
    v2 = jnp.max(rest, axis=1, keepdims=True)
    i2 = jnp.min(jnp.where(rest == v2, lane, far), axis=1, keepdims=True)
    den = v1 + v2
    comb = jnp.where(lane == i1, v1 / den * gprob, 0.0) + jnp.where(lane == i2, v2 / den * gprob, 0.0)
    return gsel, comb


def _out_proj_kernel(tm, ys_ref, yat_ref, x_ref, wglu_ref, bglu_ref, gs_ref, ga_ref, wo_ref, g2_ref, wr_ref, br_ref,
                     xt_ref, gsel_ref):
    y = _gelu(ys_ref[...])
    y = y * _sigmoid(_dot(y.astype(BF16), wglu_ref[...]) + bglu_ref[...])
    ysn = y * lax.rsqrt(jnp.mean(y * y, axis=-1, keepdims=True) + EPS) * gs_ref[...]
    yat = yat_ref[...]
    yant = yat * lax.rsqrt(jnp.mean(yat * yat, axis=0, keepdims=True) + EPS) * ga_ref[...]
    yan = yant.T
    x2 = x_ref[...] + _dot(ysn.astype(BF16), wo_ref[:D_SSM, :]) + _dot(yan.astype(BF16), wo_ref[D_SSM:, :])
    for c in range(TT_ROWS):
        xt_ref[pl.ds(c, tm, stride=TT_ROWS), :] = x2[:, c * LANES:(c + 1) * LANES]
    h2 = (x2 * lax.rsqrt(jnp.mean(x2 * x2, axis=-1, keepdims=True) + EPS) * g2_ref[...]).astype(BF16)
    gsel, _ = _route(_dot(h2, wr_ref[...]) + br_ref[...])
    gsel_ref[...] = jnp.broadcast_to(gsel, gsel_ref.shape)


def _out_proj(ys, yat, x2d, wglu, bglu, gs, ga, wo, g2, wr, br, *, seq):
    n_tok = x2d.shape[0]
    tm = PROJ_TM
    nl = seq // tm
    row = lambda i: (i, 0)
    const = lambda i: (0, 0)
    return pl.pallas_call(
        functools.partial(_out_proj_kernel, tm),
        grid=(n_tok // tm,),
        in_specs=[
            pl.BlockSpec((tm, D_SSM), row),
            pl.BlockSpec((None, D_ATT, tm), lambda i: (i // nl, 0, i % nl)),
            pl.BlockSpec((tm, D_MODEL), row),
            pl.BlockSpec((D_SSM, D_SSM), const),
            pl.BlockSpec((1, D_SSM), const),
            pl.BlockSpec((1, D_SSM), const),
            pl.BlockSpec((D_ATT, 1), const),
            pl.BlockSpec((D_SSM + D_ATT, D_MODEL), const),
            pl.BlockSpec((1, D_MODEL), const),
            pl.BlockSpec((D_MODEL, LANES), const),
            pl.BlockSpec((1, LANES), const),
        ],
        out_specs=[pl.BlockSpec((tm * TT_ROWS, LANES), row), pl.BlockSpec((tm, LANES), row)],
        out_shape=[jax.ShapeDtypeStruct((n_tok * TT_ROWS, LANES), F32), jax.ShapeDtypeStruct((n_tok, LANES), F32)],
        compiler_params=_params(("parallel",)),
        name="out_proj",
    )(ys, yat, x2d, wglu, bglu, gs, ga, wo, g2, wr, br)


def _moe_plan(gsel, n_tok):
    tmx = MOE_TM
    n_tiles = n_tok // tmx + N_EXP_GROUPS
    oh = (gsel[:, None] == jnp.arange(N_EXP_GROUPS)[None, :]).astype(jnp.int32)
    csum = jnp.cumsum(oh, axis=0)
    counts = csum[-1]
    rank = jnp.sum(csum * oh, axis=1) - 1
    nt = (counts + tmx - 1) // tmx
    tend = jnp.cumsum(nt)
    toff = tend - nt
    pos = jnp.sum(oh * toff[None, :], axis=1) * tmx + rank
    tile = jnp.arange(n_tiles)
    grp = jnp.minimum(jnp.sum((tile[:, None] >= tend[None, :]).astype(jnp.int32), axis=1), N_EXP_GROUPS - 1)
    nval = jnp.clip(counts[grp] - (tile - toff[grp]) * tmx, 0, tmx)
    return grp.astype(jnp.int32), nval.astype(jnp.int32), pos.astype(jnp.int32)


def _moe_kernel(grp_ref, nval_ref, pos_ref, x_hbm, g2_ref, wr_ref, br_ref, wg_ref, wu_ref, wd_ref, o_hbm,
                xbuf, obuf, abuf, tok_ref, gsem, ssem, *, tmx, n_tiles, n_tok):
    i = pl.program_id(0)
    slot = i % 2

    @pl.when(i == 0)
    def _():
        def place(t, c):
            tok_ref[pos_ref[t]] = t
            return c

        lax.fori_loop(0, n_tok, place, 0, unroll=8)

        def pad_tile(tile, c):
            def pad_row(r, c2):
                tok_ref[tile * tmx + r] = 0
                return c2

            return lax.fori_loop(nval_ref[tile], tmx, pad_row, c)

        lax.fori_loop(0, n_tiles, pad_tile, 0)

    def gather_row(tile, r, dst_slot):
        t = tok_ref[tile * tmx + r]
        return pltpu.make_async_copy(x_hbm.at[pl.ds(pl.multiple_of(t * TT_ROWS, TT_ROWS), TT_ROWS), :],
                                     xbuf.at[dst_slot, pl.ds(r * TT_ROWS, TT_ROWS), :], gsem.at[dst_slot])

    def scatter_row(tile, r, src_slot):
        t = tok_ref[tile * tmx + r]
        return pltpu.make_async_copy(obuf.at[src_slot, pl.ds(r, 1), :], o_hbm.at[pl.ds(t, 1), :], ssem.at[src_slot])

    def gather_wait(src_slot):
        pltpu.make_async_copy(x_hbm.at[pl.ds(0, tmx * TT_ROWS), :], xbuf.at[src_slot], gsem.at[src_slot]).wait()

    def scatter_wait(n, src_slot):
        n8 = pl.multiple_of((n // SUBLANES) * SUBLANES, SUBLANES)

        @pl.when(n8 > 0)
        def _():
            pltpu.make_async_copy(obuf.at[src_slot, pl.ds(0, n8), :], o_hbm.at[pl.ds(0, n8), :],
                                  ssem.at[src_slot]).wait()

        def one(r, c):
            pltpu.make_async_copy(obuf.at[src_slot, pl.ds(0, 1), :], o_hbm.at[pl.ds(0, 1), :],
                                  ssem.at[src_slot]).wait()
            return c

        lax.fori_loop(0, n - n8, one, 0)

    @pl.when(i == 0)
    def _():
        def body(r, c):
            gather_row(0, r, 0).start()
            return c

        lax.fori_loop(0, tmx, body, 0)

    nv = nval_ref[i]
    nv_prev = nval_ref[jnp.maximum(i - 1, 0)]

    @pl.when((i == 0) | (nv_prev > 0))
    def _():
        gather_wait(slot)

    @pl.when(i >= 2)
    def _():
        scatter_wait(nval_ref[i - 2], slot)

    nxt = jnp.minimum(i + 1, n_tiles - 1)
    per = tmx // EXPERTS_PER_GROUP
    prev_full = (i >= 1) & (nv_prev == tmx)

    def experts(with_scatter):
        x2 = jnp.concatenate([xbuf[slot, pl.ds(c, tmx, stride=TT_ROWS), :] for c in range(TT_ROWS)], axis=1)
        h = (x2 * lax.rsqrt(jnp.mean(x2 * x2, axis=-1, keepdims=True) + EPS) * g2_ref[...]).astype(BF16)
        _, cw = _route(_dot(h, wr_ref[...]) + br_ref[...], group=grp_ref[i].astype(F32))
        lane = lax.broadcasted_iota(jnp.int32, cw.shape, 1)
        first = ROUTER_OFF + EXPERTS_PER_GROUP * grp_ref[i]
        for k in range(EXPERTS_PER_GROUP):
            for r in range(k * per, (k + 1) * per):
                gather_row(nxt, r, 1 - slot).start(priority=r % 2)
                if with_scatter:
                    scatter_row(i - 1, r, 1 - slot).start(priority=r % 2)
            gate = _dot(h, wg_ref[k].astype(BF16))
            up = _dot(h, wu_ref[k].astype(BF16))
            ck = jnp.sum(jnp.where(lane == first + k, cw, 0.0), axis=1, keepdims=True)
            abuf[:, k * D_EXPERT:(k + 1) * D_EXPERT] = (gate * _sigmoid(gate) * up * ck).astype(BF16)
        obuf[slot] = x2 + _dot(abuf[...], wd_ref[...].astype(BF16))

    @pl.when(prev_full & (nv > 0))
    def _():
        experts(True)

    @pl.when(jnp.logical_not(prev_full) & (nv > 0))
    def _():
        experts(False)

    @pl.when(prev_full & (nv == 0))
    def _():
        def body(r, c):
            scatter_row(i - 1, r, 1 - slot).start()
            return c

        lax.fori_loop(0, tmx, body, 0)

    @pl.when(nv < tmx)
    def _():
        def body(r, c):
            scatter_row(i, r, slot).start()
            return c

        lax.fori_loop(0, nv, body, 0)

    @pl.when(i == n_tiles - 1)
    def _():
        @pl.when(nv == tmx)
        def _():
            def body(r, c):
                scatter_row(i, r, slot).start()
                return c

            lax.fori_loop(0, tmx, body, 0)

        @pl.when(nv > 0)
        def _():
            gather_wait(1 - slot)

        scatter_wait(nv_prev, 1 - slot)
        scatter_wait(nv, slot)


def _moe(x2t, grp, nval, pos, g2, wr, br, wg, wu, wd):
    n_tok = x2t.shape[0] // TT_ROWS
    tmx = MOE_TM
    n_tiles = grp.shape[0]
    kern = functools.partial(_moe_kernel, tmx=tmx, n_tiles=n_tiles, n_tok=n_tok)
    gk = EXPERTS_PER_GROUP * D_EXPERT
    w_bytes = 2 * 3 * EXPERTS_PER_GROUP * D_MODEL * D_EXPERT * 4
    io_bytes = 2 * tmx * (TT_ROWS * LANES + D_MODEL) * 4 + tmx * gk * 2
    vmem_limit = w_bytes + io_bytes + 6 * 1024 * 1024
    assert vmem_limit < V7X_VMEM_BYTES
    grid_spec = pltpu.PrefetchScalarGridSpec(
        num_scalar_prefetch=3,
        grid=(n_tiles,),
        in_specs=[
            pl.BlockSpec(memory_space=pl.ANY),
            pl.BlockSpec((1, D_MODEL), lambda i, g, n, t: (0, 0)),
            pl.BlockSpec((D_MODEL, LANES), lambda i, g, n, t: (0, 0)),
            pl.BlockSpec((1, LANES), lambda i, g, n, t: (0, 0)),
            pl.BlockSpec((None, EXPERTS_PER_GROUP, D_MODEL, D_EXPERT), lambda i, g, n, t: (g[i], 0, 0, 0)),
            pl.BlockSpec((None, EXPERTS_PER_GROUP, D_MODEL, D_EXPERT), lambda i, g, n, t: (g[i], 0, 0, 0)),
            pl.BlockSpec((None, gk, D_MODEL), lambda i, g, n, t: (g[i], 0, 0)),
        ],
        out_specs=pl.BlockSpec(memory_space=pl.ANY),
        scratch_shapes=[
            pltpu.VMEM((2, tmx * TT_ROWS, LANES), F32),
            pltpu.VMEM((2, tmx, D_MODEL), F32),
            pltpu.VMEM((tmx, gk), BF16),
            pltpu.SMEM((n_tiles * tmx,), jnp.int32),
            pltpu.SemaphoreType.DMA((2,)),
            pltpu.SemaphoreType.DMA((2,)),
        ],
    )
    return pl.pallas_call(
        kern,
        grid_spec=grid_spec,
        out_shape=jax.ShapeDtypeStruct((n_tok, D_MODEL), F32),
        compiler_params=_params(("arbitrary",), vmem_limit),
        name="moe",
    )(grp, nval, pos, x2t, g2, wr, br, wg, wu, wd)


def _block_diag_ones(n, blk):
    i = jnp.arange(n) // blk
    return (i[:, None] == i[None, :]).astype(BF16)


def _layer(x, norm1_g, w_in, lam_re, lam_im, log_step, b_re, b_im, c_re, c_im, d_skip,
           w_glu, b_glu, g_q, g_kc, g_ks, g_kw, pos_k, pos_v, w_ck1, w_ck2, w_cv1, w_cv2,
           out_g_ssm, out_g_att, w_out, norm2_g, w_grp, b_grp, w_exp, b_exp, w_gate, w_up, w_down):
    bsz, seq, _ = x.shape
    assert seq % PROJ_TM == 0 and seq // SEL_BLOCK <= MAX_SEL_BLOCKS
    n_tok = bsz * seq
    x2d = x.reshape(n_tok, D_MODEL)
    q8 = SSM_Q
    n_sub = seq // q8
    nch = seq // CMP_STRIDE

    o_q = D_SSM
    o_kv = D_SSM + D_ATT
    o_gt = o_kv + 6 * D_KV
    kv = lambda i: w_in[:, o_kv + i * D_KV:o_kv + (i + 1) * D_KV]
    wrow = jnp.concatenate([w_in[:, :o_q], kv(0), kv(1), kv(2), kv(4)], axis=1).astype(BF16)
    per_g = HPG * N_BRANCH
    wgt = jnp.zeros((D_MODEL, N_KV * GATE_ROWS), F32)
    for g in range(N_KV):
        wgt = wgt.at[:, g * GATE_ROWS:g * GATE_ROWS + per_g].set(w_in[:, o_gt + g * per_g:o_gt + (g + 1) * per_g])
    wcol = jnp.concatenate([w_in[:, o_q:o_kv], kv(3), kv(5), wgt], axis=1).T.astype(BF16)
    qscale = (HEAD_DIM ** -0.5) * math.log2(math.e)
    gq = (jnp.tile(g_q.astype(F32), N_HEADS) * qscale).reshape(D_ATT, 1)
    gks = jnp.tile(g_ks.astype(F32), N_KV).reshape(1, D_KV)
    gkw = jnp.tile(g_kw.astype(F32), N_KV).reshape(1, D_KV)

    u, qt, kc, vc, ksa, kw, vst, vwt, gate = _in_proj(
        x2d, norm1_g.reshape(1, D_MODEL), wrow, wcol, gq, gks, gkw,
        _block_diag_ones(D_KV, HEAD_DIM), bsz=bsz, seq=seq)

    w_c, t_c, m_c, e_state_t, e_lane_t, pw_re, pw_im, dvec = _s5_weights(
        lam_re, lam_im, log_step, b_re, b_im, c_re, c_im, d_skip, n_sub)
    ys = _s5(u.reshape(bsz, seq, D_SSM), w_c, t_c, m_c, e_state_t, e_lane_t, pw_re, pw_im, dvec,
             bsz=bsz, n_sub=n_sub).reshape(n_tok, D_SSM)

    wide = CMP_STRIDE * HEAD_DIM
    pad8 = lambda p: jnp.zeros((SUBLANES, 2 * wide), F32).at[0].set(p.reshape(-1)).astype(BF16)
    kcmp, vcmpt = _compress(
        kc, vc,
        w_ck1.astype(BF16), w_ck2.astype(BF16), w_cv1.astype(BF16), w_cv2.T.astype(BF16),
        pad8(pos_k), pad8(pos_v), g_kc.astype(F32).reshape(1, HEAD_DIM), bsz=bsz, nch=nch)
    cstart = jnp.arange(nch) * CMP_STRIDE
    sstart = jnp.arange(MAX_SEL_BLOCKS) * SEL_BLOCK
    ovlt = ((cstart[None, :] < sstart[:, None] + SEL_BLOCK) & (cstart[None, :] + CMP_BLOCK > sstart[:, None])
            & (jnp.arange(MAX_SEL_BLOCKS)[:, None] < seq // SEL_BLOCK)
            & (jnp.arange(nch)[None, :] < nch - 1)).astype(BF16)
    yat = _nsa(qt, kcmp, vcmpt, ksa, kw, vst, vwt, gate, ovlt, bsz=bsz, seq=seq, nch=nch)

    wr = jnp.zeros((D_MODEL, LANES), F32)
    wr = wr.at[:, :N_EXP_GROUPS].set(w_grp).at[:, ROUTER_OFF:ROUTER_OFF + N_EXPERTS].set(w_exp).astype(BF16)
    br = jnp.zeros((1, LANES), F32)
    br = br.at[0, :N_EXP_GROUPS].set(b_grp).at[0, ROUTER_OFF:ROUTER_OFF + N_EXPERTS].set(b_exp)
    g2 = norm2_g.reshape(1, D_MODEL).astype(F32)
    x2t, gsel = _out_proj(
        ys, yat, x2d, w_glu.astype(BF16), b_glu.reshape(1, D_SSM).astype(F32),
        out_g_ssm.reshape(1, D_SSM).astype(F32), out_g_att.reshape(D_ATT, 1).astype(F32),
        w_out.astype(BF16), g2, wr, br, seq=seq)

    grp, nval, pos = _moe_plan(gsel[:, 0].astype(jnp.int32), n_tok)
    gshape = (N_EXP_GROUPS, EXPERTS_PER_GROUP, D_MODEL, D_EXPERT)
    out = _moe(x2t, grp, nval, pos, g2, wr, br, w_gate.reshape(gshape), w_up.reshape(gshape),
               w_down.reshape(N_EXP_GROUPS, EXPERTS_PER_GROUP * D_EXPERT, D_MODEL))
    return out.reshape(bsz, seq, D_MODEL)


def kernel(x, norm1_g, w_in, lam_re, lam_im, log_step, b_re, b_im, c_re, c_im, d_skip, w_glu, b_glu, g_q, g_kc, g_ks, g_kw, pos_k, pos_v, w_ck1, w_ck2, w_cv1, w_cv2, out_g_ssm, out_g_att, w_out, norm2_g, w_grp, b_grp, w_exp, b_exp, w_gate, w_up, w_down):
    depth = norm1_g.shape[0]
    for l in range(depth):
        x = _layer(x, norm1_g[l], w_in[l], lam_re[l], lam_im[l], log_step[l], b_re[l], b_im[l], c_re[l],
                   c_im[l], d_skip[l], w_glu[l], b_glu[l], g_q[l], g_kc[l], g_ks[l], g_kw[l], pos_k[l],
                   pos_v[l], w_ck1[l], w_ck2[l], w_cv1[l], w_cv2[l], out_g_ssm[l], out_g_att[l], w_out[l],
                   norm2_g[l], w_grp[l], b_grp[l], w_exp[l], b_exp[l], w_gate[l], w_up[l], w_down[l])
    return x
```

```python
import functools
import math

import jax
import jax.numpy as jnp
from jax import lax
from jax.experimental import pallas as pl
from jax.experimental.pallas import tpu as pltpu

D_MODEL = 1024
D_SSM = 512
SSM_CH = 16
SSM_STATE = 64
D_ATT = 512
HEAD_DIM = 64
N_HEADS = D_ATT // HEAD_DIM
N_KV = 2
HPG = N_HEADS // N_KV
D_KV = N_KV * HEAD_DIM
N_BRANCH = 3
CMP_STRIDE = 16
CMP_BLOCK = 2 * CMP_STRIDE
CMP_HIDDEN = 256
SEL_BLOCK = 64
N_SELECT = 16
WINDOW = 512
N_EXP_GROUPS = 4
EXPERTS_PER_GROUP = 8
N_EXPERTS = N_EXP_GROUPS * EXPERTS_PER_GROUP
D_EXPERT = 256
EPS = 1e-6
NEG = -1e30
FORCE = 1e9

LANES = 128
SUBLANES = 8
SSM_Q = 8
SSM_LT = D_SSM // LANES
ROUTER_OFF = N_EXP_GROUPS
NSA_TQ = 256
NSA_TK = 256
NSA_UNROLL = 8
V_ROWS = HEAD_DIM + 16
MAX_SEL_BLOCKS = 64
MOE_TM = 256
TT_ROWS = D_MODEL // LANES
PROJ_TM = 1024
GATE_ROWS = 16
V7X_VMEM_BYTES = 64 * 1024 * 1024
VMEM_LIMIT = V7X_VMEM_BYTES - 8 * 1024 * 1024

F32 = jnp.float32
BF16 = jnp.bfloat16


def _dot(a, b):
    return jnp.dot(a, b, preferred_element_type=F32)


def _dot_nt(a, b):
    return lax.dot_general(a, b, (((1,), (1,)), ((), ())), preferred_element_type=F32)


def _split_dot(x, w):
    hi = x.astype(BF16)
    lo = (x - hi.astype(F32)).astype(BF16)
    return _dot(hi, w) + _dot(lo, w)


def _gelu(x):
    c = math.sqrt(2.0 / math.pi)
    return 0.5 * x * (1.0 + jnp.tanh(c * (x + 0.044715 * (x * x * x))))


def _sigmoid(x):
    return 1.0 / (1.0 + jnp.exp(-x))


def _params(sem, vmem_limit=VMEM_LIMIT):
    return pltpu.CompilerParams(dimension_semantics=sem, vmem_limit_bytes=vmem_limit)


def _in_proj_kernel(x_ref, g1_ref, wrow_ref, wcol_ref, gq_ref, gks_ref, gkw_ref, bd128_ref,
                    u_ref, qt_ref, kc_ref, vc_ref, ksa_ref, kw_ref, vst_ref, vwt_ref, gate_ref, *, tm, nl):
    x = x_ref[...]
    ms = jnp.mean(x * x, axis=-1, keepdims=True)
    hn = (x * lax.rsqrt(ms + EPS) * g1_ref[...]).astype(BF16)

    pr = _dot(hn, wrow_ref[...])
    u_ref[...] = pr[:, :D_SSM]
    kc, vc, ks, kw = [pr[:, D_SSM + i * D_KV:D_SSM + (i + 1) * D_KV] for i in range(4)]
    kss = _split_dot(ks * ks, bd128_ref[...])
    ksn = ks * lax.rsqrt(kss * (1.0 / HEAD_DIM) + EPS) * gks_ref[...]
    kws = _split_dot(kw * kw, bd128_ref[...])
    kwn = kw * lax.rsqrt(kws * (1.0 / HEAD_DIM) + EPS) * gkw_ref[...]
    t0 = (pl.program_id(0) % nl) * tm
    tpos = t0 + lax.broadcasted_iota(jnp.int32, (tm, MAX_SEL_BLOCKS), 0)
    blk = lax.broadcasted_iota(jnp.int32, (tm, MAX_SEL_BLOCKS), 1)
    onehot = jnp.where(tpos // SEL_BLOCK == blk, 1.0, 0.0).astype(BF16)
    for g in range(N_KV):
        sl = slice(g * HEAD_DIM, (g + 1) * HEAD_DIM)
        kc_ref[g] = kc[:, sl]
        vc_ref[g] = vc[:, sl]
        ksa_ref[g] = jnp.concatenate([ksn[:, sl].astype(BF16), onehot], axis=1)
        kw_ref[g] = kwn[:, sl].astype(BF16)

    pc = _dot_nt(wcol_ref[...], hn)
    gq = gq_ref[...]
    for h in range(N_HEADS):
        sl = slice(h * HEAD_DIM, (h + 1) * HEAD_DIM)
        qh = pc[sl]
        ss = jnp.sum(qh * qh, axis=0, keepdims=True)
        qt_ref[h] = (qh * lax.rsqrt(ss * (1.0 / HEAD_DIM) + EPS) * gq[sl]).astype(BF16)
    ones_rows = jnp.where(lax.broadcasted_iota(jnp.int32, (V_ROWS - HEAD_DIM, tm), 0) == 0, 1.0, 0.0)
    for g in range(N_KV):
        for o_ref, base in ((vst_ref, D_ATT), (vwt_ref, D_ATT + D_KV)):
            vt = jnp.concatenate([pc[base + g * HEAD_DIM:base + (g + 1) * HEAD_DIM], ones_rows], axis=0)
            vt = vt.astype(BF16)
            for j in range(tm // NSA_TK):
                o_ref[g, j] = vt[:, j * NSA_TK:(j + 1) * NSA_TK]
        gb = D_ATT + 2 * D_KV + g * GATE_ROWS
        gate_ref[g] = _sigmoid(pc[gb:gb + GATE_ROWS])


def _in_proj(x2d, g1, wrow, wcol, gq, gks, gkw, bd128, *, bsz, seq):
    tm = PROJ_TM
    nl = seq // tm
    n_tok = bsz * seq
    kern = functools.partial(_in_proj_kernel, tm=tm, nl=nl)
    row = lambda i: (i, 0)
    const = lambda i: (0, 0)
    bgl = lambda i: (i // nl, 0, i % nl, 0)
    n_col = wcol.shape[0]
    jt = tm // NSA_TK

    def kvspec(width):
        return pl.BlockSpec((None, N_KV, tm, width), bgl)

    def kvshape(width, dtype=BF16):
        return jax.ShapeDtypeStruct((bsz, N_KV, seq, width), dtype)

    vt_spec = pl.BlockSpec((None, N_KV, jt, V_ROWS, NSA_TK), lambda i: (i // nl, 0, i % nl, 0, 0))
    vt_shape = jax.ShapeDtypeStruct((bsz, N_KV, seq // NSA_TK, V_ROWS, NSA_TK), BF16)
    return pl.pallas_call(
        kern,
        grid=(n_tok // tm,),
        in_specs=[
            pl.BlockSpec((tm, D_MODEL), row),
            pl.BlockSpec((1, D_MODEL), const),
            pl.BlockSpec((D_MODEL, D_SSM + 4 * D_KV), const),
            pl.BlockSpec((n_col, D_MODEL), const),
            pl.BlockSpec((D_ATT, 1), const),
            pl.BlockSpec((1, D_KV), const),
            pl.BlockSpec((1, D_KV), const),
            pl.BlockSpec((D_KV, D_KV), const),
        ],
        out_specs=[
            pl.BlockSpec((tm, D_SSM), row),
            pl.BlockSpec((None, N_HEADS, HEAD_DIM, tm), lambda i: (i // nl, 0, 0, i % nl)),
            kvspec(HEAD_DIM), kvspec(HEAD_DIM), kvspec(2 * HEAD_DIM), kvspec(HEAD_DIM),
            vt_spec, vt_spec,
            pl.BlockSpec((None, N_KV, GATE_ROWS, tm), lambda i: (i // nl, 0, 0, i % nl)),
        ],
        out_shape=[
            jax.ShapeDtypeStruct((n_tok, D_SSM), F32),
            jax.ShapeDtypeStruct((bsz, N_HEADS, HEAD_DIM, seq), BF16),
            kvshape(HEAD_DIM, F32), kvshape(HEAD_DIM, F32), kvshape(2 * HEAD_DIM), kvshape(HEAD_DIM),
            vt_shape, vt_shape,
            jax.ShapeDtypeStruct((bsz, N_KV, GATE_ROWS, seq), F32),
        ],
        compiler_params=_params(("parallel",)),
        name="in_proj",
    )(x2d, g1, wrow, wcol, gq, gks, gkw, bd128)


def _s5_weights(lam_re, lam_im, log_step, b_re, b_im, c_re, c_im, d_skip, n_sub):
    q = SSM_Q
    lam = lax.complex(lam_re.astype(F32), lam_im.astype(F32))
    step = jnp.exp(log_step.astype(F32))[:, None]
    lam_bar = jnp.exp(lam * step)
    b_bar = ((lam_bar - 1.0) / lam)[..., None] * lax.complex(b_re.astype(F32), b_im.astype(F32))
    c = lax.complex(c_re.astype(F32), c_im.astype(F32))
    pows = [jnp.ones_like(lam_bar)]
    for _ in range(q):
        pows.append(pows[-1] * lam_bar)
    pw = jnp.stack(pows)
    lt, a8 = SSM_LT, LANES // SSM_CH
    hp = a8 * SSM_STATE
    e_lane = (jnp.arange(a8)[:, None] == jnp.arange(LANES)[None, :] // SSM_CH).astype(F32)
    e_state = (jnp.arange(a8)[:, None] == jnp.arange(hp)[None, :] // SSM_STATE).astype(F32)
    e_lane_t = jnp.tile(e_lane, (1, q))
    e_state_t = jnp.tile(e_state, (1, 2))

    kk = jnp.real(jnp.einsum('ghp,kgp,gpi->kghi', c, pw[:q], b_bar))
    km = kk.reshape(q, lt, a8, SSM_CH, SSM_CH).transpose(1, 4, 0, 2, 3).reshape(lt, SSM_CH, q, LANES)
    lag = jnp.arange(q)[None, :] - jnp.arange(q)[:, None]
    kg = km[:, :, jnp.clip(lag, 0, q - 1), :] * (lag >= 0)[None, None, :, :, None].astype(F32)
    kc = kg.transpose(0, 2, 1, 3, 4).reshape(lt, q, 1, SSM_CH, q * LANES)
    t_c = kc.reshape(lt, q * SSM_CH, q * LANES)

    wc = pw[q - 1 - jnp.arange(q)][..., None] * b_bar[None]
    wri = jnp.stack([jnp.real(wc), jnp.imag(wc)])
    wm = (wri.reshape(2, q, lt, a8, SSM_STATE, SSM_CH).transpose(2, 1, 5, 0, 3, 4)
          .reshape(lt, q, 1, SSM_CH, 2 * hp))
    w_c = wm.reshape(lt, q * SSM_CH, 2 * hp)

    cl = c[None] * pw[1:q + 1][:, :, None, :]
    cri = jnp.stack([jnp.real(cl), -jnp.imag(cl)])
    mm = (cri.reshape(2, q, lt, a8, SSM_CH, SSM_STATE).transpose(2, 0, 5, 1, 3, 4)
          .reshape(lt, 2, 1, SSM_STATE, q * LANES))
    m_c = mm.reshape(lt, 2 * SSM_STATE, q * LANES)

    n_lvl = max(1, (n_sub - 1).bit_length())
    lv = [pw[q]]
    for _ in range(n_lvl - 1):
        lv.append(lv[-1] * lv[-1])
    lvs = jnp.stack(lv).reshape(n_lvl, lt, 1, hp)
    pw_re = jnp.real(lvs).transpose(1, 0, 2, 3)
    pw_im = jnp.imag(lvs).transpose(1, 0, 2, 3)
    dvec = jnp.tile(d_skip.astype(F32).reshape(lt, 1, LANES), (1, 1, q))
    return w_c, t_c, m_c, e_state_t, e_lane_t, pw_re, pw_im, dvec


def _s5_kernel(u_ref, wc_ref, tc_ref, mc_ref, es_ref, el_ref, pwr_ref, pwi_ref, d_ref, y_ref, w_ref, t_ref, m_ref,
               *, n_sub, n_lvl):
    half = (LANES // SSM_CH) * SSM_STATE
    q = SSM_Q
    a8 = LANES // SSM_CH

    @pl.when(pl.program_id(1) == 0)
    def _():
        for a in range(a8):
            el = el_ref[a:a + 1, :]
            es = es_ref[a:a + 1, :]
            for s_ in range(q):
                rows = pl.ds((s_ * a8 + a) * SSM_CH, SSM_CH)
                src = pl.ds(s_ * SSM_CH, SSM_CH)
                t_ref[rows, :] = (tc_ref[src, :] * el).astype(BF16)
                w_ref[rows, :] = (wc_ref[src, :] * es).astype(BF16)
            for r in range(2):
                rows = pl.ds((r * a8 + a) * SSM_STATE, SSM_STATE)
                m_ref[rows, :] = (mc_ref[pl.ds(r * SSM_STATE, SSM_STATE), :] * el).astype(BF16)

    u = jnp.concatenate([u_ref[pl.ds(s, n_sub, stride=q), :] for s in range(q)], axis=1)
    ub = u.astype(BF16)
    s_loc = _dot(ub, w_ref[...])
    re = s_loc[:, :half]
    im = s_loc[:, half:]
    rowi = lax.broadcasted_iota(jnp.int32, (n_sub, half), 0)
    for k in range(n_lvl):
        d = 1 << k
        ar = pwr_ref[k]
        ai = pwi_ref[k]
        keep = rowi >= d
        sre = jnp.where(keep, pltpu.roll(re, d, axis=0), 0.0)
        sim = jnp.where(keep, pltpu.roll(im, d, axis=0), 0.0)
        re, im = re + (ar * sre - ai * sim), im + (ar * sim + ai * sre)
    keep = rowi >= 1
    xre = jnp.where(keep, pltpu.roll(re, 1, axis=0), 0.0)
    xim = jnp.where(keep, pltpu.roll(im, 1, axis=0), 0.0)
    xst = jnp.concatenate([xre, xim], axis=1).astype(BF16)
    y = _dot(ub, t_ref[...]) + _dot(xst, m_ref[...]) + d_ref[...] * u
    for j in range(q):
        y_ref[pl.ds(j, n_sub, stride=q), :] = y[:, j * LANES:(j + 1) * LANES]


def _s5(u, w_c, t_c, m_c, e_state_t, e_lane_t, pw_re, pw_im, dvec, *, bsz, n_sub):
    q = SSM_Q
    n_lvl = pw_re.shape[1]
    kern = functools.partial(_s5_kernel, n_sub=n_sub, n_lvl=n_lvl)
    wide = q * LANES
    seq = n_sub * q
    return pl.pallas_call(
        kern,
        grid=(SSM_LT, bsz),
        in_specs=[
            pl.BlockSpec((None, seq, LANES), lambda l, b: (b, 0, l)),
            pl.BlockSpec((None, q * SSM_CH, wide), lambda l, b: (l, 0, 0)),
            pl.BlockSpec((None, q * SSM_CH, wide), lambda l, b: (l, 0, 0)),
            pl.BlockSpec((None, 2 * SSM_STATE, wide), lambda l, b: (l, 0, 0)),
            pl.BlockSpec((LANES // SSM_CH, wide), lambda l, b: (0, 0)),
            pl.BlockSpec((LANES // SSM_CH, wide), lambda l, b: (0, 0)),
            pl.BlockSpec((None, n_lvl, 1, wide // 2), lambda l, b: (l, 0, 0, 0)),
            pl.BlockSpec((None, n_lvl, 1, wide // 2), lambda l, b: (l, 0, 0, 0)),
            pl.BlockSpec((None, 1, wide), lambda l, b: (l, 0, 0)),
        ],
        out_specs=pl.BlockSpec((None, seq, LANES), lambda l, b: (b, 0, l)),
        out_shape=jax.ShapeDtypeStruct((bsz, seq, D_SSM), F32),
        scratch_shapes=[pltpu.VMEM((wide, wide), BF16)] * 3,
        compiler_params=_params(("arbitrary", "arbitrary")),
        name="s5",
    )(u, w_c, t_c, m_c, e_state_t, e_lane_t, pw_re, pw_im, dvec)


def _compress_kernel(kc_ref, vc_ref, w1k_ref, w2k_ref, w1v_ref, w2vt_ref, posk_ref, posv_ref, gkc_ref,
                     kcmp_ref, vcmpt_ref, *, nch):
    half = CMP_STRIDE * HEAD_DIM

    def hidden(x_ref, w1_ref, pos_ref):
        x = jnp.concatenate([x_ref[pl.ds(j, nch, stride=CMP_STRIDE), :] for j in range(CMP_STRIDE)],
                            axis=1).astype(BF16)
        a = _dot(x, w1_ref[:half, :])
        b = _dot(x, w1_ref[half:, :])
        pv = _dot(pos_ref[...], w1_ref[...])[0:1, :]
        hid = a + pltpu.roll(b, nch - 1, axis=0) + pv
        return _gelu(hid).astype(BF16)

    k = _dot(hidden(kc_ref, w1k_ref, posk_ref), w2k_ref[...])
    ms = jnp.mean(k * k, axis=-1, keepdims=True)
    kcmp_ref[...] = (k * lax.rsqrt(ms + EPS) * gkc_ref[...]).astype(BF16)
    vt = _dot_nt(w2vt_ref[...], hidden(vc_ref, w1v_ref, posv_ref))
    coli = lax.broadcasted_iota(jnp.int32, vt.shape, 1)
    vcmpt_ref[...] = jnp.where(coli < nch - 1, vt, 0.0).astype(BF16)


def _compress(kcf, vcf, w1k, w2k, w1v, w2vt, posk, posv, gkc, *, bsz, nch):
    kern = functools.partial(_compress_kernel, nch=nch)
    wide = CMP_STRIDE * HEAD_DIM
    xspec = pl.BlockSpec((None, None, nch * CMP_STRIDE, HEAD_DIM), lambda b, g: (b, g, 0, 0))
    c2 = lambda b, g: (0, 0)
    return pl.pallas_call(
        kern,
        grid=(bsz, N_KV),
        in_specs=[
            xspec, xspec,
            pl.BlockSpec((2 * wide, CMP_HIDDEN), c2), pl.BlockSpec((CMP_HIDDEN, HEAD_DIM), c2),
            pl.BlockSpec((2 * wide, CMP_HIDDEN), c2), pl.BlockSpec((HEAD_DIM, CMP_HIDDEN), c2),
            pl.BlockSpec((SUBLANES, 2 * wide), c2), pl.BlockSpec((SUBLANES, 2 * wide), c2),
            pl.BlockSpec((1, HEAD_DIM), c2),
        ],
        out_specs=[pl.BlockSpec((None, None, nch, HEAD_DIM), lambda b, g: (b, g, 0, 0)),
                   pl.BlockSpec((None, None, HEAD_DIM, nch), lambda b, g: (b, g, 0, 0))],
        out_shape=[jax.ShapeDtypeStruct((bsz, N_KV, nch, HEAD_DIM), BF16),
                   jax.ShapeDtypeStruct((bsz, N_KV, HEAD_DIM, nch), BF16)],
        compiler_params=_params(("parallel", "parallel")),
        name="compress",
    )(kcf, vcf, w1k, w2k, w1v, w2vt, posk, posv, gkc)


def _nsa_kernel(qt_ref, kcmp_ref, vcmpt_ref, ksa_ref, kw_ref, vst_ref, vwt_ref, gate_ref, ovlt_ref, o_ref,
                accs_ref, accw_ref, sel_ref, *, tq, tk, nch):
    qi = pl.program_id(2)
    q0 = qi * tq
    rows = HPG * tq
    qt = jnp.concatenate([qt_ref[h] for h in range(HPG)], axis=1)
    tpos = q0 + lax.broadcasted_iota(jnp.int32, (1, rows), 1) % tq
    krow = lax.broadcasted_iota(jnp.int32, (tk, rows), 0)
    m0 = jnp.full((1, rows), NEG, F32)

    def run_tiles(k_ref, q_all, vt_ref, acc_ref, m, tiles):
        scores = []
        for kt, kind in tiles:
            kc = jnp.maximum(kt, 0) if kind in ('band', 'valid') else kt
            s = _dot(k_ref[pl.ds(pl.multiple_of(kc * tk, tk), tk), :], q_all)
            if kind == 'causal':
                s = jnp.where(kt * tk + krow <= tpos, s, NEG)
            elif kind == 'band':
                s = jnp.where((kt * tk + krow > tpos - WINDOW) & (kt >= 0), s, NEG)
            elif kind == 'valid':
                s = jnp.where(kt >= 0, s, NEG)
            scores.append((kc, s))
        for kc, s in scores:
            m_new = jnp.maximum(m, jnp.max(s, axis=0, keepdims=True))
            alpha = jnp.exp2(m - m_new)
            p = jnp.exp2(s - m_new).astype(BF16)
            acc_ref[...] = alpha * acc_ref[...] + _dot(vt_ref[kc], p)
            m = m_new
        return m

    accw_ref[...] = jnp.zeros_like(accw_ref)
    n_win = WINDOW // tk
    run_tiles(kw_ref, qt, vwt_ref, accw_ref, m0,
              [(qi - n_win, 'band')] + [(qi - n_win + t, 'valid') for t in range(1, n_win)] + [(qi, 'causal')])

    s = _dot(kcmp_ref[...], qt)
    cend = lax.broadcasted_iota(jnp.int32, (nch, rows), 0) * CMP_STRIDE + (CMP_BLOCK - 1)
    s = jnp.where(cend <= tpos, s, NEG)
    m = jnp.max(s, axis=0, keepdims=True)
    p = jnp.exp2(s - m)
    p = p * jnp.where(tpos >= CMP_BLOCK - 1, 1.0 / jnp.sum(p, axis=0, keepdims=True), 0.0)
    o_cmp = _dot(vcmpt_ref[...], p.astype(BF16))

    psum = p[:, 0:tq]
    for h in range(1, HPG):
        psum = psum + p[:, h * tq:(h + 1) * tq]
    hi = psum.astype(BF16)
    lo = (psum - hi.astype(F32)).astype(BF16)
    ovlt = ovlt_ref[...]
    imp = _dot(ovlt, hi) + _dot(ovlt, lo)
    nb = MAX_SEL_BLOCKS
    blk = lax.broadcasted_iota(jnp.int32, (nb, tq), 0)
    cur = (q0 + lax.broadcasted_iota(jnp.int32, (nb, tq), 1)) // SEL_BLOCK
    forced = (blk == 0) | (blk == cur) | (blk == cur - 1)
    imp = jnp.where(forced, FORCE, jnp.where(blk <= cur, imp, NEG))
    sub = SUBLANES
    rowl = lax.broadcasted_iota(jnp.int32, (sub, tq), 0)
    n_seen = (q0 + tq - 1) // SEL_BLOCK + 1
    sel_ref[...] = jnp.zeros_like(sel_ref)
    for nbv in range(2 * N_SELECT, nb + 1, N_SELECT):
        @pl.when((n_seen > nbv - N_SELECT) & (n_seen <= nbv))
        def _(nbv=nbv):
            groups = [imp[r:r + sub] for r in range(0, nbv, sub)]
            ranks = [jnp.zeros((sub, tq), F32) for _ in groups]
            for i in range(nbv):
                ri = jnp.broadcast_to(imp[i:i + 1, :], (sub, tq))
                for gi, x in enumerate(groups):
                    if i < gi * sub:
                        ahead = ri >= x
                    elif i >= (gi + 1) * sub:
                        ahead = ri > x
                    else:
                        ahead = (ri > x) | ((ri == x) & (rowl > i - gi * sub))
                    ranks[gi] = ranks[gi] + jnp.where(ahead, 1.0, 0.0)
            rank = jnp.concatenate(ranks, axis=0)
            sel_ref[0:nbv, :] = jnp.where(rank < N_SELECT, 0.0, NEG).astype(BF16)
    sel = sel_ref[...]
    qa = jnp.concatenate([qt, jnp.concatenate([sel] * HPG, axis=1)], axis=0)

    accs_ref[...] = jnp.zeros_like(accs_ref)
    nu = NSA_UNROLL

    def sel_multi(j, m):
        return run_tiles(ksa_ref, qa, vst_ref, accs_ref, m, [(nu * j + t, None) for t in range(nu)])

    m_sel = lax.fori_loop(0, qi // nu, sel_multi, m0)
    for r in range(nu):
        @pl.when(qi % nu == r)
        def _(r=r):
            base = qi - r
            run_tiles(ksa_ref, qa, vst_ref, accs_ref, m_sel,
                      [(base + t, None) for t in range(r)] + [(qi, 'causal')])

    accs = accs_ref[...]
    accw = accw_ref[...]
    o_sel = accs[:HEAD_DIM] * (1.0 / accs[HEAD_DIM:HEAD_DIM + 1])
    o_win = accw[:HEAD_DIM] * (1.0 / accw[HEAD_DIM:HEAD_DIM + 1])
    gt = gate_ref[...]
    for h in range(HPG):
        c = slice(h * tq, (h + 1) * tq)
        o_ref[h * HEAD_DIM:(h + 1) * HEAD_DIM, :] = (
            gt[3 * h:3 * h + 1] * o_cmp[:, c] + gt[3 * h + 1:3 * h + 2] * o_sel[:, c]
            + gt[3 * h + 2:3 * h + 3] * o_win[:, c])


def _nsa(qt, kcmp, vcmpt, ksa, kw, vst, vwt, gate, ovlt, *, bsz, seq, nch):
    tq, tk = NSA_TQ, NSA_TK
    assert tq == tk and WINDOW % tk == 0
    nq = seq // tq
    kern = functools.partial(_nsa_kernel, tq=tq, tk=tk, nch=nch)
    full = lambda b, g, i: (b, g, 0, 0)
    full5 = lambda b, g, i: (b, g, 0, 0, 0)
    qd = HPG * HEAD_DIM
    rows = HPG * tq
    return pl.pallas_call(
        kern,
        grid=(bsz, N_KV, nq),
        in_specs=[
            pl.BlockSpec((None, HPG, HEAD_DIM, tq), lambda b, g, i: (b, g, 0, i)),
            pl.BlockSpec((None, None, nch, HEAD_DIM), full),
            pl.BlockSpec((None, None, HEAD_DIM, nch), full),
            pl.BlockSpec((None, None, seq, 2 * HEAD_DIM), full),
            pl.BlockSpec((None, None, seq, HEAD_DIM), full),
            pl.BlockSpec((None, None, seq // tk, V_ROWS, tk), full5),
            pl.BlockSpec((None, None, seq // tk, V_ROWS, tk), full5),
            pl.BlockSpec((None, None, GATE_ROWS, tq), lambda b, g, i: (b, g, 0, i)),
            pl.BlockSpec((MAX_SEL_BLOCKS, nch), lambda b, g, i: (0, 0)),
        ],
        out_specs=pl.BlockSpec((None, qd, tq), lambda b, g, i: (b, g, i)),
        out_shape=jax.ShapeDtypeStruct((bsz, D_ATT, seq), F32),
        scratch_shapes=[pltpu.VMEM((V_ROWS, rows), F32), pltpu.VMEM((V_ROWS, rows), F32),
                        pltpu.VMEM((MAX_SEL_BLOCKS, tq), BF16)],
        compiler_params=_params(("parallel", "parallel", "arbitrary")),
        name="nsa",
    )(qt, kcmp, vcmpt, ksa, kw, vst, vwt, gate, ovlt)


def _route(logits, group=None):
    lane = lax.broadcasted_iota(jnp.int32, logits.shape, 1).astype(F32)
    far = float(LANES)
    is_g = lane < N_EXP_GROUPS
    glog = jnp.where(is_g, logits, -jnp.inf)
    gmax = jnp.max(glog, axis=1, keepdims=True)
    gsum = jnp.sum(jnp.where(is_g, jnp.exp(logits - gmax), 0.0), axis=1, keepdims=True)
    gsel = jnp.min(jnp.where(glog == gmax, lane, far), axis=1, keepdims=True)
    gprob = 1.0 / gsum
    lo = ROUTER_OFF + EXPERTS_PER_GROUP * (gsel if group is None else group)
    in_e = (lane >= lo) & (lane < lo + EXPERTS_PER_GROUP)
    emax = jnp.max(jnp.where(in_e, logits, -jnp.inf), axis=1, keepdims=True)
    eexp = jnp.where(in_e, jnp.exp(logits - emax), 0.0)
    eprob = jnp.where(in_e, eexp / jnp.sum(eexp, axis=1, keepdims=True), -1.0)
    v1 = jnp.max(eprob, axis=1, keepdims=True)
    i1 = jnp.min(jnp.where(eprob == v1, lane, far), axis=1, keepdims=True)
    rest = jnp.where(lane == i1, -1.0, eprob)
    v2 = jnp.max(rest, axis=1, keepdims=True)
    i2 = jnp.min(jnp.where(rest == v2, lane, far), axis=1, keepdims=True)
    den = v1 + v2
    comb = jnp.where(lane == i1, v1 / den * gprob, 0.0) + jnp.where(lane == i2, v2 / den * gprob, 0.0)
    return gsel, comb


def _out_proj_kernel(tm, ys_ref, yat_ref, x_ref, wglu_ref, bglu_ref, gs_ref, ga_ref, wo_ref, g2_ref, wr_ref, br_ref,
                     xt_ref, gsel_ref):
    y = _gelu(ys_ref[...])
    y = y * _sigmoid(_dot(y.astype(BF16), wglu_ref[...]) + bglu_ref[...])
    ysn = y * lax.rsqrt(jnp.mean(y * y, axis=-1, keepdims=True) + EPS) * gs_ref[...]
    yat = yat_ref[...]
    yant = yat * lax.rsqrt(jnp.mean(yat * yat, axis=0, keepdims=True) + EPS) * ga_ref[...]
    yan = yant.T
    x2 = x_ref[...] + _dot(ysn.astype(BF16), wo_ref[:D_SSM, :]) + _dot(yan.astype(BF16), wo_ref[D_SSM:, :])
    for c in range(TT_ROWS):
        xt_ref[pl.ds(c, tm, stride=TT_ROWS), :] = x2[:, c * LANES:(c + 1) * LANES]
    h2 = (x2 * lax.rsqrt(jnp.mean(x2 * x2, axis=-1, keepdims=True) + EPS) * g2_ref[...]).astype(BF16)
    gsel, _ = _route(_dot(h2, wr_ref[...]) + br_ref[...])
    gsel_ref[...] = jnp.broadcast_to(gsel, gsel_ref.shape)


def _out_proj(ys, yat, x2d, wglu, bglu, gs, ga, wo, g2, wr, br, *, seq):
    n_tok = x2d.shape[0]
    tm = PROJ_TM
    nl = seq // tm
    row = lambda i: (i, 0)
    const = lambda i: (0, 0)
    return pl.pallas_call(
        functools.partial(_out_proj_kernel, tm),
        grid=(n_tok // tm,),
        in_specs=[
            pl.BlockSpec((tm, D_SSM), row),
            pl.BlockSpec((None, D_ATT, tm), lambda i: (i // nl, 0, i % nl)),
            pl.BlockSpec((tm, D_MODEL), row),
            pl.BlockSpec((D_SSM, D_SSM), const),
            pl.BlockSpec((1, D_SSM), const),
            pl.BlockSpec((1, D_SSM), const),
            pl.BlockSpec((D_ATT, 1), const),
            pl.BlockSpec((D_SSM + D_ATT, D_MODEL), const),
            pl.BlockSpec((1, D_MODEL), const),
            pl.BlockSpec((D_MODEL, LANES), const),
            pl.BlockSpec((1, LANES), const),
        ],
        out_specs=[pl.BlockSpec((tm * TT_ROWS, LANES), row), pl.BlockSpec((tm, LANES), row)],
        out_shape=[jax.ShapeDtypeStruct((n_tok * TT_ROWS, LANES), F32), jax.ShapeDtypeStruct((n_tok, LANES), F32)],
        compiler_params=_params(("parallel",)),
        name="out_proj",
    )(ys, yat, x2d, wglu, bglu, gs, ga, wo, g2, wr, br)


def _moe_plan(gsel, n_tok):
    tmx = MOE_TM
    n_tiles = n_tok // tmx + N_EXP_GROUPS
    oh = (gsel[:, None] == jnp.arange(N_EXP_GROUPS)[None, :]).astype(jnp.int32)
    csum = jnp.cumsum(oh, axis=0)
    counts = csum[-1]
    rank = jnp.sum(csum * oh, axis=1) - 1
    nt = (counts + tmx - 1) // tmx
    tend = jnp.cumsum(nt)
    toff = tend - nt
    pos = jnp.sum(oh * toff[None, :], axis=1) * tmx + rank
    tile = jnp.arange(n_tiles)
    grp = jnp.minimum(jnp.sum((tile[:, None] >= tend[None, :]).astype(jnp.int32), axis=1), N_EXP_GROUPS - 1)
    nval = jnp.clip(counts[grp] - (tile - toff[grp]) * tmx, 0, tmx)
    return grp.astype(jnp.int32), nval.astype(jnp.int32), pos.astype(jnp.int32)


def _moe_kernel(grp_ref, nval_ref, pos_ref, x_hbm, g2_ref, wr_ref, br_ref, wg_ref, wu_ref, wd_ref, o_hbm,
                xbuf, obuf, abuf, tok_ref, gsem, ssem, *, tmx, n_tiles, n_tok):
    i = pl.program_id(0)
    slot = i % 2

    @pl.when(i == 0)
    def _():
        def place(t, c):
            tok_ref[pos_ref[t]] = t
            return c

        lax.fori_loop(0, n_tok, place, 0, unroll=8)

        def pad_tile(tile, c):
            def pad_row(r, c2):
                tok_ref[tile * tmx + r] = 0
                return c2

            return lax.fori_loop(nval_ref[tile], tmx, pad_row, c)

        lax.fori_loop(0, n_tiles, pad_tile, 0)

    def gather_row(tile, r, dst_slot):
        t = tok_ref[tile * tmx + r]
        return pltpu.make_async_copy(x_hbm.at[pl.ds(pl.multiple_of(t * TT_ROWS, TT_ROWS), TT_ROWS), :],
                                     xbuf.at[dst_slot, pl.ds(r * TT_ROWS, TT_ROWS), :], gsem.at[dst_slot])

    def scatter_row(tile, r, src_slot):
        t = tok_ref[tile * tmx + r]
        return pltpu.make_async_copy(obuf.at[src_slot, pl.ds(r, 1), :], o_hbm.at[pl.ds(t, 1), :], ssem.at[src_slot])

    def gather_wait(src_slot):
        pltpu.make_async_copy(x_hbm.at[pl.ds(0, tmx * TT_ROWS), :], xbuf.at[src_slot], gsem.at[src_slot]).wait()

    def scatter_wait(n, src_slot):
        n8 = pl.multiple_of((n // SUBLANES) * SUBLANES, SUBLANES)

        @pl.when(n8 > 0)
        def _():
            pltpu.make_async_copy(obuf.at[src_slot, pl.ds(0, n8), :], o_hbm.at[pl.ds(0, n8), :],
                                  ssem.at[src_slot]).wait()

        def one(r, c):
            pltpu.make_async_copy(obuf.at[src_slot, pl.ds(0, 1), :], o_hbm.at[pl.ds(0, 1), :],
                                  ssem.at[src_slot]).wait()
            return c

        lax.fori_loop(0, n - n8, one, 0)

    @pl.when(i == 0)
    def _():
        def body(r, c):
            gather_row(0, r, 0).start()
            return c

        lax.fori_loop(0, tmx, body, 0)

    nv = nval_ref[i]
    nv_prev = nval_ref[jnp.maximum(i - 1, 0)]

    @pl.when((i == 0) | (nv_prev > 0))
    def _():
        gather_wait(slot)

    @pl.when(i >= 2)
    def _():
        scatter_wait(nval_ref[i - 2], slot)

    nxt = jnp.minimum(i + 1, n_tiles - 1)
    per = tmx // EXPERTS_PER_GROUP
    prev_full = (i >= 1) & (nv_prev == tmx)

    def experts(with_scatter):
        x2 = jnp.concatenate([xbuf[slot, pl.ds(c, tmx, stride=TT_ROWS), :] for c in range(TT_ROWS)], axis=1)
        h = (x2 * lax.rsqrt(jnp.mean(x2 * x2, axis=-1, keepdims=True) + EPS) * g2_ref[...]).astype(BF16)
        _, cw = _route(_dot(h, wr_ref[...]) + br_ref[...], group=grp_ref[i].astype(F32))
        lane = lax.broadcasted_iota(jnp.int32, cw.shape, 1)
        first = ROUTER_OFF + EXPERTS_PER_GROUP * grp_ref[i]
        half = EXPERTS_PER_GROUP // 2
        for k in range(EXPERTS_PER_GROUP):
            if k < half:
                for r in range(k * 2 * per, (k + 1) * 2 * per):
                    gather_row(nxt, r, 1 - slot).start(priority=r % 2)
            elif with_scatter:
                for r in range((k - half) * 2 * per, (k - half + 1) * 2 * per):
                    scatter_row(i - 1, r, 1 - slot).start(priority=r % 2)
            gate = _dot(h, wg_ref[k].astype(BF16))
            up = _dot(h, wu_ref[k].astype(BF16))
            ck = jnp.sum(jnp.where(lane == first + k, cw, 0.0), axis=1, keepdims=True)
            abuf[:, k * D_EXPERT:(k + 1) * D_EXPERT] = (gate * _sigmoid(gate) * up * ck).astype(BF16)
        obuf[slot] = x2 + _dot(abuf[...], wd_ref[...].astype(BF16))

    @pl.when(prev_full & (nv > 0))
    def _():
        experts(True)

    @pl.when(jnp.logical_not(prev_full) & (nv > 0))
    def _():
        experts(False)

    @pl.when(prev_full & (nv == 0))
    def _():
        def body(r, c):
            scatter_row(i - 1, r, 1 - slot).start()
            return c

        lax.fori_loop(0, tmx, body, 0)

    @pl.when(nv < tmx)
    def _():
        def body(r, c):
            scatter_row(i, r, slot).start()
            return c

        lax.fori_loop(0, nv, body, 0)

    @pl.when(i == n_tiles - 1)
    def _():
        @pl.when(nv == tmx)
        def _():
            def body(r, c):
                scatter_row(i, r, slot).start()
                return c

            lax.fori_loop(0, tmx, body, 0)

        @pl.when(nv > 0)
        def _():
            gather_wait(1 - slot)

        scatter_wait(nv_prev, 1 - slot)
        scatter_wait(nv, slot)


def _moe(x2t, grp, nval, pos, g2, wr, br, wg, wu, wd):
    n_tok = x2t.shape[0] // TT_ROWS
    tmx = MOE_TM
    n_tiles = grp.shape[0]
    kern = functools.partial(_moe_kernel, tmx=tmx, n_tiles=n_tiles, n_tok=n_tok)
    gk = EXPERTS_PER_GROUP * D_EXPERT
    w_bytes = 2 * 3 * EXPERTS_PER_GROUP * D_MODEL * D_EXPERT * 4
    io_bytes = 2 * tmx * (TT_ROWS * LANES + D_MODEL) * 4 + tmx * gk * 2
    vmem_limit = w_bytes + io_bytes + 6 * 1024 * 1024
    assert vmem_limit < V7X_VMEM_BYTES
    grid_spec = pltpu.PrefetchScalarGridSpec(
        num_scalar_prefetch=3,
        grid=(n_tiles,),
        in_specs=[
            pl.BlockSpec(memory_space=pl.ANY),
            pl.BlockSpec((1, D_MODEL), lambda i, g, n, t: (0, 0)),
            pl.BlockSpec((D_MODEL, LANES), lambda i, g, n, t: (0, 0)),
            pl.BlockSpec((1, LANES), lambda i, g, n, t: (0, 0)),
            pl.BlockSpec((None, EXPERTS_PER_GROUP, D_MODEL, D_EXPERT), lambda i, g, n, t: (g[i], 0, 0, 0)),
            pl.BlockSpec((None, EXPERTS_PER_GROUP, D_MODEL, D_EXPERT), lambda i, g, n, t: (g[i], 0, 0, 0)),
            pl.BlockSpec((None, gk, D_MODEL), lambda i, g, n, t: (g[i], 0, 0)),
        ],
        out_specs=pl.BlockSpec(memory_space=pl.ANY),
        scratch_shapes=[
            pltpu.VMEM((2, tmx * TT_ROWS, LANES), F32),
            pltpu.VMEM((2, tmx, D_MODEL), F32),
            pltpu.VMEM((tmx, gk), BF16),
            pltpu.SMEM((n_tiles * tmx,), jnp.int32),
            pltpu.SemaphoreType.DMA((2,)),
            pltpu.SemaphoreType.DMA((2,)),
        ],
    )
    return pl.pallas_call(
        kern,
        grid_spec=grid_spec,
        out_shape=jax.ShapeDtypeStruct((n_tok, D_MODEL), F32),
        compiler_params=_params(("arbitrary",), vmem_limit),
        name="moe",
    )(grp, nval, pos, x2t, g2, wr, br, wg, wu, wd)


def _block_diag_ones(n, blk):
    i = jnp.arange(n) // blk
    return (i[:, None] == i[None, :]).astype(BF16)


def _layer(x, norm1_g, w_in, lam_re, lam_im, log_step, b_re, b_im, c_re, c_im, d_skip,
           w_glu, b_glu, g_q, g_kc, g_ks, g_kw, pos_k, pos_v, w_ck1, w_ck2, w_cv1, w_cv2,
           out_g_ssm, out_g_att, w_out, norm2_g, w_grp, b_grp, w_exp, b_exp, w_gate, w_up, w_down):
    bsz, seq, _ = x.shape
    assert seq % PROJ_TM == 0 and seq // SEL_BLOCK <= MAX_SEL_BLOCKS
    n_tok = bsz * seq
    x2d = x.reshape(n_tok, D_MODEL)
    q8 = SSM_Q
    n_sub = seq // q8
    nch = seq // CMP_STRIDE

    o_q = D_SSM
    o_kv = D_SSM + D_ATT
    o_gt = o_kv + 6 * D_KV
    kv = lambda i: w_in[:, o_kv + i * D_KV:o_kv + (i + 1) * D_KV]
    wrow = jnp.concatenate([w_in[:, :o_q], kv(0), kv(1), kv(2), kv(4)], axis=1).astype(BF16)
    per_g = HPG * N_BRANCH
    wgt = jnp.zeros((D_MODEL, N_KV * GATE_ROWS), F32)
    for g in range(N_KV):
        wgt = wgt.at[:, g * GATE_ROWS:g * GATE_ROWS + per_g].set(w_in[:, o_gt + g * per_g:o_gt + (g + 1) * per_g])
    wcol = jnp.concatenate([w_in[:, o_q:o_kv], kv(3), kv(5), wgt], axis=1).T.astype(BF16)
    qscale = (HEAD_DIM ** -0.5) * math.log2(math.e)
    gq = (jnp.tile(g_q.astype(F32), N_HEADS) * qscale).reshape(D_ATT, 1)
    gks = jnp.tile(g_ks.astype(F32), N_KV).reshape(1, D_KV)
    gkw = jnp.tile(g_kw.astype(F32), N_KV).reshape(1, D_KV)

    u, qt, kc, vc, ksa, kw, vst, vwt, gate = _in_proj(
        x2d, norm1_g.reshape(1, D_MODEL), wrow, wcol, gq, gks, gkw,
        _block_diag_ones(D_KV, HEAD_DIM), bsz=bsz, seq=seq)

    w_c, t_c, m_c, e_state_t, e_lane_t, pw_re, pw_im, dvec = _s5_weights(
        lam_re, lam_im, log_step, b_re, b_im, c_re, c_im, d_skip, n_sub)
    ys = _s5(u.reshape(bsz, seq, D_SSM), w_c, t_c, m_c, e_state_t, e_lane_t, pw_re, pw_im, dvec,
             bsz=bsz, n_sub=n_sub).reshape(n_tok, D_SSM)

    wide = CMP_STRIDE * HEAD_DIM
    pad8 = lambda p: jnp.zeros((SUBLANES, 2 * wide), F32).at[0].set(p.reshape(-1)).astype(BF16)
    kcmp, vcmpt = _compress(
        kc, vc,
        w_ck1.astype(BF16), w_ck2.astype(BF16), w_cv1.astype(BF16), w_cv2.T.astype(BF16),
        pad8(pos_k), pad8(pos_v), g_kc.astype(F32).reshape(1, HEAD_DIM), bsz=bsz, nch=nch)
    cstart = jnp.arange(nch) * CMP_STRIDE
    sstart = jnp.arange(MAX_SEL_BLOCKS) * SEL_BLOCK
    ovlt = ((cstart[None, :] < sstart[:, None] + SEL_BLOCK) & (cstart[None, :] + CMP_BLOCK > sstart[:, None])
            & (jnp.arange(MAX_SEL_BLOCKS)[:, None] < seq // SEL_BLOCK)
            & (jnp.arange(nch)[None, :] < nch - 1)).astype(BF16)
    yat = _nsa(qt, kcmp, vcmpt, ksa, kw, vst, vwt, gate, ovlt, bsz=bsz, seq=seq, nch=nch)

    wr = jnp.zeros((D_MODEL, LANES), F32)
    wr = wr.at[:, :N_EXP_GROUPS].set(w_grp).at[:, ROUTER_OFF:ROUTER_OFF + N_EXPERTS].set(w_exp).astype(BF16)
    br = jnp.zeros((1, LANES), F32)
    br = br.at[0, :N_EXP_GROUPS].set(b_grp).at[0, ROUTER_OFF:ROUTER_OFF + N_EXPERTS].set(b_exp)
    g2 = norm2_g.reshape(1, D_MODEL).astype(F32)
    x2t, gsel = _out_proj(
        ys, yat, x2d, w_glu.astype(BF16), b_glu.reshape(1, D_SSM).astype(F32),
        out_g_ssm.reshape(1, D_SSM).astype(F32), out_g_att.reshape(D_ATT, 1).astype(F32),
        w_out.astype(BF16), g2, wr, br, seq=seq)

    grp, nval, pos = _moe_plan(gsel[:, 0].astype(jnp.int32), n_tok)
    gshape = (N_EXP_GROUPS, EXPERTS_PER_GROUP, D_MODEL, D_EXPERT)
    out = _moe(x2t, grp, nval, pos, g2, wr, br, w_gate.reshape(gshape), w_up.reshape(gshape),
               w_down.reshape(N_EXP_GROUPS, EXPERTS_PER_GROUP * D_EXPERT, D_MODEL))
    return out.reshape(bsz, seq, D_MODEL)


def kernel(x, norm1_g, w_in, lam_re, lam_im, log_step, b_re, b_im, c_re, c_im, d_skip, w_glu, b_glu, g_q, g_kc, g_ks, g_kw, pos_k, pos_v, w_ck1, w_ck2, w_cv1, w_cv2, out_g_ssm, out_g_att, w_out, norm2_g, w_grp, b_grp, w_exp, b_exp, w_gate, w_up, w_down):
    depth = norm1_g.shape[0]
    for l in range(depth):
        x = _layer(x, norm1_g[l], w_in[l], lam_re[l], lam_im[l], log_step[l], b_re[l], b_im[l], c_re[l],
                   c_im[l], d_skip[l], w_glu[l], b_glu[l], g_q[l], g_kc[l], g_ks[l], g_kw[l], pos_k[l],
                   pos_v[l], w_ck1[l], w_ck2[l], w_cv1[l], w_cv2[l], out_g_ssm[l], out_g_att[l], w_out[l],
                   norm2_g[l], w_grp[l], b_grp[l], w_exp[l], b_exp[l], w_gate[l], w_up[l], w_down[l])
    return x
```

```python
import functools
import math

import jax
import jax.numpy as jnp
from jax import lax
from jax.experimental import pallas as pl
from jax.experimental.pallas import tpu as pltpu

D_MODEL = 1024
D_SSM = 512
SSM_CH = 16
SSM_STATE = 64
D_ATT = 512
HEAD_DIM = 64
N_HEADS = D_ATT // HEAD_DIM
N_KV = 2
HPG = N_HEADS // N_KV
D_KV = N_KV * HEAD_DIM
N_BRANCH = 3
CMP_STRIDE = 16
CMP_BLOCK = 2 * CMP_STRIDE
CMP_HIDDEN = 256
SEL_BLOCK = 64
N_SELECT = 16
WINDOW = 512
N_EXP_GROUPS = 4
EXPERTS_PER_GROUP = 8
N_EXPERTS = N_EXP_GROUPS * EXPERTS_PER_GROUP
D_EXPERT = 256
EPS = 1e-6
NEG = -1e30
FORCE = 1e9

LANES = 128
SUBLANES = 8
SSM_Q = 8
SSM_LT = D_SSM // LANES
ROUTER_OFF = N_EXP_GROUPS
NSA_TQ = 256
NSA_TK = 256
NSA_UNROLL = 8
V_ROWS = HEAD_DIM + 16
MAX_SEL_BLOCKS = 64
MOE_TM = 256
OUT_BUFS = 3
TT_ROWS = D_MODEL // LANES
PROJ_TM = 1024
GATE_ROWS = 16
V7X_VMEM_BYTES = 64 * 1024 * 1024
VMEM_LIMIT = V7X_VMEM_BYTES - 8 * 1024 * 1024

F32 = jnp.float32
BF16 = jnp.bfloat16


def _dot(a, b):
    return jnp.dot(a, b, preferred_element_type=F32)


def _dot_nt(a, b):
    return lax.dot_general(a, b, (((1,), (1,)), ((), ())), preferred_element_type=F32)


def _split_dot(x, w):
    hi = x.astype(BF16)
    lo = (x - hi.astype(F32)).astype(BF16)
    return _dot(hi, w) + _dot(lo, w)


def _gelu(x):
    c = math.sqrt(2.0 / math.pi)
    return 0.5 * x * (1.0 + jnp.tanh(c * (x + 0.044715 * (x * x * x))))


def _sigmoid(x):
    return 1.0 / (1.0 + jnp.exp(-x))


def _params(sem, vmem_limit=VMEM_LIMIT):
    return pltpu.CompilerParams(dimension_semantics=sem, vmem_limit_bytes=vmem_limit)


def _in_proj_kernel(x_ref, g1_ref, wrow_ref, wcol_ref, gq_ref, gks_ref, gkw_ref, bd128_ref,
                    u_ref, qt_ref, kc_ref, vc_ref, ksa_ref, kw_ref, vst_ref, vwt_ref, gate_ref, *, tm, nl):
    x = x_ref[...]
    ms = jnp.mean(x * x, axis=-1, keepdims=True)
    hn = (x * lax.rsqrt(ms + EPS) * g1_ref[...]).astype(BF16)

    pr = _dot(hn, wrow_ref[...])
    u_ref[...] = pr[:, :D_SSM]
    kc, vc, ks, kw = [pr[:, D_SSM + i * D_KV:D_SSM + (i + 1) * D_KV] for i in range(4)]
    kss = _split_dot(ks * ks, bd128_ref[...])
    ksn = ks * lax.rsqrt(kss * (1.0 / HEAD_DIM) + EPS) * gks_ref[...]
    kws = _split_dot(kw * kw, bd128_ref[...])
    kwn = kw * lax.rsqrt(kws * (1.0 / HEAD_DIM) + EPS) * gkw_ref[...]
    t0 = (pl.program_id(0) % nl) * tm
    tpos = t0 + lax.broadcasted_iota(jnp.int32, (tm, MAX_SEL_BLOCKS), 0)
    blk = lax.broadcasted_iota(jnp.int32, (tm, MAX_SEL_BLOCKS), 1)
    onehot = jnp.where(tpos // SEL_BLOCK == blk, 1.0, 0.0).astype(BF16)
    for g in range(N_KV):
        sl = slice(g * HEAD_DIM, (g + 1) * HEAD_DIM)
        kc_ref[g] = kc[:, sl]
        vc_ref[g] = vc[:, sl]
        ksa_ref[g] = jnp.concatenate([ksn[:, sl].astype(BF16), onehot], axis=1)
        kw_ref[g] = kwn[:, sl].astype(BF16)

    pc = _dot_nt(wcol_ref[...], hn)
    gq = gq_ref[...]
    for h in range(N_HEADS):
        sl = slice(h * HEAD_DIM, (h + 1) * HEAD_DIM)
        qh = pc[sl]
        ss = jnp.sum(qh * qh, axis=0, keepdims=True)
        qt_ref[h] = (qh * lax.rsqrt(ss * (1.0 / HEAD_DIM) + EPS) * gq[sl]).astype(BF16)
    ones_rows = jnp.where(lax.broadcasted_iota(jnp.int32, (V_ROWS - HEAD_DIM, tm), 0) == 0, 1.0, 0.0)
    for g in range(N_KV):
        for o_ref, base in ((vst_ref, D_ATT), (vwt_ref, D_ATT + D_KV)):
            vt = jnp.concatenate([pc[base + g * HEAD_DIM:base + (g + 1) * HEAD_DIM], ones_rows], axis=0)
            vt = vt.astype(BF16)
            for j in range(tm // NSA_TK):
                o_ref[g, j] = vt[:, j * NSA_TK:(j + 1) * NSA_TK]
        gb = D_ATT + 2 * D_KV + g * GATE_ROWS
        gate_ref[g] = _sigmoid(pc[gb:gb + GATE_ROWS])


def _in_proj(x2d, g1, wrow, wcol, gq, gks, gkw, bd128, *, bsz, seq):
    tm = PROJ_TM
    nl = seq // tm
    n_tok = bsz * seq
    kern = functools.partial(_in_proj_kernel, tm=tm, nl=nl)
    row = lambda i: (i, 0)
    const = lambda i: (0, 0)
    bgl = lambda i: (i // nl, 0, i % nl, 0)
    n_col = wcol.shape[0]
    jt = tm // NSA_TK

    def kvspec(width):
        return pl.BlockSpec((None, N_KV, tm, width), bgl)

    def kvshape(width, dtype=BF16):
        return jax.ShapeDtypeStruct((bsz, N_KV, seq, width), dtype)

    vt_spec = pl.BlockSpec((None, N_KV, jt, V_ROWS, NSA_TK), lambda i: (i // nl, 0, i % nl, 0, 0))
    vt_shape = jax.ShapeDtypeStruct((bsz, N_KV, seq // NSA_TK, V_ROWS, NSA_TK), BF16)
    return pl.pallas_call(
        kern,
        grid=(n_tok // tm,),
        in_specs=[
            pl.BlockSpec((tm, D_MODEL), row),
            pl.BlockSpec((1, D_MODEL), const),
            pl.BlockSpec((D_MODEL, D_SSM + 4 * D_KV), const),
            pl.BlockSpec((n_col, D_MODEL), const),
            pl.BlockSpec((D_ATT, 1), const),
            pl.BlockSpec((1, D_KV), const),
            pl.BlockSpec((1, D_KV), const),
            pl.BlockSpec((D_KV, D_KV), const),
        ],
        out_specs=[
            pl.BlockSpec((tm, D_SSM), row),
            pl.BlockSpec((None, N_HEADS, HEAD_DIM, tm), lambda i: (i // nl, 0, 0, i % nl)),
            kvspec(HEAD_DIM), kvspec(HEAD_DIM), kvspec(2 * HEAD_DIM), kvspec(HEAD_DIM),
            vt_spec, vt_spec,
            pl.BlockSpec((None, N_KV, GATE_ROWS, tm), lambda i: (i // nl, 0, 0, i % nl)),
        ],
        out_shape=[
            jax.ShapeDtypeStruct((n_tok, D_SSM), F32),
            jax.ShapeDtypeStruct((bsz, N_HEADS, HEAD_DIM, seq), BF16),
            kvshape(HEAD_DIM, F32), kvshape(HEAD_DIM, F32), kvshape(2 * HEAD_DIM), kvshape(HEAD_DIM),
            vt_shape, vt_shape,
            jax.ShapeDtypeStruct((bsz, N_KV, GATE_ROWS, seq), F32),
        ],
        compiler_params=_params(("parallel",)),
        name="in_proj",
    )(x2d, g1, wrow, wcol, gq, gks, gkw, bd128)


def _s5_weights(lam_re, lam_im, log_step, b_re, b_im, c_re, c_im, d_skip, n_sub):
    q = SSM_Q
    lam = lax.complex(lam_re.astype(F32), lam_im.astype(F32))
    step = jnp.exp(log_step.astype(F32))[:, None]
    lam_bar = jnp.exp(lam * step)
    b_bar = ((lam_bar - 1.0) / lam)[..., None] * lax.complex(b_re.astype(F32), b_im.astype(F32))
    c = lax.complex(c_re.astype(F32), c_im.astype(F32))
    pows = [jnp.ones_like(lam_bar)]
    for _ in range(q):
        pows.append(pows[-1] * lam_bar)
    pw = jnp.stack(pows)
    lt, a8 = SSM_LT, LANES // SSM_CH
    hp = a8 * SSM_STATE
    e_lane = (jnp.arange(a8)[:, None] == jnp.arange(LANES)[None, :] // SSM_CH).astype(F32)
    e_state = (jnp.arange(a8)[:, None] == jnp.arange(hp)[None, :] // SSM_STATE).astype(F32)
    e_lane_t = jnp.tile(e_lane, (1, q))
    e_state_t = jnp.tile(e_state, (1, 2))

    kk = jnp.real(jnp.einsum('ghp,kgp,gpi->kghi', c, pw[:q], b_bar))
    km = kk.reshape(q, lt, a8, SSM_CH, SSM_CH).transpose(1, 4, 0, 2, 3).reshape(lt, SSM_CH, q, LANES)
    lag = jnp.arange(q)[None, :] - jnp.arange(q)[:, None]
    kg = km[:, :, jnp.clip(lag, 0, q - 1), :] * (lag >= 0)[None, None, :, :, None].astype(F32)
    kc = kg.transpose(0, 2, 1, 3, 4).reshape(lt, q, 1, SSM_CH, q * LANES)
    t_c = kc.reshape(lt, q * SSM_CH, q * LANES)

    wc = pw[q - 1 - jnp.arange(q)][..., None] * b_bar[None]
    wri = jnp.stack([jnp.real(wc), jnp.imag(wc)])
    wm = (wri.reshape(2, q, lt, a8, SSM_STATE, SSM_CH).transpose(2, 1, 5, 0, 3, 4)
          .reshape(lt, q, 1, SSM_CH, 2 * hp))
    w_c = wm.reshape(lt, q * SSM_CH, 2 * hp)

    cl = c[None] * pw[1:q + 1][:, :, None, :]
    cri = jnp.stack([jnp.real(cl), -jnp.imag(cl)])
    mm = (cri.reshape(2, q, lt, a8, SSM_CH, SSM_STATE).transpose(2, 0, 5, 1, 3, 4)
          .reshape(lt, 2, 1, SSM_STATE, q * LANES))
    m_c = mm.reshape(lt, 2 * SSM_STATE, q * LANES)

    n_lvl = max(1, (n_sub - 1).bit_length())
    lv = [pw[q]]
    for _ in range(n_lvl - 1):
        lv.append(lv[-1] * lv[-1])
    lvs = jnp.stack(lv).reshape(n_lvl, lt, 1, hp)
    pw_re = jnp.real(lvs).transpose(1, 0, 2, 3)
    pw_im = jnp.imag(lvs).transpose(1, 0, 2, 3)
    dvec = jnp.tile(d_skip.astype(F32).reshape(lt, 1, LANES), (1, 1, q))
    return w_c, t_c, m_c, e_state_t, e_lane_t, pw_re, pw_im, dvec


def _s5_kernel(u_ref, wc_ref, tc_ref, mc_ref, es_ref, el_ref, pwr_ref, pwi_ref, d_ref, y_ref, w_ref, t_ref, m_ref,
               *, n_sub, n_lvl):
    half = (LANES // SSM_CH) * SSM_STATE
    q = SSM_Q
    a8 = LANES // SSM_CH

    @pl.when(pl.program_id(1) == 0)
    def _():
        for a in range(a8):
            el = el_ref[a:a + 1, :]
            es = es_ref[a:a + 1, :]
            for s_ in range(q):
                rows = pl.ds((s_ * a8 + a) * SSM_CH, SSM_CH)
                src = pl.ds(s_ * SSM_CH, SSM_CH)
                t_ref[rows, :] = (tc_ref[src, :] * el).astype(BF16)
                w_ref[rows, :] = (wc_ref[src, :] * es).astype(BF16)
            for r in range(2):
                rows = pl.ds((r * a8 + a) * SSM_STATE, SSM_STATE)
                m_ref[rows, :] = (mc_ref[pl.ds(r * SSM_STATE, SSM_STATE), :] * el).astype(BF16)

    u = jnp.concatenate([u_ref[pl.ds(s, n_sub, stride=q), :] for s in range(q)], axis=1)
    ub = u.astype(BF16)
    s_loc = _dot(ub, w_ref[...])
    re = s_loc[:, :half]
    im = s_loc[:, half:]
    rowi = lax.broadcasted_iota(jnp.int32, (n_sub, half), 0)
    for k in range(n_lvl):
        d = 1 << k
        ar = pwr_ref[k]
        ai = pwi_ref[k]
        keep = rowi >= d
        sre = jnp.where(keep, pltpu.roll(re, d, axis=0), 0.0)
        sim = jnp.where(keep, pltpu.roll(im, d, axis=0), 0.0)
        re, im = re + (ar * sre - ai * sim), im + (ar * sim + ai * sre)
    keep = rowi >= 1
    xre = jnp.where(keep, pltpu.roll(re, 1, axis=0), 0.0)
    xim = jnp.where(keep, pltpu.roll(im, 1, axis=0), 0.0)
    xst = jnp.concatenate([xre, xim], axis=1).astype(BF16)
    y = _dot(ub, t_ref[...]) + _dot(xst, m_ref[...]) + d_ref[...] * u
    for j in range(q):
        y_ref[pl.ds(j, n_sub, stride=q), :] = y[:, j * LANES:(j + 1) * LANES]


def _s5(u, w_c, t_c, m_c, e_state_t, e_lane_t, pw_re, pw_im, dvec, *, bsz, n_sub):
    q = SSM_Q
    n_lvl = pw_re.shape[1]
    kern = functools.partial(_s5_kernel, n_sub=n_sub, n_lvl=n_lvl)
    wide = q * LANES
    seq = n_sub * q
    return pl.pallas_call(
        kern,
        grid=(SSM_LT, bsz),
        in_specs=[
            pl.BlockSpec((None, seq, LANES), lambda l, b: (b, 0, l)),
            pl.BlockSpec((None, q * SSM_CH, wide), lambda l, b: (l, 0, 0)),
            pl.BlockSpec((None, q * SSM_CH, wide), lambda l, b: (l, 0, 0)),
            pl.BlockSpec((None, 2 * SSM_STATE, wide), lambda l, b: (l, 0, 0)),
            pl.BlockSpec((LANES // SSM_CH, wide), lambda l, b: (0, 0)),
            pl.BlockSpec((LANES // SSM_CH, wide), lambda l, b: (0, 0)),
            pl.BlockSpec((None, n_lvl, 1, wide // 2), lambda l, b: (l, 0, 0, 0)),
            pl.BlockSpec((None, n_lvl, 1, wide // 2), lambda l, b: (l, 0, 0, 0)),
            pl.BlockSpec((None, 1, wide), lambda l, b: (l, 0, 0)),
        ],
        out_specs=pl.BlockSpec((None, seq, LANES), lambda l, b: (b, 0, l)),
        out_shape=jax.ShapeDtypeStruct((bsz, seq, D_SSM), F32),
        scratch_shapes=[pltpu.VMEM((wide, wide), BF16)] * 3,
        compiler_params=_params(("arbitrary", "arbitrary")),
        name="s5",
    )(u, w_c, t_c, m_c, e_state_t, e_lane_t, pw_re, pw_im, dvec)


def _compress_kernel(kc_ref, vc_ref, w1k_ref, w2k_ref, w1v_ref, w2vt_ref, posk_ref, posv_ref, gkc_ref,
                     kcmp_ref, vcmpt_ref, *, nch):
    half = CMP_STRIDE * HEAD_DIM

    def hidden(x_ref, w1_ref, pos_ref):
        x = jnp.concatenate([x_ref[pl.ds(j, nch, stride=CMP_STRIDE), :] for j in range(CMP_STRIDE)],
                            axis=1).astype(BF16)
        a = _dot(x, w1_ref[:half, :])
        b = _dot(x, w1_ref[half:, :])
        pv = _dot(pos_ref[...], w1_ref[...])[0:1, :]
        hid = a + pltpu.roll(b, nch - 1, axis=0) + pv
        return _gelu(hid).astype(BF16)

    k = _dot(hidden(kc_ref, w1k_ref, posk_ref), w2k_ref[...])
    ms = jnp.mean(k * k, axis=-1, keepdims=True)
    kcmp_ref[...] = (k * lax.rsqrt(ms + EPS) * gkc_ref[...]).astype(BF16)
    vt = _dot_nt(w2vt_ref[...], hidden(vc_ref, w1v_ref, posv_ref))
    coli = lax.broadcasted_iota(jnp.int32, vt.shape, 1)
    vcmpt_ref[...] = jnp.where(coli < nch - 1, vt, 0.0).astype(BF16)


def _compress(kcf, vcf, w1k, w2k, w1v, w2vt, posk, posv, gkc, *, bsz, nch):
    kern = functools.partial(_compress_kernel, nch=nch)
    wide = CMP_STRIDE * HEAD_DIM
    xspec = pl.BlockSpec((None, None, nch * CMP_STRIDE, HEAD_DIM), lambda b, g: (b, g, 0, 0))
    c2 = lambda b, g: (0, 0)
    return pl.pallas_call(
        kern,
        grid=(bsz, N_KV),
        in_specs=[
            xspec, xspec,
            pl.BlockSpec((2 * wide, CMP_HIDDEN), c2), pl.BlockSpec((CMP_HIDDEN, HEAD_DIM), c2),
            pl.BlockSpec((2 * wide, CMP_HIDDEN), c2), pl.BlockSpec((HEAD_DIM, CMP_HIDDEN), c2),
            pl.BlockSpec((SUBLANES, 2 * wide), c2), pl.BlockSpec((SUBLANES, 2 * wide), c2),
            pl.BlockSpec((1, HEAD_DIM), c2),
        ],
        out_specs=[pl.BlockSpec((None, None, nch, HEAD_DIM), lambda b, g: (b, g, 0, 0)),
                   pl.BlockSpec((None, None, HEAD_DIM, nch), lambda b, g: (b, g, 0, 0))],
        out_shape=[jax.ShapeDtypeStruct((bsz, N_KV, nch, HEAD_DIM), BF16),
                   jax.ShapeDtypeStruct((bsz, N_KV, HEAD_DIM, nch), BF16)],
        compiler_params=_params(("parallel", "parallel")),
        name="compress",
    )(kcf, vcf, w1k, w2k, w1v, w2vt, posk, posv, gkc)


def _nsa_kernel(qt_ref, kcmp_ref, vcmpt_ref, ksa_ref, kw_ref, vst_ref, vwt_ref, gate_ref, ovlt_ref, o_ref,
                accs_ref, accw_ref, sel_ref, *, tq, tk, nch):
    qi = pl.program_id(2)
    q0 = qi * tq
    rows = HPG * tq
    qt = jnp.concatenate([qt_ref[h] for h in range(HPG)], axis=1)
    tpos = q0 + lax.broadcasted_iota(jnp.int32, (1, rows), 1) % tq
    krow = lax.broadcasted_iota(jnp.int32, (tk, rows), 0)
    m0 = jnp.full((1, rows), NEG, F32)

    def run_tiles(k_ref, q_all, vt_ref, acc_ref, m, tiles):
        scores = []
        for kt, kind in tiles:
            kc = jnp.maximum(kt, 0) if kind in ('band', 'valid') else kt
            s = _dot(k_ref[pl.ds(pl.multiple_of(kc * tk, tk), tk), :], q_all)
            if kind == 'causal':
                s = jnp.where(kt * tk + krow <= tpos, s, NEG)
            elif kind == 'band':
                s = jnp.where((kt * tk + krow > tpos - WINDOW) & (kt >= 0), s, NEG)
            elif kind == 'valid':
                s = jnp.where(kt >= 0, s, NEG)
            scores.append((kc, s))
        for kc, s in scores:
            m_new = jnp.maximum(m, jnp.max(s, axis=0, keepdims=True))
            alpha = jnp.exp2(m - m_new)
            p = jnp.exp2(s - m_new).astype(BF16)
            acc_ref[...] = alpha * acc_ref[...] + _dot(vt_ref[kc], p)
            m = m_new
        return m

    accw_ref[...] = jnp.zeros_like(accw_ref)
    n_win = WINDOW // tk
    run_tiles(kw_ref, qt, vwt_ref, accw_ref, m0,
              [(qi - n_win, 'band')] + [(qi - n_win + t, 'valid') for t in range(1, n_win)] + [(qi, 'causal')])

    s = _dot(kcmp_ref[...], qt)
    cend = lax.broadcasted_iota(jnp.int32, (nch, rows), 0) * CMP_STRIDE + (CMP_BLOCK - 1)
    s = jnp.where(cend <= tpos, s, NEG)
    m = jnp.max(s, axis=0, keepdims=True)
    p = jnp.exp2(s - m)
    p = p * jnp.where(tpos >= CMP_BLOCK - 1, 1.0 / jnp.sum(p, axis=0, keepdims=True), 0.0)
    o_cmp = _dot(vcmpt_ref[...], p.astype(BF16))

    psum = p[:, 0:tq]
    for h in range(1, HPG):
        psum = psum + p[:, h * tq:(h + 1) * tq]
    hi = psum.astype(BF16)
    lo = (psum - hi.astype(F32)).astype(BF16)
    ovlt = ovlt_ref[...]
    imp = _dot(ovlt, hi) + _dot(ovlt, lo)
    nb = MAX_SEL_BLOCKS
    blk = lax.broadcasted_iota(jnp.int32, (nb, tq), 0)
    cur = (q0 + lax.broadcasted_iota(jnp.int32, (nb, tq), 1)) // SEL_BLOCK
    forced = (blk == 0) | (blk == cur) | (blk == cur - 1)
    imp = jnp.where(forced, FORCE, jnp.where(blk <= cur, imp, NEG))
    sub = SUBLANES
    rowl = lax.broadcasted_iota(jnp.int32, (sub, tq), 0)
    n_seen = (q0 + tq - 1) // SEL_BLOCK + 1
    sel_ref[...] = jnp.zeros_like(sel_ref)
    for nbv in range(2 * N_SELECT, nb + 1, N_SELECT):
        @pl.when((n_seen > nbv - N_SELECT) & (n_seen <= nbv))
        def _(nbv=nbv):
            groups = [imp[r:r + sub] for r in range(0, nbv, sub)]
            ranks = [jnp.zeros((sub, tq), F32) for _ in groups]
            for i in range(nbv):
                ri = jnp.broadcast_to(imp[i:i + 1, :], (sub, tq))
                for gi, x in enumerate(groups):
                    if i < gi * sub:
                        ahead = ri >= x
                    elif i >= (gi + 1) * sub:
                        ahead = ri > x
                    else:
                        ahead = (ri > x) | ((ri == x) & (rowl > i - gi * sub))
                    ranks[gi] = ranks[gi] + jnp.where(ahead, 1.0, 0.0)
            rank = jnp.concatenate(ranks, axis=0)
            sel_ref[0:nbv, :] = jnp.where(rank < N_SELECT, 0.0, NEG).astype(BF16)
    sel = sel_ref[...]
    qa = jnp.concatenate([qt, jnp.concatenate([sel] * HPG, axis=1)], axis=0)

    accs_ref[...] = jnp.zeros_like(accs_ref)
    nu = NSA_UNROLL

    def sel_multi(j, m):
        return run_tiles(ksa_ref, qa, vst_ref, accs_ref, m, [(nu * j + t, None) for t in range(nu)])

    m_sel = lax.fori_loop(0, qi // nu, sel_multi, m0)
    for r in range(nu):
        @pl.when(qi % nu == r)
        def _(r=r):
            base = qi - r
            run_tiles(ksa_ref, qa, vst_ref, accs_ref, m_sel,
                      [(base + t, None) for t in range(r)] + [(qi, 'causal')])

    accs = accs_ref[...]
    accw = accw_ref[...]
    o_sel = accs[:HEAD_DIM] * (1.0 / accs[HEAD_DIM:HEAD_DIM + 1])
    o_win = accw[:HEAD_DIM] * (1.0 / accw[HEAD_DIM:HEAD_DIM + 1])
    gt = gate_ref[...]
    for h in range(HPG):
        c = slice(h * tq, (h + 1) * tq)
        o_ref[h * HEAD_DIM:(h + 1) * HEAD_DIM, :] = (
            gt[3 * h:3 * h + 1] * o_cmp[:, c] + gt[3 * h + 1:3 * h + 2] * o_sel[:, c]
            + gt[3 * h + 2:3 * h + 3] * o_win[:, c])


def _nsa(qt, kcmp, vcmpt, ksa, kw, vst, vwt, gate, ovlt, *, bsz, seq, nch):
    tq, tk = NSA_TQ, NSA_TK
    assert tq == tk and WINDOW % tk == 0
    nq = seq // tq
    kern = functools.partial(_nsa_kernel, tq=tq, tk=tk, nch=nch)
    full = lambda b, g, i: (b, g, 0, 0)
    full5 = lambda b, g, i: (b, g, 0, 0, 0)
    qd = HPG * HEAD_DIM
    rows = HPG * tq
    return pl.pallas_call(
        kern,
        grid=(bsz, N_KV, nq),
        in_specs=[
            pl.BlockSpec((None, HPG, HEAD_DIM, tq), lambda b, g, i: (b, g, 0, i)),
            pl.BlockSpec((None, None, nch, HEAD_DIM), full),
            pl.BlockSpec((None, None, HEAD_DIM, nch), full),
            pl.BlockSpec((None, None, seq, 2 * HEAD_DIM), full),
            pl.BlockSpec((None, None, seq, HEAD_DIM), full),
            pl.BlockSpec((None, None, seq // tk, V_ROWS, tk), full5),
            pl.BlockSpec((None, None, seq // tk, V_ROWS, tk), full5),
            pl.BlockSpec((None, None, GATE_ROWS, tq), lambda b, g, i: (b, g, 0, i)),
            pl.BlockSpec((MAX_SEL_BLOCKS, nch), lambda b, g, i: (0, 0)),
        ],
        out_specs=pl.BlockSpec((None, qd, tq), lambda b, g, i: (b, g, i)),
        out_shape=jax.ShapeDtypeStruct((bsz, D_ATT, seq), F32),
        scratch_shapes=[pltpu.VMEM((V_ROWS, rows), F32), pltpu.VMEM((V_ROWS, rows), F32),
                        pltpu.VMEM((MAX_SEL_BLOCKS, tq), BF16)],
        compiler_params=_params(("parallel", "parallel", "arbitrary")),
        name="nsa",
    )(qt, kcmp, vcmpt, ksa, kw, vst, vwt, gate, ovlt)


def _route(logits, group=None):
    lane = lax.broadcasted_iota(jnp.int32, logits.shape, 1).astype(F32)
    far = float(LANES)
    is_g = lane < N_EXP_GROUPS
    glog = jnp.where(is_g, logits, -jnp.inf)
    gmax = jnp.max(glog, axis=1, keepdims=True)
    gsum = jnp.sum(jnp.where(is_g, jnp.exp(logits - gmax), 0.0), axis=1, keepdims=True)
    gsel = jnp.min(jnp.where(glog == gmax, lane, far), axis=1, keepdims=True)
    gprob = 1.0 / gsum
    lo = ROUTER_OFF + EXPERTS_PER_GROUP * (gsel if group is None else group)
    in_e = (lane >= lo) & (lane < lo + EXPERTS_PER_GROUP)
    emax = jnp.max(jnp.where(in_e, logits, -jnp.inf), axis=1, keepdims=True)
    eexp = jnp.where(in_e, jnp.exp(logits - emax), 0.0)
    eprob = jnp.where(in_e, eexp / jnp.sum(eexp, axis=1, keepdims=True), -1.0)
    v1 = jnp.max(eprob, axis=1, keepdims=True)
    i1 = jnp.min(jnp.where(eprob == v1, lane, far), axis=1, keepdims=True)
    rest = jnp.where(lane == i1, -1.0, eprob)
    v2 = jnp.max(rest, axis=1, keepdims=True)
    i2 = jnp.min(jnp.where(rest == v2, lane, far), axis=1, keepdims=True)
    den = v1 + v2
    comb = jnp.where(lane == i1, v1 / den * gprob, 0.0) + jnp.where(lane == i2, v2 / den * gprob, 0.0)
    return gsel, comb


def _out_proj_kernel(tm, ys_ref, yat_ref, x_ref, wglu_ref, bglu_ref, gs_ref, ga_ref, wo_ref, g2_ref, wr_ref, br_ref,
                     xt_ref, gsel_ref):
    y = _gelu(ys_ref[...])
    y = y * _sigmoid(_dot(y.astype(BF16), wglu_ref[...]) + bglu_ref[...])
    ysn = y * lax.rsqrt(jnp.mean(y * y, axis=-1, keepdims=True) + EPS) * gs_ref[...]
    yat = yat_ref[...]
    yant = yat * lax.rsqrt(jnp.mean(yat * yat, axis=0, keepdims=True) + EPS) * ga_ref[...]
    yan = yant.T
    x2 = x_ref[...] + _dot(ysn.astype(BF16), wo_ref[:D_SSM, :]) + _dot(yan.astype(BF16), wo_ref[D_SSM:, :])
    for c in range(TT_ROWS):
        xt_ref[pl.ds(c, tm, stride=TT_ROWS), :] = x2[:, c * LANES:(c + 1) * LANES]
    h2 = (x2 * lax.rsqrt(jnp.mean(x2 * x2, axis=-1, keepdims=True) + EPS) * g2_ref[...]).astype(BF16)
    gsel, _ = _route(_dot(h2, wr_ref[...]) + br_ref[...])
    gsel_ref[...] = jnp.broadcast_to(gsel, gsel_ref.shape)


def _out_proj(ys, yat, x2d, wglu, bglu, gs, ga, wo, g2, wr, br, *, seq):
    n_tok = x2d.shape[0]
    tm = PROJ_TM
    nl = seq // tm
    row = lambda i: (i, 0)
    const = lambda i: (0, 0)
    return pl.pallas_call(
        functools.partial(_out_proj_kernel, tm),
        grid=(n_tok // tm,),
        in_specs=[
            pl.BlockSpec((tm, D_SSM), row),
            pl.BlockSpec((None, D_ATT, tm), lambda i: (i // nl, 0, i % nl)),
            pl.BlockSpec((tm, D_MODEL), row),
            pl.BlockSpec((D_SSM, D_SSM), const),
            pl.BlockSpec((1, D_SSM), const),
            pl.BlockSpec((1, D_SSM), const),
            pl.BlockSpec((D_ATT, 1), const),
            pl.BlockSpec((D_SSM + D_ATT, D_MODEL), const),
            pl.BlockSpec((1, D_MODEL), const),
            pl.BlockSpec((D_MODEL, LANES), const),
            pl.BlockSpec((1, LANES), const),
        ],
        out_specs=[pl.BlockSpec((tm * TT_ROWS, LANES), row), pl.BlockSpec((tm, LANES), row)],
        out_shape=[jax.ShapeDtypeStruct((n_tok * TT_ROWS, LANES), F32), jax.ShapeDtypeStruct((n_tok, LANES), F32)],
        compiler_params=_params(("parallel",)),
        name="out_proj",
    )(ys, yat, x2d, wglu, bglu, gs, ga, wo, g2, wr, br)


def _moe_plan(gsel, n_tok):
    tmx = MOE_TM
    n_tiles = n_tok // tmx + N_EXP_GROUPS
    oh = (gsel[:, None] == jnp.arange(N_EXP_GROUPS)[None, :]).astype(jnp.int32)
    csum = jnp.cumsum(oh, axis=0)
    counts = csum[-1]
    rank = jnp.sum(csum * oh, axis=1) - 1
    nt = (counts + tmx - 1) // tmx
    tend = jnp.cumsum(nt)
    toff = tend - nt
    pos = jnp.sum(oh * toff[None, :], axis=1) * tmx + rank
    tile = jnp.arange(n_tiles)
    grp = jnp.minimum(jnp.sum((tile[:, None] >= tend[None, :]).astype(jnp.int32), axis=1), N_EXP_GROUPS - 1)
    nval = jnp.clip(counts[grp] - (tile - toff[grp]) * tmx, 0, tmx)
    return grp.astype(jnp.int32), nval.astype(jnp.int32), pos.astype(jnp.int32)


def _moe_kernel(grp_ref, nval_ref, pos_ref, x_hbm, g2_ref, wr_ref, br_ref, wg_ref, wu_ref, wd_ref, o_hbm,
                xbuf, obuf, abuf, tok_ref, gsem, ssem, *, tmx, n_tiles, n_tok):
    i = pl.program_id(0)
    slot = i % 2

    @pl.when(i == 0)
    def _():
        def place(t, c):
            tok_ref[pos_ref[t]] = t
            return c

        lax.fori_loop(0, n_tok, place, 0, unroll=8)

        def pad_tile(tile, c):
            def pad_row(r, c2):
                tok_ref[tile * tmx + r] = 0
                return c2

            return lax.fori_loop(nval_ref[tile], tmx, pad_row, c)

        lax.fori_loop(0, n_tiles, pad_tile, 0)

    def gather_row(tile, r, dst_slot):
        t = tok_ref[tile * tmx + r]
        return pltpu.make_async_copy(x_hbm.at[pl.ds(pl.multiple_of(t * TT_ROWS, TT_ROWS), TT_ROWS), :],
                                     xbuf.at[dst_slot, pl.ds(r * TT_ROWS, TT_ROWS), :], gsem.at[dst_slot])

    def scatter_row(tile, r, src_slot):
        t = tok_ref[tile * tmx + r]
        return pltpu.make_async_copy(obuf.at[src_slot, pl.ds(r, 1), :], o_hbm.at[pl.ds(t, 1), :], ssem.at[src_slot])

    def gather_wait(src_slot):
        pltpu.make_async_copy(x_hbm.at[pl.ds(0, tmx * TT_ROWS), :], xbuf.at[src_slot], gsem.at[src_slot]).wait()

    def scatter_wait(n, src_slot):
        n8 = pl.multiple_of((n // SUBLANES) * SUBLANES, SUBLANES)

        @pl.when(n8 > 0)
        def _():
            pltpu.make_async_copy(obuf.at[src_slot, pl.ds(0, n8), :], o_hbm.at[pl.ds(0, n8), :],
                                  ssem.at[src_slot]).wait()

        def one(r, c):
            pltpu.make_async_copy(obuf.at[src_slot, pl.ds(0, 1), :], o_hbm.at[pl.ds(0, 1), :],
                                  ssem.at[src_slot]).wait()
            return c

        lax.fori_loop(0, n - n8, one, 0)

    @pl.when(i == 0)
    def _():
        def body(r, c):
            gather_row(0, r, 0).start()
            return c

        lax.fori_loop(0, tmx, body, 0)

    nv = nval_ref[i]
    nv_prev = nval_ref[jnp.maximum(i - 1, 0)]

    @pl.when((i == 0) | (nv_prev > 0))
    def _():
        gather_wait(slot)

    oslot = i % OUT_BUFS
    oprev = (i + OUT_BUFS - 1) % OUT_BUFS

    @pl.when(i >= OUT_BUFS)
    def _():
        scatter_wait(nval_ref[i - OUT_BUFS], oslot)

    nxt = jnp.minimum(i + 1, n_tiles - 1)
    per = tmx // EXPERTS_PER_GROUP
    prev_full = (i >= 1) & (nv_prev == tmx)

    def experts(with_scatter):
        x2 = jnp.concatenate([xbuf[slot, pl.ds(c, tmx, stride=TT_ROWS), :] for c in range(TT_ROWS)], axis=1)
        h = (x2 * lax.rsqrt(jnp.mean(x2 * x2, axis=-1, keepdims=True) + EPS) * g2_ref[...]).astype(BF16)
        _, cw = _route(_dot(h, wr_ref[...]) + br_ref[...], group=grp_ref[i].astype(F32))
        lane = lax.broadcasted_iota(jnp.int32, cw.shape, 1)
        first = ROUTER_OFF + EXPERTS_PER_GROUP * grp_ref[i]
        half = EXPERTS_PER_GROUP // 2
        for k in range(EXPERTS_PER_GROUP):
            if k < half:
                for r in range(k * 2 * per, (k + 1) * 2 * per):
                    gather_row(nxt, r, 1 - slot).start(priority=r % 2)
            elif with_scatter:
                for r in range((k - half) * 2 * per, (k - half + 1) * 2 * per):
                    scatter_row(i - 1, r, oprev).start(priority=r % 2)
            gate = _dot(h, wg_ref[k].astype(BF16))
            up = _dot(h, wu_ref[k].astype(BF16))
            ck = jnp.sum(jnp.where(lane == first + k, cw, 0.0), axis=1, keepdims=True)
            abuf[:, k * D_EXPERT:(k + 1) * D_EXPERT] = (gate * _sigmoid(gate) * up * ck).astype(BF16)
        obuf[oslot] = x2 + _dot(abuf[...], wd_ref[...].astype(BF16))

    @pl.when(prev_full & (nv > 0))
    def _():
        experts(True)

    @pl.when(jnp.logical_not(prev_full) & (nv > 0))
    def _():
        experts(False)

    @pl.when(prev_full & (nv == 0))
    def _():
        def body(r, c):
            scatter_row(i - 1, r, oprev).start()
            return c

        lax.fori_loop(0, tmx, body, 0)

    @pl.when(nv < tmx)
    def _():
        def body(r, c):
            scatter_row(i, r, oslot).start()
            return c

        lax.fori_loop(0, nv, body, 0)

    @pl.when(i == n_tiles - 1)
    def _():
        @pl.when(nv == tmx)
        def _():
            def body(r, c):
                scatter_row(i, r, oslot).start()
                return c

            lax.fori_loop(0, tmx, body, 0)

        @pl.when(nv > 0)
        def _():
            gather_wait(1 - slot)

        for back in range(OUT_BUFS - 1, -1, -1):
            scatter_wait(nval_ref[jnp.maximum(i - back, 0)], (i + OUT_BUFS - back) % OUT_BUFS)


def _moe(x2t, grp, nval, pos, g2, wr, br, wg, wu, wd):
    n_tok = x2t.shape[0] // TT_ROWS
    tmx = MOE_TM
    n_tiles = grp.shape[0]
    kern = functools.partial(_moe_kernel, tmx=tmx, n_tiles=n_tiles, n_tok=n_tok)
    gk = EXPERTS_PER_GROUP * D_EXPERT
    w_bytes = 2 * 3 * EXPERTS_PER_GROUP * D_MODEL * D_EXPERT * 4
    io_bytes = (2 * TT_ROWS * LANES + OUT_BUFS * D_MODEL) * tmx * 4 + tmx * gk * 2
    vmem_limit = w_bytes + io_bytes + 6 * 1024 * 1024
    assert vmem_limit < V7X_VMEM_BYTES
    grid_spec = pltpu.PrefetchScalarGridSpec(
        num_scalar_prefetch=3,
        grid=(n_tiles,),
        in_specs=[
            pl.BlockSpec(memory_space=pl.ANY),
            pl.BlockSpec((1, D_MODEL), lambda i, g, n, t: (0, 0)),
            pl.BlockSpec((D_MODEL, LANES), lambda i, g, n, t: (0, 0)),
            pl.BlockSpec((1, LANES), lambda i, g, n, t: (0, 0)),
            pl.BlockSpec((None, EXPERTS_PER_GROUP, D_MODEL, D_EXPERT), lambda i, g, n, t: (g[i], 0, 0, 0)),
            pl.BlockSpec((None, EXPERTS_PER_GROUP, D_MODEL, D_EXPERT), lambda i, g, n, t: (g[i], 0, 0, 0)),
            pl.BlockSpec((None, gk, D_MODEL), lambda i, g, n, t: (g[i], 0, 0)),
        ],
        out_specs=pl.BlockSpec(memory_space=pl.ANY),
        scratch_shapes=[
            pltpu.VMEM((2, tmx * TT_ROWS, LANES), F32),
            pltpu.VMEM((OUT_BUFS, tmx, D_MODEL), F32),
            pltpu.VMEM((tmx, gk), BF16),
            pltpu.SMEM((n_tiles * tmx,), jnp.int32),
            pltpu.SemaphoreType.DMA((2,)),
            pltpu.SemaphoreType.DMA((OUT_BUFS,)),
        ],
    )
    return pl.pallas_call(
        kern,
        grid_spec=grid_spec,
        out_shape=jax.ShapeDtypeStruct((n_tok, D_MODEL), F32),
        compiler_params=_params(("arbitrary",), vmem_limit),
        name="moe",
    )(grp, nval, pos, x2t, g2, wr, br, wg, wu, wd)


def _block_diag_ones(n, blk):
    i = jnp.arange(n) // blk
    return (i[:, None] == i[None, :]).astype(BF16)


def _layer(x, norm1_g, w_in, lam_re, lam_im, log_step, b_re, b_im, c_re, c_im, d_skip,
           w_glu, b_glu, g_q, g_kc, g_ks, g_kw, pos_k, pos_v, w_ck1, w_ck2, w_cv1, w_cv2,
           out_g_ssm, out_g_att, w_out, norm2_g, w_grp, b_grp, w_exp, b_exp, w_gate, w_up, w_down):
    bsz, seq, _ = x.shape
    assert seq % PROJ_TM == 0 and seq // SEL_BLOCK <= MAX_SEL_BLOCKS
    n_tok = bsz * seq
    x2d = x.reshape(n_tok, D_MODEL)
    q8 = SSM_Q
    n_sub = seq // q8
    nch = seq // CMP_STRIDE

    o_q = D_SSM
    o_kv = D_SSM + D_ATT
    o_gt = o_kv + 6 * D_KV
    kv = lambda i: w_in[:, o_kv + i * D_KV:o_kv + (i + 1) * D_KV]
    wrow = jnp.concatenate([w_in[:, :o_q], kv(0), kv(1), kv(2), kv(4)], axis=1).astype(BF16)
    per_g = HPG * N_BRANCH
    wgt = jnp.zeros((D_MODEL, N_KV * GATE_ROWS), F32)
    for g in range(N_KV):
        wgt = wgt.at[:, g * GATE_ROWS:g * GATE_ROWS + per_g].set(w_in[:, o_gt + g * per_g:o_gt + (g + 1) * per_g])
    wcol = jnp.concatenate([w_in[:, o_q:o_kv], kv(3), kv(5), wgt], axis=1).T.astype(BF16)
    qscale = (HEAD_DIM ** -0.5) * math.log2(math.e)
    gq = (jnp.tile(g_q.astype(F32), N_HEADS) * qscale).reshape(D_ATT, 1)
    gks = jnp.tile(g_ks.astype(F32), N_KV).reshape(1, D_KV)
    gkw = jnp.tile(g_kw.astype(F32), N_KV).reshape(1, D_KV)

    u, qt, kc, vc, ksa, kw, vst, vwt, gate = _in_proj(
        x2d, norm1_g.reshape(1, D_MODEL), wrow, wcol, gq, gks, gkw,
        _block_diag_ones(D_KV, HEAD_DIM), bsz=bsz, seq=seq)

    w_c, t_c, m_c, e_state_t, e_lane_t, pw_re, pw_im, dvec = _s5_weights(
        lam_re, lam_im, log_step, b_re, b_im, c_re, c_im, d_skip, n_sub)
    ys = _s5(u.reshape(bsz, seq, D_SSM), w_c, t_c, m_c, e_state_t, e_lane_t, pw_re, pw_im, dvec,
             bsz=bsz, n_sub=n_sub).reshape(n_tok, D_SSM)

    wide = CMP_STRIDE * HEAD_DIM
    pad8 = lambda p: jnp.zeros((SUBLANES, 2 * wide), F32).at[0].set(p.reshape(-1)).astype(BF16)
    kcmp, vcmpt = _compress(
        kc, vc,
        w_ck1.astype(BF16), w_ck2.astype(BF16), w_cv1.astype(BF16), w_cv2.T.astype(BF16),
        pad8(pos_k), pad8(pos_v), g_kc.astype(F32).reshape(1, HEAD_DIM), bsz=bsz, nch=nch)
    cstart = jnp.arange(nch) * CMP_STRIDE
    sstart = jnp.arange(MAX_SEL_BLOCKS) * SEL_BLOCK
    ovlt = ((cstart[None, :] < sstart[:, None] + SEL_BLOCK) & (cstart[None, :] + CMP_BLOCK > sstart[:, None])
            & (jnp.arange(MAX_SEL_BLOCKS)[:, None] < seq // SEL_BLOCK)
            & (jnp.arange(nch)[None, :] < nch - 1)).astype(BF16)
    yat = _nsa(qt, kcmp, vcmpt, ksa, kw, vst, vwt, gate, ovlt, bsz=bsz, seq=seq, nch=nch)

    wr = jnp.zeros((D_MODEL, LANES), F32)
    wr = wr.at[:, :N_EXP_GROUPS].set(w_grp).at[:, ROUTER_OFF:ROUTER_OFF + N_EXPERTS].set(w_exp).astype(BF16)
    br = jnp.zeros((1, LANES), F32)
    br = br.at[0, :N_EXP_GROUPS].set(b_grp).at[0, ROUTER_OFF:ROUTER_OFF + N_EXPERTS].set(b_exp)
    g2 = norm2_g.reshape(1, D_MODEL).astype(F32)
    x2t, gsel = _out_proj(
        ys, yat, x2d, w_glu.astype(BF16), b_glu.reshape(1, D_SSM).astype(F32),
        out_g_ssm.reshape(1, D_SSM).astype(F32), out_g_att.reshape(D_ATT, 1).astype(F32),
        w_out.astype(BF16), g2, wr, br, seq=seq)

    grp, nval, pos = _moe_plan(gsel[:, 0].astype(jnp.int32), n_tok)
    gshape = (N_EXP_GROUPS, EXPERTS_PER_GROUP, D_MODEL, D_EXPERT)
    out = _moe(x2t, grp, nval, pos, g2, wr, br, w_gate.reshape(gshape), w_up.reshape(gshape),
               w_down.reshape(N_EXP_GROUPS, EXPERTS_PER_GROUP * D_EXPERT, D_MODEL))
    return out.reshape(bsz, seq, D_MODEL)


def kernel(x, norm1_g, w_in, lam_re, lam_im, log_step, b_re, b_im, c_re, c_im, d_skip, w_glu, b_glu, g_q, g_kc, g_ks, g_kw, pos_k, pos_v, w_ck1, w_ck2, w_cv1, w_cv2, out_g_ssm, out_g_att, w_out, norm2_g, w_grp, b_grp, w_exp, b_exp, w_gate, w_up, w_down):
    depth = norm1_g.shape[0]
    for l in range(depth):
        x = _layer(x, norm1_g[l], w_in[l], lam_re[l], lam_im[l], log_step[l], b_re[l], b_im[l], c_re[l],
                   c_im[l], d_skip[l], w_glu[l], b_glu[l], g_q[l], g_kc[l], g_ks[l], g_kw[l], pos_k[l],
                   pos_v[l], w_ck1[l], w_ck2[l], w_cv1[l], w_cv2[l], out_g_ssm[l], out_g_att[l], w_out[l],
                   norm2_g[l], w_grp[l], b_grp[l], w_exp[l], b_exp[l], w_gate[l], w_up[l], w_down[l])
    return x
```

```python
import functools
import math

import jax
import jax.numpy as jnp
from jax import lax
from jax.experimental import pallas as pl
from jax.experimental.pallas import tpu as pltpu

D_MODEL = 1024
D_SSM = 512
SSM_CH = 16
SSM_STATE = 64
D_ATT = 512
HEAD_DIM = 64
N_HEADS = D_ATT // HEAD_DIM
N_KV = 2
HPG = N_HEADS // N_KV
D_KV = N_KV * HEAD_DIM
N_BRANCH = 3
CMP_STRIDE = 16
CMP_BLOCK = 2 * CMP_STRIDE
CMP_HIDDEN = 256
SEL_BLOCK = 64
N_SELECT = 16
WINDOW = 512
N_EXP_GROUPS = 4
EXPERTS_PER_GROUP = 8
N_EXPERTS = N_EXP_GROUPS * EXPERTS_PER_GROUP
D_EXPERT = 256
EPS = 1e-6
NEG = -1e30
FORCE = 1e9

LANES = 128
SUBLANES = 8
SSM_Q = 8
SSM_LT = D_SSM // LANES
ROUTER_OFF = N_EXP_GROUPS
NSA_TQ = 256
NSA_TK = 256
NSA_UNROLL = 8
V_ROWS = HEAD_DIM + 16
MAX_SEL_BLOCKS = 64
MOE_TM = 512
OUT_BUFS = 2
TT_ROWS = D_MODEL // LANES
PROJ_TM = 1024
GATE_ROWS = 16
V7X_VMEM_BYTES = 64 * 1024 * 1024
VMEM_LIMIT = V7X_VMEM_BYTES - 8 * 1024 * 1024

F32 = jnp.float32
BF16 = jnp.bfloat16


def _dot(a, b):
    return jnp.dot(a, b, preferred_element_type=F32)


def _dot_nt(a, b):
    return lax.dot_general(a, b, (((1,), (1,)), ((), ())), preferred_element_type=F32)


def _split_dot(x, w):
    hi = x.astype(BF16)
    lo = (x - hi.astype(F32)).astype(BF16)
    return _dot(hi, w) + _dot(lo, w)


def _gelu(x):
    c = math.sqrt(2.0 / math.pi)
    return 0.5 * x * (1.0 + jnp.tanh(c * (x + 0.044715 * (x * x * x))))


def _sigmoid(x):
    return 1.0 / (1.0 + jnp.exp(-x))


def _params(sem, vmem_limit=VMEM_LIMIT):
    return pltpu.CompilerParams(dimension_semantics=sem, vmem_limit_bytes=vmem_limit)


def _in_proj_kernel(x_ref, g1_ref, wrow_ref, wcol_ref, gq_ref, gks_ref, gkw_ref, bd128_ref,
                    u_ref, qt_ref, kc_ref, vc_ref, ksa_ref, kw_ref, vst_ref, vwt_ref, gate_ref, *, tm, nl):
    x = x_ref[...]
    ms = jnp.mean(x * x, axis=-1, keepdims=True)
    hn = (x * lax.rsqrt(ms + EPS) * g1_ref[...]).astype(BF16)

    pr = _dot(hn, wrow_ref[...])
    u_ref[...] = pr[:, :D_SSM]
    kc, vc, ks, kw = [pr[:, D_SSM + i * D_KV:D_SSM + (i + 1) * D_KV] for i in range(4)]
    kss = _split_dot(ks * ks, bd128_ref[...])
    ksn = ks * lax.rsqrt(kss * (1.0 / HEAD_DIM) + EPS) * gks_ref[...]
    kws = _split_dot(kw * kw, bd128_ref[...])
    kwn = kw * lax.rsqrt(kws * (1.0 / HEAD_DIM) + EPS) * gkw_ref[...]
    t0 = (pl.program_id(0) % nl) * tm
    tpos = t0 + lax.broadcasted_iota(jnp.int32, (tm, MAX_SEL_BLOCKS), 0)
    blk = lax.broadcasted_iota(jnp.int32, (tm, MAX_SEL_BLOCKS), 1)
    onehot = jnp.where(tpos // SEL_BLOCK == blk, 1.0, 0.0).astype(BF16)
    for g in range(N_KV):
        sl = slice(g * HEAD_DIM, (g + 1) * HEAD_DIM)
        kc_ref[g] = kc[:, sl]
        vc_ref[g] = vc[:, sl]
        ksa_ref[g] = jnp.concatenate([ksn[:, sl].astype(BF16), onehot], axis=1)
        kw_ref[g] = kwn[:, sl].astype(BF16)

    pc = _dot_nt(wcol_ref[...], hn)
    gq = gq_ref[...]
    for h in range(N_HEADS):
        sl = slice(h * HEAD_DIM, (h + 1) * HEAD_DIM)
        qh = pc[sl]
        ss = jnp.sum(qh * qh, axis=0, keepdims=True)
        qt_ref[h] = (qh * lax.rsqrt(ss * (1.0 / HEAD_DIM) + EPS) * gq[sl]).astype(BF16)
    ones_rows = jnp.where(lax.broadcasted_iota(jnp.int32, (V_ROWS - HEAD_DIM, tm), 0) == 0, 1.0, 0.0)
    for g in range(N_KV):
        for o_ref, base in ((vst_ref, D_ATT), (vwt_ref, D_ATT + D_KV)):
            vt = jnp.concatenate([pc[base + g * HEAD_DIM:base + (g + 1) * HEAD_DIM], ones_rows], axis=0)
            vt = vt.astype(BF16)
            for j in range(tm // NSA_TK):
                o_ref[g, j] = vt[:, j * NSA_TK:(j + 1) * NSA_TK]
        gb = D_ATT + 2 * D_KV + g * GATE_ROWS
        gate_ref[g] = _sigmoid(pc[gb:gb + GATE_ROWS])


def _in_proj(x2d, g1, wrow, wcol, gq, gks, gkw, bd128, *, bsz, seq):
    tm = PROJ_TM
    nl = seq // tm
    n_tok = bsz * seq
    kern = functools.partial(_in_proj_kernel, tm=tm, nl=nl)
    row = lambda i: (i, 0)
    const = lambda i: (0, 0)
    bgl = lambda i: (i // nl, 0, i % nl, 0)
    n_col = wcol.shape[0]
    jt = tm // NSA_TK

    def kvspec(width):
        return pl.BlockSpec((None, N_KV, tm, width), bgl)

    def kvshape(width, dtype=BF16):
        return jax.ShapeDtypeStruct((bsz, N_KV, seq, width), dtype)

    vt_spec = pl.BlockSpec((None, N_KV, jt, V_ROWS, NSA_TK), lambda i: (i // nl, 0, i % nl, 0, 0))
    vt_shape = jax.ShapeDtypeStruct((bsz, N_KV, seq // NSA_TK, V_ROWS, NSA_TK), BF16)
    return pl.pallas_call(
        kern,
        grid=(n_tok // tm,),
        in_specs=[
            pl.BlockSpec((tm, D_MODEL), row),
            pl.BlockSpec((1, D_MODEL), const),
            pl.BlockSpec((D_MODEL, D_SSM + 4 * D_KV), const),
            pl.BlockSpec((n_col, D_MODEL), const),
            pl.BlockSpec((D_ATT, 1), const),
            pl.BlockSpec((1, D_KV), const),
            pl.BlockSpec((1, D_KV), const),
            pl.BlockSpec((D_KV, D_KV), const),
        ],
        out_specs=[
            pl.BlockSpec((tm, D_SSM), row),
            pl.BlockSpec((None, N_HEADS, HEAD_DIM, tm), lambda i: (i // nl, 0, 0, i % nl)),
            kvspec(HEAD_DIM), kvspec(HEAD_DIM), kvspec(2 * HEAD_DIM), kvspec(HEAD_DIM),
            vt_spec, vt_spec,
            pl.BlockSpec((None, N_KV, GATE_ROWS, tm), lambda i: (i // nl, 0, 0, i % nl)),
        ],
        out_shape=[
            jax.ShapeDtypeStruct((n_tok, D_SSM), F32),
            jax.ShapeDtypeStruct((bsz, N_HEADS, HEAD_DIM, seq), BF16),
            kvshape(HEAD_DIM, F32), kvshape(HEAD_DIM, F32), kvshape(2 * HEAD_DIM), kvshape(HEAD_DIM),
            vt_shape, vt_shape,
            jax.ShapeDtypeStruct((bsz, N_KV, GATE_ROWS, seq), F32),
        ],
        compiler_params=_params(("parallel",)),
        name="in_proj",
    )(x2d, g1, wrow, wcol, gq, gks, gkw, bd128)


def _s5_weights(lam_re, lam_im, log_step, b_re, b_im, c_re, c_im, d_skip, n_sub):
    q = SSM_Q
    lam = lax.complex(lam_re.astype(F32), lam_im.astype(F32))
    step = jnp.exp(log_step.astype(F32))[:, None]
    lam_bar = jnp.exp(lam * step)
    b_bar = ((lam_bar - 1.0) / lam)[..., None] * lax.complex(b_re.astype(F32), b_im.astype(F32))
    c = lax.complex(c_re.astype(F32), c_im.astype(F32))
    pows = [jnp.ones_like(lam_bar)]
    for _ in range(q):
        pows.append(pows[-1] * lam_bar)
    pw = jnp.stack(pows)
    lt, a8 = SSM_LT, LANES // SSM_CH
    hp = a8 * SSM_STATE
    e_lane = (jnp.arange(a8)[:, None] == jnp.arange(LANES)[None, :] // SSM_CH).astype(F32)
    e_state = (jnp.arange(a8)[:, None] == jnp.arange(hp)[None, :] // SSM_STATE).astype(F32)
    e_lane_t = jnp.tile(e_lane, (1, q))
    e_state_t = jnp.tile(e_state, (1, 2))

    kk = jnp.real(jnp.einsum('ghp,kgp,gpi->kghi', c, pw[:q], b_bar))
    km = kk.reshape(q, lt, a8, SSM_CH, SSM_CH).transpose(1, 4, 0, 2, 3).reshape(lt, SSM_CH, q, LANES)
    lag = jnp.arange(q)[None, :] - jnp.arange(q)[:, None]
    kg = km[:, :, jnp.clip(lag, 0, q - 1), :] * (lag >= 0)[None, None, :, :, None].astype(F32)
    kc = kg.transpose(0, 2, 1, 3, 4).reshape(lt, q, 1, SSM_CH, q * LANES)
    t_c = kc.reshape(lt, q * SSM_CH, q * LANES)

    wc = pw[q - 1 - jnp.arange(q)][..., None] * b_bar[None]
    wri = jnp.stack([jnp.real(wc), jnp.imag(wc)])
    wm = (wri.reshape(2, q, lt, a8, SSM_STATE, SSM_CH).transpose(2, 1, 5, 0, 3, 4)
          .reshape(lt, q, 1, SSM_CH, 2 * hp))
    w_c = wm.reshape(lt, q * SSM_CH, 2 * hp)

    cl = c[None] * pw[1:q + 1][:, :, None, :]
    cri = jnp.stack([jnp.real(cl), -jnp.imag(cl)])
    mm = (cri.reshape(2, q, lt, a8, SSM_CH, SSM_STATE).transpose(2, 0, 5, 1, 3, 4)
          .reshape(lt, 2, 1, SSM_STATE, q * LANES))
    m_c = mm.reshape(lt, 2 * SSM_STATE, q * LANES)

    n_lvl = max(1, (n_sub - 1).bit_length())
    lv = [pw[q]]
    for _ in range(n_lvl - 1):
        lv.append(lv[-1] * lv[-1])
    lvs = jnp.stack(lv).reshape(n_lvl, lt, 1, hp)
    pw_re = jnp.real(lvs).transpose(1, 0, 2, 3)
    pw_im = jnp.imag(lvs).transpose(1, 0, 2, 3)
    dvec = jnp.tile(d_skip.astype(F32).reshape(lt, 1, LANES), (1, 1, q))
    return w_c, t_c, m_c, e_state_t, e_lane_t, pw_re, pw_im, dvec


def _s5_kernel(u_ref, wc_ref, tc_ref, mc_ref, es_ref, el_ref, pwr_ref, pwi_ref, d_ref, y_ref, w_ref, t_ref, m_ref,
               *, n_sub, n_lvl):
    half = (LANES // SSM_CH) * SSM_STATE
    q = SSM_Q
    a8 = LANES // SSM_CH

    @pl.when(pl.program_id(1) == 0)
    def _():
        for a in range(a8):
            el = el_ref[a:a + 1, :]
            es = es_ref[a:a + 1, :]
            for s_ in range(q):
                rows = pl.ds((s_ * a8 + a) * SSM_CH, SSM_CH)
                src = pl.ds(s_ * SSM_CH, SSM_CH)
                t_ref[rows, :] = (tc_ref[src, :] * el).astype(BF16)
                w_ref[rows, :] = (wc_ref[src, :] * es).astype(BF16)
            for r in range(2):
                rows = pl.ds((r * a8 + a) * SSM_STATE, SSM_STATE)
                m_ref[rows, :] = (mc_ref[pl.ds(r * SSM_STATE, SSM_STATE), :] * el).astype(BF16)

    u = jnp.concatenate([u_ref[pl.ds(s, n_sub, stride=q), :] for s in range(q)], axis=1)
    ub = u.astype(BF16)
    s_loc = _dot(ub, w_ref[...])
    re = s_loc[:, :half]
    im = s_loc[:, half:]
    rowi = lax.broadcasted_iota(jnp.int32, (n_sub, half), 0)
    for k in range(n_lvl):
        d = 1 << k
        ar = pwr_ref[k]
        ai = pwi_ref[k]
        keep = rowi >= d
        sre = jnp.where(keep, pltpu.roll(re, d, axis=0), 0.0)
        sim = jnp.where(keep, pltpu.roll(im, d, axis=0), 0.0)
        re, im = re + (ar * sre - ai * sim), im + (ar * sim + ai * sre)
    keep = rowi >= 1
    xre = jnp.where(keep, pltpu.roll(re, 1, axis=0), 0.0)
    xim = jnp.where(keep, pltpu.roll(im, 1, axis=0), 0.0)
    xst = jnp.concatenate([xre, xim], axis=1).astype(BF16)
    y = _dot(ub, t_ref[...]) + _dot(xst, m_ref[...]) + d_ref[...] * u
    for j in range(q):
        y_ref[pl.ds(j, n_sub, stride=q), :] = y[:, j * LANES:(j + 1) * LANES]


def _s5(u, w_c, t_c, m_c, e_state_t, e_lane_t, pw_re, pw_im, dvec, *, bsz, n_sub):
    q = SSM_Q
    n_lvl = pw_re.shape[1]
    kern = functools.partial(_s5_kernel, n_sub=n_sub, n_lvl=n_lvl)
    wide = q * LANES
    seq = n_sub * q
    return pl.pallas_call(
        kern,
        grid=(SSM_LT, bsz),
        in_specs=[
            pl.BlockSpec((None, seq, LANES), lambda l, b: (b, 0, l)),
            pl.BlockSpec((None, q * SSM_CH, wide), lambda l, b: (l, 0, 0)),
            pl.BlockSpec((None, q * SSM_CH, wide), lambda l, b: (l, 0, 0)),
            pl.BlockSpec((None, 2 * SSM_STATE, wide), lambda l, b: (l, 0, 0)),
            pl.BlockSpec((LANES // SSM_CH, wide), lambda l, b: (0, 0)),
            pl.BlockSpec((LANES // SSM_CH, wide), lambda l, b: (0, 0)),
            pl.BlockSpec((None, n_lvl, 1, wide // 2), lambda l, b: (l, 0, 0, 0)),
            pl.BlockSpec((None, n_lvl, 1, wide // 2), lambda l, b: (l, 0, 0, 0)),
            pl.BlockSpec((None, 1, wide), lambda l, b: (l, 0, 0)),
        ],
        out_specs=pl.BlockSpec((None, seq, LANES), lambda l, b: (b, 0, l)),
        out_shape=jax.ShapeDtypeStruct((bsz, seq, D_SSM), F32),
        scratch_shapes=[pltpu.VMEM((wide, wide), BF16)] * 3,
        compiler_params=_params(("arbitrary", "arbitrary")),
        name="s5",
    )(u, w_c, t_c, m_c, e_state_t, e_lane_t, pw_re, pw_im, dvec)


def _compress_kernel(kc_ref, vc_ref, w1k_ref, w2k_ref, w1v_ref, w2vt_ref, posk_ref, posv_ref, gkc_ref,
                     kcmp_ref, vcmpt_ref, *, nch):
    half = CMP_STRIDE * HEAD_DIM

    def hidden(x_ref, w1_ref, pos_ref):
        x = jnp.concatenate([x_ref[pl.ds(j, nch, stride=CMP_STRIDE), :] for j in range(CMP_STRIDE)],
                            axis=1).astype(BF16)
        a = _dot(x, w1_ref[:half, :])
        b = _dot(x, w1_ref[half:, :])
        pv = _dot(pos_ref[...], w1_ref[...])[0:1, :]
        hid = a + pltpu.roll(b, nch - 1, axis=0) + pv
        return _gelu(hid).astype(BF16)

    k = _dot(hidden(kc_ref, w1k_ref, posk_ref), w2k_ref[...])
    ms = jnp.mean(k * k, axis=-1, keepdims=True)
    kcmp_ref[...] = (k * lax.rsqrt(ms + EPS) * gkc_ref[...]).astype(BF16)
    vt = _dot_nt(w2vt_ref[...], hidden(vc_ref, w1v_ref, posv_ref))
    coli = lax.broadcasted_iota(jnp.int32, vt.shape, 1)
    vcmpt_ref[...] = jnp.where(coli < nch - 1, vt, 0.0).astype(BF16)


def _compress(kcf, vcf, w1k, w2k, w1v, w2vt, posk, posv, gkc, *, bsz, nch):
    kern = functools.partial(_compress_kernel, nch=nch)
    wide = CMP_STRIDE * HEAD_DIM
    xspec = pl.BlockSpec((None, None, nch * CMP_STRIDE, HEAD_DIM), lambda b, g: (b, g, 0, 0))
    c2 = lambda b, g: (0, 0)
    return pl.pallas_call(
        kern,
        grid=(bsz, N_KV),
        in_specs=[
            xspec, xspec,
            pl.BlockSpec((2 * wide, CMP_HIDDEN), c2), pl.BlockSpec((CMP_HIDDEN, HEAD_DIM), c2),
            pl.BlockSpec((2 * wide, CMP_HIDDEN), c2), pl.BlockSpec((HEAD_DIM, CMP_HIDDEN), c2),
            pl.BlockSpec((SUBLANES, 2 * wide), c2), pl.BlockSpec((SUBLANES, 2 * wide), c2),
            pl.BlockSpec((1, HEAD_DIM), c2),
        ],
        out_specs=[pl.BlockSpec((None, None, nch, HEAD_DIM), lambda b, g: (b, g, 0, 0)),
                   pl.BlockSpec((None, None, HEAD_DIM, nch), lambda b, g: (b, g, 0, 0))],
        out_shape=[jax.ShapeDtypeStruct((bsz, N_KV, nch, HEAD_DIM), BF16),
                   jax.ShapeDtypeStruct((bsz, N_KV, HEAD_DIM, nch), BF16)],
        compiler_params=_params(("parallel", "parallel")),
        name="compress",
    )(kcf, vcf, w1k, w2k, w1v, w2vt, posk, posv, gkc)


def _nsa_kernel(qt_ref, kcmp_ref, vcmpt_ref, ksa_ref, kw_ref, vst_ref, vwt_ref, gate_ref, ovlt_ref, o_ref,
                accs_ref, accw_ref, sel_ref, *, tq, tk, nch):
    qi = pl.program_id(2)
    q0 = qi * tq
    rows = HPG * tq
    qt = jnp.concatenate([qt_ref[h] for h in range(HPG)], axis=1)
    tpos = q0 + lax.broadcasted_iota(jnp.int32, (1, rows), 1) % tq
    krow = lax.broadcasted_iota(jnp.int32, (tk, rows), 0)
    m0 = jnp.full((1, rows), NEG, F32)

    def run_tiles(k_ref, q_all, vt_ref, acc_ref, m, tiles):
        scores = []
        for kt, kind in tiles:
            kc = jnp.maximum(kt, 0) if kind in ('band', 'valid') else kt
            s = _dot(k_ref[pl.ds(pl.multiple_of(kc * tk, tk), tk), :], q_all)
            if kind == 'causal':
                s = jnp.where(kt * tk + krow <= tpos, s, NEG)
            elif kind == 'band':
                s = jnp.where((kt * tk + krow > tpos - WINDOW) & (kt >= 0), s, NEG)
            elif kind == 'valid':
                s = jnp.where(kt >= 0, s, NEG)
            scores.append((kc, s))
        for kc, s in scores:
            m_new = jnp.maximum(m, jnp.max(s, axis=0, keepdims=True))
            alpha = jnp.exp2(m - m_new)
            p = jnp.exp2(s - m_new).astype(BF16)
            acc_ref[...] = alpha * acc_ref[...] + _dot(vt_ref[kc], p)
            m = m_new
        return m

    accw_ref[...] = jnp.zeros_like(accw_ref)
    n_win = WINDOW // tk
    run_tiles(kw_ref, qt, vwt_ref, accw_ref, m0,
              [(qi - n_win, 'band')] + [(qi - n_win + t, 'valid') for t in range(1, n_win)] + [(qi, 'causal')])

    s = _dot(kcmp_ref[...], qt)
    cend = lax.broadcasted_iota(jnp.int32, (nch, rows), 0) * CMP_STRIDE + (CMP_BLOCK - 1)
    s = jnp.where(cend <= tpos, s, NEG)
    m = jnp.max(s, axis=0, keepdims=True)
    p = jnp.exp2(s - m)
    p = p * jnp.where(tpos >= CMP_BLOCK - 1, 1.0 / jnp.sum(p, axis=0, keepdims=True), 0.0)
    o_cmp = _dot(vcmpt_ref[...], p.astype(BF16))

    psum = p[:, 0:tq]
    for h in range(1, HPG):
        psum = psum + p[:, h * tq:(h + 1) * tq]
    hi = psum.astype(BF16)
    lo = (psum - hi.astype(F32)).astype(BF16)
    ovlt = ovlt_ref[...]
    imp = _dot(ovlt, hi) + _dot(ovlt, lo)
    nb = MAX_SEL_BLOCKS
    blk = lax.broadcasted_iota(jnp.int32, (nb, tq), 0)
    cur = (q0 + lax.broadcasted_iota(jnp.int32, (nb, tq), 1)) // SEL_BLOCK
    forced = (blk == 0) | (blk == cur) | (blk == cur - 1)
    imp = jnp.where(forced, FORCE, jnp.where(blk <= cur, imp, NEG))
    sub = SUBLANES
    rowl = lax.broadcasted_iota(jnp.int32, (sub, tq), 0)
    n_seen = (q0 + tq - 1) // SEL_BLOCK + 1
    sel_ref[...] = jnp.zeros_like(sel_ref)
    for nbv in range(2 * N_SELECT, nb + 1, N_SELECT):
        @pl.when((n_seen > nbv - N_SELECT) & (n_seen <= nbv))
        def _(nbv=nbv):
            groups = [imp[r:r + sub] for r in range(0, nbv, sub)]
            ranks = [jnp.zeros((sub, tq), F32) for _ in groups]
            for i in range(nbv):
                ri = jnp.broadcast_to(imp[i:i + 1, :], (sub, tq))
                for gi, x in enumerate(groups):
                    if i < gi * sub:
                        ahead = ri >= x
                    elif i >= (gi + 1) * sub:
                        ahead = ri > x
                    else:
                        ahead = (ri > x) | ((ri == x) & (rowl > i - gi * sub))
                    ranks[gi] = ranks[gi] + jnp.where(ahead, 1.0, 0.0)
            rank = jnp.concatenate(ranks, axis=0)
            sel_ref[0:nbv, :] = jnp.where(rank < N_SELECT, 0.0, NEG).astype(BF16)
    sel = sel_ref[...]
    qa = jnp.concatenate([qt, jnp.concatenate([sel] * HPG, axis=1)], axis=0)

    accs_ref[...] = jnp.zeros_like(accs_ref)
    nu = NSA_UNROLL

    def sel_multi(j, m):
        return run_tiles(ksa_ref, qa, vst_ref, accs_ref, m, [(nu * j + t, None) for t in range(nu)])

    m_sel = lax.fori_loop(0, qi // nu, sel_multi, m0)
    for r in range(nu):
        @pl.when(qi % nu == r)
        def _(r=r):
            base = qi - r
            run_tiles(ksa_ref, qa, vst_ref, accs_ref, m_sel,
                      [(base + t, None) for t in range(r)] + [(qi, 'causal')])

    accs = accs_ref[...]
    accw = accw_ref[...]
    o_sel = accs[:HEAD_DIM] * (1.0 / accs[HEAD_DIM:HEAD_DIM + 1])
    o_win = accw[:HEAD_DIM] * (1.0 / accw[HEAD_DIM:HEAD_DIM + 1])
    gt = gate_ref[...]
    for h in range(HPG):
        c = slice(h * tq, (h + 1) * tq)
        o_ref[h * HEAD_DIM:(h + 1) * HEAD_DIM, :] = (
            gt[3 * h:3 * h + 1] * o_cmp[:, c] + gt[3 * h + 1:3 * h + 2] * o_sel[:, c]
            + gt[3 * h + 2:3 * h + 3] * o_win[:, c])


def _nsa(qt, kcmp, vcmpt, ksa, kw, vst, vwt, gate, ovlt, *, bsz, seq, nch):
    tq, tk = NSA_TQ, NSA_TK
    assert tq == tk and WINDOW % tk == 0
    nq = seq // tq
    kern = functools.partial(_nsa_kernel, tq=tq, tk=tk, nch=nch)
    full = lambda b, g, i: (b, g, 0, 0)
    full5 = lambda b, g, i: (b, g, 0, 0, 0)
    qd = HPG * HEAD_DIM
    rows = HPG * tq
    return pl.pallas_call(
        kern,
        grid=(bsz, N_KV, nq),
        in_specs=[
            pl.BlockSpec((None, HPG, HEAD_DIM, tq), lambda b, g, i: (b, g, 0, i)),
            pl.BlockSpec((None, None, nch, HEAD_DIM), full),
            pl.BlockSpec((None, None, HEAD_DIM, nch), full),
            pl.BlockSpec((None, None, seq, 2 * HEAD_DIM), full),
            pl.BlockSpec((None, None, seq, HEAD_DIM), full),
            pl.BlockSpec((None, None, seq // tk, V_ROWS, tk), full5),
            pl.BlockSpec((None, None, seq // tk, V_ROWS, tk), full5),
            pl.BlockSpec((None, None, GATE_ROWS, tq), lambda b, g, i: (b, g, 0, i)),
            pl.BlockSpec((MAX_SEL_BLOCKS, nch), lambda b, g, i: (0, 0)),
        ],
        out_specs=pl.BlockSpec((None, qd, tq), lambda b, g, i: (b, g, i)),
        out_shape=jax.ShapeDtypeStruct((bsz, D_ATT, seq), F32),
        scratch_shapes=[pltpu.VMEM((V_ROWS, rows), F32), pltpu.VMEM((V_ROWS, rows), F32),
                        pltpu.VMEM((MAX_SEL_BLOCKS, tq), BF16)],
        compiler_params=_params(("parallel", "parallel", "arbitrary")),
        name="nsa",
    )(qt, kcmp, vcmpt, ksa, kw, vst, vwt, gate, ovlt)


def _route(logits, group=None):
    lane = lax.broadcasted_iota(jnp.int32, logits.shape, 1).astype(F32)
    far = float(LANES)
    is_g = lane < N_EXP_GROUPS
    glog = jnp.where(is_g, logits, -jnp.inf)
    gmax = jnp.max(glog, axis=1, keepdims=True)
    gsum = jnp.sum(jnp.where(is_g, jnp.exp(logits - gmax), 0.0), axis=1, keepdims=True)
    gsel = jnp.min(jnp.where(glog == gmax, lane, far), axis=1, keepdims=True)
    gprob = 1.0 / gsum
    lo = ROUTER_OFF + EXPERTS_PER_GROUP * (gsel if group is None else group)
    in_e = (lane >= lo) & (lane < lo + EXPERTS_PER_GROUP)
    emax = jnp.max(jnp.where(in_e, logits, -jnp.inf), axis=1, keepdims=True)
    eexp = jnp.where(in_e, jnp.exp(logits - emax), 0.0)
    eprob = jnp.where(in_e, eexp / jnp.sum(eexp, axis=1, keepdims=True), -1.0)
    v1 = jnp.max(eprob, axis=1, keepdims=True)
    i1 = jnp.min(jnp.where(eprob == v1, lane, far), axis=1, keepdims=True)
    rest = jnp.where(lane == i1, -1.0, eprob)
    v2 = jnp.max(rest, axis=1, keepdims=True)
    i2 = jnp.min(jnp.where(rest == v2, lane, far), axis=1, keepdims=True)
    den = v1 + v2
    comb = jnp.where(lane == i1, v1 / den * gprob, 0.0) + jnp.where(lane == i2, v2 / den * gprob, 0.0)
    return gsel, comb


def _out_proj_kernel(tm, ys_ref, yat_ref, x_ref, wglu_ref, bglu_ref, gs_ref, ga_ref, wo_ref, g2_ref, wr_ref, br_ref,
                     xt_ref, gsel_ref):
    y = _gelu(ys_ref[...])
    y = y * _sigmoid(_dot(y.astype(BF16), wglu_ref[...]) + bglu_ref[...])
    ysn = y * lax.rsqrt(jnp.mean(y * y, axis=-1, keepdims=True) + EPS) * gs_ref[...]
    yat = yat_ref[...]
    yant = yat * lax.rsqrt(jnp.mean(yat * yat, axis=0, keepdims=True) + EPS) * ga_ref[...]
    yan = yant.T
    x2 = x_ref[...] + _dot(ysn.astype(BF16), wo_ref[:D_SSM, :]) + _dot(yan.astype(BF16), wo_ref[D_SSM:, :])
    for c in range(TT_ROWS):
        xt_ref[pl.ds(c, tm, stride=TT_ROWS), :] = x2[:, c * LANES:(c + 1) * LANES]
    h2 = (x2 * lax.rsqrt(jnp.mean(x2 * x2, axis=-1, keepdims=True) + EPS) * g2_ref[...]).astype(BF16)
    gsel, _ = _route(_dot(h2, wr_ref[...]) + br_ref[...])
    gsel_ref[...] = jnp.broadcast_to(gsel, gsel_ref.shape)


def _out_proj(ys, yat, x2d, wglu, bglu, gs, ga, wo, g2, wr, br, *, seq):
    n_tok = x2d.shape[0]
    tm = PROJ_TM
    nl = seq // tm
    row = lambda i: (i, 0)
    const = lambda i: (0, 0)
    return pl.pallas_call(
        functools.partial(_out_proj_kernel, tm),
        grid=(n_tok // tm,),
        in_specs=[
            pl.BlockSpec((tm, D_SSM), row),
            pl.BlockSpec((None, D_ATT, tm), lambda i: (i // nl, 0, i % nl)),
            pl.BlockSpec((tm, D_MODEL), row),
            pl.BlockSpec((D_SSM, D_SSM), const),
            pl.BlockSpec((1, D_SSM), const),
            pl.BlockSpec((1, D_SSM), const),
            pl.BlockSpec((D_ATT, 1), const),
            pl.BlockSpec((D_SSM + D_ATT, D_MODEL), const),
            pl.BlockSpec((1, D_MODEL), const),
            pl.BlockSpec((D_MODEL, LANES), const),
            pl.BlockSpec((1, LANES), const),
        ],
        out_specs=[pl.BlockSpec((tm * TT_ROWS, LANES), row), pl.BlockSpec((tm, LANES), row)],
        out_shape=[jax.ShapeDtypeStruct((n_tok * TT_ROWS, LANES), F32), jax.ShapeDtypeStruct((n_tok, LANES), F32)],
        compiler_params=_params(("parallel",)),
        name="out_proj",
    )(ys, yat, x2d, wglu, bglu, gs, ga, wo, g2, wr, br)


def _moe_plan(gsel, n_tok):
    tmx = MOE_TM
    n_tiles = n_tok // tmx + N_EXP_GROUPS
    oh = (gsel[:, None] == jnp.arange(N_EXP_GROUPS)[None, :]).astype(jnp.int32)
    csum = jnp.cumsum(oh, axis=0)
    counts = csum[-1]
    rank = jnp.sum(csum * oh, axis=1) - 1
    nt = (counts + tmx - 1) // tmx
    tend = jnp.cumsum(nt)
    toff = tend - nt
    pos = jnp.sum(oh * toff[None, :], axis=1) * tmx + rank
    tile = jnp.arange(n_tiles)
    grp = jnp.minimum(jnp.sum((tile[:, None] >= tend[None, :]).astype(jnp.int32), axis=1), N_EXP_GROUPS - 1)
    nval = jnp.clip(counts[grp] - (tile - toff[grp]) * tmx, 0, tmx)
    return grp.astype(jnp.int32), nval.astype(jnp.int32), pos.astype(jnp.int32)


def _moe_kernel(grp_ref, nval_ref, pos_ref, x_hbm, g2_ref, wr_ref, br_ref, wg_ref, wu_ref, wd_ref, o_hbm,
                xbuf, obuf, abuf, tok_ref, gsem, ssem, *, tmx, n_tiles, n_tok):
    i = pl.program_id(0)
    slot = i % 2

    @pl.when(i == 0)
    def _():
        def place(t, c):
            tok_ref[pos_ref[t]] = t
            return c

        lax.fori_loop(0, n_tok, place, 0, unroll=8)

        def pad_tile(tile, c):
            def pad_row(r, c2):
                tok_ref[tile * tmx + r] = 0
                return c2

            return lax.fori_loop(nval_ref[tile], tmx, pad_row, c)

        lax.fori_loop(0, n_tiles, pad_tile, 0)

    def gather_row(tile, r, dst_slot):
        t = tok_ref[tile * tmx + r]
        return pltpu.make_async_copy(x_hbm.at[pl.ds(pl.multiple_of(t * TT_ROWS, TT_ROWS), TT_ROWS), :],
                                     xbuf.at[dst_slot, pl.ds(r * TT_ROWS, TT_ROWS), :], gsem.at[dst_slot])

    def scatter_row(tile, r, src_slot):
        t = tok_ref[tile * tmx + r]
        return pltpu.make_async_copy(obuf.at[src_slot, pl.ds(r, 1), :], o_hbm.at[pl.ds(t, 1), :], ssem.at[src_slot])

    def gather_wait(src_slot):
        pltpu.make_async_copy(x_hbm.at[pl.ds(0, tmx * TT_ROWS), :], xbuf.at[src_slot], gsem.at[src_slot]).wait()

    def scatter_wait(n, src_slot):
        n8 = pl.multiple_of((n // SUBLANES) * SUBLANES, SUBLANES)

        @pl.when(n8 > 0)
        def _():
            pltpu.make_async_copy(obuf.at[src_slot, pl.ds(0, n8), :], o_hbm.at[pl.ds(0, n8), :],
                                  ssem.at[src_slot]).wait()

        def one(r, c):
            pltpu.make_async_copy(obuf.at[src_slot, pl.ds(0, 1), :], o_hbm.at[pl.ds(0, 1), :],
                                  ssem.at[src_slot]).wait()
            return c

        lax.fori_loop(0, n - n8, one, 0)

    @pl.when(i == 0)
    def _():
        def body(r, c):
            gather_row(0, r, 0).start()
            return c

        lax.fori_loop(0, tmx, body, 0)

    nv = nval_ref[i]
    nv_prev = nval_ref[jnp.maximum(i - 1, 0)]

    @pl.when((i == 0) | (nv_prev > 0))
    def _():
        gather_wait(slot)

    oslot = i % OUT_BUFS
    oprev = (i + OUT_BUFS - 1) % OUT_BUFS

    @pl.when(i >= OUT_BUFS)
    def _():
        scatter_wait(nval_ref[i - OUT_BUFS], oslot)

    nxt = jnp.minimum(i + 1, n_tiles - 1)
    per = tmx // EXPERTS_PER_GROUP
    prev_full = (i >= 1) & (nv_prev == tmx)

    def experts(with_scatter):
        x2 = jnp.concatenate([xbuf[slot, pl.ds(c, tmx, stride=TT_ROWS), :] for c in range(TT_ROWS)], axis=1)
        h = (x2 * lax.rsqrt(jnp.mean(x2 * x2, axis=-1, keepdims=True) + EPS) * g2_ref[...]).astype(BF16)
        _, cw = _route(_dot(h, wr_ref[...]) + br_ref[...], group=grp_ref[i].astype(F32))
        lane = lax.broadcasted_iota(jnp.int32, cw.shape, 1)
        first = ROUTER_OFF + EXPERTS_PER_GROUP * grp_ref[i]
        half = EXPERTS_PER_GROUP // 2
        for k in range(EXPERTS_PER_GROUP):
            if k < half:
                for r in range(k * 2 * per, (k + 1) * 2 * per):
                    gather_row(nxt, r, 1 - slot).start(priority=r % 2)
            elif with_scatter:
                for r in range((k - half) * 2 * per, (k - half + 1) * 2 * per):
                    scatter_row(i - 1, r, oprev).start(priority=r % 2)
            gate = _dot(h, wg_ref[k].astype(BF16))
            up = _dot(h, wu_ref[k].astype(BF16))
            ck = jnp.sum(jnp.where(lane == first + k, cw, 0.0), axis=1, keepdims=True)
            abuf[:, k * D_EXPERT:(k + 1) * D_EXPERT] = (gate * _sigmoid(gate) * up * ck).astype(BF16)
        obuf[oslot] = x2 + _dot(abuf[...], wd_ref[...].astype(BF16))

    @pl.when(prev_full & (nv > 0))
    def _():
        experts(True)

    @pl.when(jnp.logical_not(prev_full) & (nv > 0))
    def _():
        experts(False)

    @pl.when(prev_full & (nv == 0))
    def _():
        def body(r, c):
            scatter_row(i - 1, r, oprev).start()
            return c

        lax.fori_loop(0, tmx, body, 0)

    @pl.when(nv < tmx)
    def _():
        def body(r, c):
            scatter_row(i, r, oslot).start()
            return c

        lax.fori_loop(0, nv, body, 0)

    @pl.when(i == n_tiles - 1)
    def _():
        @pl.when(nv == tmx)
        def _():
            def body(r, c):
                scatter_row(i, r, oslot).start()
                return c

            lax.fori_loop(0, tmx, body, 0)

        @pl.when(nv > 0)
        def _():
            gather_wait(1 - slot)

        for back in range(OUT_BUFS - 1, -1, -1):
            scatter_wait(nval_ref[jnp.maximum(i - back, 0)], (i + OUT_BUFS - back) % OUT_BUFS)


def _moe(x2t, grp, nval, pos, g2, wr, br, wg, wu, wd):
    n_tok = x2t.shape[0] // TT_ROWS
    tmx = MOE_TM
    n_tiles = grp.shape[0]
    kern = functools.partial(_moe_kernel, tmx=tmx, n_tiles=n_tiles, n_tok=n_tok)
    gk = EXPERTS_PER_GROUP * D_EXPERT
    w_bytes = 3 * EXPERTS_PER_GROUP * D_MODEL * D_EXPERT * 4
    io_bytes = (2 * TT_ROWS * LANES + OUT_BUFS * D_MODEL) * tmx * 4 + tmx * gk * 2
    vmem_limit = w_bytes + io_bytes + 16 * 1024 * 1024
    assert vmem_limit < V7X_VMEM_BYTES
    grid_spec = pltpu.PrefetchScalarGridSpec(
        num_scalar_prefetch=3,
        grid=(n_tiles,),
        in_specs=[
            pl.BlockSpec(memory_space=pl.ANY),
            pl.BlockSpec((1, D_MODEL), lambda i, g, n, t: (0, 0)),
            pl.BlockSpec((D_MODEL, LANES), lambda i, g, n, t: (0, 0)),
            pl.BlockSpec((1, LANES), lambda i, g, n, t: (0, 0)),
            pl.BlockSpec((None, EXPERTS_PER_GROUP, D_MODEL, D_EXPERT), lambda i, g, n, t: (g[i], 0, 0, 0),
                         pipeline_mode=pl.Buffered(1)),
            pl.BlockSpec((None, EXPERTS_PER_GROUP, D_MODEL, D_EXPERT), lambda i, g, n, t: (g[i], 0, 0, 0),
                         pipeline_mode=pl.Buffered(1)),
            pl.BlockSpec((None, gk, D_MODEL), lambda i, g, n, t: (g[i], 0, 0), pipeline_mode=pl.Buffered(1)),
        ],
        out_specs=pl.BlockSpec(memory_space=pl.ANY),
        scratch_shapes=[
            pltpu.VMEM((2, tmx * TT_ROWS, LANES), F32),
            pltpu.VMEM((OUT_BUFS, tmx, D_MODEL), F32),
            pltpu.VMEM((tmx, gk), BF16),
            pltpu.SMEM((n_tiles * tmx,), jnp.int32),
            pltpu.SemaphoreType.DMA((2,)),
            pltpu.SemaphoreType.DMA((OUT_BUFS,)),
        ],
    )
    return pl.pallas_call(
        kern,
        grid_spec=grid_spec,
        out_shape=jax.ShapeDtypeStruct((n_tok, D_MODEL), F32),
        compiler_params=_params(("arbitrary",), vmem_limit),
        name="moe",
    )(grp, nval, pos, x2t, g2, wr, br, wg, wu, wd)


def _block_diag_ones(n, blk):
    i = jnp.arange(n) // blk
    return (i[:, None] == i[None, :]).astype(BF16)


def _layer(x, norm1_g, w_in, lam_re, lam_im, log_step, b_re, b_im, c_re, c_im, d_skip,
           w_glu, b_glu, g_q, g_kc, g_ks, g_kw, pos_k, pos_v, w_ck1, w_ck2, w_cv1, w_cv2,
           out_g_ssm, out_g_att, w_out, norm2_g, w_grp, b_grp, w_exp, b_exp, w_gate, w_up, w_down):
    bsz, seq, _ = x.shape
    assert seq % PROJ_TM == 0 and seq // SEL_BLOCK <= MAX_SEL_BLOCKS
    n_tok = bsz * seq
    x2d = x.reshape(n_tok, D_MODEL)
    q8 = SSM_Q
    n_sub = seq // q8
    nch = seq // CMP_STRIDE

    o_q = D_SSM
    o_kv = D_SSM + D_ATT
    o_gt = o_kv + 6 * D_KV
    kv = lambda i: w_in[:, o_kv + i * D_KV:o_kv + (i + 1) * D_KV]
    wrow = jnp.concatenate([w_in[:, :o_q], kv(0), kv(1), kv(2), kv(4)], axis=1).astype(BF16)
    per_g = HPG * N_BRANCH
    wgt = jnp.zeros((D_MODEL, N_KV * GATE_ROWS), F32)
    for g in range(N_KV):
        wgt = wgt.at[:, g * GATE_ROWS:g * GATE_ROWS + per_g].set(w_in[:, o_gt + g * per_g:o_gt + (g + 1) * per_g])
    wcol = jnp.concatenate([w_in[:, o_q:o_kv], kv(3), kv(5), wgt], axis=1).T.astype(BF16)
    qscale = (HEAD_DIM ** -0.5) * math.log2(math.e)
    gq = (jnp.tile(g_q.astype(F32), N_HEADS) * qscale).reshape(D_ATT, 1)
    gks = jnp.tile(g_ks.astype(F32), N_KV).reshape(1, D_KV)
    gkw = jnp.tile(g_kw.astype(F32), N_KV).reshape(1, D_KV)

    u, qt, kc, vc, ksa, kw, vst, vwt, gate = _in_proj(
        x2d, norm1_g.reshape(1, D_MODEL), wrow, wcol, gq, gks, gkw,
        _block_diag_ones(D_KV, HEAD_DIM), bsz=bsz, seq=seq)

    w_c, t_c, m_c, e_state_t, e_lane_t, pw_re, pw_im, dvec = _s5_weights(
        lam_re, lam_im, log_step, b_re, b_im, c_re, c_im, d_skip, n_sub)
    ys = _s5(u.reshape(bsz, seq, D_SSM), w_c, t_c, m_c, e_state_t, e_lane_t, pw_re, pw_im, dvec,
             bsz=bsz, n_sub=n_sub).reshape(n_tok, D_SSM)

    wide = CMP_STRIDE * HEAD_DIM
    pad8 = lambda p: jnp.zeros((SUBLANES, 2 * wide), F32).at[0].set(p.reshape(-1)).astype(BF16)
    kcmp, vcmpt = _compress(
        kc, vc,
        w_ck1.astype(BF16), w_ck2.astype(BF16), w_cv1.astype(BF16), w_cv2.T.astype(BF16),
        pad8(pos_k), pad8(pos_v), g_kc.astype(F32).reshape(1, HEAD_DIM), bsz=bsz, nch=nch)
    cstart = jnp.arange(nch) * CMP_STRIDE
    sstart = jnp.arange(MAX_SEL_BLOCKS) * SEL_BLOCK
    ovlt = ((cstart[None, :] < sstart[:, None] + SEL_BLOCK) & (cstart[None, :] + CMP_BLOCK > sstart[:, None])
            & (jnp.arange(MAX_SEL_BLOCKS)[:, None] < seq // SEL_BLOCK)
            & (jnp.arange(nch)[None, :] < nch - 1)).astype(BF16)
    yat = _nsa(qt, kcmp, vcmpt, ksa, kw, vst, vwt, gate, ovlt, bsz=bsz, seq=seq, nch=nch)

    wr = jnp.zeros((D_MODEL, LANES), F32)
    wr = wr.at[:, :N_EXP_GROUPS].set(w_grp).at[:, ROUTER_OFF:ROUTER_OFF + N_EXPERTS].set(w_exp).astype(BF16)
    br = jnp.zeros((1, LANES), F32)
    br = br.at[0, :N_EXP_GROUPS].set(b_grp).at[0, ROUTER_OFF:ROUTER_OFF + N_EXPERTS].set(b_exp)
    g2 = norm2_g.reshape(1, D_MODEL).astype(F32)
    x2t, gsel = _out_proj(
        ys, yat, x2d, w_glu.astype(BF16), b_glu.reshape(1, D_SSM).astype(F32),
        out_g_ssm.reshape(1, D_SSM).astype(F32), out_g_att.reshape(D_ATT, 1).astype(F32),
        w_out.astype(BF16), g2, wr, br, seq=seq)

    grp, nval, pos = _moe_plan(gsel[:, 0].astype(jnp.int32), n_tok)
    gshape = (N_EXP_GROUPS, EXPERTS_PER_GROUP, D_MODEL, D_EXPERT)
    out = _moe(x2t, grp, nval, pos, g2, wr, br, w_gate.reshape(gshape), w_up.reshape(gshape),
               w_down.reshape(N_EXP_GROUPS, EXPERTS_PER_GROUP * D_EXPERT, D_MODEL))
    return out.reshape(bsz, seq, D_MODEL)


def kernel(x, norm1_g, w_in, lam_re, lam_im, log_step, b_re, b_im, c_re, c_im, d_skip, w_glu, b_glu, g_q, g_kc, g_ks, g_kw, pos_k, pos_v, w_ck1, w_ck2, w_cv1, w_cv2, out_g_ssm, out_g_att, w_out, norm2_g, w_grp, b_grp, w_exp, b_exp, w_gate, w_up, w_down):
    depth = norm1_g.shape[0]
    for l in range(depth):
        x = _layer(x, norm1_g[l], w_in[l], lam_re[l], lam_im[l], log_step[l], b_re[l], b_im[l], c_re[l],
                   c_im[l], d_skip[l], w_glu[l], b_glu[l], g_q[l], g_kc[l], g_ks[l], g_kw[l], pos_k[l],
                   pos_v[l], w_ck1[l], w_ck2[l], w_cv1[l], w_cv2[l], out_g_ssm[l], out_g_att[l], w_out[l],
                   norm2_g[l], w_grp[l], b_grp[l], w_exp[l], b_exp[l], w_gate[l], w_up[l], w_down[l])
    return x
```

```python
import functools
import math

import jax
import jax.numpy as jnp
from jax import lax
from jax.experimental import pallas as pl
from jax.experimental.pallas import tpu as pltpu

D_MODEL = 1024
D_SSM = 512
SSM_CH = 16
SSM_STATE = 64
D_ATT = 512
HEAD_DIM = 64
N_HEADS = D_ATT // HEAD_DIM
N_KV = 2
HPG = N_HEADS // N_KV
D_KV = N_KV * HEAD_DIM
N_BRANCH = 3
CMP_STRIDE = 16
CMP_BLOCK = 2 * CMP_STRIDE
CMP_HIDDEN = 256
SEL_BLOCK = 64
N_SELECT = 16
WINDOW = 512
N_EXP_GROUPS = 4
EXPERTS_PER_GROUP = 8
N_EXPERTS = N_EXP_GROUPS * EXPERTS_PER_GROUP
D_EXPERT = 256
EPS = 1e-6
NEG = -1e30
FORCE = 1e9

LANES = 128
SUBLANES = 8
SSM_Q = 8
SSM_LT = D_SSM // LANES
ROUTER_OFF = N_EXP_GROUPS
NSA_TQ = 256
NSA_TK = 256
NSA_UNROLL = 8
V_ROWS = HEAD_DIM + 16
MAX_SEL_BLOCKS = 64
MOE_TM = 512
OUT_BUFS = 2
TT_ROWS = D_MODEL // LANES
PROJ_TM = 1024
GATE_ROWS = 16
V7X_VMEM_BYTES = 64 * 1024 * 1024
VMEM_LIMIT = V7X_VMEM_BYTES - 8 * 1024 * 1024

F32 = jnp.float32
BF16 = jnp.bfloat16


def _dot(a, b):
    return jnp.dot(a, b, preferred_element_type=F32)


def _dot_nt(a, b):
    return lax.dot_general(a, b, (((1,), (1,)), ((), ())), preferred_element_type=F32)


def _split_dot(x, w):
    hi = x.astype(BF16)
    lo = (x - hi.astype(F32)).astype(BF16)
    return _dot(hi, w) + _dot(lo, w)


def _gelu(x):
    c = math.sqrt(2.0 / math.pi)
    return 0.5 * x * (1.0 + jnp.tanh(c * (x + 0.044715 * (x * x * x))))


def _sigmoid(x):
    return 1.0 / (1.0 + jnp.exp(-x))


def _params(sem, vmem_limit=VMEM_LIMIT):
    return pltpu.CompilerParams(dimension_semantics=sem, vmem_limit_bytes=vmem_limit)


def _in_proj_kernel(x_ref, g1_ref, wrow_ref, wcol_ref, gq_ref, gks_ref, gkw_ref, bd128_ref,
                    u_ref, qt_ref, kc_ref, vc_ref, ksa_ref, kw_ref, vst_ref, vwt_ref, gate_ref, *, tm, nl):
    x = x_ref[...]
    ms = jnp.mean(x * x, axis=-1, keepdims=True)
    hn = (x * lax.rsqrt(ms + EPS) * g1_ref[...]).astype(BF16)

    pr = _dot(hn, wrow_ref[...])
    u_ref[...] = pr[:, :D_SSM]
    kc, vc, ks, kw = [pr[:, D_SSM + i * D_KV:D_SSM + (i + 1) * D_KV] for i in range(4)]
    kss = _split_dot(ks * ks, bd128_ref[...])
    ksn = ks * lax.rsqrt(kss * (1.0 / HEAD_DIM) + EPS) * gks_ref[...]
    kws = _split_dot(kw * kw, bd128_ref[...])
    kwn = kw * lax.rsqrt(kws * (1.0 / HEAD_DIM) + EPS) * gkw_ref[...]
    t0 = (pl.program_id(0) % nl) * tm
    tpos = t0 + lax.broadcasted_iota(jnp.int32, (tm, MAX_SEL_BLOCKS), 0)
    blk = lax.broadcasted_iota(jnp.int32, (tm, MAX_SEL_BLOCKS), 1)
    onehot = jnp.where(tpos // SEL_BLOCK == blk, 1.0, 0.0).astype(BF16)
    for g in range(N_KV):
        sl = slice(g * HEAD_DIM, (g + 1) * HEAD_DIM)
        kc_ref[g] = kc[:, sl]
        vc_ref[g] = vc[:, sl]
        ksa_ref[g] = jnp.concatenate([ksn[:, sl].astype(BF16), onehot], axis=1)
        kw_ref[g] = kwn[:, sl].astype(BF16)

    pc = _dot_nt(wcol_ref[...], hn)
    gq = gq_ref[...]
    for h in range(N_HEADS):
        sl = slice(h * HEAD_DIM, (h + 1) * HEAD_DIM)
        qh = pc[sl]
        ss = jnp.sum(qh * qh, axis=0, keepdims=True)
        qt_ref[h] = (qh * lax.rsqrt(ss * (1.0 / HEAD_DIM) + EPS) * gq[sl]).astype(BF16)
    ones_rows = jnp.where(lax.broadcasted_iota(jnp.int32, (V_ROWS - HEAD_DIM, tm), 0) == 0, 1.0, 0.0)
    for g in range(N_KV):
        for o_ref, base in ((vst_ref, D_ATT), (vwt_ref, D_ATT + D_KV)):
            vt = jnp.concatenate([pc[base + g * HEAD_DIM:base + (g + 1) * HEAD_DIM], ones_rows], axis=0)
            vt = vt.astype(BF16)
            for j in range(tm // NSA_TK):
                o_ref[g, j] = vt[:, j * NSA_TK:(j + 1) * NSA_TK]
        gb = D_ATT + 2 * D_KV + g * GATE_ROWS
        gate_ref[g] = _sigmoid(pc[gb:gb + GATE_ROWS])


def _in_proj(x2d, g1, wrow, wcol, gq, gks, gkw, bd128, *, bsz, seq):
    tm = PROJ_TM
    nl = seq // tm
    n_tok = bsz * seq
    kern = functools.partial(_in_proj_kernel, tm=tm, nl=nl)
    row = lambda i: (i, 0)
    const = lambda i: (0, 0)
    bgl = lambda i: (i // nl, 0, i % nl, 0)
    n_col = wcol.shape[0]
    jt = tm // NSA_TK

    def kvspec(width):
        return pl.BlockSpec((None, N_KV, tm, width), bgl)

    def kvshape(width, dtype=BF16):
        return jax.ShapeDtypeStruct((bsz, N_KV, seq, width), dtype)

    vt_spec = pl.BlockSpec((None, N_KV, jt, V_ROWS, NSA_TK), lambda i: (i // nl, 0, i % nl, 0, 0))
    vt_shape = jax.ShapeDtypeStruct((bsz, N_KV, seq // NSA_TK, V_ROWS, NSA_TK), BF16)
    return pl.pallas_call(
        kern,
        grid=(n_tok // tm,),
        in_specs=[
            pl.BlockSpec((tm, D_MODEL), row),
            pl.BlockSpec((1, D_MODEL), const),
            pl.BlockSpec((D_MODEL, D_SSM + 4 * D_KV), const),
            pl.BlockSpec((n_col, D_MODEL), const),
            pl.BlockSpec((D_ATT, 1), const),
            pl.BlockSpec((1, D_KV), const),
            pl.BlockSpec((1, D_KV), const),
            pl.BlockSpec((D_KV, D_KV), const),
        ],
        out_specs=[
            pl.BlockSpec((tm, D_SSM), row),
            pl.BlockSpec((None, N_HEADS, HEAD_DIM, tm), lambda i: (i // nl, 0, 0, i % nl)),
            kvspec(HEAD_DIM), kvspec(HEAD_DIM), kvspec(2 * HEAD_DIM), kvspec(HEAD_DIM),
            vt_spec, vt_spec,
            pl.BlockSpec((None, N_KV, GATE_ROWS, tm), lambda i: (i // nl, 0, 0, i % nl)),
        ],
        out_shape=[
            jax.ShapeDtypeStruct((n_tok, D_SSM), F32),
            jax.ShapeDtypeStruct((bsz, N_HEADS, HEAD_DIM, seq), BF16),
            kvshape(HEAD_DIM, F32), kvshape(HEAD_DIM, F32), kvshape(2 * HEAD_DIM), kvshape(HEAD_DIM),
            vt_shape, vt_shape,
            jax.ShapeDtypeStruct((bsz, N_KV, GATE_ROWS, seq), F32),
        ],
        compiler_params=_params(("parallel",)),
        name="in_proj",
    )(x2d, g1, wrow, wcol, gq, gks, gkw, bd128)


def _s5_weights(lam_re, lam_im, log_step, b_re, b_im, c_re, c_im, d_skip, n_sub):
    q = SSM_Q
    lam = lax.complex(lam_re.astype(F32), lam_im.astype(F32))
    step = jnp.exp(log_step.astype(F32))[:, None]
    lam_bar = jnp.exp(lam * step)
    b_bar = ((lam_bar - 1.0) / lam)[..., None] * lax.complex(b_re.astype(F32), b_im.astype(F32))
    c = lax.complex(c_re.astype(F32), c_im.astype(F32))
    pows = [jnp.ones_like(lam_bar)]
    for _ in range(q):
        pows.append(pows[-1] * lam_bar)
    pw = jnp.stack(pows)
    lt, a8 = SSM_LT, LANES // SSM_CH
    hp = a8 * SSM_STATE
    e_lane = (jnp.arange(a8)[:, None] == jnp.arange(LANES)[None, :] // SSM_CH).astype(F32)
    e_state = (jnp.arange(a8)[:, None] == jnp.arange(hp)[None, :] // SSM_STATE).astype(F32)
    e_lane_t = jnp.tile(e_lane, (1, q))
    e_state_t = jnp.tile(e_state, (1, 2))

    kk = jnp.real(jnp.einsum('ghp,kgp,gpi->kghi', c, pw[:q], b_bar))
    km = kk.reshape(q, lt, a8, SSM_CH, SSM_CH).transpose(1, 4, 0, 2, 3).reshape(lt, SSM_CH, q, LANES)
    lag = jnp.arange(q)[None, :] - jnp.arange(q)[:, None]
    kg = km[:, :, jnp.clip(lag, 0, q - 1), :] * (lag >= 0)[None, None, :, :, None].astype(F32)
    kc = kg.transpose(0, 2, 1, 3, 4).reshape(lt, q, 1, SSM_CH, q * LANES)
    t_c = kc.reshape(lt, q * SSM_CH, q * LANES)

    wc = pw[q - 1 - jnp.arange(q)][..., None] * b_bar[None]
    wri = jnp.stack([jnp.real(wc), jnp.imag(wc)])
    wm = (wri.reshape(2, q, lt, a8, SSM_STATE, SSM_CH).transpose(2, 1, 5, 0, 3, 4)
          .reshape(lt, q, 1, SSM_CH, 2 * hp))
    w_c = wm.reshape(lt, q * SSM_CH, 2 * hp)

    cl = c[None] * pw[1:q + 1][:, :, None, :]
    cri = jnp.stack([jnp.real(cl), -jnp.imag(cl)])
    mm = (cri.reshape(2, q, lt, a8, SSM_CH, SSM_STATE).transpose(2, 0, 5, 1, 3, 4)
          .reshape(lt, 2, 1, SSM_STATE, q * LANES))
    m_c = mm.reshape(lt, 2 * SSM_STATE, q * LANES)

    n_lvl = max(1, (n_sub - 1).bit_length())
    lv = [pw[q]]
    for _ in range(n_lvl - 1):
        lv.append(lv[-1] * lv[-1])
    lvs = jnp.stack(lv).reshape(n_lvl, lt, 1, hp)
    pw_re = jnp.real(lvs).transpose(1, 0, 2, 3)
    pw_im = jnp.imag(lvs).transpose(1, 0, 2, 3)
    dvec = jnp.tile(d_skip.astype(F32).reshape(lt, 1, LANES), (1, 1, q))
    lr = [pw[q]]
    for _ in range(SUBLANES - 1):
        lr.append(lr[-1] * pw[q])
    lrs = jnp.stack(lr).reshape(SUBLANES, lt, 1, hp)
    lr_re = jnp.real(lrs).transpose(1, 0, 2, 3)
    lr_im = jnp.imag(lrs).transpose(1, 0, 2, 3)
    return w_c, t_c, m_c, e_state_t, e_lane_t, pw_re, pw_im, lr_re, lr_im, dvec


def _s5_kernel(u_ref, wc_ref, tc_ref, mc_ref, es_ref, el_ref, pwr_ref, pwi_ref, lrr_ref, lri_ref, d_ref, y_ref,
               w_ref, t_ref, m_ref, xr_ref, xi_ref,
               *, n_sub, n_lvl):
    half = (LANES // SSM_CH) * SSM_STATE
    q = SSM_Q
    a8 = LANES // SSM_CH

    @pl.when(pl.program_id(1) == 0)
    def _():
        for a in range(a8):
            el = el_ref[a:a + 1, :]
            es = es_ref[a:a + 1, :]
            for s_ in range(q):
                rows = pl.ds((s_ * a8 + a) * SSM_CH, SSM_CH)
                src = pl.ds(s_ * SSM_CH, SSM_CH)
                t_ref[rows, :] = (tc_ref[src, :] * el).astype(BF16)
                w_ref[rows, :] = (wc_ref[src, :] * es).astype(BF16)
            for r in range(2):
                rows = pl.ds((r * a8 + a) * SSM_STATE, SSM_STATE)
                m_ref[rows, :] = (mc_ref[pl.ds(r * SSM_STATE, SSM_STATE), :] * el).astype(BF16)

    u = jnp.concatenate([u_ref[pl.ds(s, n_sub, stride=q), :] for s in range(q)], axis=1)
    ub = u.astype(BF16)
    s_loc = _dot(ub, w_ref[...])
    re = s_loc[:, :half]
    im = s_loc[:, half:]
    rowi = lax.broadcasted_iota(jnp.int32, (n_sub, half), 0)

    def scan_levels(re, im, first_lvl, n, rowpos, axis):
        for k in range(n):
            d = 1 << k
            ar = pwr_ref[first_lvl + k]
            ai = pwi_ref[first_lvl + k]
            keep = rowpos >= d
            sre = jnp.where(keep, pltpu.roll(re, d, axis=axis), 0.0)
            sim = jnp.where(keep, pltpu.roll(im, d, axis=axis), 0.0)
            re, im = re + (ar * sre - ai * sim), im + (ar * sim + ai * sre)
        return re, im

    blk = SUBLANES
    lb = blk.bit_length() - 1
    nblk = n_sub // blk
    rpos = lax.broadcasted_iota(jnp.int32, (nblk, blk, half), 1)
    re3, im3 = scan_levels(re.reshape(nblk, blk, half), im.reshape(nblk, blk, half), 0, lb, rpos, 1)
    re = re3.reshape(n_sub, half)
    im = im3.reshape(n_sub, half)
    lanes = [slice(c * LANES, (c + 1) * LANES) for c in range(half // LANES)]
    for c, cs in enumerate(lanes):
        xr_ref[c] = re[:, cs]
        xi_ref[c] = im[:, cs]
    ends = pl.ds(blk - 1, nblk, stride=blk)
    er = jnp.concatenate([xr_ref[c, ends, :] for c in range(len(lanes))], axis=1)
    ei = jnp.concatenate([xi_ref[c, ends, :] for c in range(len(lanes))], axis=1)
    browi = lax.broadcasted_iota(jnp.int32, (nblk, half), 0)
    er, ei = scan_levels(er, ei, lb, n_lvl - lb, browi, 0)
    cr = jnp.where(browi >= 1, pltpu.roll(er, 1, axis=0), 0.0)
    ci = jnp.where(browi >= 1, pltpu.roll(ei, 1, axis=0), 0.0)
    for r in range(blk):
        ar = lrr_ref[r]
        ai = lri_ref[r]
        dr = ar * cr - ai * ci
        di = ar * ci + ai * cr
        rows_r = pl.ds(r, nblk, stride=blk)
        for c, cs in enumerate(lanes):
            xr_ref[c, rows_r, :] = xr_ref[c, rows_r, :] + dr[:, cs]
            xi_ref[c, rows_r, :] = xi_ref[c, rows_r, :] + di[:, cs]
    re = jnp.concatenate([xr_ref[c] for c in range(len(lanes))], axis=1)
    im = jnp.concatenate([xi_ref[c] for c in range(len(lanes))], axis=1)
    keep = rowi >= 1
    xre = jnp.where(keep, pltpu.roll(re, 1, axis=0), 0.0)
    xim = jnp.where(keep, pltpu.roll(im, 1, axis=0), 0.0)
    xst = jnp.concatenate([xre, xim], axis=1).astype(BF16)
    y = _dot(ub, t_ref[...]) + _dot(xst, m_ref[...]) + d_ref[...] * u
    for j in range(q):
        y_ref[pl.ds(j, n_sub, stride=q), :] = y[:, j * LANES:(j + 1) * LANES]


def _s5(u, w_c, t_c, m_c, e_state_t, e_lane_t, pw_re, pw_im, lr_re, lr_im, dvec, *, bsz, n_sub):
    q = SSM_Q
    n_lvl = pw_re.shape[1]
    kern = functools.partial(_s5_kernel, n_sub=n_sub, n_lvl=n_lvl)
    wide = q * LANES
    seq = n_sub * q
    return pl.pallas_call(
        kern,
        grid=(SSM_LT, bsz),
        in_specs=[
            pl.BlockSpec((None, seq, LANES), lambda l, b: (b, 0, l)),
            pl.BlockSpec((None, q * SSM_CH, wide), lambda l, b: (l, 0, 0)),
            pl.BlockSpec((None, q * SSM_CH, wide), lambda l, b: (l, 0, 0)),
            pl.BlockSpec((None, 2 * SSM_STATE, wide), lambda l, b: (l, 0, 0)),
            pl.BlockSpec((LANES // SSM_CH, wide), lambda l, b: (0, 0)),
            pl.BlockSpec((LANES // SSM_CH, wide), lambda l, b: (0, 0)),
            pl.BlockSpec((None, n_lvl, 1, wide // 2), lambda l, b: (l, 0, 0, 0)),
            pl.BlockSpec((None, n_lvl, 1, wide // 2), lambda l, b: (l, 0, 0, 0)),
            pl.BlockSpec((None, SUBLANES, 1, wide // 2), lambda l, b: (l, 0, 0, 0)),
            pl.BlockSpec((None, SUBLANES, 1, wide // 2), lambda l, b: (l, 0, 0, 0)),
            pl.BlockSpec((None, 1, wide), lambda l, b: (l, 0, 0)),
        ],
        out_specs=pl.BlockSpec((None, seq, LANES), lambda l, b: (b, 0, l)),
        out_shape=jax.ShapeDtypeStruct((bsz, seq, D_SSM), F32),
        scratch_shapes=[pltpu.VMEM((wide, wide), BF16)] * 3
        + [pltpu.VMEM((wide // 2 // LANES, n_sub, LANES), F32)] * 2,
        compiler_params=_params(("arbitrary", "arbitrary")),
        name="s5",
    )(u, w_c, t_c, m_c, e_state_t, e_lane_t, pw_re, pw_im, lr_re, lr_im, dvec)


def _compress_kernel(kc_ref, vc_ref, w1k_ref, w2k_ref, w1v_ref, w2vt_ref, posk_ref, posv_ref, gkc_ref,
                     kcmp_ref, vcmpt_ref, *, nch):
    half = CMP_STRIDE * HEAD_DIM

    def hidden(x_ref, w1_ref, pos_ref):
        x = jnp.concatenate([x_ref[pl.ds(j, nch, stride=CMP_STRIDE), :] for j in range(CMP_STRIDE)],
                            axis=1).astype(BF16)
        a = _dot(x, w1_ref[:half, :])
        b = _dot(x, w1_ref[half:, :])
        pv = _dot(pos_ref[...], w1_ref[...])[0:1, :]
        hid = a + pltpu.roll(b, nch - 1, axis=0) + pv
        return _gelu(hid).astype(BF16)

    k = _dot(hidden(kc_ref, w1k_ref, posk_ref), w2k_ref[...])
    ms = jnp.mean(k * k, axis=-1, keepdims=True)
    kcmp_ref[...] = (k * lax.rsqrt(ms + EPS) * gkc_ref[...]).astype(BF16)
    vt = _dot_nt(w2vt_ref[...], hidden(vc_ref, w1v_ref, posv_ref))
    coli = lax.broadcasted_iota(jnp.int32, vt.shape, 1)
    vcmpt_ref[...] = jnp.where(coli < nch - 1, vt, 0.0).astype(BF16)


def _compress(kcf, vcf, w1k, w2k, w1v, w2vt, posk, posv, gkc, *, bsz, nch):
    kern = functools.partial(_compress_kernel, nch=nch)
    wide = CMP_STRIDE * HEAD_DIM
    xspec = pl.BlockSpec((None, None, nch * CMP_STRIDE, HEAD_DIM), lambda b, g: (b, g, 0, 0))
    c2 = lambda b, g: (0, 0)
    return pl.pallas_call(
        kern,
        grid=(bsz, N_KV),
        in_specs=[
            xspec, xspec,
            pl.BlockSpec((2 * wide, CMP_HIDDEN), c2), pl.BlockSpec((CMP_HIDDEN, HEAD_DIM), c2),
            pl.BlockSpec((2 * wide, CMP_HIDDEN), c2), pl.BlockSpec((HEAD_DIM, CMP_HIDDEN), c2),
            pl.BlockSpec((SUBLANES, 2 * wide), c2), pl.BlockSpec((SUBLANES, 2 * wide), c2),
            pl.BlockSpec((1, HEAD_DIM), c2),
        ],
        out_specs=[pl.BlockSpec((None, None, nch, HEAD_DIM), lambda b, g: (b, g, 0, 0)),
                   pl.BlockSpec((None, None, HEAD_DIM, nch), lambda b, g: (b, g, 0, 0))],
        out_shape=[jax.ShapeDtypeStruct((bsz, N_KV, nch, HEAD_DIM), BF16),
                   jax.ShapeDtypeStruct((bsz, N_KV, HEAD_DIM, nch), BF16)],
        compiler_params=_params(("parallel", "parallel")),
        name="compress",
    )(kcf, vcf, w1k, w2k, w1v, w2vt, posk, posv, gkc)


def _nsa_kernel(qt_ref, kcmp_ref, vcmpt_ref, ksa_ref, kw_ref, vst_ref, vwt_ref, gate_ref, ovlt_ref, o_ref,
                accs_ref, accw_ref, sel_ref, *, tq, tk, nch):
    qi = pl.program_id(2)
    q0 = qi * tq
    rows = HPG * tq
    qt = jnp.concatenate([qt_ref[h] for h in range(HPG)], axis=1)
    tpos = q0 + lax.broadcasted_iota(jnp.int32, (1, rows), 1) % tq
    krow = lax.broadcasted_iota(jnp.int32, (tk, rows), 0)
    m0 = jnp.full((1, rows), NEG, F32)

    def run_tiles(k_ref, q_all, vt_ref, acc_ref, m, tiles):
        scores = []
        for kt, kind in tiles:
            kc = jnp.maximum(kt, 0) if kind in ('band', 'valid') else kt
            s = _dot(k_ref[pl.ds(pl.multiple_of(kc * tk, tk), tk), :], q_all)
            if kind == 'causal':
                s = jnp.where(kt * tk + krow <= tpos, s, NEG)
            elif kind == 'band':
                s = jnp.where((kt * tk + krow > tpos - WINDOW) & (kt >= 0), s, NEG)
            elif kind == 'valid':
                s = jnp.where(kt >= 0, s, NEG)
            scores.append((kc, s))
        for kc, s in scores:
            m_new = jnp.maximum(m, jnp.max(s, axis=0, keepdims=True))
            alpha = jnp.exp2(m - m_new)
            p = jnp.exp2(s - m_new).astype(BF16)
            acc_ref[...] = alpha * acc_ref[...] + _dot(vt_ref[kc], p)
            m = m_new
        return m

    accw_ref[...] = jnp.zeros_like(accw_ref)
    n_win = WINDOW // tk
    run_tiles(kw_ref, qt, vwt_ref, accw_ref, m0,
              [(qi - n_win, 'band')] + [(qi - n_win + t, 'valid') for t in range(1, n_win)] + [(qi, 'causal')])

    s = _dot(kcmp_ref[...], qt)
    cend = lax.broadcasted_iota(jnp.int32, (nch, rows), 0) * CMP_STRIDE + (CMP_BLOCK - 1)
    s = jnp.where(cend <= tpos, s, NEG)
    m = jnp.max(s, axis=0, keepdims=True)
    p = jnp.exp2(s - m)
    p = p * jnp.where(tpos >= CMP_BLOCK - 1, 1.0 / jnp.sum(p, axis=0, keepdims=True), 0.0)
    o_cmp = _dot(vcmpt_ref[...], p.astype(BF16))

    psum = p[:, 0:tq]
    for h in range(1, HPG):
        psum = psum + p[:, h * tq:(h + 1) * tq]
    hi = psum.astype(BF16)
    lo = (psum - hi.astype(F32)).astype(BF16)
    ovlt = ovlt_ref[...]
    imp = _dot(ovlt, hi) + _dot(ovlt, lo)
    nb = MAX_SEL_BLOCKS
    blk = lax.broadcasted_iota(jnp.int32, (nb, tq), 0)
    cur = (q0 + lax.broadcasted_iota(jnp.int32, (nb, tq), 1)) // SEL_BLOCK
    forced = (blk == 0) | (blk == cur) | (blk == cur - 1)
    imp = jnp.where(forced, FORCE, jnp.where(blk <= cur, imp, NEG))
    sub = SUBLANES
    rowl = lax.broadcasted_iota(jnp.int32, (sub, tq), 0)
    n_seen = (q0 + tq - 1) // SEL_BLOCK + 1
    sel_ref[...] = jnp.zeros_like(sel_ref)
    for nbv in range(2 * N_SELECT, nb + 1, N_SELECT):
        @pl.when((n_seen > nbv - N_SELECT) & (n_seen <= nbv))
        def _(nbv=nbv):
            groups = [imp[r:r + sub] for r in range(0, nbv, sub)]
            ranks = [jnp.zeros((sub, tq), F32) for _ in groups]
            for i in range(nbv):
                ri = jnp.broadcast_to(imp[i:i + 1, :], (sub, tq))
                for gi, x in enumerate(groups):
                    if i < gi * sub:
                        ahead = ri >= x
                    elif i >= (gi + 1) * sub:
                        ahead = ri > x
                    else:
                        ahead = (ri > x) | ((ri == x) & (rowl > i - gi * sub))
                    ranks[gi] = ranks[gi] + jnp.where(ahead, 1.0, 0.0)
            rank = jnp.concatenate(ranks, axis=0)
            sel_ref[0:nbv, :] = jnp.where(rank < N_SELECT, 0.0, NEG).astype(BF16)
    sel = sel_ref[...]
    qa = jnp.concatenate([qt, jnp.concatenate([sel] * HPG, axis=1)], axis=0)

    accs_ref[...] = jnp.zeros_like(accs_ref)
    nu = NSA_UNROLL

    def sel_multi(j, m):
        return run_tiles(ksa_ref, qa, vst_ref, accs_ref, m, [(nu * j + t, None) for t in range(nu)])

    m_sel = lax.fori_loop(0, qi // nu, sel_multi, m0)
    for r in range(nu):
        @pl.when(qi % nu == r)
        def _(r=r):
            base = qi - r
            run_tiles(ksa_ref, qa, vst_ref, accs_ref, m_sel,
                      [(base + t, None) for t in range(r)] + [(qi, 'causal')])

    accs = accs_ref[...]
    accw = accw_ref[...]
    o_sel = accs[:HEAD_DIM] * (1.0 / accs[HEAD_DIM:HEAD_DIM + 1])
    o_win = accw[:HEAD_DIM] * (1.0 / accw[HEAD_DIM:HEAD_DIM + 1])
    gt = gate_ref[...]
    for h in range(HPG):
        c = slice(h * tq, (h + 1) * tq)
        o_ref[h * HEAD_DIM:(h + 1) * HEAD_DIM, :] = (
            gt[3 * h:3 * h + 1] * o_cmp[:, c] + gt[3 * h + 1:3 * h + 2] * o_sel[:, c]
            + gt[3 * h + 2:3 * h + 3] * o_win[:, c])


def _nsa(qt, kcmp, vcmpt, ksa, kw, vst, vwt, gate, ovlt, *, bsz, seq, nch):
    tq, tk = NSA_TQ, NSA_TK
    assert tq == tk and WINDOW % tk == 0
    nq = seq // tq
    kern = functools.partial(_nsa_kernel, tq=tq, tk=tk, nch=nch)
    full = lambda b, g, i: (b, g, 0, 0)
    full5 = lambda b, g, i: (b, g, 0, 0, 0)
    qd = HPG * HEAD_DIM
    rows = HPG * tq
    return pl.pallas_call(
        kern,
        grid=(bsz, N_KV, nq),
        in_specs=[
            pl.BlockSpec((None, HPG, HEAD_DIM, tq), lambda b, g, i: (b, g, 0, i)),
            pl.BlockSpec((None, None, nch, HEAD_DIM), full),
            pl.BlockSpec((None, None, HEAD_DIM, nch), full),
            pl.BlockSpec((None, None, seq, 2 * HEAD_DIM), full),
            pl.BlockSpec((None, None, seq, HEAD_DIM), full),
            pl.BlockSpec((None, None, seq // tk, V_ROWS, tk), full5),
            pl.BlockSpec((None, None, seq // tk, V_ROWS, tk), full5),
            pl.BlockSpec((None, None, GATE_ROWS, tq), lambda b, g, i: (b, g, 0, i)),
            pl.BlockSpec((MAX_SEL_BLOCKS, nch), lambda b, g, i: (0, 0)),
        ],
        out_specs=pl.BlockSpec((None, qd, tq), lambda b, g, i: (b, g, i)),
        out_shape=jax.ShapeDtypeStruct((bsz, D_ATT, seq), F32),
        scratch_shapes=[pltpu.VMEM((V_ROWS, rows), F32), pltpu.VMEM((V_ROWS, rows), F32),
                        pltpu.VMEM((MAX_SEL_BLOCKS, tq), BF16)],
        compiler_params=_params(("parallel", "parallel", "arbitrary")),
        name="nsa",
    )(qt, kcmp, vcmpt, ksa, kw, vst, vwt, gate, ovlt)


def _route(logits, group=None):
    lane = lax.broadcasted_iota(jnp.int32, logits.shape, 1).astype(F32)
    far = float(LANES)
    is_g = lane < N_EXP_GROUPS
    glog = jnp.where(is_g, logits, -jnp.inf)
    gmax = jnp.max(glog, axis=1, keepdims=True)
    gsum = jnp.sum(jnp.where(is_g, jnp.exp(logits - gmax), 0.0), axis=1, keepdims=True)
    gsel = jnp.min(jnp.where(glog == gmax, lane, far), axis=1, keepdims=True)
    gprob = 1.0 / gsum
    lo = ROUTER_OFF + EXPERTS_PER_GROUP * (gsel if group is None else group)
    in_e = (lane >= lo) & (lane < lo + EXPERTS_PER_GROUP)
    emax = jnp.max(jnp.where(in_e, logits, -jnp.inf), axis=1, keepdims=True)
    eexp = jnp.where(in_e, jnp.exp(logits - emax), 0.0)
    eprob = jnp.where(in_e, eexp / jnp.sum(eexp, axis=1, keepdims=True), -1.0)
    v1 = jnp.max(eprob, axis=1, keepdims=True)
    i1 = jnp.min(jnp.where(eprob == v1, lane, far), axis=1, keepdims=True)
    rest = jnp.where(lane == i1, -1.0, eprob)
    v2 = jnp.max(rest, axis=1, keepdims=True)
    i2 = jnp.min(jnp.where(rest == v2, lane, far), axis=1, keepdims=True)
    den = v1 + v2
    comb = jnp.where(lane == i1, v1 / den * gprob, 0.0) + jnp.where(lane == i2, v2 / den * gprob, 0.0)
    return gsel, comb


def _out_proj_kernel(tm, ys_ref, yat_ref, x_ref, wglu_ref, bglu_ref, gs_ref, ga_ref, wo_ref, g2_ref, wr_ref, br_ref,
                     xt_ref, gsel_ref):
    y = _gelu(ys_ref[...])
    y = y * _sigmoid(_dot(y.astype(BF16), wglu_ref[...]) + bglu_ref[...])
    ysn = y * lax.rsqrt(jnp.mean(y * y, axis=-1, keepdims=True) + EPS) * gs_ref[...]
    yat = yat_ref[...]
    yant = yat * lax.rsqrt(jnp.mean(yat * yat, axis=0, keepdims=True) + EPS) * ga_ref[...]
    yan = yant.T
    x2 = x_ref[...] + _dot(ysn.astype(BF16), wo_ref[:D_SSM, :]) + _dot(yan.astype(BF16), wo_ref[D_SSM:, :])
    for c in range(TT_ROWS):
        xt_ref[pl.ds(c, tm, stride=TT_ROWS), :] = x2[:, c * LANES:(c + 1) * LANES]
    h2 = (x2 * lax.rsqrt(jnp.mean(x2 * x2, axis=-1, keepdims=True) + EPS) * g2_ref[...]).astype(BF16)
    gsel, _ = _route(_dot(h2, wr_ref[...]) + br_ref[...])
    gsel_ref[...] = jnp.broadcast_to(gsel, gsel_ref.shape)


def _out_proj(ys, yat, x2d, wglu, bglu, gs, ga, wo, g2, wr, br, *, seq):
    n_tok = x2d.shape[0]
    tm = PROJ_TM
    nl = seq // tm
    row = lambda i: (i, 0)
    const = lambda i: (0, 0)
    return pl.pallas_call(
        functools.partial(_out_proj_kernel, tm),
        grid=(n_tok // tm,),
        in_specs=[
            pl.BlockSpec((tm, D_SSM), row),
            pl.BlockSpec((None, D_ATT, tm), lambda i: (i // nl, 0, i % nl)),
            pl.BlockSpec((tm, D_MODEL), row),
            pl.BlockSpec((D_SSM, D_SSM), const),
            pl.BlockSpec((1, D_SSM), const),
            pl.BlockSpec((1, D_SSM), const),
            pl.BlockSpec((D_ATT, 1), const),
            pl.BlockSpec((D_SSM + D_ATT, D_MODEL), const),
            pl.BlockSpec((1, D_MODEL), const),
            pl.BlockSpec((D_MODEL, LANES), const),
            pl.BlockSpec((1, LANES), const),
        ],
        out_specs=[pl.BlockSpec((tm * TT_ROWS, LANES), row), pl.BlockSpec((tm, LANES), row)],
        out_shape=[jax.ShapeDtypeStruct((n_tok * TT_ROWS, LANES), F32), jax.ShapeDtypeStruct((n_tok, LANES), F32)],
        compiler_params=_params(("parallel",)),
        name="out_proj",
    )(ys, yat, x2d, wglu, bglu, gs, ga, wo, g2, wr, br)


def _moe_plan(gsel, n_tok):
    tmx = MOE_TM
    n_tiles = n_tok // tmx + N_EXP_GROUPS
    oh = (gsel[:, None] == jnp.arange(N_EXP_GROUPS)[None, :]).astype(jnp.int32)
    csum = jnp.cumsum(oh, axis=0)
    counts = csum[-1]
    rank = jnp.sum(csum * oh, axis=1) - 1
    nt = (counts + tmx - 1) // tmx
    tend = jnp.cumsum(nt)
    toff = tend - nt
    pos = jnp.sum(oh * toff[None, :], axis=1) * tmx + rank
    tile = jnp.arange(n_tiles)
    grp = jnp.minimum(jnp.sum((tile[:, None] >= tend[None, :]).astype(jnp.int32), axis=1), N_EXP_GROUPS - 1)
    nval = jnp.clip(counts[grp] - (tile - toff[grp]) * tmx, 0, tmx)
    return grp.astype(jnp.int32), nval.astype(jnp.int32), pos.astype(jnp.int32)


def _moe_kernel(grp_ref, nval_ref, pos_ref, x_hbm, g2_ref, wr_ref, br_ref, wg_ref, wu_ref, wd_ref, o_hbm,
                xbuf, obuf, abuf, tok_ref, gsem, ssem, *, tmx, n_tiles, n_tok):
    i = pl.program_id(0)
    slot = i % 2

    @pl.when(i == 0)
    def _():
        def place(t, c):
            tok_ref[pos_ref[t]] = t
            return c

        lax.fori_loop(0, n_tok, place, 0, unroll=8)

        def pad_tile(tile, c):
            def pad_row(r, c2):
                tok_ref[tile * tmx + r] = 0
                return c2

            return lax.fori_loop(nval_ref[tile], tmx, pad_row, c)

        lax.fori_loop(0, n_tiles, pad_tile, 0)

    def gather_row(tile, r, dst_slot):
        t = tok_ref[tile * tmx + r]
        return pltpu.make_async_copy(x_hbm.at[pl.ds(pl.multiple_of(t * TT_ROWS, TT_ROWS), TT_ROWS), :],
                                     xbuf.at[dst_slot, pl.ds(r * TT_ROWS, TT_ROWS), :], gsem.at[dst_slot])

    def scatter_row(tile, r, src_slot):
        t = tok_ref[tile * tmx + r]
        return pltpu.make_async_copy(obuf.at[src_slot, pl.ds(r, 1), :], o_hbm.at[pl.ds(t, 1), :], ssem.at[src_slot])

    def gather_wait(src_slot):
        pltpu.make_async_copy(x_hbm.at[pl.ds(0, tmx * TT_ROWS), :], xbuf.at[src_slot], gsem.at[src_slot]).wait()

    def scatter_wait(n, src_slot):
        n8 = pl.multiple_of((n // SUBLANES) * SUBLANES, SUBLANES)

        @pl.when(n8 > 0)
        def _():
            pltpu.make_async_copy(obuf.at[src_slot, pl.ds(0, n8), :], o_hbm.at[pl.ds(0, n8), :],
                                  ssem.at[src_slot]).wait()

        def one(r, c):
            pltpu.make_async_copy(obuf.at[src_slot, pl.ds(0, 1), :], o_hbm.at[pl.ds(0, 1), :],
                                  ssem.at[src_slot]).wait()
            return c

        lax.fori_loop(0, n - n8, one, 0)

    @pl.when(i == 0)
    def _():
        def body(r, c):
            gather_row(0, r, 0).start()
            return c

        lax.fori_loop(0, tmx, body, 0)

    nv = nval_ref[i]
    nv_prev = nval_ref[jnp.maximum(i - 1, 0)]

    @pl.when((i == 0) | (nv_prev > 0))
    def _():
        gather_wait(slot)

    oslot = i % OUT_BUFS
    oprev = (i + OUT_BUFS - 1) % OUT_BUFS

    @pl.when(i >= OUT_BUFS)
    def _():
        scatter_wait(nval_ref[i - OUT_BUFS], oslot)

    nxt = jnp.minimum(i + 1, n_tiles - 1)
    per = tmx // EXPERTS_PER_GROUP
    prev_full = (i >= 1) & (nv_prev == tmx)

    def experts(with_scatter):
        x2 = jnp.concatenate([xbuf[slot, pl.ds(c, tmx, stride=TT_ROWS), :] for c in range(TT_ROWS)], axis=1)
        h = (x2 * lax.rsqrt(jnp.mean(x2 * x2, axis=-1, keepdims=True) + EPS) * g2_ref[...]).astype(BF16)
        _, cw = _route(_dot(h, wr_ref[...]) + br_ref[...], group=grp_ref[i].astype(F32))
        lane = lax.broadcasted_iota(jnp.int32, cw.shape, 1)
        first = ROUTER_OFF + EXPERTS_PER_GROUP * grp_ref[i]
        half = EXPERTS_PER_GROUP // 2
        for k in range(EXPERTS_PER_GROUP):
            if k < half:
                for r in range(k * 2 * per, (k + 1) * 2 * per):
                    gather_row(nxt, r, 1 - slot).start(priority=r % 2)
            elif with_scatter:
                for r in range((k - half) * 2 * per, (k - half + 1) * 2 * per):
                    scatter_row(i - 1, r, oprev).start(priority=r % 2)
            gate = _dot(h, wg_ref[k].astype(BF16))
            up = _dot(h, wu_ref[k].astype(BF16))
            ck = jnp.sum(jnp.where(lane == first + k, cw, 0.0), axis=1, keepdims=True)
            abuf[:, k * D_EXPERT:(k + 1) * D_EXPERT] = (gate * _sigmoid(gate) * up * ck).astype(BF16)
        obuf[oslot] = x2 + _dot(abuf[...], wd_ref[...].astype(BF16))

    @pl.when(prev_full & (nv > 0))
    def _():
        experts(True)

    @pl.when(jnp.logical_not(prev_full) & (nv > 0))
    def _():
        experts(False)

    @pl.when(prev_full & (nv == 0))
    def _():
        def body(r, c):
            scatter_row(i - 1, r, oprev).start()
            return c

        lax.fori_loop(0, tmx, body, 0)

    @pl.when(nv < tmx)
    def _():
        def body(r, c):
            scatter_row(i, r, oslot).start()
            return c

        lax.fori_loop(0, nv, body, 0)

    @pl.when(i == n_tiles - 1)
    def _():
        @pl.when(nv == tmx)
        def _():
            def body(r, c):
                scatter_row(i, r, oslot).start()
                return c

            lax.fori_loop(0, tmx, body, 0)

        @pl.when(nv > 0)
        def _():
            gather_wait(1 - slot)

        for back in range(OUT_BUFS - 1, -1, -1):
            scatter_wait(nval_ref[jnp.maximum(i - back, 0)], (i + OUT_BUFS - back) % OUT_BUFS)


def _moe(x2t, grp, nval, pos, g2, wr, br, wg, wu, wd):
    n_tok = x2t.shape[0] // TT_ROWS
    tmx = MOE_TM
    n_tiles = grp.shape[0]
    kern = functools.partial(_moe_kernel, tmx=tmx, n_tiles=n_tiles, n_tok=n_tok)
    gk = EXPERTS_PER_GROUP * D_EXPERT
    w_bytes = 3 * EXPERTS_PER_GROUP * D_MODEL * D_EXPERT * 4
    io_bytes = (2 * TT_ROWS * LANES + OUT_BUFS * D_MODEL) * tmx * 4 + tmx * gk * 2
    vmem_limit = w_bytes + io_bytes + 16 * 1024 * 1024
    assert vmem_limit < V7X_VMEM_BYTES
    grid_spec = pltpu.PrefetchScalarGridSpec(
        num_scalar_prefetch=3,
        grid=(n_tiles,),
        in_specs=[
            pl.BlockSpec(memory_space=pl.ANY),
            pl.BlockSpec((1, D_MODEL), lambda i, g, n, t: (0, 0)),
            pl.BlockSpec((D_MODEL, LANES), lambda i, g, n, t: (0, 0)),
            pl.BlockSpec((1, LANES), lambda i, g, n, t: (0, 0)),
            pl.BlockSpec((None, EXPERTS_PER_GROUP, D_MODEL, D_EXPERT), lambda i, g, n, t: (g[i], 0, 0, 0),
                         pipeline_mode=pl.Buffered(1)),
            pl.BlockSpec((None, EXPERTS_PER_GROUP, D_MODEL, D_EXPERT), lambda i, g, n, t: (g[i], 0, 0, 0),
                         pipeline_mode=pl.Buffered(1)),
            pl.BlockSpec((None, gk, D_MODEL), lambda i, g, n, t: (g[i], 0, 0), pipeline_mode=pl.Buffered(1)),
        ],
        out_specs=pl.BlockSpec(memory_space=pl.ANY),
        scratch_shapes=[
            pltpu.VMEM((2, tmx * TT_ROWS, LANES), F32),
            pltpu.VMEM((OUT_BUFS, tmx, D_MODEL), F32),
            pltpu.VMEM((tmx, gk), BF16),
            pltpu.SMEM((n_tiles * tmx,), jnp.int32),
            pltpu.SemaphoreType.DMA((2,)),
            pltpu.SemaphoreType.DMA((OUT_BUFS,)),
        ],
    )
    return pl.pallas_call(
        kern,
        grid_spec=grid_spec,
        out_shape=jax.ShapeDtypeStruct((n_tok, D_MODEL), F32),
        compiler_params=_params(("arbitrary",), vmem_limit),
        name="moe",
    )(grp, nval, pos, x2t, g2, wr, br, wg, wu, wd)


def _block_diag_ones(n, blk):
    i = jnp.arange(n) // blk
    return (i[:, None] == i[None, :]).astype(BF16)


def _layer(x, norm1_g, w_in, lam_re, lam_im, log_step, b_re, b_im, c_re, c_im, d_skip,
           w_glu, b_glu, g_q, g_kc, g_ks, g_kw, pos_k, pos_v, w_ck1, w_ck2, w_cv1, w_cv2,
           out_g_ssm, out_g_att, w_out, norm2_g, w_grp, b_grp, w_exp, b_exp, w_gate, w_up, w_down):
    bsz, seq, _ = x.shape
    assert seq % PROJ_TM == 0 and seq // SEL_BLOCK <= MAX_SEL_BLOCKS
    n_tok = bsz * seq
    x2d = x.reshape(n_tok, D_MODEL)
    q8 = SSM_Q
    n_sub = seq // q8
    nch = seq // CMP_STRIDE

    o_q = D_SSM
    o_kv = D_SSM + D_ATT
    o_gt = o_kv + 6 * D_KV
    kv = lambda i: w_in[:, o_kv + i * D_KV:o_kv + (i + 1) * D_KV]
    wrow = jnp.concatenate([w_in[:, :o_q], kv(0), kv(1), kv(2), kv(4)], axis=1).astype(BF16)
    per_g = HPG * N_BRANCH
    wgt = jnp.zeros((D_MODEL, N_KV * GATE_ROWS), F32)
    for g in range(N_KV):
        wgt = wgt.at[:, g * GATE_ROWS:g * GATE_ROWS + per_g].set(w_in[:, o_gt + g * per_g:o_gt + (g + 1) * per_g])
    wcol = jnp.concatenate([w_in[:, o_q:o_kv], kv(3), kv(5), wgt], axis=1).T.astype(BF16)
    qscale = (HEAD_DIM ** -0.5) * math.log2(math.e)
    gq = (jnp.tile(g_q.astype(F32), N_HEADS) * qscale).reshape(D_ATT, 1)
    gks = jnp.tile(g_ks.astype(F32), N_KV).reshape(1, D_KV)
    gkw = jnp.tile(g_kw.astype(F32), N_KV).reshape(1, D_KV)

    u, qt, kc, vc, ksa, kw, vst, vwt, gate = _in_proj(
        x2d, norm1_g.reshape(1, D_MODEL), wrow, wcol, gq, gks, gkw,
        _block_diag_ones(D_KV, HEAD_DIM), bsz=bsz, seq=seq)

    w_c, t_c, m_c, e_state_t, e_lane_t, pw_re, pw_im, lr_re, lr_im, dvec = _s5_weights(
        lam_re, lam_im, log_step, b_re, b_im, c_re, c_im, d_skip, n_sub)
    ys = _s5(u.reshape(bsz, seq, D_SSM), w_c, t_c, m_c, e_state_t, e_lane_t, pw_re, pw_im, lr_re, lr_im, dvec,
             bsz=bsz, n_sub=n_sub).reshape(n_tok, D_SSM)

    wide = CMP_STRIDE * HEAD_DIM
    pad8 = lambda p: jnp.zeros((SUBLANES, 2 * wide), F32).at[0].set(p.reshape(-1)).astype(BF16)
    kcmp, vcmpt = _compress(
        kc, vc,
        w_ck1.astype(BF16), w_ck2.astype(BF16), w_cv1.astype(BF16), w_cv2.T.astype(BF16),
        pad8(pos_k), pad8(pos_v), g_kc.astype(F32).reshape(1, HEAD_DIM), bsz=bsz, nch=nch)
    cstart = jnp.arange(nch) * CMP_STRIDE
    sstart = jnp.arange(MAX_SEL_BLOCKS) * SEL_BLOCK
    ovlt = ((cstart[None, :] < sstart[:, None] + SEL_BLOCK) & (cstart[None, :] + CMP_BLOCK > sstart[:, None])
            & (jnp.arange(MAX_SEL_BLOCKS)[:, None] < seq // SEL_BLOCK)
            & (jnp.arange(nch)[None, :] < nch - 1)).astype(BF16)
    yat = _nsa(qt, kcmp, vcmpt, ksa, kw, vst, vwt, gate, ovlt, bsz=bsz, seq=seq, nch=nch)

    wr = jnp.zeros((D_MODEL, LANES), F32)
    wr = wr.at[:, :N_EXP_GROUPS].set(w_grp).at[:, ROUTER_OFF:ROUTER_OFF + N_EXPERTS].set(w_exp).astype(BF16)
    br = jnp.zeros((1, LANES), F32)
    br = br.at[0, :N_EXP_GROUPS].set(b_grp).at[0, ROUTER_OFF:ROUTER_OFF + N_EXPERTS].set(b_exp)
    g2 = norm2_g.reshape(1, D_MODEL).astype(F32)
    x2t, gsel = _out_proj(
        ys, yat, x2d, w_glu.astype(BF16), b_glu.reshape(1, D_SSM).astype(F32),
        out_g_ssm.reshape(1, D_SSM).astype(F32), out_g_att.reshape(D_ATT, 1).astype(F32),
        w_out.astype(BF16), g2, wr, br, seq=seq)

    grp, nval, pos = _moe_plan(gsel[:, 0].astype(jnp.int32), n_tok)
    gshape = (N_EXP_GROUPS, EXPERTS_PER_GROUP, D_MODEL, D_EXPERT)
    out = _moe(x2t, grp, nval, pos, g2, wr, br, w_gate.reshape(gshape), w_up.reshape(gshape),
               w_down.reshape(N_EXP_GROUPS, EXPERTS_PER_GROUP * D_EXPERT, D_MODEL))
    return out.reshape(bsz, seq, D_MODEL)


def kernel(x, norm1_g, w_in, lam_re, lam_im, log_step, b_re, b_im, c_re, c_im, d_skip, w_glu, b_glu, g_q, g_kc, g_ks, g_kw, pos_k, pos_v, w_ck1, w_ck2, w_cv1, w_cv2, out_g_ssm, out_g_att, w_out, norm2_g, w_grp, b_grp, w_exp, b_exp, w_gate, w_up, w_down):
    depth = norm1_g.shape[0]
    for l in range(depth):
        x = _layer(x, norm1_g[l], w_in[l], lam_re[l], lam_im[l], log_step[l], b_re[l], b_im[l], c_re[l],
                   c_im[l], d_skip[l], w_glu[l], b_glu[l], g_q[l], g_kc[l], g_ks[l], g_kw[l], pos_k[l],
                   pos_v[l], w_ck1[l], w_ck2[l], w_cv1[l], w_cv2[l], out_g_ssm[l], out_g_att[l], w_out[l],
                   norm2_g[l], w_grp[l], b_grp[l], w_exp[l], b_exp[l], w_gate[l], w_up[l], w_down[l])
    return x
```

```python
import functools
import math

import jax
import jax.numpy as jnp
from jax import lax
from jax.experimental import pallas as pl
from jax.experimental.pallas import tpu as pltpu

D_MODEL = 1024
D_SSM = 512
SSM_CH = 16
SSM_STATE = 64
D_ATT = 512
HEAD_DIM = 64
N_HEADS = D_ATT // HEAD_DIM
N_KV = 2
HPG = N_HEADS // N_KV
D_KV = N_KV * HEAD_DIM
N_BRANCH = 3
CMP_STRIDE = 16
CMP_BLOCK = 2 * CMP_STRIDE
CMP_HIDDEN = 256
SEL_BLOCK = 64
N_SELECT = 16
WINDOW = 512
N_EXP_GROUPS = 4
EXPERTS_PER_GROUP = 8
N_EXPERTS = N_EXP_GROUPS * EXPERTS_PER_GROUP
D_EXPERT = 256
EPS = 1e-6
NEG = -1e30
FORCE = 1e9

LANES = 128
SUBLANES = 8
SSM_Q = 8
SSM_LT = D_SSM // LANES
ROUTER_OFF = N_EXP_GROUPS
NSA_TQ = 256
NSA_TK = 256
NSA_UNROLL = 8
V_ROWS = HEAD_DIM + 16
MAX_SEL_BLOCKS = 64
MOE_TM = 512
OUT_BUFS = 2
TT_ROWS = D_MODEL // LANES
PROJ_TM = 1024
GATE_ROWS = 16
V7X_VMEM_BYTES = 64 * 1024 * 1024
VMEM_LIMIT = V7X_VMEM_BYTES - 8 * 1024 * 1024

F32 = jnp.float32
BF16 = jnp.bfloat16


def _dot(a, b):
    return jnp.dot(a, b, preferred_element_type=F32)


def _dot_nt(a, b):
    return lax.dot_general(a, b, (((1,), (1,)), ((), ())), preferred_element_type=F32)


def _split_dot(x, w):
    hi = x.astype(BF16)
    lo = (x - hi.astype(F32)).astype(BF16)
    return _dot(hi, w) + _dot(lo, w)


def _gelu(x):
    c = math.sqrt(2.0 / math.pi)
    return 0.5 * x * (1.0 + jnp.tanh(c * (x + 0.044715 * (x * x * x))))


def _sigmoid(x):
    return 1.0 / (1.0 + jnp.exp(-x))


def _params(sem, vmem_limit=VMEM_LIMIT):
    return pltpu.CompilerParams(dimension_semantics=sem, vmem_limit_bytes=vmem_limit)


def _in_proj_kernel(x_ref, g1_ref, wrow_ref, wcol_ref, gq_ref, gks_ref, gkw_ref, bd128_ref,
                    u_ref, qt_ref, kc_ref, vc_ref, ksa_ref, kw_ref, vst_ref, vwt_ref, gate_ref, *, tm, nl):
    x = x_ref[...]
    ms = jnp.mean(x * x, axis=-1, keepdims=True)
    hn = (x * lax.rsqrt(ms + EPS) * g1_ref[...]).astype(BF16)

    pr = _dot(hn, wrow_ref[...])
    u_ref[...] = pr[:, :D_SSM]
    kc, vc, ks, kw = [pr[:, D_SSM + i * D_KV:D_SSM + (i + 1) * D_KV] for i in range(4)]
    kss = _split_dot(ks * ks, bd128_ref[...])
    ksn = ks * lax.rsqrt(kss * (1.0 / HEAD_DIM) + EPS) * gks_ref[...]
    kws = _split_dot(kw * kw, bd128_ref[...])
    kwn = kw * lax.rsqrt(kws * (1.0 / HEAD_DIM) + EPS) * gkw_ref[...]
    t0 = (pl.program_id(0) % nl) * tm
    tpos = t0 + lax.broadcasted_iota(jnp.int32, (tm, MAX_SEL_BLOCKS), 0)
    blk = lax.broadcasted_iota(jnp.int32, (tm, MAX_SEL_BLOCKS), 1)
    onehot = jnp.where(tpos // SEL_BLOCK == blk, 1.0, 0.0).astype(BF16)
    for g in range(N_KV):
        sl = slice(g * HEAD_DIM, (g + 1) * HEAD_DIM)
        kc_ref[g] = kc[:, sl]
        vc_ref[g] = vc[:, sl]
        ksa_ref[g] = jnp.concatenate([ksn[:, sl].astype(BF16), onehot], axis=1)
        kw_ref[g] = kwn[:, sl].astype(BF16)

    pc = _dot_nt(wcol_ref[...], hn)
    gq = gq_ref[...]
    for h in range(N_HEADS):
        sl = slice(h * HEAD_DIM, (h + 1) * HEAD_DIM)
        qh = pc[sl]
        ss = jnp.sum(qh * qh, axis=0, keepdims=True)
        qt_ref[h] = (qh * lax.rsqrt(ss * (1.0 / HEAD_DIM) + EPS) * gq[sl]).astype(BF16)
    ones_rows = jnp.where(lax.broadcasted_iota(jnp.int32, (V_ROWS - HEAD_DIM, tm), 0) == 0, 1.0, 0.0)
    for g in range(N_KV):
        for o_ref, base in ((vst_ref, D_ATT), (vwt_ref, D_ATT + D_KV)):
            vt = jnp.concatenate([pc[base + g * HEAD_DIM:base + (g + 1) * HEAD_DIM], ones_rows], axis=0)
            vt = vt.astype(BF16)
            for j in range(tm // NSA_TK):
                o_ref[g, j] = vt[:, j * NSA_TK:(j + 1) * NSA_TK]
        gb = D_ATT + 2 * D_KV + g * GATE_ROWS
        gate_ref[g] = _sigmoid(pc[gb:gb + GATE_ROWS])


def _in_proj(x2d, g1, wrow, wcol, gq, gks, gkw, bd128, *, bsz, seq):
    tm = PROJ_TM
    nl = seq // tm
    n_tok = bsz * seq
    kern = functools.partial(_in_proj_kernel, tm=tm, nl=nl)
    row = lambda i: (i, 0)
    const = lambda i: (0, 0)
    bgl = lambda i: (i // nl, 0, i % nl, 0)
    n_col = wcol.shape[0]
    jt = tm // NSA_TK

    def kvspec(width):
        return pl.BlockSpec((None, N_KV, tm, width), bgl)

    def kvshape(width, dtype=BF16):
        return jax.ShapeDtypeStruct((bsz, N_KV, seq, width), dtype)

    vt_spec = pl.BlockSpec((None, N_KV, jt, V_ROWS, NSA_TK), lambda i: (i // nl, 0, i % nl, 0, 0))
    vt_shape = jax.ShapeDtypeStruct((bsz, N_KV, seq // NSA_TK, V_ROWS, NSA_TK), BF16)
    return pl.pallas_call(
        kern,
        grid=(n_tok // tm,),
        in_specs=[
            pl.BlockSpec((tm, D_MODEL), row),
            pl.BlockSpec((1, D_MODEL), const),
            pl.BlockSpec((D_MODEL, D_SSM + 4 * D_KV), const),
            pl.BlockSpec((n_col, D_MODEL), const),
            pl.BlockSpec((D_ATT, 1), const),
            pl.BlockSpec((1, D_KV), const),
            pl.BlockSpec((1, D_KV), const),
            pl.BlockSpec((D_KV, D_KV), const),
        ],
        out_specs=[
            pl.BlockSpec((tm, D_SSM), row),
            pl.BlockSpec((None, N_HEADS, HEAD_DIM, tm), lambda i: (i // nl, 0, 0, i % nl)),
            kvspec(HEAD_DIM), kvspec(HEAD_DIM), kvspec(2 * HEAD_DIM), kvspec(HEAD_DIM),
            vt_spec, vt_spec,
            pl.BlockSpec((None, N_KV, GATE_ROWS, tm), lambda i: (i // nl, 0, 0, i % nl)),
        ],
        out_shape=[
            jax.ShapeDtypeStruct((n_tok, D_SSM), F32),
            jax.ShapeDtypeStruct((bsz, N_HEADS, HEAD_DIM, seq), BF16),
            kvshape(HEAD_DIM, F32), kvshape(HEAD_DIM, F32), kvshape(2 * HEAD_DIM), kvshape(HEAD_DIM),
            vt_shape, vt_shape,
            jax.ShapeDtypeStruct((bsz, N_KV, GATE_ROWS, seq), F32),
        ],
        compiler_params=_params(("parallel",)),
        name="in_proj",
    )(x2d, g1, wrow, wcol, gq, gks, gkw, bd128)


def _s5_weights(lam_re, lam_im, log_step, b_re, b_im, c_re, c_im, d_skip, n_sub):
    q = SSM_Q
    lam = lax.complex(lam_re.astype(F32), lam_im.astype(F32))
    step = jnp.exp(log_step.astype(F32))[:, None]
    lam_bar = jnp.exp(lam * step)
    b_bar = ((lam_bar - 1.0) / lam)[..., None] * lax.complex(b_re.astype(F32), b_im.astype(F32))
    c = lax.complex(c_re.astype(F32), c_im.astype(F32))
    pows = [jnp.ones_like(lam_bar)]
    for _ in range(q):
        pows.append(pows[-1] * lam_bar)
    pw = jnp.stack(pows)
    lt, a8 = SSM_LT, LANES // SSM_CH
    hp = a8 * SSM_STATE
    e_lane = (jnp.arange(a8)[:, None] == jnp.arange(LANES)[None, :] // SSM_CH).astype(F32)
    e_state = (jnp.arange(a8)[:, None] == jnp.arange(hp)[None, :] // SSM_STATE).astype(F32)
    e_lane_t = jnp.tile(e_lane, (1, q))
    e_state_t = jnp.tile(e_state, (1, 2))

    kk = jnp.real(jnp.einsum('ghp,kgp,gpi->kghi', c, pw[:q], b_bar))
    km = kk.reshape(q, lt, a8, SSM_CH, SSM_CH).transpose(1, 4, 0, 2, 3).reshape(lt, SSM_CH, q, LANES)
    lag = jnp.arange(q)[None, :] - jnp.arange(q)[:, None]
    kg = km[:, :, jnp.clip(lag, 0, q - 1), :] * (lag >= 0)[None, None, :, :, None].astype(F32)
    kc = kg.transpose(0, 2, 1, 3, 4).reshape(lt, q, 1, SSM_CH, q * LANES)
    t_c = kc.reshape(lt, q * SSM_CH, q * LANES)

    wc = pw[q - 1 - jnp.arange(q)][..., None] * b_bar[None]
    wri = jnp.stack([jnp.real(wc), jnp.imag(wc)])
    wm = (wri.reshape(2, q, lt, a8, SSM_STATE, SSM_CH).transpose(2, 1, 5, 0, 3, 4)
          .reshape(lt, q, 1, SSM_CH, 2 * hp))
    w_c = wm.reshape(lt, q * SSM_CH, 2 * hp)

    cl = c[None] * pw[1:q + 1][:, :, None, :]
    cri = jnp.stack([jnp.real(cl), -jnp.imag(cl)])
    mm = (cri.reshape(2, q, lt, a8, SSM_CH, SSM_STATE).transpose(2, 0, 5, 1, 3, 4)
          .reshape(lt, 2, 1, SSM_STATE, q * LANES))
    m_c = mm.reshape(lt, 2 * SSM_STATE, q * LANES)

    n_lvl = max(1, (n_sub - 1).bit_length())
    lv = [pw[q]]
    for _ in range(n_lvl - 1):
        lv.append(lv[-1] * lv[-1])
    lvs = jnp.stack(lv).reshape(n_lvl, lt, 1, hp)
    pw_re = jnp.real(lvs).transpose(1, 0, 2, 3)
    pw_im = jnp.imag(lvs).transpose(1, 0, 2, 3)
    dvec = jnp.tile(d_skip.astype(F32).reshape(lt, 1, LANES), (1, 1, q))
    lr = [pw[q]]
    for _ in range(SUBLANES - 1):
        lr.append(lr[-1] * pw[q])
    lrs = jnp.stack(lr).reshape(SUBLANES, lt, 1, hp)
    lr_re = jnp.real(lrs).transpose(1, 0, 2, 3)
    lr_im = jnp.imag(lrs).transpose(1, 0, 2, 3)
    return w_c, t_c, m_c, e_state_t, e_lane_t, pw_re, pw_im, lr_re, lr_im, dvec


def _s5_kernel(u_ref, wc_ref, tc_ref, mc_ref, es_ref, el_ref, pwr_ref, pwi_ref, lrr_ref, lri_ref, d_ref, y_ref,
               w_ref, t_ref, m_ref, xr_ref, xi_ref,
               *, n_sub, n_lvl):
    half = (LANES // SSM_CH) * SSM_STATE
    q = SSM_Q
    a8 = LANES // SSM_CH

    @pl.when(pl.program_id(1) == 0)
    def _():
        for a in range(a8):
            el = el_ref[a:a + 1, :]
            es = es_ref[a:a + 1, :]
            for s_ in range(q):
                rows = pl.ds((s_ * a8 + a) * SSM_CH, SSM_CH)
                src = pl.ds(s_ * SSM_CH, SSM_CH)
                t_ref[rows, :] = (tc_ref[src, :] * el).astype(BF16)
                w_ref[rows, :] = (wc_ref[src, :] * es).astype(BF16)
            for r in range(2):
                rows = pl.ds((r * a8 + a) * SSM_STATE, SSM_STATE)
                m_ref[rows, :] = (mc_ref[pl.ds(r * SSM_STATE, SSM_STATE), :] * el).astype(BF16)

    u = jnp.concatenate([u_ref[pl.ds(s, n_sub, stride=q), :] for s in range(q)], axis=1)
    ub = u.astype(BF16)
    s_loc = _dot(ub, w_ref[...])
    re = s_loc[:, :half]
    im = s_loc[:, half:]
    rowi = lax.broadcasted_iota(jnp.int32, (n_sub, half), 0)

    def scan_levels(re, im, first_lvl, n, rowpos, axis):
        for k in range(n):
            d = 1 << k
            ar = pwr_ref[first_lvl + k]
            ai = pwi_ref[first_lvl + k]
            keep = rowpos >= d
            sre = jnp.where(keep, pltpu.roll(re, d, axis=axis), 0.0)
            sim = jnp.where(keep, pltpu.roll(im, d, axis=axis), 0.0)
            re, im = re + (ar * sre - ai * sim), im + (ar * sim + ai * sre)
        return re, im

    blk = SUBLANES
    lb = blk.bit_length() - 1
    nblk = n_sub // blk
    rpos = lax.broadcasted_iota(jnp.int32, (nblk, blk, half), 1)
    re3, im3 = scan_levels(re.reshape(nblk, blk, half), im.reshape(nblk, blk, half), 0, lb, rpos, 1)
    re = re3.reshape(n_sub, half)
    im = im3.reshape(n_sub, half)
    lanes = [slice(c * LANES, (c + 1) * LANES) for c in range(half // LANES)]
    for c, cs in enumerate(lanes):
        xr_ref[c] = re[:, cs]
        xi_ref[c] = im[:, cs]
    ends = pl.ds(blk - 1, nblk, stride=blk)
    er = jnp.concatenate([xr_ref[c, ends, :] for c in range(len(lanes))], axis=1)
    ei = jnp.concatenate([xi_ref[c, ends, :] for c in range(len(lanes))], axis=1)
    browi = lax.broadcasted_iota(jnp.int32, (nblk, half), 0)
    er, ei = scan_levels(er, ei, lb, n_lvl - lb, browi, 0)
    cr = jnp.where(browi >= 1, pltpu.roll(er, 1, axis=0), 0.0)
    ci = jnp.where(browi >= 1, pltpu.roll(ei, 1, axis=0), 0.0)
    for r in range(blk):
        ar = lrr_ref[r]
        ai = lri_ref[r]
        dr = ar * cr - ai * ci
        di = ar * ci + ai * cr
        rows_r = pl.ds(r, nblk, stride=blk)
        for c, cs in enumerate(lanes):
            xr_ref[c, rows_r, :] = xr_ref[c, rows_r, :] + dr[:, cs]
            xi_ref[c, rows_r, :] = xi_ref[c, rows_r, :] + di[:, cs]
    re = jnp.concatenate([xr_ref[c] for c in range(len(lanes))], axis=1)
    im = jnp.concatenate([xi_ref[c] for c in range(len(lanes))], axis=1)
    keep = rowi >= 1
    xre = jnp.where(keep, pltpu.roll(re, 1, axis=0), 0.0)
    xim = jnp.where(keep, pltpu.roll(im, 1, axis=0), 0.0)
    xst = jnp.concatenate([xre, xim], axis=1).astype(BF16)
    tb = 2 * LANES
    ycols = []
    for jb in range(q * LANES // tb):
        acc = _dot(ub[:, :tb], t_ref[0:tb, jb * tb:(jb + 1) * tb])
        for sb in range(1, jb + 1):
            acc = acc + _dot(ub[:, sb * tb:(sb + 1) * tb], t_ref[sb * tb:(sb + 1) * tb, jb * tb:(jb + 1) * tb])
        ycols.append(acc)
    y = jnp.concatenate(ycols, axis=1) + _dot(xst, m_ref[...]) + d_ref[...] * u
    for j in range(q):
        y_ref[pl.ds(j, n_sub, stride=q), :] = y[:, j * LANES:(j + 1) * LANES]


def _s5(u, w_c, t_c, m_c, e_state_t, e_lane_t, pw_re, pw_im, lr_re, lr_im, dvec, *, bsz, n_sub):
    q = SSM_Q
    n_lvl = pw_re.shape[1]
    kern = functools.partial(_s5_kernel, n_sub=n_sub, n_lvl=n_lvl)
    wide = q * LANES
    seq = n_sub * q
    return pl.pallas_call(
        kern,
        grid=(SSM_LT, bsz),
        in_specs=[
            pl.BlockSpec((None, seq, LANES), lambda l, b: (b, 0, l)),
            pl.BlockSpec((None, q * SSM_CH, wide), lambda l, b: (l, 0, 0)),
            pl.BlockSpec((None, q * SSM_CH, wide), lambda l, b: (l, 0, 0)),
            pl.BlockSpec((None, 2 * SSM_STATE, wide), lambda l, b: (l, 0, 0)),
            pl.BlockSpec((LANES // SSM_CH, wide), lambda l, b: (0, 0)),
            pl.BlockSpec((LANES // SSM_CH, wide), lambda l, b: (0, 0)),
            pl.BlockSpec((None, n_lvl, 1, wide // 2), lambda l, b: (l, 0, 0, 0)),
            pl.BlockSpec((None, n_lvl, 1, wide // 2), lambda l, b: (l, 0, 0, 0)),
            pl.BlockSpec((None, SUBLANES, 1, wide // 2), lambda l, b: (l, 0, 0, 0)),
            pl.BlockSpec((None, SUBLANES, 1, wide // 2), lambda l, b: (l, 0, 0, 0)),
            pl.BlockSpec((None, 1, wide), lambda l, b: (l, 0, 0)),
        ],
        out_specs=pl.BlockSpec((None, seq, LANES), lambda l, b: (b, 0, l)),
        out_shape=jax.ShapeDtypeStruct((bsz, seq, D_SSM), F32),
        scratch_shapes=[pltpu.VMEM((wide, wide), BF16)] * 3
        + [pltpu.VMEM((wide // 2 // LANES, n_sub, LANES), F32)] * 2,
        compiler_params=_params(("arbitrary", "arbitrary")),
        name="s5",
    )(u, w_c, t_c, m_c, e_state_t, e_lane_t, pw_re, pw_im, lr_re, lr_im, dvec)


def _compress_kernel(kc_ref, vc_ref, w1k_ref, w2k_ref, w1v_ref, w2vt_ref, posk_ref, posv_ref, gkc_ref,
                     kcmp_ref, vcmpt_ref, *, nch):
    half = CMP_STRIDE * HEAD_DIM

    def hidden(x_ref, w1_ref, pos_ref):
        x = jnp.concatenate([x_ref[pl.ds(j, nch, stride=CMP_STRIDE), :] for j in range(CMP_STRIDE)],
                            axis=1).astype(BF16)
        a = _dot(x, w1_ref[:half, :])
        b = _dot(x, w1_ref[half:, :])
        pv = _dot(pos_ref[...], w1_ref[...])[0:1, :]
        hid = a + pltpu.roll(b, nch - 1, axis=0) + pv
        return _gelu(hid).astype(BF16)

    k = _dot(hidden(kc_ref, w1k_ref, posk_ref), w2k_ref[...])
    ms = jnp.mean(k * k, axis=-1, keepdims=True)
    kcmp_ref[...] = (k * lax.rsqrt(ms + EPS) * gkc_ref[...]).astype(BF16)
    vt = _dot_nt(w2vt_ref[...], hidden(vc_ref, w1v_ref, posv_ref))
    coli = lax.broadcasted_iota(jnp.int32, vt.shape, 1)
    vcmpt_ref[...] = jnp.where(coli < nch - 1, vt, 0.0).astype(BF16)


def _compress(kcf, vcf, w1k, w2k, w1v, w2vt, posk, posv, gkc, *, bsz, nch):
    kern = functools.partial(_compress_kernel, nch=nch)
    wide = CMP_STRIDE * HEAD_DIM
    xspec = pl.BlockSpec((None, None, nch * CMP_STRIDE, HEAD_DIM), lambda b, g: (b, g, 0, 0))
    c2 = lambda b, g: (0, 0)
    return pl.pallas_call(
        kern,
        grid=(bsz, N_KV),
        in_specs=[
            xspec, xspec,
            pl.BlockSpec((2 * wide, CMP_HIDDEN), c2), pl.BlockSpec((CMP_HIDDEN, HEAD_DIM), c2),
            pl.BlockSpec((2 * wide, CMP_HIDDEN), c2), pl.BlockSpec((HEAD_DIM, CMP_HIDDEN), c2),
            pl.BlockSpec((SUBLANES, 2 * wide), c2), pl.BlockSpec((SUBLANES, 2 * wide), c2),
            pl.BlockSpec((1, HEAD_DIM), c2),
        ],
        out_specs=[pl.BlockSpec((None, None, nch, HEAD_DIM), lambda b, g: (b, g, 0, 0)),
                   pl.BlockSpec((None, None, HEAD_DIM, nch), lambda b, g: (b, g, 0, 0))],
        out_shape=[jax.ShapeDtypeStruct((bsz, N_KV, nch, HEAD_DIM), BF16),
                   jax.ShapeDtypeStruct((bsz, N_KV, HEAD_DIM, nch), BF16)],
        compiler_params=_params(("parallel", "parallel")),
        name="compress",
    )(kcf, vcf, w1k, w2k, w1v, w2vt, posk, posv, gkc)


def _nsa_kernel(qt_ref, kcmp_ref, vcmpt_ref, ksa_ref, kw_ref, vst_ref, vwt_ref, gate_ref, ovlt_ref, o_ref,
                accs_ref, accw_ref, sel_ref, *, tq, tk, nch):
    qi = pl.program_id(2)
    q0 = qi * tq
    rows = HPG * tq
    qt = jnp.concatenate([qt_ref[h] for h in range(HPG)], axis=1)
    tpos = q0 + lax.broadcasted_iota(jnp.int32, (1, rows), 1) % tq
    krow = lax.broadcasted_iota(jnp.int32, (tk, rows), 0)
    m0 = jnp.full((1, rows), NEG, F32)

    def run_tiles(k_ref, q_all, vt_ref, acc_ref, m, tiles):
        scores = []
        for kt, kind in tiles:
            kc = jnp.maximum(kt, 0) if kind in ('band', 'valid') else kt
            s = _dot(k_ref[pl.ds(pl.multiple_of(kc * tk, tk), tk), :], q_all)
            if kind == 'causal':
                s = jnp.where(kt * tk + krow <= tpos, s, NEG)
            elif kind == 'band':
                s = jnp.where((kt * tk + krow > tpos - WINDOW) & (kt >= 0), s, NEG)
            elif kind == 'valid':
                s = jnp.where(kt >= 0, s, NEG)
            scores.append((kc, s))
        for kc, s in scores:
            m_new = jnp.maximum(m, jnp.max(s, axis=0, keepdims=True))
            alpha = jnp.exp2(m - m_new)
            p = jnp.exp2(s - m_new).astype(BF16)
            acc_ref[...] = alpha * acc_ref[...] + _dot(vt_ref[kc], p)
            m = m_new
        return m

    accw_ref[...] = jnp.zeros_like(accw_ref)
    n_win = WINDOW // tk
    run_tiles(kw_ref, qt, vwt_ref, accw_ref, m0,
              [(qi - n_win, 'band')] + [(qi - n_win + t, 'valid') for t in range(1, n_win)] + [(qi, 'causal')])

    s = _dot(kcmp_ref[...], qt)
    cend = lax.broadcasted_iota(jnp.int32, (nch, rows), 0) * CMP_STRIDE + (CMP_BLOCK - 1)
    s = jnp.where(cend <= tpos, s, NEG)
    m = jnp.max(s, axis=0, keepdims=True)
    p = jnp.exp2(s - m)
    p = p * jnp.where(tpos >= CMP_BLOCK - 1, 1.0 / jnp.sum(p, axis=0, keepdims=True), 0.0)
    o_cmp = _dot(vcmpt_ref[...], p.astype(BF16))

    psum = p[:, 0:tq]
    for h in range(1, HPG):
        psum = psum + p[:, h * tq:(h + 1) * tq]
    hi = psum.astype(BF16)
    lo = (psum - hi.astype(F32)).astype(BF16)
    ovlt = ovlt_ref[...]
    imp = _dot(ovlt, hi) + _dot(ovlt, lo)
    nb = MAX_SEL_BLOCKS
    blk = lax.broadcasted_iota(jnp.int32, (nb, tq), 0)
    cur = (q0 + lax.broadcasted_iota(jnp.int32, (nb, tq), 1)) // SEL_BLOCK
    forced = (blk == 0) | (blk == cur) | (blk == cur - 1)
    imp = jnp.where(forced, FORCE, jnp.where(blk <= cur, imp, NEG))
    sub = SUBLANES
    rowl = lax.broadcasted_iota(jnp.int32, (sub, tq), 0)
    n_seen = (q0 + tq - 1) // SEL_BLOCK + 1
    sel_ref[...] = jnp.zeros_like(sel_ref)
    for nbv in range(2 * N_SELECT, nb + 1, N_SELECT):
        @pl.when((n_seen > nbv - N_SELECT) & (n_seen <= nbv))
        def _(nbv=nbv):
            groups = [imp[r:r + sub] for r in range(0, nbv, sub)]
            ranks = [jnp.zeros((sub, tq), F32) for _ in groups]
            for i in range(nbv):
                ri = jnp.broadcast_to(imp[i:i + 1, :], (sub, tq))
                for gi, x in enumerate(groups):
                    if i < gi * sub:
                        ahead = ri >= x
                    elif i >= (gi + 1) * sub:
                        ahead = ri > x
                    else:
                        ahead = (ri > x) | ((ri == x) & (rowl > i - gi * sub))
                    ranks[gi] = ranks[gi] + jnp.where(ahead, 1.0, 0.0)
            rank = jnp.concatenate(ranks, axis=0)
            sel_ref[0:nbv, :] = jnp.where(rank < N_SELECT, 0.0, NEG).astype(BF16)
    sel = sel_ref[...]
    qa = jnp.concatenate([qt, jnp.concatenate([sel] * HPG, axis=1)], axis=0)

    accs_ref[...] = jnp.zeros_like(accs_ref)
    nu = NSA_UNROLL

    def sel_multi(j, m):
        return run_tiles(ksa_ref, qa, vst_ref, accs_ref, m, [(nu * j + t, None) for t in range(nu)])

    m_sel = lax.fori_loop(0, qi // nu, sel_multi, m0)
    for r in range(nu):
        @pl.when(qi % nu == r)
        def _(r=r):
            base = qi - r
            run_tiles(ksa_ref, qa, vst_ref, accs_ref, m_sel,
                      [(base + t, None) for t in range(r)] + [(qi, 'causal')])

    accs = accs_ref[...]
    accw = accw_ref[...]
    o_sel = accs[:HEAD_DIM] * (1.0 / accs[HEAD_DIM:HEAD_DIM + 1])
    o_win = accw[:HEAD_DIM] * (1.0 / accw[HEAD_DIM:HEAD_DIM + 1])
    gt = gate_ref[...]
    for h in range(HPG):
        c = slice(h * tq, (h + 1) * tq)
        o_ref[h * HEAD_DIM:(h + 1) * HEAD_DIM, :] = (
            gt[3 * h:3 * h + 1] * o_cmp[:, c] + gt[3 * h + 1:3 * h + 2] * o_sel[:, c]
            + gt[3 * h + 2:3 * h + 3] * o_win[:, c])


def _nsa(qt, kcmp, vcmpt, ksa, kw, vst, vwt, gate, ovlt, *, bsz, seq, nch):
    tq, tk = NSA_TQ, NSA_TK
    assert tq == tk and WINDOW % tk == 0
    nq = seq // tq
    kern = functools.partial(_nsa_kernel, tq=tq, tk=tk, nch=nch)
    full = lambda b, g, i: (b, g, 0, 0)
    full5 = lambda b, g, i: (b, g, 0, 0, 0)
    qd = HPG * HEAD_DIM
    rows = HPG * tq
    return pl.pallas_call(
        kern,
        grid=(bsz, N_KV, nq),
        in_specs=[
            pl.BlockSpec((None, HPG, HEAD_DIM, tq), lambda b, g, i: (b, g, 0, i)),
            pl.BlockSpec((None, None, nch, HEAD_DIM), full),
            pl.BlockSpec((None, None, HEAD_DIM, nch), full),
            pl.BlockSpec((None, None, seq, 2 * HEAD_DIM), full),
            pl.BlockSpec((None, None, seq, HEAD_DIM), full),
            pl.BlockSpec((None, None, seq // tk, V_ROWS, tk), full5),
            pl.BlockSpec((None, None, seq // tk, V_ROWS, tk), full5),
            pl.BlockSpec((None, None, GATE_ROWS, tq), lambda b, g, i: (b, g, 0, i)),
            pl.BlockSpec((MAX_SEL_BLOCKS, nch), lambda b, g, i: (0, 0)),
        ],
        out_specs=pl.BlockSpec((None, qd, tq), lambda b, g, i: (b, g, i)),
        out_shape=jax.ShapeDtypeStruct((bsz, D_ATT, seq), F32),
        scratch_shapes=[pltpu.VMEM((V_ROWS, rows), F32), pltpu.VMEM((V_ROWS, rows), F32),
                        pltpu.VMEM((MAX_SEL_BLOCKS, tq), BF16)],
        compiler_params=_params(("parallel", "parallel", "arbitrary")),
        name="nsa",
    )(qt, kcmp, vcmpt, ksa, kw, vst, vwt, gate, ovlt)


def _route(logits, group=None):
    lane = lax.broadcasted_iota(jnp.int32, logits.shape, 1).astype(F32)
    far = float(LANES)
    is_g = lane < N_EXP_GROUPS
    glog = jnp.where(is_g, logits, -jnp.inf)
    gmax = jnp.max(glog, axis=1, keepdims=True)
    gsum = jnp.sum(jnp.where(is_g, jnp.exp(logits - gmax), 0.0), axis=1, keepdims=True)
    gsel = jnp.min(jnp.where(glog == gmax, lane, far), axis=1, keepdims=True)
    gprob = 1.0 / gsum
    lo = ROUTER_OFF + EXPERTS_PER_GROUP * (gsel if group is None else group)
    in_e = (lane >= lo) & (lane < lo + EXPERTS_PER_GROUP)
    emax = jnp.max(jnp.where(in_e, logits, -jnp.inf), axis=1, keepdims=True)
    eexp = jnp.where(in_e, jnp.exp(logits - emax), 0.0)
    eprob = jnp.where(in_e, eexp / jnp.sum(eexp, axis=1, keepdims=True), -1.0)
    v1 = jnp.max(eprob, axis=1, keepdims=True)
    i1 = jnp.min(jnp.where(eprob == v1, lane, far), axis=1, keepdims=True)
    rest = jnp.where(lane == i1, -1.0, eprob)
    v2 = jnp.max(rest, axis=1, keepdims=True)
    i2 = jnp.min(jnp.where(rest == v2, lane, far), axis=1, keepdims=True)
    den = v1 + v2
    comb = jnp.where(lane == i1, v1 / den * gprob, 0.0) + jnp.where(lane == i2, v2 / den * gprob, 0.0)
    return gsel, comb


def _out_proj_kernel(tm, ys_ref, yat_ref, x_ref, wglu_ref, bglu_ref, gs_ref, ga_ref, wo_ref, g2_ref, wr_ref, br_ref,
                     xt_ref, gsel_ref):
    y = _gelu(ys_ref[...])
    y = y * _sigmoid(_dot(y.astype(BF16), wglu_ref[...]) + bglu_ref[...])
    ysn = y * lax.rsqrt(jnp.mean(y * y, axis=-1, keepdims=True) + EPS) * gs_ref[...]
    yat = yat_ref[...]
    yant = yat * lax.rsqrt(jnp.mean(yat * yat, axis=0, keepdims=True) + EPS) * ga_ref[...]
    yan = yant.T
    x2 = x_ref[...] + _dot(ysn.astype(BF16), wo_ref[:D_SSM, :]) + _dot(yan.astype(BF16), wo_ref[D_SSM:, :])
    for c in range(TT_ROWS):
        xt_ref[pl.ds(c, tm, stride=TT_ROWS), :] = x2[:, c * LANES:(c + 1) * LANES]
    h2 = (x2 * lax.rsqrt(jnp.mean(x2 * x2, axis=-1, keepdims=True) + EPS) * g2_ref[...]).astype(BF16)
    gsel, _ = _route(_dot(h2, wr_ref[...]) + br_ref[...])
    gsel_ref[...] = jnp.broadcast_to(gsel, gsel_ref.shape)


def _out_proj(ys, yat, x2d, wglu, bglu, gs, ga, wo, g2, wr, br, *, seq):
    n_tok = x2d.shape[0]
    tm = PROJ_TM
    nl = seq // tm
    row = lambda i: (i, 0)
    const = lambda i: (0, 0)
    return pl.pallas_call(
        functools.partial(_out_proj_kernel, tm),
        grid=(n_tok // tm,),
        in_specs=[
            pl.BlockSpec((tm, D_SSM), row),
            pl.BlockSpec((None, D_ATT, tm), lambda i: (i // nl, 0, i % nl)),
            pl.BlockSpec((tm, D_MODEL), row),
            pl.BlockSpec((D_SSM, D_SSM), const),
            pl.BlockSpec((1, D_SSM), const),
            pl.BlockSpec((1, D_SSM), const),
            pl.BlockSpec((D_ATT, 1), const),
            pl.BlockSpec((D_SSM + D_ATT, D_MODEL), const),
            pl.BlockSpec((1, D_MODEL), const),
            pl.BlockSpec((D_MODEL, LANES), const),
            pl.BlockSpec((1, LANES), const),
        ],
        out_specs=[pl.BlockSpec((tm * TT_ROWS, LANES), row), pl.BlockSpec((tm, LANES), row)],
        out_shape=[jax.ShapeDtypeStruct((n_tok * TT_ROWS, LANES), F32), jax.ShapeDtypeStruct((n_tok, LANES), F32)],
        compiler_params=_params(("parallel",)),
        name="out_proj",
    )(ys, yat, x2d, wglu, bglu, gs, ga, wo, g2, wr, br)


def _moe_plan(gsel, n_tok):
    tmx = MOE_TM
    n_tiles = n_tok // tmx + N_EXP_GROUPS
    oh = (gsel[:, None] == jnp.arange(N_EXP_GROUPS)[None, :]).astype(jnp.int32)
    csum = jnp.cumsum(oh, axis=0)
    counts = csum[-1]
    rank = jnp.sum(csum * oh, axis=1) - 1
    nt = (counts + tmx - 1) // tmx
    tend = jnp.cumsum(nt)
    toff = tend - nt
    pos = jnp.sum(oh * toff[None, :], axis=1) * tmx + rank
    tile = jnp.arange(n_tiles)
    grp = jnp.minimum(jnp.sum((tile[:, None] >= tend[None, :]).astype(jnp.int32), axis=1), N_EXP_GROUPS - 1)
    nval = jnp.clip(counts[grp] - (tile - toff[grp]) * tmx, 0, tmx)
    return grp.astype(jnp.int32), nval.astype(jnp.int32), pos.astype(jnp.int32)


def _moe_kernel(grp_ref, nval_ref, pos_ref, x_hbm, g2_ref, wr_ref, br_ref, wg_ref, wu_ref, wd_ref, o_hbm,
                xbuf, obuf, abuf, tok_ref, gsem, ssem, *, tmx, n_tiles, n_tok):
    i = pl.program_id(0)
    slot = i % 2

    @pl.when(i == 0)
    def _():
        def place(t, c):
            tok_ref[pos_ref[t]] = t
            return c

        lax.fori_loop(0, n_tok, place, 0, unroll=8)

        def pad_tile(tile, c):
            def pad_row(r, c2):
                tok_ref[tile * tmx + r] = 0
                return c2

            return lax.fori_loop(nval_ref[tile], tmx, pad_row, c)

        lax.fori_loop(0, n_tiles, pad_tile, 0)

    def gather_row(tile, r, dst_slot):
        t = tok_ref[tile * tmx + r]
        return pltpu.make_async_copy(x_hbm.at[pl.ds(pl.multiple_of(t * TT_ROWS, TT_ROWS), TT_ROWS), :],
                                     xbuf.at[dst_slot, pl.ds(r * TT_ROWS, TT_ROWS), :], gsem.at[dst_slot])

    def scatter_row(tile, r, src_slot):
        t = tok_ref[tile * tmx + r]
        return pltpu.make_async_copy(obuf.at[src_slot, pl.ds(r, 1), :], o_hbm.at[pl.ds(t, 1), :], ssem.at[src_slot])

    def gather_wait(src_slot):
        pltpu.make_async_copy(x_hbm.at[pl.ds(0, tmx * TT_ROWS), :], xbuf.at[src_slot], gsem.at[src_slot]).wait()

    def scatter_wait(n, src_slot):
        n8 = pl.multiple_of((n // SUBLANES) * SUBLANES, SUBLANES)

        @pl.when(n8 > 0)
        def _():
            pltpu.make_async_copy(obuf.at[src_slot, pl.ds(0, n8), :], o_hbm.at[pl.ds(0, n8), :],
                                  ssem.at[src_slot]).wait()

        def one(r, c):
            pltpu.make_async_copy(obuf.at[src_slot, pl.ds(0, 1), :], o_hbm.at[pl.ds(0, 1), :],
                                  ssem.at[src_slot]).wait()
            return c

        lax.fori_loop(0, n - n8, one, 0)

    @pl.when(i == 0)
    def _():
        def body(r, c):
            gather_row(0, r, 0).start()
            return c

        lax.fori_loop(0, tmx, body, 0)

    nv = nval_ref[i]
    nv_prev = nval_ref[jnp.maximum(i - 1, 0)]

    @pl.when((i == 0) | (nv_prev > 0))
    def _():
        gather_wait(slot)

    oslot = i % OUT_BUFS
    oprev = (i + OUT_BUFS - 1) % OUT_BUFS

    @pl.when(i >= OUT_BUFS)
    def _():
        scatter_wait(nval_ref[i - OUT_BUFS], oslot)

    nxt = jnp.minimum(i + 1, n_tiles - 1)
    per = tmx // EXPERTS_PER_GROUP
    prev_full = (i >= 1) & (nv_prev == tmx)

    def experts(with_scatter):
        x2 = jnp.concatenate([xbuf[slot, pl.ds(c, tmx, stride=TT_ROWS), :] for c in range(TT_ROWS)], axis=1)
        h = (x2 * lax.rsqrt(jnp.mean(x2 * x2, axis=-1, keepdims=True) + EPS) * g2_ref[...]).astype(BF16)
        _, cw = _route(_dot(h, wr_ref[...]) + br_ref[...], group=grp_ref[i].astype(F32))
        lane = lax.broadcasted_iota(jnp.int32, cw.shape, 1)
        first = ROUTER_OFF + EXPERTS_PER_GROUP * grp_ref[i]
        half = EXPERTS_PER_GROUP // 2
        for k in range(EXPERTS_PER_GROUP):
            if k < half:
                for r in range(k * 2 * per, (k + 1) * 2 * per):
                    gather_row(nxt, r, 1 - slot).start(priority=r % 2)
            elif with_scatter:
                for r in range((k - half) * 2 * per, (k - half + 1) * 2 * per):
                    scatter_row(i - 1, r, oprev).start(priority=r % 2)
            gate = _dot(h, wg_ref[k].astype(BF16))
            up = _dot(h, wu_ref[k].astype(BF16))
            ck = jnp.sum(jnp.where(lane == first + k, cw, 0.0), axis=1, keepdims=True)
            abuf[:, k * D_EXPERT:(k + 1) * D_EXPERT] = (gate * _sigmoid(gate) * up * ck).astype(BF16)
        obuf[oslot] = x2 + _dot(abuf[...], wd_ref[...].astype(BF16))

    @pl.when(prev_full & (nv > 0))
    def _():
        experts(True)

    @pl.when(jnp.logical_not(prev_full) & (nv > 0))
    def _():
        experts(False)

    @pl.when(prev_full & (nv == 0))
    def _():
        def body(r, c):
            scatter_row(i - 1, r, oprev).start()
            return c

        lax.fori_loop(0, tmx, body, 0)

    @pl.when(nv < tmx)
    def _():
        def body(r, c):
            scatter_row(i, r, oslot).start()
            return c

        lax.fori_loop(0, nv, body, 0)

    @pl.when(i == n_tiles - 1)
    def _():
        @pl.when(nv == tmx)
        def _():
            def body(r, c):
                scatter_row(i, r, oslot).start()
                return c

            lax.fori_loop(0, tmx, body, 0)

        @pl.when(nv > 0)
        def _():
            gather_wait(1 - slot)

        for back in range(OUT_BUFS - 1, -1, -1):
            scatter_wait(nval_ref[jnp.maximum(i - back, 0)], (i + OUT_BUFS - back) % OUT_BUFS)


def _moe(x2t, grp, nval, pos, g2, wr, br, wg, wu, wd):
    n_tok = x2t.shape[0] // TT_ROWS
    tmx = MOE_TM
    n_tiles = grp.shape[0]
    kern = functools.partial(_moe_kernel, tmx=tmx, n_tiles=n_tiles, n_tok=n_tok)
    gk = EXPERTS_PER_GROUP * D_EXPERT
    w_bytes = 3 * EXPERTS_PER_GROUP * D_MODEL * D_EXPERT * 4
    io_bytes = (2 * TT_ROWS * LANES + OUT_BUFS * D_MODEL) * tmx * 4 + tmx * gk * 2
    vmem_limit = w_bytes + io_bytes + 16 * 1024 * 1024
    assert vmem_limit < V7X_VMEM_BYTES
    grid_spec = pltpu.PrefetchScalarGridSpec(
        num_scalar_prefetch=3,
        grid=(n_tiles,),
        in_specs=[
            pl.BlockSpec(memory_space=pl.ANY),
            pl.BlockSpec((1, D_MODEL), lambda i, g, n, t: (0, 0)),
            pl.BlockSpec((D_MODEL, LANES), lambda i, g, n, t: (0, 0)),
            pl.BlockSpec((1, LANES), lambda i, g, n, t: (0, 0)),
            pl.BlockSpec((None, EXPERTS_PER_GROUP, D_MODEL, D_EXPERT), lambda i, g, n, t: (g[i], 0, 0, 0),
                         pipeline_mode=pl.Buffered(1)),
            pl.BlockSpec((None, EXPERTS_PER_GROUP, D_MODEL, D_EXPERT), lambda i, g, n, t: (g[i], 0, 0, 0),
                         pipeline_mode=pl.Buffered(1)),
            pl.BlockSpec((None, gk, D_MODEL), lambda i, g, n, t: (g[i], 0, 0), pipeline_mode=pl.Buffered(1)),
        ],
        out_specs=pl.BlockSpec(memory_space=pl.ANY),
        scratch_shapes=[
            pltpu.VMEM((2, tmx * TT_ROWS, LANES), F32),
            pltpu.VMEM((OUT_BUFS, tmx, D_MODEL), F32),
            pltpu.VMEM((tmx, gk), BF16),
            pltpu.SMEM((n_tiles * tmx,), jnp.int32),
            pltpu.SemaphoreType.DMA((2,)),
            pltpu.SemaphoreType.DMA((OUT_BUFS,)),
        ],
    )
    return pl.pallas_call(
        kern,
        grid_spec=grid_spec,
        out_shape=jax.ShapeDtypeStruct((n_tok, D_MODEL), F32),
        compiler_params=_params(("arbitrary",), vmem_limit),
        name="moe",
    )(grp, nval, pos, x2t, g2, wr, br, wg, wu, wd)


def _block_diag_ones(n, blk):
    i = jnp.arange(n) // blk
    return (i[:, None] == i[None, :]).astype(BF16)


def _layer(x, norm1_g, w_in, lam_re, lam_im, log_step, b_re, b_im, c_re, c_im, d_skip,
           w_glu, b_glu, g_q, g_kc, g_ks, g_kw, pos_k, pos_v, w_ck1, w_ck2, w_cv1, w_cv2,
           out_g_ssm, out_g_att, w_out, norm2_g, w_grp, b_grp, w_exp, b_exp, w_gate, w_up, w_down):
    bsz, seq, _ = x.shape
    assert seq % PROJ_TM == 0 and seq // SEL_BLOCK <= MAX_SEL_BLOCKS
    n_tok = bsz * seq
    x2d = x.reshape(n_tok, D_MODEL)
    q8 = SSM_Q
    n_sub = seq // q8
    nch = seq // CMP_STRIDE

    o_q = D_SSM
    o_kv = D_SSM + D_ATT
    o_gt = o_kv + 6 * D_KV
    kv = lambda i: w_in[:, o_kv + i * D_KV:o_kv + (i + 1) * D_KV]
    wrow = jnp.concatenate([w_in[:, :o_q], kv(0), kv(1), kv(2), kv(4)], axis=1).astype(BF16)
    per_g = HPG * N_BRANCH
    wgt = jnp.zeros((D_MODEL, N_KV * GATE_ROWS), F32)
    for g in range(N_KV):
        wgt = wgt.at[:, g * GATE_ROWS:g * GATE_ROWS + per_g].set(w_in[:, o_gt + g * per_g:o_gt + (g + 1) * per_g])
    wcol = jnp.concatenate([w_in[:, o_q:o_kv], kv(3), kv(5), wgt], axis=1).T.astype(BF16)
    qscale = (HEAD_DIM ** -0.5) * math.log2(math.e)
    gq = (jnp.tile(g_q.astype(F32), N_HEADS) * qscale).reshape(D_ATT, 1)
    gks = jnp.tile(g_ks.astype(F32), N_KV).reshape(1, D_KV)
    gkw = jnp.tile(g_kw.astype(F32), N_KV).reshape(1, D_KV)

    u, qt, kc, vc, ksa, kw, vst, vwt, gate = _in_proj(
        x2d, norm1_g.reshape(1, D_MODEL), wrow, wcol, gq, gks, gkw,
        _block_diag_ones(D_KV, HEAD_DIM), bsz=bsz, seq=seq)

    w_c, t_c, m_c, e_state_t, e_lane_t, pw_re, pw_im, lr_re, lr_im, dvec = _s5_weights(
        lam_re, lam_im, log_step, b_re, b_im, c_re, c_im, d_skip, n_sub)
    ys = _s5(u.reshape(bsz, seq, D_SSM), w_c, t_c, m_c, e_state_t, e_lane_t, pw_re, pw_im, lr_re, lr_im, dvec,
             bsz=bsz, n_sub=n_sub).reshape(n_tok, D_SSM)

    wide = CMP_STRIDE * HEAD_DIM
    pad8 = lambda p: jnp.zeros((SUBLANES, 2 * wide), F32).at[0].set(p.reshape(-1)).astype(BF16)
    kcmp, vcmpt = _compress(
        kc, vc,
        w_ck1.astype(BF16), w_ck2.astype(BF16), w_cv1.astype(BF16), w_cv2.T.astype(BF16),
        pad8(pos_k), pad8(pos_v), g_kc.astype(F32).reshape(1, HEAD_DIM), bsz=bsz, nch=nch)
    cstart = jnp.arange(nch) * CMP_STRIDE
    sstart = jnp.arange(MAX_SEL_BLOCKS) * SEL_BLOCK
    ovlt = ((cstart[None, :] < sstart[:, None] + SEL_BLOCK) & (cstart[None, :] + CMP_BLOCK > sstart[:, None])
            & (jnp.arange(MAX_SEL_BLOCKS)[:, None] < seq // SEL_BLOCK)
            & (jnp.arange(nch)[None, :] < nch - 1)).astype(BF16)
    yat = _nsa(qt, kcmp, vcmpt, ksa, kw, vst, vwt, gate, ovlt, bsz=bsz, seq=seq, nch=nch)

    wr = jnp.zeros((D_MODEL, LANES), F32)
    wr = wr.at[:, :N_EXP_GROUPS].set(w_grp).at[:, ROUTER_OFF:ROUTER_OFF + N_EXPERTS].set(w_exp).astype(BF16)
    br = jnp.zeros((1, LANES), F32)
    br = br.at[0, :N_EXP_GROUPS].set(b_grp).at[0, ROUTER_OFF:ROUTER_OFF + N_EXPERTS].set(b_exp)
    g2 = norm2_g.reshape(1, D_MODEL).astype(F32)
    x2t, gsel = _out_proj(
        ys, yat, x2d, w_glu.astype(BF16), b_glu.reshape(1, D_SSM).astype(F32),
        out_g_ssm.reshape(1, D_SSM).astype(F32), out_g_att.reshape(D_ATT, 1).astype(F32),
        w_out.astype(BF16), g2, wr, br, seq=seq)

    grp, nval, pos = _moe_plan(gsel[:, 0].astype(jnp.int32), n_tok)
    gshape = (N_EXP_GROUPS, EXPERTS_PER_GROUP, D_MODEL, D_EXPERT)
    out = _moe(x2t, grp, nval, pos, g2, wr, br, w_gate.reshape(gshape), w_up.reshape(gshape),
               w_down.reshape(N_EXP_GROUPS, EXPERTS_PER_GROUP * D_EXPERT, D_MODEL))
    return out.reshape(bsz, seq, D_MODEL)


def kernel(x, norm1_g, w_in, lam_re, lam_im, log_step, b_re, b_im, c_re, c_im, d_skip, w_glu, b_glu, g_q, g_kc, g_ks, g_kw, pos_k, pos_v, w_ck1, w_ck2, w_cv1, w_cv2, out_g_ssm, out_g_att, w_out, norm2_g, w_grp, b_grp, w_exp, b_exp, w_gate, w_up, w_down):
    depth = norm1_g.shape[0]
    for l in range(depth):
        x = _layer(x, norm1_g[l], w_in[l], lam_re[l], lam_im[l], log_step[l], b_re[l], b_im[l], c_re[l],
                   c_im[l], d_skip[l], w_glu[l], b_glu[l], g_q[l], g_kc[l], g_ks[l], g_kw[l], pos_k[l],
                   pos_v[l], w_ck1[l], w_ck2[l], w_cv1[l], w_cv2[l], out_g_ssm[l], out_g_att[l], w_out[l],
                   norm2_g[l], w_grp[l], b_grp[l], w_exp[l], b_exp[l], w_gate[l], w_up[l], w_down[l])
    return x
```
